```python
import jax, jax.numpy as jnp
from jax import lax
import numpy as np

D_MODEL = 1024
BATCH = 1
SEQ = 16384
DEPTH = 1
DEC_BATCH = 128
DEC_SEQ = 4
PAST_LEN = 16384
PAGE_SIZE = 128

HEAD_DIM = 64
N_HEADS = D_MODEL // HEAD_DIM
GQA_GROUP = 8
N_KV_HEADS = N_HEADS // GQA_GROUP
Q_WIDTH = N_HEADS * HEAD_DIM
KV_WIDTH = N_KV_HEADS * HEAD_DIM
WINDOW = 128
ROT_DIM = HEAD_DIM // 4
ROPE_THETA = 500000.0
CHUNK = 128
GMLP_WIDTH = D_MODEL
GMLP_GROUPS = 8
GMLP_GROUP_DIM = GMLP_WIDTH // GMLP_GROUPS
IN_WIDTH = Q_WIDTH + 2 * KV_WIDTH + 2 * GMLP_WIDTH + 2 * D_MODEL
N_EXPERTS = 32
TOP_K = 4
D_EXPERT = D_MODEL
SWIGLU_LIMIT = 7.0
SWIGLU_ALPHA = 1.702
MOE_BLOCK = 128
RMS_EPS = 1e-5
LN_EPS = 1e-5
NEG_INF = -1e30

kernel_name = "gated_parallel_gmlp_swa_moe_decoder_step"


def rms_norm(x, g):
    xf = x.astype(jnp.float32)
    y = xf * lax.rsqrt(jnp.mean(xf * xf, axis=-1, keepdims=True) + RMS_EPS)
    return (y * g.astype(jnp.float32)).astype(x.dtype)


def layer_norm(x, g, b):
    xf = x.astype(jnp.float32)
    xc = xf - jnp.mean(xf, axis=-1, keepdims=True)
    var = jnp.mean(xc * xc, axis=-1, keepdims=True)
    return (xc * lax.rsqrt(var + LN_EPS) * g.astype(jnp.float32) + b.astype(jnp.float32)).astype(x.dtype)


def rotary(x, pos):
    half = ROT_DIM // 2
    inv_freq = ROPE_THETA ** (-jnp.arange(half, dtype=jnp.float32) / half)
    ang = pos.astype(jnp.float32)[:, None] * inv_freq[None, :]
    shape = (pos.shape[0],) + (1,) * (x.ndim - 3) + (half,)
    cos = jnp.cos(ang).reshape(shape)
    sin = jnp.sin(ang).reshape(shape)
    xf = x.astype(jnp.float32)
    x1 = xf[..., :half]
    x2 = xf[..., half:ROT_DIM]
    out = jnp.concatenate([x1 * cos - x2 * sin, x2 * cos + x1 * sin, xf[..., ROT_DIM:]], axis=-1)
    return out.astype(x.dtype)


def project(xn, w_in):
    z = jnp.einsum("btd,de->bte", xn, w_in)
    cuts = np.cumsum([Q_WIDTH, KV_WIDTH, KV_WIDTH, GMLP_WIDTH, GMLP_WIDTH, D_MODEL]).tolist()
    return jnp.split(z, cuts, axis=-1)


def gmlp_branch(u, vg, ln_g, ln_b, w_s, b_s):
    u = jax.nn.gelu(u)
    vn = layer_norm(jax.nn.gelu(vg), ln_g, ln_b)
    B, T, _ = vn.shape
    pad = (-T) % CHUNK
    vp = jnp.pad(vn, ((0, 0), (0, pad), (0, 0))).reshape(B, -1, CHUNK, GMLP_GROUPS, GMLP_GROUP_DIM)
    mask = jnp.tril(jnp.ones((CHUNK, CHUNK), dtype=bool))
    w = jnp.where(mask[None], w_s, jnp.zeros_like(w_s))
    s = jnp.einsum("gts,bcsgd->bctgd", w, vp) + b_s.T[None, None, :, :, None]
    s = s.reshape(B, -1, GMLP_WIDTH)[:, :T]
    return u * s, vn


def window_attend(q, k, v, q_pos, k_pos, sinks):
    logits = jnp.einsum("bnqkgd,bnskd->bnkgqs", q, k, preferred_element_type=jnp.float32) * (HEAD_DIM ** -0.5)
    rel = q_pos[:, :, None] - k_pos[:, None, :]
    allowed = (rel >= 0) & (rel < WINDOW) & (k_pos[:, None, :] >= 0)
    logits = jnp.where(allowed[None, :, None, None], logits, NEG_INF)
    B, n, kvh, g, tq, tk = logits.shape
    sink = jnp.broadcast_to(sinks.astype(jnp.float32).reshape(1, 1, kvh, g, 1, 1), (B, n, kvh, g, tq, 1))
    probs = jax.nn.softmax(jnp.concatenate([logits, sink], axis=-1), axis=-1)[..., :tk]
    return jnp.einsum("bnkgqs,bnskd->bnqkgd", probs.astype(v.dtype), v)


def banded_attention(q, k, v, sinks):
    B, T = q.shape[:2]
    nb = T // WINDOW
    qb = q.reshape(B, nb, WINDOW, N_KV_HEADS, GQA_GROUP, HEAD_DIM)

    def with_prev(xb):
        xb = xb.reshape(B, nb, WINDOW, N_KV_HEADS, HEAD_DIM)
        prev = jnp.pad(xb[:, :-1], ((0, 0), (1, 0), (0, 0), (0, 0), (0, 0)))
        return jnp.concatenate([prev, xb], axis=2)

    q_pos = jnp.arange(T, dtype=jnp.int32).reshape(nb, WINDOW)
    k_pos = (jnp.arange(nb, dtype=jnp.int32)[:, None] - 1) * WINDOW + jnp.arange(2 * WINDOW, dtype=jnp.int32)[None, :]
    o = window_attend(qb, with_prev(k), with_prev(v), q_pos, k_pos, sinks)
    return o.reshape(B, T, Q_WIDTH)


def moe_ffn(xn, w_router, b_router, w_up, b_up, w_down, b_down):
    B, T, D = xn.shape
    n_tok = B * T
    xt = xn.reshape(n_tok, D)
    logits = xt.astype(jnp.float32) @ w_router.astype(jnp.float32) + b_router.astype(jnp.float32)
    top_val, top_idx = lax.top_k(logits, TOP_K)
    gate = jax.nn.softmax(top_val, axis=-1)
    n_assign = n_tok * TOP_K
    e_flat = top_idx.reshape(-1)
    t_flat = jnp.repeat(jnp.arange(n_tok, dtype=jnp.int32), TOP_K)
    w_flat = gate.reshape(-1)
    order = jnp.argsort(e_flat)
    e_sorted = e_flat[order]
    counts = jnp.bincount(e_flat, length=N_EXPERTS)
    padded = (counts + MOE_BLOCK - 1) // MOE_BLOCK * MOE_BLOCK
    start = jnp.cumsum(counts) - counts
    pend = jnp.cumsum(padded)
    pstart = pend - padded
    dest = pstart[e_sorted] + jnp.arange(n_assign, dtype=jnp.int32) - start[e_sorted]
    n_blocks = -(-n_assign // MOE_BLOCK) + N_EXPERTS
    n_slots = n_blocks * MOE_BLOCK
    slot_tok = jnp.zeros((n_slots,), jnp.int32).at[dest].set(t_flat[order])
    slot_w = jnp.zeros((n_slots,), jnp.float32).at[dest].set(w_flat[order])
    block_exp = jnp.searchsorted(pend, jnp.arange(n_blocks, dtype=jnp.int32) * MOE_BLOCK, side="right")
    block_exp = jnp.minimum(block_exp, N_EXPERTS - 1).astype(jnp.int32)
    xb = xt[slot_tok].reshape(n_blocks, MOE_BLOCK, D)

    def expert_block(args):
        xblk, e = args
        h = jnp.einsum("sd,df->sf", xblk, w_up[e]) + b_up[e]
        x_glu = jnp.minimum(h[:, ::2], SWIGLU_LIMIT)
        x_lin = jnp.clip(h[:, 1::2], -SWIGLU_LIMIT, SWIGLU_LIMIT)
        act = x_glu * jax.nn.sigmoid(SWIGLU_ALPHA * x_glu) * (x_lin + 1)
        return jnp.einsum("sf,fd->sd", act, w_down[e]) + b_down[e]

    yb = lax.map(expert_block, (xb, block_exp))
    y = jnp.zeros((n_tok, D), jnp.float32).at[slot_tok].add(yb.reshape(n_slots, D).astype(jnp.float32) * slot_w[:, None])
    return y.astype(xn.dtype).reshape(B, T, D)


def mixer_inputs(x, pos, norm_g, w_in, ln_v_g, ln_v_b, w_s, b_s):
    B, T, _ = x.shape
    q, k, v, u, vg, gate_a, gate_b = project(rms_norm(x, norm_g), w_in)
    q = rotary(q.reshape(B, T, N_KV_HEADS, GQA_GROUP, HEAD_DIM), pos)
    k = rotary(k.reshape(B, T, N_KV_HEADS, HEAD_DIM), pos)
    v = v.reshape(B, T, N_KV_HEADS, HEAD_DIM)
    o_gmlp, v_rows = gmlp_branch(u, vg, ln_v_g, ln_v_b, w_s, b_s)
    return q, k, v, o_gmlp, v_rows, gate_a, gate_b


def finish_layer(x, o_gmlp, o_attn, gate_a, gate_b, w_out, norm_ffn_g, w_router, b_router, w_up, b_up, w_down, b_down):
    merged = jax.nn.sigmoid(gate_a) * o_gmlp + jax.nn.sigmoid(gate_b) * o_attn
    x = x + jnp.einsum("bte,ed->btd", merged, w_out)
    return x + moe_ffn(rms_norm(x, norm_ffn_g), w_router, b_router, w_up, b_up, w_down, b_down)


def setup_inputs(seed: int = 0) -> dict:
    key = jax.random.key(seed)
    ks = jax.random.split(key, 20)
    w_buf = min(WINDOW, PAST_LEN)

    def nrm(k, shape, scale):
        return scale * jax.random.normal(k, shape, jnp.float32)

    return {
        "x_prompt": nrm(ks[0], (BATCH, SEQ, D_MODEL), 1.0),
        "x_sample": nrm(ks[1], (DEC_BATCH, DEC_SEQ, D_MODEL), 1.0),
        "cache_k_win": nrm(ks[2], (DEPTH, DEC_BATCH, w_buf, N_KV_HEADS, HEAD_DIM), 1.0),
        "cache_v_win": nrm(ks[3], (DEPTH, DEC_BATCH, w_buf, N_KV_HEADS, HEAD_DIM), 1.0),
        "norm_attn_g": 1.0 + nrm(ks[4], (DEPTH, D_MODEL), 0.02),
        "w_in": nrm(ks[5], (DEPTH, D_MODEL, IN_WIDTH), D_MODEL ** -0.5),
        "ln_v_g": 1.0 + nrm(ks[6], (DEPTH, GMLP_WIDTH), 0.02),
        "ln_v_b": nrm(ks[7], (DEPTH, GMLP_WIDTH), 0.02),
        "w_spatial": nrm(ks[8], (DEPTH, GMLP_GROUPS, CHUNK, CHUNK), CHUNK ** -0.5),
        "b_spatial": 1.0 + nrm(ks[9], (DEPTH, GMLP_GROUPS, CHUNK), 0.1),
        "attn_sinks": nrm(ks[10], (DEPTH, N_HEADS), 0.5),
        "w_out": nrm(ks[11], (DEPTH, D_MODEL, D_MODEL), D_MODEL ** -0.5),
        "norm_ffn_g": 1.0 + nrm(ks[12], (DEPTH, D_MODEL), 0.02),
        "w_router": nrm(ks[13], (DEPTH, D_MODEL, N_EXPERTS), D_MODEL ** -0.5),
        "b_router": nrm(ks[14], (DEPTH, N_EXPERTS), 0.01),
        "w_up": nrm(ks[15], (DEPTH, N_EXPERTS, D_MODEL, 2 * D_EXPERT), D_MODEL ** -0.5),
        "b_up": nrm(ks[16], (DEPTH, N_EXPERTS, 2 * D_EXPERT), 0.01),
        "w_down": nrm(ks[17], (DEPTH, N_EXPERTS, D_EXPERT, D_MODEL), D_EXPERT ** -0.5),
        "b_down": nrm(ks[18], (DEPTH, N_EXPERTS, D_MODEL), 0.01),
        "norm_final_g": 1.0 + nrm(ks[19], (D_MODEL,), 0.02),
    }


def reference(x_prompt, x_sample, cache_k_win, cache_v_win, norm_attn_g, w_in, ln_v_g, ln_v_b, w_spatial, b_spatial, attn_sinks, w_out, norm_ffn_g, w_router, b_router, w_up, b_up, w_down, b_down, norm_final_g):
    bp, tp, _ = x_prompt.shape
    bs, ts, _ = x_sample.shape
    w_buf = cache_k_win.shape[2]
    pos_p = jnp.arange(tp, dtype=jnp.int32)
    pos_s = PAST_LEN + jnp.arange(ts, dtype=jnp.int32)
    k_pos_s = PAST_LEN - w_buf + jnp.arange(w_buf + ts, dtype=jnp.int32)
    keep_p = min(WINDOW, tp)
    open_p = tp - ((tp - 1) // CHUNK) * CHUNK
    xp, xs = x_prompt, x_sample
    kp_l, vp_l, gp_l, ks_l, vs_l, gs_l = [], [], [], [], [], []
    for l in range(DEPTH):
        mix_w = (norm_attn_g[l], w_in[l], ln_v_g[l], ln_v_b[l], w_spatial[l], b_spatial[l])
        ffn_w = (w_out[l], norm_ffn_g[l], w_router[l], b_router[l], w_up[l], b_up[l], w_down[l], b_down[l])
        q, k, v, o_gmlp, v_rows, g_a, g_b = mixer_inputs(xp, pos_p, *mix_w)
        o_attn = banded_attention(q, k, v, attn_sinks[l])
        xp = finish_layer(xp, o_gmlp, o_attn, g_a, g_b, *ffn_w)
        kp_l.append(k[:, tp - keep_p:])
        vp_l.append(v[:, tp - keep_p:])
        gp_l.append(v_rows[:, tp - open_p:])
        q, k, v, o_gmlp, v_rows, g_a, g_b = mixer_inputs(xs, pos_s, *mix_w)
        k_all = jnp.concatenate([cache_k_win[l], k], axis=1)
        v_all = jnp.concatenate([cache_v_win[l], v], axis=1)
        o_attn = window_attend(q[:, None], k_all[:, None], v_all[:, None], pos_s[None], k_pos_s[None], attn_sinks[l])
        o_attn = o_attn.reshape(bs, ts, Q_WIDTH)
        xs = finish_layer(xs, o_gmlp, o_attn, g_a, g_b, *ffn_w)
        ks_l.append(k_all[:, ts:])
        vs_l.append(v_all[:, ts:])
        gs_l.append(v_rows)
    y_prompt = rms_norm(xp, norm_final_g)
    y_sample = rms_norm(xs, norm_final_g)
    k_win_prompt = jnp.stack(kp_l, axis=0)
    v_win_prompt = jnp.stack(vp_l, axis=0)
    gmlp_v_prompt = jnp.stack(gp_l, axis=0)
    k_win_sample = jnp.stack(ks_l, axis=0)
    v_win_sample = jnp.stack(vs_l, axis=0)
    gmlp_v_sample = jnp.stack(gs_l, axis=0)
    return (y_prompt, y_sample, k_win_prompt, v_win_prompt, gmlp_v_prompt, k_win_sample, v_win_sample, gmlp_v_sample)
```

```python
import functools

import numpy as np
import jax
import jax.numpy as jnp
from jax import lax
from jax.experimental import pallas as pl
from jax.experimental.pallas import tpu as pltpu

F32 = jnp.float32
BF16 = jnp.bfloat16

D_MODEL = 1024
HEAD_DIM = 64
N_HEADS = 16
GQA_GROUP = 8
N_KV_HEADS = 2
Q_WIDTH = 1024
KV_WIDTH = 128
WINDOW = 128
ROT_DIM = 16
ROPE_THETA = 500000.0
CHUNK = 128
GMLP_WIDTH = 1024
GMLP_GROUPS = 8
N_EXPERTS = 32
TOP_K = 4
SWIGLU_LIMIT = 7.0
SWIGLU_ALPHA = 1.702
RMS_EPS = 1e-5
LN_EPS = 1e-5
NEG_INF = -1e30
PAST_LEN = 16384

LANES = 128
VMEM_LIMIT = 56 * 1024 * 1024

PROJ_ROWS = 256
MIX_ROWS = 512
ROUTE_ROWS = 512
EXPERT_ROWS = 256
COMBINE_ROWS = 256

_C_Q, _C_KV, _C_U, _C_VG, _C_GA, _C_GB, _C_END = 0, 1024, 1280, 2304, 3328, 4352, 5376


def _params(sem):
    return pltpu.CompilerParams(dimension_semantics=sem, vmem_limit_bytes=VMEM_LIMIT)


def _rms(x, g):
    return x * lax.rsqrt(jnp.mean(x * x, axis=-1, keepdims=True) + RMS_EPS) * g


def _proj_body(x_ref, g_ref, w_ref, rc_ref, rs1_ref, rs2_ref, lng_ref, lnb_ref,
               q_ref, k_ref, v_ref, a_ref, vn_ref, sgb_ref):
    h = _rms(x_ref[...], g_ref[...]).astype(BF16)
    rc, rs1, rs2 = rc_ref[...], rs1_ref[...], rs2_ref[...]

    def rot(z):
        return z * rc + pltpu.roll(z, LANES - ROT_DIM // 2, 1) * rs1 + pltpu.roll(z, ROT_DIM // 2, 1) * rs2

    def mm(lo, hi):
        return jnp.dot(h, w_ref[:, lo:hi], preferred_element_type=F32)

    zq = mm(_C_Q, _C_KV)
    for c in range(Q_WIDTH // LANES):
        sl = slice(c * LANES, (c + 1) * LANES)
        q_ref[:, sl] = (rot(zq[:, sl]) * (HEAD_DIM ** -0.5)).astype(BF16)
    zkv = mm(_C_KV, _C_U)
    k_ref[...] = rot(zkv[:, :KV_WIDTH])
    v_ref[...] = zkv[:, KV_WIDTH:]
    a_ref[...] = jax.nn.sigmoid(mm(_C_GA, _C_GB)) * jax.nn.gelu(mm(_C_U, _C_VG))
    zv = jax.nn.gelu(mm(_C_VG, _C_GA))
    zc = zv - jnp.mean(zv, axis=-1, keepdims=True)
    var = jnp.mean(zc * zc, axis=-1, keepdims=True)
    vn_ref[...] = zc * lax.rsqrt(var + LN_EPS) * lng_ref[...] + lnb_ref[...]
    sgb_ref[...] = jax.nn.sigmoid(mm(_C_GB, _C_END))


def _proj(x, norm_g, w_in_bf, rc, rs1, rs2, ln_g, ln_b):
    n = x.shape[0]
    tm = PROJ_ROWS
    row = lambda w: pl.BlockSpec((tm, w), lambda i: (i, 0))
    full = lambda a: pl.BlockSpec(a.shape, lambda i: (0,) * a.ndim)
    return pl.pallas_call(
        _proj_body,
        grid=(n // tm,),
        in_specs=[row(D_MODEL), full(norm_g), full(w_in_bf), row(LANES), row(LANES), row(LANES),
                  full(ln_g), full(ln_b)],
        out_specs=[row(Q_WIDTH), row(KV_WIDTH), row(KV_WIDTH), row(GMLP_WIDTH), row(GMLP_WIDTH), row(D_MODEL)],
        out_shape=[jax.ShapeDtypeStruct((n, Q_WIDTH), BF16),
                   jax.ShapeDtypeStruct((n, KV_WIDTH), F32),
                   jax.ShapeDtypeStruct((n, KV_WIDTH), F32),
                   jax.ShapeDtypeStruct((n, GMLP_WIDTH), F32),
                   jax.ShapeDtypeStruct((n, GMLP_WIDTH), F32),
                   jax.ShapeDtypeStruct((n, D_MODEL), F32)],
        compiler_params=_params(("arbitrary",)),
        name="proj",
    )(x, norm_g, w_in_bf, rc, rs1, rs2, ln_g, ln_b)


def _rotary_tables(pos):
    half = ROT_DIM // 2
    inv_freq = ROPE_THETA ** (-jnp.arange(half, dtype=F32) / half)
    ang = pos.astype(F32)[:, None] * inv_freq[None, :]
    cos, sin = jnp.cos(ang), jnp.sin(ang)
    n = pos.shape[0]
    pad = jnp.zeros((n, HEAD_DIM - ROT_DIM), F32)
    zero = jnp.zeros((n, half), F32)
    rc = jnp.concatenate([cos, cos, pad + 1.0], axis=1)
    rs1 = jnp.concatenate([-sin, zero, pad], axis=1)
    rs2 = jnp.concatenate([zero, sin, pad], axis=1)
    tile = lambda t: jnp.tile(t, (1, LANES // HEAD_DIM))
    return tile(rc), tile(rs1), tile(rs2)


def _finish_rows(merged_bf, x, wout_ref, g2_ref, wr_ref, br_ref, xmid_ref, xn2_ref, lg_ref):
    xm = x + jnp.dot(merged_bf, wout_ref[...], preferred_element_type=F32)
    xmid_ref[...] = xm
    xn = _rms(xm, g2_ref[...])
    xn2_ref[...] = xn
    lg_ref[...] = jnp.dot(xn, wr_ref[...], preferred_element_type=F32,
                          precision=lax.Precision.HIGHEST) + br_ref[...]


def _mix_body(sinks_ref, q_ref, k_ref, kp_ref, v_ref, vp_ref, a_ref, vn_ref, sgb_ref, x_ref,
              wsp_ref, bsp_ref, wout_ref, g2_ref, wr_ref, br_ref,
              xmid_ref, xn2_ref, lg_ref, kcat, vcat, mrg):
    i = pl.program_id(0)
    nsub = MIX_ROWS // WINDOW
    kcat[0:WINDOW] = kp_ref[...]
    kcat[WINDOW:] = k_ref[...]
    vcat[0:WINDOW] = vp_ref[...]
    vcat[WINDOW:] = v_ref[...]

    pair_rows = (GQA_GROUP // 2) * WINDOW
    rq = lax.broadcasted_iota(jnp.int32, (pair_rows, 4 * WINDOW), 0) & (WINDOW - 1)
    ck = lax.broadcasted_iota(jnp.int32, (pair_rows, 4 * WINDOW), 1) & (2 * WINDOW - 1)
    band = (ck > rq) & (ck <= rq + WINDOW)
    lane_kv = lax.broadcasted_iota(jnp.int32, (2 * WINDOW, LANES), 1)
    lane_o = lax.broadcasted_iota(jnp.int32, (pair_rows, LANES), 1)
    row_p = lax.broadcasted_iota(jnp.int32, (pair_rows, 1), 0) >> 7

    def sub(j, carry):
        off = pl.multiple_of(j * WINDOW, WINDOW)
        rows = pl.ds(off, WINDOW)
        for g in range(GMLP_GROUPS):
            cols = slice(g * LANES, (g + 1) * LANES)
            s = jnp.dot(wsp_ref[g], vn_ref[rows, cols].astype(BF16), preferred_element_type=F32) + bsp_ref[g]
            mrg[rows, cols] = a_ref[rows, cols] * s
        kblk = kcat[pl.ds(off, 2 * WINDOW), :]
        vblk = vcat[pl.ds(off, 2 * WINDOW), :]
        kswp = pltpu.roll(kblk, HEAD_DIM, 1)
        vswp = pltpu.roll(vblk, HEAD_DIM, 1)
        kmin = jnp.where(jnp.logical_and(i == 0, j == 0), WINDOW, 0)
        allowed = band & (ck >= kmin)
        for kk in range(N_KV_HEADS):
            lo_src, hi_src = (kblk, kswp) if kk == 0 else (kswp, kblk)
            kbd = jnp.concatenate([jnp.where(lane_kv < HEAD_DIM, lo_src, 0.0),
                                   jnp.where(lane_kv >= HEAD_DIM, hi_src, 0.0)], axis=0).astype(BF16)
            lo_src, hi_src = (vblk, vswp) if kk == 0 else (vswp, vblk)
            vbd = jnp.concatenate([jnp.where(lane_kv < HEAD_DIM, lo_src, 0.0),
                                   jnp.where(lane_kv >= HEAD_DIM, hi_src, 0.0)], axis=0).astype(BF16)
            pair0 = kk * (GQA_GROUP // 2)
            qs = jnp.concatenate([q_ref[rows, (pair0 + p) * LANES:(pair0 + p + 1) * LANES]
                                  for p in range(GQA_GROUP // 2)], axis=0)
            lg = lax.dot_general(qs, kbd, (((1,), (1,)), ((), ())), preferred_element_type=F32)
            lg = jnp.where(allowed, lg, NEG_INF)
            h0 = kk * GQA_GROUP
            se = jnp.full((pair_rows, 1), sinks_ref[h0], F32)
            so = jnp.full((pair_rows, 1), sinks_ref[h0 + 1], F32)
            for p in range(1, GQA_GROUP // 2):
                se = jnp.where(row_p == p, sinks_ref[h0 + 2 * p], se)
                so = jnp.where(row_p == p, sinks_ref[h0 + 2 * p + 1], so)
            le, lo = lg[:, :2 * WINDOW], lg[:, 2 * WINDOW:]
            me = jnp.maximum(jnp.max(le, axis=1, keepdims=True), se)
            mo = jnp.maximum(jnp.max(lo, axis=1, keepdims=True), so)
            pe = jnp.exp(le - me)
            po = jnp.exp(lo - mo)
            de = jnp.sum(pe, axis=1, keepdims=True) + jnp.exp(se - me)
            do = jnp.sum(po, axis=1, keepdims=True) + jnp.exp(so - mo)
            pr = jnp.concatenate([pe, po], axis=1).astype(BF16)
            o = jnp.dot(pr, vbd, preferred_element_type=F32)
            o = o / jnp.where(lane_o < HEAD_DIM, de, do)
            for p in range(GQA_GROUP // 2):
                cols = slice((pair0 + p) * LANES, (pair0 + p + 1) * LANES)
                mrg[rows, cols] += sgb_ref[rows, cols] * o[p * WINDOW:(p + 1) * WINDOW]
        return carry

    lax.fori_loop(0, nsub, sub, 0)
    _finish_rows(mrg[...].astype(BF16), x_ref[...], wout_ref, g2_ref, wr_ref, br_ref, xmid_ref, xn2_ref, lg_ref)


def _mix(sinks, q, k, v, a, vn, sgb, x, wsp, bsp, wout, g2, wr, br):
    n = x.shape[0]
    tm = MIX_ROWS
    nsub = tm // WINDOW
    row = lambda w: pl.BlockSpec((tm, w), lambda i: (i, 0))
    prev = pl.BlockSpec((WINDOW, KV_WIDTH), lambda i: (jnp.maximum(i * nsub - 1, 0), 0))
    full = lambda arr: pl.BlockSpec(arr.shape, lambda i: (0,) * arr.ndim)
    smem = pl.BlockSpec(memory_space=pltpu.SMEM)
    return pl.pallas_call(
        _mix_body,
        grid=(n // tm,),
        in_specs=[smem, row(Q_WIDTH), row(KV_WIDTH), prev, row(KV_WIDTH), prev,
                  row(GMLP_WIDTH), row(GMLP_WIDTH), row(D_MODEL), row(D_MODEL),
                  full(wsp), full(bsp), full(wout), full(g2), full(wr), full(br)],
        out_specs=[row(D_MODEL), row(D_MODEL), row(N_EXPERTS)],
        out_shape=[jax.ShapeDtypeStruct((n, D_MODEL), F32),
                   jax.ShapeDtypeStruct((n, D_MODEL), F32),
                   jax.ShapeDtypeStruct((n, N_EXPERTS), F32)],
        scratch_shapes=[pltpu.VMEM((tm + WINDOW, KV_WIDTH), F32),
                        pltpu.VMEM((tm + WINDOW, KV_WIDTH), F32),
                        pltpu.VMEM((tm, D_MODEL), F32)],
        compiler_params=_params(("arbitrary",)),
        name="mix_prompt",
    )(sinks, q, k, k, v, v, a, vn, sgb, x, wsp, bsp, wout, g2, wr, br)


def _sample_attn_body(sink_ref, q_ref, k_ref, v_ref, o_ref):
    q = q_ref[0]
    k = k_ref[0]
    v = v_ref[0]
    nq, nk = q.shape[1], k.shape[1]
    lg = jnp.einsum("bqd,bkd->bqk", q, k, preferred_element_type=F32)
    t = lax.broadcasted_iota(jnp.int32, (1, nq, nk), 1) // GQA_GROUP
    j = lax.broadcasted_iota(jnp.int32, (1, nq, nk), 2)
    lg = jnp.where((j > t) & (j <= t + WINDOW), lg, NEG_INF)
    sink = sink_ref[0][None]
    m = jnp.maximum(jnp.max(lg, axis=2, keepdims=True), sink)
    p = jnp.exp(lg - m)
    den = jnp.sum(p, axis=2, keepdims=True) + jnp.exp(sink - m)
    o = jnp.einsum("bqk,bkd->bqd", p.astype(BF16), v, preferred_element_type=F32)
    o_ref[0] = o / den


def _sample_attn(sink_rows, qh, kh, vh):
    nb = qh.shape[1]
    bb = 32
    blk = lambda a: pl.BlockSpec((1, bb) + a.shape[2:], lambda kk, b: (kk, b, 0, 0))
    return pl.pallas_call(
        _sample_attn_body,
        grid=(N_KV_HEADS, nb // bb),
        in_specs=[pl.BlockSpec((1,) + sink_rows.shape[1:], lambda kk, b: (kk, 0, 0)), blk(qh), blk(kh), blk(vh)],
        out_specs=pl.BlockSpec((1, bb) + qh.shape[2:], lambda kk, b: (kk, b, 0, 0)),
        out_shape=jax.ShapeDtypeStruct(qh.shape, F32),
        compiler_params=_params(("arbitrary", "arbitrary")),
        name="attn_sample",
    )(sink_rows, qh, kh, vh)


def _mix_sample_body(a_ref, vn_ref, sgb_ref, o_ref, x_ref, coef_ref, bias_ref,
                     wout_ref, g2_ref, wr_ref, br_ref, xmid_ref, xn2_ref, lg_ref):
    vn = vn_ref[...]
    s = bias_ref[...] + coef_ref[0] * vn
    for d in range(1, coef_ref.shape[0]):
        s = s + coef_ref[d] * pltpu.roll(vn, d, 0)
    merged = a_ref[...] * s + sgb_ref[...] * o_ref[...]
    _finish_rows(merged.astype(BF16), x_ref[...], wout_ref, g2_ref, wr_ref, br_ref, xmid_ref, xn2_ref, lg_ref)


def _mix_sample(a, vn, sgb, o, x, coef, bias, wout, g2, wr, br):
    n = x.shape[0]
    args = (a, vn, sgb, o, x, coef, bias, wout, g2, wr, br)
    full = lambda arr: pl.BlockSpec(arr.shape, lambda i: (0,) * arr.ndim)
    return pl.pallas_call(
        _mix_sample_body,
        grid=(1,),
        in_specs=[full(arr) for arr in args],
        out_specs=[pl.BlockSpec((n, D_MODEL), lambda i: (0, 0)), pl.BlockSpec((n, D_MODEL), lambda i: (0, 0)),
                   pl.BlockSpec((n, N_EXPERTS), lambda i: (0, 0))],
        out_shape=[jax.ShapeDtypeStruct((n, D_MODEL), F32),
                   jax.ShapeDtypeStruct((n, D_MODEL), F32),
                   jax.ShapeDtypeStruct((n, N_EXPERTS), F32)],
        compiler_params=_params(("arbitrary",)),
        name="mix_sample",
    )(*args)


def _route_body(lg_ref, idx_ref, gate_ref, rank_ref, cnt_ref, base):
    i = pl.program_id(0)

    @pl.when(i == 0)
    def _():
        base[...] = jnp.zeros_like(base)

    l = lg_ref[...]
    tb = l.shape[0]
    lane = lax.broadcasted_iota(jnp.int32, l.shape, 1).astype(F32)
    vals, idxs, sels = [], [], []
    for _ in range(TOP_K):
        m = jnp.max(l, axis=1, keepdims=True)
        ik = jnp.min(jnp.where(l == m, lane, float(N_EXPERTS)), axis=1, keepdims=True)
        sel = lane == ik
        l = jnp.where(sel, -jnp.inf, l)
        vals.append(m)
        idxs.append(ik)
        sels.append(sel)
    es = [jnp.exp(vk - vals[0]) for vk in vals]
    den = es[0] + es[1] + es[2] + es[3]
    onehot = jnp.zeros(l.shape, F32)
    for sel in sels:
        onehot = onehot + sel.astype(F32)
    tri = (lax.broadcasted_iota(jnp.int32, (tb, tb), 0) > lax.broadcasted_iota(jnp.int32, (tb, tb), 1))
    before = jnp.dot(tri.astype(BF16), onehot.astype(BF16), preferred_element_type=F32) + base[...]
    ranks = [jnp.sum(jnp.where(sel, before, 0.0), axis=1, keepdims=True) for sel in sels]
    base[...] += jnp.sum(onehot, axis=0, keepdims=True)
    cnt_ref[...] = base[...].astype(jnp.int32)

    lane_o = lax.broadcasted_iota(jnp.int32, (tb, LANES), 1)

    def spread(cols, dtype):
        out = jnp.zeros((tb, LANES), dtype)
        for kx, col in enumerate(cols):
            out = jnp.where(lane_o == kx, col.astype(dtype), out)
        return out

    idx_ref[...] = spread(idxs, jnp.int32)
    gate_ref[...] = spread([e / den for e in es], F32)
    rank_ref[...] = spread(ranks, jnp.int32)


def _route(logits):
    n = logits.shape[0]
    tb = ROUTE_ROWS
    wide = pl.BlockSpec((tb, LANES), lambda i: (i, 0))
    return pl.pallas_call(
        _route_body,
        grid=(n // tb,),
        in_specs=[pl.BlockSpec((tb, N_EXPERTS), lambda i: (i, 0))],
        out_specs=[wide, wide, wide, pl.BlockSpec((1, N_EXPERTS), lambda i: (0, 0))],
        out_shape=[jax.ShapeDtypeStruct((n, LANES), jnp.int32),
                   jax.ShapeDtypeStruct((n, LANES), F32),
                   jax.ShapeDtypeStruct((n, LANES), jnp.int32),
                   jax.ShapeDtypeStruct((1, N_EXPERTS), jnp.int32)],
        scratch_shapes=[pltpu.VMEM((1, N_EXPERTS), F32)],
        compiler_params=_params(("arbitrary",)),
        name="route",
    )(logits)


def _start_row_gather(idx_ref, src_hbm, dst_buf, sem, slot, count):
    def body(r, carry):
        pltpu.make_async_copy(src_hbm.at[pl.ds(idx_ref[0, 0, r], 1)],
                              dst_buf.at[slot, pl.ds(r, 1)], sem.at[slot]).start()
        return carry

    lax.fori_loop(0, count, body, 0, unroll=8)


def _wait_row_gather(dst_buf, sem, slot):
    pltpu.make_async_copy(dst_buf.at[slot], dst_buf.at[slot], sem.at[slot]).wait()


def _expert_body(bexp_ref, nused_ref, tok0_ref, tokn_ref, x_hbm, wg_ref, wl_ref, wd_ref, bg_ref, bl_ref, bd_ref,
                 out_ref, xbuf, sem):
    b = pl.program_id(0)
    nused = nused_ref[0]
    slot = b % 2

    @pl.when(b == 0)
    def _():
        _start_row_gather(tok0_ref, x_hbm, xbuf, sem, 0, EXPERT_ROWS)

    @pl.when(b + 1 < nused)
    def _():
        _start_row_gather(tokn_ref, x_hbm, xbuf, sem, 1 - slot, EXPERT_ROWS)

    @pl.when(b < nused)
    def _():
        _wait_row_gather(xbuf, sem, slot)
        x = xbuf[slot].astype(BF16)
        hg = jnp.dot(x, wg_ref[0], preferred_element_type=F32) + bg_ref[0]
        hl = jnp.dot(x, wl_ref[0], preferred_element_type=F32) + bl_ref[0]
        glu = jnp.minimum(hg, SWIGLU_LIMIT)
        lin = jnp.clip(hl, -SWIGLU_LIMIT, SWIGLU_LIMIT)
        act = glu * jax.nn.sigmoid(SWIGLU_ALPHA * glu) * (lin + 1.0)
        out_ref[...] = jnp.dot(act.astype(BF16), wd_ref[0], preferred_element_type=F32) + bd_ref[0]

    @pl.when(b >= nused)
    def _():
        out_ref[...] = jnp.zeros_like(out_ref)


def _experts(block_exp, nused, slot_tok, xn, wg, wl, wd, bg, bl, bd):
    nblk = block_exp.shape[0]
    tm = EXPERT_ROWS
    wspec = lambda: pl.BlockSpec((1, D_MODEL, D_MODEL), lambda b, be, nu: (be[b], 0, 0))
    bspec = lambda: pl.BlockSpec((1, 1, D_MODEL), lambda b, be, nu: (be[b], 0, 0))
    tok_blocks = slot_tok.reshape(nblk, 1, tm)
    grid_spec = pltpu.PrefetchScalarGridSpec(
        num_scalar_prefetch=2,
        grid=(nblk,),
        in_specs=[pl.BlockSpec((1, 1, tm), lambda b, be, nu: (0, 0, 0), memory_space=pltpu.SMEM),
                  pl.BlockSpec((1, 1, tm), lambda b, be, nu: (jnp.minimum(b + 1, nblk - 1), 0, 0),
                               memory_space=pltpu.SMEM),
                  pl.BlockSpec(memory_space=pl.ANY),
                  wspec(), wspec(), wspec(), bspec(), bspec(), bspec()],
        out_specs=pl.BlockSpec((tm, D_MODEL), lambda b, be, nu: (b, 0)),
        scratch_shapes=[pltpu.VMEM((2, tm, D_MODEL), F32), pltpu.SemaphoreType.DMA((2,))],
    )
    return pl.pallas_call(
        _expert_body,
        grid_spec=grid_spec,
        out_shape=jax.ShapeDtypeStruct((nblk * tm, D_MODEL), F32),
        compiler_params=_params(("arbitrary",)),
        name="experts",
    )(block_exp, nused, tok_blocks, tok_blocks, xn, wg, wl, wd, bg, bl, bd)


def _combine_body(n, d0_ref, dn_ref, gate_ref, xmid_ref, gfin_ref, y_hbm, out_ref, ybuf, sem):
    i = pl.program_id(0)
    slot = i % 2
    tt = COMBINE_ROWS

    @pl.when(i == 0)
    def _():
        _start_row_gather(d0_ref, y_hbm, ybuf, sem, 0, TOP_K * tt)

    @pl.when(i + 1 < n)
    def _():
        _start_row_gather(dn_ref, y_hbm, ybuf, sem, 1 - slot, TOP_K * tt)

    _wait_row_gather(ybuf, sem, slot)
    gate = gate_ref[...]
    acc = xmid_ref[...]
    moe = ybuf[slot, 0:tt] * gate[:, 0:1]
    for kx in range(1, TOP_K):
        moe = moe + ybuf[slot, kx * tt:(kx + 1) * tt] * gate[:, kx:kx + 1]
    out_ref[...] = _rms(acc + moe, gfin_ref[...])


def _combine(dest, gates, xmid, gfin, y_sorted):
    n = xmid.shape[0]
    tt = COMBINE_ROWS
    nblk = n // tt
    dblocks = dest.reshape(nblk, tt, TOP_K).transpose(0, 2, 1).reshape(nblk, 1, TOP_K * tt)
    return pl.pallas_call(
        functools.partial(_combine_body, nblk),
        grid=(nblk,),
        in_specs=[pl.BlockSpec((1, 1, TOP_K * tt), lambda i: (0, 0, 0), memory_space=pltpu.SMEM),
                  pl.BlockSpec((1, 1, TOP_K * tt), lambda i: (jnp.minimum(i + 1, nblk - 1), 0, 0),
                               memory_space=pltpu.SMEM),
                  pl.BlockSpec((tt, LANES), lambda i: (i, 0)),
                  pl.BlockSpec((tt, D_MODEL), lambda i: (i, 0)),
                  pl.BlockSpec((1, D_MODEL), lambda i: (0, 0)),
                  pl.BlockSpec(memory_space=pl.ANY)],
        out_specs=pl.BlockSpec((tt, D_MODEL), lambda i: (i, 0)),
        out_shape=jax.ShapeDtypeStruct((n, D_MODEL), F32),
        scratch_shapes=[pltpu.VMEM((2, TOP_K * tt, D_MODEL), F32), pltpu.SemaphoreType.DMA((2,))],
        compiler_params=_params(("arbitrary",)),
        name="combine",
    )(dblocks, dblocks, gates, xmid, gfin, y_sorted)


def kernel(x_prompt, x_sample, cache_k_win, cache_v_win, norm_attn_g, w_in, ln_v_g, ln_v_b, w_spatial, b_spatial,
           attn_sinks, w_out, norm_ffn_g, w_router, b_router, w_up, b_up, w_down, b_down, norm_final_g):
    bp, tp, _ = x_prompt.shape
    bs, ts, _ = x_sample.shape
    w_buf = cache_k_win.shape[2]
    assert bp == 1 and tp % MIX_ROWS == 0 and w_buf == WINDOW and (bs * ts) % PROJ_ROWS == 0
    n_p, n_s = bp * tp, bs * ts
    row2 = lambda a: a.reshape(1, -1)

    w_in_bf = w_in[0].astype(BF16)
    w_out_bf = w_out[0].astype(BF16)
    tril = jnp.tril(jnp.ones((CHUNK, CHUNK), dtype=bool))
    wsp = jnp.where(tril[None], w_spatial[0], 0.0)
    wsp_bf = wsp.astype(BF16)
    bsp = jnp.broadcast_to(b_spatial[0][:, :, None], (GMLP_GROUPS, CHUNK, LANES))
    wg = w_up[0][:, :, 0::2].astype(BF16)
    wl = w_up[0][:, :, 1::2].astype(BF16)
    wd = w_down[0].astype(BF16)
    bg = b_up[0][:, None, 0::2]
    bl = b_up[0][:, None, 1::2]
    bd = b_down[0][:, None, :]
    g1, g2, gfin = row2(norm_attn_g[0]), row2(norm_ffn_g[0]), row2(norm_final_g)
    lng, lnb = row2(ln_v_g[0]), row2(ln_v_b[0])
    wr, br = w_router[0], row2(b_router[0])
    sinks = attn_sinks[0]

    xp = x_prompt.reshape(n_p, D_MODEL)
    rc, rs1, rs2 = _rotary_tables(jnp.arange(tp, dtype=jnp.int32))
    q_p, k_p, v_p, a_p, vn_p, sgb_p = _proj(xp, g1, w_in_bf, rc, rs1, rs2, lng, lnb)
    xmid_p, xn2_p, lg_p = _mix(sinks, q_p, k_p, v_p, a_p, vn_p, sgb_p, xp, wsp_bf, bsp, w_out_bf, g2, wr, br)

    xs = x_sample.reshape(n_s, D_MODEL)
    pos_s = PAST_LEN + jnp.arange(ts, dtype=jnp.int32)
    rc, rs1, rs2 = _rotary_tables(jnp.tile(pos_s, bs))
    q_s, k_s, v_s, a_s, vn_s, sgb_s = _proj(xs, g1, w_in_bf, rc, rs1, rs2, lng, lnb)
    k_all = jnp.concatenate([cache_k_win[0], k_s.reshape(bs, ts, N_KV_HEADS, HEAD_DIM)], axis=1)
    v_all = jnp.concatenate([cache_v_win[0], v_s.reshape(bs, ts, N_KV_HEADS, HEAD_DIM)], axis=1)
    n_keys = w_buf + ts
    key_pad = (-n_keys) % 8
    to_heads = lambda t: jnp.pad(t, ((0, 0), (0, key_pad), (0, 0), (0, 0))).transpose(2, 0, 1, 3).astype(BF16)
    qh = q_s.reshape(bs, ts, N_KV_HEADS, GQA_GROUP, HEAD_DIM).transpose(2, 0, 1, 3, 4)
    qh = qh.reshape(N_KV_HEADS, bs, ts * GQA_GROUP, HEAD_DIM)
    sink_rows = jnp.tile(sinks.reshape(N_KV_HEADS, 1, GQA_GROUP), (1, ts, 1)).reshape(N_KV_HEADS, ts * GQA_GROUP, 1)
    oh = _sample_attn(sink_rows, qh, to_heads(k_all), to_heads(v_all))
    o_s = oh.reshape(N_KV_HEADS, bs, ts, GQA_GROUP, HEAD_DIM).transpose(1, 2, 0, 3, 4).reshape(n_s, Q_WIDTH)
    t_idx = jnp.arange(ts)
    coef = jnp.stack([jnp.where((t_idx >= d)[None, :], wsp[:, t_idx, jnp.maximum(t_idx - d, 0)], 0.0)
                      for d in range(ts)])
    coef = jnp.repeat(coef.transpose(0, 2, 1), GMLP_WIDTH // GMLP_GROUPS, axis=2)
    coef = jnp.tile(coef, (1, bs, 1))
    bias = jnp.tile(jnp.repeat(b_spatial[0][:, :ts].T, GMLP_WIDTH // GMLP_GROUPS, axis=1), (bs, 1))
    xmid_s, xn2_s, lg_s = _mix_sample(a_s, vn_s, sgb_s, o_s, xs, coef, bias, w_out_bf, g2, wr, br)

    n_tok = n_p + n_s
    logits = jnp.concatenate([lg_p, lg_s], axis=0)
    xn2 = jnp.concatenate([xn2_p, xn2_s], axis=0)
    idx_w, gate_w, rank_w, counts = _route(logits)
    idx, rank = idx_w[:, :TOP_K], rank_w[:, :TOP_K]
    counts = counts[0]
    tm = EXPERT_ROWS
    padded = (counts + tm - 1) // tm * tm
    pend = jnp.cumsum(padded)
    pstart = pend - padded
    dest = pstart[idx] + rank
    n_blocks = (n_tok * TOP_K) // tm + N_EXPERTS
    slot_tok = jnp.zeros((n_blocks * tm,), jnp.int32).at[dest.reshape(-1)].set(
        jnp.repeat(jnp.arange(n_tok, dtype=jnp.int32), TOP_K))
    nused = (pend[-1] // tm).astype(jnp.int32)
    blk_start = jnp.minimum(jnp.arange(n_blocks, dtype=jnp.int32), nused - 1) * tm
    block_exp = jnp.minimum(jnp.searchsorted(pend, blk_start, side="right"), N_EXPERTS - 1).astype(jnp.int32)

    y_sorted = _experts(block_exp, nused.reshape(1), slot_tok, xn2, wg, wl, wd, bg, bl, bd)

    y_p = _combine(dest[:n_p], gate_w[:n_p], xmid_p, gfin, y_sorted)
    y_s = _combine(dest[n_p:], gate_w[n_p:], xmid_s, gfin, y_sorted)

    k4 = lambda t: t.reshape(1, bp, -1, N_KV_HEADS, HEAD_DIM)
    return (y_p.reshape(bp, tp, D_MODEL),
            y_s.reshape(bs, ts, D_MODEL),
            k4(k_p[n_p - WINDOW:]),
            k4(v_p[n_p - WINDOW:]),
            vn_p[n_p - CHUNK:].reshape(1, bp, CHUNK, GMLP_WIDTH),
            k_all[None, :, ts:],
            v_all[None, :, ts:],
            vn_s.reshape(1, bs, ts, GMLP_WIDTH))
```

```python
import functools

import numpy as np
import jax
import jax.numpy as jnp
from jax import lax
from jax.experimental import pallas as pl
from jax.experimental.pallas import tpu as pltpu

F32 = jnp.float32
BF16 = jnp.bfloat16

D_MODEL = 1024
HEAD_DIM = 64
N_HEADS = 16
GQA_GROUP = 8
N_KV_HEADS = 2
Q_WIDTH = 1024
KV_WIDTH = 128
WINDOW = 128
ROT_DIM = 16
ROPE_THETA = 500000.0
CHUNK = 128
GMLP_WIDTH = 1024
GMLP_GROUPS = 8
N_EXPERTS = 32
TOP_K = 4
SWIGLU_LIMIT = 7.0
SWIGLU_ALPHA = 1.702
RMS_EPS = 1e-5
LN_EPS = 1e-5
NEG_INF = -1e30
PAST_LEN = 16384

LANES = 128
VMEM_LIMIT = 56 * 1024 * 1024

PROJ_ROWS = 256
MIX_ROWS = 512
ROUTE_ROWS = 512
EXPERT_ROWS = 256
COMBINE_ROWS = 256

_C_Q, _C_KV, _C_U, _C_VG, _C_GA, _C_GB, _C_END = 0, 1024, 1280, 2304, 3328, 4352, 5376


def _params(sem):
    return pltpu.CompilerParams(dimension_semantics=sem, vmem_limit_bytes=VMEM_LIMIT)


def _rms(x, g):
    return x * lax.rsqrt(jnp.mean(x * x, axis=-1, keepdims=True) + RMS_EPS) * g


def _proj_body(x_ref, g_ref, w_ref, rc_ref, rs1_ref, rs2_ref, lng_ref, lnb_ref,
               q_ref, k_ref, v_ref, a_ref, vn_ref, sgb_ref):
    h = _rms(x_ref[...], g_ref[...]).astype(BF16)
    rc, rs1, rs2 = rc_ref[...], rs1_ref[...], rs2_ref[...]

    def rot(z):
        return z * rc + pltpu.roll(z, LANES - ROT_DIM // 2, 1) * rs1 + pltpu.roll(z, ROT_DIM // 2, 1) * rs2

    def mm(lo, hi):
        return jnp.dot(h, w_ref[:, lo:hi], preferred_element_type=F32)

    zq = mm(_C_Q, _C_KV)
    for c in range(Q_WIDTH // LANES):
        sl = slice(c * LANES, (c + 1) * LANES)
        q_ref[:, sl] = (rot(zq[:, sl]) * (HEAD_DIM ** -0.5)).astype(BF16)
    zkv = mm(_C_KV, _C_U)
    k_ref[...] = rot(zkv[:, :KV_WIDTH])
    v_ref[...] = zkv[:, KV_WIDTH:]
    a_ref[...] = jax.nn.sigmoid(mm(_C_GA, _C_GB)) * jax.nn.gelu(mm(_C_U, _C_VG))
    zv = jax.nn.gelu(mm(_C_VG, _C_GA))
    zc = zv - jnp.mean(zv, axis=-1, keepdims=True)
    var = jnp.mean(zc * zc, axis=-1, keepdims=True)
    vn_ref[...] = zc * lax.rsqrt(var + LN_EPS) * lng_ref[...] + lnb_ref[...]
    sgb_ref[...] = jax.nn.sigmoid(mm(_C_GB, _C_END))


def _proj(x, norm_g, w_in_bf, rc, rs1, rs2, ln_g, ln_b):
    n = x.shape[0]
    tm = PROJ_ROWS
    row = lambda w: pl.BlockSpec((tm, w), lambda i: (i, 0))
    full = lambda a: pl.BlockSpec(a.shape, lambda i: (0,) * a.ndim)
    return pl.pallas_call(
        _proj_body,
        grid=(n // tm,),
        in_specs=[row(D_MODEL), full(norm_g), full(w_in_bf), row(LANES), row(LANES), row(LANES),
                  full(ln_g), full(ln_b)],
        out_specs=[row(Q_WIDTH), row(KV_WIDTH), row(KV_WIDTH), row(GMLP_WIDTH), row(GMLP_WIDTH), row(D_MODEL)],
        out_shape=[jax.ShapeDtypeStruct((n, Q_WIDTH), BF16),
                   jax.ShapeDtypeStruct((n, KV_WIDTH), F32),
                   jax.ShapeDtypeStruct((n, KV_WIDTH), F32),
                   jax.ShapeDtypeStruct((n, GMLP_WIDTH), F32),
                   jax.ShapeDtypeStruct((n, GMLP_WIDTH), F32),
                   jax.ShapeDtypeStruct((n, D_MODEL), F32)],
        compiler_params=_params(("arbitrary",)),
        name="proj",
    )(x, norm_g, w_in_bf, rc, rs1, rs2, ln_g, ln_b)


def _rotary_tables(pos):
    half = ROT_DIM // 2
    inv_freq = ROPE_THETA ** (-jnp.arange(half, dtype=F32) / half)
    ang = pos.astype(F32)[:, None] * inv_freq[None, :]
    cos, sin = jnp.cos(ang), jnp.sin(ang)
    n = pos.shape[0]
    pad = jnp.zeros((n, HEAD_DIM - ROT_DIM), F32)
    zero = jnp.zeros((n, half), F32)
    rc = jnp.concatenate([cos, cos, pad + 1.0], axis=1)
    rs1 = jnp.concatenate([-sin, zero, pad], axis=1)
    rs2 = jnp.concatenate([zero, sin, pad], axis=1)
    tile = lambda t: jnp.tile(t, (1, LANES // HEAD_DIM))
    return tile(rc), tile(rs1), tile(rs2)


def _finish_rows(merged_bf, x, wout_ref, g2_ref, wr_ref, br_ref, xmid_ref, xn2_ref, lg_ref):
    xm = x + jnp.dot(merged_bf, wout_ref[...], preferred_element_type=F32)
    xmid_ref[...] = xm
    xn = _rms(xm, g2_ref[...])
    xn2_ref[...] = xn
    lg_ref[...] = jnp.dot(xn, wr_ref[...], preferred_element_type=F32,
                          precision=lax.Precision.HIGHEST) + br_ref[...]


def _mix_body(sinks_ref, q_ref, k_ref, kp_ref, v_ref, vp_ref, a_ref, vn_ref, sgb_ref, x_ref,
              wsp_ref, bsp_ref, wout_ref, g2_ref, wr_ref, br_ref,
              xmid_ref, xn2_ref, lg_ref, kcat, vcat, mrg):
    i = pl.program_id(0)
    nsub = MIX_ROWS // WINDOW
    kcat[0:WINDOW] = kp_ref[...]
    kcat[WINDOW:] = k_ref[...]
    vcat[0:WINDOW] = vp_ref[...]
    vcat[WINDOW:] = v_ref[...]

    pair_rows = (GQA_GROUP // 2) * WINDOW
    rq = lax.broadcasted_iota(jnp.int32, (pair_rows, 4 * WINDOW), 0) & (WINDOW - 1)
    ck = lax.broadcasted_iota(jnp.int32, (pair_rows, 4 * WINDOW), 1) & (2 * WINDOW - 1)
    band = (ck > rq) & (ck <= rq + WINDOW)
    lane_kv = lax.broadcasted_iota(jnp.int32, (2 * WINDOW, LANES), 1)
    lane_o = lax.broadcasted_iota(jnp.int32, (pair_rows, LANES), 1)
    row_p = lax.broadcasted_iota(jnp.int32, (pair_rows, 1), 0) >> 7

    def sub(j, carry):
        off = pl.multiple_of(j * WINDOW, WINDOW)
        rows = pl.ds(off, WINDOW)
        for g in range(GMLP_GROUPS):
            cols = slice(g * LANES, (g + 1) * LANES)
            s = jnp.dot(wsp_ref[g], vn_ref[rows, cols].astype(BF16), preferred_element_type=F32) + bsp_ref[g]
            mrg[rows, cols] = a_ref[rows, cols] * s
        kblk = kcat[pl.ds(off, 2 * WINDOW), :]
        vblk = vcat[pl.ds(off, 2 * WINDOW), :]
        kswp = pltpu.roll(kblk, HEAD_DIM, 1)
        vswp = pltpu.roll(vblk, HEAD_DIM, 1)
        kmin = jnp.where(jnp.logical_and(i == 0, j == 0), WINDOW, 0)
        allowed = band & (ck >= kmin)
        for kk in range(N_KV_HEADS):
            lo_src, hi_src = (kblk, kswp) if kk == 0 else (kswp, kblk)
            kbd = jnp.concatenate([jnp.where(lane_kv < HEAD_DIM, lo_src, 0.0),
                                   jnp.where(lane_kv >= HEAD_DIM, hi_src, 0.0)], axis=0).astype(BF16)
            lo_src, hi_src = (vblk, vswp) if kk == 0 else (vswp, vblk)
            vbd = jnp.concatenate([jnp.where(lane_kv < HEAD_DIM, lo_src, 0.0),
                                   jnp.where(lane_kv >= HEAD_DIM, hi_src, 0.0)], axis=0).astype(BF16)
            pair0 = kk * (GQA_GROUP // 2)
            qs = jnp.concatenate([q_ref[rows, (pair0 + p) * LANES:(pair0 + p + 1) * LANES]
                                  for p in range(GQA_GROUP // 2)], axis=0)
            lg = lax.dot_general(qs, kbd, (((1,), (1,)), ((), ())), preferred_element_type=F32)
            lg = jnp.where(allowed, lg, NEG_INF)
            h0 = kk * GQA_GROUP
            se = jnp.full((pair_rows, 1), sinks_ref[h0], F32)
            so = jnp.full((pair_rows, 1), sinks_ref[h0 + 1], F32)
            for p in range(1, GQA_GROUP // 2):
                se = jnp.where(row_p == p, sinks_ref[h0 + 2 * p], se)
                so = jnp.where(row_p == p, sinks_ref[h0 + 2 * p + 1], so)
            le, lo = lg[:, :2 * WINDOW], lg[:, 2 * WINDOW:]
            me = jnp.maximum(jnp.max(le, axis=1, keepdims=True), se)
            mo = jnp.maximum(jnp.max(lo, axis=1, keepdims=True), so)
            pe = jnp.exp(le - me)
            po = jnp.exp(lo - mo)
            de = jnp.sum(pe, axis=1, keepdims=True) + jnp.exp(se - me)
            do = jnp.sum(po, axis=1, keepdims=True) + jnp.exp(so - mo)
            pr = jnp.concatenate([pe, po], axis=1).astype(BF16)
            o = jnp.dot(pr, vbd, preferred_element_type=F32)
            o = o / jnp.where(lane_o < HEAD_DIM, de, do)
            for p in range(GQA_GROUP // 2):
                cols = slice((pair0 + p) * LANES, (pair0 + p + 1) * LANES)
                mrg[rows, cols] += sgb_ref[rows, cols] * o[p * WINDOW:(p + 1) * WINDOW]
        return carry

    lax.fori_loop(0, nsub, sub, 0)
    _finish_rows(mrg[...].astype(BF16), x_ref[...], wout_ref, g2_ref, wr_ref, br_ref, xmid_ref, xn2_ref, lg_ref)


def _mix(sinks, q, k, v, a, vn, sgb, x, wsp, bsp, wout, g2, wr, br):
    n = x.shape[0]
    tm = MIX_ROWS
    nsub = tm // WINDOW
    row = lambda w: pl.BlockSpec((tm, w), lambda i: (i, 0))
    prev = pl.BlockSpec((WINDOW, KV_WIDTH), lambda i: (jnp.maximum(i * nsub - 1, 0), 0))
    full = lambda arr: pl.BlockSpec(arr.shape, lambda i: (0,) * arr.ndim)
    smem = pl.BlockSpec(memory_space=pltpu.SMEM)
    return pl.pallas_call(
        _mix_body,
        grid=(n // tm,),
        in_specs=[smem, row(Q_WIDTH), row(KV_WIDTH), prev, row(KV_WIDTH), prev,
                  row(GMLP_WIDTH), row(GMLP_WIDTH), row(D_MODEL), row(D_MODEL),
                  full(wsp), full(bsp), full(wout), full(g2), full(wr), full(br)],
        out_specs=[row(D_MODEL), row(D_MODEL), row(N_EXPERTS)],
        out_shape=[jax.ShapeDtypeStruct((n, D_MODEL), F32),
                   jax.ShapeDtypeStruct((n, D_MODEL), F32),
                   jax.ShapeDtypeStruct((n, N_EXPERTS), F32)],
        scratch_shapes=[pltpu.VMEM((tm + WINDOW, KV_WIDTH), F32),
                        pltpu.VMEM((tm + WINDOW, KV_WIDTH), F32),
                        pltpu.VMEM((tm, D_MODEL), F32)],
        compiler_params=_params(("arbitrary",)),
        name="mix_prompt",
    )(sinks, q, k, k, v, v, a, vn, sgb, x, wsp, bsp, wout, g2, wr, br)


def _sample_attn_body(sink_ref, q_ref, k_ref, v_ref, o_ref):
    q = q_ref[0]
    k = k_ref[0]
    v = v_ref[0]
    nq, nk = q.shape[1], k.shape[1]
    lg = jnp.einsum("bqd,bkd->bqk", q, k, preferred_element_type=F32)
    t = lax.broadcasted_iota(jnp.int32, (1, nq, nk), 1) // GQA_GROUP
    j = lax.broadcasted_iota(jnp.int32, (1, nq, nk), 2)
    lg = jnp.where((j > t) & (j <= t + WINDOW), lg, NEG_INF)
    sink = sink_ref[0][None]
    m = jnp.maximum(jnp.max(lg, axis=2, keepdims=True), sink)
    p = jnp.exp(lg - m)
    den = jnp.sum(p, axis=2, keepdims=True) + jnp.exp(sink - m)
    o = jnp.einsum("bqk,bkd->bqd", p.astype(BF16), v, preferred_element_type=F32)
    o_ref[0] = o / den


def _sample_attn(sink_rows, qh, kh, vh):
    nb = qh.shape[1]
    bb = 32
    blk = lambda a: pl.BlockSpec((1, bb) + a.shape[2:], lambda kk, b: (kk, b, 0, 0))
    return pl.pallas_call(
        _sample_attn_body,
        grid=(N_KV_HEADS, nb // bb),
        in_specs=[pl.BlockSpec((1,) + sink_rows.shape[1:], lambda kk, b: (kk, 0, 0)), blk(qh), blk(kh), blk(vh)],
        out_specs=pl.BlockSpec((1, bb) + qh.shape[2:], lambda kk, b: (kk, b, 0, 0)),
        out_shape=jax.ShapeDtypeStruct(qh.shape, F32),
        compiler_params=_params(("arbitrary", "arbitrary")),
        name="attn_sample",
    )(sink_rows, qh, kh, vh)


def _mix_sample_body(a_ref, vn_ref, sgb_ref, o_ref, x_ref, coef_ref, bias_ref,
                     wout_ref, g2_ref, wr_ref, br_ref, xmid_ref, xn2_ref, lg_ref):
    vn = vn_ref[...]
    s = bias_ref[...] + coef_ref[0] * vn
    for d in range(1, coef_ref.shape[0]):
        s = s + coef_ref[d] * pltpu.roll(vn, d, 0)
    merged = a_ref[...] * s + sgb_ref[...] * o_ref[...]
    _finish_rows(merged.astype(BF16), x_ref[...], wout_ref, g2_ref, wr_ref, br_ref, xmid_ref, xn2_ref, lg_ref)


def _mix_sample(a, vn, sgb, o, x, coef, bias, wout, g2, wr, br):
    n = x.shape[0]
    args = (a, vn, sgb, o, x, coef, bias, wout, g2, wr, br)
    full = lambda arr: pl.BlockSpec(arr.shape, lambda i: (0,) * arr.ndim)
    return pl.pallas_call(
        _mix_sample_body,
        grid=(1,),
        in_specs=[full(arr) for arr in args],
        out_specs=[pl.BlockSpec((n, D_MODEL), lambda i: (0, 0)), pl.BlockSpec((n, D_MODEL), lambda i: (0, 0)),
                   pl.BlockSpec((n, N_EXPERTS), lambda i: (0, 0))],
        out_shape=[jax.ShapeDtypeStruct((n, D_MODEL), F32),
                   jax.ShapeDtypeStruct((n, D_MODEL), F32),
                   jax.ShapeDtypeStruct((n, N_EXPERTS), F32)],
        compiler_params=_params(("arbitrary",)),
        name="mix_sample",
    )(*args)


def _route_body(lg_ref, idx_ref, gate_ref, rank_ref, cnt_ref, base):
    i = pl.program_id(0)

    @pl.when(i == 0)
    def _():
        base[...] = jnp.zeros_like(base)

    l = lg_ref[...]
    tb = l.shape[0]
    lane = lax.broadcasted_iota(jnp.int32, l.shape, 1).astype(F32)
    vals, idxs, sels = [], [], []
    for _ in range(TOP_K):
        m = jnp.max(l, axis=1, keepdims=True)
        ik = jnp.min(jnp.where(l == m, lane, float(N_EXPERTS)), axis=1, keepdims=True)
        sel = lane == ik
        l = jnp.where(sel, -jnp.inf, l)
        vals.append(m)
        idxs.append(ik)
        sels.append(sel)
    es = [jnp.exp(vk - vals[0]) for vk in vals]
    den = es[0] + es[1] + es[2] + es[3]
    onehot = jnp.zeros(l.shape, F32)
    for sel in sels:
        onehot = onehot + sel.astype(F32)
    tri = (lax.broadcasted_iota(jnp.int32, (tb, tb), 0) > lax.broadcasted_iota(jnp.int32, (tb, tb), 1))
    before = jnp.dot(tri.astype(BF16), onehot.astype(BF16), preferred_element_type=F32) + base[...]
    ranks = [jnp.sum(jnp.where(sel, before, 0.0), axis=1, keepdims=True) for sel in sels]
    base[...] += jnp.sum(onehot, axis=0, keepdims=True)
    cnt_ref[...] = base[...].astype(jnp.int32)

    lane_o = lax.broadcasted_iota(jnp.int32, (tb, LANES), 1)

    def spread(cols, dtype):
        out = jnp.zeros((tb, LANES), dtype)
        for kx, col in enumerate(cols):
            out = jnp.where(lane_o == kx, col.astype(dtype), out)
        return out

    idx_ref[...] = spread(idxs, jnp.int32)
    gate_ref[...] = spread([e / den for e in es], F32)
    rank_ref[...] = spread(ranks, jnp.int32)


def _route(logits):
    n = logits.shape[0]
    tb = ROUTE_ROWS
    wide = pl.BlockSpec((tb, LANES), lambda i: (i, 0))
    return pl.pallas_call(
        _route_body,
        grid=(n // tb,),
        in_specs=[pl.BlockSpec((tb, N_EXPERTS), lambda i: (i, 0))],
        out_specs=[wide, wide, wide, pl.BlockSpec((1, N_EXPERTS), lambda i: (0, 0))],
        out_shape=[jax.ShapeDtypeStruct((n, LANES), jnp.int32),
                   jax.ShapeDtypeStruct((n, LANES), F32),
                   jax.ShapeDtypeStruct((n, LANES), jnp.int32),
                   jax.ShapeDtypeStruct((1, N_EXPERTS), jnp.int32)],
        scratch_shapes=[pltpu.VMEM((1, N_EXPERTS), F32)],
        compiler_params=_params(("arbitrary",)),
        name="route",
    )(logits)


def _start_row_gather(idx_ref, src_hbm, dst_buf, sem, slot, count):
    def body(r, carry):
        pltpu.make_async_copy(src_hbm.at[pl.ds(idx_ref[0, 0, r], 1)],
                              dst_buf.at[slot, pl.ds(r, 1)], sem.at[slot]).start()
        return carry

    lax.fori_loop(0, count, body, 0, unroll=8)


def _wait_row_gather(dst_buf, sem, slot):
    pltpu.make_async_copy(dst_buf.at[slot], dst_buf.at[slot], sem.at[slot]).wait()


def _expert_body(bexp_ref, nused_ref, tok0_ref, tokn_ref, x_hbm, wup_ref, wdn_ref, bup_ref, bdn_ref,
                 out_ref, xbuf, sem, wup_s, wdn_s):
    b = pl.program_id(0)
    nused = nused_ref[0]
    slot = b % 2
    pair = 2 * LANES

    @pl.when(b == 0)
    def _():
        _start_row_gather(tok0_ref, x_hbm, xbuf, sem, 0, EXPERT_ROWS)

    @pl.when(b + 1 < nused)
    def _():
        _start_row_gather(tokn_ref, x_hbm, xbuf, sem, 1 - slot, EXPERT_ROWS)

    @pl.when(jnp.logical_or(b == 0, bexp_ref[b] != bexp_ref[jnp.maximum(b - 1, 0)]))
    def _():
        r = lax.broadcasted_iota(jnp.int32, (pair, pair), 0)
        c = lax.broadcasted_iota(jnp.int32, (pair, pair), 1)
        perm = (r == jnp.where(c < LANES, 2 * c, 2 * (c - LANES) + 1)).astype(BF16)
        for g in range(2 * D_MODEL // pair):
            cols = slice(g * pair, (g + 1) * pair)
            wup_s[:, cols] = jnp.dot(wup_ref[0, :, cols].astype(BF16), perm,
                                     preferred_element_type=F32).astype(BF16)
        wdn_s[...] = wdn_ref[0].astype(BF16)

    @pl.when(b < nused)
    def _():
        _wait_row_gather(xbuf, sem, slot)
        x = xbuf[slot].astype(BF16)
        acts = []
        for g in range(2 * D_MODEL // pair):
            cols = slice(g * pair, (g + 1) * pair)
            h = jnp.dot(x, wup_s[:, cols], preferred_element_type=F32) + bup_ref[0, :, cols]
            glu = jnp.minimum(h[:, :LANES], SWIGLU_LIMIT)
            lin = jnp.clip(h[:, LANES:], -SWIGLU_LIMIT, SWIGLU_LIMIT)
            acts.append((glu * jax.nn.sigmoid(SWIGLU_ALPHA * glu) * (lin + 1.0)).astype(BF16))
        act = jnp.concatenate(acts, axis=1)
        out_ref[...] = jnp.dot(act, wdn_s[...], preferred_element_type=F32) + bdn_ref[0]

    @pl.when(b >= nused)
    def _():
        out_ref[...] = jnp.zeros_like(out_ref)


def _experts(block_exp, nused, slot_tok, xn, w_up, w_down, b_up_grouped, b_down):
    nblk = block_exp.shape[0]
    tm = EXPERT_ROWS
    per_expert = lambda a: pl.BlockSpec((1,) + a.shape[1:], lambda b, be, nu: (be[b], 0, 0))
    tok_blocks = slot_tok.reshape(nblk, 1, tm)
    grid_spec = pltpu.PrefetchScalarGridSpec(
        num_scalar_prefetch=2,
        grid=(nblk,),
        in_specs=[pl.BlockSpec((1, 1, tm), lambda b, be, nu: (0, 0, 0), memory_space=pltpu.SMEM),
                  pl.BlockSpec((1, 1, tm), lambda b, be, nu: (jnp.minimum(b + 1, nblk - 1), 0, 0),
                               memory_space=pltpu.SMEM),
                  pl.BlockSpec(memory_space=pl.ANY),
                  per_expert(w_up), per_expert(w_down), per_expert(b_up_grouped), per_expert(b_down)],
        out_specs=pl.BlockSpec((tm, D_MODEL), lambda b, be, nu: (b, 0)),
        scratch_shapes=[pltpu.VMEM((2, tm, D_MODEL), F32), pltpu.SemaphoreType.DMA((2,)),
                        pltpu.VMEM((D_MODEL, 2 * D_MODEL), BF16), pltpu.VMEM((D_MODEL, D_MODEL), BF16)],
    )
    return pl.pallas_call(
        _expert_body,
        grid_spec=grid_spec,
        out_shape=jax.ShapeDtypeStruct((nblk * tm, D_MODEL), F32),
        compiler_params=_params(("arbitrary",)),
        name="experts",
    )(block_exp, nused, tok_blocks, tok_blocks, xn, w_up, w_down, b_up_grouped, b_down)


def _combine_body(n, d0_ref, dn_ref, gate_ref, xmid_ref, gfin_ref, y_hbm, out_ref, ybuf, sem):
    i = pl.program_id(0)
    slot = i % 2
    tt = COMBINE_ROWS

    @pl.when(i == 0)
    def _():
        _start_row_gather(d0_ref, y_hbm, ybuf, sem, 0, TOP_K * tt)

    @pl.when(i + 1 < n)
    def _():
        _start_row_gather(dn_ref, y_hbm, ybuf, sem, 1 - slot, TOP_K * tt)

    _wait_row_gather(ybuf, sem, slot)
    gate = gate_ref[...]
    acc = xmid_ref[...]
    moe = ybuf[slot, 0:tt] * gate[:, 0:1]
    for kx in range(1, TOP_K):
        moe = moe + ybuf[slot, kx * tt:(kx + 1) * tt] * gate[:, kx:kx + 1]
    out_ref[...] = _rms(acc + moe, gfin_ref[...])


def _combine(dest, gates, xmid, gfin, y_sorted):
    n = xmid.shape[0]
    tt = COMBINE_ROWS
    nblk = n // tt
    dblocks = dest.reshape(nblk, tt, TOP_K).transpose(0, 2, 1).reshape(nblk, 1, TOP_K * tt)
    return pl.pallas_call(
        functools.partial(_combine_body, nblk),
        grid=(nblk,),
        in_specs=[pl.BlockSpec((1, 1, TOP_K * tt), lambda i: (0, 0, 0), memory_space=pltpu.SMEM),
                  pl.BlockSpec((1, 1, TOP_K * tt), lambda i: (jnp.minimum(i + 1, nblk - 1), 0, 0),
                               memory_space=pltpu.SMEM),
                  pl.BlockSpec((tt, LANES), lambda i: (i, 0)),
                  pl.BlockSpec((tt, D_MODEL), lambda i: (i, 0)),
                  pl.BlockSpec((1, D_MODEL), lambda i: (0, 0)),
                  pl.BlockSpec(memory_space=pl.ANY)],
        out_specs=pl.BlockSpec((tt, D_MODEL), lambda i: (i, 0)),
        out_shape=jax.ShapeDtypeStruct((n, D_MODEL), F32),
        scratch_shapes=[pltpu.VMEM((2, TOP_K * tt, D_MODEL), F32), pltpu.SemaphoreType.DMA((2,))],
        compiler_params=_params(("arbitrary",)),
        name="combine",
    )(dblocks, dblocks, gates, xmid, gfin, y_sorted)


def kernel(x_prompt, x_sample, cache_k_win, cache_v_win, norm_attn_g, w_in, ln_v_g, ln_v_b, w_spatial, b_spatial,
           attn_sinks, w_out, norm_ffn_g, w_router, b_router, w_up, b_up, w_down, b_down, norm_final_g):
    bp, tp, _ = x_prompt.shape
    bs, ts, _ = x_sample.shape
    w_buf = cache_k_win.shape[2]
    assert bp == 1 and tp % MIX_ROWS == 0 and w_buf == WINDOW and (bs * ts) % PROJ_ROWS == 0
    n_p, n_s = bp * tp, bs * ts
    row2 = lambda a: a.reshape(1, -1)

    w_in_bf = w_in[0].astype(BF16)
    w_out_bf = w_out[0].astype(BF16)
    tril = jnp.tril(jnp.ones((CHUNK, CHUNK), dtype=bool))
    wsp = jnp.where(tril[None], w_spatial[0], 0.0)
    wsp_bf = wsp.astype(BF16)
    bsp = jnp.broadcast_to(b_spatial[0][:, :, None], (GMLP_GROUPS, CHUNK, LANES))
    b_up_grouped = b_up[0].reshape(N_EXPERTS, -1, LANES, 2).transpose(0, 1, 3, 2).reshape(N_EXPERTS, 1, -1)
    bd = b_down[0][:, None, :]
    g1, g2, gfin = row2(norm_attn_g[0]), row2(norm_ffn_g[0]), row2(norm_final_g)
    lng, lnb = row2(ln_v_g[0]), row2(ln_v_b[0])
    wr, br = w_router[0], row2(b_router[0])
    sinks = attn_sinks[0]

    xp = x_prompt.reshape(n_p, D_MODEL)
    rc, rs1, rs2 = _rotary_tables(jnp.arange(tp, dtype=jnp.int32))
    q_p, k_p, v_p, a_p, vn_p, sgb_p = _proj(xp, g1, w_in_bf, rc, rs1, rs2, lng, lnb)
    xmid_p, xn2_p, lg_p = _mix(sinks, q_p, k_p, v_p, a_p, vn_p, sgb_p, xp, wsp_bf, bsp, w_out_bf, g2, wr, br)

    xs = x_sample.reshape(n_s, D_MODEL)
    pos_s = PAST_LEN + jnp.arange(ts, dtype=jnp.int32)
    rc, rs1, rs2 = _rotary_tables(jnp.tile(pos_s, bs))
    q_s, k_s, v_s, a_s, vn_s, sgb_s = _proj(xs, g1, w_in_bf, rc, rs1, rs2, lng, lnb)
    k_all = jnp.concatenate([cache_k_win[0], k_s.reshape(bs, ts, N_KV_HEADS, HEAD_DIM)], axis=1)
    v_all = jnp.concatenate([cache_v_win[0], v_s.reshape(bs, ts, N_KV_HEADS, HEAD_DIM)], axis=1)
    n_keys = w_buf + ts
    key_pad = (-n_keys) % 8
    to_heads = lambda t: jnp.pad(t, ((0, 0), (0, key_pad), (0, 0), (0, 0))).transpose(2, 0, 1, 3).astype(BF16)
    qh = q_s.reshape(bs, ts, N_KV_HEADS, GQA_GROUP, HEAD_DIM).transpose(2, 0, 1, 3, 4)
    qh = qh.reshape(N_KV_HEADS, bs, ts * GQA_GROUP, HEAD_DIM)
    sink_rows = jnp.tile(sinks.reshape(N_KV_HEADS, 1, GQA_GROUP), (1, ts, 1)).reshape(N_KV_HEADS, ts * GQA_GROUP, 1)
    oh = _sample_attn(sink_rows, qh, to_heads(k_all), to_heads(v_all))
    o_s = oh.reshape(N_KV_HEADS, bs, ts, GQA_GROUP, HEAD_DIM).transpose(1, 2, 0, 3, 4).reshape(n_s, Q_WIDTH)
    t_idx = jnp.arange(ts)
    coef = jnp.stack([jnp.where((t_idx >= d)[None, :], wsp[:, t_idx, jnp.maximum(t_idx - d, 0)], 0.0)
                      for d in range(ts)])
    coef = jnp.repeat(coef.transpose(0, 2, 1), GMLP_WIDTH // GMLP_GROUPS, axis=2)
    coef = jnp.tile(coef, (1, bs, 1))
    bias = jnp.tile(jnp.repeat(b_spatial[0][:, :ts].T, GMLP_WIDTH // GMLP_GROUPS, axis=1), (bs, 1))
    xmid_s, xn2_s, lg_s = _mix_sample(a_s, vn_s, sgb_s, o_s, xs, coef, bias, w_out_bf, g2, wr, br)

    n_tok = n_p + n_s
    logits = jnp.concatenate([lg_p, lg_s], axis=0)
    xn2 = jnp.concatenate([xn2_p, xn2_s], axis=0)
    idx_w, gate_w, rank_w, counts = _route(logits)
    idx, rank = idx_w[:, :TOP_K], rank_w[:, :TOP_K]
    counts = counts[0]
    tm = EXPERT_ROWS
    padded = (counts + tm - 1) // tm * tm
    pend = jnp.cumsum(padded)
    pstart = pend - padded
    dest = pstart[idx] + rank
    n_blocks = (n_tok * TOP_K) // tm + N_EXPERTS
    slot_tok = jnp.zeros((n_blocks * tm,), jnp.int32).at[dest.reshape(-1)].set(
        jnp.repeat(jnp.arange(n_tok, dtype=jnp.int32), TOP_K))
    nused = (pend[-1] // tm).astype(jnp.int32)
    blk_start = jnp.minimum(jnp.arange(n_blocks, dtype=jnp.int32), nused - 1) * tm
    block_exp = jnp.minimum(jnp.sum(pend[None, :] <= blk_start[:, None], axis=1), N_EXPERTS - 1).astype(jnp.int32)

    y_sorted = _experts(block_exp, nused.reshape(1), slot_tok, xn2, w_up[0], w_down[0], b_up_grouped, bd)

    y_p = _combine(dest[:n_p], gate_w[:n_p], xmid_p, gfin, y_sorted)
    y_s = _combine(dest[n_p:], gate_w[n_p:], xmid_s, gfin, y_sorted)

    k4 = lambda t: t.reshape(1, bp, -1, N_KV_HEADS, HEAD_DIM)
    return (y_p.reshape(bp, tp, D_MODEL),
            y_s.reshape(bs, ts, D_MODEL),
            k4(k_p[n_p - WINDOW:]),
            k4(v_p[n_p - WINDOW:]),
            vn_p[n_p - CHUNK:].reshape(1, bp, CHUNK, GMLP_WIDTH),
            k_all[None, :, ts:],
            v_all[None, :, ts:],
            vn_s.reshape(1, bs, ts, GMLP_WIDTH))
```

```python
import functools

import numpy as np
import jax
import jax.numpy as jnp
from jax import lax
from jax.experimental import pallas as pl
from jax.experimental.pallas import tpu as pltpu
from jax.experimental.pallas import tpu_sc as plsc

F32 = jnp.float32
BF16 = jnp.bfloat16

D_MODEL = 1024
HEAD_DIM = 64
N_HEADS = 16
GQA_GROUP = 8
N_KV_HEADS = 2
Q_WIDTH = 1024
KV_WIDTH = 128
WINDOW = 128
ROT_DIM = 16
ROPE_THETA = 500000.0
CHUNK = 128
GMLP_WIDTH = 1024
GMLP_GROUPS = 8
N_EXPERTS = 32
TOP_K = 4
SWIGLU_LIMIT = 7.0
SWIGLU_ALPHA = 1.702
RMS_EPS = 1e-5
LN_EPS = 1e-5
NEG_INF = -1e30
PAST_LEN = 16384

LANES = 128
VMEM_LIMIT = 56 * 1024 * 1024

PROJ_ROWS = 256
MIX_ROWS = 512
ROUTE_ROWS = 512
EXPERT_ROWS = 256
COMBINE_ROWS = 256

SC_CORES = 2
SC_WORKERS = 32
SC_DISPATCH_ROWS = 16
SC_COLLECT_ROWS = 32

_C_Q, _C_KV, _C_U, _C_VG, _C_GA, _C_GB, _C_END = 0, 1024, 1280, 2304, 3328, 4352, 5376


def _params(sem):
    return pltpu.CompilerParams(dimension_semantics=sem, vmem_limit_bytes=VMEM_LIMIT)


def _rms(x, g):
    return x * lax.rsqrt(jnp.mean(x * x, axis=-1, keepdims=True) + RMS_EPS) * g


def _proj_body(x_ref, g_ref, w_ref, rc_ref, rs1_ref, rs2_ref, lng_ref, lnb_ref,
               q_ref, k_ref, v_ref, a_ref, vn_ref, sgb_ref):
    h = _rms(x_ref[...], g_ref[...]).astype(BF16)
    rc, rs1, rs2 = rc_ref[...], rs1_ref[...], rs2_ref[...]

    def rot(z):
        return z * rc + pltpu.roll(z, LANES - ROT_DIM // 2, 1) * rs1 + pltpu.roll(z, ROT_DIM // 2, 1) * rs2

    def mm(lo, hi):
        return jnp.dot(h, w_ref[:, lo:hi], preferred_element_type=F32)

    zq = mm(_C_Q, _C_KV)
    for c in range(Q_WIDTH // LANES):
        sl = slice(c * LANES, (c + 1) * LANES)
        q_ref[:, sl] = (rot(zq[:, sl]) * (HEAD_DIM ** -0.5)).astype(BF16)
    zkv = mm(_C_KV, _C_U)
    k_ref[...] = rot(zkv[:, :KV_WIDTH])
    v_ref[...] = zkv[:, KV_WIDTH:]
    a_ref[...] = jax.nn.sigmoid(mm(_C_GA, _C_GB)) * jax.nn.gelu(mm(_C_U, _C_VG))
    zv = jax.nn.gelu(mm(_C_VG, _C_GA))
    zc = zv - jnp.mean(zv, axis=-1, keepdims=True)
    var = jnp.mean(zc * zc, axis=-1, keepdims=True)
    vn_ref[...] = zc * lax.rsqrt(var + LN_EPS) * lng_ref[...] + lnb_ref[...]
    sgb_ref[...] = jax.nn.sigmoid(mm(_C_GB, _C_END))


def _proj(x, norm_g, w_in_bf, rc, rs1, rs2, ln_g, ln_b):
    n = x.shape[0]
    tm = PROJ_ROWS
    row = lambda w: pl.BlockSpec((tm, w), lambda i: (i, 0))
    full = lambda a: pl.BlockSpec(a.shape, lambda i: (0,) * a.ndim)
    return pl.pallas_call(
        _proj_body,
        grid=(n // tm,),
        in_specs=[row(D_MODEL), full(norm_g), full(w_in_bf), row(LANES), row(LANES), row(LANES),
                  full(ln_g), full(ln_b)],
        out_specs=[row(Q_WIDTH), row(KV_WIDTH), row(KV_WIDTH), row(GMLP_WIDTH), row(GMLP_WIDTH), row(D_MODEL)],
        out_shape=[jax.ShapeDtypeStruct((n, Q_WIDTH), BF16),
                   jax.ShapeDtypeStruct((n, KV_WIDTH), F32),
                   jax.ShapeDtypeStruct((n, KV_WIDTH), F32),
                   jax.ShapeDtypeStruct((n, GMLP_WIDTH), F32),
                   jax.ShapeDtypeStruct((n, GMLP_WIDTH), F32),
                   jax.ShapeDtypeStruct((n, D_MODEL), F32)],
        compiler_params=_params(("arbitrary",)),
        name="proj",
    )(x, norm_g, w_in_bf, rc, rs1, rs2, ln_g, ln_b)


def _rotary_tables(pos):
    half = ROT_DIM // 2
    inv_freq = ROPE_THETA ** (-jnp.arange(half, dtype=F32) / half)
    ang = pos.astype(F32)[:, None] * inv_freq[None, :]
    cos, sin = jnp.cos(ang), jnp.sin(ang)
    n = pos.shape[0]
    pad = jnp.zeros((n, HEAD_DIM - ROT_DIM), F32)
    zero = jnp.zeros((n, half), F32)
    rc = jnp.concatenate([cos, cos, pad + 1.0], axis=1)
    rs1 = jnp.concatenate([-sin, zero, pad], axis=1)
    rs2 = jnp.concatenate([zero, sin, pad], axis=1)
    tile = lambda t: jnp.tile(t, (1, LANES // HEAD_DIM))
    return tile(rc), tile(rs1), tile(rs2)


def _finish_rows(merged_bf, x, wout_ref, g2_ref, wr_ref, br_ref, xmid_ref, xn2_ref, lg_ref):
    xm = x + jnp.dot(merged_bf, wout_ref[...], preferred_element_type=F32)
    xmid_ref[...] = xm
    xn = _rms(xm, g2_ref[...])
    xn2_ref[...] = xn
    lg_ref[...] = jnp.dot(xn, wr_ref[...], preferred_element_type=F32,
                          precision=lax.Precision.HIGHEST) + br_ref[...]


def _mix_body(sinks_ref, q_ref, k_ref, kp_ref, v_ref, vp_ref, a_ref, vn_ref, sgb_ref, x_ref,
              wsp_ref, bsp_ref, wout_ref, g2_ref, wr_ref, br_ref,
              xmid_ref, xn2_ref, lg_ref, kcat, vcat, mrg):
    i = pl.program_id(0)
    nsub = MIX_ROWS // WINDOW
    kcat[0:WINDOW] = kp_ref[...]
    kcat[WINDOW:] = k_ref[...]
    vcat[0:WINDOW] = vp_ref[...]
    vcat[WINDOW:] = v_ref[...]

    pair_rows = (GQA_GROUP // 2) * WINDOW
    rq = lax.broadcasted_iota(jnp.int32, (pair_rows, 4 * WINDOW), 0) & (WINDOW - 1)
    ck = lax.broadcasted_iota(jnp.int32, (pair_rows, 4 * WINDOW), 1) & (2 * WINDOW - 1)
    band = (ck > rq) & (ck <= rq + WINDOW)
    lane_kv = lax.broadcasted_iota(jnp.int32, (2 * WINDOW, LANES), 1)
    lane_o = lax.broadcasted_iota(jnp.int32, (pair_rows, LANES), 1)
    row_p = lax.broadcasted_iota(jnp.int32, (pair_rows, 1), 0) >> 7

    def sub(j, carry):
        off = pl.multiple_of(j * WINDOW, WINDOW)
        rows = pl.ds(off, WINDOW)
        for g in range(GMLP_GROUPS):
            cols = slice(g * LANES, (g + 1) * LANES)
            s = jnp.dot(wsp_ref[g], vn_ref[rows, cols].astype(BF16), preferred_element_type=F32) + bsp_ref[g]
            mrg[rows, cols] = a_ref[rows, cols] * s
        kblk = kcat[pl.ds(off, 2 * WINDOW), :]
        vblk = vcat[pl.ds(off, 2 * WINDOW), :]
        kswp = pltpu.roll(kblk, HEAD_DIM, 1)
        vswp = pltpu.roll(vblk, HEAD_DIM, 1)
        kmin = jnp.where(jnp.logical_and(i == 0, j == 0), WINDOW, 0)
        allowed = band & (ck >= kmin)
        for kk in range(N_KV_HEADS):
            lo_src, hi_src = (kblk, kswp) if kk == 0 else (kswp, kblk)
            kbd = jnp.concatenate([jnp.where(lane_kv < HEAD_DIM, lo_src, 0.0),
                                   jnp.where(lane_kv >= HEAD_DIM, hi_src, 0.0)], axis=0).astype(BF16)
            lo_src, hi_src = (vblk, vswp) if kk == 0 else (vswp, vblk)
            vbd = jnp.concatenate([jnp.where(lane_kv < HEAD_DIM, lo_src, 0.0),
                                   jnp.where(lane_kv >= HEAD_DIM, hi_src, 0.0)], axis=0).astype(BF16)
            pair0 = kk * (GQA_GROUP // 2)
            qs = jnp.concatenate([q_ref[rows, (pair0 + p) * LANES:(pair0 + p + 1) * LANES]
                                  for p in range(GQA_GROUP // 2)], axis=0)
            lg = lax.dot_general(qs, kbd, (((1,), (1,)), ((), ())), preferred_element_type=F32)
            lg = jnp.where(allowed, lg, NEG_INF)
            h0 = kk * GQA_GROUP
            se = jnp.full((pair_rows, 1), sinks_ref[h0], F32)
            so = jnp.full((pair_rows, 1), sinks_ref[h0 + 1], F32)
            for p in range(1, GQA_GROUP // 2):
                se = jnp.where(row_p == p, sinks_ref[h0 + 2 * p], se)
                so = jnp.where(row_p == p, sinks_ref[h0 + 2 * p + 1], so)
            le, lo = lg[:, :2 * WINDOW], lg[:, 2 * WINDOW:]
            me = jnp.maximum(jnp.max(le, axis=1, keepdims=True), se)
            mo = jnp.maximum(jnp.max(lo, axis=1, keepdims=True), so)
            pe = jnp.exp(le - me)
            po = jnp.exp(lo - mo)
            de = jnp.sum(pe, axis=1, keepdims=True) + jnp.exp(se - me)
            do = jnp.sum(po, axis=1, keepdims=True) + jnp.exp(so - mo)
            pr = jnp.concatenate([pe, po], axis=1).astype(BF16)
            o = jnp.dot(pr, vbd, preferred_element_type=F32)
            o = o / jnp.where(lane_o < HEAD_DIM, de, do)
            for p in range(GQA_GROUP // 2):
                cols = slice((pair0 + p) * LANES, (pair0 + p + 1) * LANES)
                mrg[rows, cols] += sgb_ref[rows, cols] * o[p * WINDOW:(p + 1) * WINDOW]
        return carry

    lax.fori_loop(0, nsub, sub, 0)
    _finish_rows(mrg[...].astype(BF16), x_ref[...], wout_ref, g2_ref, wr_ref, br_ref, xmid_ref, xn2_ref, lg_ref)


def _mix(sinks, q, k, v, a, vn, sgb, x, wsp, bsp, wout, g2, wr, br):
    n = x.shape[0]
    tm = MIX_ROWS
    nsub = tm // WINDOW
    row = lambda w: pl.BlockSpec((tm, w), lambda i: (i, 0))
    prev = pl.BlockSpec((WINDOW, KV_WIDTH), lambda i: (jnp.maximum(i * nsub - 1, 0), 0))
    full = lambda arr: pl.BlockSpec(arr.shape, lambda i: (0,) * arr.ndim)
    smem = pl.BlockSpec(memory_space=pltpu.SMEM)
    return pl.pallas_call(
        _mix_body,
        grid=(n // tm,),
        in_specs=[smem, row(Q_WIDTH), row(KV_WIDTH), prev, row(KV_WIDTH), prev,
                  row(GMLP_WIDTH), row(GMLP_WIDTH), row(D_MODEL), row(D_MODEL),
                  full(wsp), full(bsp), full(wout), full(g2), full(wr), full(br)],
        out_specs=[row(D_MODEL), row(D_MODEL), row(N_EXPERTS)],
        out_shape=[jax.ShapeDtypeStruct((n, D_MODEL), F32),
                   jax.ShapeDtypeStruct((n, D_MODEL), F32),
                   jax.ShapeDtypeStruct((n, N_EXPERTS), F32)],
        scratch_shapes=[pltpu.VMEM((tm + WINDOW, KV_WIDTH), F32),
                        pltpu.VMEM((tm + WINDOW, KV_WIDTH), F32),
                        pltpu.VMEM((tm, D_MODEL), F32)],
        compiler_params=_params(("arbitrary",)),
        name="mix_prompt",
    )(sinks, q, k, k, v, v, a, vn, sgb, x, wsp, bsp, wout, g2, wr, br)


def _sample_attn_body(sink_ref, q_ref, k_ref, v_ref, o_ref):
    q = q_ref[0]
    k = k_ref[0]
    v = v_ref[0]
    nq, nk = q.shape[1], k.shape[1]
    lg = jnp.einsum("bqd,bkd->bqk", q, k, preferred_element_type=F32)
    t = lax.broadcasted_iota(jnp.int32, (1, nq, nk), 1) // GQA_GROUP
    j = lax.broadcasted_iota(jnp.int32, (1, nq, nk), 2)
    lg = jnp.where((j > t) & (j <= t + WINDOW), lg, NEG_INF)
    sink = sink_ref[0][None]
    m = jnp.maximum(jnp.max(lg, axis=2, keepdims=True), sink)
    p = jnp.exp(lg - m)
    den = jnp.sum(p, axis=2, keepdims=True) + jnp.exp(sink - m)
    o = jnp.einsum("bqk,bkd->bqd", p.astype(BF16), v, preferred_element_type=F32)
    o_ref[0] = o / den


def _sample_attn(sink_rows, qh, kh, vh):
    nb = qh.shape[1]
    bb = 32
    blk = lambda a: pl.BlockSpec((1, bb) + a.shape[2:], lambda kk, b: (kk, b, 0, 0))
    return pl.pallas_call(
        _sample_attn_body,
        grid=(N_KV_HEADS, nb // bb),
        in_specs=[pl.BlockSpec((1,) + sink_rows.shape[1:], lambda kk, b: (kk, 0, 0)), blk(qh), blk(kh), blk(vh)],
        out_specs=pl.BlockSpec((1, bb) + qh.shape[2:], lambda kk, b: (kk, b, 0, 0)),
        out_shape=jax.ShapeDtypeStruct(qh.shape, F32),
        compiler_params=_params(("arbitrary", "arbitrary")),
        name="attn_sample",
    )(sink_rows, qh, kh, vh)


def _mix_sample_body(a_ref, vn_ref, sgb_ref, o_ref, x_ref, coef_ref, bias_ref,
                     wout_ref, g2_ref, wr_ref, br_ref, xmid_ref, xn2_ref, lg_ref):
    vn = vn_ref[...]
    s = bias_ref[...] + coef_ref[0] * vn
    for d in range(1, coef_ref.shape[0]):
        s = s + coef_ref[d] * pltpu.roll(vn, d, 0)
    merged = a_ref[...] * s + sgb_ref[...] * o_ref[...]
    _finish_rows(merged.astype(BF16), x_ref[...], wout_ref, g2_ref, wr_ref, br_ref, xmid_ref, xn2_ref, lg_ref)


def _mix_sample(a, vn, sgb, o, x, coef, bias, wout, g2, wr, br):
    n = x.shape[0]
    args = (a, vn, sgb, o, x, coef, bias, wout, g2, wr, br)
    full = lambda arr: pl.BlockSpec(arr.shape, lambda i: (0,) * arr.ndim)
    return pl.pallas_call(
        _mix_sample_body,
        grid=(1,),
        in_specs=[full(arr) for arr in args],
        out_specs=[pl.BlockSpec((n, D_MODEL), lambda i: (0, 0)), pl.BlockSpec((n, D_MODEL), lambda i: (0, 0)),
                   pl.BlockSpec((n, N_EXPERTS), lambda i: (0, 0))],
        out_shape=[jax.ShapeDtypeStruct((n, D_MODEL), F32),
                   jax.ShapeDtypeStruct((n, D_MODEL), F32),
                   jax.ShapeDtypeStruct((n, N_EXPERTS), F32)],
        compiler_params=_params(("arbitrary",)),
        name="mix_sample",
    )(*args)


def _route_body(lg_ref, idx_ref, gate_ref, rank_ref, cnt_ref, base):
    i = pl.program_id(0)

    @pl.when(i == 0)
    def _():
        base[...] = jnp.zeros_like(base)

    l = lg_ref[...]
    tb = l.shape[0]
    lane = lax.broadcasted_iota(jnp.int32, l.shape, 1).astype(F32)
    vals, idxs, sels = [], [], []
    for _ in range(TOP_K):
        m = jnp.max(l, axis=1, keepdims=True)
        ik = jnp.min(jnp.where(l == m, lane, float(N_EXPERTS)), axis=1, keepdims=True)
        sel = lane == ik
        l = jnp.where(sel, -jnp.inf, l)
        vals.append(m)
        idxs.append(ik)
        sels.append(sel)
    es = [jnp.exp(vk - vals[0]) for vk in vals]
    den = es[0] + es[1] + es[2] + es[3]
    onehot = jnp.zeros(l.shape, F32)
    for sel in sels:
        onehot = onehot + sel.astype(F32)
    tri = (lax.broadcasted_iota(jnp.int32, (tb, tb), 0) > lax.broadcasted_iota(jnp.int32, (tb, tb), 1))
    before = jnp.dot(tri.astype(BF16), onehot.astype(BF16), preferred_element_type=F32) + base[...]
    ranks = [jnp.sum(jnp.where(sel, before, 0.0), axis=1, keepdims=True) for sel in sels]
    base[...] += jnp.sum(onehot, axis=0, keepdims=True)
    cnt_ref[...] = base[...].astype(jnp.int32)

    lane_o = lax.broadcasted_iota(jnp.int32, (tb, LANES), 1)

    def spread(cols, dtype):
        out = jnp.zeros((tb, LANES), dtype)
        for kx, col in enumerate(cols):
            out = jnp.where(lane_o == kx, col.astype(dtype), out)
        return out

    idx_ref[...] = spread(idxs, jnp.int32)
    gate_ref[...] = spread([e / den for e in es], F32)
    rank_ref[...] = spread(ranks, jnp.int32)


def _route(logits):
    n = logits.shape[0]
    tb = ROUTE_ROWS
    wide = pl.BlockSpec((tb, LANES), lambda i: (i, 0))
    return pl.pallas_call(
        _route_body,
        grid=(n // tb,),
        in_specs=[pl.BlockSpec((tb, N_EXPERTS), lambda i: (i, 0))],
        out_specs=[wide, wide, wide, pl.BlockSpec((1, N_EXPERTS), lambda i: (0, 0))],
        out_shape=[jax.ShapeDtypeStruct((n, LANES), jnp.int32),
                   jax.ShapeDtypeStruct((n, LANES), F32),
                   jax.ShapeDtypeStruct((n, LANES), jnp.int32),
                   jax.ShapeDtypeStruct((1, N_EXPERTS), jnp.int32)],
        scratch_shapes=[pltpu.VMEM((1, N_EXPERTS), F32)],
        compiler_params=_params(("arbitrary",)),
        name="route",
    )(logits)


def _sc_mesh():
    return plsc.VectorSubcoreMesh(core_axis_name="c", subcore_axis_name="s")


def _sc_worker():
    return lax.axis_index("s") * SC_CORES + lax.axis_index("c")


def _sc_dispatch(x_p, x_s, dest_p, dest_s, n_slots):
    chunk = SC_DISPATCH_ROWS

    def per_worker(x, dest):
        per_w = x.shape[0] // SC_WORKERS
        nch = per_w // chunk
        assert per_w * SC_WORKERS == x.shape[0] and nch * chunk == per_w
        return per_w, nch, dest.reshape(SC_WORKERS, nch, chunk, TOP_K).transpose(0, 1, 3, 2)

    pw_p, nch_p, d_p = per_worker(x_p, dest_p)
    pw_s, nch_s, d_s = per_worker(x_s, dest_s)

    @functools.partial(
        pl.kernel, mesh=_sc_mesh(),
        out_type=jax.ShapeDtypeStruct((n_slots, D_MODEL), F32),
        scratch_types=[pltpu.VMEM((nch_p, TOP_K, chunk), jnp.int32),
                       pltpu.VMEM((nch_s, TOP_K, chunk), jnp.int32),
                       pltpu.VMEM((chunk, D_MODEL), F32),
                       pltpu.SemaphoreType.DMA],
        compiler_params=pltpu.CompilerParams(use_tc_tiling_on_sc=True),
        name="dispatch")
    def run(xp_hbm, xs_hbm, dp_hbm, ds_hbm, out_hbm, ip_v, is_v, rows_v, sem):
        wid = _sc_worker()
        pltpu.sync_copy(dp_hbm.at[wid], ip_v)
        pltpu.sync_copy(ds_hbm.at[wid], is_v)

        def group(x_hbm, idx_v, per_w, nch):
            def body(j, carry):
                pltpu.sync_copy(x_hbm.at[pl.ds(wid * per_w + j * chunk, chunk)], rows_v)
                copies = [pltpu.async_copy(rows_v, out_hbm.at[idx_v.at[j, kx]], sem) for kx in range(TOP_K)]
                for cp in copies:
                    cp.wait()
                return carry

            lax.fori_loop(0, nch, body, 0)

        group(xp_hbm, ip_v, pw_p, nch_p)
        group(xs_hbm, is_v, pw_s, nch_s)

    return run(x_p, x_s, d_p, d_s)


def _sc_collect(y_sorted, dest_p, dest_s):
    chunk = SC_COLLECT_ROWS

    def per_worker(dest):
        rows = dest.shape[0] * TOP_K
        per_w = rows // SC_WORKERS
        nch = per_w // chunk
        assert per_w * SC_WORKERS == rows and nch * chunk == per_w
        return rows, per_w, nch, dest.T.reshape(SC_WORKERS, nch, chunk)

    rows_p, pw_p, nch_p, d_p = per_worker(dest_p)
    rows_s, pw_s, nch_s, d_s = per_worker(dest_s)

    @functools.partial(
        pl.kernel, mesh=_sc_mesh(),
        out_type=[jax.ShapeDtypeStruct((rows_p, D_MODEL), F32), jax.ShapeDtypeStruct((rows_s, D_MODEL), F32)],
        scratch_types=[pltpu.VMEM((nch_p, chunk), jnp.int32),
                       pltpu.VMEM((nch_s, chunk), jnp.int32),
                       pltpu.VMEM((chunk, D_MODEL), F32),
                       pltpu.SemaphoreType.DMA],
        compiler_params=pltpu.CompilerParams(use_tc_tiling_on_sc=True),
        name="collect")
    def run(y_hbm, dp_hbm, ds_hbm, op_hbm, os_hbm, ip_v, is_v, rows_v, sem):
        wid = _sc_worker()
        pltpu.sync_copy(dp_hbm.at[wid], ip_v)
        pltpu.sync_copy(ds_hbm.at[wid], is_v)

        def group(idx_v, o_hbm, per_w, nch):
            def body(j, carry):
                pltpu.async_copy(y_hbm.at[idx_v.at[j]], rows_v, sem).wait()
                pltpu.sync_copy(rows_v, o_hbm.at[pl.ds(wid * per_w + j * chunk, chunk)])
                return carry

            lax.fori_loop(0, nch, body, 0)

        group(ip_v, op_hbm, pw_p, nch_p)
        group(is_v, os_hbm, pw_s, nch_s)

    return run(y_sorted, d_p, d_s)


def _expert_body(bexp_ref, nused_ref, nvalid_ref, x_ref, wup_ref, wdn_ref, bup_ref, bdn_ref,
                 out_ref, wup_s, wdn_s):
    b = pl.program_id(0)
    nused = nused_ref[0]
    pair = 2 * LANES

    @pl.when(jnp.logical_or(b == 0, bexp_ref[b] != bexp_ref[jnp.maximum(b - 1, 0)]))
    def _():
        r = lax.broadcasted_iota(jnp.int32, (pair, pair), 0)
        c = lax.broadcasted_iota(jnp.int32, (pair, pair), 1)
        perm = (r == jnp.where(c < LANES, 2 * c, 2 * (c - LANES) + 1)).astype(BF16)
        for g in range(2 * D_MODEL // pair):
            cols = slice(g * pair, (g + 1) * pair)
            wup_s[:, cols] = jnp.dot(wup_ref[0, :, cols].astype(BF16), perm,
                                     preferred_element_type=F32).astype(BF16)
        wdn_s[...] = wdn_ref[0].astype(BF16)

    @pl.when(b < nused)
    def _():
        row = lax.broadcasted_iota(jnp.int32, (EXPERT_ROWS, 1), 0)
        x = jnp.where(row < nvalid_ref[b], x_ref[...], 0.0).astype(BF16)
        acts = []
        for g in range(2 * D_MODEL // pair):
            cols = slice(g * pair, (g + 1) * pair)
            h = jnp.dot(x, wup_s[:, cols], preferred_element_type=F32) + bup_ref[0, :, cols]
            glu = jnp.minimum(h[:, :LANES], SWIGLU_LIMIT)
            lin = jnp.clip(h[:, LANES:], -SWIGLU_LIMIT, SWIGLU_LIMIT)
            acts.append((glu * jax.nn.sigmoid(SWIGLU_ALPHA * glu) * (lin + 1.0)).astype(BF16))
        act = jnp.concatenate(acts, axis=1)
        out_ref[...] = jnp.dot(act, wdn_s[...], preferred_element_type=F32) + bdn_ref[0]

    @pl.when(b >= nused)
    def _():
        out_ref[...] = jnp.zeros_like(out_ref)


def _experts(block_exp, nused, nvalid, x_sorted, w_up, w_down, b_up_grouped, b_down):
    nblk = block_exp.shape[0]
    tm = EXPERT_ROWS
    per_expert = lambda a: pl.BlockSpec((1,) + a.shape[1:], lambda b, be, nu, nv: (be[b], 0, 0))
    grid_spec = pltpu.PrefetchScalarGridSpec(
        num_scalar_prefetch=3,
        grid=(nblk,),
        in_specs=[pl.BlockSpec((tm, D_MODEL), lambda b, be, nu, nv: (jnp.minimum(b, nu[0] - 1), 0)),
                  per_expert(w_up), per_expert(w_down), per_expert(b_up_grouped), per_expert(b_down)],
        out_specs=pl.BlockSpec((tm, D_MODEL), lambda b, be, nu, nv: (b, 0)),
        scratch_shapes=[pltpu.VMEM((D_MODEL, 2 * D_MODEL), BF16), pltpu.VMEM((D_MODEL, D_MODEL), BF16)],
    )
    return pl.pallas_call(
        _expert_body,
        grid_spec=grid_spec,
        out_shape=jax.ShapeDtypeStruct((nblk * tm, D_MODEL), F32),
        compiler_params=_params(("arbitrary",)),
        name="experts",
    )(block_exp, nused, nvalid, x_sorted, w_up, w_down, b_up_grouped, b_down)


def _combine_body(gate_ref, xmid_ref, gfin_ref, y0_ref, y1_ref, y2_ref, y3_ref, out_ref):
    gate = gate_ref[...]
    moe = y0_ref[...] * gate[:, 0:1]
    for kx, y_ref in enumerate((y1_ref, y2_ref, y3_ref), start=1):
        moe = moe + y_ref[...] * gate[:, kx:kx + 1]
    out_ref[...] = _rms(xmid_ref[...] + moe, gfin_ref[...])


def _combine(gates, xmid, gfin, y_rows):
    n = xmid.shape[0]
    tt = COMBINE_ROWS
    nblk = n // tt
    choice = lambda kx: pl.BlockSpec((tt, D_MODEL), lambda i: (i + kx * nblk, 0))
    return pl.pallas_call(
        _combine_body,
        grid=(nblk,),
        in_specs=[pl.BlockSpec((tt, LANES), lambda i: (i, 0)),
                  pl.BlockSpec((tt, D_MODEL), lambda i: (i, 0)),
                  pl.BlockSpec((1, D_MODEL), lambda i: (0, 0))] + [choice(kx) for kx in range(TOP_K)],
        out_specs=pl.BlockSpec((tt, D_MODEL), lambda i: (i, 0)),
        out_shape=jax.ShapeDtypeStruct((n, D_MODEL), F32),
        compiler_params=_params(("arbitrary",)),
        name="combine",
    )(gates, xmid, gfin, y_rows, y_rows, y_rows, y_rows)


def kernel(x_prompt, x_sample, cache_k_win, cache_v_win, norm_attn_g, w_in, ln_v_g, ln_v_b, w_spatial, b_spatial,
           attn_sinks, w_out, norm_ffn_g, w_router, b_router, w_up, b_up, w_down, b_down, norm_final_g):
    bp, tp, _ = x_prompt.shape
    bs, ts, _ = x_sample.shape
    w_buf = cache_k_win.shape[2]
    assert bp == 1 and tp % MIX_ROWS == 0 and w_buf == WINDOW and (bs * ts) % PROJ_ROWS == 0
    n_p, n_s = bp * tp, bs * ts
    row2 = lambda a: a.reshape(1, -1)

    w_in_bf = w_in[0].astype(BF16)
    w_out_bf = w_out[0].astype(BF16)
    tril = jnp.tril(jnp.ones((CHUNK, CHUNK), dtype=bool))
    wsp = jnp.where(tril[None], w_spatial[0], 0.0)
    wsp_bf = wsp.astype(BF16)
    bsp = jnp.broadcast_to(b_spatial[0][:, :, None], (GMLP_GROUPS, CHUNK, LANES))
    b_up_grouped = b_up[0].reshape(N_EXPERTS, -1, LANES, 2).transpose(0, 1, 3, 2).reshape(N_EXPERTS, 1, -1)
    bd = b_down[0][:, None, :]
    g1, g2, gfin = row2(norm_attn_g[0]), row2(norm_ffn_g[0]), row2(norm_final_g)
    lng, lnb = row2(ln_v_g[0]), row2(ln_v_b[0])
    wr, br = w_router[0], row2(b_router[0])
    sinks = attn_sinks[0]

    xp = x_prompt.reshape(n_p, D_MODEL)
    rc, rs1, rs2 = _rotary_tables(jnp.arange(tp, dtype=jnp.int32))
    q_p, k_p, v_p, a_p, vn_p, sgb_p = _proj(xp, g1, w_in_bf, rc, rs1, rs2, lng, lnb)
    xmid_p, xn2_p, lg_p = _mix(sinks, q_p, k_p, v_p, a_p, vn_p, sgb_p, xp, wsp_bf, bsp, w_out_bf, g2, wr, br)

    xs = x_sample.reshape(n_s, D_MODEL)
    pos_s = PAST_LEN + jnp.arange(ts, dtype=jnp.int32)
    rc, rs1, rs2 = _rotary_tables(jnp.tile(pos_s, bs))
    q_s, k_s, v_s, a_s, vn_s, sgb_s = _proj(xs, g1, w_in_bf, rc, rs1, rs2, lng, lnb)
    k_all = jnp.concatenate([cache_k_win[0], k_s.reshape(bs, ts, N_KV_HEADS, HEAD_DIM)], axis=1)
    v_all = jnp.concatenate([cache_v_win[0], v_s.reshape(bs, ts, N_KV_HEADS, HEAD_DIM)], axis=1)
    n_keys = w_buf + ts
    key_pad = (-n_keys) % 8
    to_heads = lambda t: jnp.pad(t, ((0, 0), (0, key_pad), (0, 0), (0, 0))).transpose(2, 0, 1, 3).astype(BF16)
    qh = q_s.reshape(bs, ts, N_KV_HEADS, GQA_GROUP, HEAD_DIM).transpose(2, 0, 1, 3, 4)
    qh = qh.reshape(N_KV_HEADS, bs, ts * GQA_GROUP, HEAD_DIM)
    sink_rows = jnp.tile(sinks.reshape(N_KV_HEADS, 1, GQA_GROUP), (1, ts, 1)).reshape(N_KV_HEADS, ts * GQA_GROUP, 1)
    oh = _sample_attn(sink_rows, qh, to_heads(k_all), to_heads(v_all))
    o_s = oh.reshape(N_KV_HEADS, bs, ts, GQA_GROUP, HEAD_DIM).transpose(1, 2, 0, 3, 4).reshape(n_s, Q_WIDTH)
    t_idx = jnp.arange(ts)
    coef = jnp.stack([jnp.where((t_idx >= d)[None, :], wsp[:, t_idx, jnp.maximum(t_idx - d, 0)], 0.0)
                      for d in range(ts)])
    coef = jnp.repeat(coef.transpose(0, 2, 1), GMLP_WIDTH // GMLP_GROUPS, axis=2)
    coef = jnp.tile(coef, (1, bs, 1))
    bias = jnp.tile(jnp.repeat(b_spatial[0][:, :ts].T, GMLP_WIDTH // GMLP_GROUPS, axis=1), (bs, 1))
    xmid_s, xn2_s, lg_s = _mix_sample(a_s, vn_s, sgb_s, o_s, xs, coef, bias, w_out_bf, g2, wr, br)

    n_tok = n_p + n_s
    idx_w, gate_w, rank_w, counts = _route(jnp.concatenate([lg_p, lg_s], axis=0))
    idx, rank = idx_w[:, :TOP_K], rank_w[:, :TOP_K]
    counts = counts[0]
    tm = EXPERT_ROWS
    experts = jnp.arange(N_EXPERTS, dtype=jnp.int32)
    padded = (counts + tm - 1) // tm * tm
    pend = jnp.cumsum(padded)
    pstart = pend - padded
    dest = jnp.sum(jnp.where(idx[:, :, None] == experts, pstart, 0), axis=2) + rank
    n_blocks = (n_tok * TOP_K) // tm + N_EXPERTS
    nused = (pend[-1] // tm).astype(jnp.int32)
    blk = jnp.arange(n_blocks, dtype=jnp.int32)
    blk_start = jnp.minimum(blk, nused - 1) * tm
    block_exp = jnp.minimum(jnp.sum(pend[None, :] <= blk_start[:, None], axis=1), N_EXPERTS - 1).astype(jnp.int32)
    seg_end = jnp.sum(jnp.where(block_exp[:, None] == experts, pstart + counts, 0), axis=1)
    nvalid = jnp.where(blk < nused, jnp.clip(seg_end - blk * tm, 0, tm), 0).astype(jnp.int32)

    x_sorted = _sc_dispatch(xn2_p, xn2_s, dest[:n_p], dest[n_p:], n_blocks * tm)
    y_sorted = _experts(block_exp, nused.reshape(1), nvalid, x_sorted, w_up[0], w_down[0], b_up_grouped, bd)
    yrows_p, yrows_s = _sc_collect(y_sorted, dest[:n_p], dest[n_p:])
    y_p = _combine(gate_w[:n_p], xmid_p, gfin, yrows_p)
    y_s = _combine(gate_w[n_p:], xmid_s, gfin, yrows_s)

    k4 = lambda t: t.reshape(1, bp, -1, N_KV_HEADS, HEAD_DIM)
    return (y_p.reshape(bp, tp, D_MODEL),
            y_s.reshape(bs, ts, D_MODEL),
            k4(k_p[n_p - WINDOW:]),
            k4(v_p[n_p - WINDOW:]),
            vn_p[n_p - CHUNK:].reshape(1, bp, CHUNK, GMLP_WIDTH),
            k_all[None, :, ts:],
            v_all[None, :, ts:],
            vn_s.reshape(1, bs, ts, GMLP_WIDTH))
```

```python
import functools

import numpy as np
import jax
import jax.numpy as jnp
from jax import lax
from jax.experimental import pallas as pl
from jax.experimental.pallas import tpu as pltpu
from jax.experimental.pallas import tpu_sc as plsc

F32 = jnp.float32
BF16 = jnp.bfloat16

D_MODEL = 1024
HEAD_DIM = 64
N_HEADS = 16
GQA_GROUP = 8
N_KV_HEADS = 2
Q_WIDTH = 1024
KV_WIDTH = 128
WINDOW = 128
ROT_DIM = 16
ROPE_THETA = 500000.0
CHUNK = 128
GMLP_WIDTH = 1024
GMLP_GROUPS = 8
N_EXPERTS = 32
TOP_K = 4
SWIGLU_LIMIT = 7.0
SWIGLU_ALPHA = 1.702
RMS_EPS = 1e-5
LN_EPS = 1e-5
NEG_INF = -1e30
PAST_LEN = 16384

LANES = 128
VMEM_LIMIT = 56 * 1024 * 1024

PROJ_ROWS = 256
MIX_ROWS = 512
ROUTE_ROWS = 512
EXPERT_ROWS = 256
COMBINE_ROWS = 256

SC_CORES = 2
SC_WORKERS = 32
SC_DISPATCH_ROWS = 16
SC_COLLECT_ROWS = 32

_C_Q, _C_KV, _C_U, _C_VG, _C_GA, _C_GB, _C_END = 0, 1024, 1280, 2304, 3328, 4352, 5376


def _params(sem):
    return pltpu.CompilerParams(dimension_semantics=sem, vmem_limit_bytes=VMEM_LIMIT)


def _rms(x, g):
    return x * lax.rsqrt(jnp.mean(x * x, axis=-1, keepdims=True) + RMS_EPS) * g


def _proj_body(x_ref, g_ref, w_ref, rc_ref, rs1_ref, rs2_ref, lng_ref, lnb_ref,
               q_ref, k_ref, v_ref, a_ref, vn_ref, sgb_ref):
    h = _rms(x_ref[...], g_ref[...]).astype(BF16)
    rc, rs1, rs2 = rc_ref[...], rs1_ref[...], rs2_ref[...]

    def rot(z):
        return z * rc + pltpu.roll(z, LANES - ROT_DIM // 2, 1) * rs1 + pltpu.roll(z, ROT_DIM // 2, 1) * rs2

    def mm(lo, hi):
        return jnp.dot(h, w_ref[:, lo:hi], preferred_element_type=F32)

    zq = mm(_C_Q, _C_KV)
    for c in range(Q_WIDTH // LANES):
        sl = slice(c * LANES, (c + 1) * LANES)
        q_ref[:, sl] = (rot(zq[:, sl]) * (HEAD_DIM ** -0.5)).astype(BF16)
    zkv = mm(_C_KV, _C_U)
    k_ref[...] = rot(zkv[:, :KV_WIDTH])
    v_ref[...] = zkv[:, KV_WIDTH:]
    a_ref[...] = jax.nn.sigmoid(mm(_C_GA, _C_GB)) * jax.nn.gelu(mm(_C_U, _C_VG))
    zv = jax.nn.gelu(mm(_C_VG, _C_GA))
    zc = zv - jnp.mean(zv, axis=-1, keepdims=True)
    var = jnp.mean(zc * zc, axis=-1, keepdims=True)
    vn_ref[...] = zc * lax.rsqrt(var + LN_EPS) * lng_ref[...] + lnb_ref[...]
    sgb_ref[...] = jax.nn.sigmoid(mm(_C_GB, _C_END))


def _proj(x, norm_g, w_in_bf, rc, rs1, rs2, ln_g, ln_b):
    n = x.shape[0]
    tm = PROJ_ROWS
    row = lambda w: pl.BlockSpec((tm, w), lambda i: (i, 0))
    full = lambda a: pl.BlockSpec(a.shape, lambda i: (0,) * a.ndim)
    return pl.pallas_call(
        _proj_body,
        grid=(n // tm,),
        in_specs=[row(D_MODEL), full(norm_g), full(w_in_bf), row(LANES), row(LANES), row(LANES),
                  full(ln_g), full(ln_b)],
        out_specs=[row(Q_WIDTH), row(KV_WIDTH), row(KV_WIDTH), row(GMLP_WIDTH), row(GMLP_WIDTH), row(D_MODEL)],
        out_shape=[jax.ShapeDtypeStruct((n, Q_WIDTH), BF16),
                   jax.ShapeDtypeStruct((n, KV_WIDTH), F32),
                   jax.ShapeDtypeStruct((n, KV_WIDTH), F32),
                   jax.ShapeDtypeStruct((n, GMLP_WIDTH), F32),
                   jax.ShapeDtypeStruct((n, GMLP_WIDTH), F32),
                   jax.ShapeDtypeStruct((n, D_MODEL), F32)],
        compiler_params=_params(("arbitrary",)),
        name="proj",
    )(x, norm_g, w_in_bf, rc, rs1, rs2, ln_g, ln_b)


def _rotary_tables(pos):
    half = ROT_DIM // 2
    inv_freq = ROPE_THETA ** (-jnp.arange(half, dtype=F32) / half)
    ang = pos.astype(F32)[:, None] * inv_freq[None, :]
    cos, sin = jnp.cos(ang), jnp.sin(ang)
    n = pos.shape[0]
    pad = jnp.zeros((n, HEAD_DIM - ROT_DIM), F32)
    zero = jnp.zeros((n, half), F32)
    rc = jnp.concatenate([cos, cos, pad + 1.0], axis=1)
    rs1 = jnp.concatenate([-sin, zero, pad], axis=1)
    rs2 = jnp.concatenate([zero, sin, pad], axis=1)
    tile = lambda t: jnp.tile(t, (1, LANES // HEAD_DIM))
    return tile(rc), tile(rs1), tile(rs2)


def _finish_rows(merged_bf, x, wout_ref, g2_ref, wr_ref, br_ref, xmid_ref, xn2_ref, lg_ref):
    xm = x + jnp.dot(merged_bf, wout_ref[...], preferred_element_type=F32)
    xmid_ref[...] = xm
    xn = _rms(xm, g2_ref[...])
    lg_ref[...] = jnp.dot(xn, wr_ref[...], preferred_element_type=F32,
                          precision=lax.Precision.HIGHEST) + br_ref[...]
    bits = lax.bitcast_convert_type(xn.astype(BF16).astype(F32), jnp.uint32)
    xn2_ref[...] = (bits[:, :D_MODEL // 2] >> 16) | bits[:, D_MODEL // 2:]


def _mix_body(sinks_ref, q_ref, k_ref, kp_ref, v_ref, vp_ref, a_ref, vn_ref, sgb_ref, x_ref,
              wsp_ref, bsp_ref, wout_ref, g2_ref, wr_ref, br_ref,
              xmid_ref, xn2_ref, lg_ref, kcat, vcat, mrg):
    i = pl.program_id(0)
    nsub = MIX_ROWS // WINDOW
    kcat[0:WINDOW] = kp_ref[...]
    kcat[WINDOW:] = k_ref[...]
    vcat[0:WINDOW] = vp_ref[...]
    vcat[WINDOW:] = v_ref[...]

    pair_rows = (GQA_GROUP // 2) * WINDOW
    rq = lax.broadcasted_iota(jnp.int32, (pair_rows, 4 * WINDOW), 0) & (WINDOW - 1)
    ck = lax.broadcasted_iota(jnp.int32, (pair_rows, 4 * WINDOW), 1) & (2 * WINDOW - 1)
    band = (ck > rq) & (ck <= rq + WINDOW)
    lane_kv = lax.broadcasted_iota(jnp.int32, (2 * WINDOW, LANES), 1)
    lane_o = lax.broadcasted_iota(jnp.int32, (pair_rows, LANES), 1)
    row_p = lax.broadcasted_iota(jnp.int32, (pair_rows, 1), 0) >> 7

    def sub(j, carry):
        off = pl.multiple_of(j * WINDOW, WINDOW)
        rows = pl.ds(off, WINDOW)
        for g in range(GMLP_GROUPS):
            cols = slice(g * LANES, (g + 1) * LANES)
            s = jnp.dot(wsp_ref[g], vn_ref[rows, cols].astype(BF16), preferred_element_type=F32) + bsp_ref[g]
            mrg[rows, cols] = a_ref[rows, cols] * s
        kblk = kcat[pl.ds(off, 2 * WINDOW), :]
        vblk = vcat[pl.ds(off, 2 * WINDOW), :]
        kswp = pltpu.roll(kblk, HEAD_DIM, 1)
        vswp = pltpu.roll(vblk, HEAD_DIM, 1)
        kmin = jnp.where(jnp.logical_and(i == 0, j == 0), WINDOW, 0)
        allowed = band & (ck >= kmin)
        for kk in range(N_KV_HEADS):
            lo_src, hi_src = (kblk, kswp) if kk == 0 else (kswp, kblk)
            kbd = jnp.concatenate([jnp.where(lane_kv < HEAD_DIM, lo_src, 0.0),
                                   jnp.where(lane_kv >= HEAD_DIM, hi_src, 0.0)], axis=0).astype(BF16)
            lo_src, hi_src = (vblk, vswp) if kk == 0 else (vswp, vblk)
            vbd = jnp.concatenate([jnp.where(lane_kv < HEAD_DIM, lo_src, 0.0),
                                   jnp.where(lane_kv >= HEAD_DIM, hi_src, 0.0)], axis=0).astype(BF16)
            pair0 = kk * (GQA_GROUP // 2)
            qs = jnp.concatenate([q_ref[rows, (pair0 + p) * LANES:(pair0 + p + 1) * LANES]
                                  for p in range(GQA_GROUP // 2)], axis=0)
            lg = lax.dot_general(qs, kbd, (((1,), (1,)), ((), ())), preferred_element_type=F32)
            lg = jnp.where(allowed, lg, NEG_INF)
            h0 = kk * GQA_GROUP
            se = jnp.full((pair_rows, 1), sinks_ref[h0], F32)
            so = jnp.full((pair_rows, 1), sinks_ref[h0 + 1], F32)
            for p in range(1, GQA_GROUP // 2):
                se = jnp.where(row_p == p, sinks_ref[h0 + 2 * p], se)
                so = jnp.where(row_p == p, sinks_ref[h0 + 2 * p + 1], so)
            le, lo = lg[:, :2 * WINDOW], lg[:, 2 * WINDOW:]
            me = jnp.maximum(jnp.max(le, axis=1, keepdims=True), se)
            mo = jnp.maximum(jnp.max(lo, axis=1, keepdims=True), so)
            pe = jnp.exp(le - me)
            po = jnp.exp(lo - mo)
            de = jnp.sum(pe, axis=1, keepdims=True) + jnp.exp(se - me)
            do = jnp.sum(po, axis=1, keepdims=True) + jnp.exp(so - mo)
            pr = jnp.concatenate([pe, po], axis=1).astype(BF16)
            o = jnp.dot(pr, vbd, preferred_element_type=F32)
            o = o / jnp.where(lane_o < HEAD_DIM, de, do)
            for p in range(GQA_GROUP // 2):
                cols = slice((pair0 + p) * LANES, (pair0 + p + 1) * LANES)
                mrg[rows, cols] += sgb_ref[rows, cols] * o[p * WINDOW:(p + 1) * WINDOW]
        return carry

    lax.fori_loop(0, nsub, sub, 0)
    _finish_rows(mrg[...].astype(BF16), x_ref[...], wout_ref, g2_ref, wr_ref, br_ref, xmid_ref, xn2_ref, lg_ref)


def _mix(sinks, q, k, v, a, vn, sgb, x, wsp, bsp, wout, g2, wr, br):
    n = x.shape[0]
    tm = MIX_ROWS
    nsub = tm // WINDOW
    row = lambda w: pl.BlockSpec((tm, w), lambda i: (i, 0))
    prev = pl.BlockSpec((WINDOW, KV_WIDTH), lambda i: (jnp.maximum(i * nsub - 1, 0), 0))
    full = lambda arr: pl.BlockSpec(arr.shape, lambda i: (0,) * arr.ndim)
    smem = pl.BlockSpec(memory_space=pltpu.SMEM)
    return pl.pallas_call(
        _mix_body,
        grid=(n // tm,),
        in_specs=[smem, row(Q_WIDTH), row(KV_WIDTH), prev, row(KV_WIDTH), prev,
                  row(GMLP_WIDTH), row(GMLP_WIDTH), row(D_MODEL), row(D_MODEL),
                  full(wsp), full(bsp), full(wout), full(g2), full(wr), full(br)],
        out_specs=[row(D_MODEL), row(D_MODEL // 2), row(N_EXPERTS)],
        out_shape=[jax.ShapeDtypeStruct((n, D_MODEL), F32),
                   jax.ShapeDtypeStruct((n, D_MODEL // 2), jnp.uint32),
                   jax.ShapeDtypeStruct((n, N_EXPERTS), F32)],
        scratch_shapes=[pltpu.VMEM((tm + WINDOW, KV_WIDTH), F32),
                        pltpu.VMEM((tm + WINDOW, KV_WIDTH), F32),
                        pltpu.VMEM((tm, D_MODEL), F32)],
        compiler_params=_params(("arbitrary",)),
        name="mix_prompt",
    )(sinks, q, k, k, v, v, a, vn, sgb, x, wsp, bsp, wout, g2, wr, br)


def _sample_attn_body(sink_ref, q_ref, k_ref, v_ref, o_ref):
    q = q_ref[0]
    k = k_ref[0]
    v = v_ref[0]
    nq, nk = q.shape[1], k.shape[1]
    lg = jnp.einsum("bqd,bkd->bqk", q, k, preferred_element_type=F32)
    t = lax.broadcasted_iota(jnp.int32, (1, nq, nk), 1) // GQA_GROUP
    j = lax.broadcasted_iota(jnp.int32, (1, nq, nk), 2)
    lg = jnp.where((j > t) & (j <= t + WINDOW), lg, NEG_INF)
    sink = sink_ref[0][None]
    m = jnp.maximum(jnp.max(lg, axis=2, keepdims=True), sink)
    p = jnp.exp(lg - m)
    den = jnp.sum(p, axis=2, keepdims=True) + jnp.exp(sink - m)
    o = jnp.einsum("bqk,bkd->bqd", p.astype(BF16), v, preferred_element_type=F32)
    o_ref[0] = o / den


def _sample_attn(sink_rows, qh, kh, vh):
    nb = qh.shape[1]
    bb = 32
    blk = lambda a: pl.BlockSpec((1, bb) + a.shape[2:], lambda kk, b: (kk, b, 0, 0))
    return pl.pallas_call(
        _sample_attn_body,
        grid=(N_KV_HEADS, nb // bb),
        in_specs=[pl.BlockSpec((1,) + sink_rows.shape[1:], lambda kk, b: (kk, 0, 0)), blk(qh), blk(kh), blk(vh)],
        out_specs=pl.BlockSpec((1, bb) + qh.shape[2:], lambda kk, b: (kk, b, 0, 0)),
        out_shape=jax.ShapeDtypeStruct(qh.shape, F32),
        compiler_params=_params(("arbitrary", "arbitrary")),
        name="attn_sample",
    )(sink_rows, qh, kh, vh)


def _mix_sample_body(a_ref, vn_ref, sgb_ref, o_ref, x_ref, coef_ref, bias_ref,
                     wout_ref, g2_ref, wr_ref, br_ref, xmid_ref, xn2_ref, lg_ref):
    vn = vn_ref[...]
    s = bias_ref[...] + coef_ref[0] * vn
    for d in range(1, coef_ref.shape[0]):
        s = s + coef_ref[d] * pltpu.roll(vn, d, 0)
    merged = a_ref[...] * s + sgb_ref[...] * o_ref[...]
    _finish_rows(merged.astype(BF16), x_ref[...], wout_ref, g2_ref, wr_ref, br_ref, xmid_ref, xn2_ref, lg_ref)


def _mix_sample(a, vn, sgb, o, x, coef, bias, wout, g2, wr, br):
    n = x.shape[0]
    args = (a, vn, sgb, o, x, coef, bias, wout, g2, wr, br)
    full = lambda arr: pl.BlockSpec(arr.shape, lambda i: (0,) * arr.ndim)
    return pl.pallas_call(
        _mix_sample_body,
        grid=(1,),
        in_specs=[full(arr) for arr in args],
        out_specs=[pl.BlockSpec((n, D_MODEL), lambda i: (0, 0)), pl.BlockSpec((n, D_MODEL // 2), lambda i: (0, 0)),
                   pl.BlockSpec((n, N_EXPERTS), lambda i: (0, 0))],
        out_shape=[jax.ShapeDtypeStruct((n, D_MODEL), F32),
                   jax.ShapeDtypeStruct((n, D_MODEL // 2), jnp.uint32),
                   jax.ShapeDtypeStruct((n, N_EXPERTS), F32)],
        compiler_params=_params(("arbitrary",)),
        name="mix_sample",
    )(*args)


def _route_body(lg_ref, idx_ref, gate_ref, rank_ref, cnt_ref, base):
    i = pl.program_id(0)

    @pl.when(i == 0)
    def _():
        base[...] = jnp.zeros_like(base)

    l = lg_ref[...]
    tb = l.shape[0]
    lane = lax.broadcasted_iota(jnp.int32, l.shape, 1).astype(F32)
    vals, idxs, sels = [], [], []
    for _ in range(TOP_K):
        m = jnp.max(l, axis=1, keepdims=True)
        ik = jnp.min(jnp.where(l == m, lane, float(N_EXPERTS)), axis=1, keepdims=True)
        sel = lane == ik
        l = jnp.where(sel, -jnp.inf, l)
        vals.append(m)
        idxs.append(ik)
        sels.append(sel)
    es = [jnp.exp(vk - vals[0]) for vk in vals]
    den = es[0] + es[1] + es[2] + es[3]
    onehot = jnp.zeros(l.shape, F32)
    for sel in sels:
        onehot = onehot + sel.astype(F32)
    tri = (lax.broadcasted_iota(jnp.int32, (tb, tb), 0) > lax.broadcasted_iota(jnp.int32, (tb, tb), 1))
    before = jnp.dot(tri.astype(BF16), onehot.astype(BF16), preferred_element_type=F32) + base[...]
    ranks = [jnp.sum(jnp.where(sel, before, 0.0), axis=1, keepdims=True) for sel in sels]
    base[...] += jnp.sum(onehot, axis=0, keepdims=True)
    cnt_ref[...] = base[...].astype(jnp.int32)

    lane_o = lax.broadcasted_iota(jnp.int32, (tb, LANES), 1)

    def spread(cols, dtype):
        out = jnp.zeros((tb, LANES), dtype)
        for kx, col in enumerate(cols):
            out = jnp.where(lane_o == kx, col.astype(dtype), out)
        return out

    idx_ref[...] = spread(idxs, jnp.int32)
    gate_ref[...] = spread([e / den for e in es], F32)
    rank_ref[...] = spread(ranks, jnp.int32)


def _route(logits):
    n = logits.shape[0]
    tb = ROUTE_ROWS
    wide = pl.BlockSpec((tb, LANES), lambda i: (i, 0))
    return pl.pallas_call(
        _route_body,
        grid=(n // tb,),
        in_specs=[pl.BlockSpec((tb, N_EXPERTS), lambda i: (i, 0))],
        out_specs=[wide, wide, wide, pl.BlockSpec((1, N_EXPERTS), lambda i: (0, 0))],
        out_shape=[jax.ShapeDtypeStruct((n, LANES), jnp.int32),
                   jax.ShapeDtypeStruct((n, LANES), F32),
                   jax.ShapeDtypeStruct((n, LANES), jnp.int32),
                   jax.ShapeDtypeStruct((1, N_EXPERTS), jnp.int32)],
        scratch_shapes=[pltpu.VMEM((1, N_EXPERTS), F32)],
        compiler_params=_params(("arbitrary",)),
        name="route",
    )(logits)


def _sc_mesh():
    return plsc.VectorSubcoreMesh(core_axis_name="c", subcore_axis_name="s")


def _sc_worker():
    return lax.axis_index("s") * SC_CORES + lax.axis_index("c")


def _sc_dispatch(x_p, x_s, dest_p, dest_s, n_slots):
    chunk = SC_DISPATCH_ROWS

    def per_worker(x, dest):
        per_w = x.shape[0] // SC_WORKERS
        nch = per_w // chunk
        assert per_w * SC_WORKERS == x.shape[0] and nch * chunk == per_w
        return per_w, nch, dest.reshape(SC_WORKERS, nch, chunk, TOP_K).transpose(0, 1, 3, 2)

    pw_p, nch_p, d_p = per_worker(x_p, dest_p)
    pw_s, nch_s, d_s = per_worker(x_s, dest_s)
    width, dtype = x_p.shape[1], x_p.dtype

    @functools.partial(
        pl.kernel, mesh=_sc_mesh(),
        out_type=jax.ShapeDtypeStruct((n_slots, width), dtype),
        scratch_types=[pltpu.VMEM((nch_p, TOP_K, chunk), jnp.int32),
                       pltpu.VMEM((nch_s, TOP_K, chunk), jnp.int32),
                       pltpu.VMEM((chunk, width), dtype),
                       pltpu.SemaphoreType.DMA],
        compiler_params=pltpu.CompilerParams(use_tc_tiling_on_sc=True),
        name="dispatch")
    def run(xp_hbm, xs_hbm, dp_hbm, ds_hbm, out_hbm, ip_v, is_v, rows_v, sem):
        wid = _sc_worker()
        pltpu.sync_copy(dp_hbm.at[wid], ip_v)
        pltpu.sync_copy(ds_hbm.at[wid], is_v)

        def group(x_hbm, idx_v, per_w, nch):
            def body(j, carry):
                pltpu.sync_copy(x_hbm.at[pl.ds(wid * per_w + j * chunk, chunk)], rows_v)
                copies = [pltpu.async_copy(rows_v, out_hbm.at[idx_v.at[j, kx]], sem) for kx in range(TOP_K)]
                for cp in copies:
                    cp.wait()
                return carry

            lax.fori_loop(0, nch, body, 0)

        group(xp_hbm, ip_v, pw_p, nch_p)
        group(xs_hbm, is_v, pw_s, nch_s)

    return run(x_p, x_s, d_p, d_s)


def _sc_collect(y_sorted, dest_p, dest_s):
    chunk = SC_COLLECT_ROWS

    def per_worker(dest):
        rows = dest.shape[0] * TOP_K
        per_w = rows // SC_WORKERS
        nch = per_w // chunk
        assert per_w * SC_WORKERS == rows and nch * chunk == per_w
        return rows, per_w, nch, dest.T.reshape(SC_WORKERS, nch, chunk)

    rows_p, pw_p, nch_p, d_p = per_worker(dest_p)
    rows_s, pw_s, nch_s, d_s = per_worker(dest_s)

    @functools.partial(
        pl.kernel, mesh=_sc_mesh(),
        out_type=[jax.ShapeDtypeStruct((rows_p, D_MODEL), F32), jax.ShapeDtypeStruct((rows_s, D_MODEL), F32)],
        scratch_types=[pltpu.VMEM((nch_p, chunk), jnp.int32),
                       pltpu.VMEM((nch_s, chunk), jnp.int32),
                       pltpu.VMEM((chunk, D_MODEL), F32),
                       pltpu.SemaphoreType.DMA],
        compiler_params=pltpu.CompilerParams(use_tc_tiling_on_sc=True),
        name="collect")
    def run(y_hbm, dp_hbm, ds_hbm, op_hbm, os_hbm, ip_v, is_v, rows_v, sem):
        wid = _sc_worker()
        pltpu.sync_copy(dp_hbm.at[wid], ip_v)
        pltpu.sync_copy(ds_hbm.at[wid], is_v)

        def group(idx_v, o_hbm, per_w, nch):
            def body(j, carry):
                pltpu.async_copy(y_hbm.at[idx_v.at[j]], rows_v, sem).wait()
                pltpu.sync_copy(rows_v, o_hbm.at[pl.ds(wid * per_w + j * chunk, chunk)])
                return carry

            lax.fori_loop(0, nch, body, 0)

        group(ip_v, op_hbm, pw_p, nch_p)
        group(is_v, os_hbm, pw_s, nch_s)

    return run(y_sorted, d_p, d_s)


def _expert_body(blk0_ref, nblk_ref, cnt_ref, wup_ref, wdn_ref, bup_ref, bdn_ref, x_hbm, y_hbm,
                 wup_s, wdn_s, xbuf, obuf, in_sem, out_sem):
    e = pl.program_id(0)
    nb = nblk_ref[e]
    blk0 = blk0_ref[e]
    cnt = cnt_ref[e]
    tm = EXPERT_ROWS
    pair = 2 * LANES

    def x_copy(i, slot):
        rows = pl.ds(pl.multiple_of((blk0 + i) * tm, tm), tm)
        return pltpu.make_async_copy(x_hbm.at[rows], xbuf.at[slot], in_sem.at[slot])

    def y_copy(i, slot):
        rows = pl.ds(pl.multiple_of((blk0 + i) * tm, tm), tm)
        return pltpu.make_async_copy(obuf.at[slot], y_hbm.at[rows], out_sem.at[slot])

    @pl.when(nb > 0)
    def _():
        x_copy(0, 0).start()
        r = lax.broadcasted_iota(jnp.int32, (pair, pair), 0)
        c = lax.broadcasted_iota(jnp.int32, (pair, pair), 1)
        perm = (r == jnp.where(c < LANES, 2 * c, 2 * (c - LANES) + 1)).astype(BF16)
        for g in range(2 * D_MODEL // pair):
            cols = slice(g * pair, (g + 1) * pair)
            wup_s[:, cols] = jnp.dot(wup_ref[0, :, cols].astype(BF16), perm,
                                     preferred_element_type=F32).astype(BF16)
        wdn_s[...] = wdn_ref[0].astype(BF16)

        def block(i, carry):
            slot = i % 2
            x_copy(i, slot).wait()

            @pl.when(i + 1 < nb)
            def _():
                x_copy(i + 1, 1 - slot).start()

            @pl.when(i >= 2)
            def _():
                y_copy(i - 2, slot).wait()

            words = xbuf[slot]
            lo = lax.bitcast_convert_type(words << 16, F32)
            hi = lax.bitcast_convert_type(words & jnp.uint32(0xFFFF0000), F32)
            row = lax.broadcasted_iota(jnp.int32, (tm, 1), 0)
            x = jnp.where(row < cnt - i * tm, jnp.concatenate([lo, hi], axis=1), 0.0).astype(BF16)
            acts = []
            for g in range(2 * D_MODEL // pair):
                cols = slice(g * pair, (g + 1) * pair)
                h = jnp.dot(x, wup_s[:, cols], preferred_element_type=F32) + bup_ref[0, :, cols]
                glu = jnp.minimum(h[:, :LANES], SWIGLU_LIMIT)
                lin = jnp.clip(h[:, LANES:], -SWIGLU_LIMIT, SWIGLU_LIMIT)
                acts.append((glu * jax.nn.sigmoid(SWIGLU_ALPHA * glu) * (lin + 1.0)).astype(BF16))
            act = jnp.concatenate(acts, axis=1)
            obuf[slot] = jnp.dot(act, wdn_s[...], preferred_element_type=F32) + bdn_ref[0]
            y_copy(i, slot).start()
            return carry

        lax.fori_loop(0, nb, block, 0)

        @pl.when(nb >= 2)
        def _():
            y_copy(nb - 2, nb % 2).wait()

        y_copy(nb - 1, (nb - 1) % 2).wait()


def _experts(blk0, nblk, cnt, x_sorted, w_up, w_down, b_up_grouped, b_down):
    tm = EXPERT_ROWS
    per_expert = lambda a: pl.BlockSpec((1,) + a.shape[1:], lambda e, b0, nb, ct: (e, 0, 0))
    grid_spec = pltpu.PrefetchScalarGridSpec(
        num_scalar_prefetch=3,
        grid=(N_EXPERTS,),
        in_specs=[per_expert(w_up), per_expert(w_down), per_expert(b_up_grouped), per_expert(b_down),
                  pl.BlockSpec(memory_space=pl.ANY)],
        out_specs=pl.BlockSpec(memory_space=pl.ANY),
        scratch_shapes=[pltpu.VMEM((D_MODEL, 2 * D_MODEL), BF16), pltpu.VMEM((D_MODEL, D_MODEL), BF16),
                        pltpu.VMEM((2, tm, x_sorted.shape[1]), x_sorted.dtype), pltpu.VMEM((2, tm, D_MODEL), F32),
                        pltpu.SemaphoreType.DMA((2,)), pltpu.SemaphoreType.DMA((2,))],
    )
    return pl.pallas_call(
        _expert_body,
        grid_spec=grid_spec,
        out_shape=jax.ShapeDtypeStruct((x_sorted.shape[0], D_MODEL), F32),
        compiler_params=_params(("arbitrary",)),
        name="experts",
    )(blk0, nblk, cnt, w_up, w_down, b_up_grouped, b_down, x_sorted)


def _combine_body(gate_ref, xmid_ref, gfin_ref, y0_ref, y1_ref, y2_ref, y3_ref, out_ref):
    gate = gate_ref[...]
    moe = y0_ref[...] * gate[:, 0:1]
    for kx, y_ref in enumerate((y1_ref, y2_ref, y3_ref), start=1):
        moe = moe + y_ref[...] * gate[:, kx:kx + 1]
    out_ref[...] = _rms(xmid_ref[...] + moe, gfin_ref[...])


def _combine(gates, xmid, gfin, y_rows):
    n = xmid.shape[0]
    tt = COMBINE_ROWS
    nblk = n // tt
    choice = lambda kx: pl.BlockSpec((tt, D_MODEL), lambda i: (i + kx * nblk, 0))
    return pl.pallas_call(
        _combine_body,
        grid=(nblk,),
        in_specs=[pl.BlockSpec((tt, LANES), lambda i: (i, 0)),
                  pl.BlockSpec((tt, D_MODEL), lambda i: (i, 0)),
                  pl.BlockSpec((1, D_MODEL), lambda i: (0, 0))] + [choice(kx) for kx in range(TOP_K)],
        out_specs=pl.BlockSpec((tt, D_MODEL), lambda i: (i, 0)),
        out_shape=jax.ShapeDtypeStruct((n, D_MODEL), F32),
        compiler_params=_params(("arbitrary",)),
        name="combine",
    )(gates, xmid, gfin, y_rows, y_rows, y_rows, y_rows)


def kernel(x_prompt, x_sample, cache_k_win, cache_v_win, norm_attn_g, w_in, ln_v_g, ln_v_b, w_spatial, b_spatial,
           attn_sinks, w_out, norm_ffn_g, w_router, b_router, w_up, b_up, w_down, b_down, norm_final_g):
    bp, tp, _ = x_prompt.shape
    bs, ts, _ = x_sample.shape
    w_buf = cache_k_win.shape[2]
    assert bp == 1 and tp % MIX_ROWS == 0 and w_buf == WINDOW and (bs * ts) % PROJ_ROWS == 0
    n_p, n_s = bp * tp, bs * ts
    row2 = lambda a: a.reshape(1, -1)

    w_in_bf = w_in[0].astype(BF16)
    w_out_bf = w_out[0].astype(BF16)
    tril = jnp.tril(jnp.ones((CHUNK, CHUNK), dtype=bool))
    wsp = jnp.where(tril[None], w_spatial[0], 0.0)
    wsp_bf = wsp.astype(BF16)
    bsp = jnp.broadcast_to(b_spatial[0][:, :, None], (GMLP_GROUPS, CHUNK, LANES))
    b_up_grouped = b_up[0].reshape(N_EXPERTS, -1, LANES, 2).transpose(0, 1, 3, 2).reshape(N_EXPERTS, 1, -1)
    bd = b_down[0][:, None, :]
    g1, g2, gfin = row2(norm_attn_g[0]), row2(norm_ffn_g[0]), row2(norm_final_g)
    lng, lnb = row2(ln_v_g[0]), row2(ln_v_b[0])
    wr, br = w_router[0], row2(b_router[0])
    sinks = attn_sinks[0]

    xp = x_prompt.reshape(n_p, D_MODEL)
    rc, rs1, rs2 = _rotary_tables(jnp.arange(tp, dtype=jnp.int32))
    q_p, k_p, v_p, a_p, vn_p, sgb_p = _proj(xp, g1, w_in_bf, rc, rs1, rs2, lng, lnb)
    xmid_p, xn2_p, lg_p = _mix(sinks, q_p, k_p, v_p, a_p, vn_p, sgb_p, xp, wsp_bf, bsp, w_out_bf, g2, wr, br)

    xs = x_sample.reshape(n_s, D_MODEL)
    pos_s = PAST_LEN + jnp.arange(ts, dtype=jnp.int32)
    rc, rs1, rs2 = _rotary_tables(jnp.tile(pos_s, bs))
    q_s, k_s, v_s, a_s, vn_s, sgb_s = _proj(xs, g1, w_in_bf, rc, rs1, rs2, lng, lnb)
    k_all = jnp.concatenate([cache_k_win[0], k_s.reshape(bs, ts, N_KV_HEADS, HEAD_DIM)], axis=1)
    v_all = jnp.concatenate([cache_v_win[0], v_s.reshape(bs, ts, N_KV_HEADS, HEAD_DIM)], axis=1)
    n_keys = w_buf + ts
    key_pad = (-n_keys) % 8
    to_heads = lambda t: jnp.pad(t, ((0, 0), (0, key_pad), (0, 0), (0, 0))).transpose(2, 0, 1, 3).astype(BF16)
    qh = q_s.reshape(bs, ts, N_KV_HEADS, GQA_GROUP, HEAD_DIM).transpose(2, 0, 1, 3, 4)
    qh = qh.reshape(N_KV_HEADS, bs, ts * GQA_GROUP, HEAD_DIM)
    sink_rows = jnp.tile(sinks.reshape(N_KV_HEADS, 1, GQA_GROUP), (1, ts, 1)).reshape(N_KV_HEADS, ts * GQA_GROUP, 1)
    oh = _sample_attn(sink_rows, qh, to_heads(k_all), to_heads(v_all))
    o_s = oh.reshape(N_KV_HEADS, bs, ts, GQA_GROUP, HEAD_DIM).transpose(1, 2, 0, 3, 4).reshape(n_s, Q_WIDTH)
    t_idx = jnp.arange(ts)
    coef = jnp.stack([jnp.where((t_idx >= d)[None, :], wsp[:, t_idx, jnp.maximum(t_idx - d, 0)], 0.0)
                      for d in range(ts)])
    coef = jnp.repeat(coef.transpose(0, 2, 1), GMLP_WIDTH // GMLP_GROUPS, axis=2)
    coef = jnp.tile(coef, (1, bs, 1))
    bias = jnp.tile(jnp.repeat(b_spatial[0][:, :ts].T, GMLP_WIDTH // GMLP_GROUPS, axis=1), (bs, 1))
    xmid_s, xn2_s, lg_s = _mix_sample(a_s, vn_s, sgb_s, o_s, xs, coef, bias, w_out_bf, g2, wr, br)

    n_tok = n_p + n_s
    idx_w, gate_w, rank_w, counts = _route(jnp.concatenate([lg_p, lg_s], axis=0))
    idx, rank = idx_w[:, :TOP_K], rank_w[:, :TOP_K]
    counts = counts[0]
    tm = EXPERT_ROWS
    experts = jnp.arange(N_EXPERTS, dtype=jnp.int32)
    padded = (counts + tm - 1) // tm * tm
    pend = jnp.cumsum(padded)
    pstart = pend - padded
    dest = jnp.sum(jnp.where(idx[:, :, None] == experts, pstart, 0), axis=2) + rank
    n_blocks = (n_tok * TOP_K) // tm + N_EXPERTS

    x_sorted = _sc_dispatch(xn2_p, xn2_s, dest[:n_p], dest[n_p:], n_blocks * tm)
    y_sorted = _experts(pstart // tm, padded // tm, counts, x_sorted, w_up[0], w_down[0], b_up_grouped, bd)
    yrows_p, yrows_s = _sc_collect(y_sorted, dest[:n_p], dest[n_p:])
    y_p = _combine(gate_w[:n_p], xmid_p, gfin, yrows_p)
    y_s = _combine(gate_w[n_p:], xmid_s, gfin, yrows_s)

    k4 = lambda t: t.reshape(1, bp, -1, N_KV_HEADS, HEAD_DIM)
    return (y_p.reshape(bp, tp, D_MODEL),
            y_s.reshape(bs, ts, D_MODEL),
            k4(k_p[n_p - WINDOW:]),
            k4(v_p[n_p - WINDOW:]),
            vn_p[n_p - CHUNK:].reshape(1, bp, CHUNK, GMLP_WIDTH),
            k_all[None, :, ts:],
            v_all[None, :, ts:],
            vn_s.reshape(1, bs, ts, GMLP_WIDTH))
```

```python
import functools

import numpy as np
import jax
import jax.numpy as jnp
from jax import lax
from jax.experimental import pallas as pl
from jax.experimental.pallas import tpu as pltpu
from jax.experimental.pallas import tpu_sc as plsc

F32 = jnp.float32
BF16 = jnp.bfloat16

D_MODEL = 1024
HEAD_DIM = 64
N_HEADS = 16
GQA_GROUP = 8
N_KV_HEADS = 2
Q_WIDTH = 1024
KV_WIDTH = 128
WINDOW = 128
ROT_DIM = 16
ROPE_THETA = 500000.0
CHUNK = 128
GMLP_WIDTH = 1024
GMLP_GROUPS = 8
N_EXPERTS = 32
TOP_K = 4
SWIGLU_LIMIT = 7.0
SWIGLU_ALPHA = 1.702
RMS_EPS = 1e-5
LN_EPS = 1e-5
NEG_INF = -1e30
PAST_LEN = 16384

LANES = 128
VMEM_LIMIT = 56 * 1024 * 1024

PROJ_ROWS = 256
MIX_ROWS = 512
ROUTE_ROWS = 512
EXPERT_ROWS = 256
COMBINE_ROWS = 256

SC_CORES = 2
SC_WORKERS = 32
SC_DISPATCH_ROWS = 16
SC_COLLECT_ROWS = 32

_C_Q, _C_KV, _C_U, _C_VG, _C_GA, _C_GB, _C_END = 0, 1024, 1280, 2304, 3328, 4352, 5376


def _params(sem):
    return pltpu.CompilerParams(dimension_semantics=sem, vmem_limit_bytes=VMEM_LIMIT)


def _rms(x, g):
    return x * lax.rsqrt(jnp.mean(x * x, axis=-1, keepdims=True) + RMS_EPS) * g


def _proj_body(x_ref, g_ref, w_ref, rc_ref, rs1_ref, rs2_ref, lng_ref, lnb_ref,
               q_ref, k_ref, v_ref, a_ref, vn_ref, sgb_ref):
    h = _rms(x_ref[...], g_ref[...]).astype(BF16)
    rc, rs1, rs2 = rc_ref[...], rs1_ref[...], rs2_ref[...]

    def rot(z):
        return z * rc + pltpu.roll(z, LANES - ROT_DIM // 2, 1) * rs1 + pltpu.roll(z, ROT_DIM // 2, 1) * rs2

    def mm(lo, hi):
        return jnp.dot(h, w_ref[:, lo:hi], preferred_element_type=F32)

    zq = mm(_C_Q, _C_KV)
    for c in range(Q_WIDTH // LANES):
        sl = slice(c * LANES, (c + 1) * LANES)
        q_ref[:, sl] = (rot(zq[:, sl]) * (HEAD_DIM ** -0.5)).astype(BF16)
    zkv = mm(_C_KV, _C_U)
    k_ref[...] = rot(zkv[:, :KV_WIDTH])
    v_ref[...] = zkv[:, KV_WIDTH:]
    a_ref[...] = jax.nn.sigmoid(mm(_C_GA, _C_GB)) * jax.nn.gelu(mm(_C_U, _C_VG))
    zv = jax.nn.gelu(mm(_C_VG, _C_GA))
    zc = zv - jnp.mean(zv, axis=-1, keepdims=True)
    var = jnp.mean(zc * zc, axis=-1, keepdims=True)
    vn_ref[...] = zc * lax.rsqrt(var + LN_EPS) * lng_ref[...] + lnb_ref[...]
    sgb_ref[...] = jax.nn.sigmoid(mm(_C_GB, _C_END))


def _proj(x, norm_g, w_in_bf, rc, rs1, rs2, ln_g, ln_b):
    n = x.shape[0]
    tm = PROJ_ROWS
    row = lambda w: pl.BlockSpec((tm, w), lambda i: (i, 0))
    full = lambda a: pl.BlockSpec(a.shape, lambda i: (0,) * a.ndim)
    return pl.pallas_call(
        _proj_body,
        grid=(n // tm,),
        in_specs=[row(D_MODEL), full(norm_g), full(w_in_bf), row(LANES), row(LANES), row(LANES),
                  full(ln_g), full(ln_b)],
        out_specs=[row(Q_WIDTH), row(KV_WIDTH), row(KV_WIDTH), row(GMLP_WIDTH), row(GMLP_WIDTH), row(D_MODEL)],
        out_shape=[jax.ShapeDtypeStruct((n, Q_WIDTH), BF16),
                   jax.ShapeDtypeStruct((n, KV_WIDTH), F32),
                   jax.ShapeDtypeStruct((n, KV_WIDTH), F32),
                   jax.ShapeDtypeStruct((n, GMLP_WIDTH), F32),
                   jax.ShapeDtypeStruct((n, GMLP_WIDTH), F32),
                   jax.ShapeDtypeStruct((n, D_MODEL), F32)],
        compiler_params=_params(("arbitrary",)),
        name="proj",
    )(x, norm_g, w_in_bf, rc, rs1, rs2, ln_g, ln_b)


def _rotary_tables(pos):
    half = ROT_DIM // 2
    inv_freq = ROPE_THETA ** (-jnp.arange(half, dtype=F32) / half)
    ang = pos.astype(F32)[:, None] * inv_freq[None, :]
    cos, sin = jnp.cos(ang), jnp.sin(ang)
    n = pos.shape[0]
    pad = jnp.zeros((n, HEAD_DIM - ROT_DIM), F32)
    zero = jnp.zeros((n, half), F32)
    rc = jnp.concatenate([cos, cos, pad + 1.0], axis=1)
    rs1 = jnp.concatenate([-sin, zero, pad], axis=1)
    rs2 = jnp.concatenate([zero, sin, pad], axis=1)
    tile = lambda t: jnp.tile(t, (1, LANES // HEAD_DIM))
    return tile(rc), tile(rs1), tile(rs2)


def _finish_rows(merged_bf, x, wout_ref, g2_ref, wr_ref, br_ref, xmid_ref, xn2_ref, lg_ref):
    xm = x + jnp.dot(merged_bf, wout_ref[...], preferred_element_type=F32)
    xmid_ref[...] = xm
    xn = _rms(xm, g2_ref[...])
    lg_ref[...] = jnp.dot(xn, wr_ref[...], preferred_element_type=F32,
                          precision=lax.Precision.HIGHEST) + br_ref[...]
    bits = lax.bitcast_convert_type(xn.astype(BF16).astype(F32), jnp.uint32)
    xn2_ref[...] = (bits[:, :D_MODEL // 2] >> 16) | bits[:, D_MODEL // 2:]


def _mix_body(sinks_ref, q_ref, k_ref, kp_ref, v_ref, vp_ref, a_ref, vn_ref, sgb_ref, x_ref,
              wsp_ref, bsp_ref, wout_ref, g2_ref, wr_ref, br_ref,
              xmid_ref, xn2_ref, lg_ref, kcat, vcat, mrg):
    i = pl.program_id(0)
    nsub = MIX_ROWS // WINDOW
    kcat[0:WINDOW] = kp_ref[...]
    kcat[WINDOW:] = k_ref[...]
    vcat[0:WINDOW] = vp_ref[...]
    vcat[WINDOW:] = v_ref[...]

    pair_rows = (GQA_GROUP // 2) * WINDOW
    rq = lax.broadcasted_iota(jnp.int32, (pair_rows, 4 * WINDOW), 0) & (WINDOW - 1)
    ck = lax.broadcasted_iota(jnp.int32, (pair_rows, 4 * WINDOW), 1) & (2 * WINDOW - 1)
    band = (ck > rq) & (ck <= rq + WINDOW)
    lane_kv = lax.broadcasted_iota(jnp.int32, (2 * WINDOW, LANES), 1)
    lane_o = lax.broadcasted_iota(jnp.int32, (pair_rows, LANES), 1)
    row_p = lax.broadcasted_iota(jnp.int32, (pair_rows, 1), 0) >> 7

    def sub(j, carry):
        off = pl.multiple_of(j * WINDOW, WINDOW)
        rows = pl.ds(off, WINDOW)
        for g in range(GMLP_GROUPS):
            cols = slice(g * LANES, (g + 1) * LANES)
            s = jnp.dot(wsp_ref[g], vn_ref[rows, cols].astype(BF16), preferred_element_type=F32) + bsp_ref[g]
            mrg[rows, cols] = a_ref[rows, cols] * s
        kblk = kcat[pl.ds(off, 2 * WINDOW), :]
        vblk = vcat[pl.ds(off, 2 * WINDOW), :]
        kswp = pltpu.roll(kblk, HEAD_DIM, 1)
        vswp = pltpu.roll(vblk, HEAD_DIM, 1)
        kmin = jnp.where(jnp.logical_and(i == 0, j == 0), WINDOW, 0)
        allowed = band & (ck >= kmin)
        for kk in range(N_KV_HEADS):
            lo_src, hi_src = (kblk, kswp) if kk == 0 else (kswp, kblk)
            kbd = jnp.concatenate([jnp.where(lane_kv < HEAD_DIM, lo_src, 0.0),
                                   jnp.where(lane_kv >= HEAD_DIM, hi_src, 0.0)], axis=0).astype(BF16)
            lo_src, hi_src = (vblk, vswp) if kk == 0 else (vswp, vblk)
            vbd = jnp.concatenate([jnp.where(lane_kv < HEAD_DIM, lo_src, 0.0),
                                   jnp.where(lane_kv >= HEAD_DIM, hi_src, 0.0)], axis=0).astype(BF16)
            pair0 = kk * (GQA_GROUP // 2)
            qs = jnp.concatenate([q_ref[rows, (pair0 + p) * LANES:(pair0 + p + 1) * LANES]
                                  for p in range(GQA_GROUP // 2)], axis=0)
            lg = lax.dot_general(qs, kbd, (((1,), (1,)), ((), ())), preferred_element_type=F32)
            lg = jnp.where(allowed, lg, NEG_INF)
            h0 = kk * GQA_GROUP
            se = jnp.full((pair_rows, 1), sinks_ref[h0], F32)
            so = jnp.full((pair_rows, 1), sinks_ref[h0 + 1], F32)
            for p in range(1, GQA_GROUP // 2):
                se = jnp.where(row_p == p, sinks_ref[h0 + 2 * p], se)
                so = jnp.where(row_p == p, sinks_ref[h0 + 2 * p + 1], so)
            le, lo = lg[:, :2 * WINDOW], lg[:, 2 * WINDOW:]
            me = jnp.maximum(jnp.max(le, axis=1, keepdims=True), se)
            mo = jnp.maximum(jnp.max(lo, axis=1, keepdims=True), so)
            pe = jnp.exp(le - me)
            po = jnp.exp(lo - mo)
            de = jnp.sum(pe, axis=1, keepdims=True) + jnp.exp(se - me)
            do = jnp.sum(po, axis=1, keepdims=True) + jnp.exp(so - mo)
            pr = jnp.concatenate([pe, po], axis=1).astype(BF16)
            o = jnp.dot(pr, vbd, preferred_element_type=F32)
            o = o / jnp.where(lane_o < HEAD_DIM, de, do)
            for p in range(GQA_GROUP // 2):
                cols = slice((pair0 + p) * LANES, (pair0 + p + 1) * LANES)
                mrg[rows, cols] += sgb_ref[rows, cols] * o[p * WINDOW:(p + 1) * WINDOW]
        return carry

    lax.fori_loop(0, nsub, sub, 0)
    _finish_rows(mrg[...].astype(BF16), x_ref[...], wout_ref, g2_ref, wr_ref, br_ref, xmid_ref, xn2_ref, lg_ref)


def _mix(sinks, q, k, v, a, vn, sgb, x, wsp, bsp, wout, g2, wr, br):
    n = x.shape[0]
    tm = MIX_ROWS
    nsub = tm // WINDOW
    row = lambda w: pl.BlockSpec((tm, w), lambda i: (i, 0))
    prev = pl.BlockSpec((WINDOW, KV_WIDTH), lambda i: (jnp.maximum(i * nsub - 1, 0), 0))
    full = lambda arr: pl.BlockSpec(arr.shape, lambda i: (0,) * arr.ndim)
    smem = pl.BlockSpec(memory_space=pltpu.SMEM)
    return pl.pallas_call(
        _mix_body,
        grid=(n // tm,),
        in_specs=[smem, row(Q_WIDTH), row(KV_WIDTH), prev, row(KV_WIDTH), prev,
                  row(GMLP_WIDTH), row(GMLP_WIDTH), row(D_MODEL), row(D_MODEL),
                  full(wsp), full(bsp), full(wout), full(g2), full(wr), full(br)],
        out_specs=[row(D_MODEL), row(D_MODEL // 2), row(N_EXPERTS)],
        out_shape=[jax.ShapeDtypeStruct((n, D_MODEL), F32),
                   jax.ShapeDtypeStruct((n, D_MODEL // 2), jnp.uint32),
                   jax.ShapeDtypeStruct((n, N_EXPERTS), F32)],
        scratch_shapes=[pltpu.VMEM((tm + WINDOW, KV_WIDTH), F32),
                        pltpu.VMEM((tm + WINDOW, KV_WIDTH), F32),
                        pltpu.VMEM((tm, D_MODEL), F32)],
        compiler_params=_params(("arbitrary",)),
        name="mix_prompt",
    )(sinks, q, k, k, v, v, a, vn, sgb, x, wsp, bsp, wout, g2, wr, br)


def _sample_attn_body(sink_ref, q_ref, k_ref, v_ref, o_ref):
    q = q_ref[0]
    k = k_ref[0]
    v = v_ref[0]
    nq, nk = q.shape[1], k.shape[1]
    lg = jnp.einsum("bqd,bkd->bqk", q, k, preferred_element_type=F32)
    t = lax.broadcasted_iota(jnp.int32, (1, nq, nk), 1) // GQA_GROUP
    j = lax.broadcasted_iota(jnp.int32, (1, nq, nk), 2)
    lg = jnp.where((j > t) & (j <= t + WINDOW), lg, NEG_INF)
    sink = sink_ref[0][None]
    m = jnp.maximum(jnp.max(lg, axis=2, keepdims=True), sink)
    p = jnp.exp(lg - m)
    den = jnp.sum(p, axis=2, keepdims=True) + jnp.exp(sink - m)
    o = jnp.einsum("bqk,bkd->bqd", p.astype(BF16), v, preferred_element_type=F32)
    o_ref[0] = o / den


def _sample_attn(sink_rows, qh, kh, vh):
    nb = qh.shape[1]
    bb = 32
    blk = lambda a: pl.BlockSpec((1, bb) + a.shape[2:], lambda kk, b: (kk, b, 0, 0))
    return pl.pallas_call(
        _sample_attn_body,
        grid=(N_KV_HEADS, nb // bb),
        in_specs=[pl.BlockSpec((1,) + sink_rows.shape[1:], lambda kk, b: (kk, 0, 0)), blk(qh), blk(kh), blk(vh)],
        out_specs=pl.BlockSpec((1, bb) + qh.shape[2:], lambda kk, b: (kk, b, 0, 0)),
        out_shape=jax.ShapeDtypeStruct(qh.shape, F32),
        compiler_params=_params(("arbitrary", "arbitrary")),
        name="attn_sample",
    )(sink_rows, qh, kh, vh)


def _mix_sample_body(a_ref, vn_ref, sgb_ref, o_ref, x_ref, coef_ref, bias_ref,
                     wout_ref, g2_ref, wr_ref, br_ref, xmid_ref, xn2_ref, lg_ref):
    vn = vn_ref[...]
    s = bias_ref[...] + coef_ref[0] * vn
    for d in range(1, coef_ref.shape[0]):
        s = s + coef_ref[d] * pltpu.roll(vn, d, 0)
    merged = a_ref[...] * s + sgb_ref[...] * o_ref[...]
    _finish_rows(merged.astype(BF16), x_ref[...], wout_ref, g2_ref, wr_ref, br_ref, xmid_ref, xn2_ref, lg_ref)


def _mix_sample(a, vn, sgb, o, x, coef, bias, wout, g2, wr, br):
    n = x.shape[0]
    args = (a, vn, sgb, o, x, coef, bias, wout, g2, wr, br)
    full = lambda arr: pl.BlockSpec(arr.shape, lambda i: (0,) * arr.ndim)
    return pl.pallas_call(
        _mix_sample_body,
        grid=(1,),
        in_specs=[full(arr) for arr in args],
        out_specs=[pl.BlockSpec((n, D_MODEL), lambda i: (0, 0)), pl.BlockSpec((n, D_MODEL // 2), lambda i: (0, 0)),
                   pl.BlockSpec((n, N_EXPERTS), lambda i: (0, 0))],
        out_shape=[jax.ShapeDtypeStruct((n, D_MODEL), F32),
                   jax.ShapeDtypeStruct((n, D_MODEL // 2), jnp.uint32),
                   jax.ShapeDtypeStruct((n, N_EXPERTS), F32)],
        compiler_params=_params(("arbitrary",)),
        name="mix_sample",
    )(*args)


def _route_body(lg_ref, idx_ref, gate_ref, rank_ref, cnt_ref, base):
    i = pl.program_id(0)

    @pl.when(i == 0)
    def _():
        base[...] = jnp.zeros_like(base)

    l = lg_ref[...]
    tb = l.shape[0]
    lane = lax.broadcasted_iota(jnp.int32, l.shape, 1).astype(F32)
    vals, idxs, sels = [], [], []
    for _ in range(TOP_K):
        m = jnp.max(l, axis=1, keepdims=True)
        ik = jnp.min(jnp.where(l == m, lane, float(N_EXPERTS)), axis=1, keepdims=True)
        sel = lane == ik
        l = jnp.where(sel, -jnp.inf, l)
        vals.append(m)
        idxs.append(ik)
        sels.append(sel)
    es = [jnp.exp(vk - vals[0]) for vk in vals]
    den = es[0] + es[1] + es[2] + es[3]
    onehot = jnp.zeros(l.shape, F32)
    for sel in sels:
        onehot = onehot + sel.astype(F32)
    tri = (lax.broadcasted_iota(jnp.int32, (tb, tb), 0) > lax.broadcasted_iota(jnp.int32, (tb, tb), 1))
    before = jnp.dot(tri.astype(BF16), onehot.astype(BF16), preferred_element_type=F32) + base[...]
    ranks = [jnp.sum(jnp.where(sel, before, 0.0), axis=1, keepdims=True) for sel in sels]
    base[...] += jnp.sum(onehot, axis=0, keepdims=True)
    cnt_ref[...] = base[...].astype(jnp.int32)

    lane_o = lax.broadcasted_iota(jnp.int32, (tb, LANES), 1)

    def spread(cols, dtype):
        out = jnp.zeros((tb, LANES), dtype)
        for kx, col in enumerate(cols):
            out = jnp.where(lane_o == kx, col.astype(dtype), out)
        return out

    idx_ref[...] = spread(idxs, jnp.int32)
    gate_ref[...] = spread([e / den for e in es], F32)
    rank_ref[...] = spread(ranks, jnp.int32)


def _route(logits):
    n = logits.shape[0]
    tb = ROUTE_ROWS
    wide = pl.BlockSpec((tb, LANES), lambda i: (i, 0))
    return pl.pallas_call(
        _route_body,
        grid=(n // tb,),
        in_specs=[pl.BlockSpec((tb, N_EXPERTS), lambda i: (i, 0))],
        out_specs=[wide, wide, wide, pl.BlockSpec((1, N_EXPERTS), lambda i: (0, 0))],
        out_shape=[jax.ShapeDtypeStruct((n, LANES), jnp.int32),
                   jax.ShapeDtypeStruct((n, LANES), F32),
                   jax.ShapeDtypeStruct((n, LANES), jnp.int32),
                   jax.ShapeDtypeStruct((1, N_EXPERTS), jnp.int32)],
        scratch_shapes=[pltpu.VMEM((1, N_EXPERTS), F32)],
        compiler_params=_params(("arbitrary",)),
        name="route",
    )(logits)


def _sc_mesh():
    return plsc.VectorSubcoreMesh(core_axis_name="c", subcore_axis_name="s")


def _sc_worker():
    return lax.axis_index("s") * SC_CORES + lax.axis_index("c")


def _sc_dispatch(x_p, x_s, dest_p, dest_s, n_slots):
    chunk = SC_DISPATCH_ROWS

    def per_worker(x, dest):
        per_w = x.shape[0] // SC_WORKERS
        nch = per_w // chunk
        assert per_w * SC_WORKERS == x.shape[0] and nch * chunk == per_w
        return per_w, nch, dest.reshape(SC_WORKERS, nch, chunk, TOP_K).transpose(0, 1, 3, 2)

    pw_p, nch_p, d_p = per_worker(x_p, dest_p)
    pw_s, nch_s, d_s = per_worker(x_s, dest_s)
    width, dtype = x_p.shape[1], x_p.dtype

    @functools.partial(
        pl.kernel, mesh=_sc_mesh(),
        out_type=jax.ShapeDtypeStruct((n_slots, width), dtype),
        scratch_types=[pltpu.VMEM((nch_p, TOP_K, chunk), jnp.int32),
                       pltpu.VMEM((nch_s, TOP_K, chunk), jnp.int32),
                       pltpu.VMEM((chunk, width), dtype),
                       pltpu.SemaphoreType.DMA],
        compiler_params=pltpu.CompilerParams(use_tc_tiling_on_sc=True),
        name="dispatch")
    def run(xp_hbm, xs_hbm, dp_hbm, ds_hbm, out_hbm, ip_v, is_v, rows_v, sem):
        wid = _sc_worker()
        pltpu.sync_copy(dp_hbm.at[wid], ip_v)
        pltpu.sync_copy(ds_hbm.at[wid], is_v)

        def group(x_hbm, idx_v, per_w, nch):
            def body(j, carry):
                pltpu.sync_copy(x_hbm.at[pl.ds(wid * per_w + j * chunk, chunk)], rows_v)
                copies = [pltpu.async_copy(rows_v, out_hbm.at[idx_v.at[j, kx]], sem) for kx in range(TOP_K)]
                for cp in copies:
                    cp.wait()
                return carry

            lax.fori_loop(0, nch, body, 0)

        group(xp_hbm, ip_v, pw_p, nch_p)
        group(xs_hbm, is_v, pw_s, nch_s)

    return run(x_p, x_s, d_p, d_s)


def _sc_collect(y_sorted, dest_p, dest_s):
    chunk = SC_COLLECT_ROWS

    def per_worker(dest):
        rows = dest.shape[0] * TOP_K
        per_w = rows // SC_WORKERS
        nch = per_w // chunk
        assert per_w * SC_WORKERS == rows and nch * chunk == per_w
        return rows, per_w, nch, dest.T.reshape(SC_WORKERS, nch, chunk)

    rows_p, pw_p, nch_p, d_p = per_worker(dest_p)
    rows_s, pw_s, nch_s, d_s = per_worker(dest_s)

    @functools.partial(
        pl.kernel, mesh=_sc_mesh(),
        out_type=[jax.ShapeDtypeStruct((rows_p, D_MODEL), F32), jax.ShapeDtypeStruct((rows_s, D_MODEL), F32)],
        scratch_types=[pltpu.VMEM((nch_p, chunk), jnp.int32),
                       pltpu.VMEM((nch_s, chunk), jnp.int32),
                       pltpu.VMEM((chunk, D_MODEL), F32),
                       pltpu.SemaphoreType.DMA],
        compiler_params=pltpu.CompilerParams(use_tc_tiling_on_sc=True),
        name="collect")
    def run(y_hbm, dp_hbm, ds_hbm, op_hbm, os_hbm, ip_v, is_v, rows_v, sem):
        wid = _sc_worker()
        pltpu.sync_copy(dp_hbm.at[wid], ip_v)
        pltpu.sync_copy(ds_hbm.at[wid], is_v)

        def group(idx_v, o_hbm, per_w, nch):
            def body(j, carry):
                pltpu.async_copy(y_hbm.at[idx_v.at[j]], rows_v, sem).wait()
                pltpu.sync_copy(rows_v, o_hbm.at[pl.ds(wid * per_w + j * chunk, chunk)])
                return carry

            lax.fori_loop(0, nch, body, 0)

        group(ip_v, op_hbm, pw_p, nch_p)
        group(is_v, os_hbm, pw_s, nch_s)

    return run(y_sorted, d_p, d_s)


def _expert_body(blk0_ref, nblk_ref, cnt_ref, wup_ref, wdn_ref, bup_ref, bdn_ref, x_hbm, y_hbm,
                 wup_s, wdn_s, xbuf, obuf, in_sem, out_sem):
    e = pl.program_id(0)
    nb = nblk_ref[e]
    blk0 = blk0_ref[e]
    cnt = cnt_ref[e]
    tm = EXPERT_ROWS
    pair = 2 * LANES

    def x_copy(i, slot):
        rows = pl.ds(pl.multiple_of((blk0 + i) * tm, tm), tm)
        return pltpu.make_async_copy(x_hbm.at[rows], xbuf.at[slot], in_sem.at[slot])

    def y_copy(i, slot):
        rows = pl.ds(pl.multiple_of((blk0 + i) * tm, tm), tm)
        return pltpu.make_async_copy(obuf.at[slot], y_hbm.at[rows], out_sem.at[slot])

    @pl.when(nb > 0)
    def _():
        x_copy(0, 0).start()
        r = lax.broadcasted_iota(jnp.int32, (pair, pair), 0)
        c = lax.broadcasted_iota(jnp.int32, (pair, pair), 1)
        perm = (r == jnp.where(c < LANES, 2 * c, 2 * (c - LANES) + 1)).astype(BF16)
        for g in range(2 * D_MODEL // pair):
            cols = slice(g * pair, (g + 1) * pair)
            wup_s[g] = jnp.dot(wup_ref[0, :, cols].astype(BF16), perm, preferred_element_type=F32).astype(BF16)
        for g in range(D_MODEL // pair):
            wdn_s[g] = wdn_ref[0, :, g * pair:(g + 1) * pair].astype(BF16)

        def block(i, carry):
            slot = i % 2
            x_copy(i, slot).wait()

            @pl.when(i + 1 < nb)
            def _():
                x_copy(i + 1, 1 - slot).start()

            @pl.when(i >= 2)
            def _():
                y_copy(i - 2, slot).wait()

            words = xbuf[slot]
            lo = lax.bitcast_convert_type(words << 16, F32)
            hi = lax.bitcast_convert_type(words & jnp.uint32(0xFFFF0000), F32)
            row = lax.broadcasted_iota(jnp.int32, (tm, 1), 0)
            x = jnp.where(row < cnt - i * tm, jnp.concatenate([lo, hi], axis=1), 0.0).astype(BF16)
            acts = []
            for g in range(2 * D_MODEL // pair):
                cols = slice(g * pair, (g + 1) * pair)
                h = jnp.dot(x, wup_s[g], preferred_element_type=F32) + bup_ref[0, :, cols]
                glu = jnp.minimum(h[:, :LANES], SWIGLU_LIMIT)
                lin = jnp.clip(h[:, LANES:], -SWIGLU_LIMIT, SWIGLU_LIMIT)
                acts.append((glu * jax.nn.sigmoid(SWIGLU_ALPHA * glu) * (lin + 1.0)).astype(BF16))
            act = jnp.concatenate(acts, axis=1)
            for g in range(D_MODEL // pair):
                cols = slice(g * pair, (g + 1) * pair)
                obuf[slot, :, cols] = jnp.dot(act, wdn_s[g], preferred_element_type=F32) + bdn_ref[0, :, cols]
            y_copy(i, slot).start()
            return carry

        lax.fori_loop(0, nb, block, 0)

        @pl.when(nb >= 2)
        def _():
            y_copy(nb - 2, nb % 2).wait()

        y_copy(nb - 1, (nb - 1) % 2).wait()


def _experts(blk0, nblk, cnt, x_sorted, w_up, w_down, b_up_grouped, b_down):
    tm = EXPERT_ROWS
    per_expert = lambda a: pl.BlockSpec((1,) + a.shape[1:], lambda e, b0, nb, ct: (e, 0, 0))
    grid_spec = pltpu.PrefetchScalarGridSpec(
        num_scalar_prefetch=3,
        grid=(N_EXPERTS,),
        in_specs=[per_expert(w_up), per_expert(w_down), per_expert(b_up_grouped), per_expert(b_down),
                  pl.BlockSpec(memory_space=pl.ANY)],
        out_specs=pl.BlockSpec(memory_space=pl.ANY),
        scratch_shapes=[pltpu.VMEM((2 * D_MODEL // (2 * LANES), D_MODEL, 2 * LANES), BF16),
                        pltpu.VMEM((D_MODEL // (2 * LANES), D_MODEL, 2 * LANES), BF16),
                        pltpu.VMEM((2, tm, x_sorted.shape[1]), x_sorted.dtype), pltpu.VMEM((2, tm, D_MODEL), F32),
                        pltpu.SemaphoreType.DMA((2,)), pltpu.SemaphoreType.DMA((2,))],
    )
    return pl.pallas_call(
        _expert_body,
        grid_spec=grid_spec,
        out_shape=jax.ShapeDtypeStruct((x_sorted.shape[0], D_MODEL), F32),
        compiler_params=_params(("arbitrary",)),
        name="experts",
    )(blk0, nblk, cnt, w_up, w_down, b_up_grouped, b_down, x_sorted)


def _combine_body(gate_ref, xmid_ref, gfin_ref, y0_ref, y1_ref, y2_ref, y3_ref, out_ref):
    gate = gate_ref[...]
    moe = y0_ref[...] * gate[:, 0:1]
    for kx, y_ref in enumerate((y1_ref, y2_ref, y3_ref), start=1):
        moe = moe + y_ref[...] * gate[:, kx:kx + 1]
    out_ref[...] = _rms(xmid_ref[...] + moe, gfin_ref[...])


def _combine(gates, xmid, gfin, y_rows):
    n = xmid.shape[0]
    tt = COMBINE_ROWS
    nblk = n // tt
    choice = lambda kx: pl.BlockSpec((tt, D_MODEL), lambda i: (i + kx * nblk, 0))
    return pl.pallas_call(
        _combine_body,
        grid=(nblk,),
        in_specs=[pl.BlockSpec((tt, LANES), lambda i: (i, 0)),
                  pl.BlockSpec((tt, D_MODEL), lambda i: (i, 0)),
                  pl.BlockSpec((1, D_MODEL), lambda i: (0, 0))] + [choice(kx) for kx in range(TOP_K)],
        out_specs=pl.BlockSpec((tt, D_MODEL), lambda i: (i, 0)),
        out_shape=jax.ShapeDtypeStruct((n, D_MODEL), F32),
        compiler_params=_params(("arbitrary",)),
        name="combine",
    )(gates, xmid, gfin, y_rows, y_rows, y_rows, y_rows)


def kernel(x_prompt, x_sample, cache_k_win, cache_v_win, norm_attn_g, w_in, ln_v_g, ln_v_b, w_spatial, b_spatial,
           attn_sinks, w_out, norm_ffn_g, w_router, b_router, w_up, b_up, w_down, b_down, norm_final_g):
    bp, tp, _ = x_prompt.shape
    bs, ts, _ = x_sample.shape
    w_buf = cache_k_win.shape[2]
    assert bp == 1 and tp % MIX_ROWS == 0 and w_buf == WINDOW and (bs * ts) % PROJ_ROWS == 0
    n_p, n_s = bp * tp, bs * ts
    row2 = lambda a: a.reshape(1, -1)

    w_in_bf = w_in[0].astype(BF16)
    w_out_bf = w_out[0].astype(BF16)
    tril = jnp.tril(jnp.ones((CHUNK, CHUNK), dtype=bool))
    wsp = jnp.where(tril[None], w_spatial[0], 0.0)
    wsp_bf = wsp.astype(BF16)
    bsp = jnp.broadcast_to(b_spatial[0][:, :, None], (GMLP_GROUPS, CHUNK, LANES))
    b_up_grouped = b_up[0].reshape(N_EXPERTS, -1, LANES, 2).transpose(0, 1, 3, 2).reshape(N_EXPERTS, 1, -1)
    bd = b_down[0][:, None, :]
    g1, g2, gfin = row2(norm_attn_g[0]), row2(norm_ffn_g[0]), row2(norm_final_g)
    lng, lnb = row2(ln_v_g[0]), row2(ln_v_b[0])
    wr, br = w_router[0], row2(b_router[0])
    sinks = attn_sinks[0]

    xp = x_prompt.reshape(n_p, D_MODEL)
    rc, rs1, rs2 = _rotary_tables(jnp.arange(tp, dtype=jnp.int32))
    q_p, k_p, v_p, a_p, vn_p, sgb_p = _proj(xp, g1, w_in_bf, rc, rs1, rs2, lng, lnb)
    xmid_p, xn2_p, lg_p = _mix(sinks, q_p, k_p, v_p, a_p, vn_p, sgb_p, xp, wsp_bf, bsp, w_out_bf, g2, wr, br)

    xs = x_sample.reshape(n_s, D_MODEL)
    pos_s = PAST_LEN + jnp.arange(ts, dtype=jnp.int32)
    rc, rs1, rs2 = _rotary_tables(jnp.tile(pos_s, bs))
    q_s, k_s, v_s, a_s, vn_s, sgb_s = _proj(xs, g1, w_in_bf, rc, rs1, rs2, lng, lnb)
    k_all = jnp.concatenate([cache_k_win[0], k_s.reshape(bs, ts, N_KV_HEADS, HEAD_DIM)], axis=1)
    v_all = jnp.concatenate([cache_v_win[0], v_s.reshape(bs, ts, N_KV_HEADS, HEAD_DIM)], axis=1)
    n_keys = w_buf + ts
    key_pad = (-n_keys) % 8
    to_heads = lambda t: jnp.pad(t, ((0, 0), (0, key_pad), (0, 0), (0, 0))).transpose(2, 0, 1, 3).astype(BF16)
    qh = q_s.reshape(bs, ts, N_KV_HEADS, GQA_GROUP, HEAD_DIM).transpose(2, 0, 1, 3, 4)
    qh = qh.reshape(N_KV_HEADS, bs, ts * GQA_GROUP, HEAD_DIM)
    sink_rows = jnp.tile(sinks.reshape(N_KV_HEADS, 1, GQA_GROUP), (1, ts, 1)).reshape(N_KV_HEADS, ts * GQA_GROUP, 1)
    oh = _sample_attn(sink_rows, qh, to_heads(k_all), to_heads(v_all))
    o_s = oh.reshape(N_KV_HEADS, bs, ts, GQA_GROUP, HEAD_DIM).transpose(1, 2, 0, 3, 4).reshape(n_s, Q_WIDTH)
    t_idx = jnp.arange(ts)
    coef = jnp.stack([jnp.where((t_idx >= d)[None, :], wsp[:, t_idx, jnp.maximum(t_idx - d, 0)], 0.0)
                      for d in range(ts)])
    coef = jnp.repeat(coef.transpose(0, 2, 1), GMLP_WIDTH // GMLP_GROUPS, axis=2)
    coef = jnp.tile(coef, (1, bs, 1))
    bias = jnp.tile(jnp.repeat(b_spatial[0][:, :ts].T, GMLP_WIDTH // GMLP_GROUPS, axis=1), (bs, 1))
    xmid_s, xn2_s, lg_s = _mix_sample(a_s, vn_s, sgb_s, o_s, xs, coef, bias, w_out_bf, g2, wr, br)

    n_tok = n_p + n_s
    idx_w, gate_w, rank_w, counts = _route(jnp.concatenate([lg_p, lg_s], axis=0))
    idx, rank = idx_w[:, :TOP_K], rank_w[:, :TOP_K]
    counts = counts[0]
    tm = EXPERT_ROWS
    experts = jnp.arange(N_EXPERTS, dtype=jnp.int32)
    padded = (counts + tm - 1) // tm * tm
    pend = jnp.cumsum(padded)
    pstart = pend - padded
    dest = jnp.sum(jnp.where(idx[:, :, None] == experts, pstart, 0), axis=2) + rank
    n_blocks = (n_tok * TOP_K) // tm + N_EXPERTS

    x_sorted = _sc_dispatch(xn2_p, xn2_s, dest[:n_p], dest[n_p:], n_blocks * tm)
    y_sorted = _experts(pstart // tm, padded // tm, counts, x_sorted, w_up[0], w_down[0], b_up_grouped, bd)
    yrows_p, yrows_s = _sc_collect(y_sorted, dest[:n_p], dest[n_p:])
    y_p = _combine(gate_w[:n_p], xmid_p, gfin, yrows_p)
    y_s = _combine(gate_w[n_p:], xmid_s, gfin, yrows_s)

    k4 = lambda t: t.reshape(1, bp, -1, N_KV_HEADS, HEAD_DIM)
    return (y_p.reshape(bp, tp, D_MODEL),
            y_s.reshape(bs, ts, D_MODEL),
            k4(k_p[n_p - WINDOW:]),
            k4(v_p[n_p - WINDOW:]),
            vn_p[n_p - CHUNK:].reshape(1, bp, CHUNK, GMLP_WIDTH),
            k_all[None, :, ts:],
            v_all[None, :, ts:],
            vn_s.reshape(1, bs, ts, GMLP_WIDTH))
```

```python
import functools

import numpy as np
import jax
import jax.numpy as jnp
from jax import lax
from jax.experimental import pallas as pl
from jax.experimental.pallas import tpu as pltpu
from jax.experimental.pallas import tpu_sc as plsc

F32 = jnp.float32
BF16 = jnp.bfloat16

D_MODEL = 1024
HEAD_DIM = 64
N_HEADS = 16
GQA_GROUP = 8
N_KV_HEADS = 2
Q_WIDTH = 1024
KV_WIDTH = 128
WINDOW = 128
ROT_DIM = 16
ROPE_THETA = 500000.0
CHUNK = 128
GMLP_WIDTH = 1024
GMLP_GROUPS = 8
N_EXPERTS = 32
TOP_K = 4
SWIGLU_LIMIT = 7.0
SWIGLU_ALPHA = 1.702
RMS_EPS = 1e-5
LN_EPS = 1e-5
NEG_INF = -1e30
PAST_LEN = 16384

LANES = 128
VMEM_LIMIT = 56 * 1024 * 1024

PROJ_ROWS = 256
MIX_ROWS = 512
ROUTE_ROWS = 512
EXPERT_ROWS = 512
COMBINE_ROWS = 256

SC_CORES = 2
SC_WORKERS = 32
SC_DISPATCH_ROWS = 16
SC_COLLECT_ROWS = 32

_C_Q, _C_KV, _C_U, _C_VG, _C_GA, _C_GB, _C_END = 0, 1024, 1280, 2304, 3328, 4352, 5376


def _params(sem):
    return pltpu.CompilerParams(dimension_semantics=sem, vmem_limit_bytes=VMEM_LIMIT)


def _rms(x, g):
    return x * lax.rsqrt(jnp.mean(x * x, axis=-1, keepdims=True) + RMS_EPS) * g


def _proj_body(x_ref, g_ref, w_ref, rc_ref, rs1_ref, rs2_ref, lng_ref, lnb_ref,
               q_ref, k_ref, v_ref, a_ref, vn_ref, sgb_ref):
    h = _rms(x_ref[...], g_ref[...]).astype(BF16)
    rc, rs1, rs2 = rc_ref[...], rs1_ref[...], rs2_ref[...]

    def rot(z):
        return z * rc + pltpu.roll(z, LANES - ROT_DIM // 2, 1) * rs1 + pltpu.roll(z, ROT_DIM // 2, 1) * rs2

    def mm(lo, hi):
        return jnp.dot(h, w_ref[:, lo:hi], preferred_element_type=F32)

    zq = mm(_C_Q, _C_KV)
    for c in range(Q_WIDTH // LANES):
        sl = slice(c * LANES, (c + 1) * LANES)
        q_ref[:, sl] = (rot(zq[:, sl]) * (HEAD_DIM ** -0.5)).astype(BF16)
    zkv = mm(_C_KV, _C_U)
    k_ref[...] = rot(zkv[:, :KV_WIDTH])
    v_ref[...] = zkv[:, KV_WIDTH:]
    a_ref[...] = jax.nn.sigmoid(mm(_C_GA, _C_GB)) * jax.nn.gelu(mm(_C_U, _C_VG))
    zv = jax.nn.gelu(mm(_C_VG, _C_GA))
    zc = zv - jnp.mean(zv, axis=-1, keepdims=True)
    var = jnp.mean(zc * zc, axis=-1, keepdims=True)
    vn_ref[...] = zc * lax.rsqrt(var + LN_EPS) * lng_ref[...] + lnb_ref[...]
    sgb_ref[...] = jax.nn.sigmoid(mm(_C_GB, _C_END))


def _proj(x, norm_g, w_in_bf, rc, rs1, rs2, ln_g, ln_b):
    n = x.shape[0]
    tm = PROJ_ROWS
    row = lambda w: pl.BlockSpec((tm, w), lambda i: (i, 0))
    full = lambda a: pl.BlockSpec(a.shape, lambda i: (0,) * a.ndim)
    return pl.pallas_call(
        _proj_body,
        grid=(n // tm,),
        in_specs=[row(D_MODEL), full(norm_g), full(w_in_bf), row(LANES), row(LANES), row(LANES),
                  full(ln_g), full(ln_b)],
        out_specs=[row(Q_WIDTH), row(KV_WIDTH), row(KV_WIDTH), row(GMLP_WIDTH), row(GMLP_WIDTH), row(D_MODEL)],
        out_shape=[jax.ShapeDtypeStruct((n, Q_WIDTH), BF16),
                   jax.ShapeDtypeStruct((n, KV_WIDTH), F32),
                   jax.ShapeDtypeStruct((n, KV_WIDTH), F32),
                   jax.ShapeDtypeStruct((n, GMLP_WIDTH), F32),
                   jax.ShapeDtypeStruct((n, GMLP_WIDTH), F32),
                   jax.ShapeDtypeStruct((n, D_MODEL), F32)],
        compiler_params=_params(("arbitrary",)),
        name="proj",
    )(x, norm_g, w_in_bf, rc, rs1, rs2, ln_g, ln_b)


def _rotary_tables(pos):
    half = ROT_DIM // 2
    inv_freq = ROPE_THETA ** (-jnp.arange(half, dtype=F32) / half)
    ang = pos.astype(F32)[:, None] * inv_freq[None, :]
    cos, sin = jnp.cos(ang), jnp.sin(ang)
    n = pos.shape[0]
    pad = jnp.zeros((n, HEAD_DIM - ROT_DIM), F32)
    zero = jnp.zeros((n, half), F32)
    rc = jnp.concatenate([cos, cos, pad + 1.0], axis=1)
    rs1 = jnp.concatenate([-sin, zero, pad], axis=1)
    rs2 = jnp.concatenate([zero, sin, pad], axis=1)
    tile = lambda t: jnp.tile(t, (1, LANES // HEAD_DIM))
    return tile(rc), tile(rs1), tile(rs2)


def _finish_rows(merged_bf, x, wout_ref, g2_ref, wr_ref, br_ref, xmid_ref, xn2_ref, lg_ref):
    xm = x + jnp.dot(merged_bf, wout_ref[...], preferred_element_type=F32)
    xmid_ref[...] = xm
    xn = _rms(xm, g2_ref[...])
    lg_ref[...] = jnp.dot(xn, wr_ref[...], preferred_element_type=F32,
                          precision=lax.Precision.HIGHEST) + br_ref[...]
    bits = lax.bitcast_convert_type(xn.astype(BF16).astype(F32), jnp.uint32)
    xn2_ref[...] = (bits[:, :D_MODEL // 2] >> 16) | bits[:, D_MODEL // 2:]


def _mix_body(sinks_ref, q_ref, k_ref, kp_ref, v_ref, vp_ref, a_ref, vn_ref, sgb_ref, x_ref,
              wsp_ref, bsp_ref, wout_ref, g2_ref, wr_ref, br_ref,
              xmid_ref, xn2_ref, lg_ref, kcat, vcat, mrg):
    i = pl.program_id(0)
    nsub = MIX_ROWS // WINDOW
    kcat[0:WINDOW] = kp_ref[...]
    kcat[WINDOW:] = k_ref[...]
    vcat[0:WINDOW] = vp_ref[...]
    vcat[WINDOW:] = v_ref[...]

    pair_rows = (GQA_GROUP // 2) * WINDOW
    rq = lax.broadcasted_iota(jnp.int32, (pair_rows, 4 * WINDOW), 0) & (WINDOW - 1)
    ck = lax.broadcasted_iota(jnp.int32, (pair_rows, 4 * WINDOW), 1) & (2 * WINDOW - 1)
    band = (ck > rq) & (ck <= rq + WINDOW)
    lane_kv = lax.broadcasted_iota(jnp.int32, (2 * WINDOW, LANES), 1)
    lane_o = lax.broadcasted_iota(jnp.int32, (pair_rows, LANES), 1)
    row_p = lax.broadcasted_iota(jnp.int32, (pair_rows, 1), 0) >> 7

    def sub(j, carry):
        off = pl.multiple_of(j * WINDOW, WINDOW)
        rows = pl.ds(off, WINDOW)
        for g in range(GMLP_GROUPS):
            cols = slice(g * LANES, (g + 1) * LANES)
            s = jnp.dot(wsp_ref[g], vn_ref[rows, cols].astype(BF16), preferred_element_type=F32) + bsp_ref[g]
            mrg[rows, cols] = a_ref[rows, cols] * s
        kblk = kcat[pl.ds(off, 2 * WINDOW), :]
        vblk = vcat[pl.ds(off, 2 * WINDOW), :]
        kswp = pltpu.roll(kblk, HEAD_DIM, 1)
        vswp = pltpu.roll(vblk, HEAD_DIM, 1)
        kmin = jnp.where(jnp.logical_and(i == 0, j == 0), WINDOW, 0)
        allowed = band & (ck >= kmin)
        for kk in range(N_KV_HEADS):
            lo_src, hi_src = (kblk, kswp) if kk == 0 else (kswp, kblk)
            kbd = jnp.concatenate([jnp.where(lane_kv < HEAD_DIM, lo_src, 0.0),
                                   jnp.where(lane_kv >= HEAD_DIM, hi_src, 0.0)], axis=0).astype(BF16)
            lo_src, hi_src = (vblk, vswp) if kk == 0 else (vswp, vblk)
            vbd = jnp.concatenate([jnp.where(lane_kv < HEAD_DIM, lo_src, 0.0),
                                   jnp.where(lane_kv >= HEAD_DIM, hi_src, 0.0)], axis=0).astype(BF16)
            pair0 = kk * (GQA_GROUP // 2)
            qs = jnp.concatenate([q_ref[rows, (pair0 + p) * LANES:(pair0 + p + 1) * LANES]
                                  for p in range(GQA_GROUP // 2)], axis=0)
            lg = lax.dot_general(qs, kbd, (((1,), (1,)), ((), ())), preferred_element_type=F32)
            lg = jnp.where(allowed, lg, NEG_INF)
            h0 = kk * GQA_GROUP
            se = jnp.full((pair_rows, 1), sinks_ref[h0], F32)
            so = jnp.full((pair_rows, 1), sinks_ref[h0 + 1], F32)
            for p in range(1, GQA_GROUP // 2):
                se = jnp.where(row_p == p, sinks_ref[h0 + 2 * p], se)
                so = jnp.where(row_p == p, sinks_ref[h0 + 2 * p + 1], so)
            le, lo = lg[:, :2 * WINDOW], lg[:, 2 * WINDOW:]
            me = jnp.maximum(jnp.max(le, axis=1, keepdims=True), se)
            mo = jnp.maximum(jnp.max(lo, axis=1, keepdims=True), so)
            pe = jnp.exp(le - me)
            po = jnp.exp(lo - mo)
            de = jnp.sum(pe, axis=1, keepdims=True) + jnp.exp(se - me)
            do = jnp.sum(po, axis=1, keepdims=True) + jnp.exp(so - mo)
            pr = jnp.concatenate([pe, po], axis=1).astype(BF16)
            o = jnp.dot(pr, vbd, preferred_element_type=F32)
            o = o / jnp.where(lane_o < HEAD_DIM, de, do)
            for p in range(GQA_GROUP // 2):
                cols = slice((pair0 + p) * LANES, (pair0 + p + 1) * LANES)
                mrg[rows, cols] += sgb_ref[rows, cols] * o[p * WINDOW:(p + 1) * WINDOW]
        return carry

    lax.fori_loop(0, nsub, sub, 0)
    _finish_rows(mrg[...].astype(BF16), x_ref[...], wout_ref, g2_ref, wr_ref, br_ref, xmid_ref, xn2_ref, lg_ref)


def _mix(sinks, q, k, v, a, vn, sgb, x, wsp, bsp, wout, g2, wr, br):
    n = x.shape[0]
    tm = MIX_ROWS
    nsub = tm // WINDOW
    row = lambda w: pl.BlockSpec((tm, w), lambda i: (i, 0))
    prev = pl.BlockSpec((WINDOW, KV_WIDTH), lambda i: (jnp.maximum(i * nsub - 1, 0), 0))
    full = lambda arr: pl.BlockSpec(arr.shape, lambda i: (0,) * arr.ndim)
    smem = pl.BlockSpec(memory_space=pltpu.SMEM)
    return pl.pallas_call(
        _mix_body,
        grid=(n // tm,),
        in_specs=[smem, row(Q_WIDTH), row(KV_WIDTH), prev, row(KV_WIDTH), prev,
                  row(GMLP_WIDTH), row(GMLP_WIDTH), row(D_MODEL), row(D_MODEL),
                  full(wsp), full(bsp), full(wout), full(g2), full(wr), full(br)],
        out_specs=[row(D_MODEL), row(D_MODEL // 2), row(N_EXPERTS)],
        out_shape=[jax.ShapeDtypeStruct((n, D_MODEL), F32),
                   jax.ShapeDtypeStruct((n, D_MODEL // 2), jnp.uint32),
                   jax.ShapeDtypeStruct((n, N_EXPERTS), F32)],
        scratch_shapes=[pltpu.VMEM((tm + WINDOW, KV_WIDTH), F32),
                        pltpu.VMEM((tm + WINDOW, KV_WIDTH), F32),
                        pltpu.VMEM((tm, D_MODEL), F32)],
        compiler_params=_params(("arbitrary",)),
        name="mix_prompt",
    )(sinks, q, k, k, v, v, a, vn, sgb, x, wsp, bsp, wout, g2, wr, br)


def _sample_attn_body(sink_ref, q_ref, k_ref, v_ref, o_ref):
    q = q_ref[0]
    k = k_ref[0]
    v = v_ref[0]
    nq, nk = q.shape[1], k.shape[1]
    lg = jnp.einsum("bqd,bkd->bqk", q, k, preferred_element_type=F32)
    t = lax.broadcasted_iota(jnp.int32, (1, nq, nk), 1) // GQA_GROUP
    j = lax.broadcasted_iota(jnp.int32, (1, nq, nk), 2)
    lg = jnp.where((j > t) & (j <= t + WINDOW), lg, NEG_INF)
    sink = sink_ref[0][None]
    m = jnp.maximum(jnp.max(lg, axis=2, keepdims=True), sink)
    p = jnp.exp(lg - m)
    den = jnp.sum(p, axis=2, keepdims=True) + jnp.exp(sink - m)
    o = jnp.einsum("bqk,bkd->bqd", p.astype(BF16), v, preferred_element_type=F32)
    o_ref[0] = o / den


def _sample_attn(sink_rows, qh, kh, vh):
    nb = qh.shape[1]
    bb = 32
    blk = lambda a: pl.BlockSpec((1, bb) + a.shape[2:], lambda kk, b: (kk, b, 0, 0))
    return pl.pallas_call(
        _sample_attn_body,
        grid=(N_KV_HEADS, nb // bb),
        in_specs=[pl.BlockSpec((1,) + sink_rows.shape[1:], lambda kk, b: (kk, 0, 0)), blk(qh), blk(kh), blk(vh)],
        out_specs=pl.BlockSpec((1, bb) + qh.shape[2:], lambda kk, b: (kk, b, 0, 0)),
        out_shape=jax.ShapeDtypeStruct(qh.shape, F32),
        compiler_params=_params(("arbitrary", "arbitrary")),
        name="attn_sample",
    )(sink_rows, qh, kh, vh)


def _mix_sample_body(a_ref, vn_ref, sgb_ref, o_ref, x_ref, coef_ref, bias_ref,
                     wout_ref, g2_ref, wr_ref, br_ref, xmid_ref, xn2_ref, lg_ref):
    vn = vn_ref[...]
    s = bias_ref[...] + coef_ref[0] * vn
    for d in range(1, coef_ref.shape[0]):
        s = s + coef_ref[d] * pltpu.roll(vn, d, 0)
    merged = a_ref[...] * s + sgb_ref[...] * o_ref[...]
    _finish_rows(merged.astype(BF16), x_ref[...], wout_ref, g2_ref, wr_ref, br_ref, xmid_ref, xn2_ref, lg_ref)


def _mix_sample(a, vn, sgb, o, x, coef, bias, wout, g2, wr, br):
    n = x.shape[0]
    args = (a, vn, sgb, o, x, coef, bias, wout, g2, wr, br)
    full = lambda arr: pl.BlockSpec(arr.shape, lambda i: (0,) * arr.ndim)
    return pl.pallas_call(
        _mix_sample_body,
        grid=(1,),
        in_specs=[full(arr) for arr in args],
        out_specs=[pl.BlockSpec((n, D_MODEL), lambda i: (0, 0)), pl.BlockSpec((n, D_MODEL // 2), lambda i: (0, 0)),
                   pl.BlockSpec((n, N_EXPERTS), lambda i: (0, 0))],
        out_shape=[jax.ShapeDtypeStruct((n, D_MODEL), F32),
                   jax.ShapeDtypeStruct((n, D_MODEL // 2), jnp.uint32),
                   jax.ShapeDtypeStruct((n, N_EXPERTS), F32)],
        compiler_params=_params(("arbitrary",)),
        name="mix_sample",
    )(*args)


def _route_body(lg_ref, idx_ref, gate_ref, rank_ref, cnt_ref, base):
    i = pl.program_id(0)

    @pl.when(i == 0)
    def _():
        base[...] = jnp.zeros_like(base)

    l = lg_ref[...]
    tb = l.shape[0]
    lane = lax.broadcasted_iota(jnp.int32, l.shape, 1).astype(F32)
    vals, idxs, sels = [], [], []
    for _ in range(TOP_K):
        m = jnp.max(l, axis=1, keepdims=True)
        ik = jnp.min(jnp.where(l == m, lane, float(N_EXPERTS)), axis=1, keepdims=True)
        sel = lane == ik
        l = jnp.where(sel, -jnp.inf, l)
        vals.append(m)
        idxs.append(ik)
        sels.append(sel)
    es = [jnp.exp(vk - vals[0]) for vk in vals]
    den = es[0] + es[1] + es[2] + es[3]
    onehot = jnp.zeros(l.shape, F32)
    for sel in sels:
        onehot = onehot + sel.astype(F32)
    tri = (lax.broadcasted_iota(jnp.int32, (tb, tb), 0) > lax.broadcasted_iota(jnp.int32, (tb, tb), 1))
    before = jnp.dot(tri.astype(BF16), onehot.astype(BF16), preferred_element_type=F32) + base[...]
    ranks = [jnp.sum(jnp.where(sel, before, 0.0), axis=1, keepdims=True) for sel in sels]
    base[...] += jnp.sum(onehot, axis=0, keepdims=True)
    cnt_ref[...] = base[...].astype(jnp.int32)

    lane_o = lax.broadcasted_iota(jnp.int32, (tb, LANES), 1)

    def spread(cols, dtype):
        out = jnp.zeros((tb, LANES), dtype)
        for kx, col in enumerate(cols):
            out = jnp.where(lane_o == kx, col.astype(dtype), out)
        return out

    idx_ref[...] = spread(idxs, jnp.int32)
    gate_ref[...] = spread([e / den for e in es], F32)
    rank_ref[...] = spread(ranks, jnp.int32)


def _route(logits):
    n = logits.shape[0]
    tb = ROUTE_ROWS
    wide = pl.BlockSpec((tb, LANES), lambda i: (i, 0))
    return pl.pallas_call(
        _route_body,
        grid=(n // tb,),
        in_specs=[pl.BlockSpec((tb, N_EXPERTS), lambda i: (i, 0))],
        out_specs=[wide, wide, wide, pl.BlockSpec((1, N_EXPERTS), lambda i: (0, 0))],
        out_shape=[jax.ShapeDtypeStruct((n, LANES), jnp.int32),
                   jax.ShapeDtypeStruct((n, LANES), F32),
                   jax.ShapeDtypeStruct((n, LANES), jnp.int32),
                   jax.ShapeDtypeStruct((1, N_EXPERTS), jnp.int32)],
        scratch_shapes=[pltpu.VMEM((1, N_EXPERTS), F32)],
        compiler_params=_params(("arbitrary",)),
        name="route",
    )(logits)


def _sc_mesh():
    return plsc.VectorSubcoreMesh(core_axis_name="c", subcore_axis_name="s")


def _sc_worker():
    return lax.axis_index("s") * SC_CORES + lax.axis_index("c")


def _sc_dispatch(x_p, x_s, dest_p, dest_s, n_slots):
    chunk = SC_DISPATCH_ROWS

    def per_worker(x, dest):
        per_w = x.shape[0] // SC_WORKERS
        nch = per_w // chunk
        assert per_w * SC_WORKERS == x.shape[0] and nch * chunk == per_w
        return per_w, nch, dest.reshape(SC_WORKERS, nch, chunk, TOP_K).transpose(0, 1, 3, 2)

    pw_p, nch_p, d_p = per_worker(x_p, dest_p)
    pw_s, nch_s, d_s = per_worker(x_s, dest_s)
    width, dtype = x_p.shape[1], x_p.dtype

    @functools.partial(
        pl.kernel, mesh=_sc_mesh(),
        out_type=jax.ShapeDtypeStruct((n_slots, width), dtype),
        scratch_types=[pltpu.VMEM((nch_p, TOP_K, chunk), jnp.int32),
                       pltpu.VMEM((nch_s, TOP_K, chunk), jnp.int32),
                       pltpu.VMEM((chunk, width), dtype),
                       pltpu.SemaphoreType.DMA],
        compiler_params=pltpu.CompilerParams(use_tc_tiling_on_sc=True),
        name="dispatch")
    def run(xp_hbm, xs_hbm, dp_hbm, ds_hbm, out_hbm, ip_v, is_v, rows_v, sem):
        wid = _sc_worker()
        pltpu.sync_copy(dp_hbm.at[wid], ip_v)
        pltpu.sync_copy(ds_hbm.at[wid], is_v)

        def group(x_hbm, idx_v, per_w, nch):
            def body(j, carry):
                pltpu.sync_copy(x_hbm.at[pl.ds(wid * per_w + j * chunk, chunk)], rows_v)
                copies = [pltpu.async_copy(rows_v, out_hbm.at[idx_v.at[j, kx]], sem) for kx in range(TOP_K)]
                for cp in copies:
                    cp.wait()
                return carry

            lax.fori_loop(0, nch, body, 0)

        group(xp_hbm, ip_v, pw_p, nch_p)
        group(xs_hbm, is_v, pw_s, nch_s)

    return run(x_p, x_s, d_p, d_s)


def _sc_collect(y_sorted, dest_p, dest_s):
    chunk = SC_COLLECT_ROWS

    def per_worker(dest):
        rows = dest.shape[0] * TOP_K
        per_w = rows // SC_WORKERS
        nch = per_w // chunk
        assert per_w * SC_WORKERS == rows and nch * chunk == per_w
        return rows, per_w, nch, dest.T.reshape(SC_WORKERS, nch, chunk)

    rows_p, pw_p, nch_p, d_p = per_worker(dest_p)
    rows_s, pw_s, nch_s, d_s = per_worker(dest_s)

    @functools.partial(
        pl.kernel, mesh=_sc_mesh(),
        out_type=[jax.ShapeDtypeStruct((rows_p, D_MODEL), F32), jax.ShapeDtypeStruct((rows_s, D_MODEL), F32)],
        scratch_types=[pltpu.VMEM((nch_p, chunk), jnp.int32),
                       pltpu.VMEM((nch_s, chunk), jnp.int32),
                       pltpu.VMEM((chunk, D_MODEL), F32),
                       pltpu.SemaphoreType.DMA],
        compiler_params=pltpu.CompilerParams(use_tc_tiling_on_sc=True),
        name="collect")
    def run(y_hbm, dp_hbm, ds_hbm, op_hbm, os_hbm, ip_v, is_v, rows_v, sem):
        wid = _sc_worker()
        pltpu.sync_copy(dp_hbm.at[wid], ip_v)
        pltpu.sync_copy(ds_hbm.at[wid], is_v)

        def group(idx_v, o_hbm, per_w, nch):
            def body(j, carry):
                pltpu.async_copy(y_hbm.at[idx_v.at[j]], rows_v, sem).wait()
                pltpu.sync_copy(rows_v, o_hbm.at[pl.ds(wid * per_w + j * chunk, chunk)])
                return carry

            lax.fori_loop(0, nch, body, 0)

        group(ip_v, op_hbm, pw_p, nch_p)
        group(is_v, os_hbm, pw_s, nch_s)

    return run(y_sorted, d_p, d_s)


def _expert_body(blk0_ref, nblk_ref, cnt_ref, wup_ref, wdn_ref, bup_ref, bdn_ref, x_hbm, y_hbm,
                 wup_s, wdn_s, xbuf, obuf, in_sem, out_sem):
    e = pl.program_id(0)
    nb = nblk_ref[e]
    blk0 = blk0_ref[e]
    cnt = cnt_ref[e]
    tm = EXPERT_ROWS
    pair = 2 * LANES

    def x_copy(i, slot):
        rows = pl.ds(pl.multiple_of((blk0 + i) * tm, tm), tm)
        return pltpu.make_async_copy(x_hbm.at[rows], xbuf.at[slot], in_sem.at[slot])

    def y_copy(i, slot):
        rows = pl.ds(pl.multiple_of((blk0 + i) * tm, tm), tm)
        return pltpu.make_async_copy(obuf.at[slot], y_hbm.at[rows], out_sem.at[slot])

    @pl.when(nb > 0)
    def _():
        x_copy(0, 0).start()
        r = lax.broadcasted_iota(jnp.int32, (pair, pair), 0)
        c = lax.broadcasted_iota(jnp.int32, (pair, pair), 1)
        perm = (r == jnp.where(c < LANES, 2 * c, 2 * (c - LANES) + 1)).astype(BF16)
        for g in range(2 * D_MODEL // pair):
            cols = slice(g * pair, (g + 1) * pair)
            wup_s[g] = jnp.dot(wup_ref[0, :, cols].astype(BF16), perm, preferred_element_type=F32).astype(BF16)
        for g in range(D_MODEL // pair):
            wdn_s[g] = wdn_ref[0, :, g * pair:(g + 1) * pair].astype(BF16)

        def block(i, carry):
            slot = i % 2
            x_copy(i, slot).wait()

            @pl.when(i + 1 < nb)
            def _():
                x_copy(i + 1, 1 - slot).start()

            @pl.when(i >= 2)
            def _():
                y_copy(i - 2, slot).wait()

            words = xbuf[slot]
            lo = lax.bitcast_convert_type(words << 16, F32)
            hi = lax.bitcast_convert_type(words & jnp.uint32(0xFFFF0000), F32)
            row = lax.broadcasted_iota(jnp.int32, (tm, 1), 0)
            x = jnp.where(row < cnt - i * tm, jnp.concatenate([lo, hi], axis=1), 0.0).astype(BF16)
            acts = []
            for g in range(2 * D_MODEL // pair):
                cols = slice(g * pair, (g + 1) * pair)
                h = jnp.dot(x, wup_s[g], preferred_element_type=F32) + bup_ref[0, :, cols]
                glu = jnp.minimum(h[:, :LANES], SWIGLU_LIMIT)
                lin = jnp.clip(h[:, LANES:], -SWIGLU_LIMIT, SWIGLU_LIMIT)
                acts.append((glu * jax.nn.sigmoid(SWIGLU_ALPHA * glu) * (lin + 1.0)).astype(BF16))
            act = jnp.concatenate(acts, axis=1)
            for g in range(D_MODEL // pair):
                cols = slice(g * pair, (g + 1) * pair)
                obuf[slot, :, cols] = jnp.dot(act, wdn_s[g], preferred_element_type=F32) + bdn_ref[0, :, cols]
            y_copy(i, slot).start()
            return carry

        lax.fori_loop(0, nb, block, 0)

        @pl.when(nb >= 2)
        def _():
            y_copy(nb - 2, nb % 2).wait()

        y_copy(nb - 1, (nb - 1) % 2).wait()


def _experts(blk0, nblk, cnt, x_sorted, w_up, w_down, b_up_grouped, b_down):
    tm = EXPERT_ROWS
    per_expert = lambda a: pl.BlockSpec((1,) + a.shape[1:], lambda e, b0, nb, ct: (e, 0, 0))
    grid_spec = pltpu.PrefetchScalarGridSpec(
        num_scalar_prefetch=3,
        grid=(N_EXPERTS,),
        in_specs=[per_expert(w_up), per_expert(w_down), per_expert(b_up_grouped), per_expert(b_down),
                  pl.BlockSpec(memory_space=pl.ANY)],
        out_specs=pl.BlockSpec(memory_space=pl.ANY),
        scratch_shapes=[pltpu.VMEM((2 * D_MODEL // (2 * LANES), D_MODEL, 2 * LANES), BF16),
                        pltpu.VMEM((D_MODEL // (2 * LANES), D_MODEL, 2 * LANES), BF16),
                        pltpu.VMEM((2, tm, x_sorted.shape[1]), x_sorted.dtype), pltpu.VMEM((2, tm, D_MODEL), F32),
                        pltpu.SemaphoreType.DMA((2,)), pltpu.SemaphoreType.DMA((2,))],
    )
    return pl.pallas_call(
        _expert_body,
        grid_spec=grid_spec,
        out_shape=jax.ShapeDtypeStruct((x_sorted.shape[0], D_MODEL), F32),
        compiler_params=_params(("arbitrary",)),
        name="experts",
    )(blk0, nblk, cnt, w_up, w_down, b_up_grouped, b_down, x_sorted)


def _combine_body(gate_ref, xmid_ref, gfin_ref, y0_ref, y1_ref, y2_ref, y3_ref, out_ref):
    gate = gate_ref[...]
    moe = y0_ref[...] * gate[:, 0:1]
    for kx, y_ref in enumerate((y1_ref, y2_ref, y3_ref), start=1):
        moe = moe + y_ref[...] * gate[:, kx:kx + 1]
    out_ref[...] = _rms(xmid_ref[...] + moe, gfin_ref[...])


def _combine(gates, xmid, gfin, y_rows):
    n = xmid.shape[0]
    tt = COMBINE_ROWS
    nblk = n // tt
    choice = lambda kx: pl.BlockSpec((tt, D_MODEL), lambda i: (i + kx * nblk, 0))
    return pl.pallas_call(
        _combine_body,
        grid=(nblk,),
        in_specs=[pl.BlockSpec((tt, LANES), lambda i: (i, 0)),
                  pl.BlockSpec((tt, D_MODEL), lambda i: (i, 0)),
                  pl.BlockSpec((1, D_MODEL), lambda i: (0, 0))] + [choice(kx) for kx in range(TOP_K)],
        out_specs=pl.BlockSpec((tt, D_MODEL), lambda i: (i, 0)),
        out_shape=jax.ShapeDtypeStruct((n, D_MODEL), F32),
        compiler_params=_params(("arbitrary",)),
        name="combine",
    )(gates, xmid, gfin, y_rows, y_rows, y_rows, y_rows)


def kernel(x_prompt, x_sample, cache_k_win, cache_v_win, norm_attn_g, w_in, ln_v_g, ln_v_b, w_spatial, b_spatial,
           attn_sinks, w_out, norm_ffn_g, w_router, b_router, w_up, b_up, w_down, b_down, norm_final_g):
    bp, tp, _ = x_prompt.shape
    bs, ts, _ = x_sample.shape
    w_buf = cache_k_win.shape[2]
    assert bp == 1 and tp % MIX_ROWS == 0 and w_buf == WINDOW and (bs * ts) % PROJ_ROWS == 0
    n_p, n_s = bp * tp, bs * ts
    row2 = lambda a: a.reshape(1, -1)

    w_in_bf = w_in[0].astype(BF16)
    w_out_bf = w_out[0].astype(BF16)
    tril = jnp.tril(jnp.ones((CHUNK, CHUNK), dtype=bool))
    wsp = jnp.where(tril[None], w_spatial[0], 0.0)
    wsp_bf = wsp.astype(BF16)
    bsp = jnp.broadcast_to(b_spatial[0][:, :, None], (GMLP_GROUPS, CHUNK, LANES))
    b_up_grouped = b_up[0].reshape(N_EXPERTS, -1, LANES, 2).transpose(0, 1, 3, 2).reshape(N_EXPERTS, 1, -1)
    bd = b_down[0][:, None, :]
    g1, g2, gfin = row2(norm_attn_g[0]), row2(norm_ffn_g[0]), row2(norm_final_g)
    lng, lnb = row2(ln_v_g[0]), row2(ln_v_b[0])
    wr, br = w_router[0], row2(b_router[0])
    sinks = attn_sinks[0]

    xp = x_prompt.reshape(n_p, D_MODEL)
    rc, rs1, rs2 = _rotary_tables(jnp.arange(tp, dtype=jnp.int32))
    q_p, k_p, v_p, a_p, vn_p, sgb_p = _proj(xp, g1, w_in_bf, rc, rs1, rs2, lng, lnb)
    xmid_p, xn2_p, lg_p = _mix(sinks, q_p, k_p, v_p, a_p, vn_p, sgb_p, xp, wsp_bf, bsp, w_out_bf, g2, wr, br)

    xs = x_sample.reshape(n_s, D_MODEL)
    pos_s = PAST_LEN + jnp.arange(ts, dtype=jnp.int32)
    rc, rs1, rs2 = _rotary_tables(jnp.tile(pos_s, bs))
    q_s, k_s, v_s, a_s, vn_s, sgb_s = _proj(xs, g1, w_in_bf, rc, rs1, rs2, lng, lnb)
    k_all = jnp.concatenate([cache_k_win[0], k_s.reshape(bs, ts, N_KV_HEADS, HEAD_DIM)], axis=1)
    v_all = jnp.concatenate([cache_v_win[0], v_s.reshape(bs, ts, N_KV_HEADS, HEAD_DIM)], axis=1)
    n_keys = w_buf + ts
    key_pad = (-n_keys) % 8
    to_heads = lambda t: jnp.pad(t, ((0, 0), (0, key_pad), (0, 0), (0, 0))).transpose(2, 0, 1, 3).astype(BF16)
    qh = q_s.reshape(bs, ts, N_KV_HEADS, GQA_GROUP, HEAD_DIM).transpose(2, 0, 1, 3, 4)
    qh = qh.reshape(N_KV_HEADS, bs, ts * GQA_GROUP, HEAD_DIM)
    sink_rows = jnp.tile(sinks.reshape(N_KV_HEADS, 1, GQA_GROUP), (1, ts, 1)).reshape(N_KV_HEADS, ts * GQA_GROUP, 1)
    oh = _sample_attn(sink_rows, qh, to_heads(k_all), to_heads(v_all))
    o_s = oh.reshape(N_KV_HEADS, bs, ts, GQA_GROUP, HEAD_DIM).transpose(1, 2, 0, 3, 4).reshape(n_s, Q_WIDTH)
    t_idx = jnp.arange(ts)
    coef = jnp.stack([jnp.where((t_idx >= d)[None, :], wsp[:, t_idx, jnp.maximum(t_idx - d, 0)], 0.0)
                      for d in range(ts)])
    coef = jnp.repeat(coef.transpose(0, 2, 1), GMLP_WIDTH // GMLP_GROUPS, axis=2)
    coef = jnp.tile(coef, (1, bs, 1))
    bias = jnp.tile(jnp.repeat(b_spatial[0][:, :ts].T, GMLP_WIDTH // GMLP_GROUPS, axis=1), (bs, 1))
    xmid_s, xn2_s, lg_s = _mix_sample(a_s, vn_s, sgb_s, o_s, xs, coef, bias, w_out_bf, g2, wr, br)

    n_tok = n_p + n_s
    idx_w, gate_w, rank_w, counts = _route(jnp.concatenate([lg_p, lg_s], axis=0))
    idx, rank = idx_w[:, :TOP_K], rank_w[:, :TOP_K]
    counts = counts[0]
    tm = EXPERT_ROWS
    experts = jnp.arange(N_EXPERTS, dtype=jnp.int32)
    padded = (counts + tm - 1) // tm * tm
    pend = jnp.cumsum(padded)
    pstart = pend - padded
    dest = jnp.sum(jnp.where(idx[:, :, None] == experts, pstart, 0), axis=2) + rank
    n_blocks = (n_tok * TOP_K) // tm + N_EXPERTS

    x_sorted = _sc_dispatch(xn2_p, xn2_s, dest[:n_p], dest[n_p:], n_blocks * tm)
    y_sorted = _experts(pstart // tm, padded // tm, counts, x_sorted, w_up[0], w_down[0], b_up_grouped, bd)
    yrows_p, yrows_s = _sc_collect(y_sorted, dest[:n_p], dest[n_p:])
    y_p = _combine(gate_w[:n_p], xmid_p, gfin, yrows_p)
    y_s = _combine(gate_w[n_p:], xmid_s, gfin, yrows_s)

    k4 = lambda t: t.reshape(1, bp, -1, N_KV_HEADS, HEAD_DIM)
    return (y_p.reshape(bp, tp, D_MODEL),
            y_s.reshape(bs, ts, D_MODEL),
            k4(k_p[n_p - WINDOW:]),
            k4(v_p[n_p - WINDOW:]),
            vn_p[n_p - CHUNK:].reshape(1, bp, CHUNK, GMLP_WIDTH),
            k_all[None, :, ts:],
            v_all[None, :, ts:],
            vn_s.reshape(1, bs, ts, GMLP_WIDTH))
```

```python
import functools

import numpy as np
import jax
import jax.numpy as jnp
from jax import lax
from jax.experimental import pallas as pl
from jax.experimental.pallas import tpu as pltpu
from jax.experimental.pallas import tpu_sc as plsc

F32 = jnp.float32
BF16 = jnp.bfloat16

D_MODEL = 1024
HEAD_DIM = 64
N_HEADS = 16
GQA_GROUP = 8
N_KV_HEADS = 2
Q_WIDTH = 1024
KV_WIDTH = 128
WINDOW = 128
ROT_DIM = 16
ROPE_THETA = 500000.0
CHUNK = 128
GMLP_WIDTH = 1024
GMLP_GROUPS = 8
N_EXPERTS = 32
TOP_K = 4
SWIGLU_LIMIT = 7.0
SWIGLU_ALPHA = 1.702
RMS_EPS = 1e-5
LN_EPS = 1e-5
NEG_INF = -1e30
PAST_LEN = 16384

LANES = 128
VMEM_LIMIT = 56 * 1024 * 1024

PROJ_ROWS = 256
MIX_ROWS = 512
ROUTE_ROWS = 512
EXPERT_ROWS = 256
COMBINE_ROWS = 256

SC_CORES = 2
SC_WORKERS = 32
SC_DISPATCH_ROWS = 16
SC_COLLECT_ROWS = 32

_C_Q, _C_KV, _C_U, _C_VG, _C_GA, _C_GB, _C_END = 0, 1024, 1280, 2304, 3328, 4352, 5376


def _params(sem):
    return pltpu.CompilerParams(dimension_semantics=sem, vmem_limit_bytes=VMEM_LIMIT)


def _rms(x, g):
    return x * lax.rsqrt(jnp.mean(x * x, axis=-1, keepdims=True) + RMS_EPS) * g


def _proj_body(x_ref, g_ref, w_ref, cs_ref, rot_ref, lng_ref, lnb_ref,
               q_ref, k_ref, v_ref, a_ref, vn_ref, sgb_ref):
    h = _rms(x_ref[...], g_ref[...]).astype(BF16)
    tabs = jnp.dot(cs_ref[...], rot_ref[...], preferred_element_type=F32, precision=lax.Precision.HIGHEST)
    rc, rs1, rs2 = tabs[:, :LANES], tabs[:, LANES:2 * LANES], tabs[:, 2 * LANES:]

    def rot(z):
        return z * rc + pltpu.roll(z, LANES - ROT_DIM // 2, 1) * rs1 + pltpu.roll(z, ROT_DIM // 2, 1) * rs2

    def mm(lo, hi):
        return jnp.dot(h, w_ref[:, lo:hi], preferred_element_type=F32)

    zq = mm(_C_Q, _C_KV)
    for c in range(Q_WIDTH // LANES):
        sl = slice(c * LANES, (c + 1) * LANES)
        q_ref[:, sl] = (rot(zq[:, sl]) * (HEAD_DIM ** -0.5)).astype(BF16)
    zkv = mm(_C_KV, _C_U)
    k_ref[...] = rot(zkv[:, :KV_WIDTH])
    v_ref[...] = zkv[:, KV_WIDTH:]
    a_ref[...] = jax.nn.sigmoid(mm(_C_GA, _C_GB)) * jax.nn.gelu(mm(_C_U, _C_VG))
    zv = jax.nn.gelu(mm(_C_VG, _C_GA))
    zc = zv - jnp.mean(zv, axis=-1, keepdims=True)
    var = jnp.mean(zc * zc, axis=-1, keepdims=True)
    vn_ref[...] = zc * lax.rsqrt(var + LN_EPS) * lng_ref[...] + lnb_ref[...]
    sgb_ref[...] = jax.nn.sigmoid(mm(_C_GB, _C_END))


def _proj(x, norm_g, w_in_bf, cs, ln_g, ln_b):
    n = x.shape[0]
    tm = PROJ_ROWS
    row = lambda w: pl.BlockSpec((tm, w), lambda i: (i, 0))
    full = lambda a: pl.BlockSpec(a.shape, lambda i: (0,) * a.ndim)
    rot = jnp.asarray(_ROT_EXPAND)
    return pl.pallas_call(
        _proj_body,
        grid=(n // tm,),
        in_specs=[row(D_MODEL), full(norm_g), full(w_in_bf), row(cs.shape[1]), full(rot),
                  full(ln_g), full(ln_b)],
        out_specs=[row(Q_WIDTH), row(KV_WIDTH), row(KV_WIDTH), row(GMLP_WIDTH), row(GMLP_WIDTH), row(D_MODEL)],
        out_shape=[jax.ShapeDtypeStruct((n, Q_WIDTH), BF16),
                   jax.ShapeDtypeStruct((n, KV_WIDTH), F32),
                   jax.ShapeDtypeStruct((n, KV_WIDTH), F32),
                   jax.ShapeDtypeStruct((n, GMLP_WIDTH), F32),
                   jax.ShapeDtypeStruct((n, GMLP_WIDTH), F32),
                   jax.ShapeDtypeStruct((n, D_MODEL), F32)],
        compiler_params=_params(("arbitrary",)),
        name="proj",
    )(x, norm_g, w_in_bf, cs, rot, ln_g, ln_b)


_ROT_COLS = 32


def _rot_expand():
    half = ROT_DIM // 2
    m = np.zeros((_ROT_COLS, 3 * LANES), np.float32)
    for lane in range(LANES):
        d = lane % HEAD_DIM
        if d < ROT_DIM:
            m[d % half, lane] = 1.0
        else:
            m[2 * half, lane] = 1.0
        if d < half:
            m[half + d, LANES + lane] = -1.0
        elif d < ROT_DIM:
            m[half + d - half, 2 * LANES + lane] = 1.0
    return m


_ROT_EXPAND = _rot_expand()


def _rotary_inputs(pos):
    half = ROT_DIM // 2
    inv_freq = ROPE_THETA ** (-jnp.arange(half, dtype=F32) / half)
    ang = pos.astype(F32)[:, None] * inv_freq[None, :]
    n = pos.shape[0]
    return jnp.concatenate([jnp.cos(ang), jnp.sin(ang), jnp.ones((n, 1), F32),
                            jnp.zeros((n, _ROT_COLS - 2 * half - 1), F32)], axis=1)


def _finish_rows(merged_bf, x, wout_ref, g2_ref, wr_ref, br_ref, xmid_ref, xn2_ref, lg_ref):
    xm = x + jnp.dot(merged_bf, wout_ref[...], preferred_element_type=F32)
    xmid_ref[...] = xm
    xn = _rms(xm, g2_ref[...])
    x_hi = xn.astype(BF16)
    x_lo = (xn - x_hi.astype(F32)).astype(BF16)
    w_hl = wr_ref[...]
    p_hi = jnp.dot(x_hi, w_hl, preferred_element_type=F32)
    p_lo = jnp.dot(x_lo, w_hl[:, :N_EXPERTS], preferred_element_type=F32)
    lg_ref[...] = p_hi[:, :N_EXPERTS] + (p_hi[:, N_EXPERTS:] + p_lo) + br_ref[...]
    bits = lax.bitcast_convert_type(xn.astype(BF16).astype(F32), jnp.uint32)
    xn2_ref[...] = (bits[:, :D_MODEL // 2] >> 16) | bits[:, D_MODEL // 2:]


def _mix_body(sinks_ref, q_ref, k_ref, kp_ref, v_ref, vp_ref, a_ref, vn_ref, sgb_ref, x_ref,
              wsp_ref, bsp_ref, wout_ref, g2_ref, wr_ref, br_ref,
              xmid_ref, xn2_ref, lg_ref, kcat, vcat, mrg):
    i = pl.program_id(0)
    nsub = MIX_ROWS // WINDOW
    kcat[0:WINDOW] = kp_ref[...]
    kcat[WINDOW:] = k_ref[...]
    vcat[0:WINDOW] = vp_ref[...]
    vcat[WINDOW:] = v_ref[...]

    pair_rows = (GQA_GROUP // 2) * WINDOW
    rq = lax.broadcasted_iota(jnp.int32, (pair_rows, 4 * WINDOW), 0) & (WINDOW - 1)
    ck = lax.broadcasted_iota(jnp.int32, (pair_rows, 4 * WINDOW), 1) & (2 * WINDOW - 1)
    band = (ck > rq) & (ck <= rq + WINDOW)
    lane_kv = lax.broadcasted_iota(jnp.int32, (2 * WINDOW, LANES), 1)
    lane_o = lax.broadcasted_iota(jnp.int32, (pair_rows, LANES), 1)
    row_p = lax.broadcasted_iota(jnp.int32, (pair_rows, 1), 0) >> 7

    def sub(j, carry):
        off = pl.multiple_of(j * WINDOW, WINDOW)
        rows = pl.ds(off, WINDOW)
        for g in range(GMLP_GROUPS):
            cols = slice(g * LANES, (g + 1) * LANES)
            s = jnp.dot(wsp_ref[g], vn_ref[rows, cols].astype(BF16), preferred_element_type=F32) + bsp_ref[g]
            mrg[rows, cols] = a_ref[rows, cols] * s
        kblk = kcat[pl.ds(off, 2 * WINDOW), :]
        vblk = vcat[pl.ds(off, 2 * WINDOW), :]
        kswp = pltpu.roll(kblk, HEAD_DIM, 1)
        vswp = pltpu.roll(vblk, HEAD_DIM, 1)
        kmin = jnp.where(jnp.logical_and(i == 0, j == 0), WINDOW, 0)
        allowed = band & (ck >= kmin)
        for kk in range(N_KV_HEADS):
            lo_src, hi_src = (kblk, kswp) if kk == 0 else (kswp, kblk)
            kbd = jnp.concatenate([jnp.where(lane_kv < HEAD_DIM, lo_src, 0.0),
                                   jnp.where(lane_kv >= HEAD_DIM, hi_src, 0.0)], axis=0).astype(BF16)
            lo_src, hi_src = (vblk, vswp) if kk == 0 else (vswp, vblk)
            vbd = jnp.concatenate([jnp.where(lane_kv < HEAD_DIM, lo_src, 0.0),
                                   jnp.where(lane_kv >= HEAD_DIM, hi_src, 0.0)], axis=0).astype(BF16)
            pair0 = kk * (GQA_GROUP // 2)
            qs = jnp.concatenate([q_ref[rows, (pair0 + p) * LANES:(pair0 + p + 1) * LANES]
                                  for p in range(GQA_GROUP // 2)], axis=0)
            lg = lax.dot_general(qs, kbd, (((1,), (1,)), ((), ())), preferred_element_type=F32)
            lg = jnp.where(allowed, lg, NEG_INF)
            h0 = kk * GQA_GROUP
            se = jnp.full((pair_rows, 1), sinks_ref[h0], F32)
            so = jnp.full((pair_rows, 1), sinks_ref[h0 + 1], F32)
            for p in range(1, GQA_GROUP // 2):
                se = jnp.where(row_p == p, sinks_ref[h0 + 2 * p], se)
                so = jnp.where(row_p == p, sinks_ref[h0 + 2 * p + 1], so)
            le, lo = lg[:, :2 * WINDOW], lg[:, 2 * WINDOW:]
            me = jnp.maximum(jnp.max(le, axis=1, keepdims=True), se)
            mo = jnp.maximum(jnp.max(lo, axis=1, keepdims=True), so)
            pe = jnp.exp(le - me)
            po = jnp.exp(lo - mo)
            de = jnp.sum(pe, axis=1, keepdims=True) + jnp.exp(se - me)
            do = jnp.sum(po, axis=1, keepdims=True) + jnp.exp(so - mo)
            pr = jnp.concatenate([pe, po], axis=1).astype(BF16)
            o = jnp.dot(pr, vbd, preferred_element_type=F32)
            o = o / jnp.where(lane_o < HEAD_DIM, de, do)
            for p in range(GQA_GROUP // 2):
                cols = slice((pair0 + p) * LANES, (pair0 + p + 1) * LANES)
                mrg[rows, cols] += sgb_ref[rows, cols] * o[p * WINDOW:(p + 1) * WINDOW]
        return carry

    lax.fori_loop(0, nsub, sub, 0)
    _finish_rows(mrg[...].astype(BF16), x_ref[...], wout_ref, g2_ref, wr_ref, br_ref, xmid_ref, xn2_ref, lg_ref)


def _mix(sinks, q, k, v, a, vn, sgb, x, wsp, bsp, wout, g2, wr, br):
    n = x.shape[0]
    tm = MIX_ROWS
    nsub = tm // WINDOW
    row = lambda w: pl.BlockSpec((tm, w), lambda i: (i, 0))
    prev = pl.BlockSpec((WINDOW, KV_WIDTH), lambda i: (jnp.maximum(i * nsub - 1, 0), 0))
    full = lambda arr: pl.BlockSpec(arr.shape, lambda i: (0,) * arr.ndim)
    smem = pl.BlockSpec(memory_space=pltpu.SMEM)
    return pl.pallas_call(
        _mix_body,
        grid=(n // tm,),
        in_specs=[smem, row(Q_WIDTH), row(KV_WIDTH), prev, row(KV_WIDTH), prev,
                  row(GMLP_WIDTH), row(GMLP_WIDTH), row(D_MODEL), row(D_MODEL),
                  full(wsp), full(bsp), full(wout), full(g2), full(wr), full(br)],
        out_specs=[row(D_MODEL), row(D_MODEL // 2), row(N_EXPERTS)],
        out_shape=[jax.ShapeDtypeStruct((n, D_MODEL), F32),
                   jax.ShapeDtypeStruct((n, D_MODEL // 2), jnp.uint32),
                   jax.ShapeDtypeStruct((n, N_EXPERTS), F32)],
        scratch_shapes=[pltpu.VMEM((tm + WINDOW, KV_WIDTH), F32),
                        pltpu.VMEM((tm + WINDOW, KV_WIDTH), F32),
                        pltpu.VMEM((tm, D_MODEL), F32)],
        compiler_params=_params(("arbitrary",)),
        name="mix_prompt",
    )(sinks, q, k, k, v, v, a, vn, sgb, x, wsp, bsp, wout, g2, wr, br)


def _sample_attn_body(sink_ref, q_ref, k_ref, v_ref, o_ref):
    q = q_ref[0]
    k = k_ref[0]
    v = v_ref[0]
    nq, nk = q.shape[1], k.shape[1]
    lg = jnp.einsum("bqd,bkd->bqk", q, k, preferred_element_type=F32)
    t = lax.broadcasted_iota(jnp.int32, (1, nq, nk), 1) // GQA_GROUP
    j = lax.broadcasted_iota(jnp.int32, (1, nq, nk), 2)
    lg = jnp.where((j > t) & (j <= t + WINDOW), lg, NEG_INF)
    sink = sink_ref[0][None]
    m = jnp.maximum(jnp.max(lg, axis=2, keepdims=True), sink)
    p = jnp.exp(lg - m)
    den = jnp.sum(p, axis=2, keepdims=True) + jnp.exp(sink - m)
    o = jnp.einsum("bqk,bkd->bqd", p.astype(BF16), v, preferred_element_type=F32)
    o_ref[0] = o / den


def _sample_attn(sink_rows, qh, kh, vh):
    nb = qh.shape[1]
    bb = 32
    blk = lambda a: pl.BlockSpec((1, bb) + a.shape[2:], lambda kk, b: (kk, b, 0, 0))
    return pl.pallas_call(
        _sample_attn_body,
        grid=(N_KV_HEADS, nb // bb),
        in_specs=[pl.BlockSpec((1,) + sink_rows.shape[1:], lambda kk, b: (kk, 0, 0)), blk(qh), blk(kh), blk(vh)],
        out_specs=pl.BlockSpec((1, bb) + qh.shape[2:], lambda kk, b: (kk, b, 0, 0)),
        out_shape=jax.ShapeDtypeStruct(qh.shape, F32),
        compiler_params=_params(("arbitrary", "arbitrary")),
        name="attn_sample",
    )(sink_rows, qh, kh, vh)


def _mix_sample_body(a_ref, vn_ref, sgb_ref, o_ref, x_ref, coef_ref, bias_ref,
                     wout_ref, g2_ref, wr_ref, br_ref, xmid_ref, xn2_ref, lg_ref):
    vn = vn_ref[...]
    n, width = vn.shape
    rows8 = lambda t: t.reshape(n // 8, 8, width)
    s = bias_ref[...][None] + coef_ref[0][None] * rows8(vn)
    for d in range(1, coef_ref.shape[0]):
        s = s + coef_ref[d][None] * rows8(pltpu.roll(vn, d, 0))
    merged = a_ref[...] * s.reshape(n, width) + sgb_ref[...] * o_ref[...]
    _finish_rows(merged.astype(BF16), x_ref[...], wout_ref, g2_ref, wr_ref, br_ref, xmid_ref, xn2_ref, lg_ref)


def _mix_sample(a, vn, sgb, o, x, coef, bias, wout, g2, wr, br):
    n = x.shape[0]
    args = (a, vn, sgb, o, x, coef, bias, wout, g2, wr, br)
    full = lambda arr: pl.BlockSpec(arr.shape, lambda i: (0,) * arr.ndim)
    return pl.pallas_call(
        _mix_sample_body,
        grid=(1,),
        in_specs=[full(arr) for arr in args],
        out_specs=[pl.BlockSpec((n, D_MODEL), lambda i: (0, 0)), pl.BlockSpec((n, D_MODEL // 2), lambda i: (0, 0)),
                   pl.BlockSpec((n, N_EXPERTS), lambda i: (0, 0))],
        out_shape=[jax.ShapeDtypeStruct((n, D_MODEL), F32),
                   jax.ShapeDtypeStruct((n, D_MODEL // 2), jnp.uint32),
                   jax.ShapeDtypeStruct((n, N_EXPERTS), F32)],
        compiler_params=_params(("arbitrary",)),
        name="mix_sample",
    )(*args)


def _route_body(lg_ref, idx_ref, gate_ref, rank_ref, cnt_ref, base):
    i = pl.program_id(0)

    @pl.when(i == 0)
    def _():
        base[...] = jnp.zeros_like(base)

    l = lg_ref[...]
    tb = l.shape[0]
    lane = lax.broadcasted_iota(jnp.int32, l.shape, 1).astype(F32)
    vals, idxs, sels = [], [], []
    for _ in range(TOP_K):
        m = jnp.max(l, axis=1, keepdims=True)
        ik = jnp.min(jnp.where(l == m, lane, float(N_EXPERTS)), axis=1, keepdims=True)
        sel = lane == ik
        l = jnp.where(sel, -jnp.inf, l)
        vals.append(m)
        idxs.append(ik)
        sels.append(sel)
    es = [jnp.exp(vk - vals[0]) for vk in vals]
    den = es[0] + es[1] + es[2] + es[3]
    onehot = jnp.zeros(l.shape, F32)
    for sel in sels:
        onehot = onehot + sel.astype(F32)
    tri = (lax.broadcasted_iota(jnp.int32, (tb, tb), 0) > lax.broadcasted_iota(jnp.int32, (tb, tb), 1))
    before = jnp.dot(tri.astype(BF16), onehot.astype(BF16), preferred_element_type=F32) + base[...]
    ranks = [jnp.sum(jnp.where(sel, before, 0.0), axis=1, keepdims=True) for sel in sels]
    base[...] += jnp.sum(onehot, axis=0, keepdims=True)
    cnt_ref[...] = base[...].astype(jnp.int32)

    lane_o = lax.broadcasted_iota(jnp.int32, (tb, LANES), 1)

    def spread(cols, dtype):
        out = jnp.zeros((tb, LANES), dtype)
        for kx, col in enumerate(cols):
            out = jnp.where(lane_o == kx, col.astype(dtype), out)
        return out

    idx_ref[...] = spread(idxs, jnp.int32)
    gate_ref[...] = spread([e / den for e in es], F32)
    rank_ref[...] = spread(ranks, jnp.int32)


def _route(logits):
    n = logits.shape[0]
    tb = ROUTE_ROWS
    wide = pl.BlockSpec((tb, LANES), lambda i: (i, 0))
    return pl.pallas_call(
        _route_body,
        grid=(n // tb,),
        in_specs=[pl.BlockSpec((tb, N_EXPERTS), lambda i: (i, 0))],
        out_specs=[wide, wide, wide, pl.BlockSpec((1, N_EXPERTS), lambda i: (0, 0))],
        out_shape=[jax.ShapeDtypeStruct((n, LANES), jnp.int32),
                   jax.ShapeDtypeStruct((n, LANES), F32),
                   jax.ShapeDtypeStruct((n, LANES), jnp.int32),
                   jax.ShapeDtypeStruct((1, N_EXPERTS), jnp.int32)],
        scratch_shapes=[pltpu.VMEM((1, N_EXPERTS), F32)],
        compiler_params=_params(("arbitrary",)),
        name="route",
    )(logits)


def _slots_body(idx_ref, rank_ref, pstart_ref, dest_ref):
    idx = idx_ref[...]
    rank = rank_ref[...]
    tb = idx.shape[0]
    lane_e = lax.broadcasted_iota(jnp.int32, (tb, N_EXPERTS), 1)
    lane_o = lax.broadcasted_iota(jnp.int32, (tb, LANES), 1)
    pstart = pstart_ref[...]
    dest = jnp.zeros((tb, LANES), F32)
    for kx in range(TOP_K):
        seg = jnp.sum(jnp.where(lane_e == idx[:, kx:kx + 1], pstart, 0.0), axis=1, keepdims=True)
        dest = jnp.where(lane_o == kx, seg + rank[:, kx:kx + 1].astype(F32), dest)
    dest_ref[...] = dest.T[:dest_ref.shape[0]].astype(jnp.int32)


def _slots(idx_w, rank_w, pstart):
    n = idx_w.shape[0]
    tb = ROUTE_ROWS
    wide = pl.BlockSpec((tb, LANES), lambda i: (i, 0))
    return pl.pallas_call(
        _slots_body,
        grid=(n // tb,),
        in_specs=[wide, wide, pl.BlockSpec((1, N_EXPERTS), lambda i: (0, 0))],
        out_specs=pl.BlockSpec((8, tb), lambda i: (0, i)),
        out_shape=jax.ShapeDtypeStruct((8, n), jnp.int32),
        compiler_params=_params(("arbitrary",)),
        name="slots",
    )(idx_w, rank_w, pstart.astype(F32).reshape(1, N_EXPERTS))


def _sc_mesh():
    return plsc.VectorSubcoreMesh(core_axis_name="c", subcore_axis_name="s")


def _sc_worker():
    return lax.axis_index("s") * SC_CORES + lax.axis_index("c")


def _sc_dispatch(x_p, x_s, dest_t, n_slots):
    chunk = SC_DISPATCH_ROWS
    n_p, n_s = x_p.shape[0], x_s.shape[0]
    nch_p, nch_s = n_p // (SC_WORKERS * chunk), n_s // (SC_WORKERS * chunk)
    assert nch_p * SC_WORKERS * chunk == n_p and nch_s * SC_WORKERS * chunk == n_s
    d3 = dest_t.reshape(dest_t.shape[0], (n_p + n_s) // chunk, chunk)
    width, dtype = x_p.shape[1], x_p.dtype

    @functools.partial(
        pl.kernel, mesh=_sc_mesh(),
        out_type=jax.ShapeDtypeStruct((n_slots, width), dtype),
        scratch_types=[pltpu.VMEM((TOP_K, nch_p, chunk), jnp.int32),
                       pltpu.VMEM((TOP_K, nch_s, chunk), jnp.int32),
                       pltpu.VMEM((chunk, width), dtype),
                       pltpu.SemaphoreType.DMA],
        compiler_params=pltpu.CompilerParams(use_tc_tiling_on_sc=True),
        name="dispatch")
    def run(xp_hbm, xs_hbm, d_hbm, out_hbm, ip_v, is_v, rows_v, sem):
        wid = _sc_worker()
        pltpu.sync_copy(d_hbm.at[pl.ds(0, TOP_K), pl.ds(wid * nch_p, nch_p)], ip_v)
        pltpu.sync_copy(d_hbm.at[pl.ds(0, TOP_K), pl.ds(n_p // chunk + wid * nch_s, nch_s)], is_v)

        def group(x_hbm, idx_v, nch):
            def body(j, carry):
                pltpu.sync_copy(x_hbm.at[pl.ds((wid * nch + j) * chunk, chunk)], rows_v)
                copies = [pltpu.async_copy(rows_v, out_hbm.at[idx_v.at[kx, j]], sem) for kx in range(TOP_K)]
                for cp in copies:
                    cp.wait()
                return carry

            lax.fori_loop(0, nch, body, 0)

        group(xp_hbm, ip_v, nch_p)
        group(xs_hbm, is_v, nch_s)

    return run(x_p, x_s, d3)


def _sc_collect(y_sorted, dest_t, n_p, n_s):
    chunk = SC_COLLECT_ROWS
    per_choice = SC_WORKERS // TOP_K
    nch_p, nch_s = n_p // (per_choice * chunk), n_s // (per_choice * chunk)
    assert nch_p * per_choice * chunk == n_p and nch_s * per_choice * chunk == n_s
    d3 = dest_t.reshape(dest_t.shape[0], (n_p + n_s) // chunk, chunk)

    @functools.partial(
        pl.kernel, mesh=_sc_mesh(),
        out_type=[jax.ShapeDtypeStruct((TOP_K * n_p, D_MODEL), F32), jax.ShapeDtypeStruct((TOP_K * n_s, D_MODEL), F32)],
        scratch_types=[pltpu.VMEM((nch_p, chunk), jnp.int32),
                       pltpu.VMEM((nch_s, chunk), jnp.int32),
                       pltpu.VMEM((chunk, D_MODEL), F32),
                       pltpu.SemaphoreType.DMA],
        compiler_params=pltpu.CompilerParams(use_tc_tiling_on_sc=True),
        name="collect")
    def run(y_hbm, d_hbm, op_hbm, os_hbm, ip_v, is_v, rows_v, sem):
        wid = _sc_worker()
        choice = wid // per_choice
        part = wid % per_choice
        pltpu.sync_copy(d_hbm.at[choice, pl.ds(part * nch_p, nch_p)], ip_v)
        pltpu.sync_copy(d_hbm.at[choice, pl.ds(n_p // chunk + part * nch_s, nch_s)], is_v)

        def group(idx_v, o_hbm, nch):
            def body(j, carry):
                pltpu.async_copy(y_hbm.at[idx_v.at[j]], rows_v, sem).wait()
                pltpu.sync_copy(rows_v, o_hbm.at[pl.ds((wid * nch + j) * chunk, chunk)])
                return carry

            lax.fori_loop(0, nch, body, 0)

        group(ip_v, op_hbm, nch_p)
        group(is_v, os_hbm, nch_s)

    return run(y_sorted, d3)


def _expert_body(blk0_ref, nblk_ref, cnt_ref, wup_ref, wdn_ref, bup_ref, bdn_ref, x_hbm, y_hbm,
                 wup_s, wdn_s, xbuf, obuf, in_sem, out_sem):
    e = pl.program_id(0)
    nb = nblk_ref[e]
    blk0 = blk0_ref[e]
    cnt = cnt_ref[e]
    tm = EXPERT_ROWS
    pair = 2 * LANES

    def x_copy(i, slot):
        rows = pl.ds(pl.multiple_of((blk0 + i) * tm, tm), tm)
        return pltpu.make_async_copy(x_hbm.at[rows], xbuf.at[slot], in_sem.at[slot])

    def y_copy(i, slot):
        rows = pl.ds(pl.multiple_of((blk0 + i) * tm, tm), tm)
        return pltpu.make_async_copy(obuf.at[slot], y_hbm.at[rows], out_sem.at[slot])

    @pl.when(nb > 0)
    def _():
        x_copy(0, 0).start()
        r = lax.broadcasted_iota(jnp.int32, (pair, pair), 0)
        c = lax.broadcasted_iota(jnp.int32, (pair, pair), 1)
        perm = (r == jnp.where(c < LANES, 2 * c, 2 * (c - LANES) + 1)).astype(BF16)
        for g in range(2 * D_MODEL // pair):
            cols = slice(g * pair, (g + 1) * pair)
            wup_s[g] = jnp.dot(wup_ref[0, :, cols].astype(BF16), perm, preferred_element_type=F32).astype(BF16)
        for g in range(D_MODEL // pair):
            wdn_s[g] = wdn_ref[0, :, g * pair:(g + 1) * pair].astype(BF16)

        def block(i, carry):
            slot = i % 2
            x_copy(i, slot).wait()

            @pl.when(i + 1 < nb)
            def _():
                x_copy(i + 1, 1 - slot).start()

            @pl.when(i >= 2)
            def _():
                y_copy(i - 2, slot).wait()

            words = xbuf[slot]
            lo = lax.bitcast_convert_type(words << 16, F32)
            hi = lax.bitcast_convert_type(words & jnp.uint32(0xFFFF0000), F32)
            row = lax.broadcasted_iota(jnp.int32, (tm, 1), 0)
            x = jnp.where(row < cnt - i * tm, jnp.concatenate([lo, hi], axis=1), 0.0).astype(BF16)
            acts = []
            for g in range(2 * D_MODEL // pair):
                cols = slice(g * pair, (g + 1) * pair)
                h = jnp.dot(x, wup_s[g], preferred_element_type=F32) + bup_ref[0, :, cols]
                glu = jnp.minimum(h[:, :LANES], SWIGLU_LIMIT)
                lin = jnp.clip(h[:, LANES:], -SWIGLU_LIMIT, SWIGLU_LIMIT)
                acts.append((glu * jax.nn.sigmoid(SWIGLU_ALPHA * glu) * (lin + 1.0)).astype(BF16))
            act = jnp.concatenate(acts, axis=1)
            for g in range(D_MODEL // pair):
                cols = slice(g * pair, (g + 1) * pair)
                obuf[slot, :, cols] = jnp.dot(act, wdn_s[g], preferred_element_type=F32) + bdn_ref[0, :, cols]
            y_copy(i, slot).start()
            return carry

        lax.fori_loop(0, nb, block, 0)

        @pl.when(nb >= 2)
        def _():
            y_copy(nb - 2, nb % 2).wait()

        y_copy(nb - 1, (nb - 1) % 2).wait()


def _experts(blk0, nblk, cnt, x_sorted, w_up, w_down, b_up_grouped, b_down):
    tm = EXPERT_ROWS
    per_expert = lambda a: pl.BlockSpec((1,) + a.shape[1:], lambda e, b0, nb, ct: (e, 0, 0))
    grid_spec = pltpu.PrefetchScalarGridSpec(
        num_scalar_prefetch=3,
        grid=(N_EXPERTS,),
        in_specs=[per_expert(w_up), per_expert(w_down), per_expert(b_up_grouped), per_expert(b_down),
                  pl.BlockSpec(memory_space=pl.ANY)],
        out_specs=pl.BlockSpec(memory_space=pl.ANY),
        scratch_shapes=[pltpu.VMEM((2 * D_MODEL // (2 * LANES), D_MODEL, 2 * LANES), BF16),
                        pltpu.VMEM((D_MODEL // (2 * LANES), D_MODEL, 2 * LANES), BF16),
                        pltpu.VMEM((2, tm, x_sorted.shape[1]), x_sorted.dtype), pltpu.VMEM((2, tm, D_MODEL), F32),
                        pltpu.SemaphoreType.DMA((2,)), pltpu.SemaphoreType.DMA((2,))],
    )
    return pl.pallas_call(
        _expert_body,
        grid_spec=grid_spec,
        out_shape=jax.ShapeDtypeStruct((x_sorted.shape[0], D_MODEL), F32),
        compiler_params=_params(("arbitrary",)),
        name="experts",
    )(blk0, nblk, cnt, w_up, w_down, b_up_grouped, b_down, x_sorted)


def _combine_body(gate_ref, xmid_ref, gfin_ref, y0_ref, y1_ref, y2_ref, y3_ref, out_ref):
    gate = gate_ref[...]
    moe = y0_ref[...] * gate[:, 0:1]
    for kx, y_ref in enumerate((y1_ref, y2_ref, y3_ref), start=1):
        moe = moe + y_ref[...] * gate[:, kx:kx + 1]
    out_ref[...] = _rms(xmid_ref[...] + moe, gfin_ref[...])


def _combine(gates, first_token, xmid, gfin, y_rows):
    n = xmid.shape[0]
    tt = COMBINE_ROWS
    nblk = n // tt
    blk0 = first_token // tt
    assert blk0 * tt == first_token
    choice = lambda kx: pl.BlockSpec((tt, D_MODEL), lambda i: (i + kx * nblk, 0))
    return pl.pallas_call(
        _combine_body,
        grid=(nblk,),
        in_specs=[pl.BlockSpec((tt, LANES), lambda i: (i + blk0, 0)),
                  pl.BlockSpec((tt, D_MODEL), lambda i: (i, 0)),
                  pl.BlockSpec((1, D_MODEL), lambda i: (0, 0))] + [choice(kx) for kx in range(TOP_K)],
        out_specs=pl.BlockSpec((tt, D_MODEL), lambda i: (i, 0)),
        out_shape=jax.ShapeDtypeStruct((n, D_MODEL), F32),
        compiler_params=_params(("arbitrary",)),
        name="combine",
    )(gates, xmid, gfin, y_rows, y_rows, y_rows, y_rows)


def kernel(x_prompt, x_sample, cache_k_win, cache_v_win, norm_attn_g, w_in, ln_v_g, ln_v_b, w_spatial, b_spatial,
           attn_sinks, w_out, norm_ffn_g, w_router, b_router, w_up, b_up, w_down, b_down, norm_final_g):
    bp, tp, _ = x_prompt.shape
    bs, ts, _ = x_sample.shape
    w_buf = cache_k_win.shape[2]
    assert bp == 1 and tp % MIX_ROWS == 0 and w_buf == WINDOW and (bs * ts) % PROJ_ROWS == 0 and 8 % ts == 0
    n_p, n_s = bp * tp, bs * ts
    row2 = lambda a: a.reshape(1, -1)

    w_in_bf = w_in[0].astype(BF16)
    w_out_bf = w_out[0].astype(BF16)
    tril = jnp.tril(jnp.ones((CHUNK, CHUNK), dtype=bool))
    wsp = jnp.where(tril[None], w_spatial[0], 0.0)
    wsp_bf = wsp.astype(BF16)
    bsp = jnp.broadcast_to(b_spatial[0][:, :, None], (GMLP_GROUPS, CHUNK, LANES))
    b_up_grouped = b_up[0].reshape(N_EXPERTS, -1, LANES, 2).transpose(0, 1, 3, 2).reshape(N_EXPERTS, 1, -1)
    bd = b_down[0][:, None, :]
    g1, g2, gfin = row2(norm_attn_g[0]), row2(norm_ffn_g[0]), row2(norm_final_g)
    lng, lnb = row2(ln_v_g[0]), row2(ln_v_b[0])
    wr_hi = w_router[0].astype(BF16)
    wr = jnp.concatenate([wr_hi, (w_router[0] - wr_hi.astype(F32)).astype(BF16)], axis=1)
    br = row2(b_router[0])
    sinks = attn_sinks[0]

    xp = x_prompt.reshape(n_p, D_MODEL)
    cs_p = _rotary_inputs(jnp.arange(tp, dtype=jnp.int32))
    q_p, k_p, v_p, a_p, vn_p, sgb_p = _proj(xp, g1, w_in_bf, cs_p, lng, lnb)
    xmid_p, xn2_p, lg_p = _mix(sinks, q_p, k_p, v_p, a_p, vn_p, sgb_p, xp, wsp_bf, bsp, w_out_bf, g2, wr, br)

    xs = x_sample.reshape(n_s, D_MODEL)
    pos_s = PAST_LEN + jnp.arange(ts, dtype=jnp.int32)
    cs_s = _rotary_inputs(jnp.tile(pos_s, bs))
    q_s, k_s, v_s, a_s, vn_s, sgb_s = _proj(xs, g1, w_in_bf, cs_s, lng, lnb)
    k_all = jnp.concatenate([cache_k_win[0], k_s.reshape(bs, ts, N_KV_HEADS, HEAD_DIM)], axis=1)
    v_all = jnp.concatenate([cache_v_win[0], v_s.reshape(bs, ts, N_KV_HEADS, HEAD_DIM)], axis=1)
    n_keys = w_buf + ts
    key_pad = (-n_keys) % 8
    to_heads = lambda t: jnp.pad(t, ((0, 0), (0, key_pad), (0, 0), (0, 0))).transpose(2, 0, 1, 3).astype(BF16)
    qh = q_s.reshape(bs, ts, N_KV_HEADS, GQA_GROUP, HEAD_DIM).transpose(2, 0, 1, 3, 4)
    qh = qh.reshape(N_KV_HEADS, bs, ts * GQA_GROUP, HEAD_DIM)
    sink_rows = jnp.tile(sinks.reshape(N_KV_HEADS, 1, GQA_GROUP), (1, ts, 1)).reshape(N_KV_HEADS, ts * GQA_GROUP, 1)
    oh = _sample_attn(sink_rows, qh, to_heads(k_all), to_heads(v_all))
    o_s = oh.reshape(N_KV_HEADS, bs, ts, GQA_GROUP, HEAD_DIM).transpose(1, 2, 0, 3, 4).reshape(n_s, Q_WIDTH)
    t_idx = jnp.arange(ts)
    coef = jnp.stack([jnp.where((t_idx >= d)[None, :], wsp[:, t_idx, jnp.maximum(t_idx - d, 0)], 0.0)
                      for d in range(ts)])
    coef = jnp.repeat(coef.transpose(0, 2, 1), GMLP_WIDTH // GMLP_GROUPS, axis=2)
    coef = jnp.tile(coef, (1, 8 // ts, 1))
    bias = jnp.tile(jnp.repeat(b_spatial[0][:, :ts].T, GMLP_WIDTH // GMLP_GROUPS, axis=1), (8 // ts, 1))
    xmid_s, xn2_s, lg_s = _mix_sample(a_s, vn_s, sgb_s, o_s, xs, coef, bias, w_out_bf, g2, wr, br)

    n_tok = n_p + n_s
    idx_w, gate_w, rank_w, counts = _route(jnp.concatenate([lg_p, lg_s], axis=0))
    counts = counts[0]
    tm = EXPERT_ROWS
    padded = (counts + tm - 1) // tm * tm
    pstart = jnp.cumsum(padded) - padded
    dest_t = _slots(idx_w, rank_w, pstart)
    n_blocks = (n_tok * TOP_K) // tm + N_EXPERTS

    x_sorted = _sc_dispatch(xn2_p, xn2_s, dest_t, n_blocks * tm)
    y_sorted = _experts(pstart // tm, padded // tm, counts, x_sorted, w_up[0], w_down[0], b_up_grouped, bd)
    yrows_p, yrows_s = _sc_collect(y_sorted, dest_t, n_p, n_s)
    y_p = _combine(gate_w, 0, xmid_p, gfin, yrows_p)
    y_s = _combine(gate_w, n_p, xmid_s, gfin, yrows_s)

    k4 = lambda t: t.reshape(1, bp, -1, N_KV_HEADS, HEAD_DIM)
    return (y_p.reshape(bp, tp, D_MODEL),
            y_s.reshape(bs, ts, D_MODEL),
            k4(k_p[n_p - WINDOW:]),
            k4(v_p[n_p - WINDOW:]),
            vn_p[n_p - CHUNK:].reshape(1, bp, CHUNK, GMLP_WIDTH),
            k_all[None, :, ts:],
            v_all[None, :, ts:],
            vn_s.reshape(1, bs, ts, GMLP_WIDTH))
```

```python
import functools

import numpy as np
import jax
import jax.numpy as jnp
from jax import lax
from jax.experimental import pallas as pl
from jax.experimental.pallas import tpu as pltpu
from jax.experimental.pallas import tpu_sc as plsc

F32 = jnp.float32
BF16 = jnp.bfloat16

D_MODEL = 1024
HEAD_DIM = 64
N_HEADS = 16
GQA_GROUP = 8
N_KV_HEADS = 2
Q_WIDTH = 1024
KV_WIDTH = 128
WINDOW = 128
ROT_DIM = 16
ROPE_THETA = 500000.0
CHUNK = 128
GMLP_WIDTH = 1024
GMLP_GROUPS = 8
N_EXPERTS = 32
TOP_K = 4
SWIGLU_LIMIT = 7.0
SWIGLU_ALPHA = 1.702
RMS_EPS = 1e-5
LN_EPS = 1e-5
NEG_INF = -1e30
PAST_LEN = 16384

LANES = 128
VMEM_LIMIT = 56 * 1024 * 1024

PROJ_ROWS = 256
MIX_ROWS = 512
ROUTE_ROWS = 512
SLOT_ROWS = 1536
EXPERT_ROWS = 256
COMBINE_ROWS = 256

SC_CORES = 2
SC_WORKERS = 32
SC_DISPATCH_ROWS = 16
SC_COLLECT_ROWS = 32

_C_Q, _C_KV, _C_U, _C_VG, _C_GA, _C_GB, _C_END = 0, 1024, 1280, 2304, 3328, 4352, 5376


def _params(sem):
    return pltpu.CompilerParams(dimension_semantics=sem, vmem_limit_bytes=VMEM_LIMIT)


def _rms(x, g):
    return x * lax.rsqrt(jnp.mean(x * x, axis=-1, keepdims=True) + RMS_EPS) * g


def _proj_body(x_ref, g_ref, w_ref, cs_ref, rot_ref, lng_ref, lnb_ref,
               q_ref, k_ref, v_ref, a_ref, vn_ref, sgb_ref):
    h = _rms(x_ref[...], g_ref[...]).astype(BF16)
    tabs = lax.dot_general(cs_ref[...], rot_ref[...], (((0,), (0,)), ((), ())),
                           preferred_element_type=F32, precision=lax.Precision.HIGHEST)
    rc, rs1, rs2 = tabs[:, :LANES], tabs[:, LANES:2 * LANES], tabs[:, 2 * LANES:]

    def rot(z):
        return z * rc + pltpu.roll(z, LANES - ROT_DIM // 2, 1) * rs1 + pltpu.roll(z, ROT_DIM // 2, 1) * rs2

    def mm(lo, hi):
        return jnp.dot(h, w_ref[:, lo:hi], preferred_element_type=F32)

    zq = mm(_C_Q, _C_KV)
    for c in range(Q_WIDTH // LANES):
        sl = slice(c * LANES, (c + 1) * LANES)
        q_ref[:, sl] = (rot(zq[:, sl]) * (HEAD_DIM ** -0.5)).astype(BF16)
    zkv = mm(_C_KV, _C_U)
    k_ref[...] = rot(zkv[:, :KV_WIDTH])
    v_ref[...] = zkv[:, KV_WIDTH:]
    a_ref[...] = jax.nn.sigmoid(mm(_C_GA, _C_GB)) * jax.nn.gelu(mm(_C_U, _C_VG))
    zv = jax.nn.gelu(mm(_C_VG, _C_GA))
    zc = zv - jnp.mean(zv, axis=-1, keepdims=True)
    var = jnp.mean(zc * zc, axis=-1, keepdims=True)
    vn_ref[...] = zc * lax.rsqrt(var + LN_EPS) * lng_ref[...] + lnb_ref[...]
    sgb_ref[...] = jax.nn.sigmoid(mm(_C_GB, _C_END))


def _proj(x, norm_g, w_in_bf, cs, ln_g, ln_b):
    n = x.shape[0]
    tm = PROJ_ROWS
    row = lambda w: pl.BlockSpec((tm, w), lambda i: (i, 0))
    full = lambda a: pl.BlockSpec(a.shape, lambda i: (0,) * a.ndim)
    rot = jnp.asarray(_ROT_EXPAND)
    return pl.pallas_call(
        _proj_body,
        grid=(n // tm,),
        in_specs=[row(D_MODEL), full(norm_g), full(w_in_bf), pl.BlockSpec((cs.shape[0], tm), lambda i: (0, i)),
                  full(rot),
                  full(ln_g), full(ln_b)],
        out_specs=[row(Q_WIDTH), row(KV_WIDTH), row(KV_WIDTH), row(GMLP_WIDTH), row(GMLP_WIDTH), row(D_MODEL)],
        out_shape=[jax.ShapeDtypeStruct((n, Q_WIDTH), BF16),
                   jax.ShapeDtypeStruct((n, KV_WIDTH), F32),
                   jax.ShapeDtypeStruct((n, KV_WIDTH), F32),
                   jax.ShapeDtypeStruct((n, GMLP_WIDTH), F32),
                   jax.ShapeDtypeStruct((n, GMLP_WIDTH), F32),
                   jax.ShapeDtypeStruct((n, D_MODEL), F32)],
        compiler_params=_params(("arbitrary",)),
        name="proj",
    )(x, norm_g, w_in_bf, cs, rot, ln_g, ln_b)


_ROT_COLS = 32


def _rot_expand():
    half = ROT_DIM // 2
    m = np.zeros((_ROT_COLS, 3 * LANES), np.float32)
    for lane in range(LANES):
        d = lane % HEAD_DIM
        if d < ROT_DIM:
            m[d % half, lane] = 1.0
        else:
            m[2 * half, lane] = 1.0
        if d < half:
            m[half + d, LANES + lane] = -1.0
        elif d < ROT_DIM:
            m[half + d - half, 2 * LANES + lane] = 1.0
    return m


_ROT_EXPAND = _rot_expand()


def _rotary_inputs(pos):
    half = ROT_DIM // 2
    inv_freq = ROPE_THETA ** (-jnp.arange(half, dtype=F32) / half)
    ang = inv_freq[:, None] * pos.astype(F32)[None, :]
    n = pos.shape[0]
    return jnp.concatenate([jnp.cos(ang), jnp.sin(ang), jnp.ones((1, n), F32),
                            jnp.zeros((_ROT_COLS - 2 * half - 1, n), F32)], axis=0)


def _finish_rows(merged_bf, x, wout_ref, g2_ref, wr_ref, br_ref, xmid_ref, xn2_ref, lg_ref):
    xm = x + jnp.dot(merged_bf, wout_ref[...], preferred_element_type=F32)
    xmid_ref[...] = xm
    xn = _rms(xm, g2_ref[...])
    x_hi = xn.astype(BF16)
    x_lo = (xn - x_hi.astype(F32)).astype(BF16)
    w_hl = wr_ref[...]
    p_hi = jnp.dot(x_hi, w_hl, preferred_element_type=F32)
    p_lo = jnp.dot(x_lo, w_hl[:, :N_EXPERTS], preferred_element_type=F32)
    lg_ref[...] = p_hi[:, :N_EXPERTS] + (p_hi[:, N_EXPERTS:] + p_lo) + br_ref[...]
    bits = lax.bitcast_convert_type(xn.astype(BF16).astype(F32), jnp.uint32)
    xn2_ref[...] = (bits[:, :D_MODEL // 2] >> 16) | bits[:, D_MODEL // 2:]


def _mix_body(sinks_ref, q_ref, k_ref, kp_ref, v_ref, vp_ref, a_ref, vn_ref, sgb_ref, x_ref,
              wsp_ref, bsp_ref, wout_ref, g2_ref, wr_ref, br_ref,
              xmid_ref, xn2_ref, lg_ref, kcat, vcat, mrg):
    i = pl.program_id(0)
    nsub = MIX_ROWS // WINDOW
    kcat[0:WINDOW] = kp_ref[...]
    kcat[WINDOW:] = k_ref[...]
    vcat[0:WINDOW] = vp_ref[...]
    vcat[WINDOW:] = v_ref[...]

    pair_rows = (GQA_GROUP // 2) * WINDOW
    rq = lax.broadcasted_iota(jnp.int32, (pair_rows, 4 * WINDOW), 0) & (WINDOW - 1)
    ck = lax.broadcasted_iota(jnp.int32, (pair_rows, 4 * WINDOW), 1) & (2 * WINDOW - 1)
    band = (ck > rq) & (ck <= rq + WINDOW)
    lane_kv = lax.broadcasted_iota(jnp.int32, (2 * WINDOW, LANES), 1)
    lane_o = lax.broadcasted_iota(jnp.int32, (pair_rows, LANES), 1)
    row_p = lax.broadcasted_iota(jnp.int32, (pair_rows, 1), 0) >> 7

    def sub(j, carry):
        off = pl.multiple_of(j * WINDOW, WINDOW)
        rows = pl.ds(off, WINDOW)
        for g in range(GMLP_GROUPS):
            cols = slice(g * LANES, (g + 1) * LANES)
            s = jnp.dot(wsp_ref[g], vn_ref[rows, cols].astype(BF16), preferred_element_type=F32) + bsp_ref[g]
            mrg[rows, cols] = a_ref[rows, cols] * s
        kblk = kcat[pl.ds(off, 2 * WINDOW), :]
        vblk = vcat[pl.ds(off, 2 * WINDOW), :]
        kswp = pltpu.roll(kblk, HEAD_DIM, 1)
        vswp = pltpu.roll(vblk, HEAD_DIM, 1)
        kmin = jnp.where(jnp.logical_and(i == 0, j == 0), WINDOW, 0)
        allowed = band & (ck >= kmin)
        for kk in range(N_KV_HEADS):
            lo_src, hi_src = (kblk, kswp) if kk == 0 else (kswp, kblk)
            kbd = jnp.concatenate([jnp.where(lane_kv < HEAD_DIM, lo_src, 0.0),
                                   jnp.where(lane_kv >= HEAD_DIM, hi_src, 0.0)], axis=0).astype(BF16)
            lo_src, hi_src = (vblk, vswp) if kk == 0 else (vswp, vblk)
            vbd = jnp.concatenate([jnp.where(lane_kv < HEAD_DIM, lo_src, 0.0),
                                   jnp.where(lane_kv >= HEAD_DIM, hi_src, 0.0)], axis=0).astype(BF16)
            pair0 = kk * (GQA_GROUP // 2)
            qs = jnp.concatenate([q_ref[rows, (pair0 + p) * LANES:(pair0 + p + 1) * LANES]
                                  for p in range(GQA_GROUP // 2)], axis=0)
            lg = lax.dot_general(qs, kbd, (((1,), (1,)), ((), ())), preferred_element_type=F32)
            lg = jnp.where(allowed, lg, NEG_INF)
            h0 = kk * GQA_GROUP
            se = jnp.full((pair_rows, 1), sinks_ref[h0], F32)
            so = jnp.full((pair_rows, 1), sinks_ref[h0 + 1], F32)
            for p in range(1, GQA_GROUP // 2):
                se = jnp.where(row_p == p, sinks_ref[h0 + 2 * p], se)
                so = jnp.where(row_p == p, sinks_ref[h0 + 2 * p + 1], so)
            le, lo = lg[:, :2 * WINDOW], lg[:, 2 * WINDOW:]
            me = jnp.maximum(jnp.max(le, axis=1, keepdims=True), se)
            mo = jnp.maximum(jnp.max(lo, axis=1, keepdims=True), so)
            pe = jnp.exp(le - me)
            po = jnp.exp(lo - mo)
            de = jnp.sum(pe, axis=1, keepdims=True) + jnp.exp(se - me)
            do = jnp.sum(po, axis=1, keepdims=True) + jnp.exp(so - mo)
            pr = jnp.concatenate([pe, po], axis=1).astype(BF16)
            o = jnp.dot(pr, vbd, preferred_element_type=F32)
            o = o / jnp.where(lane_o < HEAD_DIM, de, do)
            for p in range(GQA_GROUP // 2):
                cols = slice((pair0 + p) * LANES, (pair0 + p + 1) * LANES)
                mrg[rows, cols] += sgb_ref[rows, cols] * o[p * WINDOW:(p + 1) * WINDOW]
        return carry

    lax.fori_loop(0, nsub, sub, 0)
    _finish_rows(mrg[...].astype(BF16), x_ref[...], wout_ref, g2_ref, wr_ref, br_ref, xmid_ref, xn2_ref, lg_ref)


def _mix(sinks, q, k, v, a, vn, sgb, x, wsp, bsp, wout, g2, wr, br):
    n = x.shape[0]
    tm = MIX_ROWS
    nsub = tm // WINDOW
    row = lambda w: pl.BlockSpec((tm, w), lambda i: (i, 0))
    prev = pl.BlockSpec((WINDOW, KV_WIDTH), lambda i: (jnp.maximum(i * nsub - 1, 0), 0))
    full = lambda arr: pl.BlockSpec(arr.shape, lambda i: (0,) * arr.ndim)
    smem = pl.BlockSpec(memory_space=pltpu.SMEM)
    return pl.pallas_call(
        _mix_body,
        grid=(n // tm,),
        in_specs=[smem, row(Q_WIDTH), row(KV_WIDTH), prev, row(KV_WIDTH), prev,
                  row(GMLP_WIDTH), row(GMLP_WIDTH), row(D_MODEL), row(D_MODEL),
                  full(wsp), full(bsp), full(wout), full(g2), full(wr), full(br)],
        out_specs=[row(D_MODEL), row(D_MODEL // 2), row(N_EXPERTS)],
        out_shape=[jax.ShapeDtypeStruct((n, D_MODEL), F32),
                   jax.ShapeDtypeStruct((n, D_MODEL // 2), jnp.uint32),
                   jax.ShapeDtypeStruct((n, N_EXPERTS), F32)],
        scratch_shapes=[pltpu.VMEM((tm + WINDOW, KV_WIDTH), F32),
                        pltpu.VMEM((tm + WINDOW, KV_WIDTH), F32),
                        pltpu.VMEM((tm, D_MODEL), F32)],
        compiler_params=_params(("arbitrary",)),
        name="mix_prompt",
    )(sinks, q, k, k, v, v, a, vn, sgb, x, wsp, bsp, wout, g2, wr, br)


def _sample_attn_body(sink_ref, q_ref, k_ref, v_ref, o_ref):
    q = q_ref[0]
    k = k_ref[0]
    v = v_ref[0]
    nq, nk = q.shape[1], k.shape[1]
    lg = jnp.einsum("bqd,bkd->bqk", q, k, preferred_element_type=F32)
    t = lax.broadcasted_iota(jnp.int32, (1, nq, nk), 1) // GQA_GROUP
    j = lax.broadcasted_iota(jnp.int32, (1, nq, nk), 2)
    lg = jnp.where((j > t) & (j <= t + WINDOW), lg, NEG_INF)
    sink = sink_ref[0][None]
    m = jnp.maximum(jnp.max(lg, axis=2, keepdims=True), sink)
    p = jnp.exp(lg - m)
    den = jnp.sum(p, axis=2, keepdims=True) + jnp.exp(sink - m)
    o = jnp.einsum("bqk,bkd->bqd", p.astype(BF16), v, preferred_element_type=F32)
    o_ref[0] = o / den


def _sample_attn(sink_rows, qh, kh, vh):
    nb = qh.shape[1]
    bb = 32
    blk = lambda a: pl.BlockSpec((1, bb) + a.shape[2:], lambda kk, b: (kk, b, 0, 0))
    return pl.pallas_call(
        _sample_attn_body,
        grid=(N_KV_HEADS, nb // bb),
        in_specs=[pl.BlockSpec((1,) + sink_rows.shape[1:], lambda kk, b: (kk, 0, 0)), blk(qh), blk(kh), blk(vh)],
        out_specs=pl.BlockSpec((1, bb) + qh.shape[2:], lambda kk, b: (kk, b, 0, 0)),
        out_shape=jax.ShapeDtypeStruct(qh.shape, F32),
        compiler_params=_params(("arbitrary", "arbitrary")),
        name="attn_sample",
    )(sink_rows, qh, kh, vh)


def _mix_sample_body(a_ref, vn_ref, sgb_ref, o_ref, x_ref, coef_ref, bias_ref,
                     wout_ref, g2_ref, wr_ref, br_ref, xmid_ref, xn2_ref, lg_ref):
    vn = vn_ref[...]
    n, width = vn.shape
    rows8 = lambda t: t.reshape(n // 8, 8, width)
    s = bias_ref[...][None] + coef_ref[0][None] * rows8(vn)
    for d in range(1, coef_ref.shape[0]):
        s = s + coef_ref[d][None] * rows8(pltpu.roll(vn, d, 0))
    merged = a_ref[...] * s.reshape(n, width) + sgb_ref[...] * o_ref[...]
    _finish_rows(merged.astype(BF16), x_ref[...], wout_ref, g2_ref, wr_ref, br_ref, xmid_ref, xn2_ref, lg_ref)


def _mix_sample(a, vn, sgb, o, x, coef, bias, wout, g2, wr, br):
    n = x.shape[0]
    args = (a, vn, sgb, o, x, coef, bias, wout, g2, wr, br)
    full = lambda arr: pl.BlockSpec(arr.shape, lambda i: (0,) * arr.ndim)
    return pl.pallas_call(
        _mix_sample_body,
        grid=(1,),
        in_specs=[full(arr) for arr in args],
        out_specs=[pl.BlockSpec((n, D_MODEL), lambda i: (0, 0)), pl.BlockSpec((n, D_MODEL // 2), lambda i: (0, 0)),
                   pl.BlockSpec((n, N_EXPERTS), lambda i: (0, 0))],
        out_shape=[jax.ShapeDtypeStruct((n, D_MODEL), F32),
                   jax.ShapeDtypeStruct((n, D_MODEL // 2), jnp.uint32),
                   jax.ShapeDtypeStruct((n, N_EXPERTS), F32)],
        compiler_params=_params(("arbitrary",)),
        name="mix_sample",
    )(*args)


def _route_body(nblk_p, lgp_ref, lgs_ref, idx_ref, gate_ref, rank_ref, cnt_ref, base):
    i = pl.program_id(0)

    @pl.when(i == 0)
    def _():
        base[...] = jnp.zeros_like(base)

    l = jnp.where(jnp.full(lgp_ref.shape, i, jnp.int32) < nblk_p, lgp_ref[...], lgs_ref[...])
    tb = l.shape[0]
    lane = lax.broadcasted_iota(jnp.int32, l.shape, 1).astype(F32)
    vals, idxs, sels = [], [], []
    for _ in range(TOP_K):
        m = jnp.max(l, axis=1, keepdims=True)
        ik = jnp.min(jnp.where(l == m, lane, float(N_EXPERTS)), axis=1, keepdims=True)
        sel = lane == ik
        l = jnp.where(sel, -jnp.inf, l)
        vals.append(m)
        idxs.append(ik)
        sels.append(sel)
    es = [jnp.exp(vk - vals[0]) for vk in vals]
    den = es[0] + es[1] + es[2] + es[3]
    onehot = jnp.zeros(l.shape, F32)
    for sel in sels:
        onehot = onehot + sel.astype(F32)
    tri = (lax.broadcasted_iota(jnp.int32, (tb, tb), 0) > lax.broadcasted_iota(jnp.int32, (tb, tb), 1))
    before = jnp.dot(tri.astype(BF16), onehot.astype(BF16), preferred_element_type=F32) + base[...]
    ranks = [jnp.sum(jnp.where(sel, before, 0.0), axis=1, keepdims=True) for sel in sels]
    base[...] += jnp.sum(onehot, axis=0, keepdims=True)
    cnt_ref[...] = base[...].astype(jnp.int32)

    lane_o = lax.broadcasted_iota(jnp.int32, (tb, LANES), 1)

    def spread(cols, dtype):
        out = jnp.zeros((tb, LANES), dtype)
        for kx, col in enumerate(cols):
            out = jnp.where(lane_o == kx, col.astype(dtype), out)
        return out

    idx_ref[...] = spread(idxs, jnp.int32)
    gate_ref[...] = spread([e / den for e in es], F32)
    rank_ref[...] = spread(ranks, jnp.int32)


def _route(logits_p, logits_s):
    tb = ROUTE_ROWS
    nblk_p, nblk_s = logits_p.shape[0] // tb, logits_s.shape[0] // tb
    n = (nblk_p + nblk_s) * tb
    assert n == logits_p.shape[0] + logits_s.shape[0]
    wide = pl.BlockSpec((tb, LANES), lambda i: (i, 0))
    return pl.pallas_call(
        functools.partial(_route_body, nblk_p),
        grid=(n // tb,),
        in_specs=[pl.BlockSpec((tb, N_EXPERTS), lambda i: (jnp.minimum(i, nblk_p - 1), 0)),
                  pl.BlockSpec((tb, N_EXPERTS), lambda i: (jnp.maximum(i - nblk_p, 0), 0))],
        out_specs=[wide, wide, wide, pl.BlockSpec((1, N_EXPERTS), lambda i: (0, 0))],
        out_shape=[jax.ShapeDtypeStruct((n, LANES), jnp.int32),
                   jax.ShapeDtypeStruct((n, LANES), F32),
                   jax.ShapeDtypeStruct((n, LANES), jnp.int32),
                   jax.ShapeDtypeStruct((1, N_EXPERTS), jnp.int32)],
        scratch_shapes=[pltpu.VMEM((1, N_EXPERTS), F32)],
        compiler_params=_params(("arbitrary",)),
        name="route",
    )(logits_p, logits_s)


def _slots_body(idx_ref, rank_ref, pstart_ref, dest_ref):
    idx = idx_ref[...]
    rank = rank_ref[...]
    tb = idx.shape[0]
    lane_e = lax.broadcasted_iota(jnp.int32, (tb, N_EXPERTS), 1)
    lane_o = lax.broadcasted_iota(jnp.int32, (tb, LANES), 1)
    pstart = pstart_ref[...]
    dest = jnp.zeros((tb, LANES), F32)
    for kx in range(TOP_K):
        seg = jnp.sum(jnp.where(lane_e == idx[:, kx:kx + 1], pstart, 0.0), axis=1, keepdims=True)
        dest = jnp.where(lane_o == kx, seg + rank[:, kx:kx + 1].astype(F32), dest)
    dest_ref[...] = dest.T[:dest_ref.shape[0]].astype(jnp.int32)


def _slots(idx_w, rank_w, pstart):
    n = idx_w.shape[0]
    tb = SLOT_ROWS
    assert n % tb == 0
    wide = pl.BlockSpec((tb, LANES), lambda i: (i, 0))
    return pl.pallas_call(
        _slots_body,
        grid=(n // tb,),
        in_specs=[wide, wide, pl.BlockSpec((1, N_EXPERTS), lambda i: (0, 0))],
        out_specs=pl.BlockSpec((8, tb), lambda i: (0, i)),
        out_shape=jax.ShapeDtypeStruct((8, n), jnp.int32),
        compiler_params=_params(("arbitrary",)),
        name="slots",
    )(idx_w, rank_w, pstart.astype(F32).reshape(1, N_EXPERTS))


def _sc_mesh():
    return plsc.VectorSubcoreMesh(core_axis_name="c", subcore_axis_name="s")


def _sc_worker():
    return lax.axis_index("s") * SC_CORES + lax.axis_index("c")


def _sc_dispatch(x_p, x_s, dest_t, n_slots):
    chunk = SC_DISPATCH_ROWS
    n_p, n_s = x_p.shape[0], x_s.shape[0]
    nch_p, nch_s = n_p // (SC_WORKERS * chunk), n_s // (SC_WORKERS * chunk)
    assert nch_p * SC_WORKERS * chunk == n_p and nch_s * SC_WORKERS * chunk == n_s
    d3 = dest_t.reshape(dest_t.shape[0], (n_p + n_s) // chunk, chunk)
    width, dtype = x_p.shape[1], x_p.dtype

    @functools.partial(
        pl.kernel, mesh=_sc_mesh(),
        out_type=jax.ShapeDtypeStruct((n_slots, width), dtype),
        scratch_types=[pltpu.VMEM((TOP_K, nch_p, chunk), jnp.int32),
                       pltpu.VMEM((TOP_K, nch_s, chunk), jnp.int32),
                       pltpu.VMEM((chunk, width), dtype),
                       pltpu.SemaphoreType.DMA],
        compiler_params=pltpu.CompilerParams(use_tc_tiling_on_sc=True),
        name="dispatch")
    def run(xp_hbm, xs_hbm, d_hbm, out_hbm, ip_v, is_v, rows_v, sem):
        wid = _sc_worker()
        pltpu.sync_copy(d_hbm.at[pl.ds(0, TOP_K), pl.ds(wid * nch_p, nch_p)], ip_v)
        pltpu.sync_copy(d_hbm.at[pl.ds(0, TOP_K), pl.ds(n_p // chunk + wid * nch_s, nch_s)], is_v)

        def group(x_hbm, idx_v, nch):
            def body(j, carry):
                pltpu.sync_copy(x_hbm.at[pl.ds((wid * nch + j) * chunk, chunk)], rows_v)
                copies = [pltpu.async_copy(rows_v, out_hbm.at[idx_v.at[kx, j]], sem) for kx in range(TOP_K)]
                for cp in copies:
                    cp.wait()
                return carry

            lax.fori_loop(0, nch, body, 0)

        group(xp_hbm, ip_v, nch_p)
        group(xs_hbm, is_v, nch_s)

    return run(x_p, x_s, d3)


def _sc_collect(y_sorted, dest_t, n_p, n_s):
    chunk = SC_COLLECT_ROWS
    per_choice = SC_WORKERS // TOP_K
    nch_p, nch_s = n_p // (per_choice * chunk), n_s // (per_choice * chunk)
    assert nch_p * per_choice * chunk == n_p and nch_s * per_choice * chunk == n_s
    d3 = dest_t.reshape(dest_t.shape[0], (n_p + n_s) // chunk, chunk)

    @functools.partial(
        pl.kernel, mesh=_sc_mesh(),
        out_type=[jax.ShapeDtypeStruct((TOP_K * n_p, D_MODEL), F32), jax.ShapeDtypeStruct((TOP_K * n_s, D_MODEL), F32)],
        scratch_types=[pltpu.VMEM((nch_p, chunk), jnp.int32),
                       pltpu.VMEM((nch_s, chunk), jnp.int32),
                       pltpu.VMEM((chunk, D_MODEL), F32),
                       pltpu.SemaphoreType.DMA],
        compiler_params=pltpu.CompilerParams(use_tc_tiling_on_sc=True),
        name="collect")
    def run(y_hbm, d_hbm, op_hbm, os_hbm, ip_v, is_v, rows_v, sem):
        wid = _sc_worker()
        choice = wid // per_choice
        part = wid % per_choice
        pltpu.sync_copy(d_hbm.at[choice, pl.ds(part * nch_p, nch_p)], ip_v)
        pltpu.sync_copy(d_hbm.at[choice, pl.ds(n_p // chunk + part * nch_s, nch_s)], is_v)

        def group(idx_v, o_hbm, nch):
            def body(j, carry):
                pltpu.async_copy(y_hbm.at[idx_v.at[j]], rows_v, sem).wait()
                pltpu.sync_copy(rows_v, o_hbm.at[pl.ds((wid * nch + j) * chunk, chunk)])
                return carry

            lax.fori_loop(0, nch, body, 0)

        group(ip_v, op_hbm, nch_p)
        group(is_v, os_hbm, nch_s)

    return run(y_sorted, d3)


def _expert_body(blk0_ref, nblk_ref, cnt_ref, wup_ref, wdn_ref, bup_ref, bdn_ref, x_hbm, y_hbm,
                 wup_s, wdn_s, xbuf, obuf, in_sem, out_sem):
    e = pl.program_id(0)
    nb = nblk_ref[e]
    blk0 = blk0_ref[e]
    cnt = cnt_ref[e]
    tm = EXPERT_ROWS
    pair = 2 * LANES

    def x_copy(i, slot):
        rows = pl.ds(pl.multiple_of((blk0 + i) * tm, tm), tm)
        return pltpu.make_async_copy(x_hbm.at[rows], xbuf.at[slot], in_sem.at[slot])

    def y_copy(i, slot):
        rows = pl.ds(pl.multiple_of((blk0 + i) * tm, tm), tm)
        return pltpu.make_async_copy(obuf.at[slot], y_hbm.at[rows], out_sem.at[slot])

    @pl.when(nb > 0)
    def _():
        x_copy(0, 0).start(priority=1)
        r = lax.broadcasted_iota(jnp.int32, (pair, pair), 0)
        c = lax.broadcasted_iota(jnp.int32, (pair, pair), 1)
        perm = (r == jnp.where(c < LANES, 2 * c, 2 * (c - LANES) + 1)).astype(BF16)
        for g in range(2 * D_MODEL // pair):
            cols = slice(g * pair, (g + 1) * pair)
            wup_s[g] = jnp.dot(wup_ref[0, :, cols].astype(BF16), perm, preferred_element_type=F32).astype(BF16)
        for g in range(D_MODEL // pair):
            wdn_s[g] = wdn_ref[0, :, g * pair:(g + 1) * pair].astype(BF16)

        def block(i, carry):
            slot = i % 2
            x_copy(i, slot).wait()

            @pl.when(i + 1 < nb)
            def _():
                x_copy(i + 1, 1 - slot).start(priority=1)

            @pl.when(i >= 2)
            def _():
                y_copy(i - 2, slot).wait()

            words = xbuf[slot]
            lo = lax.bitcast_convert_type(words << 16, F32)
            hi = lax.bitcast_convert_type(words & jnp.uint32(0xFFFF0000), F32)
            row = lax.broadcasted_iota(jnp.int32, (tm, 1), 0)
            x = jnp.where(row < cnt - i * tm, jnp.concatenate([lo, hi], axis=1), 0.0).astype(BF16)
            acts = []
            for g in range(2 * D_MODEL // pair):
                cols = slice(g * pair, (g + 1) * pair)
                h = jnp.dot(x, wup_s[g], preferred_element_type=F32) + bup_ref[0, :, cols]
                glu = jnp.minimum(h[:, :LANES], SWIGLU_LIMIT)
                lin = jnp.clip(h[:, LANES:], -SWIGLU_LIMIT, SWIGLU_LIMIT)
                acts.append((glu * jax.nn.sigmoid(SWIGLU_ALPHA * glu) * (lin + 1.0)).astype(BF16))
            act = jnp.concatenate(acts, axis=1)
            for g in range(D_MODEL // pair):
                cols = slice(g * pair, (g + 1) * pair)
                obuf[slot, :, cols] = jnp.dot(act, wdn_s[g], preferred_element_type=F32) + bdn_ref[0, :, cols]
            y_copy(i, slot).start(priority=1)
            return carry

        lax.fori_loop(0, nb, block, 0)

        @pl.when(nb >= 2)
        def _():
            y_copy(nb - 2, nb % 2).wait()

        y_copy(nb - 1, (nb - 1) % 2).wait()


def _experts(blk0, nblk, cnt, x_sorted, w_up, w_down, b_up_grouped, b_down):
    tm = EXPERT_ROWS
    per_expert = lambda a: pl.BlockSpec((1,) + a.shape[1:], lambda e, b0, nb, ct: (e, 0, 0))
    grid_spec = pltpu.PrefetchScalarGridSpec(
        num_scalar_prefetch=3,
        grid=(N_EXPERTS,),
        in_specs=[per_expert(w_up), per_expert(w_down), per_expert(b_up_grouped), per_expert(b_down),
                  pl.BlockSpec(memory_space=pl.ANY)],
        out_specs=pl.BlockSpec(memory_space=pl.ANY),
        scratch_shapes=[pltpu.VMEM((2 * D_MODEL // (2 * LANES), D_MODEL, 2 * LANES), BF16),
                        pltpu.VMEM((D_MODEL // (2 * LANES), D_MODEL, 2 * LANES), BF16),
                        pltpu.VMEM((2, tm, x_sorted.shape[1]), x_sorted.dtype), pltpu.VMEM((2, tm, D_MODEL), F32),
                        pltpu.SemaphoreType.DMA((2,)), pltpu.SemaphoreType.DMA((2,))],
    )
    return pl.pallas_call(
        _expert_body,
        grid_spec=grid_spec,
        out_shape=jax.ShapeDtypeStruct((x_sorted.shape[0], D_MODEL), F32),
        compiler_params=_params(("arbitrary",)),
        name="experts",
    )(blk0, nblk, cnt, w_up, w_down, b_up_grouped, b_down, x_sorted)


def _combine_body(gate_ref, xmid_ref, gfin_ref, y0_ref, y1_ref, y2_ref, y3_ref, out_ref):
    gate = gate_ref[...]
    moe = y0_ref[...] * gate[:, 0:1]
    for kx, y_ref in enumerate((y1_ref, y2_ref, y3_ref), start=1):
        moe = moe + y_ref[...] * gate[:, kx:kx + 1]
    out_ref[...] = _rms(xmid_ref[...] + moe, gfin_ref[...])


def _combine(gates, first_token, xmid, gfin, y_rows):
    n = xmid.shape[0]
    tt = COMBINE_ROWS
    nblk = n // tt
    blk0 = first_token // tt
    assert blk0 * tt == first_token
    choice = lambda kx: pl.BlockSpec((tt, D_MODEL), lambda i: (i + kx * nblk, 0))
    return pl.pallas_call(
        _combine_body,
        grid=(nblk,),
        in_specs=[pl.BlockSpec((tt, LANES), lambda i: (i + blk0, 0)),
                  pl.BlockSpec((tt, D_MODEL), lambda i: (i, 0)),
                  pl.BlockSpec((1, D_MODEL), lambda i: (0, 0))] + [choice(kx) for kx in range(TOP_K)],
        out_specs=pl.BlockSpec((tt, D_MODEL), lambda i: (i, 0)),
        out_shape=jax.ShapeDtypeStruct((n, D_MODEL), F32),
        compiler_params=_params(("arbitrary",)),
        name="combine",
    )(gates, xmid, gfin, y_rows, y_rows, y_rows, y_rows)


def kernel(x_prompt, x_sample, cache_k_win, cache_v_win, norm_attn_g, w_in, ln_v_g, ln_v_b, w_spatial, b_spatial,
           attn_sinks, w_out, norm_ffn_g, w_router, b_router, w_up, b_up, w_down, b_down, norm_final_g):
    bp, tp, _ = x_prompt.shape
    bs, ts, _ = x_sample.shape
    w_buf = cache_k_win.shape[2]
    assert bp == 1 and tp % MIX_ROWS == 0 and w_buf == WINDOW and (bs * ts) % PROJ_ROWS == 0 and 8 % ts == 0
    n_p, n_s = bp * tp, bs * ts
    row2 = lambda a: a.reshape(1, -1)

    w_in_bf = w_in[0].astype(BF16)
    w_out_bf = w_out[0].astype(BF16)
    tril = jnp.tril(jnp.ones((CHUNK, CHUNK), dtype=bool))
    wsp = jnp.where(tril[None], w_spatial[0], 0.0)
    wsp_bf = wsp.astype(BF16)
    bsp = jnp.broadcast_to(b_spatial[0][:, :, None], (GMLP_GROUPS, CHUNK, LANES))
    b_up_grouped = b_up[0].reshape(N_EXPERTS, -1, LANES, 2).transpose(0, 1, 3, 2).reshape(N_EXPERTS, 1, -1)
    bd = b_down[0][:, None, :]
    g1, g2, gfin = row2(norm_attn_g[0]), row2(norm_ffn_g[0]), row2(norm_final_g)
    lng, lnb = row2(ln_v_g[0]), row2(ln_v_b[0])
    wr_hi = w_router[0].astype(BF16)
    wr = jnp.concatenate([wr_hi, (w_router[0] - wr_hi.astype(F32)).astype(BF16)], axis=1)
    br = row2(b_router[0])
    sinks = attn_sinks[0]

    xp = x_prompt.reshape(n_p, D_MODEL)
    cs_p = _rotary_inputs(jnp.arange(tp, dtype=jnp.int32))
    q_p, k_p, v_p, a_p, vn_p, sgb_p = _proj(xp, g1, w_in_bf, cs_p, lng, lnb)
    xmid_p, xn2_p, lg_p = _mix(sinks, q_p, k_p, v_p, a_p, vn_p, sgb_p, xp, wsp_bf, bsp, w_out_bf, g2, wr, br)

    xs = x_sample.reshape(n_s, D_MODEL)
    pos_s = PAST_LEN + jnp.arange(ts, dtype=jnp.int32)
    cs_s = _rotary_inputs(jnp.tile(pos_s, bs))
    q_s, k_s, v_s, a_s, vn_s, sgb_s = _proj(xs, g1, w_in_bf, cs_s, lng, lnb)
    k_all = jnp.concatenate([cache_k_win[0], k_s.reshape(bs, ts, N_KV_HEADS, HEAD_DIM)], axis=1)
    v_all = jnp.concatenate([cache_v_win[0], v_s.reshape(bs, ts, N_KV_HEADS, HEAD_DIM)], axis=1)
    n_keys = w_buf + ts
    key_pad = (-n_keys) % 8
    to_heads = lambda t: jnp.pad(t, ((0, 0), (0, key_pad), (0, 0), (0, 0))).transpose(2, 0, 1, 3).astype(BF16)
    qh = q_s.reshape(bs, ts, N_KV_HEADS, GQA_GROUP, HEAD_DIM).transpose(2, 0, 1, 3, 4)
    qh = qh.reshape(N_KV_HEADS, bs, ts * GQA_GROUP, HEAD_DIM)
    sink_rows = jnp.tile(sinks.reshape(N_KV_HEADS, 1, GQA_GROUP), (1, ts, 1)).reshape(N_KV_HEADS, ts * GQA_GROUP, 1)
    oh = _sample_attn(sink_rows, qh, to_heads(k_all), to_heads(v_all))
    o_s = oh.reshape(N_KV_HEADS, bs, ts, GQA_GROUP, HEAD_DIM).transpose(1, 2, 0, 3, 4).reshape(n_s, Q_WIDTH)
    t_idx = jnp.arange(ts)
    coef = jnp.stack([jnp.where((t_idx >= d)[None, :], wsp[:, t_idx, jnp.maximum(t_idx - d, 0)], 0.0)
                      for d in range(ts)])
    coef = jnp.repeat(coef.transpose(0, 2, 1), GMLP_WIDTH // GMLP_GROUPS, axis=2)
    coef = jnp.tile(coef, (1, 8 // ts, 1))
    bias = jnp.tile(jnp.repeat(b_spatial[0][:, :ts].T, GMLP_WIDTH // GMLP_GROUPS, axis=1), (8 // ts, 1))
    xmid_s, xn2_s, lg_s = _mix_sample(a_s, vn_s, sgb_s, o_s, xs, coef, bias, w_out_bf, g2, wr, br)

    n_tok = n_p + n_s
    idx_w, gate_w, rank_w, counts = _route(lg_p, lg_s)
    counts = counts[0]
    tm = EXPERT_ROWS
    padded = (counts + tm - 1) // tm * tm
    pstart = jnp.cumsum(padded) - padded
    dest_t = _slots(idx_w, rank_w, pstart)
    n_blocks = (n_tok * TOP_K) // tm + N_EXPERTS

    x_sorted = _sc_dispatch(xn2_p, xn2_s, dest_t, n_blocks * tm)
    y_sorted = _experts(pstart // tm, padded // tm, counts, x_sorted, w_up[0], w_down[0], b_up_grouped, bd)
    yrows_p, yrows_s = _sc_collect(y_sorted, dest_t, n_p, n_s)
    y_p = _combine(gate_w, 0, xmid_p, gfin, yrows_p)
    y_s = _combine(gate_w, n_p, xmid_s, gfin, yrows_s)

    k4 = lambda t: t.reshape(1, bp, -1, N_KV_HEADS, HEAD_DIM)
    return (y_p.reshape(bp, tp, D_MODEL),
            y_s.reshape(bs, ts, D_MODEL),
            k4(k_p[n_p - WINDOW:]),
            k4(v_p[n_p - WINDOW:]),
            vn_p[n_p - CHUNK:].reshape(1, bp, CHUNK, GMLP_WIDTH),
            k_all[None, :, ts:],
            v_all[None, :, ts:],
            vn_s.reshape(1, bs, ts, GMLP_WIDTH))
```

```python
import functools

import numpy as np
import jax
import jax.numpy as jnp
from jax import lax
from jax.experimental import pallas as pl
from jax.experimental.pallas import tpu as pltpu
from jax.experimental.pallas import tpu_sc as plsc

F32 = jnp.float32
BF16 = jnp.bfloat16

D_MODEL = 1024
HEAD_DIM = 64
N_HEADS = 16
GQA_GROUP = 8
N_KV_HEADS = 2
Q_WIDTH = 1024
KV_WIDTH = 128
WINDOW = 128
ROT_DIM = 16
ROPE_THETA = 500000.0
CHUNK = 128
GMLP_WIDTH = 1024
GMLP_GROUPS = 8
N_EXPERTS = 32
TOP_K = 4
SWIGLU_LIMIT = 7.0
SWIGLU_ALPHA = 1.702
RMS_EPS = 1e-5
LN_EPS = 1e-5
NEG_INF = -1e30
PAST_LEN = 16384

LANES = 128
VMEM_LIMIT = 56 * 1024 * 1024

PROJ_ROWS = 256
MIX_ROWS = 512
ROUTE_ROWS = 512
SLOT_ROWS = 1536
EXPERT_ROWS = 256
COMBINE_ROWS = 256

SC_CORES = 2
SC_WORKERS = 32
SC_DISPATCH_ROWS = 16
SC_COLLECT_ROWS = 32

_C_Q, _C_KV, _C_U, _C_VG, _C_GA, _C_GB, _C_END = 0, 1024, 1280, 2304, 3328, 4352, 5376


def _params(sem):
    return pltpu.CompilerParams(dimension_semantics=sem, vmem_limit_bytes=VMEM_LIMIT)


def _rms(x, g):
    return x * lax.rsqrt(jnp.mean(x * x, axis=-1, keepdims=True) + RMS_EPS) * g


def _pack_bf16_pair(lo, hi):
    lo_bits = lax.bitcast_convert_type(lo.astype(BF16).astype(F32), jnp.uint32)
    hi_bits = lax.bitcast_convert_type(hi.astype(BF16).astype(F32), jnp.uint32)
    return (lo_bits >> 16) | hi_bits


def _unpack_bf16_pair(words):
    lo = lax.bitcast_convert_type(words << 16, F32)
    hi = lax.bitcast_convert_type(words & jnp.uint32(0xFFFF0000), F32)
    return jnp.concatenate([lo, hi], axis=1)


def _proj_body(x_ref, g_ref, w_ref, cs_ref, rot_ref, lng_ref, lnb_ref,
               q_ref, k_ref, v_ref, a_ref, vn_ref, sgb_ref):
    h = _rms(x_ref[...], g_ref[...]).astype(BF16)
    tabs = lax.dot_general(cs_ref[...], rot_ref[...], (((0,), (0,)), ((), ())),
                           preferred_element_type=F32, precision=lax.Precision.HIGHEST)
    rc, rs1, rs2 = tabs[:, :LANES], tabs[:, LANES:2 * LANES], tabs[:, 2 * LANES:]

    def rot(z):
        return z * rc + pltpu.roll(z, LANES - ROT_DIM // 2, 1) * rs1 + pltpu.roll(z, ROT_DIM // 2, 1) * rs2

    def mm(lo, hi):
        return jnp.dot(h, w_ref[:, lo:hi], preferred_element_type=F32)

    zq = mm(_C_Q, _C_KV)
    for c in range(Q_WIDTH // LANES):
        sl = slice(c * LANES, (c + 1) * LANES)
        q_ref[:, sl] = (rot(zq[:, sl]) * (HEAD_DIM ** -0.5)).astype(BF16)
    zkv = mm(_C_KV, _C_U)
    k_ref[...] = rot(zkv[:, :KV_WIDTH])
    v_ref[...] = zkv[:, KV_WIDTH:]
    a_ref[...] = jax.nn.sigmoid(mm(_C_GA, _C_GB)) * jax.nn.gelu(mm(_C_U, _C_VG))
    zv = jax.nn.gelu(mm(_C_VG, _C_GA))
    zc = zv - jnp.mean(zv, axis=-1, keepdims=True)
    var = jnp.mean(zc * zc, axis=-1, keepdims=True)
    vn_ref[...] = zc * lax.rsqrt(var + LN_EPS) * lng_ref[...] + lnb_ref[...]
    sgb_ref[...] = jax.nn.sigmoid(mm(_C_GB, _C_END))


def _proj(x, norm_g, w_in_bf, cs, ln_g, ln_b):
    n = x.shape[0]
    tm = PROJ_ROWS
    row = lambda w: pl.BlockSpec((tm, w), lambda i: (i, 0))
    full = lambda a: pl.BlockSpec(a.shape, lambda i: (0,) * a.ndim)
    rot = jnp.asarray(_ROT_EXPAND)
    return pl.pallas_call(
        _proj_body,
        grid=(n // tm,),
        in_specs=[row(D_MODEL), full(norm_g), full(w_in_bf), pl.BlockSpec((cs.shape[0], tm), lambda i: (0, i)),
                  full(rot),
                  full(ln_g), full(ln_b)],
        out_specs=[row(Q_WIDTH), row(KV_WIDTH), row(KV_WIDTH), row(GMLP_WIDTH), row(GMLP_WIDTH), row(D_MODEL)],
        out_shape=[jax.ShapeDtypeStruct((n, Q_WIDTH), BF16),
                   jax.ShapeDtypeStruct((n, KV_WIDTH), F32),
                   jax.ShapeDtypeStruct((n, KV_WIDTH), F32),
                   jax.ShapeDtypeStruct((n, GMLP_WIDTH), F32),
                   jax.ShapeDtypeStruct((n, GMLP_WIDTH), F32),
                   jax.ShapeDtypeStruct((n, D_MODEL), F32)],
        compiler_params=_params(("arbitrary",)),
        name="proj",
    )(x, norm_g, w_in_bf, cs, rot, ln_g, ln_b)


_ROT_COLS = 32


def _rot_expand():
    half = ROT_DIM // 2
    m = np.zeros((_ROT_COLS, 3 * LANES), np.float32)
    for lane in range(LANES):
        d = lane % HEAD_DIM
        if d < ROT_DIM:
            m[d % half, lane] = 1.0
        else:
            m[2 * half, lane] = 1.0
        if d < half:
            m[half + d, LANES + lane] = -1.0
        elif d < ROT_DIM:
            m[half + d - half, 2 * LANES + lane] = 1.0
    return m


_ROT_EXPAND = _rot_expand()


def _rotary_inputs(pos):
    half = ROT_DIM // 2
    inv_freq = ROPE_THETA ** (-jnp.arange(half, dtype=F32) / half)
    ang = inv_freq[:, None] * pos.astype(F32)[None, :]
    n = pos.shape[0]
    return jnp.concatenate([jnp.cos(ang), jnp.sin(ang), jnp.ones((1, n), F32),
                            jnp.zeros((_ROT_COLS - 2 * half - 1, n), F32)], axis=0)


def _finish_rows(merged_bf, x, wout_ref, g2_ref, wr_ref, br_ref, xmid_ref, xn2_ref, lg_ref):
    xm = x + jnp.dot(merged_bf, wout_ref[...], preferred_element_type=F32)
    xmid_ref[...] = xm
    xn = _rms(xm, g2_ref[...])
    x_hi = xn.astype(BF16)
    x_lo = (xn - x_hi.astype(F32)).astype(BF16)
    w_hl = wr_ref[...]
    p_hi = jnp.dot(x_hi, w_hl, preferred_element_type=F32)
    p_lo = jnp.dot(x_lo, w_hl[:, :N_EXPERTS], preferred_element_type=F32)
    lg_ref[...] = p_hi[:, :N_EXPERTS] + (p_hi[:, N_EXPERTS:] + p_lo) + br_ref[...]
    xn2_ref[...] = _pack_bf16_pair(xn[:, :D_MODEL // 2], xn[:, D_MODEL // 2:])


def _mix_body(sinks_ref, q_ref, k_ref, kp_ref, v_ref, vp_ref, a_ref, vn_ref, sgb_ref, x_ref,
              wsp_ref, bsp_ref, wout_ref, g2_ref, wr_ref, br_ref,
              xmid_ref, xn2_ref, lg_ref, kcat, vcat, mrg):
    i = pl.program_id(0)
    nsub = MIX_ROWS // WINDOW
    kcat[0:WINDOW] = kp_ref[...]
    kcat[WINDOW:] = k_ref[...]
    vcat[0:WINDOW] = vp_ref[...]
    vcat[WINDOW:] = v_ref[...]

    pair_rows = (GQA_GROUP // 2) * WINDOW
    rq = lax.broadcasted_iota(jnp.int32, (pair_rows, 4 * WINDOW), 0) & (WINDOW - 1)
    ck = lax.broadcasted_iota(jnp.int32, (pair_rows, 4 * WINDOW), 1) & (2 * WINDOW - 1)
    band = (ck > rq) & (ck <= rq + WINDOW)
    lane_kv = lax.broadcasted_iota(jnp.int32, (2 * WINDOW, LANES), 1)
    lane_o = lax.broadcasted_iota(jnp.int32, (pair_rows, LANES), 1)
    row_p = lax.broadcasted_iota(jnp.int32, (pair_rows, 1), 0) >> 7

    def sub(j, carry):
        off = pl.multiple_of(j * WINDOW, WINDOW)
        rows = pl.ds(off, WINDOW)
        for g in range(GMLP_GROUPS):
            cols = slice(g * LANES, (g + 1) * LANES)
            s = jnp.dot(wsp_ref[g], vn_ref[rows, cols].astype(BF16), preferred_element_type=F32) + bsp_ref[g]
            mrg[rows, cols] = a_ref[rows, cols] * s
        kblk = kcat[pl.ds(off, 2 * WINDOW), :]
        vblk = vcat[pl.ds(off, 2 * WINDOW), :]
        kswp = pltpu.roll(kblk, HEAD_DIM, 1)
        vswp = pltpu.roll(vblk, HEAD_DIM, 1)
        kmin = jnp.where(jnp.logical_and(i == 0, j == 0), WINDOW, 0)
        allowed = band & (ck >= kmin)
        for kk in range(N_KV_HEADS):
            lo_src, hi_src = (kblk, kswp) if kk == 0 else (kswp, kblk)
            kbd = jnp.concatenate([jnp.where(lane_kv < HEAD_DIM, lo_src, 0.0),
                                   jnp.where(lane_kv >= HEAD_DIM, hi_src, 0.0)], axis=0).astype(BF16)
            lo_src, hi_src = (vblk, vswp) if kk == 0 else (vswp, vblk)
            vbd = jnp.concatenate([jnp.where(lane_kv < HEAD_DIM, lo_src, 0.0),
                                   jnp.where(lane_kv >= HEAD_DIM, hi_src, 0.0)], axis=0).astype(BF16)
            pair0 = kk * (GQA_GROUP // 2)
            qs = jnp.concatenate([q_ref[rows, (pair0 + p) * LANES:(pair0 + p + 1) * LANES]
                                  for p in range(GQA_GROUP // 2)], axis=0)
            lg = lax.dot_general(qs, kbd, (((1,), (1,)), ((), ())), preferred_element_type=F32)
            lg = jnp.where(allowed, lg, NEG_INF)
            h0 = kk * GQA_GROUP
            se = jnp.full((pair_rows, 1), sinks_ref[h0], F32)
            so = jnp.full((pair_rows, 1), sinks_ref[h0 + 1], F32)
            for p in range(1, GQA_GROUP // 2):
                se = jnp.where(row_p == p, sinks_ref[h0 + 2 * p], se)
                so = jnp.where(row_p == p, sinks_ref[h0 + 2 * p + 1], so)
            le, lo = lg[:, :2 * WINDOW], lg[:, 2 * WINDOW:]
            me = jnp.maximum(jnp.max(le, axis=1, keepdims=True), se)
            mo = jnp.maximum(jnp.max(lo, axis=1, keepdims=True), so)
            pe = jnp.exp(le - me)
            po = jnp.exp(lo - mo)
            de = jnp.sum(pe, axis=1, keepdims=True) + jnp.exp(se - me)
            do = jnp.sum(po, axis=1, keepdims=True) + jnp.exp(so - mo)
            pr = jnp.concatenate([pe, po], axis=1).astype(BF16)
            o = jnp.dot(pr, vbd, preferred_element_type=F32)
            o = o / jnp.where(lane_o < HEAD_DIM, de, do)
            for p in range(GQA_GROUP // 2):
                cols = slice((pair0 + p) * LANES, (pair0 + p + 1) * LANES)
                mrg[rows, cols] += sgb_ref[rows, cols] * o[p * WINDOW:(p + 1) * WINDOW]
        return carry

    lax.fori_loop(0, nsub, sub, 0)
    _finish_rows(mrg[...].astype(BF16), x_ref[...], wout_ref, g2_ref, wr_ref, br_ref, xmid_ref, xn2_ref, lg_ref)


def _mix(sinks, q, k, v, a, vn, sgb, x, wsp, bsp, wout, g2, wr, br):
    n = x.shape[0]
    tm = MIX_ROWS
    nsub = tm // WINDOW
    row = lambda w: pl.BlockSpec((tm, w), lambda i: (i, 0))
    prev = pl.BlockSpec((WINDOW, KV_WIDTH), lambda i: (jnp.maximum(i * nsub - 1, 0), 0))
    full = lambda arr: pl.BlockSpec(arr.shape, lambda i: (0,) * arr.ndim)
    smem = pl.BlockSpec(memory_space=pltpu.SMEM)
    return pl.pallas_call(
        _mix_body,
        grid=(n // tm,),
        in_specs=[smem, row(Q_WIDTH), row(KV_WIDTH), prev, row(KV_WIDTH), prev,
                  row(GMLP_WIDTH), row(GMLP_WIDTH), row(D_MODEL), row(D_MODEL),
                  full(wsp), full(bsp), full(wout), full(g2), full(wr), full(br)],
        out_specs=[row(D_MODEL), row(D_MODEL // 2), row(N_EXPERTS)],
        out_shape=[jax.ShapeDtypeStruct((n, D_MODEL), F32),
                   jax.ShapeDtypeStruct((n, D_MODEL // 2), jnp.uint32),
                   jax.ShapeDtypeStruct((n, N_EXPERTS), F32)],
        scratch_shapes=[pltpu.VMEM((tm + WINDOW, KV_WIDTH), F32),
                        pltpu.VMEM((tm + WINDOW, KV_WIDTH), F32),
                        pltpu.VMEM((tm, D_MODEL), F32)],
        compiler_params=_params(("arbitrary",)),
        name="mix_prompt",
    )(sinks, q, k, k, v, v, a, vn, sgb, x, wsp, bsp, wout, g2, wr, br)


def _sample_attn_body(sink_ref, q_ref, k_ref, v_ref, o_ref):
    q = q_ref[0]
    k = k_ref[0]
    v = v_ref[0]
    nq, nk = q.shape[1], k.shape[1]
    lg = jnp.einsum("bqd,bkd->bqk", q, k, preferred_element_type=F32)
    t = lax.broadcasted_iota(jnp.int32, (1, nq, nk), 1) // GQA_GROUP
    j = lax.broadcasted_iota(jnp.int32, (1, nq, nk), 2)
    lg = jnp.where((j > t) & (j <= t + WINDOW), lg, NEG_INF)
    sink = sink_ref[0][None]
    m = jnp.maximum(jnp.max(lg, axis=2, keepdims=True), sink)
    p = jnp.exp(lg - m)
    den = jnp.sum(p, axis=2, keepdims=True) + jnp.exp(sink - m)
    o = jnp.einsum("bqk,bkd->bqd", p.astype(BF16), v, preferred_element_type=F32)
    o_ref[0] = o / den


def _sample_attn(sink_rows, qh, kh, vh):
    nb = qh.shape[1]
    bb = 32
    blk = lambda a: pl.BlockSpec((1, bb) + a.shape[2:], lambda kk, b: (kk, b, 0, 0))
    return pl.pallas_call(
        _sample_attn_body,
        grid=(N_KV_HEADS, nb // bb),
        in_specs=[pl.BlockSpec((1,) + sink_rows.shape[1:], lambda kk, b: (kk, 0, 0)), blk(qh), blk(kh), blk(vh)],
        out_specs=pl.BlockSpec((1, bb) + qh.shape[2:], lambda kk, b: (kk, b, 0, 0)),
        out_shape=jax.ShapeDtypeStruct(qh.shape, F32),
        compiler_params=_params(("arbitrary", "arbitrary")),
        name="attn_sample",
    )(sink_rows, qh, kh, vh)


def _mix_sample_body(a_ref, vn_ref, sgb_ref, o_ref, x_ref, coef_ref, bias_ref,
                     wout_ref, g2_ref, wr_ref, br_ref, xmid_ref, xn2_ref, lg_ref):
    vn = vn_ref[...]
    n, width = vn.shape
    rows8 = lambda t: t.reshape(n // 8, 8, width)
    s = bias_ref[...][None] + coef_ref[0][None] * rows8(vn)
    for d in range(1, coef_ref.shape[0]):
        s = s + coef_ref[d][None] * rows8(pltpu.roll(vn, d, 0))
    merged = a_ref[...] * s.reshape(n, width) + sgb_ref[...] * o_ref[...]
    _finish_rows(merged.astype(BF16), x_ref[...], wout_ref, g2_ref, wr_ref, br_ref, xmid_ref, xn2_ref, lg_ref)


def _mix_sample(a, vn, sgb, o, x, coef, bias, wout, g2, wr, br):
    n = x.shape[0]
    args = (a, vn, sgb, o, x, coef, bias, wout, g2, wr, br)
    full = lambda arr: pl.BlockSpec(arr.shape, lambda i: (0,) * arr.ndim)
    return pl.pallas_call(
        _mix_sample_body,
        grid=(1,),
        in_specs=[full(arr) for arr in args],
        out_specs=[pl.BlockSpec((n, D_MODEL), lambda i: (0, 0)), pl.BlockSpec((n, D_MODEL // 2), lambda i: (0, 0)),
                   pl.BlockSpec((n, N_EXPERTS), lambda i: (0, 0))],
        out_shape=[jax.ShapeDtypeStruct((n, D_MODEL), F32),
                   jax.ShapeDtypeStruct((n, D_MODEL // 2), jnp.uint32),
                   jax.ShapeDtypeStruct((n, N_EXPERTS), F32)],
        compiler_params=_params(("arbitrary",)),
        name="mix_sample",
    )(*args)


def _route_body(nblk_p, lgp_ref, lgs_ref, idx_ref, gate_ref, rank_ref, cnt_ref, base):
    i = pl.program_id(0)

    @pl.when(i == 0)
    def _():
        base[...] = jnp.zeros_like(base)

    l = jnp.where(jnp.full(lgp_ref.shape, i, jnp.int32) < nblk_p, lgp_ref[...], lgs_ref[...])
    tb = l.shape[0]
    lane = lax.broadcasted_iota(jnp.int32, l.shape, 1).astype(F32)
    vals, idxs, sels = [], [], []
    for _ in range(TOP_K):
        m = jnp.max(l, axis=1, keepdims=True)
        ik = jnp.min(jnp.where(l == m, lane, float(N_EXPERTS)), axis=1, keepdims=True)
        sel = lane == ik
        l = jnp.where(sel, -jnp.inf, l)
        vals.append(m)
        idxs.append(ik)
        sels.append(sel)
    es = [jnp.exp(vk - vals[0]) for vk in vals]
    den = es[0] + es[1] + es[2] + es[3]
    onehot = jnp.zeros(l.shape, F32)
    for sel in sels:
        onehot = onehot + sel.astype(F32)
    tri = (lax.broadcasted_iota(jnp.int32, (tb, tb), 0) > lax.broadcasted_iota(jnp.int32, (tb, tb), 1))
    before = jnp.dot(tri.astype(BF16), onehot.astype(BF16), preferred_element_type=F32) + base[...]
    ranks = [jnp.sum(jnp.where(sel, before, 0.0), axis=1, keepdims=True) for sel in sels]
    base[...] += jnp.sum(onehot, axis=0, keepdims=True)
    cnt_ref[...] = base[...].astype(jnp.int32)

    lane_o = lax.broadcasted_iota(jnp.int32, (tb, LANES), 1)

    def spread(cols, dtype):
        out = jnp.zeros((tb, LANES), dtype)
        for kx, col in enumerate(cols):
            out = jnp.where(lane_o == kx, col.astype(dtype), out)
        return out

    idx_ref[...] = spread(idxs, jnp.int32)
    gate_ref[...] = spread([e / den for e in es], F32)
    rank_ref[...] = spread(ranks, jnp.int32)


def _route(logits_p, logits_s):
    tb = ROUTE_ROWS
    nblk_p, nblk_s = logits_p.shape[0] // tb, logits_s.shape[0] // tb
    n = (nblk_p + nblk_s) * tb
    assert n == logits_p.shape[0] + logits_s.shape[0]
    wide = pl.BlockSpec((tb, LANES), lambda i: (i, 0))
    return pl.pallas_call(
        functools.partial(_route_body, nblk_p),
        grid=(n // tb,),
        in_specs=[pl.BlockSpec((tb, N_EXPERTS), lambda i: (jnp.minimum(i, nblk_p - 1), 0)),
                  pl.BlockSpec((tb, N_EXPERTS), lambda i: (jnp.maximum(i - nblk_p, 0), 0))],
        out_specs=[wide, wide, wide, pl.BlockSpec((1, N_EXPERTS), lambda i: (0, 0))],
        out_shape=[jax.ShapeDtypeStruct((n, LANES), jnp.int32),
                   jax.ShapeDtypeStruct((n, LANES), F32),
                   jax.ShapeDtypeStruct((n, LANES), jnp.int32),
                   jax.ShapeDtypeStruct((1, N_EXPERTS), jnp.int32)],
        scratch_shapes=[pltpu.VMEM((1, N_EXPERTS), F32)],
        compiler_params=_params(("arbitrary",)),
        name="route",
    )(logits_p, logits_s)


def _slots_body(idx_ref, rank_ref, pstart_ref, dest_ref):
    idx = idx_ref[...]
    rank = rank_ref[...]
    tb = idx.shape[0]
    lane_e = lax.broadcasted_iota(jnp.int32, (tb, N_EXPERTS), 1)
    lane_o = lax.broadcasted_iota(jnp.int32, (tb, LANES), 1)
    pstart = pstart_ref[...]
    dest = jnp.zeros((tb, LANES), F32)
    for kx in range(TOP_K):
        seg = jnp.sum(jnp.where(lane_e == idx[:, kx:kx + 1], pstart, 0.0), axis=1, keepdims=True)
        dest = jnp.where(lane_o == kx, seg + rank[:, kx:kx + 1].astype(F32), dest)
    dest_ref[...] = dest.T[:dest_ref.shape[0]].astype(jnp.int32)


def _slots(idx_w, rank_w, pstart):
    n = idx_w.shape[0]
    tb = SLOT_ROWS
    assert n % tb == 0
    wide = pl.BlockSpec((tb, LANES), lambda i: (i, 0))
    return pl.pallas_call(
        _slots_body,
        grid=(n // tb,),
        in_specs=[wide, wide, pl.BlockSpec((1, N_EXPERTS), lambda i: (0, 0))],
        out_specs=pl.BlockSpec((8, tb), lambda i: (0, i)),
        out_shape=jax.ShapeDtypeStruct((8, n), jnp.int32),
        compiler_params=_params(("arbitrary",)),
        name="slots",
    )(idx_w, rank_w, pstart.astype(F32).reshape(1, N_EXPERTS))


def _sc_mesh():
    return plsc.VectorSubcoreMesh(core_axis_name="c", subcore_axis_name="s")


def _sc_worker():
    return lax.axis_index("s") * SC_CORES + lax.axis_index("c")


def _sc_dispatch(x_p, x_s, dest_t, n_slots):
    chunk = SC_DISPATCH_ROWS
    n_p, n_s = x_p.shape[0], x_s.shape[0]
    nch_p, nch_s = n_p // (SC_WORKERS * chunk), n_s // (SC_WORKERS * chunk)
    assert nch_p * SC_WORKERS * chunk == n_p and nch_s * SC_WORKERS * chunk == n_s
    d3 = dest_t.reshape(dest_t.shape[0], (n_p + n_s) // chunk, chunk)
    width, dtype = x_p.shape[1], x_p.dtype

    @functools.partial(
        pl.kernel, mesh=_sc_mesh(),
        out_type=jax.ShapeDtypeStruct((n_slots, width), dtype),
        scratch_types=[pltpu.VMEM((TOP_K, nch_p, chunk), jnp.int32),
                       pltpu.VMEM((TOP_K, nch_s, chunk), jnp.int32),
                       pltpu.VMEM((chunk, width), dtype),
                       pltpu.SemaphoreType.DMA],
        compiler_params=pltpu.CompilerParams(use_tc_tiling_on_sc=True),
        name="dispatch")
    def run(xp_hbm, xs_hbm, d_hbm, out_hbm, ip_v, is_v, rows_v, sem):
        wid = _sc_worker()
        pltpu.sync_copy(d_hbm.at[pl.ds(0, TOP_K), pl.ds(wid * nch_p, nch_p)], ip_v)
        pltpu.sync_copy(d_hbm.at[pl.ds(0, TOP_K), pl.ds(n_p // chunk + wid * nch_s, nch_s)], is_v)

        def group(x_hbm, idx_v, nch):
            def body(j, carry):
                pltpu.sync_copy(x_hbm.at[pl.ds((wid * nch + j) * chunk, chunk)], rows_v)
                copies = [pltpu.async_copy(rows_v, out_hbm.at[idx_v.at[kx, j]], sem) for kx in range(TOP_K)]
                for cp in copies:
                    cp.wait()
                return carry

            lax.fori_loop(0, nch, body, 0)

        group(xp_hbm, ip_v, nch_p)
        group(xs_hbm, is_v, nch_s)

    return run(x_p, x_s, d3)


def _sc_collect(y_sorted, dest_t, n_p, n_s):
    chunk = SC_COLLECT_ROWS
    per_choice = SC_WORKERS // TOP_K
    nch_p, nch_s = n_p // (per_choice * chunk), n_s // (per_choice * chunk)
    assert nch_p * per_choice * chunk == n_p and nch_s * per_choice * chunk == n_s
    d3 = dest_t.reshape(dest_t.shape[0], (n_p + n_s) // chunk, chunk)
    width, dtype = y_sorted.shape[1], y_sorted.dtype

    @functools.partial(
        pl.kernel, mesh=_sc_mesh(),
        out_type=[jax.ShapeDtypeStruct((TOP_K * n_p, width), dtype), jax.ShapeDtypeStruct((TOP_K * n_s, width), dtype)],
        scratch_types=[pltpu.VMEM((nch_p, chunk), jnp.int32),
                       pltpu.VMEM((nch_s, chunk), jnp.int32),
                       pltpu.VMEM((chunk, width), dtype),
                       pltpu.SemaphoreType.DMA],
        compiler_params=pltpu.CompilerParams(use_tc_tiling_on_sc=True),
        name="collect")
    def run(y_hbm, d_hbm, op_hbm, os_hbm, ip_v, is_v, rows_v, sem):
        wid = _sc_worker()
        choice = wid // per_choice
        part = wid % per_choice
        pltpu.sync_copy(d_hbm.at[choice, pl.ds(part * nch_p, nch_p)], ip_v)
        pltpu.sync_copy(d_hbm.at[choice, pl.ds(n_p // chunk + part * nch_s, nch_s)], is_v)

        def group(idx_v, o_hbm, nch):
            def body(j, carry):
                pltpu.async_copy(y_hbm.at[idx_v.at[j]], rows_v, sem).wait()
                pltpu.sync_copy(rows_v, o_hbm.at[pl.ds((wid * nch + j) * chunk, chunk)])
                return carry

            lax.fori_loop(0, nch, body, 0)

        group(ip_v, op_hbm, nch_p)
        group(is_v, os_hbm, nch_s)

    return run(y_sorted, d3)


def _expert_body(blk0_ref, nblk_ref, cnt_ref, wup_ref, wdn_ref, bup_ref, bdn_ref, x_hbm, y_hbm,
                 wup_s, wdn_s, xbuf, obuf, in_sem, out_sem):
    e = pl.program_id(0)
    nb = nblk_ref[e]
    blk0 = blk0_ref[e]
    cnt = cnt_ref[e]
    tm = EXPERT_ROWS
    pair = 2 * LANES

    def x_copy(i, slot):
        rows = pl.ds(pl.multiple_of((blk0 + i) * tm, tm), tm)
        return pltpu.make_async_copy(x_hbm.at[rows], xbuf.at[slot], in_sem.at[slot])

    def y_copy(i, slot):
        rows = pl.ds(pl.multiple_of((blk0 + i) * tm, tm), tm)
        return pltpu.make_async_copy(obuf.at[slot], y_hbm.at[rows], out_sem.at[slot])

    @pl.when(nb > 0)
    def _():
        x_copy(0, 0).start(priority=1)
        r = lax.broadcasted_iota(jnp.int32, (pair, pair), 0)
        c = lax.broadcasted_iota(jnp.int32, (pair, pair), 1)
        perm = (r == jnp.where(c < LANES, 2 * c, 2 * (c - LANES) + 1)).astype(BF16)
        for g in range(2 * D_MODEL // pair):
            cols = slice(g * pair, (g + 1) * pair)
            wup_s[g] = jnp.dot(wup_ref[0, :, cols].astype(BF16), perm, preferred_element_type=F32).astype(BF16)
        for g in range(D_MODEL // pair):
            wdn_s[g] = wdn_ref[0, :, g * pair:(g + 1) * pair].astype(BF16)

        def block(i, carry):
            slot = i % 2
            x_copy(i, slot).wait()

            @pl.when(i + 1 < nb)
            def _():
                x_copy(i + 1, 1 - slot).start(priority=1)

            @pl.when(i >= 2)
            def _():
                y_copy(i - 2, slot).wait()

            row = lax.broadcasted_iota(jnp.int32, (tm, 1), 0)
            x = jnp.where(row < cnt - i * tm, _unpack_bf16_pair(xbuf[slot]), 0.0).astype(BF16)
            acts = []
            for g in range(2 * D_MODEL // pair):
                cols = slice(g * pair, (g + 1) * pair)
                h = jnp.dot(x, wup_s[g], preferred_element_type=F32) + bup_ref[0, :, cols]
                glu = jnp.minimum(h[:, :LANES], SWIGLU_LIMIT)
                lin = jnp.clip(h[:, LANES:], -SWIGLU_LIMIT, SWIGLU_LIMIT)
                acts.append((glu * jax.nn.sigmoid(SWIGLU_ALPHA * glu) * (lin + 1.0)).astype(BF16))
            act = jnp.concatenate(acts, axis=1)
            half_groups = D_MODEL // pair // 2
            for g in range(half_groups):
                ys = []
                for gg in (g, g + half_groups):
                    cols = slice(gg * pair, (gg + 1) * pair)
                    ys.append(jnp.dot(act, wdn_s[gg], preferred_element_type=F32) + bdn_ref[0, :, cols])
                obuf[slot, :, g * pair:(g + 1) * pair] = _pack_bf16_pair(ys[0], ys[1])
            y_copy(i, slot).start(priority=1)
            return carry

        lax.fori_loop(0, nb, block, 0)

        @pl.when(nb >= 2)
        def _():
            y_copy(nb - 2, nb % 2).wait()

        y_copy(nb - 1, (nb - 1) % 2).wait()


def _experts(blk0, nblk, cnt, x_sorted, w_up, w_down, b_up_grouped, b_down):
    tm = EXPERT_ROWS
    per_expert = lambda a: pl.BlockSpec((1,) + a.shape[1:], lambda e, b0, nb, ct: (e, 0, 0))
    grid_spec = pltpu.PrefetchScalarGridSpec(
        num_scalar_prefetch=3,
        grid=(N_EXPERTS,),
        in_specs=[per_expert(w_up), per_expert(w_down), per_expert(b_up_grouped), per_expert(b_down),
                  pl.BlockSpec(memory_space=pl.ANY)],
        out_specs=pl.BlockSpec(memory_space=pl.ANY),
        scratch_shapes=[pltpu.VMEM((2 * D_MODEL // (2 * LANES), D_MODEL, 2 * LANES), BF16),
                        pltpu.VMEM((D_MODEL // (2 * LANES), D_MODEL, 2 * LANES), BF16),
                        pltpu.VMEM((2, tm, x_sorted.shape[1]), x_sorted.dtype),
                        pltpu.VMEM((2, tm, D_MODEL // 2), jnp.uint32),
                        pltpu.SemaphoreType.DMA((2,)), pltpu.SemaphoreType.DMA((2,))],
    )
    return pl.pallas_call(
        _expert_body,
        grid_spec=grid_spec,
        out_shape=jax.ShapeDtypeStruct((x_sorted.shape[0], D_MODEL // 2), jnp.uint32),
        compiler_params=_params(("arbitrary",)),
        name="experts",
    )(blk0, nblk, cnt, w_up, w_down, b_up_grouped, b_down, x_sorted)


def _combine_body(gate_ref, xmid_ref, gfin_ref, y0_ref, y1_ref, y2_ref, y3_ref, out_ref):
    gate = gate_ref[...]
    moe = _unpack_bf16_pair(y0_ref[...]) * gate[:, 0:1]
    for kx, y_ref in enumerate((y1_ref, y2_ref, y3_ref), start=1):
        moe = moe + _unpack_bf16_pair(y_ref[...]) * gate[:, kx:kx + 1]
    out_ref[...] = _rms(xmid_ref[...] + moe, gfin_ref[...])


def _combine(gates, first_token, xmid, gfin, y_rows):
    n = xmid.shape[0]
    tt = COMBINE_ROWS
    nblk = n // tt
    blk0 = first_token // tt
    assert blk0 * tt == first_token
    choice = lambda kx: pl.BlockSpec((tt, y_rows.shape[1]), lambda i: (i + kx * nblk, 0))
    return pl.pallas_call(
        _combine_body,
        grid=(nblk,),
        in_specs=[pl.BlockSpec((tt, LANES), lambda i: (i + blk0, 0)),
                  pl.BlockSpec((tt, D_MODEL), lambda i: (i, 0)),
                  pl.BlockSpec((1, D_MODEL), lambda i: (0, 0))] + [choice(kx) for kx in range(TOP_K)],
        out_specs=pl.BlockSpec((tt, D_MODEL), lambda i: (i, 0)),
        out_shape=jax.ShapeDtypeStruct((n, D_MODEL), F32),
        compiler_params=_params(("arbitrary",)),
        name="combine",
    )(gates, xmid, gfin, y_rows, y_rows, y_rows, y_rows)


def kernel(x_prompt, x_sample, cache_k_win, cache_v_win, norm_attn_g, w_in, ln_v_g, ln_v_b, w_spatial, b_spatial,
           attn_sinks, w_out, norm_ffn_g, w_router, b_router, w_up, b_up, w_down, b_down, norm_final_g):
    bp, tp, _ = x_prompt.shape
    bs, ts, _ = x_sample.shape
    w_buf = cache_k_win.shape[2]
    assert bp == 1 and tp % MIX_ROWS == 0 and w_buf == WINDOW and (bs * ts) % PROJ_ROWS == 0 and 8 % ts == 0
    n_p, n_s = bp * tp, bs * ts
    row2 = lambda a: a.reshape(1, -1)

    w_in_bf = w_in[0].astype(BF16)
    w_out_bf = w_out[0].astype(BF16)
    tril = jnp.tril(jnp.ones((CHUNK, CHUNK), dtype=bool))
    wsp = jnp.where(tril[None], w_spatial[0], 0.0)
    wsp_bf = wsp.astype(BF16)
    bsp = jnp.broadcast_to(b_spatial[0][:, :, None], (GMLP_GROUPS, CHUNK, LANES))
    b_up_grouped = b_up[0].reshape(N_EXPERTS, -1, LANES, 2).transpose(0, 1, 3, 2).reshape(N_EXPERTS, 1, -1)
    bd = b_down[0][:, None, :]
    g1, g2, gfin = row2(norm_attn_g[0]), row2(norm_ffn_g[0]), row2(norm_final_g)
    lng, lnb = row2(ln_v_g[0]), row2(ln_v_b[0])
    wr_hi = w_router[0].astype(BF16)
    wr = jnp.concatenate([wr_hi, (w_router[0] - wr_hi.astype(F32)).astype(BF16)], axis=1)
    br = row2(b_router[0])
    sinks = attn_sinks[0]

    xp = x_prompt.reshape(n_p, D_MODEL)
    cs_p = _rotary_inputs(jnp.arange(tp, dtype=jnp.int32))
    q_p, k_p, v_p, a_p, vn_p, sgb_p = _proj(xp, g1, w_in_bf, cs_p, lng, lnb)
    xmid_p, xn2_p, lg_p = _mix(sinks, q_p, k_p, v_p, a_p, vn_p, sgb_p, xp, wsp_bf, bsp, w_out_bf, g2, wr, br)

    xs = x_sample.reshape(n_s, D_MODEL)
    pos_s = PAST_LEN + jnp.arange(ts, dtype=jnp.int32)
    cs_s = _rotary_inputs(jnp.tile(pos_s, bs))
    q_s, k_s, v_s, a_s, vn_s, sgb_s = _proj(xs, g1, w_in_bf, cs_s, lng, lnb)
    k_all = jnp.concatenate([cache_k_win[0], k_s.reshape(bs, ts, N_KV_HEADS, HEAD_DIM)], axis=1)
    v_all = jnp.concatenate([cache_v_win[0], v_s.reshape(bs, ts, N_KV_HEADS, HEAD_DIM)], axis=1)
    n_keys = w_buf + ts
    key_pad = (-n_keys) % 8
    to_heads = lambda t: jnp.pad(t, ((0, 0), (0, key_pad), (0, 0), (0, 0))).transpose(2, 0, 1, 3).astype(BF16)
    qh = q_s.reshape(bs, ts, N_KV_HEADS, GQA_GROUP, HEAD_DIM).transpose(2, 0, 1, 3, 4)
    qh = qh.reshape(N_KV_HEADS, bs, ts * GQA_GROUP, HEAD_DIM)
    sink_rows = jnp.tile(sinks.reshape(N_KV_HEADS, 1, GQA_GROUP), (1, ts, 1)).reshape(N_KV_HEADS, ts * GQA_GROUP, 1)
    oh = _sample_attn(sink_rows, qh, to_heads(k_all), to_heads(v_all))
    o_s = oh.reshape(N_KV_HEADS, bs, ts, GQA_GROUP, HEAD_DIM).transpose(1, 2, 0, 3, 4).reshape(n_s, Q_WIDTH)
    t_idx = jnp.arange(ts)
    coef = jnp.stack([jnp.where((t_idx >= d)[None, :], wsp[:, t_idx, jnp.maximum(t_idx - d, 0)], 0.0)
                      for d in range(ts)])
    coef = jnp.repeat(coef.transpose(0, 2, 1), GMLP_WIDTH // GMLP_GROUPS, axis=2)
    coef = jnp.tile(coef, (1, 8 // ts, 1))
    bias = jnp.tile(jnp.repeat(b_spatial[0][:, :ts].T, GMLP_WIDTH // GMLP_GROUPS, axis=1), (8 // ts, 1))
    xmid_s, xn2_s, lg_s = _mix_sample(a_s, vn_s, sgb_s, o_s, xs, coef, bias, w_out_bf, g2, wr, br)

    n_tok = n_p + n_s
    idx_w, gate_w, rank_w, counts = _route(lg_p, lg_s)
    counts = counts[0]
    tm = EXPERT_ROWS
    padded = (counts + tm - 1) // tm * tm
    pstart = jnp.cumsum(padded) - padded
    dest_t = _slots(idx_w, rank_w, pstart)
    n_blocks = (n_tok * TOP_K) // tm + N_EXPERTS

    x_sorted = _sc_dispatch(xn2_p, xn2_s, dest_t, n_blocks * tm)
    y_sorted = _experts(pstart // tm, padded // tm, counts, x_sorted, w_up[0], w_down[0], b_up_grouped, bd)
    yrows_p, yrows_s = _sc_collect(y_sorted, dest_t, n_p, n_s)
    y_p = _combine(gate_w, 0, xmid_p, gfin, yrows_p)
    y_s = _combine(gate_w, n_p, xmid_s, gfin, yrows_s)

    k4 = lambda t: t.reshape(1, bp, -1, N_KV_HEADS, HEAD_DIM)
    return (y_p.reshape(bp, tp, D_MODEL),
            y_s.reshape(bs, ts, D_MODEL),
            k4(k_p[n_p - WINDOW:]),
            k4(v_p[n_p - WINDOW:]),
            vn_p[n_p - CHUNK:].reshape(1, bp, CHUNK, GMLP_WIDTH),
            k_all[None, :, ts:],
            v_all[None, :, ts:],
            vn_s.reshape(1, bs, ts, GMLP_WIDTH))
```

```python
import functools

import numpy as np
import jax
import jax.numpy as jnp
from jax import lax
from jax.experimental import pallas as pl
from jax.experimental.pallas import tpu as pltpu
from jax.experimental.pallas import tpu_sc as plsc

F32 = jnp.float32
BF16 = jnp.bfloat16

D_MODEL = 1024
HEAD_DIM = 64
N_HEADS = 16
GQA_GROUP = 8
N_KV_HEADS = 2
Q_WIDTH = 1024
KV_WIDTH = 128
WINDOW = 128
ROT_DIM = 16
ROPE_THETA = 500000.0
CHUNK = 128
GMLP_WIDTH = 1024
GMLP_GROUPS = 8
N_EXPERTS = 32
TOP_K = 4
SWIGLU_LIMIT = 7.0
SWIGLU_ALPHA = 1.702
RMS_EPS = 1e-5
LN_EPS = 1e-5
NEG_INF = -1e30
PAST_LEN = 16384

LANES = 128
VMEM_LIMIT = 56 * 1024 * 1024

PROJ_ROWS = 256
MIX_ROWS = 512
ROUTE_ROWS = 512
SLOT_ROWS = 1536
EXPERT_ROWS = 256
COMBINE_ROWS = 256

SC_CORES = 2
SC_WORKERS = 32
SC_DISPATCH_ROWS = 16
SC_COLLECT_ROWS = 32

_C_Q, _C_KV, _C_U, _C_VG, _C_GA, _C_GB, _C_END = 0, 1024, 1280, 2304, 3328, 4352, 5376


def _params(sem):
    return pltpu.CompilerParams(dimension_semantics=sem, vmem_limit_bytes=VMEM_LIMIT)


def _rms(x, g):
    return x * lax.rsqrt(jnp.mean(x * x, axis=-1, keepdims=True) + RMS_EPS) * g


def _pack_bf16_pair(lo, hi):
    lo_bits = lax.bitcast_convert_type(lo.astype(BF16).astype(F32), jnp.uint32)
    hi_bits = lax.bitcast_convert_type(hi.astype(BF16).astype(F32), jnp.uint32)
    return (lo_bits >> 16) | hi_bits


def _unpack_bf16_pair(words):
    lo = lax.bitcast_convert_type(words << 16, F32)
    hi = lax.bitcast_convert_type(words & jnp.uint32(0xFFFF0000), F32)
    return jnp.concatenate([lo, hi], axis=1)


def _proj_body(x_ref, g_ref, w_ref, cs_ref, rot_ref, lng_ref, lnb_ref,
               q_ref, k_ref, v_ref, a_ref, vn_ref, sgb_ref):
    h = _rms(x_ref[...], g_ref[...]).astype(BF16)
    tabs = lax.dot_general(cs_ref[...], rot_ref[...], (((0,), (0,)), ((), ())),
                           preferred_element_type=F32, precision=lax.Precision.HIGHEST)
    rc, rs1, rs2 = tabs[:, :LANES], tabs[:, LANES:2 * LANES], tabs[:, 2 * LANES:]

    def rot(z):
        return z * rc + pltpu.roll(z, LANES - ROT_DIM // 2, 1) * rs1 + pltpu.roll(z, ROT_DIM // 2, 1) * rs2

    def mm(lo, hi):
        return jnp.dot(h, w_ref[:, lo:hi], preferred_element_type=F32)

    zq = mm(_C_Q, _C_KV)
    for c in range(Q_WIDTH // LANES):
        sl = slice(c * LANES, (c + 1) * LANES)
        q_ref[:, sl] = (rot(zq[:, sl]) * (HEAD_DIM ** -0.5)).astype(BF16)
    zkv = mm(_C_KV, _C_U)
    k_ref[...] = rot(zkv[:, :KV_WIDTH])
    v_ref[...] = zkv[:, KV_WIDTH:]
    a_ref[...] = jax.nn.sigmoid(mm(_C_GA, _C_GB)) * jax.nn.gelu(mm(_C_U, _C_VG))
    zv = jax.nn.gelu(mm(_C_VG, _C_GA))
    zc = zv - jnp.mean(zv, axis=-1, keepdims=True)
    var = jnp.mean(zc * zc, axis=-1, keepdims=True)
    vn_ref[...] = zc * lax.rsqrt(var + LN_EPS) * lng_ref[...] + lnb_ref[...]
    sgb_ref[...] = jax.nn.sigmoid(mm(_C_GB, _C_END))


def _proj(x, norm_g, w_in_bf, cs, ln_g, ln_b):
    n = x.shape[0]
    tm = PROJ_ROWS
    row = lambda w: pl.BlockSpec((tm, w), lambda i: (i, 0))
    full = lambda a: pl.BlockSpec(a.shape, lambda i: (0,) * a.ndim)
    rot = jnp.asarray(_ROT_EXPAND)
    return pl.pallas_call(
        _proj_body,
        grid=(n // tm,),
        in_specs=[row(D_MODEL), full(norm_g), full(w_in_bf), pl.BlockSpec((cs.shape[0], tm), lambda i: (0, i)),
                  full(rot),
                  full(ln_g), full(ln_b)],
        out_specs=[row(Q_WIDTH), row(KV_WIDTH), row(KV_WIDTH), row(GMLP_WIDTH), row(GMLP_WIDTH), row(D_MODEL)],
        out_shape=[jax.ShapeDtypeStruct((n, Q_WIDTH), BF16),
                   jax.ShapeDtypeStruct((n, KV_WIDTH), F32),
                   jax.ShapeDtypeStruct((n, KV_WIDTH), F32),
                   jax.ShapeDtypeStruct((n, GMLP_WIDTH), F32),
                   jax.ShapeDtypeStruct((n, GMLP_WIDTH), F32),
                   jax.ShapeDtypeStruct((n, D_MODEL), F32)],
        compiler_params=_params(("arbitrary",)),
        name="proj",
    )(x, norm_g, w_in_bf, cs, rot, ln_g, ln_b)


_ROT_COLS = 32


def _rot_expand():
    half = ROT_DIM // 2
    m = np.zeros((_ROT_COLS, 3 * LANES), np.float32)
    for lane in range(LANES):
        d = lane % HEAD_DIM
        if d < ROT_DIM:
            m[d % half, lane] = 1.0
        else:
            m[2 * half, lane] = 1.0
        if d < half:
            m[half + d, LANES + lane] = -1.0
        elif d < ROT_DIM:
            m[half + d - half, 2 * LANES + lane] = 1.0
    return m


_ROT_EXPAND = _rot_expand()


def _rotary_inputs(pos):
    half = ROT_DIM // 2
    inv_freq = ROPE_THETA ** (-jnp.arange(half, dtype=F32) / half)
    ang = inv_freq[:, None] * pos.astype(F32)[None, :]
    n = pos.shape[0]
    return jnp.concatenate([jnp.cos(ang), jnp.sin(ang), jnp.ones((1, n), F32),
                            jnp.zeros((_ROT_COLS - 2 * half - 1, n), F32)], axis=0)


def _finish_rows(merged_bf, x, wout_ref, g2_ref, wr_ref, br_ref, xmid_ref, xn2_ref, lg_ref):
    xm = x + jnp.dot(merged_bf, wout_ref[...], preferred_element_type=F32)
    xmid_ref[...] = xm
    xn = _rms(xm, g2_ref[...])
    x_hi = xn.astype(BF16)
    x_lo = (xn - x_hi.astype(F32)).astype(BF16)
    w_hl = wr_ref[...]
    p_hi = jnp.dot(x_hi, w_hl, preferred_element_type=F32)
    p_lo = jnp.dot(x_lo, w_hl[:, :N_EXPERTS], preferred_element_type=F32)
    lg_ref[...] = p_hi[:, :N_EXPERTS] + (p_hi[:, N_EXPERTS:] + p_lo) + br_ref[...]
    xn2_ref[...] = _pack_bf16_pair(xn[:, :D_MODEL // 2], xn[:, D_MODEL // 2:])


def _mix_body(sinks_ref, q_ref, k_ref, kp_ref, v_ref, vp_ref, a_ref, vn_ref, sgb_ref, x_ref,
              wsp_ref, bsp_ref, wout_ref, g2_ref, wr_ref, br_ref,
              xmid_ref, xn2_ref, lg_ref, kcat, vcat, mrg):
    i = pl.program_id(0)
    nsub = MIX_ROWS // WINDOW
    kcat[0:WINDOW] = kp_ref[...]
    kcat[WINDOW:] = k_ref[...]
    vcat[0:WINDOW] = vp_ref[...]
    vcat[WINDOW:] = v_ref[...]

    pair_rows = (GQA_GROUP // 2) * WINDOW
    rq = lax.broadcasted_iota(jnp.int32, (pair_rows, 4 * WINDOW), 0) & (WINDOW - 1)
    ck = lax.broadcasted_iota(jnp.int32, (pair_rows, 4 * WINDOW), 1) & (2 * WINDOW - 1)
    band = (ck > rq) & (ck <= rq + WINDOW)
    lane_kv = lax.broadcasted_iota(jnp.int32, (2 * WINDOW, LANES), 1)
    lane_o = lax.broadcasted_iota(jnp.int32, (pair_rows, LANES), 1)
    row_p = lax.broadcasted_iota(jnp.int32, (pair_rows, 1), 0) >> 7

    def sub(j, carry):
        off = pl.multiple_of(j * WINDOW, WINDOW)
        rows = pl.ds(off, WINDOW)
        for g in range(GMLP_GROUPS):
            cols = slice(g * LANES, (g + 1) * LANES)
            s = jnp.dot(wsp_ref[g], vn_ref[rows, cols].astype(BF16), preferred_element_type=F32) + bsp_ref[g]
            mrg[rows, cols] = a_ref[rows, cols] * s
        kblk = kcat[pl.ds(off, 2 * WINDOW), :]
        vblk = vcat[pl.ds(off, 2 * WINDOW), :]
        kswp = pltpu.roll(kblk, HEAD_DIM, 1)
        vswp = pltpu.roll(vblk, HEAD_DIM, 1)
        kmin = jnp.where(jnp.logical_and(i == 0, j == 0), WINDOW, 0)
        allowed = band & (ck >= kmin)
        for kk in range(N_KV_HEADS):
            lo_src, hi_src = (kblk, kswp) if kk == 0 else (kswp, kblk)
            kbd = jnp.concatenate([jnp.where(lane_kv < HEAD_DIM, lo_src, 0.0),
                                   jnp.where(lane_kv >= HEAD_DIM, hi_src, 0.0)], axis=0).astype(BF16)
            lo_src, hi_src = (vblk, vswp) if kk == 0 else (vswp, vblk)
            vbd = jnp.concatenate([jnp.where(lane_kv < HEAD_DIM, lo_src, 0.0),
                                   jnp.where(lane_kv >= HEAD_DIM, hi_src, 0.0)], axis=0).astype(BF16)
            pair0 = kk * (GQA_GROUP // 2)
            qs = jnp.concatenate([q_ref[rows, (pair0 + p) * LANES:(pair0 + p + 1) * LANES]
                                  for p in range(GQA_GROUP // 2)], axis=0)
            lg = lax.dot_general(qs, kbd, (((1,), (1,)), ((), ())), preferred_element_type=F32)
            lg = jnp.where(allowed, lg, NEG_INF)
            h0 = kk * GQA_GROUP
            se = jnp.full((pair_rows, 1), sinks_ref[h0], F32)
            so = jnp.full((pair_rows, 1), sinks_ref[h0 + 1], F32)
            for p in range(1, GQA_GROUP // 2):
                se = jnp.where(row_p == p, sinks_ref[h0 + 2 * p], se)
                so = jnp.where(row_p == p, sinks_ref[h0 + 2 * p + 1], so)
            le, lo = lg[:, :2 * WINDOW], lg[:, 2 * WINDOW:]
            me = jnp.maximum(jnp.max(le, axis=1, keepdims=True), se)
            mo = jnp.maximum(jnp.max(lo, axis=1, keepdims=True), so)
            pe = jnp.exp(le - me)
            po = jnp.exp(lo - mo)
            de = jnp.sum(pe, axis=1, keepdims=True) + jnp.exp(se - me)
            do = jnp.sum(po, axis=1, keepdims=True) + jnp.exp(so - mo)
            pr = jnp.concatenate([pe, po], axis=1).astype(BF16)
            o = jnp.dot(pr, vbd, preferred_element_type=F32)
            o = o / jnp.where(lane_o < HEAD_DIM, de, do)
            for p in range(GQA_GROUP // 2):
                cols = slice((pair0 + p) * LANES, (pair0 + p + 1) * LANES)
                mrg[rows, cols] += sgb_ref[rows, cols] * o[p * WINDOW:(p + 1) * WINDOW]
        return carry

    lax.fori_loop(0, nsub, sub, 0)
    _finish_rows(mrg[...].astype(BF16), x_ref[...], wout_ref, g2_ref, wr_ref, br_ref, xmid_ref, xn2_ref, lg_ref)


def _mix(sinks, q, k, v, a, vn, sgb, x, wsp, bsp, wout, g2, wr, br):
    n = x.shape[0]
    tm = MIX_ROWS
    nsub = tm // WINDOW
    row = lambda w: pl.BlockSpec((tm, w), lambda i: (i, 0))
    prev = pl.BlockSpec((WINDOW, KV_WIDTH), lambda i: (jnp.maximum(i * nsub - 1, 0), 0))
    full = lambda arr: pl.BlockSpec(arr.shape, lambda i: (0,) * arr.ndim)
    smem = pl.BlockSpec(memory_space=pltpu.SMEM)
    return pl.pallas_call(
        _mix_body,
        grid=(n // tm,),
        in_specs=[smem, row(Q_WIDTH), row(KV_WIDTH), prev, row(KV_WIDTH), prev,
                  row(GMLP_WIDTH), row(GMLP_WIDTH), row(D_MODEL), row(D_MODEL),
                  full(wsp), full(bsp), full(wout), full(g2), full(wr), full(br)],
        out_specs=[row(D_MODEL), row(D_MODEL // 2), row(N_EXPERTS)],
        out_shape=[jax.ShapeDtypeStruct((n, D_MODEL), F32),
                   jax.ShapeDtypeStruct((n, D_MODEL // 2), jnp.uint32),
                   jax.ShapeDtypeStruct((n, N_EXPERTS), F32)],
        scratch_shapes=[pltpu.VMEM((tm + WINDOW, KV_WIDTH), F32),
                        pltpu.VMEM((tm + WINDOW, KV_WIDTH), F32),
                        pltpu.VMEM((tm, D_MODEL), F32)],
        compiler_params=_params(("arbitrary",)),
        name="mix_prompt",
    )(sinks, q, k, k, v, v, a, vn, sgb, x, wsp, bsp, wout, g2, wr, br)


def _sample_attn_body(sink_ref, q_ref, k_ref, v_ref, o_ref):
    q = q_ref[0]
    k = k_ref[0]
    v = v_ref[0]
    nq, nk = q.shape[1], k.shape[1]
    lg = jnp.einsum("bqd,bkd->bqk", q, k, preferred_element_type=F32)
    t = lax.broadcasted_iota(jnp.int32, (1, nq, nk), 1) // GQA_GROUP
    j = lax.broadcasted_iota(jnp.int32, (1, nq, nk), 2)
    lg = jnp.where((j > t) & (j <= t + WINDOW), lg, NEG_INF)
    sink = sink_ref[0][None]
    m = jnp.maximum(jnp.max(lg, axis=2, keepdims=True), sink)
    p = jnp.exp(lg - m)
    den = jnp.sum(p, axis=2, keepdims=True) + jnp.exp(sink - m)
    o = jnp.einsum("bqk,bkd->bqd", p.astype(BF16), v, preferred_element_type=F32)
    o_ref[0] = o / den


def _sample_attn(sink_rows, qh, kh, vh):
    nb = qh.shape[1]
    bb = 32
    blk = lambda a: pl.BlockSpec((1, bb) + a.shape[2:], lambda kk, b: (kk, b, 0, 0))
    return pl.pallas_call(
        _sample_attn_body,
        grid=(N_KV_HEADS, nb // bb),
        in_specs=[pl.BlockSpec((1,) + sink_rows.shape[1:], lambda kk, b: (kk, 0, 0)), blk(qh), blk(kh), blk(vh)],
        out_specs=pl.BlockSpec((1, bb) + qh.shape[2:], lambda kk, b: (kk, b, 0, 0)),
        out_shape=jax.ShapeDtypeStruct(qh.shape, F32),
        compiler_params=_params(("arbitrary", "arbitrary")),
        name="attn_sample",
    )(sink_rows, qh, kh, vh)


def _mix_sample_body(a_ref, vn_ref, sgb_ref, o_ref, x_ref, coef_ref, bias_ref,
                     wout_ref, g2_ref, wr_ref, br_ref, xmid_ref, xn2_ref, lg_ref):
    vn = vn_ref[...]
    n, width = vn.shape
    rows8 = lambda t: t.reshape(n // 8, 8, width)
    s = bias_ref[...][None] + coef_ref[0][None] * rows8(vn)
    for d in range(1, coef_ref.shape[0]):
        s = s + coef_ref[d][None] * rows8(pltpu.roll(vn, d, 0))
    merged = a_ref[...] * s.reshape(n, width) + sgb_ref[...] * o_ref[...]
    _finish_rows(merged.astype(BF16), x_ref[...], wout_ref, g2_ref, wr_ref, br_ref, xmid_ref, xn2_ref, lg_ref)


def _mix_sample(a, vn, sgb, o, x, coef, bias, wout, g2, wr, br):
    n = x.shape[0]
    args = (a, vn, sgb, o, x, coef, bias, wout, g2, wr, br)
    full = lambda arr: pl.BlockSpec(arr.shape, lambda i: (0,) * arr.ndim)
    return pl.pallas_call(
        _mix_sample_body,
        grid=(1,),
        in_specs=[full(arr) for arr in args],
        out_specs=[pl.BlockSpec((n, D_MODEL), lambda i: (0, 0)), pl.BlockSpec((n, D_MODEL // 2), lambda i: (0, 0)),
                   pl.BlockSpec((n, N_EXPERTS), lambda i: (0, 0))],
        out_shape=[jax.ShapeDtypeStruct((n, D_MODEL), F32),
                   jax.ShapeDtypeStruct((n, D_MODEL // 2), jnp.uint32),
                   jax.ShapeDtypeStruct((n, N_EXPERTS), F32)],
        compiler_params=_params(("arbitrary",)),
        name="mix_sample",
    )(*args)


def _route_body(nblk_p, lgp_ref, lgs_ref, idx_ref, gate_ref, rank_ref, cnt_ref, base):
    i = pl.program_id(0)

    @pl.when(i == 0)
    def _():
        base[...] = jnp.zeros_like(base)

    l = jnp.where(jnp.full(lgp_ref.shape, i, jnp.int32) < nblk_p, lgp_ref[...], lgs_ref[...])
    tb = l.shape[0]
    lane = lax.broadcasted_iota(jnp.int32, l.shape, 1).astype(F32)
    vals, idxs, sels = [], [], []
    for _ in range(TOP_K):
        m = jnp.max(l, axis=1, keepdims=True)
        ik = jnp.min(jnp.where(l == m, lane, float(N_EXPERTS)), axis=1, keepdims=True)
        sel = lane == ik
        l = jnp.where(sel, -jnp.inf, l)
        vals.append(m)
        idxs.append(ik)
        sels.append(sel)
    es = [jnp.exp(vk - vals[0]) for vk in vals]
    den = es[0] + es[1] + es[2] + es[3]
    onehot = jnp.zeros(l.shape, F32)
    for sel in sels:
        onehot = onehot + sel.astype(F32)
    tri = (lax.broadcasted_iota(jnp.int32, (tb, tb), 0) > lax.broadcasted_iota(jnp.int32, (tb, tb), 1))
    before = jnp.dot(tri.astype(BF16), onehot.astype(BF16), preferred_element_type=F32) + base[...]
    ranks = [jnp.sum(jnp.where(sel, before, 0.0), axis=1, keepdims=True) for sel in sels]
    base[...] += jnp.sum(onehot, axis=0, keepdims=True)
    cnt_ref[...] = base[...].astype(jnp.int32)

    lane_o = lax.broadcasted_iota(jnp.int32, (tb, LANES), 1)

    def spread(cols, dtype):
        out = jnp.zeros((tb, LANES), dtype)
        for kx, col in enumerate(cols):
            out = jnp.where(lane_o == kx, col.astype(dtype), out)
        return out

    idx_ref[...] = spread(idxs, jnp.int32)
    gate_ref[...] = spread([e / den for e in es], F32)
    rank_ref[...] = spread(ranks, jnp.int32)


def _route(logits_p, logits_s):
    tb = ROUTE_ROWS
    nblk_p, nblk_s = logits_p.shape[0] // tb, logits_s.shape[0] // tb
    n = (nblk_p + nblk_s) * tb
    assert n == logits_p.shape[0] + logits_s.shape[0]
    wide = pl.BlockSpec((tb, LANES), lambda i: (i, 0))
    return pl.pallas_call(
        functools.partial(_route_body, nblk_p),
        grid=(n // tb,),
        in_specs=[pl.BlockSpec((tb, N_EXPERTS), lambda i: (jnp.minimum(i, nblk_p - 1), 0)),
                  pl.BlockSpec((tb, N_EXPERTS), lambda i: (jnp.maximum(i - nblk_p, 0), 0))],
        out_specs=[wide, wide, wide, pl.BlockSpec((1, N_EXPERTS), lambda i: (0, 0))],
        out_shape=[jax.ShapeDtypeStruct((n, LANES), jnp.int32),
                   jax.ShapeDtypeStruct((n, LANES), F32),
                   jax.ShapeDtypeStruct((n, LANES), jnp.int32),
                   jax.ShapeDtypeStruct((1, N_EXPERTS), jnp.int32)],
        scratch_shapes=[pltpu.VMEM((1, N_EXPERTS), F32)],
        compiler_params=_params(("arbitrary",)),
        name="route",
    )(logits_p, logits_s)


def _slots_body(idx_ref, rank_ref, pstart_ref, dest_ref):
    idx = idx_ref[...]
    rank = rank_ref[...]
    tb = idx.shape[0]
    lane_e = lax.broadcasted_iota(jnp.int32, (tb, N_EXPERTS), 1)
    lane_o = lax.broadcasted_iota(jnp.int32, (tb, LANES), 1)
    pstart = pstart_ref[...]
    dest = jnp.zeros((tb, LANES), F32)
    for kx in range(TOP_K):
        seg = jnp.sum(jnp.where(lane_e == idx[:, kx:kx + 1], pstart, 0.0), axis=1, keepdims=True)
        dest = jnp.where(lane_o == kx, seg + rank[:, kx:kx + 1].astype(F32), dest)
    dest_ref[...] = dest.T[:dest_ref.shape[0]].astype(jnp.int32)


def _slots(idx_w, rank_w, pstart):
    n = idx_w.shape[0]
    tb = SLOT_ROWS
    assert n % tb == 0
    wide = pl.BlockSpec((tb, LANES), lambda i: (i, 0))
    return pl.pallas_call(
        _slots_body,
        grid=(n // tb,),
        in_specs=[wide, wide, pl.BlockSpec((1, N_EXPERTS), lambda i: (0, 0))],
        out_specs=pl.BlockSpec((8, tb), lambda i: (0, i)),
        out_shape=jax.ShapeDtypeStruct((8, n), jnp.int32),
        compiler_params=_params(("arbitrary",)),
        name="slots",
    )(idx_w, rank_w, pstart.astype(F32).reshape(1, N_EXPERTS))


def _sc_mesh():
    return plsc.VectorSubcoreMesh(core_axis_name="c", subcore_axis_name="s")


def _sc_worker():
    return lax.axis_index("s") * SC_CORES + lax.axis_index("c")


def _sc_dispatch(x_p, x_s, dest_t, n_slots):
    chunk = SC_DISPATCH_ROWS
    n_p, n_s = x_p.shape[0], x_s.shape[0]
    nch_p, nch_s = n_p // (SC_WORKERS * chunk), n_s // (SC_WORKERS * chunk)
    assert nch_p * SC_WORKERS * chunk == n_p and nch_s * SC_WORKERS * chunk == n_s
    d3 = dest_t.reshape(dest_t.shape[0], (n_p + n_s) // chunk, chunk)
    width, dtype = x_p.shape[1], x_p.dtype

    @functools.partial(
        pl.kernel, mesh=_sc_mesh(),
        out_type=jax.ShapeDtypeStruct((n_slots, width), dtype),
        scratch_types=[pltpu.VMEM((TOP_K, nch_p, chunk), jnp.int32),
                       pltpu.VMEM((TOP_K, nch_s, chunk), jnp.int32),
                       pltpu.VMEM((chunk, width), dtype),
                       pltpu.SemaphoreType.DMA],
        compiler_params=pltpu.CompilerParams(use_tc_tiling_on_sc=True),
        name="dispatch")
    def run(xp_hbm, xs_hbm, d_hbm, out_hbm, ip_v, is_v, rows_v, sem):
        wid = _sc_worker()
        pltpu.sync_copy(d_hbm.at[pl.ds(0, TOP_K), pl.ds(wid * nch_p, nch_p)], ip_v)
        pltpu.sync_copy(d_hbm.at[pl.ds(0, TOP_K), pl.ds(n_p // chunk + wid * nch_s, nch_s)], is_v)

        def group(x_hbm, idx_v, nch):
            def body(j, carry):
                pltpu.sync_copy(x_hbm.at[pl.ds((wid * nch + j) * chunk, chunk)], rows_v)
                copies = [pltpu.async_copy(rows_v, out_hbm.at[idx_v.at[kx, j]], sem) for kx in range(TOP_K)]
                for cp in copies:
                    cp.wait()
                return carry

            lax.fori_loop(0, nch, body, 0)

        group(xp_hbm, ip_v, nch_p)
        group(xs_hbm, is_v, nch_s)

    return run(x_p, x_s, d3)


def _sc_collect(y_sorted, dest_t, n_p, n_s):
    chunk = SC_COLLECT_ROWS
    per_choice = SC_WORKERS // TOP_K
    nch_p, nch_s = n_p // (per_choice * chunk), n_s // (per_choice * chunk)
    assert nch_p * per_choice * chunk == n_p and nch_s * per_choice * chunk == n_s
    d3 = dest_t.reshape(dest_t.shape[0], (n_p + n_s) // chunk, chunk)
    width, dtype = y_sorted.shape[1], y_sorted.dtype

    @functools.partial(
        pl.kernel, mesh=_sc_mesh(),
        out_type=[jax.ShapeDtypeStruct((TOP_K * n_p, width), dtype), jax.ShapeDtypeStruct((TOP_K * n_s, width), dtype)],
        scratch_types=[pltpu.VMEM((nch_p, chunk), jnp.int32),
                       pltpu.VMEM((nch_s, chunk), jnp.int32),
                       pltpu.VMEM((chunk, width), dtype),
                       pltpu.SemaphoreType.DMA],
        compiler_params=pltpu.CompilerParams(use_tc_tiling_on_sc=True),
        name="collect")
    def run(y_hbm, d_hbm, op_hbm, os_hbm, ip_v, is_v, rows_v, sem):
        wid = _sc_worker()
        choice = wid // per_choice
        part = wid % per_choice
        pltpu.sync_copy(d_hbm.at[choice, pl.ds(part * nch_p, nch_p)], ip_v)
        pltpu.sync_copy(d_hbm.at[choice, pl.ds(n_p // chunk + part * nch_s, nch_s)], is_v)

        def group(idx_v, o_hbm, nch):
            def body(j, carry):
                pltpu.async_copy(y_hbm.at[idx_v.at[j]], rows_v, sem).wait()
                pltpu.sync_copy(rows_v, o_hbm.at[pl.ds((wid * nch + j) * chunk, chunk)])
                return carry

            lax.fori_loop(0, nch, body, 0)

        group(ip_v, op_hbm, nch_p)
        group(is_v, os_hbm, nch_s)

    return run(y_sorted, d3)


def _expert_body(blk0_ref, nblk_ref, cnt_ref, wup_hbm, wdn_hbm, bup_ref, bdn_ref, x_hbm, y_hbm,
                 wup_f, wdn_f, wup_s, wdn_s, xbuf, obuf, w_sem, in_sem, out_sem):
    e = pl.program_id(0)
    nb = nblk_ref[e]
    blk0 = blk0_ref[e]
    cnt = cnt_ref[e]
    tm = EXPERT_ROWS
    pair = 2 * LANES
    wslot = e % 2

    def w_copies(ex, slot):
        return (pltpu.make_async_copy(wup_hbm.at[ex], wup_f.at[slot], w_sem.at[0, slot]),
                pltpu.make_async_copy(wdn_hbm.at[ex], wdn_f.at[slot], w_sem.at[1, slot]))

    @pl.when(e == 0)
    def _():
        for cp in w_copies(0, 0):
            cp.start()

    @pl.when(e + 1 < N_EXPERTS)
    def _():
        for cp in w_copies(e + 1, 1 - wslot):
            cp.start()

    for cp in w_copies(e, wslot):
        cp.wait()
    wup_ref = wup_f.at[wslot]
    wdn_ref = wdn_f.at[wslot]

    def x_copy(i, slot):
        rows = pl.ds(pl.multiple_of((blk0 + i) * tm, tm), tm)
        return pltpu.make_async_copy(x_hbm.at[rows], xbuf.at[slot], in_sem.at[slot])

    def y_copy(i, slot):
        rows = pl.ds(pl.multiple_of((blk0 + i) * tm, tm), tm)
        return pltpu.make_async_copy(obuf.at[slot], y_hbm.at[rows], out_sem.at[slot])

    @pl.when(nb > 0)
    def _():
        x_copy(0, 0).start(priority=1)
        r = lax.broadcasted_iota(jnp.int32, (pair, pair), 0)
        c = lax.broadcasted_iota(jnp.int32, (pair, pair), 1)
        perm = (r == jnp.where(c < LANES, 2 * c, 2 * (c - LANES) + 1)).astype(BF16)
        for g in range(2 * D_MODEL // pair):
            cols = slice(g * pair, (g + 1) * pair)
            wup_s[g] = jnp.dot(wup_ref[:, cols].astype(BF16), perm, preferred_element_type=F32).astype(BF16)
        for g in range(D_MODEL // pair):
            wdn_s[g] = wdn_ref[:, g * pair:(g + 1) * pair].astype(BF16)

        def block(i, carry):
            slot = i % 2
            x_copy(i, slot).wait()

            @pl.when(i + 1 < nb)
            def _():
                x_copy(i + 1, 1 - slot).start(priority=1)

            @pl.when(i >= 2)
            def _():
                y_copy(i - 2, slot).wait()

            row = lax.broadcasted_iota(jnp.int32, (tm, 1), 0)
            x = jnp.where(row < cnt - i * tm, _unpack_bf16_pair(xbuf[slot]), 0.0).astype(BF16)
            acts = []
            for g in range(2 * D_MODEL // pair):
                cols = slice(g * pair, (g + 1) * pair)
                h = jnp.dot(x, wup_s[g], preferred_element_type=F32) + bup_ref[0, :, cols]
                glu = jnp.minimum(h[:, :LANES], SWIGLU_LIMIT)
                lin = jnp.clip(h[:, LANES:], -SWIGLU_LIMIT, SWIGLU_LIMIT)
                acts.append((glu * jax.nn.sigmoid(SWIGLU_ALPHA * glu) * (lin + 1.0)).astype(BF16))
            act = jnp.concatenate(acts, axis=1)
            half_groups = D_MODEL // pair // 2
            for g in range(half_groups):
                ys = []
                for gg in (g, g + half_groups):
                    cols = slice(gg * pair, (gg + 1) * pair)
                    ys.append(jnp.dot(act, wdn_s[gg], preferred_element_type=F32) + bdn_ref[0, :, cols])
                obuf[slot, :, g * pair:(g + 1) * pair] = _pack_bf16_pair(ys[0], ys[1])
            y_copy(i, slot).start(priority=1)
            return carry

        lax.fori_loop(0, nb, block, 0)

        @pl.when(nb >= 2)
        def _():
            y_copy(nb - 2, nb % 2).wait()

        y_copy(nb - 1, (nb - 1) % 2).wait()


def _experts(blk0, nblk, cnt, x_sorted, w_up, w_down, b_up_grouped, b_down):
    tm = EXPERT_ROWS
    per_expert = lambda a: pl.BlockSpec((1,) + a.shape[1:], lambda e, b0, nb, ct: (e, 0, 0))
    hbm = pl.BlockSpec(memory_space=pl.ANY)
    grid_spec = pltpu.PrefetchScalarGridSpec(
        num_scalar_prefetch=3,
        grid=(N_EXPERTS,),
        in_specs=[hbm, hbm, per_expert(b_up_grouped), per_expert(b_down), hbm],
        out_specs=hbm,
        scratch_shapes=[pltpu.VMEM((2,) + w_up.shape[1:], F32),
                        pltpu.VMEM((2,) + w_down.shape[1:], F32),
                        pltpu.VMEM((2 * D_MODEL // (2 * LANES), D_MODEL, 2 * LANES), BF16),
                        pltpu.VMEM((D_MODEL // (2 * LANES), D_MODEL, 2 * LANES), BF16),
                        pltpu.VMEM((2, tm, x_sorted.shape[1]), x_sorted.dtype),
                        pltpu.VMEM((2, tm, D_MODEL // 2), jnp.uint32),
                        pltpu.SemaphoreType.DMA((2, 2)), pltpu.SemaphoreType.DMA((2,)),
                        pltpu.SemaphoreType.DMA((2,))],
    )
    return pl.pallas_call(
        _expert_body,
        grid_spec=grid_spec,
        out_shape=jax.ShapeDtypeStruct((x_sorted.shape[0], D_MODEL // 2), jnp.uint32),
        compiler_params=_params(("arbitrary",)),
        name="experts",
    )(blk0, nblk, cnt, w_up, w_down, b_up_grouped, b_down, x_sorted)


def _combine_body(gate_ref, xmid_ref, gfin_ref, y0_ref, y1_ref, y2_ref, y3_ref, out_ref):
    gate = gate_ref[...]
    moe = _unpack_bf16_pair(y0_ref[...]) * gate[:, 0:1]
    for kx, y_ref in enumerate((y1_ref, y2_ref, y3_ref), start=1):
        moe = moe + _unpack_bf16_pair(y_ref[...]) * gate[:, kx:kx + 1]
    out_ref[...] = _rms(xmid_ref[...] + moe, gfin_ref[...])


def _combine(gates, first_token, xmid, gfin, y_rows):
    n = xmid.shape[0]
    tt = COMBINE_ROWS
    nblk = n // tt
    blk0 = first_token // tt
    assert blk0 * tt == first_token
    choice = lambda kx: pl.BlockSpec((tt, y_rows.shape[1]), lambda i: (i + kx * nblk, 0))
    return pl.pallas_call(
        _combine_body,
        grid=(nblk,),
        in_specs=[pl.BlockSpec((tt, LANES), lambda i: (i + blk0, 0)),
                  pl.BlockSpec((tt, D_MODEL), lambda i: (i, 0)),
                  pl.BlockSpec((1, D_MODEL), lambda i: (0, 0))] + [choice(kx) for kx in range(TOP_K)],
        out_specs=pl.BlockSpec((tt, D_MODEL), lambda i: (i, 0)),
        out_shape=jax.ShapeDtypeStruct((n, D_MODEL), F32),
        compiler_params=_params(("arbitrary",)),
        name="combine",
    )(gates, xmid, gfin, y_rows, y_rows, y_rows, y_rows)


def kernel(x_prompt, x_sample, cache_k_win, cache_v_win, norm_attn_g, w_in, ln_v_g, ln_v_b, w_spatial, b_spatial,
           attn_sinks, w_out, norm_ffn_g, w_router, b_router, w_up, b_up, w_down, b_down, norm_final_g):
    bp, tp, _ = x_prompt.shape
    bs, ts, _ = x_sample.shape
    w_buf = cache_k_win.shape[2]
    assert bp == 1 and tp % MIX_ROWS == 0 and w_buf == WINDOW and (bs * ts) % PROJ_ROWS == 0 and 8 % ts == 0
    n_p, n_s = bp * tp, bs * ts
    row2 = lambda a: a.reshape(1, -1)

    w_in_bf = w_in[0].astype(BF16)
    w_out_bf = w_out[0].astype(BF16)
    tril = jnp.tril(jnp.ones((CHUNK, CHUNK), dtype=bool))
    wsp = jnp.where(tril[None], w_spatial[0], 0.0)
    wsp_bf = wsp.astype(BF16)
    bsp = jnp.broadcast_to(b_spatial[0][:, :, None], (GMLP_GROUPS, CHUNK, LANES))
    b_up_grouped = b_up[0].reshape(N_EXPERTS, -1, LANES, 2).transpose(0, 1, 3, 2).reshape(N_EXPERTS, 1, -1)
    bd = b_down[0][:, None, :]
    g1, g2, gfin = row2(norm_attn_g[0]), row2(norm_ffn_g[0]), row2(norm_final_g)
    lng, lnb = row2(ln_v_g[0]), row2(ln_v_b[0])
    wr_hi = w_router[0].astype(BF16)
    wr = jnp.concatenate([wr_hi, (w_router[0] - wr_hi.astype(F32)).astype(BF16)], axis=1)
    br = row2(b_router[0])
    sinks = attn_sinks[0]

    xp = x_prompt.reshape(n_p, D_MODEL)
    cs_p = _rotary_inputs(jnp.arange(tp, dtype=jnp.int32))
    q_p, k_p, v_p, a_p, vn_p, sgb_p = _proj(xp, g1, w_in_bf, cs_p, lng, lnb)
    xmid_p, xn2_p, lg_p = _mix(sinks, q_p, k_p, v_p, a_p, vn_p, sgb_p, xp, wsp_bf, bsp, w_out_bf, g2, wr, br)

    xs = x_sample.reshape(n_s, D_MODEL)
    pos_s = PAST_LEN + jnp.arange(ts, dtype=jnp.int32)
    cs_s = _rotary_inputs(jnp.tile(pos_s, bs))
    q_s, k_s, v_s, a_s, vn_s, sgb_s = _proj(xs, g1, w_in_bf, cs_s, lng, lnb)
    k_all = jnp.concatenate([cache_k_win[0], k_s.reshape(bs, ts, N_KV_HEADS, HEAD_DIM)], axis=1)
    v_all = jnp.concatenate([cache_v_win[0], v_s.reshape(bs, ts, N_KV_HEADS, HEAD_DIM)], axis=1)
    n_keys = w_buf + ts
    key_pad = (-n_keys) % 8
    to_heads = lambda t: jnp.pad(t, ((0, 0), (0, key_pad), (0, 0), (0, 0))).transpose(2, 0, 1, 3).astype(BF16)
    qh = q_s.reshape(bs, ts, N_KV_HEADS, GQA_GROUP, HEAD_DIM).transpose(2, 0, 1, 3, 4)
    qh = qh.reshape(N_KV_HEADS, bs, ts * GQA_GROUP, HEAD_DIM)
    sink_rows = jnp.tile(sinks.reshape(N_KV_HEADS, 1, GQA_GROUP), (1, ts, 1)).reshape(N_KV_HEADS, ts * GQA_GROUP, 1)
    oh = _sample_attn(sink_rows, qh, to_heads(k_all), to_heads(v_all))
    o_s = oh.reshape(N_KV_HEADS, bs, ts, GQA_GROUP, HEAD_DIM).transpose(1, 2, 0, 3, 4).reshape(n_s, Q_WIDTH)
    t_idx = jnp.arange(ts)
    coef = jnp.stack([jnp.where((t_idx >= d)[None, :], wsp[:, t_idx, jnp.maximum(t_idx - d, 0)], 0.0)
                      for d in range(ts)])
    coef = jnp.repeat(coef.transpose(0, 2, 1), GMLP_WIDTH // GMLP_GROUPS, axis=2)
    coef = jnp.tile(coef, (1, 8 // ts, 1))
    bias = jnp.tile(jnp.repeat(b_spatial[0][:, :ts].T, GMLP_WIDTH // GMLP_GROUPS, axis=1), (8 // ts, 1))
    xmid_s, xn2_s, lg_s = _mix_sample(a_s, vn_s, sgb_s, o_s, xs, coef, bias, w_out_bf, g2, wr, br)

    n_tok = n_p + n_s
    idx_w, gate_w, rank_w, counts = _route(lg_p, lg_s)
    counts = counts[0]
    tm = EXPERT_ROWS
    padded = (counts + tm - 1) // tm * tm
    pstart = jnp.cumsum(padded) - padded
    dest_t = _slots(idx_w, rank_w, pstart)
    n_blocks = (n_tok * TOP_K) // tm + N_EXPERTS

    x_sorted = _sc_dispatch(xn2_p, xn2_s, dest_t, n_blocks * tm)
    y_sorted = _experts(pstart // tm, padded // tm, counts, x_sorted, w_up[0], w_down[0], b_up_grouped, bd)
    yrows_p, yrows_s = _sc_collect(y_sorted, dest_t, n_p, n_s)
    y_p = _combine(gate_w, 0, xmid_p, gfin, yrows_p)
    y_s = _combine(gate_w, n_p, xmid_s, gfin, yrows_s)

    k4 = lambda t: t.reshape(1, bp, -1, N_KV_HEADS, HEAD_DIM)
    return (y_p.reshape(bp, tp, D_MODEL),
            y_s.reshape(bs, ts, D_MODEL),
            k4(k_p[n_p - WINDOW:]),
            k4(v_p[n_p - WINDOW:]),
            vn_p[n_p - CHUNK:].reshape(1, bp, CHUNK, GMLP_WIDTH),
            k_all[None, :, ts:],
            v_all[None, :, ts:],
            vn_s.reshape(1, bs, ts, GMLP_WIDTH))
```

```python
import functools

import numpy as np
import jax
import jax.numpy as jnp
from jax import lax
from jax.experimental import pallas as pl
from jax.experimental.pallas import tpu as pltpu
from jax.experimental.pallas import tpu_sc as plsc

F32 = jnp.float32
BF16 = jnp.bfloat16

D_MODEL = 1024
HEAD_DIM = 64
N_HEADS = 16
GQA_GROUP = 8
N_KV_HEADS = 2
Q_WIDTH = 1024
KV_WIDTH = 128
WINDOW = 128
ROT_DIM = 16
ROPE_THETA = 500000.0
CHUNK = 128
GMLP_WIDTH = 1024
GMLP_GROUPS = 8
N_EXPERTS = 32
TOP_K = 4
SWIGLU_LIMIT = 7.0
SWIGLU_ALPHA = 1.702
RMS_EPS = 1e-5
LN_EPS = 1e-5
NEG_INF = -1e30
PAST_LEN = 16384

LANES = 128
VMEM_LIMIT = 56 * 1024 * 1024

PROJ_ROWS = 256
MIX_ROWS = 512
ROUTE_ROWS = 512
EXPERT_ROWS = 256
COMBINE_ROWS = 256

SC_CORES = 2
SC_WORKERS = 32
SC_DISPATCH_ROWS = 16
SC_COLLECT_ROWS = 32

_C_Q, _C_KV, _C_U, _C_VG, _C_GA, _C_GB, _C_END = 0, 1024, 1280, 2304, 3328, 4352, 5376


def _params(sem):
    return pltpu.CompilerParams(dimension_semantics=sem, vmem_limit_bytes=VMEM_LIMIT)


def _rms(x, g):
    return x * lax.rsqrt(jnp.mean(x * x, axis=-1, keepdims=True) + RMS_EPS) * g


def _pack_bf16_pair(lo, hi):
    lo_bits = lax.bitcast_convert_type(lo.astype(BF16).astype(F32), jnp.uint32)
    hi_bits = lax.bitcast_convert_type(hi.astype(BF16).astype(F32), jnp.uint32)
    return (lo_bits >> 16) | hi_bits


def _unpack_bf16_pair(words):
    lo = lax.bitcast_convert_type(words << 16, F32)
    hi = lax.bitcast_convert_type(words & jnp.uint32(0xFFFF0000), F32)
    return jnp.concatenate([lo, hi], axis=1)


def _proj_body(x_ref, g_ref, w_ref, cs_ref, rot_ref, lng_ref, lnb_ref,
               q_ref, k_ref, v_ref, a_ref, vn_ref, sgb_ref):
    h = _rms(x_ref[...], g_ref[...]).astype(BF16)
    tabs = lax.dot_general(cs_ref[...], rot_ref[...], (((0,), (0,)), ((), ())),
                           preferred_element_type=F32, precision=lax.Precision.HIGHEST)
    rc, rs1, rs2 = tabs[:, :LANES], tabs[:, LANES:2 * LANES], tabs[:, 2 * LANES:]

    def rot(z):
        return z * rc + pltpu.roll(z, LANES - ROT_DIM // 2, 1) * rs1 + pltpu.roll(z, ROT_DIM // 2, 1) * rs2

    def mm(lo, hi):
        return jnp.dot(h, w_ref[:, lo:hi], preferred_element_type=F32)

    zq = mm(_C_Q, _C_KV)
    for c in range(Q_WIDTH // LANES):
        sl = slice(c * LANES, (c + 1) * LANES)
        q_ref[:, sl] = (rot(zq[:, sl]) * (HEAD_DIM ** -0.5)).astype(BF16)
    zkv = mm(_C_KV, _C_U)
    k_ref[...] = rot(zkv[:, :KV_WIDTH])
    v_ref[...] = zkv[:, KV_WIDTH:]
    a_ref[...] = jax.nn.sigmoid(mm(_C_GA, _C_GB)) * jax.nn.gelu(mm(_C_U, _C_VG))
    zv = jax.nn.gelu(mm(_C_VG, _C_GA))
    zc = zv - jnp.mean(zv, axis=-1, keepdims=True)
    var = jnp.mean(zc * zc, axis=-1, keepdims=True)
    vn_ref[...] = zc * lax.rsqrt(var + LN_EPS) * lng_ref[...] + lnb_ref[...]
    sgb_ref[...] = jax.nn.sigmoid(mm(_C_GB, _C_END))


def _proj(x, norm_g, w_in_bf, cs, ln_g, ln_b):
    n = x.shape[0]
    tm = PROJ_ROWS
    row = lambda w: pl.BlockSpec((tm, w), lambda i: (i, 0))
    full = lambda a: pl.BlockSpec(a.shape, lambda i: (0,) * a.ndim)
    rot = jnp.asarray(_ROT_EXPAND)
    return pl.pallas_call(
        _proj_body,
        grid=(n // tm,),
        in_specs=[row(D_MODEL), full(norm_g), full(w_in_bf), pl.BlockSpec((cs.shape[0], tm), lambda i: (0, i)),
                  full(rot),
                  full(ln_g), full(ln_b)],
        out_specs=[row(Q_WIDTH), row(KV_WIDTH), row(KV_WIDTH), row(GMLP_WIDTH), row(GMLP_WIDTH), row(D_MODEL)],
        out_shape=[jax.ShapeDtypeStruct((n, Q_WIDTH), BF16),
                   jax.ShapeDtypeStruct((n, KV_WIDTH), F32),
                   jax.ShapeDtypeStruct((n, KV_WIDTH), F32),
                   jax.ShapeDtypeStruct((n, GMLP_WIDTH), F32),
                   jax.ShapeDtypeStruct((n, GMLP_WIDTH), F32),
                   jax.ShapeDtypeStruct((n, D_MODEL), F32)],
        compiler_params=_params(("arbitrary",)),
        name="proj",
    )(x, norm_g, w_in_bf, cs, rot, ln_g, ln_b)


_ROT_COLS = 32


def _rot_expand():
    half = ROT_DIM // 2
    m = np.zeros((_ROT_COLS, 3 * LANES), np.float32)
    for lane in range(LANES):
        d = lane % HEAD_DIM
        if d < ROT_DIM:
            m[d % half, lane] = 1.0
        else:
            m[2 * half, lane] = 1.0
        if d < half:
            m[half + d, LANES + lane] = -1.0
        elif d < ROT_DIM:
            m[half + d - half, 2 * LANES + lane] = 1.0
    return m


_ROT_EXPAND = _rot_expand()


def _rotary_inputs(pos):
    half = ROT_DIM // 2
    inv_freq = ROPE_THETA ** (-jnp.arange(half, dtype=F32) / half)
    ang = inv_freq[:, None] * pos.astype(F32)[None, :]
    n = pos.shape[0]
    return jnp.concatenate([jnp.cos(ang), jnp.sin(ang), jnp.ones((1, n), F32),
                            jnp.zeros((_ROT_COLS - 2 * half - 1, n), F32)], axis=0)


def _finish_rows(merged_bf, x, wout_ref, g2_ref, wr_ref, br_ref, xmid_ref, xn2_ref, lg_ref):
    xm = x + jnp.dot(merged_bf, wout_ref[...], preferred_element_type=F32)
    xmid_ref[...] = xm
    xn = _rms(xm, g2_ref[...])
    x_hi = xn.astype(BF16)
    x_lo = (xn - x_hi.astype(F32)).astype(BF16)
    w_hl = wr_ref[...]
    p_hi = jnp.dot(x_hi, w_hl, preferred_element_type=F32)
    p_lo = jnp.dot(x_lo, w_hl[:, :N_EXPERTS], preferred_element_type=F32)
    lg = p_hi[:, :N_EXPERTS] + (p_hi[:, N_EXPERTS:] + p_lo) + br_ref[...]
    wide = jnp.concatenate([lg, jnp.zeros((lg.shape[0], LANES - N_EXPERTS), F32)], axis=1)
    lg_ref[...] = wide.T[:N_EXPERTS]
    xn2_ref[...] = _pack_bf16_pair(xn[:, :D_MODEL // 2], xn[:, D_MODEL // 2:])


def _mix_body(sinks_ref, q_ref, k_ref, kp_ref, v_ref, vp_ref, a_ref, vn_ref, sgb_ref, x_ref,
              wsp_ref, bsp_ref, wout_ref, g2_ref, wr_ref, br_ref,
              xmid_ref, xn2_ref, lg_ref, kcat, vcat, mrg):
    i = pl.program_id(0)
    nsub = MIX_ROWS // WINDOW
    kcat[0:WINDOW] = kp_ref[...]
    kcat[WINDOW:] = k_ref[...]
    vcat[0:WINDOW] = vp_ref[...]
    vcat[WINDOW:] = v_ref[...]

    pair_rows = (GQA_GROUP // 2) * WINDOW
    rq = lax.broadcasted_iota(jnp.int32, (pair_rows, 4 * WINDOW), 0) & (WINDOW - 1)
    ck = lax.broadcasted_iota(jnp.int32, (pair_rows, 4 * WINDOW), 1) & (2 * WINDOW - 1)
    band = (ck > rq) & (ck <= rq + WINDOW)
    lane_kv = lax.broadcasted_iota(jnp.int32, (2 * WINDOW, LANES), 1)
    lane_o = lax.broadcasted_iota(jnp.int32, (pair_rows, LANES), 1)
    row_p = lax.broadcasted_iota(jnp.int32, (pair_rows, 1), 0) >> 7

    def sub(j, carry):
        off = pl.multiple_of(j * WINDOW, WINDOW)
        rows = pl.ds(off, WINDOW)
        for g in range(GMLP_GROUPS):
            cols = slice(g * LANES, (g + 1) * LANES)
            s = jnp.dot(wsp_ref[g], vn_ref[rows, cols].astype(BF16), preferred_element_type=F32) + bsp_ref[g]
            mrg[rows, cols] = a_ref[rows, cols] * s
        kblk = kcat[pl.ds(off, 2 * WINDOW), :]
        vblk = vcat[pl.ds(off, 2 * WINDOW), :]
        kswp = pltpu.roll(kblk, HEAD_DIM, 1)
        vswp = pltpu.roll(vblk, HEAD_DIM, 1)
        kmin = jnp.where(jnp.logical_and(i == 0, j == 0), WINDOW, 0)
        allowed = band & (ck >= kmin)
        for kk in range(N_KV_HEADS):
            lo_src, hi_src = (kblk, kswp) if kk == 0 else (kswp, kblk)
            kbd = jnp.concatenate([jnp.where(lane_kv < HEAD_DIM, lo_src, 0.0),
                                   jnp.where(lane_kv >= HEAD_DIM, hi_src, 0.0)], axis=0).astype(BF16)
            lo_src, hi_src = (vblk, vswp) if kk == 0 else (vswp, vblk)
            vbd = jnp.concatenate([jnp.where(lane_kv < HEAD_DIM, lo_src, 0.0),
                                   jnp.where(lane_kv >= HEAD_DIM, hi_src, 0.0)], axis=0).astype(BF16)
            pair0 = kk * (GQA_GROUP // 2)
            qs = jnp.concatenate([q_ref[rows, (pair0 + p) * LANES:(pair0 + p + 1) * LANES]
                                  for p in range(GQA_GROUP // 2)], axis=0)
            lg = lax.dot_general(qs, kbd, (((1,), (1,)), ((), ())), preferred_element_type=F32)
            lg = jnp.where(allowed, lg, NEG_INF)
            h0 = kk * GQA_GROUP
            se = jnp.full((pair_rows, 1), sinks_ref[h0], F32)
            so = jnp.full((pair_rows, 1), sinks_ref[h0 + 1], F32)
            for p in range(1, GQA_GROUP // 2):
                se = jnp.where(row_p == p, sinks_ref[h0 + 2 * p], se)
                so = jnp.where(row_p == p, sinks_ref[h0 + 2 * p + 1], so)
            le, lo = lg[:, :2 * WINDOW], lg[:, 2 * WINDOW:]
            me = jnp.maximum(jnp.max(le, axis=1, keepdims=True), se)
            mo = jnp.maximum(jnp.max(lo, axis=1, keepdims=True), so)
            pe = jnp.exp(le - me)
            po = jnp.exp(lo - mo)
            de = jnp.sum(pe, axis=1, keepdims=True) + jnp.exp(se - me)
            do = jnp.sum(po, axis=1, keepdims=True) + jnp.exp(so - mo)
            pr = jnp.concatenate([pe, po], axis=1).astype(BF16)
            o = jnp.dot(pr, vbd, preferred_element_type=F32)
            o = o / jnp.where(lane_o < HEAD_DIM, de, do)
            for p in range(GQA_GROUP // 2):
                cols = slice((pair0 + p) * LANES, (pair0 + p + 1) * LANES)
                mrg[rows, cols] += sgb_ref[rows, cols] * o[p * WINDOW:(p + 1) * WINDOW]
        return carry

    lax.fori_loop(0, nsub, sub, 0)
    _finish_rows(mrg[...].astype(BF16), x_ref[...], wout_ref, g2_ref, wr_ref, br_ref, xmid_ref, xn2_ref, lg_ref)


def _mix(sinks, q, k, v, a, vn, sgb, x, wsp, bsp, wout, g2, wr, br):
    n = x.shape[0]
    tm = MIX_ROWS
    nsub = tm // WINDOW
    row = lambda w: pl.BlockSpec((tm, w), lambda i: (i, 0))
    prev = pl.BlockSpec((WINDOW, KV_WIDTH), lambda i: (jnp.maximum(i * nsub - 1, 0), 0))
    full = lambda arr: pl.BlockSpec(arr.shape, lambda i: (0,) * arr.ndim)
    smem = pl.BlockSpec(memory_space=pltpu.SMEM)
    return pl.pallas_call(
        _mix_body,
        grid=(n // tm,),
        in_specs=[smem, row(Q_WIDTH), row(KV_WIDTH), prev, row(KV_WIDTH), prev,
                  row(GMLP_WIDTH), row(GMLP_WIDTH), row(D_MODEL), row(D_MODEL),
                  full(wsp), full(bsp), full(wout), full(g2), full(wr), full(br)],
        out_specs=[row(D_MODEL), row(D_MODEL // 2), pl.BlockSpec((N_EXPERTS, tm), lambda i: (0, i))],
        out_shape=[jax.ShapeDtypeStruct((n, D_MODEL), F32),
                   jax.ShapeDtypeStruct((n, D_MODEL // 2), jnp.uint32),
                   jax.ShapeDtypeStruct((N_EXPERTS, n), F32)],
        scratch_shapes=[pltpu.VMEM((tm + WINDOW, KV_WIDTH), F32),
                        pltpu.VMEM((tm + WINDOW, KV_WIDTH), F32),
                        pltpu.VMEM((tm, D_MODEL), F32)],
        compiler_params=_params(("arbitrary",)),
        name="mix_prompt",
    )(sinks, q, k, k, v, v, a, vn, sgb, x, wsp, bsp, wout, g2, wr, br)


def _sample_attn_body(sink_ref, q_ref, k_ref, v_ref, o_ref):
    q = q_ref[0]
    k = k_ref[0]
    v = v_ref[0]
    nq, nk = q.shape[1], k.shape[1]
    lg = jnp.einsum("bqd,bkd->bqk", q, k, preferred_element_type=F32)
    t = lax.broadcasted_iota(jnp.int32, (1, nq, nk), 1) // GQA_GROUP
    j = lax.broadcasted_iota(jnp.int32, (1, nq, nk), 2)
    lg = jnp.where((j > t) & (j <= t + WINDOW), lg, NEG_INF)
    sink = sink_ref[0][None]
    m = jnp.maximum(jnp.max(lg, axis=2, keepdims=True), sink)
    p = jnp.exp(lg - m)
    den = jnp.sum(p, axis=2, keepdims=True) + jnp.exp(sink - m)
    o = jnp.einsum("bqk,bkd->bqd", p.astype(BF16), v, preferred_element_type=F32)
    o_ref[0] = o / den


def _sample_attn(sink_rows, qh, kh, vh):
    nb = qh.shape[1]
    bb = 32
    blk = lambda a: pl.BlockSpec((1, bb) + a.shape[2:], lambda kk, b: (kk, b, 0, 0))
    return pl.pallas_call(
        _sample_attn_body,
        grid=(N_KV_HEADS, nb // bb),
        in_specs=[pl.BlockSpec((1,) + sink_rows.shape[1:], lambda kk, b: (kk, 0, 0)), blk(qh), blk(kh), blk(vh)],
        out_specs=pl.BlockSpec((1, bb) + qh.shape[2:], lambda kk, b: (kk, b, 0, 0)),
        out_shape=jax.ShapeDtypeStruct(qh.shape, F32),
        compiler_params=_params(("arbitrary", "arbitrary")),
        name="attn_sample",
    )(sink_rows, qh, kh, vh)


def _mix_sample_body(a_ref, vn_ref, sgb_ref, o_ref, x_ref, coef_ref, bias_ref,
                     wout_ref, g2_ref, wr_ref, br_ref, xmid_ref, xn2_ref, lg_ref):
    vn = vn_ref[...]
    n, width = vn.shape
    rows8 = lambda t: t.reshape(n // 8, 8, width)
    s = bias_ref[...][None] + coef_ref[0][None] * rows8(vn)
    for d in range(1, coef_ref.shape[0]):
        s = s + coef_ref[d][None] * rows8(pltpu.roll(vn, d, 0))
    merged = a_ref[...] * s.reshape(n, width) + sgb_ref[...] * o_ref[...]
    _finish_rows(merged.astype(BF16), x_ref[...], wout_ref, g2_ref, wr_ref, br_ref, xmid_ref, xn2_ref, lg_ref)


def _mix_sample(a, vn, sgb, o, x, coef, bias, wout, g2, wr, br):
    n = x.shape[0]
    args = (a, vn, sgb, o, x, coef, bias, wout, g2, wr, br)
    full = lambda arr: pl.BlockSpec(arr.shape, lambda i: (0,) * arr.ndim)
    return pl.pallas_call(
        _mix_sample_body,
        grid=(1,),
        in_specs=[full(arr) for arr in args],
        out_specs=[pl.BlockSpec((n, D_MODEL), lambda i: (0, 0)), pl.BlockSpec((n, D_MODEL // 2), lambda i: (0, 0)),
                   pl.BlockSpec((N_EXPERTS, n), lambda i: (0, 0))],
        out_shape=[jax.ShapeDtypeStruct((n, D_MODEL), F32),
                   jax.ShapeDtypeStruct((n, D_MODEL // 2), jnp.uint32),
                   jax.ShapeDtypeStruct((N_EXPERTS, n), F32)],
        compiler_params=_params(("arbitrary",)),
        name="mix_sample",
    )(*args)


def _rows8(rows, dtype):
    n = rows[0].shape[1]
    sub = lax.broadcasted_iota(jnp.int32, (8, n), 0)
    out = jnp.zeros((8, n), dtype)
    for kx, r in enumerate(rows):
        out = jnp.where(sub == kx, r.astype(dtype), out)
    return out


def _route_body(nblk_p, nblk, lgp_ref, lgs_ref, gate_ref, dest_ref, meta_ref, idx_s, rank_s, base):
    i = pl.program_id(0)

    @pl.when(i == 0)
    def _():
        base[...] = jnp.zeros_like(base)

    l = jnp.where(jnp.full(lgp_ref.shape, i, jnp.int32) < nblk_p, lgp_ref[...], lgs_ref[...])
    tb = l.shape[1]
    sub = lax.broadcasted_iota(jnp.int32, l.shape, 0).astype(F32)
    vals, idxs, sels = [], [], []
    for _ in range(TOP_K):
        m = jnp.max(l, axis=0, keepdims=True)
        ik = jnp.min(jnp.where(l == m, sub, float(N_EXPERTS)), axis=0, keepdims=True)
        sel = sub == ik
        l = jnp.where(sel, -jnp.inf, l)
        vals.append(m)
        idxs.append(ik)
        sels.append(sel)
    es = [jnp.exp(vk - vals[0]) for vk in vals]
    den = es[0] + es[1] + es[2] + es[3]
    onehot = jnp.zeros(l.shape, F32)
    for sel in sels:
        onehot = onehot + sel.astype(F32)
    earlier = (lax.broadcasted_iota(jnp.int32, (tb, tb), 0) < lax.broadcasted_iota(jnp.int32, (tb, tb), 1))
    before = jnp.dot(onehot.astype(BF16), earlier.astype(BF16), preferred_element_type=F32) + base[...]
    ranks = [jnp.sum(jnp.where(sel, before, 0.0), axis=0, keepdims=True) for sel in sels]
    base[...] += jnp.sum(onehot, axis=1, keepdims=True)
    idx_s[i] = _rows8(idxs, F32)
    rank_s[i] = _rows8(ranks, F32)
    gates = jnp.concatenate([_rows8([e / den for e in es], F32), jnp.zeros((LANES - 8, tb), F32)], axis=0)
    gate_ref[...] = gates.T

    @pl.when(i == nblk - 1)
    def _():
        cnt = base[...]
        padded = jnp.ceil(cnt / EXPERT_ROWS) * EXPERT_ROWS
        lower = (lax.broadcasted_iota(jnp.int32, (N_EXPERTS, N_EXPERTS), 1) <
                 lax.broadcasted_iota(jnp.int32, (N_EXPERTS, N_EXPERTS), 0)).astype(F32)
        pstart = jnp.dot(lower, jnp.broadcast_to(padded, (N_EXPERTS, LANES)), preferred_element_type=F32,
                         precision=lax.Precision.HIGHEST)[:, :1]
        lane = lax.broadcasted_iota(jnp.int32, (N_EXPERTS, LANES), 1)
        meta = jnp.where(lane == 0, pstart / EXPERT_ROWS,
                         jnp.where(lane == 1, padded / EXPERT_ROWS, jnp.where(lane == 2, cnt, 0.0)))
        meta_ref[...] = meta.astype(jnp.int32)
        sub_e = lax.broadcasted_iota(jnp.int32, (N_EXPERTS, tb), 0).astype(F32)
        for b in range(nblk):
            idx, rank = idx_s[b], rank_s[b]
            rows = [jnp.sum(jnp.where(sub_e == idx[kx:kx + 1], pstart, 0.0), axis=0, keepdims=True)
                    + rank[kx:kx + 1] for kx in range(TOP_K)]
            dest_ref[:, b * tb:(b + 1) * tb] = _rows8(rows, jnp.int32)


def _route(logits_p, logits_s):
    tb = ROUTE_ROWS
    nblk_p, nblk_s = logits_p.shape[1] // tb, logits_s.shape[1] // tb
    nblk = nblk_p + nblk_s
    n = nblk * tb
    assert n == logits_p.shape[1] + logits_s.shape[1]
    return pl.pallas_call(
        functools.partial(_route_body, nblk_p, nblk),
        grid=(nblk,),
        in_specs=[pl.BlockSpec((N_EXPERTS, tb), lambda i: (0, jnp.minimum(i, nblk_p - 1))),
                  pl.BlockSpec((N_EXPERTS, tb), lambda i: (0, jnp.maximum(i - nblk_p, 0)))],
        out_specs=[pl.BlockSpec((tb, LANES), lambda i: (i, 0)),
                   pl.BlockSpec((8, n), lambda i: (0, 0)),
                   pl.BlockSpec((N_EXPERTS, LANES), lambda i: (0, 0))],
        out_shape=[jax.ShapeDtypeStruct((n, LANES), F32),
                   jax.ShapeDtypeStruct((8, n), jnp.int32),
                   jax.ShapeDtypeStruct((N_EXPERTS, LANES), jnp.int32)],
        scratch_shapes=[pltpu.VMEM((nblk, 8, tb), F32), pltpu.VMEM((nblk, 8, tb), F32),
                        pltpu.VMEM((N_EXPERTS, 1), F32)],
        compiler_params=_params(("arbitrary",)),
        name="route",
    )(logits_p, logits_s)


def _sc_mesh():
    return plsc.VectorSubcoreMesh(core_axis_name="c", subcore_axis_name="s")


def _sc_worker():
    return lax.axis_index("s") * SC_CORES + lax.axis_index("c")


def _sc_dispatch(x_p, x_s, dest_t, n_slots):
    chunk = SC_DISPATCH_ROWS
    n_p, n_s = x_p.shape[0], x_s.shape[0]
    nch_p, nch_s = n_p // (SC_WORKERS * chunk), n_s // (SC_WORKERS * chunk)
    assert nch_p * SC_WORKERS * chunk == n_p and nch_s * SC_WORKERS * chunk == n_s
    d3 = dest_t.reshape(dest_t.shape[0], (n_p + n_s) // chunk, chunk)
    width, dtype = x_p.shape[1], x_p.dtype

    @functools.partial(
        pl.kernel, mesh=_sc_mesh(),
        out_type=jax.ShapeDtypeStruct((n_slots, width), dtype),
        scratch_types=[pltpu.VMEM((TOP_K, nch_p, chunk), jnp.int32),
                       pltpu.VMEM((TOP_K, nch_s, chunk), jnp.int32),
                       pltpu.VMEM((chunk, width), dtype),
                       pltpu.SemaphoreType.DMA],
        compiler_params=pltpu.CompilerParams(use_tc_tiling_on_sc=True),
        name="dispatch")
    def run(xp_hbm, xs_hbm, d_hbm, out_hbm, ip_v, is_v, rows_v, sem):
        wid = _sc_worker()
        pltpu.sync_copy(d_hbm.at[pl.ds(0, TOP_K), pl.ds(wid * nch_p, nch_p)], ip_v)
        pltpu.sync_copy(d_hbm.at[pl.ds(0, TOP_K), pl.ds(n_p // chunk + wid * nch_s, nch_s)], is_v)

        def group(x_hbm, idx_v, nch):
            def body(j, carry):
                pltpu.sync_copy(x_hbm.at[pl.ds((wid * nch + j) * chunk, chunk)], rows_v)
                copies = [pltpu.async_copy(rows_v, out_hbm.at[idx_v.at[kx, j]], sem) for kx in range(TOP_K)]
                for cp in copies:
                    cp.wait()
                return carry

            lax.fori_loop(0, nch, body, 0)

        group(xp_hbm, ip_v, nch_p)
        group(xs_hbm, is_v, nch_s)

    return run(x_p, x_s, d3)


def _sc_collect(y_sorted, dest_t, n_p, n_s):
    chunk = SC_COLLECT_ROWS
    per_choice = SC_WORKERS // TOP_K
    nch_p, nch_s = n_p // (per_choice * chunk), n_s // (per_choice * chunk)
    assert nch_p * per_choice * chunk == n_p and nch_s * per_choice * chunk == n_s
    d3 = dest_t.reshape(dest_t.shape[0], (n_p + n_s) // chunk, chunk)
    width, dtype = y_sorted.shape[1], y_sorted.dtype

    @functools.partial(
        pl.kernel, mesh=_sc_mesh(),
        out_type=[jax.ShapeDtypeStruct((TOP_K * n_p, width), dtype), jax.ShapeDtypeStruct((TOP_K * n_s, width), dtype)],
        scratch_types=[pltpu.VMEM((nch_p, chunk), jnp.int32),
                       pltpu.VMEM((nch_s, chunk), jnp.int32),
                       pltpu.VMEM((chunk, width), dtype),
                       pltpu.SemaphoreType.DMA],
        compiler_params=pltpu.CompilerParams(use_tc_tiling_on_sc=True),
        name="collect")
    def run(y_hbm, d_hbm, op_hbm, os_hbm, ip_v, is_v, rows_v, sem):
        wid = _sc_worker()
        choice = wid // per_choice
        part = wid % per_choice
        pltpu.sync_copy(d_hbm.at[choice, pl.ds(part * nch_p, nch_p)], ip_v)
        pltpu.sync_copy(d_hbm.at[choice, pl.ds(n_p // chunk + part * nch_s, nch_s)], is_v)

        def group(idx_v, o_hbm, nch):
            def body(j, carry):
                pltpu.async_copy(y_hbm.at[idx_v.at[j]], rows_v, sem).wait()
                pltpu.sync_copy(rows_v, o_hbm.at[pl.ds((wid * nch + j) * chunk, chunk)])
                return carry

            lax.fori_loop(0, nch, body, 0)

        group(ip_v, op_hbm, nch_p)
        group(is_v, os_hbm, nch_s)

    return run(y_sorted, d3)


def _expert_body(blk0_ref, nblk_ref, cnt_ref, wup_ref, wdn_ref, bup_ref, bdn_ref, x_hbm, y_hbm,
                 wup_s, wdn_s, xbuf, obuf, in_sem, out_sem):
    e = pl.program_id(0)
    nb = nblk_ref[e]
    blk0 = blk0_ref[e]
    cnt = cnt_ref[e]
    tm = EXPERT_ROWS
    pair = 2 * LANES

    def x_copy(i, slot):
        rows = pl.ds(pl.multiple_of((blk0 + i) * tm, tm), tm)
        return pltpu.make_async_copy(x_hbm.at[rows], xbuf.at[slot], in_sem.at[slot])

    def y_copy(i, slot):
        rows = pl.ds(pl.multiple_of((blk0 + i) * tm, tm), tm)
        return pltpu.make_async_copy(obuf.at[slot], y_hbm.at[rows], out_sem.at[slot])

    @pl.when(nb > 0)
    def _():
        x_copy(0, 0).start(priority=1)
        r = lax.broadcasted_iota(jnp.int32, (pair, pair), 0)
        c = lax.broadcasted_iota(jnp.int32, (pair, pair), 1)
        perm = (r == jnp.where(c < LANES, 2 * c, 2 * (c - LANES) + 1)).astype(BF16)
        for g in range(2 * D_MODEL // pair):
            cols = slice(g * pair, (g + 1) * pair)
            wup_s[g] = jnp.dot(wup_ref[0, :, cols].astype(BF16), perm, preferred_element_type=F32).astype(BF16)
        for g in range(D_MODEL // pair):
            wdn_s[g] = wdn_ref[0, :, g * pair:(g + 1) * pair].astype(BF16)

        def block(i, carry):
            slot = i % 2
            x_copy(i, slot).wait()

            @pl.when(i + 1 < nb)
            def _():
                x_copy(i + 1, 1 - slot).start(priority=1)

            @pl.when(i >= 2)
            def _():
                y_copy(i - 2, slot).wait()

            row = lax.broadcasted_iota(jnp.int32, (tm, 1), 0)
            x = jnp.where(row < cnt - i * tm, _unpack_bf16_pair(xbuf[slot]), 0.0).astype(BF16)
            acts = []
            for g in range(2 * D_MODEL // pair):
                cols = slice(g * pair, (g + 1) * pair)
                h = jnp.dot(x, wup_s[g], preferred_element_type=F32) + bup_ref[0, :, cols]
                glu = jnp.minimum(h[:, :LANES], SWIGLU_LIMIT)
                lin = jnp.clip(h[:, LANES:], -SWIGLU_LIMIT, SWIGLU_LIMIT)
                acts.append((glu * jax.nn.sigmoid(SWIGLU_ALPHA * glu) * (lin + 1.0)).astype(BF16))
            act = jnp.concatenate(acts, axis=1)
            half_groups = D_MODEL // pair // 2
            for g in range(half_groups):
                ys = []
                for gg in (g, g + half_groups):
                    cols = slice(gg * pair, (gg + 1) * pair)
                    ys.append(jnp.dot(act, wdn_s[gg], preferred_element_type=F32) + bdn_ref[0, :, cols])
                obuf[slot, :, g * pair:(g + 1) * pair] = _pack_bf16_pair(ys[0], ys[1])
            y_copy(i, slot).start(priority=1)
            return carry

        lax.fori_loop(0, nb, block, 0)

        @pl.when(nb >= 2)
        def _():
            y_copy(nb - 2, nb % 2).wait()

        y_copy(nb - 1, (nb - 1) % 2).wait()


def _experts(blk0, nblk, cnt, x_sorted, w_up, w_down, b_up_grouped, b_down):
    tm = EXPERT_ROWS
    per_expert = lambda a: pl.BlockSpec((1,) + a.shape[1:], lambda e, b0, nb, ct: (e, 0, 0))
    grid_spec = pltpu.PrefetchScalarGridSpec(
        num_scalar_prefetch=3,
        grid=(N_EXPERTS,),
        in_specs=[per_expert(w_up), per_expert(w_down), per_expert(b_up_grouped), per_expert(b_down),
                  pl.BlockSpec(memory_space=pl.ANY)],
        out_specs=pl.BlockSpec(memory_space=pl.ANY),
        scratch_shapes=[pltpu.VMEM((2 * D_MODEL // (2 * LANES), D_MODEL, 2 * LANES), BF16),
                        pltpu.VMEM((D_MODEL // (2 * LANES), D_MODEL, 2 * LANES), BF16),
                        pltpu.VMEM((2, tm, x_sorted.shape[1]), x_sorted.dtype),
                        pltpu.VMEM((2, tm, D_MODEL // 2), jnp.uint32),
                        pltpu.SemaphoreType.DMA((2,)), pltpu.SemaphoreType.DMA((2,))],
    )
    return pl.pallas_call(
        _expert_body,
        grid_spec=grid_spec,
        out_shape=jax.ShapeDtypeStruct((x_sorted.shape[0], D_MODEL // 2), jnp.uint32),
        compiler_params=_params(("arbitrary",)),
        name="experts",
    )(blk0, nblk, cnt, w_up, w_down, b_up_grouped, b_down, x_sorted)


def _combine_body(gate_ref, xmid_ref, gfin_ref, y0_ref, y1_ref, y2_ref, y3_ref, out_ref):
    gate = gate_ref[...]
    moe = _unpack_bf16_pair(y0_ref[...]) * gate[:, 0:1]
    for kx, y_ref in enumerate((y1_ref, y2_ref, y3_ref), start=1):
        moe = moe + _unpack_bf16_pair(y_ref[...]) * gate[:, kx:kx + 1]
    out_ref[...] = _rms(xmid_ref[...] + moe, gfin_ref[...])


def _combine(gates, first_token, xmid, gfin, y_rows):
    n = xmid.shape[0]
    tt = COMBINE_ROWS
    nblk = n // tt
    blk0 = first_token // tt
    assert blk0 * tt == first_token
    choice = lambda kx: pl.BlockSpec((tt, y_rows.shape[1]), lambda i: (i + kx * nblk, 0))
    return pl.pallas_call(
        _combine_body,
        grid=(nblk,),
        in_specs=[pl.BlockSpec((tt, LANES), lambda i: (i + blk0, 0)),
                  pl.BlockSpec((tt, D_MODEL), lambda i: (i, 0)),
                  pl.BlockSpec((1, D_MODEL), lambda i: (0, 0))] + [choice(kx) for kx in range(TOP_K)],
        out_specs=pl.BlockSpec((tt, D_MODEL), lambda i: (i, 0)),
        out_shape=jax.ShapeDtypeStruct((n, D_MODEL), F32),
        compiler_params=_params(("arbitrary",)),
        name="combine",
    )(gates, xmid, gfin, y_rows, y_rows, y_rows, y_rows)


def kernel(x_prompt, x_sample, cache_k_win, cache_v_win, norm_attn_g, w_in, ln_v_g, ln_v_b, w_spatial, b_spatial,
           attn_sinks, w_out, norm_ffn_g, w_router, b_router, w_up, b_up, w_down, b_down, norm_final_g):
    bp, tp, _ = x_prompt.shape
    bs, ts, _ = x_sample.shape
    w_buf = cache_k_win.shape[2]
    assert bp == 1 and tp % MIX_ROWS == 0 and w_buf == WINDOW and (bs * ts) % PROJ_ROWS == 0 and 8 % ts == 0
    n_p, n_s = bp * tp, bs * ts
    row2 = lambda a: a.reshape(1, -1)

    w_in_bf = w_in[0].astype(BF16)
    w_out_bf = w_out[0].astype(BF16)
    tril = jnp.tril(jnp.ones((CHUNK, CHUNK), dtype=bool))
    wsp = jnp.where(tril[None], w_spatial[0], 0.0)
    wsp_bf = wsp.astype(BF16)
    bsp = jnp.broadcast_to(b_spatial[0][:, :, None], (GMLP_GROUPS, CHUNK, LANES))
    b_up_grouped = b_up[0].reshape(N_EXPERTS, -1, LANES, 2).transpose(0, 1, 3, 2).reshape(N_EXPERTS, 1, -1)
    bd = b_down[0][:, None, :]
    g1, g2, gfin = row2(norm_attn_g[0]), row2(norm_ffn_g[0]), row2(norm_final_g)
    lng, lnb = row2(ln_v_g[0]), row2(ln_v_b[0])
    wr_hi = w_router[0].astype(BF16)
    wr = jnp.concatenate([wr_hi, (w_router[0] - wr_hi.astype(F32)).astype(BF16)], axis=1)
    br = row2(b_router[0])
    sinks = attn_sinks[0]

    xp = x_prompt.reshape(n_p, D_MODEL)
    cs_p = _rotary_inputs(jnp.arange(tp, dtype=jnp.int32))
    q_p, k_p, v_p, a_p, vn_p, sgb_p = _proj(xp, g1, w_in_bf, cs_p, lng, lnb)
    xmid_p, xn2_p, lg_p = _mix(sinks, q_p, k_p, v_p, a_p, vn_p, sgb_p, xp, wsp_bf, bsp, w_out_bf, g2, wr, br)

    xs = x_sample.reshape(n_s, D_MODEL)
    pos_s = PAST_LEN + jnp.arange(ts, dtype=jnp.int32)
    cs_s = _rotary_inputs(jnp.tile(pos_s, bs))
    q_s, k_s, v_s, a_s, vn_s, sgb_s = _proj(xs, g1, w_in_bf, cs_s, lng, lnb)
    k_all = jnp.concatenate([cache_k_win[0], k_s.reshape(bs, ts, N_KV_HEADS, HEAD_DIM)], axis=1)
    v_all = jnp.concatenate([cache_v_win[0], v_s.reshape(bs, ts, N_KV_HEADS, HEAD_DIM)], axis=1)
    n_keys = w_buf + ts
    key_pad = (-n_keys) % 8
    to_heads = lambda t: jnp.pad(t, ((0, 0), (0, key_pad), (0, 0), (0, 0))).transpose(2, 0, 1, 3).astype(BF16)
    qh = q_s.reshape(bs, ts, N_KV_HEADS, GQA_GROUP, HEAD_DIM).transpose(2, 0, 1, 3, 4)
    qh = qh.reshape(N_KV_HEADS, bs, ts * GQA_GROUP, HEAD_DIM)
    sink_rows = jnp.tile(sinks.reshape(N_KV_HEADS, 1, GQA_GROUP), (1, ts, 1)).reshape(N_KV_HEADS, ts * GQA_GROUP, 1)
    oh = _sample_attn(sink_rows, qh, to_heads(k_all), to_heads(v_all))
    o_s = oh.reshape(N_KV_HEADS, bs, ts, GQA_GROUP, HEAD_DIM).transpose(1, 2, 0, 3, 4).reshape(n_s, Q_WIDTH)
    t_idx = jnp.arange(ts)
    coef = jnp.stack([jnp.where((t_idx >= d)[None, :], wsp[:, t_idx, jnp.maximum(t_idx - d, 0)], 0.0)
                      for d in range(ts)])
    coef = jnp.repeat(coef.transpose(0, 2, 1), GMLP_WIDTH // GMLP_GROUPS, axis=2)
    coef = jnp.tile(coef, (1, 8 // ts, 1))
    bias = jnp.tile(jnp.repeat(b_spatial[0][:, :ts].T, GMLP_WIDTH // GMLP_GROUPS, axis=1), (8 // ts, 1))
    xmid_s, xn2_s, lg_s = _mix_sample(a_s, vn_s, sgb_s, o_s, xs, coef, bias, w_out_bf, g2, wr, br)

    n_tok = n_p + n_s
    gate_w, dest_t, meta = _route(lg_p, lg_s)
    tm = EXPERT_ROWS
    n_blocks = (n_tok * TOP_K) // tm + N_EXPERTS

    x_sorted = _sc_dispatch(xn2_p, xn2_s, dest_t, n_blocks * tm)
    y_sorted = _experts(meta[:, 0], meta[:, 1], meta[:, 2], x_sorted, w_up[0], w_down[0], b_up_grouped, bd)
    yrows_p, yrows_s = _sc_collect(y_sorted, dest_t, n_p, n_s)
    y_p = _combine(gate_w, 0, xmid_p, gfin, yrows_p)
    y_s = _combine(gate_w, n_p, xmid_s, gfin, yrows_s)

    k4 = lambda t: t.reshape(1, bp, -1, N_KV_HEADS, HEAD_DIM)
    return (y_p.reshape(bp, tp, D_MODEL),
            y_s.reshape(bs, ts, D_MODEL),
            k4(k_p[n_p - WINDOW:]),
            k4(v_p[n_p - WINDOW:]),
            vn_p[n_p - CHUNK:].reshape(1, bp, CHUNK, GMLP_WIDTH),
            k_all[None, :, ts:],
            v_all[None, :, ts:],
            vn_s.reshape(1, bs, ts, GMLP_WIDTH))
```

```python
import functools

import numpy as np
import jax
import jax.numpy as jnp
from jax import lax
from jax.experimental import pallas as pl
from jax.experimental.pallas import tpu as pltpu
from jax.experimental.pallas import tpu_sc as plsc

F32 = jnp.float32
BF16 = jnp.bfloat16

D_MODEL = 1024
HEAD_DIM = 64
N_HEADS = 16
GQA_GROUP = 8
N_KV_HEADS = 2
Q_WIDTH = 1024
KV_WIDTH = 128
WINDOW = 128
ROT_DIM = 16
ROPE_THETA = 500000.0
CHUNK = 128
GMLP_WIDTH = 1024
GMLP_GROUPS = 8
N_EXPERTS = 32
TOP_K = 4
SWIGLU_LIMIT = 7.0
SWIGLU_ALPHA = 1.702
RMS_EPS = 1e-5
LN_EPS = 1e-5
NEG_INF = -1e30
PAST_LEN = 16384

LANES = 128
VMEM_LIMIT = 56 * 1024 * 1024

PROJ_ROWS = 256
MIX_ROWS = 512
ROUTE_ROWS = 512
EXPERT_ROWS = 256
COMBINE_ROWS = 256

SC_CORES = 2
SC_WORKERS = 32
SC_ROWS = 64

_C_Q, _C_KV, _C_U, _C_VG, _C_GA, _C_GB, _C_END = 0, 1024, 1280, 2304, 3328, 4352, 5376


def _params(sem):
    return pltpu.CompilerParams(dimension_semantics=sem, vmem_limit_bytes=VMEM_LIMIT)


def _rms(x, g):
    return x * lax.rsqrt(jnp.mean(x * x, axis=-1, keepdims=True) + RMS_EPS) * g


def _pack_bf16_pair(lo, hi):
    lo_bits = lax.bitcast_convert_type(lo.astype(BF16).astype(F32), jnp.uint32)
    hi_bits = lax.bitcast_convert_type(hi.astype(BF16).astype(F32), jnp.uint32)
    return (lo_bits >> 16) | hi_bits


def _unpack_bf16_pair(words):
    lo = lax.bitcast_convert_type(words << 16, F32)
    hi = lax.bitcast_convert_type(words & jnp.uint32(0xFFFF0000), F32)
    return jnp.concatenate([lo, hi], axis=1)


def _proj_body(x_ref, g_ref, w_ref, cs_ref, rot_ref, lng_ref, lnb_ref,
               q_ref, k_ref, v_ref, a_ref, vn_ref, sgb_ref):
    h = _rms(x_ref[...], g_ref[...]).astype(BF16)
    tabs = lax.dot_general(cs_ref[...], rot_ref[...], (((0,), (0,)), ((), ())),
                           preferred_element_type=F32, precision=lax.Precision.HIGHEST)
    rc, rs1, rs2 = tabs[:, :LANES], tabs[:, LANES:2 * LANES], tabs[:, 2 * LANES:]

    def rot(z):
        return z * rc + pltpu.roll(z, LANES - ROT_DIM // 2, 1) * rs1 + pltpu.roll(z, ROT_DIM // 2, 1) * rs2

    def mm(lo, hi):
        return jnp.dot(h, w_ref[:, lo:hi], preferred_element_type=F32)

    zq = mm(_C_Q, _C_KV)
    for c in range(Q_WIDTH // LANES):
        sl = slice(c * LANES, (c + 1) * LANES)
        q_ref[:, sl] = (rot(zq[:, sl]) * (HEAD_DIM ** -0.5)).astype(BF16)
    zkv = mm(_C_KV, _C_U)
    k_ref[...] = rot(zkv[:, :KV_WIDTH])
    v_ref[...] = zkv[:, KV_WIDTH:]
    a_ref[...] = jax.nn.sigmoid(mm(_C_GA, _C_GB)) * jax.nn.gelu(mm(_C_U, _C_VG))
    zv = jax.nn.gelu(mm(_C_VG, _C_GA))
    zc = zv - jnp.mean(zv, axis=-1, keepdims=True)
    var = jnp.mean(zc * zc, axis=-1, keepdims=True)
    vn_ref[...] = zc * lax.rsqrt(var + LN_EPS) * lng_ref[...] + lnb_ref[...]
    sgb_ref[...] = jax.nn.sigmoid(mm(_C_GB, _C_END))


def _proj(x, norm_g, w_in_bf, cs, ln_g, ln_b):
    n = x.shape[0]
    tm = PROJ_ROWS
    row = lambda w: pl.BlockSpec((tm, w), lambda i: (i, 0))
    full = lambda a: pl.BlockSpec(a.shape, lambda i: (0,) * a.ndim)
    rot = jnp.asarray(_ROT_EXPAND)
    return pl.pallas_call(
        _proj_body,
        grid=(n // tm,),
        in_specs=[row(D_MODEL), full(norm_g), full(w_in_bf), pl.BlockSpec((cs.shape[0], tm), lambda i: (0, i)),
                  full(rot),
                  full(ln_g), full(ln_b)],
        out_specs=[row(Q_WIDTH), row(KV_WIDTH), row(KV_WIDTH), row(GMLP_WIDTH), row(GMLP_WIDTH), row(D_MODEL)],
        out_shape=[jax.ShapeDtypeStruct((n, Q_WIDTH), BF16),
                   jax.ShapeDtypeStruct((n, KV_WIDTH), F32),
                   jax.ShapeDtypeStruct((n, KV_WIDTH), F32),
                   jax.ShapeDtypeStruct((n, GMLP_WIDTH), F32),
                   jax.ShapeDtypeStruct((n, GMLP_WIDTH), F32),
                   jax.ShapeDtypeStruct((n, D_MODEL), F32)],
        compiler_params=_params(("arbitrary",)),
        name="proj",
    )(x, norm_g, w_in_bf, cs, rot, ln_g, ln_b)


_ROT_COLS = 32


def _rot_expand():
    half = ROT_DIM // 2
    m = np.zeros((_ROT_COLS, 3 * LANES), np.float32)
    for lane in range(LANES):
        d = lane % HEAD_DIM
        if d < ROT_DIM:
            m[d % half, lane] = 1.0
        else:
            m[2 * half, lane] = 1.0
        if d < half:
            m[half + d, LANES + lane] = -1.0
        elif d < ROT_DIM:
            m[half + d - half, 2 * LANES + lane] = 1.0
    return m


_ROT_EXPAND = _rot_expand()


def _rotary_inputs(pos):
    half = ROT_DIM // 2
    inv_freq = ROPE_THETA ** (-jnp.arange(half, dtype=F32) / half)
    ang = inv_freq[:, None] * pos.astype(F32)[None, :]
    n = pos.shape[0]
    return jnp.concatenate([jnp.cos(ang), jnp.sin(ang), jnp.ones((1, n), F32),
                            jnp.zeros((_ROT_COLS - 2 * half - 1, n), F32)], axis=0)


def _finish_rows(merged_bf, x, wout_ref, g2_ref, wr_ref, br_ref, xmid_ref, xn2_ref, lg_ref):
    xm = x + jnp.dot(merged_bf, wout_ref[...], preferred_element_type=F32)
    xmid_ref[...] = xm
    xn = _rms(xm, g2_ref[...])
    x_hi = xn.astype(BF16)
    x_lo = (xn - x_hi.astype(F32)).astype(BF16)
    w_hl = wr_ref[...]
    p_hi = jnp.dot(x_hi, w_hl, preferred_element_type=F32)
    p_lo = jnp.dot(x_lo, w_hl[:, :N_EXPERTS], preferred_element_type=F32)
    lg = p_hi[:, :N_EXPERTS] + (p_hi[:, N_EXPERTS:] + p_lo) + br_ref[...]
    wide = jnp.concatenate([lg, jnp.zeros((lg.shape[0], LANES - N_EXPERTS), F32)], axis=1)
    lg_ref[...] = wide.T[:N_EXPERTS]
    xn2_ref[...] = _pack_bf16_pair(xn[:, :D_MODEL // 2], xn[:, D_MODEL // 2:])


def _mix_body(sinks_ref, q_ref, k_ref, kp_ref, v_ref, vp_ref, a_ref, vn_ref, sgb_ref, x_ref,
              wsp_ref, bsp_ref, wout_ref, g2_ref, wr_ref, br_ref,
              xmid_ref, xn2_ref, lg_ref, kcat, vcat, mrg):
    i = pl.program_id(0)
    nsub = MIX_ROWS // WINDOW
    kcat[0:WINDOW] = kp_ref[...]
    kcat[WINDOW:] = k_ref[...]
    vcat[0:WINDOW] = vp_ref[...]
    vcat[WINDOW:] = v_ref[...]

    pair_rows = (GQA_GROUP // 2) * WINDOW
    rq = lax.broadcasted_iota(jnp.int32, (pair_rows, 4 * WINDOW), 0) & (WINDOW - 1)
    ck = lax.broadcasted_iota(jnp.int32, (pair_rows, 4 * WINDOW), 1) & (2 * WINDOW - 1)
    band = (ck > rq) & (ck <= rq + WINDOW)
    lane_kv = lax.broadcasted_iota(jnp.int32, (2 * WINDOW, LANES), 1)
    lane_o = lax.broadcasted_iota(jnp.int32, (pair_rows, LANES), 1)
    row_p = lax.broadcasted_iota(jnp.int32, (pair_rows, 1), 0) >> 7

    def sub(j, carry):
        off = pl.multiple_of(j * WINDOW, WINDOW)
        rows = pl.ds(off, WINDOW)
        for g in range(GMLP_GROUPS):
            cols = slice(g * LANES, (g + 1) * LANES)
            s = jnp.dot(wsp_ref[g], vn_ref[rows, cols].astype(BF16), preferred_element_type=F32) + bsp_ref[g]
            mrg[rows, cols] = a_ref[rows, cols] * s
        kblk = kcat[pl.ds(off, 2 * WINDOW), :]
        vblk = vcat[pl.ds(off, 2 * WINDOW), :]
        kswp = pltpu.roll(kblk, HEAD_DIM, 1)
        vswp = pltpu.roll(vblk, HEAD_DIM, 1)
        kmin = jnp.where(jnp.logical_and(i == 0, j == 0), WINDOW, 0)
        allowed = band & (ck >= kmin)
        for kk in range(N_KV_HEADS):
            lo_src, hi_src = (kblk, kswp) if kk == 0 else (kswp, kblk)
            kbd = jnp.concatenate([jnp.where(lane_kv < HEAD_DIM, lo_src, 0.0),
                                   jnp.where(lane_kv >= HEAD_DIM, hi_src, 0.0)], axis=0).astype(BF16)
            lo_src, hi_src = (vblk, vswp) if kk == 0 else (vswp, vblk)
            vbd = jnp.concatenate([jnp.where(lane_kv < HEAD_DIM, lo_src, 0.0),
                                   jnp.where(lane_kv >= HEAD_DIM, hi_src, 0.0)], axis=0).astype(BF16)
            pair0 = kk * (GQA_GROUP // 2)
            qs = jnp.concatenate([q_ref[rows, (pair0 + p) * LANES:(pair0 + p + 1) * LANES]
                                  for p in range(GQA_GROUP // 2)], axis=0)
            lg = lax.dot_general(qs, kbd, (((1,), (1,)), ((), ())), preferred_element_type=F32)
            lg = jnp.where(allowed, lg, NEG_INF)
            h0 = kk * GQA_GROUP
            se = jnp.full((pair_rows, 1), sinks_ref[h0], F32)
            so = jnp.full((pair_rows, 1), sinks_ref[h0 + 1], F32)
            for p in range(1, GQA_GROUP // 2):
                se = jnp.where(row_p == p, sinks_ref[h0 + 2 * p], se)
                so = jnp.where(row_p == p, sinks_ref[h0 + 2 * p + 1], so)
            le, lo = lg[:, :2 * WINDOW], lg[:, 2 * WINDOW:]
            me = jnp.maximum(jnp.max(le, axis=1, keepdims=True), se)
            mo = jnp.maximum(jnp.max(lo, axis=1, keepdims=True), so)
            pe = jnp.exp(le - me)
            po = jnp.exp(lo - mo)
            de = jnp.sum(pe, axis=1, keepdims=True) + jnp.exp(se - me)
            do = jnp.sum(po, axis=1, keepdims=True) + jnp.exp(so - mo)
            pr = jnp.concatenate([pe, po], axis=1).astype(BF16)
            o = jnp.dot(pr, vbd, preferred_element_type=F32)
            o = o / jnp.where(lane_o < HEAD_DIM, de, do)
            for p in range(GQA_GROUP // 2):
                cols = slice((pair0 + p) * LANES, (pair0 + p + 1) * LANES)
                mrg[rows, cols] += sgb_ref[rows, cols] * o[p * WINDOW:(p + 1) * WINDOW]
        return carry

    lax.fori_loop(0, nsub, sub, 0)
    _finish_rows(mrg[...].astype(BF16), x_ref[...], wout_ref, g2_ref, wr_ref, br_ref, xmid_ref, xn2_ref, lg_ref)


def _mix(sinks, q, k, v, a, vn, sgb, x, wsp, bsp, wout, g2, wr, br):
    n = x.shape[0]
    tm = MIX_ROWS
    nsub = tm // WINDOW
    row = lambda w: pl.BlockSpec((tm, w), lambda i: (i, 0))
    prev = pl.BlockSpec((WINDOW, KV_WIDTH), lambda i: (jnp.maximum(i * nsub - 1, 0), 0))
    full = lambda arr: pl.BlockSpec(arr.shape, lambda i: (0,) * arr.ndim)
    smem = pl.BlockSpec(memory_space=pltpu.SMEM)
    return pl.pallas_call(
        _mix_body,
        grid=(n // tm,),
        in_specs=[smem, row(Q_WIDTH), row(KV_WIDTH), prev, row(KV_WIDTH), prev,
                  row(GMLP_WIDTH), row(GMLP_WIDTH), row(D_MODEL), row(D_MODEL),
                  full(wsp), full(bsp), full(wout), full(g2), full(wr), full(br)],
        out_specs=[row(D_MODEL), row(D_MODEL // 2), pl.BlockSpec((N_EXPERTS, tm), lambda i: (0, i))],
        out_shape=[jax.ShapeDtypeStruct((n, D_MODEL), F32),
                   jax.ShapeDtypeStruct((n, D_MODEL // 2), jnp.uint32),
                   jax.ShapeDtypeStruct((N_EXPERTS, n), F32)],
        scratch_shapes=[pltpu.VMEM((tm + WINDOW, KV_WIDTH), F32),
                        pltpu.VMEM((tm + WINDOW, KV_WIDTH), F32),
                        pltpu.VMEM((tm, D_MODEL), F32)],
        compiler_params=_params(("arbitrary",)),
        name="mix_prompt",
    )(sinks, q, k, k, v, v, a, vn, sgb, x, wsp, bsp, wout, g2, wr, br)


def _sample_attn_body(sink_ref, q_ref, k_ref, v_ref, o_ref):
    q = q_ref[0]
    k = k_ref[0]
    v = v_ref[0]
    nq, nk = q.shape[1], k.shape[1]
    lg = jnp.einsum("bqd,bkd->bqk", q, k, preferred_element_type=F32)
    t = lax.broadcasted_iota(jnp.int32, (1, nq, nk), 1) // GQA_GROUP
    j = lax.broadcasted_iota(jnp.int32, (1, nq, nk), 2)
    lg = jnp.where((j > t) & (j <= t + WINDOW), lg, NEG_INF)
    sink = sink_ref[0][None]
    m = jnp.maximum(jnp.max(lg, axis=2, keepdims=True), sink)
    p = jnp.exp(lg - m)
    den = jnp.sum(p, axis=2, keepdims=True) + jnp.exp(sink - m)
    o = jnp.einsum("bqk,bkd->bqd", p.astype(BF16), v, preferred_element_type=F32)
    o_ref[0] = o / den


def _sample_attn(sink_rows, qh, kh, vh):
    nb = qh.shape[1]
    bb = 32
    blk = lambda a: pl.BlockSpec((1, bb) + a.shape[2:], lambda kk, b: (kk, b, 0, 0))
    return pl.pallas_call(
        _sample_attn_body,
        grid=(N_KV_HEADS, nb // bb),
        in_specs=[pl.BlockSpec((1,) + sink_rows.shape[1:], lambda kk, b: (kk, 0, 0)), blk(qh), blk(kh), blk(vh)],
        out_specs=pl.BlockSpec((1, bb) + qh.shape[2:], lambda kk, b: (kk, b, 0, 0)),
        out_shape=jax.ShapeDtypeStruct(qh.shape, F32),
        compiler_params=_params(("arbitrary", "arbitrary")),
        name="attn_sample",
    )(sink_rows, qh, kh, vh)


def _mix_sample_body(a_ref, vn_ref, sgb_ref, o_ref, x_ref, coef_ref, bias_ref,
                     wout_ref, g2_ref, wr_ref, br_ref, xmid_ref, xn2_ref, lg_ref):
    vn = vn_ref[...]
    n, width = vn.shape
    rows8 = lambda t: t.reshape(n // 8, 8, width)
    s = bias_ref[...][None] + coef_ref[0][None] * rows8(vn)
    for d in range(1, coef_ref.shape[0]):
        s = s + coef_ref[d][None] * rows8(pltpu.roll(vn, d, 0))
    merged = a_ref[...] * s.reshape(n, width) + sgb_ref[...] * o_ref[...]
    _finish_rows(merged.astype(BF16), x_ref[...], wout_ref, g2_ref, wr_ref, br_ref, xmid_ref, xn2_ref, lg_ref)


def _mix_sample(a, vn, sgb, o, x, coef, bias, wout, g2, wr, br):
    n = x.shape[0]
    args = (a, vn, sgb, o, x, coef, bias, wout, g2, wr, br)
    full = lambda arr: pl.BlockSpec(arr.shape, lambda i: (0,) * arr.ndim)
    return pl.pallas_call(
        _mix_sample_body,
        grid=(1,),
        in_specs=[full(arr) for arr in args],
        out_specs=[pl.BlockSpec((n, D_MODEL), lambda i: (0, 0)), pl.BlockSpec((n, D_MODEL // 2), lambda i: (0, 0)),
                   pl.BlockSpec((N_EXPERTS, n), lambda i: (0, 0))],
        out_shape=[jax.ShapeDtypeStruct((n, D_MODEL), F32),
                   jax.ShapeDtypeStruct((n, D_MODEL // 2), jnp.uint32),
                   jax.ShapeDtypeStruct((N_EXPERTS, n), F32)],
        compiler_params=_params(("arbitrary",)),
        name="mix_sample",
    )(*args)


def _rows8(rows, dtype):
    n = rows[0].shape[1]
    sub = lax.broadcasted_iota(jnp.int32, (8, n), 0)
    out = jnp.zeros((8, n), dtype)
    for kx, r in enumerate(rows):
        out = jnp.where(sub == kx, r.astype(dtype), out)
    return out


def _route_body(nblk_p, nblk, lgp_ref, lgs_ref, gate_ref, dest_ref, meta_ref, idx_s, rank_s, base):
    i = pl.program_id(0)

    @pl.when(i == 0)
    def _():
        base[...] = jnp.zeros_like(base)

    l = jnp.where(jnp.full(lgp_ref.shape, i, jnp.int32) < nblk_p, lgp_ref[...], lgs_ref[...])
    tb = l.shape[1]
    sub = lax.broadcasted_iota(jnp.int32, l.shape, 0).astype(F32)
    vals, idxs, sels = [], [], []
    for _ in range(TOP_K):
        m = jnp.max(l, axis=0, keepdims=True)
        ik = jnp.min(jnp.where(l == m, sub, float(N_EXPERTS)), axis=0, keepdims=True)
        sel = sub == ik
        l = jnp.where(sel, -jnp.inf, l)
        vals.append(m)
        idxs.append(ik)
        sels.append(sel)
    es = [jnp.exp(vk - vals[0]) for vk in vals]
    den = es[0] + es[1] + es[2] + es[3]
    onehot = jnp.zeros(l.shape, F32)
    for sel in sels:
        onehot = onehot + sel.astype(F32)
    earlier = (lax.broadcasted_iota(jnp.int32, (tb, tb), 0) < lax.broadcasted_iota(jnp.int32, (tb, tb), 1))
    before = jnp.dot(onehot.astype(BF16), earlier.astype(BF16), preferred_element_type=F32) + base[...]
    ranks = [jnp.sum(jnp.where(sel, before, 0.0), axis=0, keepdims=True) for sel in sels]
    base[...] += jnp.sum(onehot, axis=1, keepdims=True)
    idx_s[i] = _rows8(idxs, F32)
    rank_s[i] = _rows8(ranks, F32)
    gates = jnp.concatenate([_rows8([e / den for e in es], F32), jnp.zeros((LANES - 8, tb), F32)], axis=0)
    gate_ref[...] = gates.T

    @pl.when(i == nblk - 1)
    def _():
        cnt = base[...]
        padded = jnp.ceil(cnt / EXPERT_ROWS) * EXPERT_ROWS
        lower = (lax.broadcasted_iota(jnp.int32, (N_EXPERTS, N_EXPERTS), 1) <
                 lax.broadcasted_iota(jnp.int32, (N_EXPERTS, N_EXPERTS), 0)).astype(F32)
        pstart = jnp.dot(lower, jnp.broadcast_to(padded, (N_EXPERTS, LANES)), preferred_element_type=F32,
                         precision=lax.Precision.HIGHEST)[:, :1]
        lane = lax.broadcasted_iota(jnp.int32, (N_EXPERTS, LANES), 1)
        meta = jnp.where(lane == 0, pstart / EXPERT_ROWS,
                         jnp.where(lane == 1, padded / EXPERT_ROWS, jnp.where(lane == 2, cnt, 0.0)))
        meta_ref[...] = meta.astype(jnp.int32)
        sub_e = lax.broadcasted_iota(jnp.int32, (N_EXPERTS, tb), 0).astype(F32)
        for b in range(nblk):
            idx, rank = idx_s[b], rank_s[b]
            rows = [jnp.sum(jnp.where(sub_e == idx[kx:kx + 1], pstart, 0.0), axis=0, keepdims=True)
                    + rank[kx:kx + 1] for kx in range(TOP_K)]
            dest_ref[:, b * tb:(b + 1) * tb] = _rows8(rows, jnp.int32)


def _route(logits_p, logits_s):
    tb = ROUTE_ROWS
    nblk_p, nblk_s = logits_p.shape[1] // tb, logits_s.shape[1] // tb
    nblk = nblk_p + nblk_s
    n = nblk * tb
    assert n == logits_p.shape[1] + logits_s.shape[1]
    return pl.pallas_call(
        functools.partial(_route_body, nblk_p, nblk),
        grid=(nblk,),
        in_specs=[pl.BlockSpec((N_EXPERTS, tb), lambda i: (0, jnp.minimum(i, nblk_p - 1))),
                  pl.BlockSpec((N_EXPERTS, tb), lambda i: (0, jnp.maximum(i - nblk_p, 0)))],
        out_specs=[pl.BlockSpec((tb, LANES), lambda i: (i, 0)),
                   pl.BlockSpec((8, n), lambda i: (0, 0)),
                   pl.BlockSpec((N_EXPERTS, LANES), lambda i: (0, 0))],
        out_shape=[jax.ShapeDtypeStruct((n, LANES), F32),
                   jax.ShapeDtypeStruct((8, n), jnp.int32),
                   jax.ShapeDtypeStruct((N_EXPERTS, LANES), jnp.int32)],
        scratch_shapes=[pltpu.VMEM((nblk, 8, tb), F32), pltpu.VMEM((nblk, 8, tb), F32),
                        pltpu.VMEM((N_EXPERTS, 1), F32)],
        compiler_params=_params(("arbitrary",)),
        name="route",
    )(logits_p, logits_s)


def _sc_mesh():
    return plsc.VectorSubcoreMesh(core_axis_name="c", subcore_axis_name="s")


def _sc_worker():
    return lax.axis_index("s") * SC_CORES + lax.axis_index("c")


def _sc_dispatch(x_p, x_s, dest_t, n_slots):
    chunk = SC_ROWS
    n_p, n_s = x_p.shape[0], x_s.shape[0]
    per_w = n_p // (SC_WORKERS * chunk)
    ns_chunks = n_s // chunk
    assert per_w * SC_WORKERS * chunk == n_p and per_w % 2 == 0
    assert ns_chunks * chunk == n_s and ns_chunks <= SC_WORKERS
    d3 = dest_t.reshape(dest_t.shape[0], (n_p + n_s) // chunk, chunk)
    width, dtype = x_p.shape[1], x_p.dtype

    @functools.partial(
        pl.kernel, mesh=_sc_mesh(),
        out_type=jax.ShapeDtypeStruct((n_slots, width), dtype),
        scratch_types=[pltpu.VMEM((TOP_K, per_w, chunk), jnp.int32),
                       pltpu.VMEM((TOP_K, 1, chunk), jnp.int32),
                       pltpu.VMEM((2, chunk, width), dtype),
                       pltpu.SemaphoreType.DMA, pltpu.SemaphoreType.DMA],
        compiler_params=pltpu.CompilerParams(use_tc_tiling_on_sc=True),
        name="dispatch")
    def run(xp_hbm, xs_hbm, d_hbm, out_hbm, ip_v, is_v, rows_v, rsem, wsem):
        wid = _sc_worker()
        pltpu.sync_copy(d_hbm.at[pl.ds(0, TOP_K), pl.ds(wid * per_w, per_w)], ip_v)

        def read(j, slot):
            return pltpu.make_async_copy(xp_hbm.at[pl.ds((wid * per_w + j) * chunk, chunk)], rows_v.at[slot], rsem)

        def scatter(idx_v, j, slot):
            copies = [pltpu.async_copy(rows_v.at[slot], out_hbm.at[idx_v.at[kx, j]], wsem) for kx in range(TOP_K)]
            for cp in copies:
                cp.wait()

        read(0, 0).start()

        def body(h, carry):
            j = 2 * h
            read(j, 0).wait()
            read(j + 1, 1).start()
            scatter(ip_v, j, 0)
            read(j + 1, 1).wait()

            @pl.when(j + 2 < per_w)
            def _():
                read(j + 2, 0).start()

            scatter(ip_v, j + 1, 1)
            return carry

        lax.fori_loop(0, per_w // 2, body, 0)

        @pl.when(wid < ns_chunks)
        def _():
            pltpu.sync_copy(d_hbm.at[pl.ds(0, TOP_K), pl.ds(n_p // chunk + wid, 1)], is_v)
            pltpu.sync_copy(xs_hbm.at[pl.ds(wid * chunk, chunk)], rows_v.at[0])
            scatter(is_v, 0, 0)

    return run(x_p, x_s, d3)


def _sc_collect(y_sorted, dest_t, n_p, n_s):
    chunk = SC_ROWS
    per_choice = SC_WORKERS // TOP_K
    per_w = n_p // (per_choice * chunk)
    assert per_w * per_choice * chunk == n_p and per_w % 2 == 0
    assert n_s == per_choice * chunk
    d3 = dest_t.reshape(dest_t.shape[0], (n_p + n_s) // chunk, chunk)
    width, dtype = y_sorted.shape[1], y_sorted.dtype

    @functools.partial(
        pl.kernel, mesh=_sc_mesh(),
        out_type=[jax.ShapeDtypeStruct((TOP_K * n_p, width), dtype), jax.ShapeDtypeStruct((TOP_K * n_s, width), dtype)],
        scratch_types=[pltpu.VMEM((per_w, chunk), jnp.int32),
                       pltpu.VMEM((1, chunk), jnp.int32),
                       pltpu.VMEM((2, chunk, width), dtype),
                       pltpu.SemaphoreType.DMA, pltpu.SemaphoreType.DMA],
        compiler_params=pltpu.CompilerParams(use_tc_tiling_on_sc=True),
        name="collect")
    def run(y_hbm, d_hbm, op_hbm, os_hbm, ip_v, is_v, rows_v, gsem, wsem):
        wid = _sc_worker()
        choice = wid // per_choice
        part = wid % per_choice
        pltpu.sync_copy(d_hbm.at[choice, pl.ds(part * per_w, per_w)], ip_v)
        pltpu.sync_copy(d_hbm.at[choice, pl.ds(n_p // chunk + part, 1)], is_v)

        def gather(idx_v, j, slot):
            return pltpu.make_async_copy(y_hbm.at[idx_v.at[j]], rows_v.at[slot], gsem)

        def write(j, slot):
            return pltpu.make_async_copy(rows_v.at[slot], op_hbm.at[pl.ds((wid * per_w + j) * chunk, chunk)], wsem)

        gather(ip_v, 0, 0).start()

        def body(h, carry):
            j = 2 * h
            gather(ip_v, j, 0).wait()

            @pl.when(h > 0)
            def _():
                write(j - 1, 1).wait()

            gather(ip_v, j + 1, 1).start()
            write(j, 0).start()
            gather(ip_v, j + 1, 1).wait()
            write(j, 0).wait()

            @pl.when(j + 2 < per_w)
            def _():
                gather(ip_v, j + 2, 0).start()

            write(j + 1, 1).start()
            return carry

        lax.fori_loop(0, per_w // 2, body, 0)
        write(per_w - 1, 1).wait()

        gather(is_v, 0, 0).start()
        gather(is_v, 0, 0).wait()
        pltpu.sync_copy(rows_v.at[0], os_hbm.at[pl.ds(wid * chunk, chunk)])

    return run(y_sorted, d3)


def _expert_body(blk0_ref, nblk_ref, cnt_ref, wup_ref, wdn_ref, bup_ref, bdn_ref, x_hbm, y_hbm,
                 wup_s, wdn_s, xbuf, obuf, in_sem, out_sem):
    e = pl.program_id(0)
    nb = nblk_ref[e]
    blk0 = blk0_ref[e]
    cnt = cnt_ref[e]
    tm = EXPERT_ROWS
    pair = 2 * LANES

    def x_copy(i, slot):
        rows = pl.ds(pl.multiple_of((blk0 + i) * tm, tm), tm)
        return pltpu.make_async_copy(x_hbm.at[rows], xbuf.at[slot], in_sem.at[slot])

    def y_copy(i, slot):
        rows = pl.ds(pl.multiple_of((blk0 + i) * tm, tm), tm)
        return pltpu.make_async_copy(obuf.at[slot], y_hbm.at[rows], out_sem.at[slot])

    @pl.when(nb > 0)
    def _():
        x_copy(0, 0).start(priority=1)
        r = lax.broadcasted_iota(jnp.int32, (pair, pair), 0)
        c = lax.broadcasted_iota(jnp.int32, (pair, pair), 1)
        perm = (r == jnp.where(c < LANES, 2 * c, 2 * (c - LANES) + 1)).astype(BF16)
        for g in range(2 * D_MODEL // pair):
            cols = slice(g * pair, (g + 1) * pair)
            wup_s[g] = jnp.dot(wup_ref[0, :, cols].astype(BF16), perm, preferred_element_type=F32).astype(BF16)
        for g in range(D_MODEL // pair):
            wdn_s[g] = wdn_ref[0, :, g * pair:(g + 1) * pair].astype(BF16)

        def block(i, carry):
            slot = i % 2
            x_copy(i, slot).wait()

            @pl.when(i + 1 < nb)
            def _():
                x_copy(i + 1, 1 - slot).start(priority=1)

            @pl.when(i >= 2)
            def _():
                y_copy(i - 2, slot).wait()

            row = lax.broadcasted_iota(jnp.int32, (tm, 1), 0)
            x = jnp.where(row < cnt - i * tm, _unpack_bf16_pair(xbuf[slot]), 0.0).astype(BF16)
            acts = []
            for g in range(2 * D_MODEL // pair):
                cols = slice(g * pair, (g + 1) * pair)
                h = jnp.dot(x, wup_s[g], preferred_element_type=F32) + bup_ref[0, :, cols]
                glu = jnp.minimum(h[:, :LANES], SWIGLU_LIMIT)
                lin = jnp.clip(h[:, LANES:], -SWIGLU_LIMIT, SWIGLU_LIMIT)
                acts.append((glu * jax.nn.sigmoid(SWIGLU_ALPHA * glu) * (lin + 1.0)).astype(BF16))
            act = jnp.concatenate(acts, axis=1)
            half_groups = D_MODEL // pair // 2
            for g in range(half_groups):
                ys = []
                for gg in (g, g + half_groups):
                    cols = slice(gg * pair, (gg + 1) * pair)
                    ys.append(jnp.dot(act, wdn_s[gg], preferred_element_type=F32) + bdn_ref[0, :, cols])
                obuf[slot, :, g * pair:(g + 1) * pair] = _pack_bf16_pair(ys[0], ys[1])
            y_copy(i, slot).start(priority=1)
            return carry

        lax.fori_loop(0, nb, block, 0)

        @pl.when(nb >= 2)
        def _():
            y_copy(nb - 2, nb % 2).wait()

        y_copy(nb - 1, (nb - 1) % 2).wait()


def _experts(blk0, nblk, cnt, x_sorted, w_up, w_down, b_up_grouped, b_down):
    tm = EXPERT_ROWS
    per_expert = lambda a: pl.BlockSpec((1,) + a.shape[1:], lambda e, b0, nb, ct: (e, 0, 0))
    grid_spec = pltpu.PrefetchScalarGridSpec(
        num_scalar_prefetch=3,
        grid=(N_EXPERTS,),
        in_specs=[per_expert(w_up), per_expert(w_down), per_expert(b_up_grouped), per_expert(b_down),
                  pl.BlockSpec(memory_space=pl.ANY)],
        out_specs=pl.BlockSpec(memory_space=pl.ANY),
        scratch_shapes=[pltpu.VMEM((2 * D_MODEL // (2 * LANES), D_MODEL, 2 * LANES), BF16),
                        pltpu.VMEM((D_MODEL // (2 * LANES), D_MODEL, 2 * LANES), BF16),
                        pltpu.VMEM((2, tm, x_sorted.shape[1]), x_sorted.dtype),
                        pltpu.VMEM((2, tm, D_MODEL // 2), jnp.uint32),
                        pltpu.SemaphoreType.DMA((2,)), pltpu.SemaphoreType.DMA((2,))],
    )
    return pl.pallas_call(
        _expert_body,
        grid_spec=grid_spec,
        out_shape=jax.ShapeDtypeStruct((x_sorted.shape[0], D_MODEL // 2), jnp.uint32),
        compiler_params=_params(("arbitrary",)),
        name="experts",
    )(blk0, nblk, cnt, w_up, w_down, b_up_grouped, b_down, x_sorted)


def _combine_body(gate_ref, xmid_ref, gfin_ref, y0_ref, y1_ref, y2_ref, y3_ref, out_ref):
    gate = gate_ref[...]
    moe = _unpack_bf16_pair(y0_ref[...]) * gate[:, 0:1]
    for kx, y_ref in enumerate((y1_ref, y2_ref, y3_ref), start=1):
        moe = moe + _unpack_bf16_pair(y_ref[...]) * gate[:, kx:kx + 1]
    out_ref[...] = _rms(xmid_ref[...] + moe, gfin_ref[...])


def _combine(gates, first_token, xmid, gfin, y_rows):
    n = xmid.shape[0]
    tt = COMBINE_ROWS
    nblk = n // tt
    blk0 = first_token // tt
    assert blk0 * tt == first_token
    choice = lambda kx: pl.BlockSpec((tt, y_rows.shape[1]), lambda i: (i + kx * nblk, 0))
    return pl.pallas_call(
        _combine_body,
        grid=(nblk,),
        in_specs=[pl.BlockSpec((tt, LANES), lambda i: (i + blk0, 0)),
                  pl.BlockSpec((tt, D_MODEL), lambda i: (i, 0)),
                  pl.BlockSpec((1, D_MODEL), lambda i: (0, 0))] + [choice(kx) for kx in range(TOP_K)],
        out_specs=pl.BlockSpec((tt, D_MODEL), lambda i: (i, 0)),
        out_shape=jax.ShapeDtypeStruct((n, D_MODEL), F32),
        compiler_params=_params(("arbitrary",)),
        name="combine",
    )(gates, xmid, gfin, y_rows, y_rows, y_rows, y_rows)


def kernel(x_prompt, x_sample, cache_k_win, cache_v_win, norm_attn_g, w_in, ln_v_g, ln_v_b, w_spatial, b_spatial,
           attn_sinks, w_out, norm_ffn_g, w_router, b_router, w_up, b_up, w_down, b_down, norm_final_g):
    bp, tp, _ = x_prompt.shape
    bs, ts, _ = x_sample.shape
    w_buf = cache_k_win.shape[2]
    assert bp == 1 and tp % MIX_ROWS == 0 and w_buf == WINDOW and (bs * ts) % PROJ_ROWS == 0 and 8 % ts == 0
    n_p, n_s = bp * tp, bs * ts
    row2 = lambda a: a.reshape(1, -1)

    w_in_bf = w_in[0].astype(BF16)
    w_out_bf = w_out[0].astype(BF16)
    tril = jnp.tril(jnp.ones((CHUNK, CHUNK), dtype=bool))
    wsp = jnp.where(tril[None], w_spatial[0], 0.0)
    wsp_bf = wsp.astype(BF16)
    bsp = jnp.broadcast_to(b_spatial[0][:, :, None], (GMLP_GROUPS, CHUNK, LANES))
    b_up_grouped = b_up[0].reshape(N_EXPERTS, -1, LANES, 2).transpose(0, 1, 3, 2).reshape(N_EXPERTS, 1, -1)
    bd = b_down[0][:, None, :]
    g1, g2, gfin = row2(norm_attn_g[0]), row2(norm_ffn_g[0]), row2(norm_final_g)
    lng, lnb = row2(ln_v_g[0]), row2(ln_v_b[0])
    wr_hi = w_router[0].astype(BF16)
    wr = jnp.concatenate([wr_hi, (w_router[0] - wr_hi.astype(F32)).astype(BF16)], axis=1)
    br = row2(b_router[0])
    sinks = attn_sinks[0]

    xp = x_prompt.reshape(n_p, D_MODEL)
    cs_p = _rotary_inputs(jnp.arange(tp, dtype=jnp.int32))
    q_p, k_p, v_p, a_p, vn_p, sgb_p = _proj(xp, g1, w_in_bf, cs_p, lng, lnb)
    xmid_p, xn2_p, lg_p = _mix(sinks, q_p, k_p, v_p, a_p, vn_p, sgb_p, xp, wsp_bf, bsp, w_out_bf, g2, wr, br)

    xs = x_sample.reshape(n_s, D_MODEL)
    pos_s = PAST_LEN + jnp.arange(ts, dtype=jnp.int32)
    cs_s = _rotary_inputs(jnp.tile(pos_s, bs))
    q_s, k_s, v_s, a_s, vn_s, sgb_s = _proj(xs, g1, w_in_bf, cs_s, lng, lnb)
    k_all = jnp.concatenate([cache_k_win[0], k_s.reshape(bs, ts, N_KV_HEADS, HEAD_DIM)], axis=1)
    v_all = jnp.concatenate([cache_v_win[0], v_s.reshape(bs, ts, N_KV_HEADS, HEAD_DIM)], axis=1)
    n_keys = w_buf + ts
    key_pad = (-n_keys) % 8
    to_heads = lambda t: jnp.pad(t, ((0, 0), (0, key_pad), (0, 0), (0, 0))).transpose(2, 0, 1, 3).astype(BF16)
    qh = q_s.reshape(bs, ts, N_KV_HEADS, GQA_GROUP, HEAD_DIM).transpose(2, 0, 1, 3, 4)
    qh = qh.reshape(N_KV_HEADS, bs, ts * GQA_GROUP, HEAD_DIM)
    sink_rows = jnp.tile(sinks.reshape(N_KV_HEADS, 1, GQA_GROUP), (1, ts, 1)).reshape(N_KV_HEADS, ts * GQA_GROUP, 1)
    oh = _sample_attn(sink_rows, qh, to_heads(k_all), to_heads(v_all))
    o_s = oh.reshape(N_KV_HEADS, bs, ts, GQA_GROUP, HEAD_DIM).transpose(1, 2, 0, 3, 4).reshape(n_s, Q_WIDTH)
    t_idx = jnp.arange(ts)
    coef = jnp.stack([jnp.where((t_idx >= d)[None, :], wsp[:, t_idx, jnp.maximum(t_idx - d, 0)], 0.0)
                      for d in range(ts)])
    coef = jnp.repeat(coef.transpose(0, 2, 1), GMLP_WIDTH // GMLP_GROUPS, axis=2)
    coef = jnp.tile(coef, (1, 8 // ts, 1))
    bias = jnp.tile(jnp.repeat(b_spatial[0][:, :ts].T, GMLP_WIDTH // GMLP_GROUPS, axis=1), (8 // ts, 1))
    xmid_s, xn2_s, lg_s = _mix_sample(a_s, vn_s, sgb_s, o_s, xs, coef, bias, w_out_bf, g2, wr, br)

    n_tok = n_p + n_s
    gate_w, dest_t, meta = _route(lg_p, lg_s)
    tm = EXPERT_ROWS
    n_blocks = (n_tok * TOP_K) // tm + N_EXPERTS

    x_sorted = _sc_dispatch(xn2_p, xn2_s, dest_t, n_blocks * tm)
    y_sorted = _experts(meta[:, 0], meta[:, 1], meta[:, 2], x_sorted, w_up[0], w_down[0], b_up_grouped, bd)
    yrows_p, yrows_s = _sc_collect(y_sorted, dest_t, n_p, n_s)
    y_p = _combine(gate_w, 0, xmid_p, gfin, yrows_p)
    y_s = _combine(gate_w, n_p, xmid_s, gfin, yrows_s)

    k4 = lambda t: t.reshape(1, bp, -1, N_KV_HEADS, HEAD_DIM)
    return (y_p.reshape(bp, tp, D_MODEL),
            y_s.reshape(bs, ts, D_MODEL),
            k4(k_p[n_p - WINDOW:]),
            k4(v_p[n_p - WINDOW:]),
            vn_p[n_p - CHUNK:].reshape(1, bp, CHUNK, GMLP_WIDTH),
            k_all[None, :, ts:],
            v_all[None, :, ts:],
            vn_s.reshape(1, bs, ts, GMLP_WIDTH))
```

```python
import functools

import numpy as np
import jax
import jax.numpy as jnp
from jax import lax
from jax.experimental import pallas as pl
from jax.experimental.pallas import tpu as pltpu
from jax.experimental.pallas import tpu_sc as plsc

F32 = jnp.float32
BF16 = jnp.bfloat16

D_MODEL = 1024
HEAD_DIM = 64
N_HEADS = 16
GQA_GROUP = 8
N_KV_HEADS = 2
Q_WIDTH = 1024
KV_WIDTH = 128
WINDOW = 128
ROT_DIM = 16
ROPE_THETA = 500000.0
CHUNK = 128
GMLP_WIDTH = 1024
GMLP_GROUPS = 8
N_EXPERTS = 32
TOP_K = 4
SWIGLU_LIMIT = 7.0
SWIGLU_ALPHA = 1.702
RMS_EPS = 1e-5
LN_EPS = 1e-5
NEG_INF = -1e30
PAST_LEN = 16384

LANES = 128
VMEM_LIMIT = 56 * 1024 * 1024

PROJ_ROWS = 256
MIX_ROWS = 512
ROUTE_ROWS = 512
EXPERT_ROWS = 256
COMBINE_ROWS = 256

SC_CORES = 2
SC_WORKERS = 32
SC_ROWS = 64

_C_Q, _C_KV, _C_U, _C_VG, _C_GA, _C_GB, _C_END = 0, 1024, 1280, 2304, 3328, 4352, 5376


def _params(sem):
    return pltpu.CompilerParams(dimension_semantics=sem, vmem_limit_bytes=VMEM_LIMIT)


def _rms(x, g):
    return x * lax.rsqrt(jnp.mean(x * x, axis=-1, keepdims=True) + RMS_EPS) * g


def _pack_bf16_pair(lo, hi):
    lo_bits = lax.bitcast_convert_type(lo.astype(BF16).astype(F32), jnp.uint32)
    hi_bits = lax.bitcast_convert_type(hi.astype(BF16).astype(F32), jnp.uint32)
    return (lo_bits >> 16) | hi_bits


def _unpack_bf16_pair(words):
    lo = lax.bitcast_convert_type(words << 16, F32)
    hi = lax.bitcast_convert_type(words & jnp.uint32(0xFFFF0000), F32)
    return jnp.concatenate([lo, hi], axis=1)


def _proj_body(x_ref, g_ref, w_ref, cs_ref, rot_ref, lng_ref, lnb_ref,
               q_ref, k_ref, v_ref, a_ref, vn_ref, sgb_ref):
    h = _rms(x_ref[...], g_ref[...]).astype(BF16)
    tabs = lax.dot_general(cs_ref[...], rot_ref[...], (((0,), (0,)), ((), ())),
                           preferred_element_type=F32, precision=lax.Precision.HIGHEST)
    rc, rs1, rs2 = tabs[:, :LANES], tabs[:, LANES:2 * LANES], tabs[:, 2 * LANES:]

    def rot(z):
        return z * rc + pltpu.roll(z, LANES - ROT_DIM // 2, 1) * rs1 + pltpu.roll(z, ROT_DIM // 2, 1) * rs2

    def mm(lo, hi):
        return jnp.dot(h, w_ref[:, lo:hi], preferred_element_type=F32)

    zq = mm(_C_Q, _C_KV)
    for c in range(Q_WIDTH // LANES):
        sl = slice(c * LANES, (c + 1) * LANES)
        q_ref[:, sl] = (rot(zq[:, sl]) * (HEAD_DIM ** -0.5)).astype(BF16)
    zkv = mm(_C_KV, _C_U)
    k_ref[...] = rot(zkv[:, :KV_WIDTH])
    v_ref[...] = zkv[:, KV_WIDTH:]
    a_ref[...] = jax.nn.sigmoid(mm(_C_GA, _C_GB)) * jax.nn.gelu(mm(_C_U, _C_VG))
    zv = jax.nn.gelu(mm(_C_VG, _C_GA))
    zc = zv - jnp.mean(zv, axis=-1, keepdims=True)
    var = jnp.mean(zc * zc, axis=-1, keepdims=True)
    vn_ref[...] = zc * lax.rsqrt(var + LN_EPS) * lng_ref[...] + lnb_ref[...]
    sgb_ref[...] = jax.nn.sigmoid(mm(_C_GB, _C_END))


def _proj(x, norm_g, w_in_bf, cs, ln_g, ln_b):
    n = x.shape[0]
    tm = PROJ_ROWS
    row = lambda w: pl.BlockSpec((tm, w), lambda i: (i, 0))
    full = lambda a: pl.BlockSpec(a.shape, lambda i: (0,) * a.ndim)
    rot = jnp.asarray(_ROT_EXPAND)
    return pl.pallas_call(
        _proj_body,
        grid=(n // tm,),
        in_specs=[row(D_MODEL), full(norm_g), full(w_in_bf), pl.BlockSpec((cs.shape[0], tm), lambda i: (0, i)),
                  full(rot),
                  full(ln_g), full(ln_b)],
        out_specs=[row(Q_WIDTH), row(KV_WIDTH), row(KV_WIDTH), row(GMLP_WIDTH), row(GMLP_WIDTH), row(D_MODEL)],
        out_shape=[jax.ShapeDtypeStruct((n, Q_WIDTH), BF16),
                   jax.ShapeDtypeStruct((n, KV_WIDTH), F32),
                   jax.ShapeDtypeStruct((n, KV_WIDTH), F32),
                   jax.ShapeDtypeStruct((n, GMLP_WIDTH), F32),
                   jax.ShapeDtypeStruct((n, GMLP_WIDTH), F32),
                   jax.ShapeDtypeStruct((n, D_MODEL), F32)],
        compiler_params=_params(("arbitrary",)),
        name="proj",
    )(x, norm_g, w_in_bf, cs, rot, ln_g, ln_b)


_ROT_COLS = 32


def _rot_expand():
    half = ROT_DIM // 2
    m = np.zeros((_ROT_COLS, 3 * LANES), np.float32)
    for lane in range(LANES):
        d = lane % HEAD_DIM
        if d < ROT_DIM:
            m[d % half, lane] = 1.0
        else:
            m[2 * half, lane] = 1.0
        if d < half:
            m[half + d, LANES + lane] = -1.0
        elif d < ROT_DIM:
            m[half + d - half, 2 * LANES + lane] = 1.0
    return m


_ROT_EXPAND = _rot_expand()


def _rotary_inputs(pos):
    half = ROT_DIM // 2
    inv_freq = ROPE_THETA ** (-jnp.arange(half, dtype=F32) / half)
    ang = inv_freq[:, None] * pos.astype(F32)[None, :]
    n = pos.shape[0]
    return jnp.concatenate([jnp.cos(ang), jnp.sin(ang), jnp.ones((1, n), F32),
                            jnp.zeros((_ROT_COLS - 2 * half - 1, n), F32)], axis=0)


def _finish_rows(merged_bf, x, wout_ref, g2_ref, wr_ref, br_ref, xmid_ref, xn2_ref, lg_ref):
    xm = x + jnp.dot(merged_bf, wout_ref[...], preferred_element_type=F32)
    xmid_ref[...] = xm
    xn = _rms(xm, g2_ref[...])
    x_hi = xn.astype(BF16)
    x_lo = (xn - x_hi.astype(F32)).astype(BF16)
    w_hl = wr_ref[...]
    p_hi = jnp.dot(x_hi, w_hl, preferred_element_type=F32)
    p_lo = jnp.dot(x_lo, w_hl[:, :N_EXPERTS], preferred_element_type=F32)
    lg = p_hi[:, :N_EXPERTS] + (p_hi[:, N_EXPERTS:] + p_lo) + br_ref[...]
    wide = jnp.concatenate([lg, jnp.zeros((lg.shape[0], LANES - N_EXPERTS), F32)], axis=1)
    lg_ref[...] = wide.T[:N_EXPERTS]
    xn2_ref[...] = _pack_bf16_pair(xn[:, :D_MODEL // 2], xn[:, D_MODEL // 2:])


def _mix_body(sinks_ref, q_ref, k_ref, kp_ref, v_ref, vp_ref, a_ref, vn_ref, sgb_ref, x_ref,
              wsp_ref, bsp_ref, wout_ref, g2_ref, wr_ref, br_ref,
              xmid_ref, xn2_ref, lg_ref, kcat, vcat, mrg):
    i = pl.program_id(0)
    nsub = MIX_ROWS // WINDOW
    kcat[0:WINDOW] = kp_ref[...]
    kcat[WINDOW:] = k_ref[...]
    vcat[0:WINDOW] = vp_ref[...]
    vcat[WINDOW:] = v_ref[...]

    pair_rows = (GQA_GROUP // 2) * WINDOW
    rq = lax.broadcasted_iota(jnp.int32, (pair_rows, 4 * WINDOW), 0) & (WINDOW - 1)
    ck = lax.broadcasted_iota(jnp.int32, (pair_rows, 4 * WINDOW), 1) & (2 * WINDOW - 1)
    band = (ck > rq) & (ck <= rq + WINDOW)
    lane_kv = lax.broadcasted_iota(jnp.int32, (2 * WINDOW, LANES), 1)
    lane_o = lax.broadcasted_iota(jnp.int32, (pair_rows, LANES), 1)
    row_p = lax.broadcasted_iota(jnp.int32, (pair_rows, 1), 0) >> 7

    def sub(j, carry):
        off = pl.multiple_of(j * WINDOW, WINDOW)
        rows = pl.ds(off, WINDOW)
        for g in range(GMLP_GROUPS):
            cols = slice(g * LANES, (g + 1) * LANES)
            s = jnp.dot(wsp_ref[g], vn_ref[rows, cols].astype(BF16), preferred_element_type=F32) + bsp_ref[g]
            mrg[rows, cols] = a_ref[rows, cols] * s
        kblk = kcat[pl.ds(off, 2 * WINDOW), :]
        vblk = vcat[pl.ds(off, 2 * WINDOW), :]
        kswp = pltpu.roll(kblk, HEAD_DIM, 1)
        vswp = pltpu.roll(vblk, HEAD_DIM, 1)
        kmin = jnp.where(jnp.logical_and(i == 0, j == 0), WINDOW, 0)
        allowed = band & (ck >= kmin)
        for kk in range(N_KV_HEADS):
            lo_src, hi_src = (kblk, kswp) if kk == 0 else (kswp, kblk)
            kbd = jnp.concatenate([jnp.where(lane_kv < HEAD_DIM, lo_src, 0.0),
                                   jnp.where(lane_kv >= HEAD_DIM, hi_src, 0.0)], axis=0).astype(BF16)
            lo_src, hi_src = (vblk, vswp) if kk == 0 else (vswp, vblk)
            vbd = jnp.concatenate([jnp.where(lane_kv < HEAD_DIM, lo_src, 0.0),
                                   jnp.where(lane_kv >= HEAD_DIM, hi_src, 0.0)], axis=0).astype(BF16)
            pair0 = kk * (GQA_GROUP // 2)
            qs = jnp.concatenate([q_ref[rows, (pair0 + p) * LANES:(pair0 + p + 1) * LANES]
                                  for p in range(GQA_GROUP // 2)], axis=0)
            lg = lax.dot_general(qs, kbd, (((1,), (1,)), ((), ())), preferred_element_type=F32)
            lg = jnp.where(allowed, lg, NEG_INF)
            h0 = kk * GQA_GROUP
            se = jnp.full((pair_rows, 1), sinks_ref[h0], F32)
            so = jnp.full((pair_rows, 1), sinks_ref[h0 + 1], F32)
            for p in range(1, GQA_GROUP // 2):
                se = jnp.where(row_p == p, sinks_ref[h0 + 2 * p], se)
                so = jnp.where(row_p == p, sinks_ref[h0 + 2 * p + 1], so)
            le, lo = lg[:, :2 * WINDOW], lg[:, 2 * WINDOW:]
            me = jnp.maximum(jnp.max(le, axis=1, keepdims=True), se)
            mo = jnp.maximum(jnp.max(lo, axis=1, keepdims=True), so)
            pe = jnp.exp(le - me)
            po = jnp.exp(lo - mo)
            de = jnp.sum(pe, axis=1, keepdims=True) + jnp.exp(se - me)
            do = jnp.sum(po, axis=1, keepdims=True) + jnp.exp(so - mo)
            pr = jnp.concatenate([pe, po], axis=1).astype(BF16)
            o = jnp.dot(pr, vbd, preferred_element_type=F32)
            o = o / jnp.where(lane_o < HEAD_DIM, de, do)
            for p in range(GQA_GROUP // 2):
                cols = slice((pair0 + p) * LANES, (pair0 + p + 1) * LANES)
                mrg[rows, cols] += sgb_ref[rows, cols] * o[p * WINDOW:(p + 1) * WINDOW]
        return carry

    lax.fori_loop(0, nsub, sub, 0)
    _finish_rows(mrg[...].astype(BF16), x_ref[...], wout_ref, g2_ref, wr_ref, br_ref, xmid_ref, xn2_ref, lg_ref)


def _mix(sinks, q, k, v, a, vn, sgb, x, wsp, bsp, wout, g2, wr, br):
    n = x.shape[0]
    tm = MIX_ROWS
    nsub = tm // WINDOW
    row = lambda w: pl.BlockSpec((tm, w), lambda i: (i, 0))
    prev = pl.BlockSpec((WINDOW, KV_WIDTH), lambda i: (jnp.maximum(i * nsub - 1, 0), 0))
    full = lambda arr: pl.BlockSpec(arr.shape, lambda i: (0,) * arr.ndim)
    smem = pl.BlockSpec(memory_space=pltpu.SMEM)
    return pl.pallas_call(
        _mix_body,
        grid=(n // tm,),
        in_specs=[smem, row(Q_WIDTH), row(KV_WIDTH), prev, row(KV_WIDTH), prev,
                  row(GMLP_WIDTH), row(GMLP_WIDTH), row(D_MODEL), row(D_MODEL),
                  full(wsp), full(bsp), full(wout), full(g2), full(wr), full(br)],
        out_specs=[row(D_MODEL), row(D_MODEL // 2), pl.BlockSpec((N_EXPERTS, tm), lambda i: (0, i))],
        out_shape=[jax.ShapeDtypeStruct((n, D_MODEL), F32),
                   jax.ShapeDtypeStruct((n, D_MODEL // 2), jnp.uint32),
                   jax.ShapeDtypeStruct((N_EXPERTS, n), F32)],
        scratch_shapes=[pltpu.VMEM((tm + WINDOW, KV_WIDTH), F32),
                        pltpu.VMEM((tm + WINDOW, KV_WIDTH), F32),
                        pltpu.VMEM((tm, D_MODEL), F32)],
        compiler_params=_params(("arbitrary",)),
        name="mix_prompt",
    )(sinks, q, k, k, v, v, a, vn, sgb, x, wsp, bsp, wout, g2, wr, br)


def _sample_attn_body(sink_ref, q_ref, k_ref, v_ref, o_ref):
    q = q_ref[...]
    k = k_ref[...]
    v = v_ref[...]
    nq, nk = q.shape[1], k.shape[1]
    heads = lambda t, kk: t[:, :, kk * HEAD_DIM:(kk + 1) * HEAD_DIM].astype(BF16)
    row = lax.broadcasted_iota(jnp.int32, (1, nq, nk), 1)
    first_kv = (row % N_HEADS) < GQA_GROUP
    lg = jnp.where(first_kv,
                   jnp.einsum("bqd,bkd->bqk", q, heads(k, 0), preferred_element_type=F32),
                   jnp.einsum("bqd,bkd->bqk", q, heads(k, 1), preferred_element_type=F32))
    t = row // N_HEADS
    j = lax.broadcasted_iota(jnp.int32, (1, nq, nk), 2)
    lg = jnp.where((j > t) & (j <= t + WINDOW), lg, NEG_INF)
    sink = sink_ref[...][None]
    m = jnp.maximum(jnp.max(lg, axis=2, keepdims=True), sink)
    p = jnp.exp(lg - m)
    den = jnp.sum(p, axis=2, keepdims=True) + jnp.exp(sink - m)
    pb = p.astype(BF16)
    row_o = lax.broadcasted_iota(jnp.int32, (1, nq, HEAD_DIM), 1)
    o = jnp.where((row_o % N_HEADS) < GQA_GROUP,
                  jnp.einsum("bqk,bkd->bqd", pb, heads(v, 0), preferred_element_type=F32),
                  jnp.einsum("bqk,bkd->bqd", pb, heads(v, 1), preferred_element_type=F32))
    o_ref[...] = o / den


def _sample_attn(sink_col, q3, k_all, v_all):
    nb = q3.shape[0]
    bb = 32
    blk = lambda a: pl.BlockSpec((bb,) + a.shape[1:], lambda b: (b, 0, 0))
    return pl.pallas_call(
        _sample_attn_body,
        grid=(nb // bb,),
        in_specs=[pl.BlockSpec(sink_col.shape, lambda b: (0, 0)), blk(q3), blk(k_all), blk(v_all)],
        out_specs=blk(q3),
        out_shape=jax.ShapeDtypeStruct(q3.shape, F32),
        compiler_params=_params(("arbitrary",)),
        name="attn_sample",
    )(sink_col, q3, k_all, v_all)


def _mix_sample_body(a_ref, vn_ref, sgb_ref, o_ref, x_ref, coef_ref, bias_ref,
                     wout_ref, g2_ref, wr_ref, br_ref, xmid_ref, xn2_ref, lg_ref):
    vn = vn_ref[...]
    n, width = vn.shape
    rows8 = lambda t: t.reshape(n // 8, 8, width)
    s = bias_ref[...][None] + coef_ref[0][None] * rows8(vn)
    for d in range(1, coef_ref.shape[0]):
        s = s + coef_ref[d][None] * rows8(pltpu.roll(vn, d, 0))
    merged = a_ref[...] * s.reshape(n, width) + sgb_ref[...] * o_ref[...]
    _finish_rows(merged.astype(BF16), x_ref[...], wout_ref, g2_ref, wr_ref, br_ref, xmid_ref, xn2_ref, lg_ref)


def _mix_sample(a, vn, sgb, o, x, coef, bias, wout, g2, wr, br):
    n = x.shape[0]
    args = (a, vn, sgb, o, x, coef, bias, wout, g2, wr, br)
    full = lambda arr: pl.BlockSpec(arr.shape, lambda i: (0,) * arr.ndim)
    return pl.pallas_call(
        _mix_sample_body,
        grid=(1,),
        in_specs=[full(arr) for arr in args],
        out_specs=[pl.BlockSpec((n, D_MODEL), lambda i: (0, 0)), pl.BlockSpec((n, D_MODEL // 2), lambda i: (0, 0)),
                   pl.BlockSpec((N_EXPERTS, n), lambda i: (0, 0))],
        out_shape=[jax.ShapeDtypeStruct((n, D_MODEL), F32),
                   jax.ShapeDtypeStruct((n, D_MODEL // 2), jnp.uint32),
                   jax.ShapeDtypeStruct((N_EXPERTS, n), F32)],
        compiler_params=_params(("arbitrary",)),
        name="mix_sample",
    )(*args)


def _rows8(rows, dtype):
    n = rows[0].shape[1]
    sub = lax.broadcasted_iota(jnp.int32, (8, n), 0)
    out = jnp.zeros((8, n), dtype)
    for kx, r in enumerate(rows):
        out = jnp.where(sub == kx, r.astype(dtype), out)
    return out


def _route_body(nblk_p, nblk, lgp_ref, lgs_ref, gate_ref, dest_ref, meta_ref, idx_s, rank_s, base):
    i = pl.program_id(0)

    @pl.when(i == 0)
    def _():
        base[...] = jnp.zeros_like(base)

    l = jnp.where(jnp.full(lgp_ref.shape, i, jnp.int32) < nblk_p, lgp_ref[...], lgs_ref[...])
    tb = l.shape[1]
    sub = lax.broadcasted_iota(jnp.int32, l.shape, 0).astype(F32)
    vals, idxs, sels = [], [], []
    for _ in range(TOP_K):
        m = jnp.max(l, axis=0, keepdims=True)
        ik = jnp.min(jnp.where(l == m, sub, float(N_EXPERTS)), axis=0, keepdims=True)
        sel = sub == ik
        l = jnp.where(sel, -jnp.inf, l)
        vals.append(m)
        idxs.append(ik)
        sels.append(sel)
    es = [jnp.exp(vk - vals[0]) for vk in vals]
    den = es[0] + es[1] + es[2] + es[3]
    onehot = jnp.zeros(l.shape, F32)
    for sel in sels:
        onehot = onehot + sel.astype(F32)
    earlier = (lax.broadcasted_iota(jnp.int32, (tb, tb), 0) < lax.broadcasted_iota(jnp.int32, (tb, tb), 1))
    before = jnp.dot(onehot.astype(BF16), earlier.astype(BF16), preferred_element_type=F32) + base[...]
    ranks = [jnp.sum(jnp.where(sel, before, 0.0), axis=0, keepdims=True) for sel in sels]
    base[...] += jnp.sum(onehot, axis=1, keepdims=True)
    idx_s[i] = _rows8(idxs, F32)
    rank_s[i] = _rows8(ranks, F32)
    gates = jnp.concatenate([_rows8([e / den for e in es], F32), jnp.zeros((LANES - 8, tb), F32)], axis=0)
    gate_ref[...] = gates.T

    @pl.when(i == nblk - 1)
    def _():
        cnt = base[...]
        padded = jnp.ceil(cnt / EXPERT_ROWS) * EXPERT_ROWS
        lower = (lax.broadcasted_iota(jnp.int32, (N_EXPERTS, N_EXPERTS), 1) <
                 lax.broadcasted_iota(jnp.int32, (N_EXPERTS, N_EXPERTS), 0)).astype(F32)
        pstart = jnp.dot(lower, jnp.broadcast_to(padded, (N_EXPERTS, LANES)), preferred_element_type=F32,
                         precision=lax.Precision.HIGHEST)[:, :1]
        lane = lax.broadcasted_iota(jnp.int32, (N_EXPERTS, LANES), 1)
        meta = jnp.where(lane == 0, pstart / EXPERT_ROWS,
                         jnp.where(lane == 1, padded / EXPERT_ROWS, jnp.where(lane == 2, cnt, 0.0)))
        meta_ref[...] = meta.astype(jnp.int32)
        sub_e = lax.broadcasted_iota(jnp.int32, (N_EXPERTS, tb), 0).astype(F32)
        for b in range(nblk):
            idx, rank = idx_s[b], rank_s[b]
            rows = [jnp.sum(jnp.where(sub_e == idx[kx:kx + 1], pstart, 0.0), axis=0, keepdims=True)
                    + rank[kx:kx + 1] for kx in range(TOP_K)]
            dest_ref[:, b * tb:(b + 1) * tb] = _rows8(rows, jnp.int32)


def _route(logits_p, logits_s):
    tb = ROUTE_ROWS
    nblk_p, nblk_s = logits_p.shape[1] // tb, logits_s.shape[1] // tb
    nblk = nblk_p + nblk_s
    n = nblk * tb
    assert n == logits_p.shape[1] + logits_s.shape[1]
    return pl.pallas_call(
        functools.partial(_route_body, nblk_p, nblk),
        grid=(nblk,),
        in_specs=[pl.BlockSpec((N_EXPERTS, tb), lambda i: (0, jnp.minimum(i, nblk_p - 1))),
                  pl.BlockSpec((N_EXPERTS, tb), lambda i: (0, jnp.maximum(i - nblk_p, 0)))],
        out_specs=[pl.BlockSpec((tb, LANES), lambda i: (i, 0)),
                   pl.BlockSpec((8, n), lambda i: (0, 0)),
                   pl.BlockSpec((N_EXPERTS, LANES), lambda i: (0, 0))],
        out_shape=[jax.ShapeDtypeStruct((n, LANES), F32),
                   jax.ShapeDtypeStruct((8, n), jnp.int32),
                   jax.ShapeDtypeStruct((N_EXPERTS, LANES), jnp.int32)],
        scratch_shapes=[pltpu.VMEM((nblk, 8, tb), F32), pltpu.VMEM((nblk, 8, tb), F32),
                        pltpu.VMEM((N_EXPERTS, 1), F32)],
        compiler_params=_params(("arbitrary",)),
        name="route",
    )(logits_p, logits_s)


def _sc_mesh():
    return plsc.VectorSubcoreMesh(core_axis_name="c", subcore_axis_name="s")


def _sc_worker():
    return lax.axis_index("s") * SC_CORES + lax.axis_index("c")


def _sc_dispatch(x_p, x_s, dest_t, n_slots):
    chunk = SC_ROWS
    n_p, n_s = x_p.shape[0], x_s.shape[0]
    per_w = n_p // (SC_WORKERS * chunk)
    ns_chunks = n_s // chunk
    assert per_w * SC_WORKERS * chunk == n_p and per_w % 2 == 0
    assert ns_chunks * chunk == n_s and ns_chunks <= SC_WORKERS
    d3 = dest_t.reshape(dest_t.shape[0], (n_p + n_s) // chunk, chunk)
    width, dtype = x_p.shape[1], x_p.dtype

    @functools.partial(
        pl.kernel, mesh=_sc_mesh(),
        out_type=jax.ShapeDtypeStruct((n_slots, width), dtype),
        scratch_types=[pltpu.VMEM((TOP_K, per_w, chunk), jnp.int32),
                       pltpu.VMEM((TOP_K, 1, chunk), jnp.int32),
                       pltpu.VMEM((2, chunk, width), dtype),
                       pltpu.SemaphoreType.DMA, pltpu.SemaphoreType.DMA],
        compiler_params=pltpu.CompilerParams(use_tc_tiling_on_sc=True),
        name="dispatch")
    def run(xp_hbm, xs_hbm, d_hbm, out_hbm, ip_v, is_v, rows_v, rsem, wsem):
        wid = _sc_worker()
        pltpu.sync_copy(d_hbm.at[pl.ds(0, TOP_K), pl.ds(wid * per_w, per_w)], ip_v)

        def read(j, slot):
            return pltpu.make_async_copy(xp_hbm.at[pl.ds((wid * per_w + j) * chunk, chunk)], rows_v.at[slot], rsem)

        def scatter(idx_v, j, slot):
            copies = [pltpu.async_copy(rows_v.at[slot], out_hbm.at[idx_v.at[kx, j]], wsem) for kx in range(TOP_K)]
            for cp in copies:
                cp.wait()

        read(0, 0).start()

        def body(h, carry):
            j = 2 * h
            read(j, 0).wait()
            read(j + 1, 1).start()
            scatter(ip_v, j, 0)
            read(j + 1, 1).wait()

            @pl.when(j + 2 < per_w)
            def _():
                read(j + 2, 0).start()

            scatter(ip_v, j + 1, 1)
            return carry

        lax.fori_loop(0, per_w // 2, body, 0)

        @pl.when(wid < ns_chunks)
        def _():
            pltpu.sync_copy(d_hbm.at[pl.ds(0, TOP_K), pl.ds(n_p // chunk + wid, 1)], is_v)
            pltpu.sync_copy(xs_hbm.at[pl.ds(wid * chunk, chunk)], rows_v.at[0])
            scatter(is_v, 0, 0)

    return run(x_p, x_s, d3)


def _sc_collect(y_sorted, dest_t, n_p, n_s):
    chunk = SC_ROWS
    per_choice = SC_WORKERS // TOP_K
    per_w = n_p // (per_choice * chunk)
    assert per_w * per_choice * chunk == n_p and per_w % 2 == 0
    assert n_s == per_choice * chunk
    d3 = dest_t.reshape(dest_t.shape[0], (n_p + n_s) // chunk, chunk)
    width, dtype = y_sorted.shape[1], y_sorted.dtype

    @functools.partial(
        pl.kernel, mesh=_sc_mesh(),
        out_type=[jax.ShapeDtypeStruct((TOP_K * n_p, width), dtype), jax.ShapeDtypeStruct((TOP_K * n_s, width), dtype)],
        scratch_types=[pltpu.VMEM((per_w, chunk), jnp.int32),
                       pltpu.VMEM((1, chunk), jnp.int32),
                       pltpu.VMEM((2, chunk, width), dtype),
                       pltpu.SemaphoreType.DMA, pltpu.SemaphoreType.DMA],
        compiler_params=pltpu.CompilerParams(use_tc_tiling_on_sc=True),
        name="collect")
    def run(y_hbm, d_hbm, op_hbm, os_hbm, ip_v, is_v, rows_v, gsem, wsem):
        wid = _sc_worker()
        choice = wid // per_choice
        part = wid % per_choice
        pltpu.sync_copy(d_hbm.at[choice, pl.ds(part * per_w, per_w)], ip_v)
        pltpu.sync_copy(d_hbm.at[choice, pl.ds(n_p // chunk + part, 1)], is_v)

        def gather(idx_v, j, slot):
            return pltpu.make_async_copy(y_hbm.at[idx_v.at[j]], rows_v.at[slot], gsem)

        def write(j, slot):
            return pltpu.make_async_copy(rows_v.at[slot], op_hbm.at[pl.ds((wid * per_w + j) * chunk, chunk)], wsem)

        gather(ip_v, 0, 0).start()

        def body(h, carry):
            j = 2 * h
            gather(ip_v, j, 0).wait()

            @pl.when(h > 0)
            def _():
                write(j - 1, 1).wait()

            gather(ip_v, j + 1, 1).start()
            write(j, 0).start()
            gather(ip_v, j + 1, 1).wait()
            write(j, 0).wait()

            @pl.when(j + 2 < per_w)
            def _():
                gather(ip_v, j + 2, 0).start()

            write(j + 1, 1).start()
            return carry

        lax.fori_loop(0, per_w // 2, body, 0)
        write(per_w - 1, 1).wait()

        gather(is_v, 0, 0).start()
        gather(is_v, 0, 0).wait()
        pltpu.sync_copy(rows_v.at[0], os_hbm.at[pl.ds(wid * chunk, chunk)])

    return run(y_sorted, d3)


def _expert_body(blk0_ref, nblk_ref, cnt_ref, wup_ref, wdn_ref, bup_ref, bdn_ref, x_hbm, y_hbm,
                 wup_s, wdn_s, xbuf, obuf, in_sem, out_sem):
    e = pl.program_id(0)
    nb = nblk_ref[e]
    blk0 = blk0_ref[e]
    cnt = cnt_ref[e]
    tm = EXPERT_ROWS
    pair = 2 * LANES

    def x_copy(i, slot):
        rows = pl.ds(pl.multiple_of((blk0 + i) * tm, tm), tm)
        return pltpu.make_async_copy(x_hbm.at[rows], xbuf.at[slot], in_sem.at[slot])

    def y_copy(i, slot):
        rows = pl.ds(pl.multiple_of((blk0 + i) * tm, tm), tm)
        return pltpu.make_async_copy(obuf.at[slot], y_hbm.at[rows], out_sem.at[slot])

    @pl.when(nb > 0)
    def _():
        x_copy(0, 0).start(priority=1)
        r = lax.broadcasted_iota(jnp.int32, (pair, pair), 0)
        c = lax.broadcasted_iota(jnp.int32, (pair, pair), 1)
        perm = (r == jnp.where(c < LANES, 2 * c, 2 * (c - LANES) + 1)).astype(BF16)
        for g in range(2 * D_MODEL // pair):
            cols = slice(g * pair, (g + 1) * pair)
            wup_s[g] = jnp.dot(wup_ref[0, :, cols].astype(BF16), perm, preferred_element_type=F32).astype(BF16)
        for g in range(D_MODEL // pair):
            wdn_s[g] = wdn_ref[0, :, g * pair:(g + 1) * pair].astype(BF16)

        def block(i, carry):
            slot = i % 2
            x_copy(i, slot).wait()

            @pl.when(i + 1 < nb)
            def _():
                x_copy(i + 1, 1 - slot).start(priority=1)

            @pl.when(i >= 2)
            def _():
                y_copy(i - 2, slot).wait()

            row = lax.broadcasted_iota(jnp.int32, (tm, 1), 0)
            x = jnp.where(row < cnt - i * tm, _unpack_bf16_pair(xbuf[slot]), 0.0).astype(BF16)
            acts = []
            for g in range(2 * D_MODEL // pair):
                cols = slice(g * pair, (g + 1) * pair)
                h = jnp.dot(x, wup_s[g], preferred_element_type=F32) + bup_ref[0, :, cols]
                glu = jnp.minimum(h[:, :LANES], SWIGLU_LIMIT)
                lin = jnp.clip(h[:, LANES:], -SWIGLU_LIMIT, SWIGLU_LIMIT)
                acts.append((glu * jax.nn.sigmoid(SWIGLU_ALPHA * glu) * (lin + 1.0)).astype(BF16))
            act = jnp.concatenate(acts, axis=1)
            half_groups = D_MODEL // pair // 2
            for g in range(half_groups):
                ys = []
                for gg in (g, g + half_groups):
                    cols = slice(gg * pair, (gg + 1) * pair)
                    ys.append(jnp.dot(act, wdn_s[gg], preferred_element_type=F32) + bdn_ref[0, :, cols])
                obuf[slot, :, g * pair:(g + 1) * pair] = _pack_bf16_pair(ys[0], ys[1])
            y_copy(i, slot).start(priority=1)
            return carry

        lax.fori_loop(0, nb, block, 0)

        @pl.when(nb >= 2)
        def _():
            y_copy(nb - 2, nb % 2).wait()

        y_copy(nb - 1, (nb - 1) % 2).wait()


def _experts(blk0, nblk, cnt, x_sorted, w_up, w_down, b_up_grouped, b_down):
    tm = EXPERT_ROWS
    per_expert = lambda a: pl.BlockSpec((1,) + a.shape[1:], lambda e, b0, nb, ct: (e, 0, 0))
    grid_spec = pltpu.PrefetchScalarGridSpec(
        num_scalar_prefetch=3,
        grid=(N_EXPERTS,),
        in_specs=[per_expert(w_up), per_expert(w_down), per_expert(b_up_grouped), per_expert(b_down),
                  pl.BlockSpec(memory_space=pl.ANY)],
        out_specs=pl.BlockSpec(memory_space=pl.ANY),
        scratch_shapes=[pltpu.VMEM((2 * D_MODEL // (2 * LANES), D_MODEL, 2 * LANES), BF16),
                        pltpu.VMEM((D_MODEL // (2 * LANES), D_MODEL, 2 * LANES), BF16),
                        pltpu.VMEM((2, tm, x_sorted.shape[1]), x_sorted.dtype),
                        pltpu.VMEM((2, tm, D_MODEL // 2), jnp.uint32),
                        pltpu.SemaphoreType.DMA((2,)), pltpu.SemaphoreType.DMA((2,))],
    )
    return pl.pallas_call(
        _expert_body,
        grid_spec=grid_spec,
        out_shape=jax.ShapeDtypeStruct((x_sorted.shape[0], D_MODEL // 2), jnp.uint32),
        compiler_params=_params(("arbitrary",)),
        name="experts",
    )(blk0, nblk, cnt, w_up, w_down, b_up_grouped, b_down, x_sorted)


def _combine_body(gate_ref, xmid_ref, gfin_ref, y0_ref, y1_ref, y2_ref, y3_ref, out_ref):
    gate = gate_ref[...]
    moe = _unpack_bf16_pair(y0_ref[...]) * gate[:, 0:1]
    for kx, y_ref in enumerate((y1_ref, y2_ref, y3_ref), start=1):
        moe = moe + _unpack_bf16_pair(y_ref[...]) * gate[:, kx:kx + 1]
    out_ref[...] = _rms(xmid_ref[...] + moe, gfin_ref[...])


def _combine(gates, first_token, xmid, gfin, y_rows):
    n = xmid.shape[0]
    tt = COMBINE_ROWS
    nblk = n // tt
    blk0 = first_token // tt
    assert blk0 * tt == first_token
    choice = lambda kx: pl.BlockSpec((tt, y_rows.shape[1]), lambda i: (i + kx * nblk, 0))
    return pl.pallas_call(
        _combine_body,
        grid=(nblk,),
        in_specs=[pl.BlockSpec((tt, LANES), lambda i: (i + blk0, 0)),
                  pl.BlockSpec((tt, D_MODEL), lambda i: (i, 0)),
                  pl.BlockSpec((1, D_MODEL), lambda i: (0, 0))] + [choice(kx) for kx in range(TOP_K)],
        out_specs=pl.BlockSpec((tt, D_MODEL), lambda i: (i, 0)),
        out_shape=jax.ShapeDtypeStruct((n, D_MODEL), F32),
        compiler_params=_params(("arbitrary",)),
        name="combine",
    )(gates, xmid, gfin, y_rows, y_rows, y_rows, y_rows)


def kernel(x_prompt, x_sample, cache_k_win, cache_v_win, norm_attn_g, w_in, ln_v_g, ln_v_b, w_spatial, b_spatial,
           attn_sinks, w_out, norm_ffn_g, w_router, b_router, w_up, b_up, w_down, b_down, norm_final_g):
    bp, tp, _ = x_prompt.shape
    bs, ts, _ = x_sample.shape
    w_buf = cache_k_win.shape[2]
    assert bp == 1 and tp % MIX_ROWS == 0 and w_buf == WINDOW and (bs * ts) % PROJ_ROWS == 0 and 8 % ts == 0
    n_p, n_s = bp * tp, bs * ts
    row2 = lambda a: a.reshape(1, -1)

    w_in_bf = w_in[0].astype(BF16)
    w_out_bf = w_out[0].astype(BF16)
    tril = jnp.tril(jnp.ones((CHUNK, CHUNK), dtype=bool))
    wsp = jnp.where(tril[None], w_spatial[0], 0.0)
    wsp_bf = wsp.astype(BF16)
    bsp = jnp.broadcast_to(b_spatial[0][:, :, None], (GMLP_GROUPS, CHUNK, LANES))
    b_up_grouped = b_up[0].reshape(N_EXPERTS, -1, LANES, 2).transpose(0, 1, 3, 2).reshape(N_EXPERTS, 1, -1)
    bd = b_down[0][:, None, :]
    g1, g2, gfin = row2(norm_attn_g[0]), row2(norm_ffn_g[0]), row2(norm_final_g)
    lng, lnb = row2(ln_v_g[0]), row2(ln_v_b[0])
    wr_hi = w_router[0].astype(BF16)
    wr = jnp.concatenate([wr_hi, (w_router[0] - wr_hi.astype(F32)).astype(BF16)], axis=1)
    br = row2(b_router[0])
    sinks = attn_sinks[0]

    xp = x_prompt.reshape(n_p, D_MODEL)
    cs_p = _rotary_inputs(jnp.arange(tp, dtype=jnp.int32))
    q_p, k_p, v_p, a_p, vn_p, sgb_p = _proj(xp, g1, w_in_bf, cs_p, lng, lnb)
    xmid_p, xn2_p, lg_p = _mix(sinks, q_p, k_p, v_p, a_p, vn_p, sgb_p, xp, wsp_bf, bsp, w_out_bf, g2, wr, br)

    xs = x_sample.reshape(n_s, D_MODEL)
    pos_s = PAST_LEN + jnp.arange(ts, dtype=jnp.int32)
    cs_s = _rotary_inputs(jnp.tile(pos_s, bs))
    q_s, k_s, v_s, a_s, vn_s, sgb_s = _proj(xs, g1, w_in_bf, cs_s, lng, lnb)
    n_keys = w_buf + ts
    key_pad = jnp.zeros((bs, (-n_keys) % 8, KV_WIDTH), F32)
    with_new = lambda cache, new: jnp.concatenate(
        [cache[0].reshape(bs, w_buf, KV_WIDTH), new.reshape(bs, ts, KV_WIDTH), key_pad], axis=1)
    k_all = with_new(cache_k_win, k_s)
    v_all = with_new(cache_v_win, v_s)
    sink_col = jnp.tile(sinks, ts).reshape(ts * N_HEADS, 1)
    o_s = _sample_attn(sink_col, q_s.reshape(bs, ts * N_HEADS, HEAD_DIM), k_all, v_all).reshape(n_s, Q_WIDTH)
    t_idx = jnp.arange(ts)
    coef = jnp.stack([jnp.where((t_idx >= d)[None, :], wsp[:, t_idx, jnp.maximum(t_idx - d, 0)], 0.0)
                      for d in range(ts)])
    coef = jnp.repeat(coef.transpose(0, 2, 1), GMLP_WIDTH // GMLP_GROUPS, axis=2)
    coef = jnp.tile(coef, (1, 8 // ts, 1))
    bias = jnp.tile(jnp.repeat(b_spatial[0][:, :ts].T, GMLP_WIDTH // GMLP_GROUPS, axis=1), (8 // ts, 1))
    xmid_s, xn2_s, lg_s = _mix_sample(a_s, vn_s, sgb_s, o_s, xs, coef, bias, w_out_bf, g2, wr, br)

    n_tok = n_p + n_s
    gate_w, dest_t, meta = _route(lg_p, lg_s)
    tm = EXPERT_ROWS
    n_blocks = (n_tok * TOP_K) // tm + N_EXPERTS

    x_sorted = _sc_dispatch(xn2_p, xn2_s, dest_t, n_blocks * tm)
    y_sorted = _experts(meta[:, 0], meta[:, 1], meta[:, 2], x_sorted, w_up[0], w_down[0], b_up_grouped, bd)
    yrows_p, yrows_s = _sc_collect(y_sorted, dest_t, n_p, n_s)
    y_p = _combine(gate_w, 0, xmid_p, gfin, yrows_p)
    y_s = _combine(gate_w, n_p, xmid_s, gfin, yrows_s)

    k4 = lambda t: t.reshape(1, bp, -1, N_KV_HEADS, HEAD_DIM)
    return (y_p.reshape(bp, tp, D_MODEL),
            y_s.reshape(bs, ts, D_MODEL),
            k4(k_p[n_p - WINDOW:]),
            k4(v_p[n_p - WINDOW:]),
            vn_p[n_p - CHUNK:].reshape(1, bp, CHUNK, GMLP_WIDTH),
            k_all[:, ts:n_keys].reshape(1, bs, w_buf, N_KV_HEADS, HEAD_DIM),
            v_all[:, ts:n_keys].reshape(1, bs, w_buf, N_KV_HEADS, HEAD_DIM),
            vn_s.reshape(1, bs, ts, GMLP_WIDTH))
```

```python
import functools

import numpy as np
import jax
import jax.numpy as jnp
from jax import lax
from jax.experimental import pallas as pl
from jax.experimental.pallas import tpu as pltpu
from jax.experimental.pallas import tpu_sc as plsc

F32 = jnp.float32
BF16 = jnp.bfloat16

D_MODEL = 1024
HEAD_DIM = 64
N_HEADS = 16
GQA_GROUP = 8
N_KV_HEADS = 2
Q_WIDTH = 1024
KV_WIDTH = 128
WINDOW = 128
ROT_DIM = 16
ROPE_THETA = 500000.0
CHUNK = 128
GMLP_WIDTH = 1024
GMLP_GROUPS = 8
N_EXPERTS = 32
TOP_K = 4
SWIGLU_LIMIT = 7.0
SWIGLU_ALPHA = 1.702
RMS_EPS = 1e-5
LN_EPS = 1e-5
NEG_INF = -1e30
PAST_LEN = 16384

LANES = 128
VMEM_LIMIT = 56 * 1024 * 1024

PROJ_ROWS = 256
MIX_ROWS = 512
ROUTE_ROWS = 512
EXPERT_ROWS = 256
W_PIECES = 8
COMBINE_ROWS = 256

SC_CORES = 2
SC_WORKERS = 32
SC_ROWS = 64

_C_Q, _C_KV, _C_U, _C_VG, _C_GA, _C_GB, _C_END = 0, 1024, 1280, 2304, 3328, 4352, 5376


def _params(sem):
    return pltpu.CompilerParams(dimension_semantics=sem, vmem_limit_bytes=VMEM_LIMIT)


def _rms(x, g):
    return x * lax.rsqrt(jnp.mean(x * x, axis=-1, keepdims=True) + RMS_EPS) * g


def _pack_bf16_pair(lo, hi):
    lo_bits = lax.bitcast_convert_type(lo.astype(BF16).astype(F32), jnp.uint32)
    hi_bits = lax.bitcast_convert_type(hi.astype(BF16).astype(F32), jnp.uint32)
    return (lo_bits >> 16) | hi_bits


def _unpack_bf16_pair(words):
    lo = lax.bitcast_convert_type(words << 16, F32)
    hi = lax.bitcast_convert_type(words & jnp.uint32(0xFFFF0000), F32)
    return jnp.concatenate([lo, hi], axis=1)


def _proj_body(x_ref, g_ref, w_ref, cs_ref, rot_ref, lng_ref, lnb_ref,
               q_ref, k_ref, v_ref, a_ref, vn_ref, sgb_ref):
    h = _rms(x_ref[...], g_ref[...]).astype(BF16)
    tabs = lax.dot_general(cs_ref[...], rot_ref[...], (((0,), (0,)), ((), ())),
                           preferred_element_type=F32, precision=lax.Precision.HIGHEST)
    rc, rs1, rs2 = tabs[:, :LANES], tabs[:, LANES:2 * LANES], tabs[:, 2 * LANES:]

    def rot(z):
        return z * rc + pltpu.roll(z, LANES - ROT_DIM // 2, 1) * rs1 + pltpu.roll(z, ROT_DIM // 2, 1) * rs2

    def mm(lo, hi):
        return jnp.dot(h, w_ref[:, lo:hi], preferred_element_type=F32)

    zq = mm(_C_Q, _C_KV)
    for c in range(Q_WIDTH // LANES):
        sl = slice(c * LANES, (c + 1) * LANES)
        q_ref[:, sl] = (rot(zq[:, sl]) * (HEAD_DIM ** -0.5)).astype(BF16)
    zkv = mm(_C_KV, _C_U)
    k_ref[...] = rot(zkv[:, :KV_WIDTH])
    v_ref[...] = zkv[:, KV_WIDTH:]
    a_ref[...] = jax.nn.sigmoid(mm(_C_GA, _C_GB)) * jax.nn.gelu(mm(_C_U, _C_VG))
    zv = jax.nn.gelu(mm(_C_VG, _C_GA))
    zc = zv - jnp.mean(zv, axis=-1, keepdims=True)
    var = jnp.mean(zc * zc, axis=-1, keepdims=True)
    vn_ref[...] = zc * lax.rsqrt(var + LN_EPS) * lng_ref[...] + lnb_ref[...]
    sgb_ref[...] = jax.nn.sigmoid(mm(_C_GB, _C_END))


def _proj(x, norm_g, w_in_bf, cs, ln_g, ln_b):
    n = x.shape[0]
    tm = PROJ_ROWS
    row = lambda w: pl.BlockSpec((tm, w), lambda i: (i, 0))
    full = lambda a: pl.BlockSpec(a.shape, lambda i: (0,) * a.ndim)
    rot = jnp.asarray(_ROT_EXPAND)
    return pl.pallas_call(
        _proj_body,
        grid=(n // tm,),
        in_specs=[row(D_MODEL), full(norm_g), full(w_in_bf), pl.BlockSpec((cs.shape[0], tm), lambda i: (0, i)),
                  full(rot),
                  full(ln_g), full(ln_b)],
        out_specs=[row(Q_WIDTH), row(KV_WIDTH), row(KV_WIDTH), row(GMLP_WIDTH), row(GMLP_WIDTH), row(D_MODEL)],
        out_shape=[jax.ShapeDtypeStruct((n, Q_WIDTH), BF16),
                   jax.ShapeDtypeStruct((n, KV_WIDTH), F32),
                   jax.ShapeDtypeStruct((n, KV_WIDTH), F32),
                   jax.ShapeDtypeStruct((n, GMLP_WIDTH), F32),
                   jax.ShapeDtypeStruct((n, GMLP_WIDTH), F32),
                   jax.ShapeDtypeStruct((n, D_MODEL), F32)],
        compiler_params=_params(("arbitrary",)),
        name="proj",
    )(x, norm_g, w_in_bf, cs, rot, ln_g, ln_b)


_ROT_COLS = 32


def _rot_expand():
    half = ROT_DIM // 2
    m = np.zeros((_ROT_COLS, 3 * LANES), np.float32)
    for lane in range(LANES):
        d = lane % HEAD_DIM
        if d < ROT_DIM:
            m[d % half, lane] = 1.0
        else:
            m[2 * half, lane] = 1.0
        if d < half:
            m[half + d, LANES + lane] = -1.0
        elif d < ROT_DIM:
            m[half + d - half, 2 * LANES + lane] = 1.0
    return m


_ROT_EXPAND = _rot_expand()


def _rotary_inputs(pos):
    half = ROT_DIM // 2
    inv_freq = ROPE_THETA ** (-jnp.arange(half, dtype=F32) / half)
    ang = inv_freq[:, None] * pos.astype(F32)[None, :]
    n = pos.shape[0]
    return jnp.concatenate([jnp.cos(ang), jnp.sin(ang), jnp.ones((1, n), F32),
                            jnp.zeros((_ROT_COLS - 2 * half - 1, n), F32)], axis=0)


def _finish_rows(merged_bf, x, wout_ref, g2_ref, wr_ref, br_ref, xmid_ref, xn2_ref, lg_ref):
    xm = x + jnp.dot(merged_bf, wout_ref[...], preferred_element_type=F32)
    xmid_ref[...] = xm
    xn = _rms(xm, g2_ref[...])
    x_hi = xn.astype(BF16)
    x_lo = (xn - x_hi.astype(F32)).astype(BF16)
    w_hl = wr_ref[...]
    p_hi = jnp.dot(x_hi, w_hl, preferred_element_type=F32)
    p_lo = jnp.dot(x_lo, w_hl[:, :N_EXPERTS], preferred_element_type=F32)
    lg = p_hi[:, :N_EXPERTS] + (p_hi[:, N_EXPERTS:] + p_lo) + br_ref[...]
    wide = jnp.concatenate([lg, jnp.zeros((lg.shape[0], LANES - N_EXPERTS), F32)], axis=1)
    lg_ref[...] = wide.T[:N_EXPERTS]
    xn2_ref[...] = _pack_bf16_pair(xn[:, :D_MODEL // 2], xn[:, D_MODEL // 2:])


def _mix_body(sinks_ref, q_ref, k_ref, kp_ref, v_ref, vp_ref, a_ref, vn_ref, sgb_ref, x_ref,
              wsp_ref, bsp_ref, wout_ref, g2_ref, wr_ref, br_ref,
              xmid_ref, xn2_ref, lg_ref, kcat, vcat, mrg):
    i = pl.program_id(0)
    nsub = MIX_ROWS // WINDOW
    kcat[0:WINDOW] = kp_ref[...]
    kcat[WINDOW:] = k_ref[...]
    vcat[0:WINDOW] = vp_ref[...]
    vcat[WINDOW:] = v_ref[...]

    pair_rows = (GQA_GROUP // 2) * WINDOW
    rq = lax.broadcasted_iota(jnp.int32, (pair_rows, 4 * WINDOW), 0) & (WINDOW - 1)
    ck = lax.broadcasted_iota(jnp.int32, (pair_rows, 4 * WINDOW), 1) & (2 * WINDOW - 1)
    band = (ck > rq) & (ck <= rq + WINDOW)
    lane_kv = lax.broadcasted_iota(jnp.int32, (2 * WINDOW, LANES), 1)
    lane_o = lax.broadcasted_iota(jnp.int32, (pair_rows, LANES), 1)
    row_p = lax.broadcasted_iota(jnp.int32, (pair_rows, 1), 0) >> 7

    def sub(j, carry):
        off = pl.multiple_of(j * WINDOW, WINDOW)
        rows = pl.ds(off, WINDOW)
        for g in range(GMLP_GROUPS):
            cols = slice(g * LANES, (g + 1) * LANES)
            s = jnp.dot(wsp_ref[g], vn_ref[rows, cols].astype(BF16), preferred_element_type=F32) + bsp_ref[g]
            mrg[rows, cols] = a_ref[rows, cols] * s
        kblk = kcat[pl.ds(off, 2 * WINDOW), :]
        vblk = vcat[pl.ds(off, 2 * WINDOW), :]
        kswp = pltpu.roll(kblk, HEAD_DIM, 1)
        vswp = pltpu.roll(vblk, HEAD_DIM, 1)
        kmin = jnp.where(jnp.logical_and(i == 0, j == 0), WINDOW, 0)
        allowed = band & (ck >= kmin)
        for kk in range(N_KV_HEADS):
            lo_src, hi_src = (kblk, kswp) if kk == 0 else (kswp, kblk)
            kbd = jnp.concatenate([jnp.where(lane_kv < HEAD_DIM, lo_src, 0.0),
                                   jnp.where(lane_kv >= HEAD_DIM, hi_src, 0.0)], axis=0).astype(BF16)
            lo_src, hi_src = (vblk, vswp) if kk == 0 else (vswp, vblk)
            vbd = jnp.concatenate([jnp.where(lane_kv < HEAD_DIM, lo_src, 0.0),
                                   jnp.where(lane_kv >= HEAD_DIM, hi_src, 0.0)], axis=0).astype(BF16)
            pair0 = kk * (GQA_GROUP // 2)
            qs = jnp.concatenate([q_ref[rows, (pair0 + p) * LANES:(pair0 + p + 1) * LANES]
                                  for p in range(GQA_GROUP // 2)], axis=0)
            lg = lax.dot_general(qs, kbd, (((1,), (1,)), ((), ())), preferred_element_type=F32)
            lg = jnp.where(allowed, lg, NEG_INF)
            h0 = kk * GQA_GROUP
            se = jnp.full((pair_rows, 1), sinks_ref[h0], F32)
            so = jnp.full((pair_rows, 1), sinks_ref[h0 + 1], F32)
            for p in range(1, GQA_GROUP // 2):
                se = jnp.where(row_p == p, sinks_ref[h0 + 2 * p], se)
                so = jnp.where(row_p == p, sinks_ref[h0 + 2 * p + 1], so)
            le, lo = lg[:, :2 * WINDOW], lg[:, 2 * WINDOW:]
            me = jnp.maximum(jnp.max(le, axis=1, keepdims=True), se)
            mo = jnp.maximum(jnp.max(lo, axis=1, keepdims=True), so)
            pe = jnp.exp(le - me)
            po = jnp.exp(lo - mo)
            de = jnp.sum(pe, axis=1, keepdims=True) + jnp.exp(se - me)
            do = jnp.sum(po, axis=1, keepdims=True) + jnp.exp(so - mo)
            pr = jnp.concatenate([pe, po], axis=1).astype(BF16)
            o = jnp.dot(pr, vbd, preferred_element_type=F32)
            o = o / jnp.where(lane_o < HEAD_DIM, de, do)
            for p in range(GQA_GROUP // 2):
                cols = slice((pair0 + p) * LANES, (pair0 + p + 1) * LANES)
                mrg[rows, cols] += sgb_ref[rows, cols] * o[p * WINDOW:(p + 1) * WINDOW]
        return carry

    lax.fori_loop(0, nsub, sub, 0)
    _finish_rows(mrg[...].astype(BF16), x_ref[...], wout_ref, g2_ref, wr_ref, br_ref, xmid_ref, xn2_ref, lg_ref)


def _mix(sinks, q, k, v, a, vn, sgb, x, wsp, bsp, wout, g2, wr, br):
    n = x.shape[0]
    tm = MIX_ROWS
    nsub = tm // WINDOW
    row = lambda w: pl.BlockSpec((tm, w), lambda i: (i, 0))
    prev = pl.BlockSpec((WINDOW, KV_WIDTH), lambda i: (jnp.maximum(i * nsub - 1, 0), 0))
    full = lambda arr: pl.BlockSpec(arr.shape, lambda i: (0,) * arr.ndim)
    smem = pl.BlockSpec(memory_space=pltpu.SMEM)
    return pl.pallas_call(
        _mix_body,
        grid=(n // tm,),
        in_specs=[smem, row(Q_WIDTH), row(KV_WIDTH), prev, row(KV_WIDTH), prev,
                  row(GMLP_WIDTH), row(GMLP_WIDTH), row(D_MODEL), row(D_MODEL),
                  full(wsp), full(bsp), full(wout), full(g2), full(wr), full(br)],
        out_specs=[row(D_MODEL), row(D_MODEL // 2), pl.BlockSpec((N_EXPERTS, tm), lambda i: (0, i))],
        out_shape=[jax.ShapeDtypeStruct((n, D_MODEL), F32),
                   jax.ShapeDtypeStruct((n, D_MODEL // 2), jnp.uint32),
                   jax.ShapeDtypeStruct((N_EXPERTS, n), F32)],
        scratch_shapes=[pltpu.VMEM((tm + WINDOW, KV_WIDTH), F32),
                        pltpu.VMEM((tm + WINDOW, KV_WIDTH), F32),
                        pltpu.VMEM((tm, D_MODEL), F32)],
        compiler_params=_params(("arbitrary",)),
        name="mix_prompt",
    )(sinks, q, k, k, v, v, a, vn, sgb, x, wsp, bsp, wout, g2, wr, br)


def _sample_attn_body(sink_ref, q_ref, k_ref, v_ref, o_ref):
    q = q_ref[...]
    k = k_ref[...]
    v = v_ref[...]
    nq, nk = q.shape[1], k.shape[1]
    heads = lambda t, kk: t[:, :, kk * HEAD_DIM:(kk + 1) * HEAD_DIM].astype(BF16)
    row = lax.broadcasted_iota(jnp.int32, (1, nq, nk), 1)
    first_kv = (row % N_HEADS) < GQA_GROUP
    lg = jnp.where(first_kv,
                   jnp.einsum("bqd,bkd->bqk", q, heads(k, 0), preferred_element_type=F32),
                   jnp.einsum("bqd,bkd->bqk", q, heads(k, 1), preferred_element_type=F32))
    t = row // N_HEADS
    j = lax.broadcasted_iota(jnp.int32, (1, nq, nk), 2)
    lg = jnp.where((j > t) & (j <= t + WINDOW), lg, NEG_INF)
    sink = sink_ref[...][None]
    m = jnp.maximum(jnp.max(lg, axis=2, keepdims=True), sink)
    p = jnp.exp(lg - m)
    den = jnp.sum(p, axis=2, keepdims=True) + jnp.exp(sink - m)
    pb = p.astype(BF16)
    row_o = lax.broadcasted_iota(jnp.int32, (1, nq, HEAD_DIM), 1)
    o = jnp.where((row_o % N_HEADS) < GQA_GROUP,
                  jnp.einsum("bqk,bkd->bqd", pb, heads(v, 0), preferred_element_type=F32),
                  jnp.einsum("bqk,bkd->bqd", pb, heads(v, 1), preferred_element_type=F32))
    o_ref[...] = o / den


def _sample_attn(sink_col, q3, k_all, v_all):
    nb = q3.shape[0]
    bb = 32
    blk = lambda a: pl.BlockSpec((bb,) + a.shape[1:], lambda b: (b, 0, 0))
    return pl.pallas_call(
        _sample_attn_body,
        grid=(nb // bb,),
        in_specs=[pl.BlockSpec(sink_col.shape, lambda b: (0, 0)), blk(q3), blk(k_all), blk(v_all)],
        out_specs=blk(q3),
        out_shape=jax.ShapeDtypeStruct(q3.shape, F32),
        compiler_params=_params(("arbitrary",)),
        name="attn_sample",
    )(sink_col, q3, k_all, v_all)


def _mix_sample_body(a_ref, vn_ref, sgb_ref, o_ref, x_ref, coef_ref, bias_ref,
                     wout_ref, g2_ref, wr_ref, br_ref, xmid_ref, xn2_ref, lg_ref):
    vn = vn_ref[...]
    n, width = vn.shape
    rows8 = lambda t: t.reshape(n // 8, 8, width)
    s = bias_ref[...][None] + coef_ref[0][None] * rows8(vn)
    for d in range(1, coef_ref.shape[0]):
        s = s + coef_ref[d][None] * rows8(pltpu.roll(vn, d, 0))
    merged = a_ref[...] * s.reshape(n, width) + sgb_ref[...] * o_ref[...]
    _finish_rows(merged.astype(BF16), x_ref[...], wout_ref, g2_ref, wr_ref, br_ref, xmid_ref, xn2_ref, lg_ref)


def _mix_sample(a, vn, sgb, o, x, coef, bias, wout, g2, wr, br):
    n = x.shape[0]
    args = (a, vn, sgb, o, x, coef, bias, wout, g2, wr, br)
    full = lambda arr: pl.BlockSpec(arr.shape, lambda i: (0,) * arr.ndim)
    return pl.pallas_call(
        _mix_sample_body,
        grid=(1,),
        in_specs=[full(arr) for arr in args],
        out_specs=[pl.BlockSpec((n, D_MODEL), lambda i: (0, 0)), pl.BlockSpec((n, D_MODEL // 2), lambda i: (0, 0)),
                   pl.BlockSpec((N_EXPERTS, n), lambda i: (0, 0))],
        out_shape=[jax.ShapeDtypeStruct((n, D_MODEL), F32),
                   jax.ShapeDtypeStruct((n, D_MODEL // 2), jnp.uint32),
                   jax.ShapeDtypeStruct((N_EXPERTS, n), F32)],
        compiler_params=_params(("arbitrary",)),
        name="mix_sample",
    )(*args)


def _rows8(rows, dtype):
    n = rows[0].shape[1]
    sub = lax.broadcasted_iota(jnp.int32, (8, n), 0)
    out = jnp.zeros((8, n), dtype)
    for kx, r in enumerate(rows):
        out = jnp.where(sub == kx, r.astype(dtype), out)
    return out


def _route_body(nblk_p, nblk, lgp_ref, lgs_ref, gate_ref, dest_ref, meta_ref, idx_s, rank_s, base):
    i = pl.program_id(0)

    @pl.when(i == 0)
    def _():
        base[...] = jnp.zeros_like(base)

    l = jnp.where(jnp.full(lgp_ref.shape, i, jnp.int32) < nblk_p, lgp_ref[...], lgs_ref[...])
    tb = l.shape[1]
    sub = lax.broadcasted_iota(jnp.int32, l.shape, 0).astype(F32)
    vals, idxs, sels = [], [], []
    for _ in range(TOP_K):
        m = jnp.max(l, axis=0, keepdims=True)
        ik = jnp.min(jnp.where(l == m, sub, float(N_EXPERTS)), axis=0, keepdims=True)
        sel = sub == ik
        l = jnp.where(sel, -jnp.inf, l)
        vals.append(m)
        idxs.append(ik)
        sels.append(sel)
    es = [jnp.exp(vk - vals[0]) for vk in vals]
    den = es[0] + es[1] + es[2] + es[3]
    onehot = jnp.zeros(l.shape, F32)
    for sel in sels:
        onehot = onehot + sel.astype(F32)
    earlier = (lax.broadcasted_iota(jnp.int32, (tb, tb), 0) < lax.broadcasted_iota(jnp.int32, (tb, tb), 1))
    before = jnp.dot(onehot.astype(BF16), earlier.astype(BF16), preferred_element_type=F32) + base[...]
    ranks = [jnp.sum(jnp.where(sel, before, 0.0), axis=0, keepdims=True) for sel in sels]
    base[...] += jnp.sum(onehot, axis=1, keepdims=True)
    idx_s[i] = _rows8(idxs, F32)
    rank_s[i] = _rows8(ranks, F32)
    gates = jnp.concatenate([_rows8([e / den for e in es], F32), jnp.zeros((LANES - 8, tb), F32)], axis=0)
    gate_ref[...] = gates.T

    @pl.when(i == nblk - 1)
    def _():
        cnt = base[...]
        padded = jnp.ceil(cnt / EXPERT_ROWS) * EXPERT_ROWS
        lower = (lax.broadcasted_iota(jnp.int32, (N_EXPERTS, N_EXPERTS), 1) <
                 lax.broadcasted_iota(jnp.int32, (N_EXPERTS, N_EXPERTS), 0)).astype(F32)
        pstart = jnp.dot(lower, jnp.broadcast_to(padded, (N_EXPERTS, LANES)), preferred_element_type=F32,
                         precision=lax.Precision.HIGHEST)[:, :1]
        lane = lax.broadcasted_iota(jnp.int32, (N_EXPERTS, LANES), 1)
        meta = jnp.where(lane == 0, pstart / EXPERT_ROWS,
                         jnp.where(lane == 1, padded / EXPERT_ROWS, jnp.where(lane == 2, cnt, 0.0)))
        meta_ref[...] = meta.astype(jnp.int32)
        sub_e = lax.broadcasted_iota(jnp.int32, (N_EXPERTS, tb), 0).astype(F32)
        for b in range(nblk):
            idx, rank = idx_s[b], rank_s[b]
            rows = [jnp.sum(jnp.where(sub_e == idx[kx:kx + 1], pstart, 0.0), axis=0, keepdims=True)
                    + rank[kx:kx + 1] for kx in range(TOP_K)]
            dest_ref[:, b * tb:(b + 1) * tb] = _rows8(rows, jnp.int32)


def _route(logits_p, logits_s):
    tb = ROUTE_ROWS
    nblk_p, nblk_s = logits_p.shape[1] // tb, logits_s.shape[1] // tb
    nblk = nblk_p + nblk_s
    n = nblk * tb
    assert n == logits_p.shape[1] + logits_s.shape[1]
    return pl.pallas_call(
        functools.partial(_route_body, nblk_p, nblk),
        grid=(nblk,),
        in_specs=[pl.BlockSpec((N_EXPERTS, tb), lambda i: (0, jnp.minimum(i, nblk_p - 1))),
                  pl.BlockSpec((N_EXPERTS, tb), lambda i: (0, jnp.maximum(i - nblk_p, 0)))],
        out_specs=[pl.BlockSpec((tb, LANES), lambda i: (i, 0)),
                   pl.BlockSpec((8, n), lambda i: (0, 0)),
                   pl.BlockSpec((N_EXPERTS, LANES), lambda i: (0, 0))],
        out_shape=[jax.ShapeDtypeStruct((n, LANES), F32),
                   jax.ShapeDtypeStruct((8, n), jnp.int32),
                   jax.ShapeDtypeStruct((N_EXPERTS, LANES), jnp.int32)],
        scratch_shapes=[pltpu.VMEM((nblk, 8, tb), F32), pltpu.VMEM((nblk, 8, tb), F32),
                        pltpu.VMEM((N_EXPERTS, 1), F32)],
        compiler_params=_params(("arbitrary",)),
        name="route",
    )(logits_p, logits_s)


def _sc_mesh():
    return plsc.VectorSubcoreMesh(core_axis_name="c", subcore_axis_name="s")


def _sc_worker():
    return lax.axis_index("s") * SC_CORES + lax.axis_index("c")


def _sc_dispatch(x_p, x_s, dest_t, n_slots):
    chunk = SC_ROWS
    n_p, n_s = x_p.shape[0], x_s.shape[0]
    per_w = n_p // (SC_WORKERS * chunk)
    ns_chunks = n_s // chunk
    assert per_w * SC_WORKERS * chunk == n_p and per_w % 2 == 0
    assert ns_chunks * chunk == n_s and ns_chunks <= SC_WORKERS
    d3 = dest_t.reshape(dest_t.shape[0], (n_p + n_s) // chunk, chunk)
    width, dtype = x_p.shape[1], x_p.dtype

    @functools.partial(
        pl.kernel, mesh=_sc_mesh(),
        out_type=jax.ShapeDtypeStruct((n_slots, width), dtype),
        scratch_types=[pltpu.VMEM((TOP_K, per_w, chunk), jnp.int32),
                       pltpu.VMEM((TOP_K, 1, chunk), jnp.int32),
                       pltpu.VMEM((2, chunk, width), dtype),
                       pltpu.SemaphoreType.DMA, pltpu.SemaphoreType.DMA],
        compiler_params=pltpu.CompilerParams(use_tc_tiling_on_sc=True),
        name="dispatch")
    def run(xp_hbm, xs_hbm, d_hbm, out_hbm, ip_v, is_v, rows_v, rsem, wsem):
        wid = _sc_worker()
        pltpu.sync_copy(d_hbm.at[pl.ds(0, TOP_K), pl.ds(wid * per_w, per_w)], ip_v)

        def read(j, slot):
            return pltpu.make_async_copy(xp_hbm.at[pl.ds((wid * per_w + j) * chunk, chunk)], rows_v.at[slot], rsem)

        def scatter(idx_v, j, slot):
            copies = [pltpu.async_copy(rows_v.at[slot], out_hbm.at[idx_v.at[kx, j]], wsem) for kx in range(TOP_K)]
            for cp in copies:
                cp.wait()

        read(0, 0).start()

        def body(h, carry):
            j = 2 * h
            read(j, 0).wait()
            read(j + 1, 1).start()
            scatter(ip_v, j, 0)
            read(j + 1, 1).wait()

            @pl.when(j + 2 < per_w)
            def _():
                read(j + 2, 0).start()

            scatter(ip_v, j + 1, 1)
            return carry

        lax.fori_loop(0, per_w // 2, body, 0)

        @pl.when(wid < ns_chunks)
        def _():
            pltpu.sync_copy(d_hbm.at[pl.ds(0, TOP_K), pl.ds(n_p // chunk + wid, 1)], is_v)
            pltpu.sync_copy(xs_hbm.at[pl.ds(wid * chunk, chunk)], rows_v.at[0])
            scatter(is_v, 0, 0)

    return run(x_p, x_s, d3)


def _sc_collect(y_sorted, dest_t, n_p, n_s):
    chunk = SC_ROWS
    per_choice = SC_WORKERS // TOP_K
    per_w = n_p // (per_choice * chunk)
    assert per_w * per_choice * chunk == n_p and per_w % 2 == 0
    assert n_s == per_choice * chunk
    d3 = dest_t.reshape(dest_t.shape[0], (n_p + n_s) // chunk, chunk)
    width, dtype = y_sorted.shape[1], y_sorted.dtype

    @functools.partial(
        pl.kernel, mesh=_sc_mesh(),
        out_type=[jax.ShapeDtypeStruct((TOP_K * n_p, width), dtype), jax.ShapeDtypeStruct((TOP_K * n_s, width), dtype)],
        scratch_types=[pltpu.VMEM((per_w, chunk), jnp.int32),
                       pltpu.VMEM((1, chunk), jnp.int32),
                       pltpu.VMEM((2, chunk, width), dtype),
                       pltpu.SemaphoreType.DMA, pltpu.SemaphoreType.DMA],
        compiler_params=pltpu.CompilerParams(use_tc_tiling_on_sc=True),
        name="collect")
    def run(y_hbm, d_hbm, op_hbm, os_hbm, ip_v, is_v, rows_v, gsem, wsem):
        wid = _sc_worker()
        choice = wid // per_choice
        part = wid % per_choice
        pltpu.sync_copy(d_hbm.at[choice, pl.ds(part * per_w, per_w)], ip_v)
        pltpu.sync_copy(d_hbm.at[choice, pl.ds(n_p // chunk + part, 1)], is_v)

        def gather(idx_v, j, slot):
            return pltpu.make_async_copy(y_hbm.at[idx_v.at[j]], rows_v.at[slot], gsem)

        def write(j, slot):
            return pltpu.make_async_copy(rows_v.at[slot], op_hbm.at[pl.ds((wid * per_w + j) * chunk, chunk)], wsem)

        gather(ip_v, 0, 0).start()

        def body(h, carry):
            j = 2 * h
            gather(ip_v, j, 0).wait()

            @pl.when(h > 0)
            def _():
                write(j - 1, 1).wait()

            gather(ip_v, j + 1, 1).start()
            write(j, 0).start()
            gather(ip_v, j + 1, 1).wait()
            write(j, 0).wait()

            @pl.when(j + 2 < per_w)
            def _():
                gather(ip_v, j + 2, 0).start()

            write(j + 1, 1).start()
            return carry

        lax.fori_loop(0, per_w // 2, body, 0)
        write(per_w - 1, 1).wait()

        gather(is_v, 0, 0).start()
        gather(is_v, 0, 0).wait()
        pltpu.sync_copy(rows_v.at[0], os_hbm.at[pl.ds(wid * chunk, chunk)])

    return run(y_sorted, d3)


def _expert_body(blk0_ref, nblk_ref, cnt_ref, wup_hbm, wdn_hbm, bup_ref, bdn_ref, x_hbm, y_hbm,
                 wup_f, wdn_f, wup_s, wdn_s, xbuf, obuf, w_sem, in_sem, out_sem):
    e = pl.program_id(0)
    nb = nblk_ref[e]
    blk0 = blk0_ref[e]
    cnt = cnt_ref[e]
    tm = EXPERT_ROWS
    pair = 2 * LANES
    wslot = e % 2
    up_rows = D_MODEL // W_PIECES
    dn_rows = D_MODEL // (W_PIECES // 2)

    def w_piece(hbm, buf, ex, slot, p, rows):
        start = p * rows if isinstance(p, int) else pl.multiple_of(p * rows, rows)
        r = pl.ds(start, rows)
        return pltpu.make_async_copy(hbm.at[ex, r], buf.at[slot, r], w_sem.at[slot])

    def w_start(ex, slot, p):
        w_piece(wup_hbm, wup_f, ex, slot, p, up_rows).start()
        if isinstance(p, int):
            if p < W_PIECES // 2:
                w_piece(wdn_hbm, wdn_f, ex, slot, p, dn_rows).start()
        else:
            @pl.when(p < W_PIECES // 2)
            def _():
                w_piece(wdn_hbm, wdn_f, ex, slot, p, dn_rows).start()

    def w_wait(ex, slot):
        for p in range(W_PIECES):
            w_piece(wup_hbm, wup_f, ex, slot, p, up_rows).wait()
        for p in range(W_PIECES // 2):
            w_piece(wdn_hbm, wdn_f, ex, slot, p, dn_rows).wait()

    @pl.when(e == 0)
    def _():
        for p in range(W_PIECES):
            w_start(0, 0, p)

    w_wait(e, wslot)
    wup_ref = wup_f.at[wslot]
    wdn_ref = wdn_f.at[wslot]
    more = e + 1 < N_EXPERTS

    def x_copy(i, slot):
        rows = pl.ds(pl.multiple_of((blk0 + i) * tm, tm), tm)
        return pltpu.make_async_copy(x_hbm.at[rows], xbuf.at[slot], in_sem.at[slot])

    def y_copy(i, slot):
        rows = pl.ds(pl.multiple_of((blk0 + i) * tm, tm), tm)
        return pltpu.make_async_copy(obuf.at[slot], y_hbm.at[rows], out_sem.at[slot])

    @pl.when(nb > 0)
    def _():
        x_copy(0, 0).start(priority=1)
        r = lax.broadcasted_iota(jnp.int32, (pair, pair), 0)
        c = lax.broadcasted_iota(jnp.int32, (pair, pair), 1)
        perm = (r == jnp.where(c < LANES, 2 * c, 2 * (c - LANES) + 1)).astype(BF16)
        for g in range(2 * D_MODEL // pair):
            cols = slice(g * pair, (g + 1) * pair)
            wup_s[g] = jnp.dot(wup_ref[:, cols].astype(BF16), perm, preferred_element_type=F32).astype(BF16)
        for g in range(D_MODEL // pair):
            wdn_s[g] = wdn_ref[:, g * pair:(g + 1) * pair].astype(BF16)

        def block(i, carry):
            slot = i % 2
            x_copy(i, slot).wait()

            @pl.when(i + 1 < nb)
            def _():
                x_copy(i + 1, 1 - slot).start(priority=1)

            @pl.when(i >= 2)
            def _():
                y_copy(i - 2, slot).wait()

            @pl.when(jnp.logical_and(more, i < W_PIECES))
            def _():
                w_start(e + 1, 1 - wslot, i)

            row = lax.broadcasted_iota(jnp.int32, (tm, 1), 0)
            x = jnp.where(row < cnt - i * tm, _unpack_bf16_pair(xbuf[slot]), 0.0).astype(BF16)
            acts = []
            for g in range(2 * D_MODEL // pair):
                cols = slice(g * pair, (g + 1) * pair)
                h = jnp.dot(x, wup_s[g], preferred_element_type=F32) + bup_ref[0, :, cols]
                glu = jnp.minimum(h[:, :LANES], SWIGLU_LIMIT)
                lin = jnp.clip(h[:, LANES:], -SWIGLU_LIMIT, SWIGLU_LIMIT)
                acts.append((glu * jax.nn.sigmoid(SWIGLU_ALPHA * glu) * (lin + 1.0)).astype(BF16))
            act = jnp.concatenate(acts, axis=1)
            half_groups = D_MODEL // pair // 2
            for g in range(half_groups):
                ys = []
                for gg in (g, g + half_groups):
                    cols = slice(gg * pair, (gg + 1) * pair)
                    ys.append(jnp.dot(act, wdn_s[gg], preferred_element_type=F32) + bdn_ref[0, :, cols])
                obuf[slot, :, g * pair:(g + 1) * pair] = _pack_bf16_pair(ys[0], ys[1])
            y_copy(i, slot).start(priority=1)
            return carry

        lax.fori_loop(0, nb, block, 0)

        @pl.when(nb >= 2)
        def _():
            y_copy(nb - 2, nb % 2).wait()

        y_copy(nb - 1, (nb - 1) % 2).wait()

    for p in range(W_PIECES):
        @pl.when(jnp.logical_and(more, p >= nb))
        def _():
            w_start(e + 1, 1 - wslot, p)


def _experts(blk0, nblk, cnt, x_sorted, w_up, w_down, b_up_grouped, b_down):
    tm = EXPERT_ROWS
    per_expert = lambda a: pl.BlockSpec((1,) + a.shape[1:], lambda e, b0, nb, ct: (e, 0, 0))
    hbm = pl.BlockSpec(memory_space=pl.ANY)
    grid_spec = pltpu.PrefetchScalarGridSpec(
        num_scalar_prefetch=3,
        grid=(N_EXPERTS,),
        in_specs=[hbm, hbm, per_expert(b_up_grouped), per_expert(b_down), hbm],
        out_specs=hbm,
        scratch_shapes=[pltpu.VMEM((2,) + w_up.shape[1:], F32), pltpu.VMEM((2,) + w_down.shape[1:], F32),
                        pltpu.VMEM((2 * D_MODEL // (2 * LANES), D_MODEL, 2 * LANES), BF16),
                        pltpu.VMEM((D_MODEL // (2 * LANES), D_MODEL, 2 * LANES), BF16),
                        pltpu.VMEM((2, tm, x_sorted.shape[1]), x_sorted.dtype),
                        pltpu.VMEM((2, tm, D_MODEL // 2), jnp.uint32),
                        pltpu.SemaphoreType.DMA((2,)), pltpu.SemaphoreType.DMA((2,)), pltpu.SemaphoreType.DMA((2,))],
    )
    return pl.pallas_call(
        _expert_body,
        grid_spec=grid_spec,
        out_shape=jax.ShapeDtypeStruct((x_sorted.shape[0], D_MODEL // 2), jnp.uint32),
        compiler_params=_params(("arbitrary",)),
        name="experts",
    )(blk0, nblk, cnt, w_up, w_down, b_up_grouped, b_down, x_sorted)


def _combine_body(gate_ref, xmid_ref, gfin_ref, y0_ref, y1_ref, y2_ref, y3_ref, out_ref):
    gate = gate_ref[...]
    moe = _unpack_bf16_pair(y0_ref[...]) * gate[:, 0:1]
    for kx, y_ref in enumerate((y1_ref, y2_ref, y3_ref), start=1):
        moe = moe + _unpack_bf16_pair(y_ref[...]) * gate[:, kx:kx + 1]
    out_ref[...] = _rms(xmid_ref[...] + moe, gfin_ref[...])


def _combine(gates, first_token, xmid, gfin, y_rows):
    n = xmid.shape[0]
    tt = COMBINE_ROWS
    nblk = n // tt
    blk0 = first_token // tt
    assert blk0 * tt == first_token
    choice = lambda kx: pl.BlockSpec((tt, y_rows.shape[1]), lambda i: (i + kx * nblk, 0))
    return pl.pallas_call(
        _combine_body,
        grid=(nblk,),
        in_specs=[pl.BlockSpec((tt, LANES), lambda i: (i + blk0, 0)),
                  pl.BlockSpec((tt, D_MODEL), lambda i: (i, 0)),
                  pl.BlockSpec((1, D_MODEL), lambda i: (0, 0))] + [choice(kx) for kx in range(TOP_K)],
        out_specs=pl.BlockSpec((tt, D_MODEL), lambda i: (i, 0)),
        out_shape=jax.ShapeDtypeStruct((n, D_MODEL), F32),
        compiler_params=_params(("arbitrary",)),
        name="combine",
    )(gates, xmid, gfin, y_rows, y_rows, y_rows, y_rows)


def kernel(x_prompt, x_sample, cache_k_win, cache_v_win, norm_attn_g, w_in, ln_v_g, ln_v_b, w_spatial, b_spatial,
           attn_sinks, w_out, norm_ffn_g, w_router, b_router, w_up, b_up, w_down, b_down, norm_final_g):
    bp, tp, _ = x_prompt.shape
    bs, ts, _ = x_sample.shape
    w_buf = cache_k_win.shape[2]
    assert bp == 1 and tp % MIX_ROWS == 0 and w_buf == WINDOW and (bs * ts) % PROJ_ROWS == 0 and 8 % ts == 0
    n_p, n_s = bp * tp, bs * ts
    row2 = lambda a: a.reshape(1, -1)

    w_in_bf = w_in[0].astype(BF16)
    w_out_bf = w_out[0].astype(BF16)
    tril = jnp.tril(jnp.ones((CHUNK, CHUNK), dtype=bool))
    wsp = jnp.where(tril[None], w_spatial[0], 0.0)
    wsp_bf = wsp.astype(BF16)
    bsp = jnp.broadcast_to(b_spatial[0][:, :, None], (GMLP_GROUPS, CHUNK, LANES))
    b_up_grouped = b_up[0].reshape(N_EXPERTS, -1, LANES, 2).transpose(0, 1, 3, 2).reshape(N_EXPERTS, 1, -1)
    bd = b_down[0][:, None, :]
    g1, g2, gfin = row2(norm_attn_g[0]), row2(norm_ffn_g[0]), row2(norm_final_g)
    lng, lnb = row2(ln_v_g[0]), row2(ln_v_b[0])
    wr_hi = w_router[0].astype(BF16)
    wr = jnp.concatenate([wr_hi, (w_router[0] - wr_hi.astype(F32)).astype(BF16)], axis=1)
    br = row2(b_router[0])
    sinks = attn_sinks[0]

    xp = x_prompt.reshape(n_p, D_MODEL)
    cs_p = _rotary_inputs(jnp.arange(tp, dtype=jnp.int32))
    q_p, k_p, v_p, a_p, vn_p, sgb_p = _proj(xp, g1, w_in_bf, cs_p, lng, lnb)
    xmid_p, xn2_p, lg_p = _mix(sinks, q_p, k_p, v_p, a_p, vn_p, sgb_p, xp, wsp_bf, bsp, w_out_bf, g2, wr, br)

    xs = x_sample.reshape(n_s, D_MODEL)
    pos_s = PAST_LEN + jnp.arange(ts, dtype=jnp.int32)
    cs_s = _rotary_inputs(jnp.tile(pos_s, bs))
    q_s, k_s, v_s, a_s, vn_s, sgb_s = _proj(xs, g1, w_in_bf, cs_s, lng, lnb)
    n_keys = w_buf + ts
    key_pad = jnp.zeros((bs, (-n_keys) % 8, KV_WIDTH), F32)
    with_new = lambda cache, new: jnp.concatenate(
        [cache[0].reshape(bs, w_buf, KV_WIDTH), new.reshape(bs, ts, KV_WIDTH), key_pad], axis=1)
    k_all = with_new(cache_k_win, k_s)
    v_all = with_new(cache_v_win, v_s)
    sink_col = jnp.tile(sinks, ts).reshape(ts * N_HEADS, 1)
    o_s = _sample_attn(sink_col, q_s.reshape(bs, ts * N_HEADS, HEAD_DIM), k_all, v_all).reshape(n_s, Q_WIDTH)
    t_idx = jnp.arange(ts)
    coef = jnp.stack([jnp.where((t_idx >= d)[None, :], wsp[:, t_idx, jnp.maximum(t_idx - d, 0)], 0.0)
                      for d in range(ts)])
    coef = jnp.repeat(coef.transpose(0, 2, 1), GMLP_WIDTH // GMLP_GROUPS, axis=2)
    coef = jnp.tile(coef, (1, 8 // ts, 1))
    bias = jnp.tile(jnp.repeat(b_spatial[0][:, :ts].T, GMLP_WIDTH // GMLP_GROUPS, axis=1), (8 // ts, 1))
    xmid_s, xn2_s, lg_s = _mix_sample(a_s, vn_s, sgb_s, o_s, xs, coef, bias, w_out_bf, g2, wr, br)

    n_tok = n_p + n_s
    gate_w, dest_t, meta = _route(lg_p, lg_s)
    tm = EXPERT_ROWS
    n_blocks = (n_tok * TOP_K) // tm + N_EXPERTS

    x_sorted = _sc_dispatch(xn2_p, xn2_s, dest_t, n_blocks * tm)
    y_sorted = _experts(meta[:, 0], meta[:, 1], meta[:, 2], x_sorted, w_up[0], w_down[0], b_up_grouped, bd)
    yrows_p, yrows_s = _sc_collect(y_sorted, dest_t, n_p, n_s)
    y_p = _combine(gate_w, 0, xmid_p, gfin, yrows_p)
    y_s = _combine(gate_w, n_p, xmid_s, gfin, yrows_s)

    k4 = lambda t: t.reshape(1, bp, -1, N_KV_HEADS, HEAD_DIM)
    return (y_p.reshape(bp, tp, D_MODEL),
            y_s.reshape(bs, ts, D_MODEL),
            k4(k_p[n_p - WINDOW:]),
            k4(v_p[n_p - WINDOW:]),
            vn_p[n_p - CHUNK:].reshape(1, bp, CHUNK, GMLP_WIDTH),
            k_all[:, ts:n_keys].reshape(1, bs, w_buf, N_KV_HEADS, HEAD_DIM),
            v_all[:, ts:n_keys].reshape(1, bs, w_buf, N_KV_HEADS, HEAD_DIM),
            vn_s.reshape(1, bs, ts, GMLP_WIDTH))
```

```python
import functools

import numpy as np
import jax
import jax.numpy as jnp
from jax import lax
from jax.experimental import pallas as pl
from jax.experimental.pallas import tpu as pltpu
from jax.experimental.pallas import tpu_sc as plsc

F32 = jnp.float32
BF16 = jnp.bfloat16

D_MODEL = 1024
HEAD_DIM = 64
N_HEADS = 16
GQA_GROUP = 8
N_KV_HEADS = 2
Q_WIDTH = 1024
KV_WIDTH = 128
WINDOW = 128
ROT_DIM = 16
ROPE_THETA = 500000.0
CHUNK = 128
GMLP_WIDTH = 1024
GMLP_GROUPS = 8
N_EXPERTS = 32
TOP_K = 4
SWIGLU_LIMIT = 7.0
SWIGLU_ALPHA = 1.702
RMS_EPS = 1e-5
LN_EPS = 1e-5
NEG_INF = -1e30
PAST_LEN = 16384

LANES = 128
VMEM_LIMIT = 56 * 1024 * 1024

PROJ_ROWS = 512
MIX_ROWS = 512
ROUTE_ROWS = 512
EXPERT_ROWS = 256
W_PIECES = 8
COMBINE_ROWS = 256

SC_CORES = 2
SC_WORKERS = 32
SC_ROWS = 64

_C_Q, _C_KV, _C_U, _C_VG, _C_GA, _C_GB, _C_END = 0, 1024, 1280, 2304, 3328, 4352, 5376


def _params(sem):
    return pltpu.CompilerParams(dimension_semantics=sem, vmem_limit_bytes=VMEM_LIMIT)


def _rms(x, g):
    return x * lax.rsqrt(jnp.mean(x * x, axis=-1, keepdims=True) + RMS_EPS) * g


def _pack_bf16_pair(lo, hi):
    lo_bits = lax.bitcast_convert_type(lo.astype(BF16).astype(F32), jnp.uint32)
    hi_bits = lax.bitcast_convert_type(hi.astype(BF16).astype(F32), jnp.uint32)
    return (lo_bits >> 16) | hi_bits


def _unpack_bf16_pair(words):
    lo = lax.bitcast_convert_type(words << 16, F32)
    hi = lax.bitcast_convert_type(words & jnp.uint32(0xFFFF0000), F32)
    return jnp.concatenate([lo, hi], axis=1)


def _proj_body(x_ref, g_ref, w_ref, cs_ref, rot_ref, lng_ref, lnb_ref,
               q_ref, k_ref, v_ref, a_ref, vn_ref, sgb_ref):
    h = _rms(x_ref[...], g_ref[...]).astype(BF16)
    tabs = lax.dot_general(cs_ref[...], rot_ref[...], (((0,), (0,)), ((), ())),
                           preferred_element_type=F32, precision=lax.Precision.HIGHEST)
    rc, rs1, rs2 = tabs[:, :LANES], tabs[:, LANES:2 * LANES], tabs[:, 2 * LANES:]

    def rot(z):
        return z * rc + pltpu.roll(z, LANES - ROT_DIM // 2, 1) * rs1 + pltpu.roll(z, ROT_DIM // 2, 1) * rs2

    def mm(lo, hi):
        return jnp.dot(h, w_ref[:, lo:hi], preferred_element_type=F32)

    zq = mm(_C_Q, _C_KV)
    for c in range(Q_WIDTH // LANES):
        sl = slice(c * LANES, (c + 1) * LANES)
        q_ref[:, sl] = (rot(zq[:, sl]) * (HEAD_DIM ** -0.5)).astype(BF16)
    zkv = mm(_C_KV, _C_U)
    k_ref[...] = rot(zkv[:, :KV_WIDTH])
    v_ref[...] = zkv[:, KV_WIDTH:]
    a_ref[...] = jax.nn.sigmoid(mm(_C_GA, _C_GB)) * jax.nn.gelu(mm(_C_U, _C_VG))
    zv = jax.nn.gelu(mm(_C_VG, _C_GA))
    zc = zv - jnp.mean(zv, axis=-1, keepdims=True)
    var = jnp.mean(zc * zc, axis=-1, keepdims=True)
    vn_ref[...] = zc * lax.rsqrt(var + LN_EPS) * lng_ref[...] + lnb_ref[...]
    sgb_ref[...] = jax.nn.sigmoid(mm(_C_GB, _C_END))


def _proj(x, norm_g, w_in_bf, cs, ln_g, ln_b):
    n = x.shape[0]
    tm = PROJ_ROWS
    row = lambda w: pl.BlockSpec((tm, w), lambda i: (i, 0))
    full = lambda a: pl.BlockSpec(a.shape, lambda i: (0,) * a.ndim)
    rot = jnp.asarray(_ROT_EXPAND)
    return pl.pallas_call(
        _proj_body,
        grid=(n // tm,),
        in_specs=[row(D_MODEL), full(norm_g), full(w_in_bf), pl.BlockSpec((cs.shape[0], tm), lambda i: (0, i)),
                  full(rot),
                  full(ln_g), full(ln_b)],
        out_specs=[row(Q_WIDTH), row(KV_WIDTH), row(KV_WIDTH), row(GMLP_WIDTH), row(GMLP_WIDTH), row(D_MODEL)],
        out_shape=[jax.ShapeDtypeStruct((n, Q_WIDTH), BF16),
                   jax.ShapeDtypeStruct((n, KV_WIDTH), F32),
                   jax.ShapeDtypeStruct((n, KV_WIDTH), F32),
                   jax.ShapeDtypeStruct((n, GMLP_WIDTH), F32),
                   jax.ShapeDtypeStruct((n, GMLP_WIDTH), F32),
                   jax.ShapeDtypeStruct((n, D_MODEL), F32)],
        compiler_params=_params(("arbitrary",)),
        name="proj",
    )(x, norm_g, w_in_bf, cs, rot, ln_g, ln_b)


_ROT_COLS = 32


def _rot_expand():
    half = ROT_DIM // 2
    m = np.zeros((_ROT_COLS, 3 * LANES), np.float32)
    for lane in range(LANES):
        d = lane % HEAD_DIM
        if d < ROT_DIM:
            m[d % half, lane] = 1.0
        else:
            m[2 * half, lane] = 1.0
        if d < half:
            m[half + d, LANES + lane] = -1.0
        elif d < ROT_DIM:
            m[half + d - half, 2 * LANES + lane] = 1.0
    return m


_ROT_EXPAND = _rot_expand()


def _rotary_inputs(pos):
    half = ROT_DIM // 2
    inv_freq = ROPE_THETA ** (-jnp.arange(half, dtype=F32) / half)
    ang = inv_freq[:, None] * pos.astype(F32)[None, :]
    n = pos.shape[0]
    return jnp.concatenate([jnp.cos(ang), jnp.sin(ang), jnp.ones((1, n), F32),
                            jnp.zeros((_ROT_COLS - 2 * half - 1, n), F32)], axis=0)


def _finish_rows(merged_bf, x, wout_ref, g2_ref, wr_ref, br_ref, xmid_ref, xn2_ref, lg_ref):
    xm = x + jnp.dot(merged_bf, wout_ref[...], preferred_element_type=F32)
    xmid_ref[...] = xm
    xn = _rms(xm, g2_ref[...])
    x_hi = xn.astype(BF16)
    x_lo = (xn - x_hi.astype(F32)).astype(BF16)
    w_hl = wr_ref[...]
    p_hi = jnp.dot(x_hi, w_hl, preferred_element_type=F32)
    p_lo = jnp.dot(x_lo, w_hl[:, :N_EXPERTS], preferred_element_type=F32)
    lg = p_hi[:, :N_EXPERTS] + (p_hi[:, N_EXPERTS:] + p_lo) + br_ref[...]
    wide = jnp.concatenate([lg, jnp.zeros((lg.shape[0], LANES - N_EXPERTS), F32)], axis=1)
    lg_ref[...] = wide.T[:N_EXPERTS]
    xn2_ref[...] = _pack_bf16_pair(xn[:, :D_MODEL // 2], xn[:, D_MODEL // 2:])


def _mix_body(sinks_ref, q_ref, k_ref, kp_ref, v_ref, vp_ref, a_ref, vn_ref, sgb_ref, x_ref,
              wsp_ref, bsp_ref, wout_ref, g2_ref, wr_ref, br_ref,
              xmid_ref, xn2_ref, lg_ref, kcat, vcat, mrg):
    i = pl.program_id(0)
    nsub = MIX_ROWS // WINDOW
    kcat[0:WINDOW] = kp_ref[...]
    kcat[WINDOW:] = k_ref[...]
    vcat[0:WINDOW] = vp_ref[...]
    vcat[WINDOW:] = v_ref[...]

    pair_rows = (GQA_GROUP // 2) * WINDOW
    rq = lax.broadcasted_iota(jnp.int32, (pair_rows, 4 * WINDOW), 0) & (WINDOW - 1)
    ck = lax.broadcasted_iota(jnp.int32, (pair_rows, 4 * WINDOW), 1) & (2 * WINDOW - 1)
    band = (ck > rq) & (ck <= rq + WINDOW)
    lane_kv = lax.broadcasted_iota(jnp.int32, (2 * WINDOW, LANES), 1)
    lane_o = lax.broadcasted_iota(jnp.int32, (pair_rows, LANES), 1)
    row_p = lax.broadcasted_iota(jnp.int32, (pair_rows, 1), 0) >> 7

    def sub(j, carry):
        off = pl.multiple_of(j * WINDOW, WINDOW)
        rows = pl.ds(off, WINDOW)
        for g in range(GMLP_GROUPS):
            cols = slice(g * LANES, (g + 1) * LANES)
            s = jnp.dot(wsp_ref[g], vn_ref[rows, cols].astype(BF16), preferred_element_type=F32) + bsp_ref[g]
            mrg[rows, cols] = a_ref[rows, cols] * s
        kblk = kcat[pl.ds(off, 2 * WINDOW), :]
        vblk = vcat[pl.ds(off, 2 * WINDOW), :]
        kswp = pltpu.roll(kblk, HEAD_DIM, 1)
        vswp = pltpu.roll(vblk, HEAD_DIM, 1)
        kmin = jnp.where(jnp.logical_and(i == 0, j == 0), WINDOW, 0)
        allowed = band & (ck >= kmin)
        for kk in range(N_KV_HEADS):
            lo_src, hi_src = (kblk, kswp) if kk == 0 else (kswp, kblk)
            kbd = jnp.concatenate([jnp.where(lane_kv < HEAD_DIM, lo_src, 0.0),
                                   jnp.where(lane_kv >= HEAD_DIM, hi_src, 0.0)], axis=0).astype(BF16)
            lo_src, hi_src = (vblk, vswp) if kk == 0 else (vswp, vblk)
            vbd = jnp.concatenate([jnp.where(lane_kv < HEAD_DIM, lo_src, 0.0),
                                   jnp.where(lane_kv >= HEAD_DIM, hi_src, 0.0)], axis=0).astype(BF16)
            pair0 = kk * (GQA_GROUP // 2)
            qs = jnp.concatenate([q_ref[rows, (pair0 + p) * LANES:(pair0 + p + 1) * LANES]
                                  for p in range(GQA_GROUP // 2)], axis=0)
            lg = lax.dot_general(qs, kbd, (((1,), (1,)), ((), ())), preferred_element_type=F32)
            lg = jnp.where(allowed, lg, NEG_INF)
            h0 = kk * GQA_GROUP
            se = jnp.full((pair_rows, 1), sinks_ref[h0], F32)
            so = jnp.full((pair_rows, 1), sinks_ref[h0 + 1], F32)
            for p in range(1, GQA_GROUP // 2):
                se = jnp.where(row_p == p, sinks_ref[h0 + 2 * p], se)
                so = jnp.where(row_p == p, sinks_ref[h0 + 2 * p + 1], so)
            le, lo = lg[:, :2 * WINDOW], lg[:, 2 * WINDOW:]
            me = jnp.maximum(jnp.max(le, axis=1, keepdims=True), se)
            mo = jnp.maximum(jnp.max(lo, axis=1, keepdims=True), so)
            pe = jnp.exp(le - me)
            po = jnp.exp(lo - mo)
            de = jnp.sum(pe, axis=1, keepdims=True) + jnp.exp(se - me)
            do = jnp.sum(po, axis=1, keepdims=True) + jnp.exp(so - mo)
            pr = jnp.concatenate([pe, po], axis=1).astype(BF16)
            o = jnp.dot(pr, vbd, preferred_element_type=F32)
            o = o / jnp.where(lane_o < HEAD_DIM, de, do)
            for p in range(GQA_GROUP // 2):
                cols = slice((pair0 + p) * LANES, (pair0 + p + 1) * LANES)
                mrg[rows, cols] += sgb_ref[rows, cols] * o[p * WINDOW:(p + 1) * WINDOW]
        return carry

    lax.fori_loop(0, nsub, sub, 0)
    _finish_rows(mrg[...].astype(BF16), x_ref[...], wout_ref, g2_ref, wr_ref, br_ref, xmid_ref, xn2_ref, lg_ref)


def _mix(sinks, q, k, v, a, vn, sgb, x, wsp, bsp, wout, g2, wr, br):
    n = x.shape[0]
    tm = MIX_ROWS
    nsub = tm // WINDOW
    row = lambda w: pl.BlockSpec((tm, w), lambda i: (i, 0))
    prev = pl.BlockSpec((WINDOW, KV_WIDTH), lambda i: (jnp.maximum(i * nsub - 1, 0), 0))
    full = lambda arr: pl.BlockSpec(arr.shape, lambda i: (0,) * arr.ndim)
    smem = pl.BlockSpec(memory_space=pltpu.SMEM)
    return pl.pallas_call(
        _mix_body,
        grid=(n // tm,),
        in_specs=[smem, row(Q_WIDTH), row(KV_WIDTH), prev, row(KV_WIDTH), prev,
                  row(GMLP_WIDTH), row(GMLP_WIDTH), row(D_MODEL), row(D_MODEL),
                  full(wsp), full(bsp), full(wout), full(g2), full(wr), full(br)],
        out_specs=[row(D_MODEL), row(D_MODEL // 2), pl.BlockSpec((N_EXPERTS, tm), lambda i: (0, i))],
        out_shape=[jax.ShapeDtypeStruct((n, D_MODEL), F32),
                   jax.ShapeDtypeStruct((n, D_MODEL // 2), jnp.uint32),
                   jax.ShapeDtypeStruct((N_EXPERTS, n), F32)],
        scratch_shapes=[pltpu.VMEM((tm + WINDOW, KV_WIDTH), F32),
                        pltpu.VMEM((tm + WINDOW, KV_WIDTH), F32),
                        pltpu.VMEM((tm, D_MODEL), F32)],
        compiler_params=_params(("arbitrary",)),
        name="mix_prompt",
    )(sinks, q, k, k, v, v, a, vn, sgb, x, wsp, bsp, wout, g2, wr, br)


def _sample_attn_body(sink_ref, q_ref, k_ref, v_ref, o_ref):
    q = q_ref[...]
    k = k_ref[...]
    v = v_ref[...]
    nq, nk = q.shape[1], k.shape[1]
    heads = lambda t, kk: t[:, :, kk * HEAD_DIM:(kk + 1) * HEAD_DIM].astype(BF16)
    row = lax.broadcasted_iota(jnp.int32, (1, nq, nk), 1)
    first_kv = (row % N_HEADS) < GQA_GROUP
    lg = jnp.where(first_kv,
                   jnp.einsum("bqd,bkd->bqk", q, heads(k, 0), preferred_element_type=F32),
                   jnp.einsum("bqd,bkd->bqk", q, heads(k, 1), preferred_element_type=F32))
    t = row // N_HEADS
    j = lax.broadcasted_iota(jnp.int32, (1, nq, nk), 2)
    lg = jnp.where((j > t) & (j <= t + WINDOW), lg, NEG_INF)
    sink = sink_ref[...][None]
    m = jnp.maximum(jnp.max(lg, axis=2, keepdims=True), sink)
    p = jnp.exp(lg - m)
    den = jnp.sum(p, axis=2, keepdims=True) + jnp.exp(sink - m)
    pb = p.astype(BF16)
    row_o = lax.broadcasted_iota(jnp.int32, (1, nq, HEAD_DIM), 1)
    o = jnp.where((row_o % N_HEADS) < GQA_GROUP,
                  jnp.einsum("bqk,bkd->bqd", pb, heads(v, 0), preferred_element_type=F32),
                  jnp.einsum("bqk,bkd->bqd", pb, heads(v, 1), preferred_element_type=F32))
    o_ref[...] = o / den


def _sample_attn(sink_col, q3, k_all, v_all):
    nb = q3.shape[0]
    bb = 32
    blk = lambda a: pl.BlockSpec((bb,) + a.shape[1:], lambda b: (b, 0, 0))
    return pl.pallas_call(
        _sample_attn_body,
        grid=(nb // bb,),
        in_specs=[pl.BlockSpec(sink_col.shape, lambda b: (0, 0)), blk(q3), blk(k_all), blk(v_all)],
        out_specs=blk(q3),
        out_shape=jax.ShapeDtypeStruct(q3.shape, F32),
        compiler_params=_params(("arbitrary",)),
        name="attn_sample",
    )(sink_col, q3, k_all, v_all)


def _mix_sample_body(a_ref, vn_ref, sgb_ref, o_ref, x_ref, coef_ref, bias_ref,
                     wout_ref, g2_ref, wr_ref, br_ref, xmid_ref, xn2_ref, lg_ref):
    vn = vn_ref[...]
    n, width = vn.shape
    rows8 = lambda t: t.reshape(n // 8, 8, width)
    s = bias_ref[...][None] + coef_ref[0][None] * rows8(vn)
    for d in range(1, coef_ref.shape[0]):
        s = s + coef_ref[d][None] * rows8(pltpu.roll(vn, d, 0))
    merged = a_ref[...] * s.reshape(n, width) + sgb_ref[...] * o_ref[...]
    _finish_rows(merged.astype(BF16), x_ref[...], wout_ref, g2_ref, wr_ref, br_ref, xmid_ref, xn2_ref, lg_ref)


def _mix_sample(a, vn, sgb, o, x, coef, bias, wout, g2, wr, br):
    n = x.shape[0]
    args = (a, vn, sgb, o, x, coef, bias, wout, g2, wr, br)
    full = lambda arr: pl.BlockSpec(arr.shape, lambda i: (0,) * arr.ndim)
    return pl.pallas_call(
        _mix_sample_body,
        grid=(1,),
        in_specs=[full(arr) for arr in args],
        out_specs=[pl.BlockSpec((n, D_MODEL), lambda i: (0, 0)), pl.BlockSpec((n, D_MODEL // 2), lambda i: (0, 0)),
                   pl.BlockSpec((N_EXPERTS, n), lambda i: (0, 0))],
        out_shape=[jax.ShapeDtypeStruct((n, D_MODEL), F32),
                   jax.ShapeDtypeStruct((n, D_MODEL // 2), jnp.uint32),
                   jax.ShapeDtypeStruct((N_EXPERTS, n), F32)],
        compiler_params=_params(("arbitrary",)),
        name="mix_sample",
    )(*args)


def _rows8(rows, dtype):
    n = rows[0].shape[1]
    sub = lax.broadcasted_iota(jnp.int32, (8, n), 0)
    out = jnp.zeros((8, n), dtype)
    for kx, r in enumerate(rows):
        out = jnp.where(sub == kx, r.astype(dtype), out)
    return out


def _route_body(nblk_p, nblk, lgp_ref, lgs_ref, gate_ref, dest_ref, meta_ref, idx_s, rank_s, base):
    i = pl.program_id(0)

    @pl.when(i == 0)
    def _():
        base[...] = jnp.zeros_like(base)

    l = jnp.where(jnp.full(lgp_ref.shape, i, jnp.int32) < nblk_p, lgp_ref[...], lgs_ref[...])
    tb = l.shape[1]
    sub = lax.broadcasted_iota(jnp.int32, l.shape, 0).astype(F32)
    vals, idxs, sels = [], [], []
    for _ in range(TOP_K):
        m = jnp.max(l, axis=0, keepdims=True)
        ik = jnp.min(jnp.where(l == m, sub, float(N_EXPERTS)), axis=0, keepdims=True)
        sel = sub == ik
        l = jnp.where(sel, -jnp.inf, l)
        vals.append(m)
        idxs.append(ik)
        sels.append(sel)
    es = [jnp.exp(vk - vals[0]) for vk in vals]
    den = es[0] + es[1] + es[2] + es[3]
    onehot = jnp.zeros(l.shape, F32)
    for sel in sels:
        onehot = onehot + sel.astype(F32)
    earlier = (lax.broadcasted_iota(jnp.int32, (tb, tb), 0) < lax.broadcasted_iota(jnp.int32, (tb, tb), 1))
    before = jnp.dot(onehot.astype(BF16), earlier.astype(BF16), preferred_element_type=F32) + base[...]
    ranks = [jnp.sum(jnp.where(sel, before, 0.0), axis=0, keepdims=True) for sel in sels]
    base[...] += jnp.sum(onehot, axis=1, keepdims=True)
    idx_s[i] = _rows8(idxs, F32)
    rank_s[i] = _rows8(ranks, F32)
    gates = jnp.concatenate([_rows8([e / den for e in es], F32), jnp.zeros((LANES - 8, tb), F32)], axis=0)
    gate_ref[...] = gates.T

    @pl.when(i == nblk - 1)
    def _():
        cnt = base[...]
        padded = jnp.ceil(cnt / EXPERT_ROWS) * EXPERT_ROWS
        lower = (lax.broadcasted_iota(jnp.int32, (N_EXPERTS, N_EXPERTS), 1) <
                 lax.broadcasted_iota(jnp.int32, (N_EXPERTS, N_EXPERTS), 0)).astype(F32)
        pstart = jnp.dot(lower, jnp.broadcast_to(padded, (N_EXPERTS, LANES)), preferred_element_type=F32,
                         precision=lax.Precision.HIGHEST)[:, :1]
        lane = lax.broadcasted_iota(jnp.int32, (N_EXPERTS, LANES), 1)
        meta = jnp.where(lane == 0, pstart / EXPERT_ROWS,
                         jnp.where(lane == 1, padded / EXPERT_ROWS, jnp.where(lane == 2, cnt, 0.0)))
        meta_ref[...] = meta.astype(jnp.int32)
        sub_e = lax.broadcasted_iota(jnp.int32, (N_EXPERTS, tb), 0).astype(F32)
        for b in range(nblk):
            idx, rank = idx_s[b], rank_s[b]
            rows = [jnp.sum(jnp.where(sub_e == idx[kx:kx + 1], pstart, 0.0), axis=0, keepdims=True)
                    + rank[kx:kx + 1] for kx in range(TOP_K)]
            dest_ref[:, b * tb:(b + 1) * tb] = _rows8(rows, jnp.int32)


def _route(logits_p, logits_s):
    tb = ROUTE_ROWS
    nblk_p, nblk_s = logits_p.shape[1] // tb, logits_s.shape[1] // tb
    nblk = nblk_p + nblk_s
    n = nblk * tb
    assert n == logits_p.shape[1] + logits_s.shape[1]
    return pl.pallas_call(
        functools.partial(_route_body, nblk_p, nblk),
        grid=(nblk,),
        in_specs=[pl.BlockSpec((N_EXPERTS, tb), lambda i: (0, jnp.minimum(i, nblk_p - 1))),
                  pl.BlockSpec((N_EXPERTS, tb), lambda i: (0, jnp.maximum(i - nblk_p, 0)))],
        out_specs=[pl.BlockSpec((tb, LANES), lambda i: (i, 0)),
                   pl.BlockSpec((8, n), lambda i: (0, 0)),
                   pl.BlockSpec((N_EXPERTS, LANES), lambda i: (0, 0))],
        out_shape=[jax.ShapeDtypeStruct((n, LANES), F32),
                   jax.ShapeDtypeStruct((8, n), jnp.int32),
                   jax.ShapeDtypeStruct((N_EXPERTS, LANES), jnp.int32)],
        scratch_shapes=[pltpu.VMEM((nblk, 8, tb), F32), pltpu.VMEM((nblk, 8, tb), F32),
                        pltpu.VMEM((N_EXPERTS, 1), F32)],
        compiler_params=_params(("arbitrary",)),
        name="route",
    )(logits_p, logits_s)


def _sc_mesh():
    return plsc.VectorSubcoreMesh(core_axis_name="c", subcore_axis_name="s")


def _sc_worker():
    return lax.axis_index("s") * SC_CORES + lax.axis_index("c")


def _sc_dispatch(x_p, x_s, dest_t, n_slots):
    chunk = SC_ROWS
    n_p, n_s = x_p.shape[0], x_s.shape[0]
    per_w = n_p // (SC_WORKERS * chunk)
    ns_chunks = n_s // chunk
    assert per_w * SC_WORKERS * chunk == n_p and per_w % 2 == 0
    assert ns_chunks * chunk == n_s and ns_chunks <= SC_WORKERS
    d3 = dest_t.reshape(dest_t.shape[0], (n_p + n_s) // chunk, chunk)
    width, dtype = x_p.shape[1], x_p.dtype

    @functools.partial(
        pl.kernel, mesh=_sc_mesh(),
        out_type=jax.ShapeDtypeStruct((n_slots, width), dtype),
        scratch_types=[pltpu.VMEM((TOP_K, per_w, chunk), jnp.int32),
                       pltpu.VMEM((TOP_K, 1, chunk), jnp.int32),
                       pltpu.VMEM((2, chunk, width), dtype),
                       pltpu.SemaphoreType.DMA, pltpu.SemaphoreType.DMA],
        compiler_params=pltpu.CompilerParams(use_tc_tiling_on_sc=True),
        name="dispatch")
    def run(xp_hbm, xs_hbm, d_hbm, out_hbm, ip_v, is_v, rows_v, rsem, wsem):
        wid = _sc_worker()
        pltpu.sync_copy(d_hbm.at[pl.ds(0, TOP_K), pl.ds(wid * per_w, per_w)], ip_v)

        def read(j, slot):
            return pltpu.make_async_copy(xp_hbm.at[pl.ds((wid * per_w + j) * chunk, chunk)], rows_v.at[slot], rsem)

        def scatter(idx_v, j, slot):
            copies = [pltpu.async_copy(rows_v.at[slot], out_hbm.at[idx_v.at[kx, j]], wsem) for kx in range(TOP_K)]
            for cp in copies:
                cp.wait()

        read(0, 0).start()

        def body(h, carry):
            j = 2 * h
            read(j, 0).wait()
            read(j + 1, 1).start()
            scatter(ip_v, j, 0)
            read(j + 1, 1).wait()

            @pl.when(j + 2 < per_w)
            def _():
                read(j + 2, 0).start()

            scatter(ip_v, j + 1, 1)
            return carry

        lax.fori_loop(0, per_w // 2, body, 0)

        @pl.when(wid < ns_chunks)
        def _():
            pltpu.sync_copy(d_hbm.at[pl.ds(0, TOP_K), pl.ds(n_p // chunk + wid, 1)], is_v)
            pltpu.sync_copy(xs_hbm.at[pl.ds(wid * chunk, chunk)], rows_v.at[0])
            scatter(is_v, 0, 0)

    return run(x_p, x_s, d3)


def _sc_collect(y_sorted, dest_t, n_p, n_s):
    chunk = SC_ROWS
    per_choice = SC_WORKERS // TOP_K
    per_w = n_p // (per_choice * chunk)
    assert per_w * per_choice * chunk == n_p and per_w % 2 == 0
    assert n_s == per_choice * chunk
    d3 = dest_t.reshape(dest_t.shape[0], (n_p + n_s) // chunk, chunk)
    width, dtype = y_sorted.shape[1], y_sorted.dtype

    @functools.partial(
        pl.kernel, mesh=_sc_mesh(),
        out_type=[jax.ShapeDtypeStruct((TOP_K * n_p, width), dtype), jax.ShapeDtypeStruct((TOP_K * n_s, width), dtype)],
        scratch_types=[pltpu.VMEM((per_w, chunk), jnp.int32),
                       pltpu.VMEM((1, chunk), jnp.int32),
                       pltpu.VMEM((2, chunk, width), dtype),
                       pltpu.SemaphoreType.DMA, pltpu.SemaphoreType.DMA],
        compiler_params=pltpu.CompilerParams(use_tc_tiling_on_sc=True),
        name="collect")
    def run(y_hbm, d_hbm, op_hbm, os_hbm, ip_v, is_v, rows_v, gsem, wsem):
        wid = _sc_worker()
        choice = wid // per_choice
        part = wid % per_choice
        pltpu.sync_copy(d_hbm.at[choice, pl.ds(part * per_w, per_w)], ip_v)
        pltpu.sync_copy(d_hbm.at[choice, pl.ds(n_p // chunk + part, 1)], is_v)

        def gather(idx_v, j, slot):
            return pltpu.make_async_copy(y_hbm.at[idx_v.at[j]], rows_v.at[slot], gsem)

        def write(j, slot):
            return pltpu.make_async_copy(rows_v.at[slot], op_hbm.at[pl.ds((wid * per_w + j) * chunk, chunk)], wsem)

        gather(ip_v, 0, 0).start()

        def body(h, carry):
            j = 2 * h
            gather(ip_v, j, 0).wait()

            @pl.when(h > 0)
            def _():
                write(j - 1, 1).wait()

            gather(ip_v, j + 1, 1).start()
            write(j, 0).start()
            gather(ip_v, j + 1, 1).wait()
            write(j, 0).wait()

            @pl.when(j + 2 < per_w)
            def _():
                gather(ip_v, j + 2, 0).start()

            write(j + 1, 1).start()
            return carry

        lax.fori_loop(0, per_w // 2, body, 0)
        write(per_w - 1, 1).wait()

        gather(is_v, 0, 0).start()
        gather(is_v, 0, 0).wait()
        pltpu.sync_copy(rows_v.at[0], os_hbm.at[pl.ds(wid * chunk, chunk)])

    return run(y_sorted, d3)


def _expert_body(blk0_ref, nblk_ref, cnt_ref, wup_hbm, wdn_hbm, bup_ref, bdn_ref, x_hbm, y_hbm,
                 wup_f, wdn_f, wup_s, wdn_s, xbuf, obuf, w_sem, in_sem, out_sem):
    e = pl.program_id(0)
    nb = nblk_ref[e]
    blk0 = blk0_ref[e]
    cnt = cnt_ref[e]
    tm = EXPERT_ROWS
    pair = 2 * LANES
    wslot = e % 2
    up_rows = D_MODEL // W_PIECES
    dn_rows = D_MODEL // (W_PIECES // 2)

    def w_piece(hbm, buf, ex, slot, p, rows):
        start = p * rows if isinstance(p, int) else pl.multiple_of(p * rows, rows)
        r = pl.ds(start, rows)
        return pltpu.make_async_copy(hbm.at[ex, r], buf.at[slot, r], w_sem.at[slot])

    def w_start(ex, slot, p):
        w_piece(wup_hbm, wup_f, ex, slot, p, up_rows).start()
        if isinstance(p, int):
            if p < W_PIECES // 2:
                w_piece(wdn_hbm, wdn_f, ex, slot, p, dn_rows).start()
        else:
            @pl.when(p < W_PIECES // 2)
            def _():
                w_piece(wdn_hbm, wdn_f, ex, slot, p, dn_rows).start()

    def w_wait(ex, slot):
        for p in range(W_PIECES):
            w_piece(wup_hbm, wup_f, ex, slot, p, up_rows).wait()
        for p in range(W_PIECES // 2):
            w_piece(wdn_hbm, wdn_f, ex, slot, p, dn_rows).wait()

    @pl.when(e == 0)
    def _():
        for p in range(W_PIECES):
            w_start(0, 0, p)

    w_wait(e, wslot)
    wup_ref = wup_f.at[wslot]
    wdn_ref = wdn_f.at[wslot]
    more = e + 1 < N_EXPERTS

    def x_copy(i, slot):
        rows = pl.ds(pl.multiple_of((blk0 + i) * tm, tm), tm)
        return pltpu.make_async_copy(x_hbm.at[rows], xbuf.at[slot], in_sem.at[slot])

    def y_copy(i, slot):
        rows = pl.ds(pl.multiple_of((blk0 + i) * tm, tm), tm)
        return pltpu.make_async_copy(obuf.at[slot], y_hbm.at[rows], out_sem.at[slot])

    @pl.when(nb > 0)
    def _():
        x_copy(0, 0).start(priority=1)
        r = lax.broadcasted_iota(jnp.int32, (pair, pair), 0)
        c = lax.broadcasted_iota(jnp.int32, (pair, pair), 1)
        perm = (r == jnp.where(c < LANES, 2 * c, 2 * (c - LANES) + 1)).astype(BF16)
        for g in range(2 * D_MODEL // pair):
            cols = slice(g * pair, (g + 1) * pair)
            wup_s[g] = jnp.dot(wup_ref[:, cols].astype(BF16), perm, preferred_element_type=F32).astype(BF16)
        for g in range(D_MODEL // pair):
            wdn_s[g] = wdn_ref[:, g * pair:(g + 1) * pair].astype(BF16)

        def block(i, carry):
            slot = i % 2
            x_copy(i, slot).wait()

            @pl.when(i + 1 < nb)
            def _():
                x_copy(i + 1, 1 - slot).start(priority=1)

            @pl.when(i >= 2)
            def _():
                y_copy(i - 2, slot).wait()

            @pl.when(jnp.logical_and(more, i < W_PIECES))
            def _():
                w_start(e + 1, 1 - wslot, i)

            row = lax.broadcasted_iota(jnp.int32, (tm, 1), 0)
            x = jnp.where(row < cnt - i * tm, _unpack_bf16_pair(xbuf[slot]), 0.0).astype(BF16)
            acts = []
            for g in range(2 * D_MODEL // pair):
                cols = slice(g * pair, (g + 1) * pair)
                h = jnp.dot(x, wup_s[g], preferred_element_type=F32) + bup_ref[0, :, cols]
                glu = jnp.minimum(h[:, :LANES], SWIGLU_LIMIT)
                lin = jnp.clip(h[:, LANES:], -SWIGLU_LIMIT, SWIGLU_LIMIT)
                acts.append((glu * jax.nn.sigmoid(SWIGLU_ALPHA * glu) * (lin + 1.0)).astype(BF16))
            act = jnp.concatenate(acts, axis=1)
            half_groups = D_MODEL // pair // 2
            for g in range(half_groups):
                ys = []
                for gg in (g, g + half_groups):
                    cols = slice(gg * pair, (gg + 1) * pair)
                    ys.append(jnp.dot(act, wdn_s[gg], preferred_element_type=F32) + bdn_ref[0, :, cols])
                obuf[slot, :, g * pair:(g + 1) * pair] = _pack_bf16_pair(ys[0], ys[1])
            y_copy(i, slot).start(priority=1)
            return carry

        lax.fori_loop(0, nb, block, 0)

        @pl.when(nb >= 2)
        def _():
            y_copy(nb - 2, nb % 2).wait()

        y_copy(nb - 1, (nb - 1) % 2).wait()

    for p in range(W_PIECES):
        @pl.when(jnp.logical_and(more, p >= nb))
        def _():
            w_start(e + 1, 1 - wslot, p)


def _experts(blk0, nblk, cnt, x_sorted, w_up, w_down, b_up_grouped, b_down):
    tm = EXPERT_ROWS
    per_expert = lambda a: pl.BlockSpec((1,) + a.shape[1:], lambda e, b0, nb, ct: (e, 0, 0))
    hbm = pl.BlockSpec(memory_space=pl.ANY)
    grid_spec = pltpu.PrefetchScalarGridSpec(
        num_scalar_prefetch=3,
        grid=(N_EXPERTS,),
        in_specs=[hbm, hbm, per_expert(b_up_grouped), per_expert(b_down), hbm],
        out_specs=hbm,
        scratch_shapes=[pltpu.VMEM((2,) + w_up.shape[1:], F32), pltpu.VMEM((2,) + w_down.shape[1:], F32),
                        pltpu.VMEM((2 * D_MODEL // (2 * LANES), D_MODEL, 2 * LANES), BF16),
                        pltpu.VMEM((D_MODEL // (2 * LANES), D_MODEL, 2 * LANES), BF16),
                        pltpu.VMEM((2, tm, x_sorted.shape[1]), x_sorted.dtype),
                        pltpu.VMEM((2, tm, D_MODEL // 2), jnp.uint32),
                        pltpu.SemaphoreType.DMA((2,)), pltpu.SemaphoreType.DMA((2,)), pltpu.SemaphoreType.DMA((2,))],
    )
    return pl.pallas_call(
        _expert_body,
        grid_spec=grid_spec,
        out_shape=jax.ShapeDtypeStruct((x_sorted.shape[0], D_MODEL // 2), jnp.uint32),
        compiler_params=_params(("arbitrary",)),
        name="experts",
    )(blk0, nblk, cnt, w_up, w_down, b_up_grouped, b_down, x_sorted)


def _combine_body(gate_ref, xmid_ref, gfin_ref, y0_ref, y1_ref, y2_ref, y3_ref, out_ref):
    gate = gate_ref[...]
    moe = _unpack_bf16_pair(y0_ref[...]) * gate[:, 0:1]
    for kx, y_ref in enumerate((y1_ref, y2_ref, y3_ref), start=1):
        moe = moe + _unpack_bf16_pair(y_ref[...]) * gate[:, kx:kx + 1]
    out_ref[...] = _rms(xmid_ref[...] + moe, gfin_ref[...])


def _combine(gates, first_token, xmid, gfin, y_rows):
    n = xmid.shape[0]
    tt = COMBINE_ROWS
    nblk = n // tt
    blk0 = first_token // tt
    assert blk0 * tt == first_token
    choice = lambda kx: pl.BlockSpec((tt, y_rows.shape[1]), lambda i: (i + kx * nblk, 0))
    return pl.pallas_call(
        _combine_body,
        grid=(nblk,),
        in_specs=[pl.BlockSpec((tt, LANES), lambda i: (i + blk0, 0)),
                  pl.BlockSpec((tt, D_MODEL), lambda i: (i, 0)),
                  pl.BlockSpec((1, D_MODEL), lambda i: (0, 0))] + [choice(kx) for kx in range(TOP_K)],
        out_specs=pl.BlockSpec((tt, D_MODEL), lambda i: (i, 0)),
        out_shape=jax.ShapeDtypeStruct((n, D_MODEL), F32),
        compiler_params=_params(("arbitrary",)),
        name="combine",
    )(gates, xmid, gfin, y_rows, y_rows, y_rows, y_rows)


def kernel(x_prompt, x_sample, cache_k_win, cache_v_win, norm_attn_g, w_in, ln_v_g, ln_v_b, w_spatial, b_spatial,
           attn_sinks, w_out, norm_ffn_g, w_router, b_router, w_up, b_up, w_down, b_down, norm_final_g):
    bp, tp, _ = x_prompt.shape
    bs, ts, _ = x_sample.shape
    w_buf = cache_k_win.shape[2]
    assert bp == 1 and tp % MIX_ROWS == 0 and w_buf == WINDOW and (bs * ts) % PROJ_ROWS == 0 and 8 % ts == 0
    n_p, n_s = bp * tp, bs * ts
    row2 = lambda a: a.reshape(1, -1)

    w_in_bf = w_in[0].astype(BF16)
    w_out_bf = w_out[0].astype(BF16)
    tril = jnp.tril(jnp.ones((CHUNK, CHUNK), dtype=bool))
    wsp = jnp.where(tril[None], w_spatial[0], 0.0)
    wsp_bf = wsp.astype(BF16)
    bsp = jnp.broadcast_to(b_spatial[0][:, :, None], (GMLP_GROUPS, CHUNK, LANES))
    b_up_grouped = b_up[0].reshape(N_EXPERTS, -1, LANES, 2).transpose(0, 1, 3, 2).reshape(N_EXPERTS, 1, -1)
    bd = b_down[0][:, None, :]
    g1, g2, gfin = row2(norm_attn_g[0]), row2(norm_ffn_g[0]), row2(norm_final_g)
    lng, lnb = row2(ln_v_g[0]), row2(ln_v_b[0])
    wr_hi = w_router[0].astype(BF16)
    wr = jnp.concatenate([wr_hi, (w_router[0] - wr_hi.astype(F32)).astype(BF16)], axis=1)
    br = row2(b_router[0])
    sinks = attn_sinks[0]

    xp = x_prompt.reshape(n_p, D_MODEL)
    cs_p = _rotary_inputs(jnp.arange(tp, dtype=jnp.int32))
    q_p, k_p, v_p, a_p, vn_p, sgb_p = _proj(xp, g1, w_in_bf, cs_p, lng, lnb)
    xmid_p, xn2_p, lg_p = _mix(sinks, q_p, k_p, v_p, a_p, vn_p, sgb_p, xp, wsp_bf, bsp, w_out_bf, g2, wr, br)

    xs = x_sample.reshape(n_s, D_MODEL)
    pos_s = PAST_LEN + jnp.arange(ts, dtype=jnp.int32)
    cs_s = _rotary_inputs(jnp.tile(pos_s, bs))
    q_s, k_s, v_s, a_s, vn_s, sgb_s = _proj(xs, g1, w_in_bf, cs_s, lng, lnb)
    n_keys = w_buf + ts
    key_pad = jnp.zeros((bs, (-n_keys) % 8, KV_WIDTH), F32)
    with_new = lambda cache, new: jnp.concatenate(
        [cache[0].reshape(bs, w_buf, KV_WIDTH), new.reshape(bs, ts, KV_WIDTH), key_pad], axis=1)
    k_all = with_new(cache_k_win, k_s)
    v_all = with_new(cache_v_win, v_s)
    sink_col = jnp.tile(sinks, ts).reshape(ts * N_HEADS, 1)
    o_s = _sample_attn(sink_col, q_s.reshape(bs, ts * N_HEADS, HEAD_DIM), k_all, v_all).reshape(n_s, Q_WIDTH)
    t_idx = jnp.arange(ts)
    coef = jnp.stack([jnp.where((t_idx >= d)[None, :], wsp[:, t_idx, jnp.maximum(t_idx - d, 0)], 0.0)
                      for d in range(ts)])
    coef = jnp.repeat(coef.transpose(0, 2, 1), GMLP_WIDTH // GMLP_GROUPS, axis=2)
    coef = jnp.tile(coef, (1, 8 // ts, 1))
    bias = jnp.tile(jnp.repeat(b_spatial[0][:, :ts].T, GMLP_WIDTH // GMLP_GROUPS, axis=1), (8 // ts, 1))
    xmid_s, xn2_s, lg_s = _mix_sample(a_s, vn_s, sgb_s, o_s, xs, coef, bias, w_out_bf, g2, wr, br)

    n_tok = n_p + n_s
    gate_w, dest_t, meta = _route(lg_p, lg_s)
    tm = EXPERT_ROWS
    n_blocks = (n_tok * TOP_K) // tm + N_EXPERTS

    x_sorted = _sc_dispatch(xn2_p, xn2_s, dest_t, n_blocks * tm)
    y_sorted = _experts(meta[:, 0], meta[:, 1], meta[:, 2], x_sorted, w_up[0], w_down[0], b_up_grouped, bd)
    yrows_p, yrows_s = _sc_collect(y_sorted, dest_t, n_p, n_s)
    y_p = _combine(gate_w, 0, xmid_p, gfin, yrows_p)
    y_s = _combine(gate_w, n_p, xmid_s, gfin, yrows_s)

    k4 = lambda t: t.reshape(1, bp, -1, N_KV_HEADS, HEAD_DIM)
    return (y_p.reshape(bp, tp, D_MODEL),
            y_s.reshape(bs, ts, D_MODEL),
            k4(k_p[n_p - WINDOW:]),
            k4(v_p[n_p - WINDOW:]),
            vn_p[n_p - CHUNK:].reshape(1, bp, CHUNK, GMLP_WIDTH),
            k_all[:, ts:n_keys].reshape(1, bs, w_buf, N_KV_HEADS, HEAD_DIM),
            v_all[:, ts:n_keys].reshape(1, bs, w_buf, N_KV_HEADS, HEAD_DIM),
            vn_s.reshape(1, bs, ts, GMLP_WIDTH))
```

```python
import functools

import numpy as np
import jax
import jax.numpy as jnp
from jax import lax
from jax.experimental import pallas as pl
from jax.experimental.pallas import tpu as pltpu
from jax.experimental.pallas import tpu_sc as plsc

F32 = jnp.float32
BF16 = jnp.bfloat16

D_MODEL = 1024
HEAD_DIM = 64
N_HEADS = 16
GQA_GROUP = 8
N_KV_HEADS = 2
Q_WIDTH = 1024
KV_WIDTH = 128
WINDOW = 128
ROT_DIM = 16
ROPE_THETA = 500000.0
CHUNK = 128
GMLP_WIDTH = 1024
GMLP_GROUPS = 8
N_EXPERTS = 32
TOP_K = 4
SWIGLU_LIMIT = 7.0
SWIGLU_ALPHA = 1.702
RMS_EPS = 1e-5
LN_EPS = 1e-5
NEG_INF = -1e30
PAST_LEN = 16384

LANES = 128
VMEM_LIMIT = 56 * 1024 * 1024

PROJ_ROWS = 256
MIX_ROWS = 512
ROUTE_ROWS = 512
EXPERT_ROWS = 256
W_PIECES = 8
COMBINE_ROWS = 256

SC_CORES = 2
SC_WORKERS = 32
SC_ROWS = 64

_C_Q, _C_KV, _C_U, _C_VG, _C_GA, _C_GB, _C_END = 0, 1024, 1280, 2304, 3328, 4352, 5376


def _params(sem):
    return pltpu.CompilerParams(dimension_semantics=sem, vmem_limit_bytes=VMEM_LIMIT)


def _rms(x, g):
    return x * lax.rsqrt(jnp.mean(x * x, axis=-1, keepdims=True) + RMS_EPS) * g


def _pack_bf16_pair(lo, hi):
    lo_bits = lax.bitcast_convert_type(lo.astype(BF16).astype(F32), jnp.uint32)
    hi_bits = lax.bitcast_convert_type(hi.astype(BF16).astype(F32), jnp.uint32)
    return (lo_bits >> 16) | hi_bits


def _unpack_bf16_pair(words):
    lo = lax.bitcast_convert_type(words << 16, F32)
    hi = lax.bitcast_convert_type(words & jnp.uint32(0xFFFF0000), F32)
    return jnp.concatenate([lo, hi], axis=1)


def _proj_body(x_ref, g_ref, w_ref, cs_ref, rot_ref, lng_ref, lnb_ref,
               q_ref, k_ref, v_ref, a_ref, vn_ref, sgb_ref):
    h = _rms(x_ref[...], g_ref[...]).astype(BF16)
    tabs = lax.dot_general(cs_ref[...], rot_ref[...], (((0,), (0,)), ((), ())), preferred_element_type=F32)
    rc, rs1, rs2 = tabs[:, :LANES], tabs[:, LANES:2 * LANES], tabs[:, 2 * LANES:]

    def rot(z):
        return z * rc + pltpu.roll(z, LANES - ROT_DIM // 2, 1) * rs1 + pltpu.roll(z, ROT_DIM // 2, 1) * rs2

    def mm(lo, hi):
        return jnp.dot(h, w_ref[:, lo:hi], preferred_element_type=F32)

    zq = mm(_C_Q, _C_KV)
    for c in range(Q_WIDTH // LANES):
        sl = slice(c * LANES, (c + 1) * LANES)
        q_ref[:, sl] = (rot(zq[:, sl]) * (HEAD_DIM ** -0.5)).astype(BF16)
    zkv = mm(_C_KV, _C_U)
    k_ref[...] = rot(zkv[:, :KV_WIDTH])
    v_ref[...] = zkv[:, KV_WIDTH:]
    a_ref[...] = jax.nn.sigmoid(mm(_C_GA, _C_GB)) * jax.nn.gelu(mm(_C_U, _C_VG))
    zv = jax.nn.gelu(mm(_C_VG, _C_GA))
    zc = zv - jnp.mean(zv, axis=-1, keepdims=True)
    var = jnp.mean(zc * zc, axis=-1, keepdims=True)
    vn_ref[...] = zc * lax.rsqrt(var + LN_EPS) * lng_ref[...] + lnb_ref[...]
    sgb_ref[...] = jax.nn.sigmoid(mm(_C_GB, _C_END))


def _proj(x, norm_g, w_in_bf, cs, ln_g, ln_b):
    n = x.shape[0]
    tm = PROJ_ROWS
    row = lambda w: pl.BlockSpec((tm, w), lambda i: (i, 0))
    full = lambda a: pl.BlockSpec(a.shape, lambda i: (0,) * a.ndim)
    rot = jnp.asarray(np.tile(_ROT_EXPAND, (3, 1)), dtype=BF16)
    return pl.pallas_call(
        _proj_body,
        grid=(n // tm,),
        in_specs=[row(D_MODEL), full(norm_g), full(w_in_bf), pl.BlockSpec((cs.shape[0], tm), lambda i: (0, i)),
                  full(rot),
                  full(ln_g), full(ln_b)],
        out_specs=[row(Q_WIDTH), row(KV_WIDTH), row(KV_WIDTH), row(GMLP_WIDTH), row(GMLP_WIDTH), row(D_MODEL)],
        out_shape=[jax.ShapeDtypeStruct((n, Q_WIDTH), BF16),
                   jax.ShapeDtypeStruct((n, KV_WIDTH), F32),
                   jax.ShapeDtypeStruct((n, KV_WIDTH), F32),
                   jax.ShapeDtypeStruct((n, GMLP_WIDTH), F32),
                   jax.ShapeDtypeStruct((n, GMLP_WIDTH), F32),
                   jax.ShapeDtypeStruct((n, D_MODEL), F32)],
        compiler_params=_params(("arbitrary",)),
        name="proj",
    )(x, norm_g, w_in_bf, cs, rot, ln_g, ln_b)


_ROT_COLS = 32


def _rot_expand():
    half = ROT_DIM // 2
    m = np.zeros((_ROT_COLS, 3 * LANES), np.float32)
    for lane in range(LANES):
        d = lane % HEAD_DIM
        if d < ROT_DIM:
            m[d % half, lane] = 1.0
        else:
            m[2 * half, lane] = 1.0
        if d < half:
            m[half + d, LANES + lane] = -1.0
        elif d < ROT_DIM:
            m[half + d - half, 2 * LANES + lane] = 1.0
    return m


_ROT_EXPAND = _rot_expand()


def _rotary_inputs(pos):
    half = ROT_DIM // 2
    inv_freq = ROPE_THETA ** (-jnp.arange(half, dtype=F32) / half)
    ang = inv_freq[:, None] * pos.astype(F32)[None, :]
    n = pos.shape[0]
    cs = jnp.concatenate([jnp.cos(ang), jnp.sin(ang), jnp.ones((1, n), F32),
                          jnp.zeros((_ROT_COLS - 2 * half - 1, n), F32)], axis=0)
    hi = cs.astype(BF16)
    rest = cs - hi.astype(F32)
    mid = rest.astype(BF16)
    lo = (rest - mid.astype(F32)).astype(BF16)
    return jnp.concatenate([hi, mid, lo], axis=0)


def _finish_rows(merged_bf, x, wout_ref, g2_ref, wr_ref, br_ref, xmid_ref, xn2_ref, lg_ref):
    xm = x + jnp.dot(merged_bf, wout_ref[...], preferred_element_type=F32)
    xmid_ref[...] = xm
    xn = _rms(xm, g2_ref[...])
    x_hi = xn.astype(BF16)
    x_lo = (xn - x_hi.astype(F32)).astype(BF16)
    w_hl = wr_ref[...]
    p_hi = jnp.dot(x_hi, w_hl, preferred_element_type=F32)
    p_lo = jnp.dot(x_lo, w_hl[:, :N_EXPERTS], preferred_element_type=F32)
    lg = p_hi[:, :N_EXPERTS] + (p_hi[:, N_EXPERTS:] + p_lo) + br_ref[...]
    wide = jnp.concatenate([lg, jnp.zeros((lg.shape[0], LANES - N_EXPERTS), F32)], axis=1)
    lg_ref[...] = wide.T[:N_EXPERTS]
    xn2_ref[...] = _pack_bf16_pair(xn[:, :D_MODEL // 2], xn[:, D_MODEL // 2:])


def _mix_body(sinks_ref, q_ref, k_ref, kp_ref, v_ref, vp_ref, a_ref, vn_ref, sgb_ref, x_ref,
              wsp_ref, bsp_ref, wout_ref, g2_ref, wr_ref, br_ref,
              xmid_ref, xn2_ref, lg_ref, kcat, vcat, mrg):
    i = pl.program_id(0)
    nsub = MIX_ROWS // WINDOW
    kcat[0:WINDOW] = kp_ref[...]
    kcat[WINDOW:] = k_ref[...]
    vcat[0:WINDOW] = vp_ref[...]
    vcat[WINDOW:] = v_ref[...]

    pair_rows = (GQA_GROUP // 2) * WINDOW
    rq = lax.broadcasted_iota(jnp.int32, (pair_rows, 4 * WINDOW), 0) & (WINDOW - 1)
    ck = lax.broadcasted_iota(jnp.int32, (pair_rows, 4 * WINDOW), 1) & (2 * WINDOW - 1)
    band = (ck > rq) & (ck <= rq + WINDOW)
    lane_kv = lax.broadcasted_iota(jnp.int32, (2 * WINDOW, LANES), 1)
    lane_o = lax.broadcasted_iota(jnp.int32, (pair_rows, LANES), 1)
    row_p = lax.broadcasted_iota(jnp.int32, (pair_rows, 1), 0) >> 7

    def sub(j, carry):
        off = pl.multiple_of(j * WINDOW, WINDOW)
        rows = pl.ds(off, WINDOW)
        for g in range(GMLP_GROUPS):
            cols = slice(g * LANES, (g + 1) * LANES)
            s = jnp.dot(wsp_ref[g], vn_ref[rows, cols].astype(BF16), preferred_element_type=F32) + bsp_ref[g]
            mrg[rows, cols] = a_ref[rows, cols] * s
        kblk = kcat[pl.ds(off, 2 * WINDOW), :]
        vblk = vcat[pl.ds(off, 2 * WINDOW), :]
        kswp = pltpu.roll(kblk, HEAD_DIM, 1)
        vswp = pltpu.roll(vblk, HEAD_DIM, 1)
        kmin = jnp.where(jnp.logical_and(i == 0, j == 0), WINDOW, 0)
        allowed = band & (ck >= kmin)
        for kk in range(N_KV_HEADS):
            lo_src, hi_src = (kblk, kswp) if kk == 0 else (kswp, kblk)
            kbd = jnp.concatenate([jnp.where(lane_kv < HEAD_DIM, lo_src, 0.0),
                                   jnp.where(lane_kv >= HEAD_DIM, hi_src, 0.0)], axis=0).astype(BF16)
            lo_src, hi_src = (vblk, vswp) if kk == 0 else (vswp, vblk)
            vbd = jnp.concatenate([jnp.where(lane_kv < HEAD_DIM, lo_src, 0.0),
                                   jnp.where(lane_kv >= HEAD_DIM, hi_src, 0.0)], axis=0).astype(BF16)
            pair0 = kk * (GQA_GROUP // 2)
            qs = jnp.concatenate([q_ref[rows, (pair0 + p) * LANES:(pair0 + p + 1) * LANES]
                                  for p in range(GQA_GROUP // 2)], axis=0)
            lg = lax.dot_general(qs, kbd, (((1,), (1,)), ((), ())), preferred_element_type=F32)
            lg = jnp.where(allowed, lg, NEG_INF)
            h0 = kk * GQA_GROUP
            se = jnp.full((pair_rows, 1), sinks_ref[h0], F32)
            so = jnp.full((pair_rows, 1), sinks_ref[h0 + 1], F32)
            for p in range(1, GQA_GROUP // 2):
                se = jnp.where(row_p == p, sinks_ref[h0 + 2 * p], se)
                so = jnp.where(row_p == p, sinks_ref[h0 + 2 * p + 1], so)
            le, lo = lg[:, :2 * WINDOW], lg[:, 2 * WINDOW:]
            me = jnp.maximum(jnp.max(le, axis=1, keepdims=True), se)
            mo = jnp.maximum(jnp.max(lo, axis=1, keepdims=True), so)
            pe = jnp.exp(le - me)
            po = jnp.exp(lo - mo)
            de = jnp.sum(pe, axis=1, keepdims=True) + jnp.exp(se - me)
            do = jnp.sum(po, axis=1, keepdims=True) + jnp.exp(so - mo)
            pr = jnp.concatenate([pe, po], axis=1).astype(BF16)
            o = jnp.dot(pr, vbd, preferred_element_type=F32)
            o = o / jnp.where(lane_o < HEAD_DIM, de, do)
            for p in range(GQA_GROUP // 2):
                cols = slice((pair0 + p) * LANES, (pair0 + p + 1) * LANES)
                mrg[rows, cols] += sgb_ref[rows, cols] * o[p * WINDOW:(p + 1) * WINDOW]
        return carry

    lax.fori_loop(0, nsub, sub, 0)
    _finish_rows(mrg[...].astype(BF16), x_ref[...], wout_ref, g2_ref, wr_ref, br_ref, xmid_ref, xn2_ref, lg_ref)


def _mix(sinks, q, k, v, a, vn, sgb, x, wsp, bsp, wout, g2, wr, br):
    n = x.shape[0]
    tm = MIX_ROWS
    nsub = tm // WINDOW
    row = lambda w: pl.BlockSpec((tm, w), lambda i: (i, 0))
    prev = pl.BlockSpec((WINDOW, KV_WIDTH), lambda i: (jnp.maximum(i * nsub - 1, 0), 0))
    full = lambda arr: pl.BlockSpec(arr.shape, lambda i: (0,) * arr.ndim)
    smem = pl.BlockSpec(memory_space=pltpu.SMEM)
    return pl.pallas_call(
        _mix_body,
        grid=(n // tm,),
        in_specs=[smem, row(Q_WIDTH), row(KV_WIDTH), prev, row(KV_WIDTH), prev,
                  row(GMLP_WIDTH), row(GMLP_WIDTH), row(D_MODEL), row(D_MODEL),
                  full(wsp), full(bsp), full(wout), full(g2), full(wr), full(br)],
        out_specs=[row(D_MODEL), row(D_MODEL // 2), pl.BlockSpec((N_EXPERTS, tm), lambda i: (0, i))],
        out_shape=[jax.ShapeDtypeStruct((n, D_MODEL), F32),
                   jax.ShapeDtypeStruct((n, D_MODEL // 2), jnp.uint32),
                   jax.ShapeDtypeStruct((N_EXPERTS, n), F32)],
        scratch_shapes=[pltpu.VMEM((tm + WINDOW, KV_WIDTH), F32),
                        pltpu.VMEM((tm + WINDOW, KV_WIDTH), F32),
                        pltpu.VMEM((tm, D_MODEL), F32)],
        compiler_params=_params(("arbitrary",)),
        name="mix_prompt",
    )(sinks, q, k, k, v, v, a, vn, sgb, x, wsp, bsp, wout, g2, wr, br)


def _sample_attn_body(sink_ref, q_ref, k_ref, v_ref, o_ref):
    q = q_ref[...]
    k = k_ref[...]
    v = v_ref[...]
    nq, nk = q.shape[1], k.shape[1]
    heads = lambda t, kk: t[:, :, kk * HEAD_DIM:(kk + 1) * HEAD_DIM].astype(BF16)
    row = lax.broadcasted_iota(jnp.int32, (1, nq, nk), 1)
    first_kv = (row % N_HEADS) < GQA_GROUP
    lg = jnp.where(first_kv,
                   jnp.einsum("bqd,bkd->bqk", q, heads(k, 0), preferred_element_type=F32),
                   jnp.einsum("bqd,bkd->bqk", q, heads(k, 1), preferred_element_type=F32))
    t = row // N_HEADS
    j = lax.broadcasted_iota(jnp.int32, (1, nq, nk), 2)
    lg = jnp.where((j > t) & (j <= t + WINDOW), lg, NEG_INF)
    sink = sink_ref[...][None]
    m = jnp.maximum(jnp.max(lg, axis=2, keepdims=True), sink)
    p = jnp.exp(lg - m)
    den = jnp.sum(p, axis=2, keepdims=True) + jnp.exp(sink - m)
    pb = p.astype(BF16)
    row_o = lax.broadcasted_iota(jnp.int32, (1, nq, HEAD_DIM), 1)
    o = jnp.where((row_o % N_HEADS) < GQA_GROUP,
                  jnp.einsum("bqk,bkd->bqd", pb, heads(v, 0), preferred_element_type=F32),
                  jnp.einsum("bqk,bkd->bqd", pb, heads(v, 1), preferred_element_type=F32))
    o_ref[...] = o / den


def _sample_attn(sink_col, q3, k_all, v_all):
    nb = q3.shape[0]
    bb = 32
    blk = lambda a: pl.BlockSpec((bb,) + a.shape[1:], lambda b: (b, 0, 0))
    return pl.pallas_call(
        _sample_attn_body,
        grid=(nb // bb,),
        in_specs=[pl.BlockSpec(sink_col.shape, lambda b: (0, 0)), blk(q3), blk(k_all), blk(v_all)],
        out_specs=blk(q3),
        out_shape=jax.ShapeDtypeStruct(q3.shape, F32),
        compiler_params=_params(("arbitrary",)),
        name="attn_sample",
    )(sink_col, q3, k_all, v_all)


def _mix_sample_body(a_ref, vn_ref, sgb_ref, o_ref, x_ref, coef_ref, bias_ref,
                     wout_ref, g2_ref, wr_ref, br_ref, xmid_ref, xn2_ref, lg_ref):
    vn = vn_ref[...]
    n, width = vn.shape
    rows8 = lambda t: t.reshape(n // 8, 8, width)
    s = bias_ref[...][None] + coef_ref[0][None] * rows8(vn)
    for d in range(1, coef_ref.shape[0]):
        s = s + coef_ref[d][None] * rows8(pltpu.roll(vn, d, 0))
    merged = a_ref[...] * s.reshape(n, width) + sgb_ref[...] * o_ref[...]
    _finish_rows(merged.astype(BF16), x_ref[...], wout_ref, g2_ref, wr_ref, br_ref, xmid_ref, xn2_ref, lg_ref)


def _mix_sample(a, vn, sgb, o, x, coef, bias, wout, g2, wr, br):
    n = x.shape[0]
    args = (a, vn, sgb, o, x, coef, bias, wout, g2, wr, br)
    full = lambda arr: pl.BlockSpec(arr.shape, lambda i: (0,) * arr.ndim)
    return pl.pallas_call(
        _mix_sample_body,
        grid=(1,),
        in_specs=[full(arr) for arr in args],
        out_specs=[pl.BlockSpec((n, D_MODEL), lambda i: (0, 0)), pl.BlockSpec((n, D_MODEL // 2), lambda i: (0, 0)),
                   pl.BlockSpec((N_EXPERTS, n), lambda i: (0, 0))],
        out_shape=[jax.ShapeDtypeStruct((n, D_MODEL), F32),
                   jax.ShapeDtypeStruct((n, D_MODEL // 2), jnp.uint32),
                   jax.ShapeDtypeStruct((N_EXPERTS, n), F32)],
        compiler_params=_params(("arbitrary",)),
        name="mix_sample",
    )(*args)


def _rows8(rows, dtype):
    n = rows[0].shape[1]
    sub = lax.broadcasted_iota(jnp.int32, (8, n), 0)
    out = jnp.zeros((8, n), dtype)
    for kx, r in enumerate(rows):
        out = jnp.where(sub == kx, r.astype(dtype), out)
    return out


def _route_body(nblk_p, nblk, lgp_ref, lgs_ref, gate_ref, dest_ref, meta_ref, idx_s, rank_s, base):
    i = pl.program_id(0)

    @pl.when(i == 0)
    def _():
        base[...] = jnp.zeros_like(base)

    l = jnp.where(jnp.full(lgp_ref.shape, i, jnp.int32) < nblk_p, lgp_ref[...], lgs_ref[...])
    tb = l.shape[1]
    sub = lax.broadcasted_iota(jnp.int32, l.shape, 0).astype(F32)
    vals, idxs, sels = [], [], []
    for _ in range(TOP_K):
        m = jnp.max(l, axis=0, keepdims=True)
        ik = jnp.min(jnp.where(l == m, sub, float(N_EXPERTS)), axis=0, keepdims=True)
        sel = sub == ik
        l = jnp.where(sel, -jnp.inf, l)
        vals.append(m)
        idxs.append(ik)
        sels.append(sel)
    es = [jnp.exp(vk - vals[0]) for vk in vals]
    den = es[0] + es[1] + es[2] + es[3]
    onehot = jnp.zeros(l.shape, F32)
    for sel in sels:
        onehot = onehot + sel.astype(F32)
    earlier = (lax.broadcasted_iota(jnp.int32, (tb, tb), 0) < lax.broadcasted_iota(jnp.int32, (tb, tb), 1))
    before = jnp.dot(onehot.astype(BF16), earlier.astype(BF16), preferred_element_type=F32) + base[...]
    ranks = [jnp.sum(jnp.where(sel, before, 0.0), axis=0, keepdims=True) for sel in sels]
    base[...] += jnp.sum(onehot, axis=1, keepdims=True)
    idx_s[i] = _rows8(idxs, F32)
    rank_s[i] = _rows8(ranks, F32)
    gates = jnp.concatenate([_rows8([e / den for e in es], F32), jnp.zeros((LANES - 8, tb), F32)], axis=0)
    gate_ref[...] = gates.T

    @pl.when(i == nblk - 1)
    def _():
        cnt = base[...]
        padded = jnp.ceil(cnt / EXPERT_ROWS) * EXPERT_ROWS
        lower = (lax.broadcasted_iota(jnp.int32, (N_EXPERTS, N_EXPERTS), 1) <
                 lax.broadcasted_iota(jnp.int32, (N_EXPERTS, N_EXPERTS), 0)).astype(F32)
        pstart = jnp.dot(lower, jnp.broadcast_to(padded, (N_EXPERTS, LANES)), preferred_element_type=F32,
                         precision=lax.Precision.HIGHEST)[:, :1]
        lane = lax.broadcasted_iota(jnp.int32, (N_EXPERTS, LANES), 1)
        meta = jnp.where(lane == 0, pstart / EXPERT_ROWS,
                         jnp.where(lane == 1, padded / EXPERT_ROWS, jnp.where(lane == 2, cnt, 0.0)))
        meta_ref[...] = meta.astype(jnp.int32)
        sub_e = lax.broadcasted_iota(jnp.int32, (N_EXPERTS, tb), 0).astype(F32)
        for b in range(nblk):
            idx, rank = idx_s[b], rank_s[b]
            rows = [jnp.sum(jnp.where(sub_e == idx[kx:kx + 1], pstart, 0.0), axis=0, keepdims=True)
                    + rank[kx:kx + 1] for kx in range(TOP_K)]
            dest_ref[:, b * tb:(b + 1) * tb] = _rows8(rows, jnp.int32)


def _route(logits_p, logits_s):
    tb = ROUTE_ROWS
    nblk_p, nblk_s = logits_p.shape[1] // tb, logits_s.shape[1] // tb
    nblk = nblk_p + nblk_s
    n = nblk * tb
    assert n == logits_p.shape[1] + logits_s.shape[1]
    return pl.pallas_call(
        functools.partial(_route_body, nblk_p, nblk),
        grid=(nblk,),
        in_specs=[pl.BlockSpec((N_EXPERTS, tb), lambda i: (0, jnp.minimum(i, nblk_p - 1))),
                  pl.BlockSpec((N_EXPERTS, tb), lambda i: (0, jnp.maximum(i - nblk_p, 0)))],
        out_specs=[pl.BlockSpec((tb, LANES), lambda i: (i, 0)),
                   pl.BlockSpec((8, n), lambda i: (0, 0)),
                   pl.BlockSpec((N_EXPERTS, LANES), lambda i: (0, 0))],
        out_shape=[jax.ShapeDtypeStruct((n, LANES), F32),
                   jax.ShapeDtypeStruct((8, n), jnp.int32),
                   jax.ShapeDtypeStruct((N_EXPERTS, LANES), jnp.int32)],
        scratch_shapes=[pltpu.VMEM((nblk, 8, tb), F32), pltpu.VMEM((nblk, 8, tb), F32),
                        pltpu.VMEM((N_EXPERTS, 1), F32)],
        compiler_params=_params(("arbitrary",)),
        name="route",
    )(logits_p, logits_s)


def _sc_mesh():
    return plsc.VectorSubcoreMesh(core_axis_name="c", subcore_axis_name="s")


def _sc_worker():
    return lax.axis_index("s") * SC_CORES + lax.axis_index("c")


def _sc_dispatch(x_p, x_s, dest_t, n_slots):
    chunk = SC_ROWS
    n_p, n_s = x_p.shape[0], x_s.shape[0]
    per_w = n_p // (SC_WORKERS * chunk)
    ns_chunks = n_s // chunk
    assert per_w * SC_WORKERS * chunk == n_p and per_w % 2 == 0
    assert ns_chunks * chunk == n_s and ns_chunks <= SC_WORKERS
    d3 = dest_t.reshape(dest_t.shape[0], (n_p + n_s) // chunk, chunk)
    width, dtype = x_p.shape[1], x_p.dtype

    @functools.partial(
        pl.kernel, mesh=_sc_mesh(),
        out_type=jax.ShapeDtypeStruct((n_slots, width), dtype),
        scratch_types=[pltpu.VMEM((TOP_K, per_w, chunk), jnp.int32),
                       pltpu.VMEM((TOP_K, 1, chunk), jnp.int32),
                       pltpu.VMEM((2, chunk, width), dtype),
                       pltpu.SemaphoreType.DMA, pltpu.SemaphoreType.DMA],
        compiler_params=pltpu.CompilerParams(use_tc_tiling_on_sc=True),
        name="dispatch")
    def run(xp_hbm, xs_hbm, d_hbm, out_hbm, ip_v, is_v, rows_v, rsem, wsem):
        wid = _sc_worker()
        pltpu.sync_copy(d_hbm.at[pl.ds(0, TOP_K), pl.ds(wid * per_w, per_w)], ip_v)

        def read(j, slot):
            return pltpu.make_async_copy(xp_hbm.at[pl.ds((wid * per_w + j) * chunk, chunk)], rows_v.at[slot], rsem)

        def scatter(idx_v, j, slot):
            copies = [pltpu.async_copy(rows_v.at[slot], out_hbm.at[idx_v.at[kx, j]], wsem) for kx in range(TOP_K)]
            for cp in copies:
                cp.wait()

        read(0, 0).start()

        def body(h, carry):
            j = 2 * h
            read(j, 0).wait()
            read(j + 1, 1).start()
            scatter(ip_v, j, 0)
            read(j + 1, 1).wait()

            @pl.when(j + 2 < per_w)
            def _():
                read(j + 2, 0).start()

            scatter(ip_v, j + 1, 1)
            return carry

        lax.fori_loop(0, per_w // 2, body, 0)

        @pl.when(wid < ns_chunks)
        def _():
            pltpu.sync_copy(d_hbm.at[pl.ds(0, TOP_K), pl.ds(n_p // chunk + wid, 1)], is_v)
            pltpu.sync_copy(xs_hbm.at[pl.ds(wid * chunk, chunk)], rows_v.at[0])
            scatter(is_v, 0, 0)

    return run(x_p, x_s, d3)


def _sc_collect(y_sorted, dest_t, n_p, n_s):
    chunk = SC_ROWS
    per_choice = SC_WORKERS // TOP_K
    per_w = n_p // (per_choice * chunk)
    assert per_w * per_choice * chunk == n_p and per_w % 2 == 0
    assert n_s == per_choice * chunk
    d3 = dest_t.reshape(dest_t.shape[0], (n_p + n_s) // chunk, chunk)
    width, dtype = y_sorted.shape[1], y_sorted.dtype

    @functools.partial(
        pl.kernel, mesh=_sc_mesh(),
        out_type=[jax.ShapeDtypeStruct((TOP_K * n_p, width), dtype), jax.ShapeDtypeStruct((TOP_K * n_s, width), dtype)],
        scratch_types=[pltpu.VMEM((per_w, chunk), jnp.int32),
                       pltpu.VMEM((1, chunk), jnp.int32),
                       pltpu.VMEM((2, chunk, width), dtype),
                       pltpu.SemaphoreType.DMA, pltpu.SemaphoreType.DMA],
        compiler_params=pltpu.CompilerParams(use_tc_tiling_on_sc=True),
        name="collect")
    def run(y_hbm, d_hbm, op_hbm, os_hbm, ip_v, is_v, rows_v, gsem, wsem):
        wid = _sc_worker()
        choice = wid // per_choice
        part = wid % per_choice
        pltpu.sync_copy(d_hbm.at[choice, pl.ds(part * per_w, per_w)], ip_v)
        pltpu.sync_copy(d_hbm.at[choice, pl.ds(n_p // chunk + part, 1)], is_v)

        def gather(idx_v, j, slot):
            return pltpu.make_async_copy(y_hbm.at[idx_v.at[j]], rows_v.at[slot], gsem)

        def write(j, slot):
            return pltpu.make_async_copy(rows_v.at[slot], op_hbm.at[pl.ds((wid * per_w + j) * chunk, chunk)], wsem)

        gather(ip_v, 0, 0).start()

        def body(h, carry):
            j = 2 * h
            gather(ip_v, j, 0).wait()

            @pl.when(h > 0)
            def _():
                write(j - 1, 1).wait()

            gather(ip_v, j + 1, 1).start()
            write(j, 0).start()
            gather(ip_v, j + 1, 1).wait()
            write(j, 0).wait()

            @pl.when(j + 2 < per_w)
            def _():
                gather(ip_v, j + 2, 0).start()

            write(j + 1, 1).start()
            return carry

        lax.fori_loop(0, per_w // 2, body, 0)
        write(per_w - 1, 1).wait()

        gather(is_v, 0, 0).start()
        gather(is_v, 0, 0).wait()
        pltpu.sync_copy(rows_v.at[0], os_hbm.at[pl.ds(wid * chunk, chunk)])

    return run(y_sorted, d3)


def _expert_body(blk0_ref, nblk_ref, cnt_ref, wup_hbm, wdn_hbm, bup_ref, bdn_ref, x_hbm, y_hbm,
                 wup_f, wdn_f, wup_s, wdn_s, xbuf, obuf, w_sem, in_sem, out_sem):
    e = pl.program_id(0)
    nb = nblk_ref[e]
    blk0 = blk0_ref[e]
    cnt = cnt_ref[e]
    tm = EXPERT_ROWS
    pair = 2 * LANES
    wslot = e % 2
    up_rows = D_MODEL // W_PIECES
    dn_rows = D_MODEL // (W_PIECES // 2)

    def w_piece(hbm, buf, ex, slot, p, rows):
        start = p * rows if isinstance(p, int) else pl.multiple_of(p * rows, rows)
        r = pl.ds(start, rows)
        return pltpu.make_async_copy(hbm.at[ex, r], buf.at[slot, r], w_sem.at[slot])

    def w_start(ex, slot, p):
        w_piece(wup_hbm, wup_f, ex, slot, p, up_rows).start()
        if isinstance(p, int):
            if p < W_PIECES // 2:
                w_piece(wdn_hbm, wdn_f, ex, slot, p, dn_rows).start()
        else:
            @pl.when(p < W_PIECES // 2)
            def _():
                w_piece(wdn_hbm, wdn_f, ex, slot, p, dn_rows).start()

    def w_wait(ex, slot):
        for p in range(W_PIECES):
            w_piece(wup_hbm, wup_f, ex, slot, p, up_rows).wait()
        for p in range(W_PIECES // 2):
            w_piece(wdn_hbm, wdn_f, ex, slot, p, dn_rows).wait()

    @pl.when(e == 0)
    def _():
        for p in range(W_PIECES):
            w_start(0, 0, p)

    w_wait(e, wslot)
    wup_ref = wup_f.at[wslot]
    wdn_ref = wdn_f.at[wslot]
    more = e + 1 < N_EXPERTS

    def x_copy(i, slot):
        rows = pl.ds(pl.multiple_of((blk0 + i) * tm, tm), tm)
        return pltpu.make_async_copy(x_hbm.at[rows], xbuf.at[slot], in_sem.at[slot])

    def y_copy(i, slot):
        rows = pl.ds(pl.multiple_of((blk0 + i) * tm, tm), tm)
        return pltpu.make_async_copy(obuf.at[slot], y_hbm.at[rows], out_sem.at[slot])

    @pl.when(nb > 0)
    def _():
        x_copy(0, 0).start(priority=1)
        r = lax.broadcasted_iota(jnp.int32, (pair, pair), 0)
        c = lax.broadcasted_iota(jnp.int32, (pair, pair), 1)
        perm = (r == jnp.where(c < LANES, 2 * c, 2 * (c - LANES) + 1)).astype(BF16)
        for g in range(2 * D_MODEL // pair):
            cols = slice(g * pair, (g + 1) * pair)
            wup_s[g] = jnp.dot(wup_ref[:, cols].astype(BF16), perm, preferred_element_type=F32).astype(BF16)
        for g in range(D_MODEL // pair):
            wdn_s[g] = wdn_ref[:, g * pair:(g + 1) * pair].astype(BF16)

        def block(i, carry):
            slot = i % 2
            x_copy(i, slot).wait()

            @pl.when(i + 1 < nb)
            def _():
                x_copy(i + 1, 1 - slot).start(priority=1)

            @pl.when(i >= 2)
            def _():
                y_copy(i - 2, slot).wait()

            @pl.when(jnp.logical_and(more, i < W_PIECES))
            def _():
                w_start(e + 1, 1 - wslot, i)

            row = lax.broadcasted_iota(jnp.int32, (tm, 1), 0)
            x = jnp.where(row < cnt - i * tm, _unpack_bf16_pair(xbuf[slot]), 0.0).astype(BF16)
            acts = []
            for g in range(2 * D_MODEL // pair):
                cols = slice(g * pair, (g + 1) * pair)
                h = jnp.dot(x, wup_s[g], preferred_element_type=F32) + bup_ref[0, :, cols]
                glu = jnp.minimum(h[:, :LANES], SWIGLU_LIMIT)
                lin = jnp.clip(h[:, LANES:], -SWIGLU_LIMIT, SWIGLU_LIMIT)
                acts.append((glu * jax.nn.sigmoid(SWIGLU_ALPHA * glu) * (lin + 1.0)).astype(BF16))
            act = jnp.concatenate(acts, axis=1)
            half_groups = D_MODEL // pair // 2
            for g in range(half_groups):
                ys = []
                for gg in (g, g + half_groups):
                    cols = slice(gg * pair, (gg + 1) * pair)
                    ys.append(jnp.dot(act, wdn_s[gg], preferred_element_type=F32) + bdn_ref[0, :, cols])
                obuf[slot, :, g * pair:(g + 1) * pair] = _pack_bf16_pair(ys[0], ys[1])
            y_copy(i, slot).start(priority=1)
            return carry

        lax.fori_loop(0, nb, block, 0)

        @pl.when(nb >= 2)
        def _():
            y_copy(nb - 2, nb % 2).wait()

        y_copy(nb - 1, (nb - 1) % 2).wait()

    for p in range(W_PIECES):
        @pl.when(jnp.logical_and(more, p >= nb))
        def _():
            w_start(e + 1, 1 - wslot, p)


def _experts(blk0, nblk, cnt, x_sorted, w_up, w_down, b_up_grouped, b_down):
    tm = EXPERT_ROWS
    per_expert = lambda a: pl.BlockSpec((1,) + a.shape[1:], lambda e, b0, nb, ct: (e, 0, 0))
    hbm = pl.BlockSpec(memory_space=pl.ANY)
    grid_spec = pltpu.PrefetchScalarGridSpec(
        num_scalar_prefetch=3,
        grid=(N_EXPERTS,),
        in_specs=[hbm, hbm, per_expert(b_up_grouped), per_expert(b_down), hbm],
        out_specs=hbm,
        scratch_shapes=[pltpu.VMEM((2,) + w_up.shape[1:], F32), pltpu.VMEM((2,) + w_down.shape[1:], F32),
                        pltpu.VMEM((2 * D_MODEL // (2 * LANES), D_MODEL, 2 * LANES), BF16),
                        pltpu.VMEM((D_MODEL // (2 * LANES), D_MODEL, 2 * LANES), BF16),
                        pltpu.VMEM((2, tm, x_sorted.shape[1]), x_sorted.dtype),
                        pltpu.VMEM((2, tm, D_MODEL // 2), jnp.uint32),
                        pltpu.SemaphoreType.DMA((2,)), pltpu.SemaphoreType.DMA((2,)), pltpu.SemaphoreType.DMA((2,))],
    )
    return pl.pallas_call(
        _expert_body,
        grid_spec=grid_spec,
        out_shape=jax.ShapeDtypeStruct((x_sorted.shape[0], D_MODEL // 2), jnp.uint32),
        compiler_params=_params(("arbitrary",)),
        name="experts",
    )(blk0, nblk, cnt, w_up, w_down, b_up_grouped, b_down, x_sorted)


def _combine_body(gate_ref, xmid_ref, gfin_ref, y0_ref, y1_ref, y2_ref, y3_ref, out_ref):
    gate = gate_ref[...]
    moe = _unpack_bf16_pair(y0_ref[...]) * gate[:, 0:1]
    for kx, y_ref in enumerate((y1_ref, y2_ref, y3_ref), start=1):
        moe = moe + _unpack_bf16_pair(y_ref[...]) * gate[:, kx:kx + 1]
    out_ref[...] = _rms(xmid_ref[...] + moe, gfin_ref[...])


def _combine(gates, first_token, xmid, gfin, y_rows):
    n = xmid.shape[0]
    tt = COMBINE_ROWS
    nblk = n // tt
    blk0 = first_token // tt
    assert blk0 * tt == first_token
    choice = lambda kx: pl.BlockSpec((tt, y_rows.shape[1]), lambda i: (i + kx * nblk, 0))
    return pl.pallas_call(
        _combine_body,
        grid=(nblk,),
        in_specs=[pl.BlockSpec((tt, LANES), lambda i: (i + blk0, 0)),
                  pl.BlockSpec((tt, D_MODEL), lambda i: (i, 0)),
                  pl.BlockSpec((1, D_MODEL), lambda i: (0, 0))] + [choice(kx) for kx in range(TOP_K)],
        out_specs=pl.BlockSpec((tt, D_MODEL), lambda i: (i, 0)),
        out_shape=jax.ShapeDtypeStruct((n, D_MODEL), F32),
        compiler_params=_params(("arbitrary",)),
        name="combine",
    )(gates, xmid, gfin, y_rows, y_rows, y_rows, y_rows)


def kernel(x_prompt, x_sample, cache_k_win, cache_v_win, norm_attn_g, w_in, ln_v_g, ln_v_b, w_spatial, b_spatial,
           attn_sinks, w_out, norm_ffn_g, w_router, b_router, w_up, b_up, w_down, b_down, norm_final_g):
    bp, tp, _ = x_prompt.shape
    bs, ts, _ = x_sample.shape
    w_buf = cache_k_win.shape[2]
    assert bp == 1 and tp % MIX_ROWS == 0 and w_buf == WINDOW and (bs * ts) % PROJ_ROWS == 0 and 8 % ts == 0
    n_p, n_s = bp * tp, bs * ts
    row2 = lambda a: a.reshape(1, -1)

    w_in_bf = w_in[0].astype(BF16)
    w_out_bf = w_out[0].astype(BF16)
    tril = jnp.tril(jnp.ones((CHUNK, CHUNK), dtype=bool))
    wsp = jnp.where(tril[None], w_spatial[0], 0.0)
    wsp_bf = wsp.astype(BF16)
    bsp = jnp.broadcast_to(b_spatial[0][:, :, None], (GMLP_GROUPS, CHUNK, LANES))
    b_up_grouped = b_up[0].reshape(N_EXPERTS, -1, LANES, 2).transpose(0, 1, 3, 2).reshape(N_EXPERTS, 1, -1)
    bd = b_down[0][:, None, :]
    g1, g2, gfin = row2(norm_attn_g[0]), row2(norm_ffn_g[0]), row2(norm_final_g)
    lng, lnb = row2(ln_v_g[0]), row2(ln_v_b[0])
    wr_hi = w_router[0].astype(BF16)
    wr = jnp.concatenate([wr_hi, (w_router[0] - wr_hi.astype(F32)).astype(BF16)], axis=1)
    br = row2(b_router[0])
    sinks = attn_sinks[0]

    xp = x_prompt.reshape(n_p, D_MODEL)
    cs_p = _rotary_inputs(jnp.arange(tp, dtype=jnp.int32))
    q_p, k_p, v_p, a_p, vn_p, sgb_p = _proj(xp, g1, w_in_bf, cs_p, lng, lnb)
    xmid_p, xn2_p, lg_p = _mix(sinks, q_p, k_p, v_p, a_p, vn_p, sgb_p, xp, wsp_bf, bsp, w_out_bf, g2, wr, br)

    xs = x_sample.reshape(n_s, D_MODEL)
    pos_s = PAST_LEN + jnp.arange(ts, dtype=jnp.int32)
    cs_s = _rotary_inputs(jnp.tile(pos_s, bs))
    q_s, k_s, v_s, a_s, vn_s, sgb_s = _proj(xs, g1, w_in_bf, cs_s, lng, lnb)
    n_keys = w_buf + ts
    key_pad = jnp.zeros((bs, (-n_keys) % 8, KV_WIDTH), F32)
    with_new = lambda cache, new: jnp.concatenate(
        [cache[0].reshape(bs, w_buf, KV_WIDTH), new.reshape(bs, ts, KV_WIDTH), key_pad], axis=1)
    k_all = with_new(cache_k_win, k_s)
    v_all = with_new(cache_v_win, v_s)
    sink_col = jnp.tile(sinks, ts).reshape(ts * N_HEADS, 1)
    o_s = _sample_attn(sink_col, q_s.reshape(bs, ts * N_HEADS, HEAD_DIM), k_all, v_all).reshape(n_s, Q_WIDTH)
    t_idx = jnp.arange(ts)
    coef = jnp.stack([jnp.where((t_idx >= d)[None, :], wsp[:, t_idx, jnp.maximum(t_idx - d, 0)], 0.0)
                      for d in range(ts)])
    coef = jnp.repeat(coef.transpose(0, 2, 1), GMLP_WIDTH // GMLP_GROUPS, axis=2)
    coef = jnp.tile(coef, (1, 8 // ts, 1))
    bias = jnp.tile(jnp.repeat(b_spatial[0][:, :ts].T, GMLP_WIDTH // GMLP_GROUPS, axis=1), (8 // ts, 1))
    xmid_s, xn2_s, lg_s = _mix_sample(a_s, vn_s, sgb_s, o_s, xs, coef, bias, w_out_bf, g2, wr, br)

    n_tok = n_p + n_s
    gate_w, dest_t, meta = _route(lg_p, lg_s)
    tm = EXPERT_ROWS
    n_blocks = (n_tok * TOP_K) // tm + N_EXPERTS

    x_sorted = _sc_dispatch(xn2_p, xn2_s, dest_t, n_blocks * tm)
    y_sorted = _experts(meta[:, 0], meta[:, 1], meta[:, 2], x_sorted, w_up[0], w_down[0], b_up_grouped, bd)
    yrows_p, yrows_s = _sc_collect(y_sorted, dest_t, n_p, n_s)
    y_p = _combine(gate_w, 0, xmid_p, gfin, yrows_p)
    y_s = _combine(gate_w, n_p, xmid_s, gfin, yrows_s)

    k4 = lambda t: t.reshape(1, bp, -1, N_KV_HEADS, HEAD_DIM)
    return (y_p.reshape(bp, tp, D_MODEL),
            y_s.reshape(bs, ts, D_MODEL),
            k4(k_p[n_p - WINDOW:]),
            k4(v_p[n_p - WINDOW:]),
            vn_p[n_p - CHUNK:].reshape(1, bp, CHUNK, GMLP_WIDTH),
            k_all[:, ts:n_keys].reshape(1, bs, w_buf, N_KV_HEADS, HEAD_DIM),
            v_all[:, ts:n_keys].reshape(1, bs, w_buf, N_KV_HEADS, HEAD_DIM),
            vn_s.reshape(1, bs, ts, GMLP_WIDTH))
```

```python
import functools

import numpy as np
import jax
import jax.numpy as jnp
from jax import lax
from jax.experimental import pallas as pl
from jax.experimental.pallas import tpu as pltpu
from jax.experimental.pallas import tpu_sc as plsc

F32 = jnp.float32
BF16 = jnp.bfloat16

D_MODEL = 1024
HEAD_DIM = 64
N_HEADS = 16
GQA_GROUP = 8
N_KV_HEADS = 2
Q_WIDTH = 1024
KV_WIDTH = 128
WINDOW = 128
ROT_DIM = 16
ROPE_THETA = 500000.0
CHUNK = 128
GMLP_WIDTH = 1024
GMLP_GROUPS = 8
N_EXPERTS = 32
TOP_K = 4
SWIGLU_LIMIT = 7.0
SWIGLU_ALPHA = 1.702
RMS_EPS = 1e-5
LN_EPS = 1e-5
NEG_INF = -1e30
PAST_LEN = 16384

LANES = 128
VMEM_LIMIT = 56 * 1024 * 1024

PROJ_ROWS = 256
MIX_ROWS = 512
ROUTE_ROWS = 512
EXPERT_ROWS = 256
W_PIECES = 8
COMBINE_ROWS = 512

SC_CORES = 2
SC_WORKERS = 32
SC_ROWS = 64

_C_Q, _C_KV, _C_U, _C_VG, _C_GA, _C_GB, _C_END = 0, 1024, 1280, 2304, 3328, 4352, 5376


def _params(sem):
    return pltpu.CompilerParams(dimension_semantics=sem, vmem_limit_bytes=VMEM_LIMIT)


def _rms(x, g):
    return x * lax.rsqrt(jnp.mean(x * x, axis=-1, keepdims=True) + RMS_EPS) * g


def _pack_bf16_pair(lo, hi):
    lo_bits = lax.bitcast_convert_type(lo.astype(BF16).astype(F32), jnp.uint32)
    hi_bits = lax.bitcast_convert_type(hi.astype(BF16).astype(F32), jnp.uint32)
    return (lo_bits >> 16) | hi_bits


def _unpack_bf16_pair(words):
    lo = lax.bitcast_convert_type(words << 16, F32)
    hi = lax.bitcast_convert_type(words & jnp.uint32(0xFFFF0000), F32)
    return jnp.concatenate([lo, hi], axis=1)


def _proj_body(x_ref, g_ref, w_ref, cs_ref, rot_ref, lng_ref, lnb_ref,
               q_ref, k_ref, v_ref, a_ref, vn_ref, sgb_ref):
    h = _rms(x_ref[...], g_ref[...]).astype(BF16)
    tabs = lax.dot_general(cs_ref[...], rot_ref[...], (((0,), (0,)), ((), ())), preferred_element_type=F32)
    rc, rs1, rs2 = tabs[:, :LANES], tabs[:, LANES:2 * LANES], tabs[:, 2 * LANES:]

    def rot(z):
        return z * rc + pltpu.roll(z, LANES - ROT_DIM // 2, 1) * rs1 + pltpu.roll(z, ROT_DIM // 2, 1) * rs2

    def mm(lo, hi):
        return jnp.dot(h, w_ref[:, lo:hi], preferred_element_type=F32)

    zq = mm(_C_Q, _C_KV)
    for c in range(Q_WIDTH // LANES):
        sl = slice(c * LANES, (c + 1) * LANES)
        q_ref[:, sl] = (rot(zq[:, sl]) * (HEAD_DIM ** -0.5)).astype(BF16)
    zkv = mm(_C_KV, _C_U)
    k_ref[...] = rot(zkv[:, :KV_WIDTH])
    v_ref[...] = zkv[:, KV_WIDTH:]
    a_ref[...] = jax.nn.sigmoid(mm(_C_GA, _C_GB)) * jax.nn.gelu(mm(_C_U, _C_VG))
    zv = jax.nn.gelu(mm(_C_VG, _C_GA))
    zc = zv - jnp.mean(zv, axis=-1, keepdims=True)
    var = jnp.mean(zc * zc, axis=-1, keepdims=True)
    vn_ref[...] = zc * lax.rsqrt(var + LN_EPS) * lng_ref[...] + lnb_ref[...]
    sgb_ref[...] = jax.nn.sigmoid(mm(_C_GB, _C_END))


def _proj(x, norm_g, w_in_bf, cs, ln_g, ln_b):
    n = x.shape[0]
    tm = PROJ_ROWS
    row = lambda w: pl.BlockSpec((tm, w), lambda i: (i, 0))
    full = lambda a: pl.BlockSpec(a.shape, lambda i: (0,) * a.ndim)
    rot = jnp.asarray(np.tile(_ROT_EXPAND, (3, 1)), dtype=BF16)
    return pl.pallas_call(
        _proj_body,
        grid=(n // tm,),
        in_specs=[row(D_MODEL), full(norm_g), full(w_in_bf), pl.BlockSpec((cs.shape[0], tm), lambda i: (0, i)),
                  full(rot),
                  full(ln_g), full(ln_b)],
        out_specs=[row(Q_WIDTH), row(KV_WIDTH), row(KV_WIDTH), row(GMLP_WIDTH), row(GMLP_WIDTH), row(D_MODEL)],
        out_shape=[jax.ShapeDtypeStruct((n, Q_WIDTH), BF16),
                   jax.ShapeDtypeStruct((n, KV_WIDTH), F32),
                   jax.ShapeDtypeStruct((n, KV_WIDTH), F32),
                   jax.ShapeDtypeStruct((n, GMLP_WIDTH), F32),
                   jax.ShapeDtypeStruct((n, GMLP_WIDTH), F32),
                   jax.ShapeDtypeStruct((n, D_MODEL), F32)],
        compiler_params=_params(("arbitrary",)),
        name="proj",
    )(x, norm_g, w_in_bf, cs, rot, ln_g, ln_b)


_ROT_COLS = 32


def _rot_expand():
    half = ROT_DIM // 2
    m = np.zeros((_ROT_COLS, 3 * LANES), np.float32)
    for lane in range(LANES):
        d = lane % HEAD_DIM
        if d < ROT_DIM:
            m[d % half, lane] = 1.0
        else:
            m[2 * half, lane] = 1.0
        if d < half:
            m[half + d, LANES + lane] = -1.0
        elif d < ROT_DIM:
            m[half + d - half, 2 * LANES + lane] = 1.0
    return m


_ROT_EXPAND = _rot_expand()


def _rotary_inputs(pos):
    half = ROT_DIM // 2
    inv_freq = ROPE_THETA ** (-jnp.arange(half, dtype=F32) / half)
    ang = inv_freq[:, None] * pos.astype(F32)[None, :]
    n = pos.shape[0]
    cs = jnp.concatenate([jnp.cos(ang), jnp.sin(ang), jnp.ones((1, n), F32),
                          jnp.zeros((_ROT_COLS - 2 * half - 1, n), F32)], axis=0)
    hi = cs.astype(BF16)
    rest = cs - hi.astype(F32)
    mid = rest.astype(BF16)
    lo = (rest - mid.astype(F32)).astype(BF16)
    return jnp.concatenate([hi, mid, lo], axis=0)


def _finish_rows(merged_bf, x, wout_ref, g2_ref, wr_ref, br_ref, xmid_ref, xn2_ref, lg_ref):
    xm = x + jnp.dot(merged_bf, wout_ref[...], preferred_element_type=F32)
    xmid_ref[...] = xm
    xn = _rms(xm, g2_ref[...])
    x_hi = xn.astype(BF16)
    x_lo = (xn - x_hi.astype(F32)).astype(BF16)
    w_hl = wr_ref[...]
    p_hi = jnp.dot(x_hi, w_hl, preferred_element_type=F32)
    p_lo = jnp.dot(x_lo, w_hl[:, :N_EXPERTS], preferred_element_type=F32)
    lg = p_hi[:, :N_EXPERTS] + (p_hi[:, N_EXPERTS:] + p_lo) + br_ref[...]
    wide = jnp.concatenate([lg, jnp.zeros((lg.shape[0], LANES - N_EXPERTS), F32)], axis=1)
    lg_ref[...] = wide.T[:N_EXPERTS]
    xn2_ref[...] = _pack_bf16_pair(xn[:, :D_MODEL // 2], xn[:, D_MODEL // 2:])


def _mix_body(sinks_ref, q_ref, k_ref, kp_ref, v_ref, vp_ref, a_ref, vn_ref, sgb_ref, x_ref,
              wsp_ref, bsp_ref, wout_ref, g2_ref, wr_ref, br_ref,
              xmid_ref, xn2_ref, lg_ref, kcat, vcat, mrg):
    i = pl.program_id(0)
    nsub = MIX_ROWS // WINDOW
    kcat[0:WINDOW] = kp_ref[...]
    kcat[WINDOW:] = k_ref[...]
    vcat[0:WINDOW] = vp_ref[...]
    vcat[WINDOW:] = v_ref[...]

    pair_rows = (GQA_GROUP // 2) * WINDOW
    rq = lax.broadcasted_iota(jnp.int32, (pair_rows, 4 * WINDOW), 0) & (WINDOW - 1)
    ck = lax.broadcasted_iota(jnp.int32, (pair_rows, 4 * WINDOW), 1) & (2 * WINDOW - 1)
    band = (ck > rq) & (ck <= rq + WINDOW)
    lane_kv = lax.broadcasted_iota(jnp.int32, (2 * WINDOW, LANES), 1)
    lane_o = lax.broadcasted_iota(jnp.int32, (pair_rows, LANES), 1)
    row_p = lax.broadcasted_iota(jnp.int32, (pair_rows, 1), 0) >> 7
    sink_cols = []
    for kk in range(N_KV_HEADS):
        h0 = kk * GQA_GROUP
        se = jnp.full((pair_rows, 1), sinks_ref[h0], F32)
        so = jnp.full((pair_rows, 1), sinks_ref[h0 + 1], F32)
        for p in range(1, GQA_GROUP // 2):
            se = jnp.where(row_p == p, sinks_ref[h0 + 2 * p], se)
            so = jnp.where(row_p == p, sinks_ref[h0 + 2 * p + 1], so)
        sink_cols.append((se, so))

    def sub(j, carry):
        off = pl.multiple_of(j * WINDOW, WINDOW)
        rows = pl.ds(off, WINDOW)
        for g in range(GMLP_GROUPS):
            cols = slice(g * LANES, (g + 1) * LANES)
            s = jnp.dot(wsp_ref[g], vn_ref[rows, cols].astype(BF16), preferred_element_type=F32) + bsp_ref[g]
            mrg[rows, cols] = a_ref[rows, cols] * s
        kblk = kcat[pl.ds(off, 2 * WINDOW), :]
        vblk = vcat[pl.ds(off, 2 * WINDOW), :]
        kswp = pltpu.roll(kblk, HEAD_DIM, 1)
        vswp = pltpu.roll(vblk, HEAD_DIM, 1)
        kmin = jnp.where(jnp.logical_and(i == 0, j == 0), WINDOW, 0)
        allowed = band & (ck >= kmin)
        for kk in range(N_KV_HEADS):
            lo_src, hi_src = (kblk, kswp) if kk == 0 else (kswp, kblk)
            kbd = jnp.concatenate([jnp.where(lane_kv < HEAD_DIM, lo_src, 0.0),
                                   jnp.where(lane_kv >= HEAD_DIM, hi_src, 0.0)], axis=0).astype(BF16)
            lo_src, hi_src = (vblk, vswp) if kk == 0 else (vswp, vblk)
            vbd = jnp.concatenate([jnp.where(lane_kv < HEAD_DIM, lo_src, 0.0),
                                   jnp.where(lane_kv >= HEAD_DIM, hi_src, 0.0)], axis=0).astype(BF16)
            pair0 = kk * (GQA_GROUP // 2)
            qs = jnp.concatenate([q_ref[rows, (pair0 + p) * LANES:(pair0 + p + 1) * LANES]
                                  for p in range(GQA_GROUP // 2)], axis=0)
            lg = lax.dot_general(qs, kbd, (((1,), (1,)), ((), ())), preferred_element_type=F32)
            lg = jnp.where(allowed, lg, NEG_INF)
            se, so = sink_cols[kk]
            le, lo = lg[:, :2 * WINDOW], lg[:, 2 * WINDOW:]
            me = jnp.maximum(jnp.max(le, axis=1, keepdims=True), se)
            mo = jnp.maximum(jnp.max(lo, axis=1, keepdims=True), so)
            pe = jnp.exp(le - me)
            po = jnp.exp(lo - mo)
            de = jnp.sum(pe, axis=1, keepdims=True) + jnp.exp(se - me)
            do = jnp.sum(po, axis=1, keepdims=True) + jnp.exp(so - mo)
            pr = jnp.concatenate([pe, po], axis=1).astype(BF16)
            o = jnp.dot(pr, vbd, preferred_element_type=F32)
            o = o / jnp.where(lane_o < HEAD_DIM, de, do)
            for p in range(GQA_GROUP // 2):
                cols = slice((pair0 + p) * LANES, (pair0 + p + 1) * LANES)
                mrg[rows, cols] += sgb_ref[rows, cols] * o[p * WINDOW:(p + 1) * WINDOW]
        return carry

    lax.fori_loop(0, nsub, sub, 0)
    _finish_rows(mrg[...].astype(BF16), x_ref[...], wout_ref, g2_ref, wr_ref, br_ref, xmid_ref, xn2_ref, lg_ref)


def _mix(sinks, q, k, v, a, vn, sgb, x, wsp, bsp, wout, g2, wr, br):
    n = x.shape[0]
    tm = MIX_ROWS
    nsub = tm // WINDOW
    row = lambda w: pl.BlockSpec((tm, w), lambda i: (i, 0))
    prev = pl.BlockSpec((WINDOW, KV_WIDTH), lambda i: (jnp.maximum(i * nsub - 1, 0), 0))
    full = lambda arr: pl.BlockSpec(arr.shape, lambda i: (0,) * arr.ndim)
    smem = pl.BlockSpec(memory_space=pltpu.SMEM)
    return pl.pallas_call(
        _mix_body,
        grid=(n // tm,),
        in_specs=[smem, row(Q_WIDTH), row(KV_WIDTH), prev, row(KV_WIDTH), prev,
                  row(GMLP_WIDTH), row(GMLP_WIDTH), row(D_MODEL), row(D_MODEL),
                  full(wsp), full(bsp), full(wout), full(g2), full(wr), full(br)],
        out_specs=[row(D_MODEL), row(D_MODEL // 2), pl.BlockSpec((N_EXPERTS, tm), lambda i: (0, i))],
        out_shape=[jax.ShapeDtypeStruct((n, D_MODEL), F32),
                   jax.ShapeDtypeStruct((n, D_MODEL // 2), jnp.uint32),
                   jax.ShapeDtypeStruct((N_EXPERTS, n), F32)],
        scratch_shapes=[pltpu.VMEM((tm + WINDOW, KV_WIDTH), F32),
                        pltpu.VMEM((tm + WINDOW, KV_WIDTH), F32),
                        pltpu.VMEM((tm, D_MODEL), F32)],
        compiler_params=_params(("arbitrary",)),
        name="mix_prompt",
    )(sinks, q, k, k, v, v, a, vn, sgb, x, wsp, bsp, wout, g2, wr, br)


def _sample_attn_body(sink_ref, q_ref, k_ref, v_ref, o_ref):
    q = q_ref[...]
    k = k_ref[...]
    v = v_ref[...]
    nq, nk = q.shape[1], k.shape[1]
    heads = lambda t, kk: t[:, :, kk * HEAD_DIM:(kk + 1) * HEAD_DIM].astype(BF16)
    row = lax.broadcasted_iota(jnp.int32, (1, nq, nk), 1)
    first_kv = (row % N_HEADS) < GQA_GROUP
    lg = jnp.where(first_kv,
                   jnp.einsum("bqd,bkd->bqk", q, heads(k, 0), preferred_element_type=F32),
                   jnp.einsum("bqd,bkd->bqk", q, heads(k, 1), preferred_element_type=F32))
    t = row // N_HEADS
    j = lax.broadcasted_iota(jnp.int32, (1, nq, nk), 2)
    lg = jnp.where((j > t) & (j <= t + WINDOW), lg, NEG_INF)
    sink = sink_ref[...][None]
    m = jnp.maximum(jnp.max(lg, axis=2, keepdims=True), sink)
    p = jnp.exp(lg - m)
    den = jnp.sum(p, axis=2, keepdims=True) + jnp.exp(sink - m)
    pb = p.astype(BF16)
    row_o = lax.broadcasted_iota(jnp.int32, (1, nq, HEAD_DIM), 1)
    o = jnp.where((row_o % N_HEADS) < GQA_GROUP,
                  jnp.einsum("bqk,bkd->bqd", pb, heads(v, 0), preferred_element_type=F32),
                  jnp.einsum("bqk,bkd->bqd", pb, heads(v, 1), preferred_element_type=F32))
    o_ref[...] = o / den


def _sample_attn(sink_col, q3, k_all, v_all):
    nb = q3.shape[0]
    bb = 32
    blk = lambda a: pl.BlockSpec((bb,) + a.shape[1:], lambda b: (b, 0, 0))
    return pl.pallas_call(
        _sample_attn_body,
        grid=(nb // bb,),
        in_specs=[pl.BlockSpec(sink_col.shape, lambda b: (0, 0)), blk(q3), blk(k_all), blk(v_all)],
        out_specs=blk(q3),
        out_shape=jax.ShapeDtypeStruct(q3.shape, F32),
        compiler_params=_params(("arbitrary",)),
        name="attn_sample",
    )(sink_col, q3, k_all, v_all)


def _mix_sample_body(a_ref, vn_ref, sgb_ref, o_ref, x_ref, coef_ref, bias_ref,
                     wout_ref, g2_ref, wr_ref, br_ref, xmid_ref, xn2_ref, lg_ref):
    vn = vn_ref[...]
    n, width = vn.shape
    rows8 = lambda t: t.reshape(n // 8, 8, width)
    s = bias_ref[...][None] + coef_ref[0][None] * rows8(vn)
    for d in range(1, coef_ref.shape[0]):
        s = s + coef_ref[d][None] * rows8(pltpu.roll(vn, d, 0))
    merged = a_ref[...] * s.reshape(n, width) + sgb_ref[...] * o_ref[...]
    _finish_rows(merged.astype(BF16), x_ref[...], wout_ref, g2_ref, wr_ref, br_ref, xmid_ref, xn2_ref, lg_ref)


def _mix_sample(a, vn, sgb, o, x, coef, bias, wout, g2, wr, br):
    n = x.shape[0]
    args = (a, vn, sgb, o, x, coef, bias, wout, g2, wr, br)
    full = lambda arr: pl.BlockSpec(arr.shape, lambda i: (0,) * arr.ndim)
    return pl.pallas_call(
        _mix_sample_body,
        grid=(1,),
        in_specs=[full(arr) for arr in args],
        out_specs=[pl.BlockSpec((n, D_MODEL), lambda i: (0, 0)), pl.BlockSpec((n, D_MODEL // 2), lambda i: (0, 0)),
                   pl.BlockSpec((N_EXPERTS, n), lambda i: (0, 0))],
        out_shape=[jax.ShapeDtypeStruct((n, D_MODEL), F32),
                   jax.ShapeDtypeStruct((n, D_MODEL // 2), jnp.uint32),
                   jax.ShapeDtypeStruct((N_EXPERTS, n), F32)],
        compiler_params=_params(("arbitrary",)),
        name="mix_sample",
    )(*args)


def _rows8(rows, dtype):
    n = rows[0].shape[1]
    sub = lax.broadcasted_iota(jnp.int32, (8, n), 0)
    out = jnp.zeros((8, n), dtype)
    for kx, r in enumerate(rows):
        out = jnp.where(sub == kx, r.astype(dtype), out)
    return out


def _route_body(nblk_p, nblk, lgp_ref, lgs_ref, gate_ref, dest_ref, meta_ref, idx_s, rank_s, base):
    i = pl.program_id(0)

    @pl.when(i == 0)
    def _():
        base[...] = jnp.zeros_like(base)

    l = jnp.where(jnp.full(lgp_ref.shape, i, jnp.int32) < nblk_p, lgp_ref[...], lgs_ref[...])
    tb = l.shape[1]
    sub = lax.broadcasted_iota(jnp.int32, l.shape, 0).astype(F32)
    vals, idxs, sels = [], [], []
    for _ in range(TOP_K):
        m = jnp.max(l, axis=0, keepdims=True)
        ik = jnp.min(jnp.where(l == m, sub, float(N_EXPERTS)), axis=0, keepdims=True)
        sel = sub == ik
        l = jnp.where(sel, -jnp.inf, l)
        vals.append(m)
        idxs.append(ik)
        sels.append(sel)
    es = [jnp.exp(vk - vals[0]) for vk in vals]
    den = es[0] + es[1] + es[2] + es[3]
    onehot = jnp.zeros(l.shape, F32)
    for sel in sels:
        onehot = onehot + sel.astype(F32)
    earlier = (lax.broadcasted_iota(jnp.int32, (tb, tb), 0) < lax.broadcasted_iota(jnp.int32, (tb, tb), 1))
    before = jnp.dot(onehot.astype(BF16), earlier.astype(BF16), preferred_element_type=F32) + base[...]
    ranks = [jnp.sum(jnp.where(sel, before, 0.0), axis=0, keepdims=True) for sel in sels]
    base[...] += jnp.sum(onehot, axis=1, keepdims=True)
    idx_s[i] = _rows8(idxs, F32)
    rank_s[i] = _rows8(ranks, F32)
    gates = jnp.concatenate([_rows8([e / den for e in es], F32), jnp.zeros((LANES - 8, tb), F32)], axis=0)
    gate_ref[...] = gates.T

    @pl.when(i == nblk - 1)
    def _():
        cnt = base[...]
        padded = jnp.ceil(cnt / EXPERT_ROWS) * EXPERT_ROWS
        lower = (lax.broadcasted_iota(jnp.int32, (N_EXPERTS, N_EXPERTS), 1) <
                 lax.broadcasted_iota(jnp.int32, (N_EXPERTS, N_EXPERTS), 0)).astype(F32)
        pstart = jnp.dot(lower, jnp.broadcast_to(padded, (N_EXPERTS, LANES)), preferred_element_type=F32,
                         precision=lax.Precision.HIGHEST)[:, :1]
        lane = lax.broadcasted_iota(jnp.int32, (N_EXPERTS, LANES), 1)
        meta = jnp.where(lane == 0, pstart / EXPERT_ROWS,
                         jnp.where(lane == 1, padded / EXPERT_ROWS, jnp.where(lane == 2, cnt, 0.0)))
        meta_ref[...] = meta.astype(jnp.int32)
        sub_e = lax.broadcasted_iota(jnp.int32, (N_EXPERTS, tb), 0).astype(F32)
        for b in range(nblk):
            idx, rank = idx_s[b], rank_s[b]
            rows = [jnp.sum(jnp.where(sub_e == idx[kx:kx + 1], pstart, 0.0), axis=0, keepdims=True)
                    + rank[kx:kx + 1] for kx in range(TOP_K)]
            dest_ref[:, b * tb:(b + 1) * tb] = _rows8(rows, jnp.int32)


def _route(logits_p, logits_s):
    tb = ROUTE_ROWS
    nblk_p, nblk_s = logits_p.shape[1] // tb, logits_s.shape[1] // tb
    nblk = nblk_p + nblk_s
    n = nblk * tb
    assert n == logits_p.shape[1] + logits_s.shape[1]
    return pl.pallas_call(
        functools.partial(_route_body, nblk_p, nblk),
        grid=(nblk,),
        in_specs=[pl.BlockSpec((N_EXPERTS, tb), lambda i: (0, jnp.minimum(i, nblk_p - 1))),
                  pl.BlockSpec((N_EXPERTS, tb), lambda i: (0, jnp.maximum(i - nblk_p, 0)))],
        out_specs=[pl.BlockSpec((tb, LANES), lambda i: (i, 0)),
                   pl.BlockSpec((8, n), lambda i: (0, 0)),
                   pl.BlockSpec((N_EXPERTS, LANES), lambda i: (0, 0))],
        out_shape=[jax.ShapeDtypeStruct((n, LANES), F32),
                   jax.ShapeDtypeStruct((8, n), jnp.int32),
                   jax.ShapeDtypeStruct((N_EXPERTS, LANES), jnp.int32)],
        scratch_shapes=[pltpu.VMEM((nblk, 8, tb), F32), pltpu.VMEM((nblk, 8, tb), F32),
                        pltpu.VMEM((N_EXPERTS, 1), F32)],
        compiler_params=_params(("arbitrary",)),
        name="route",
    )(logits_p, logits_s)


def _sc_mesh():
    return plsc.VectorSubcoreMesh(core_axis_name="c", subcore_axis_name="s")


def _sc_worker():
    return lax.axis_index("s") * SC_CORES + lax.axis_index("c")


def _sc_dispatch(x_p, x_s, dest_t, n_slots):
    chunk = SC_ROWS
    n_p, n_s = x_p.shape[0], x_s.shape[0]
    per_w = n_p // (SC_WORKERS * chunk)
    ns_chunks = n_s // chunk
    assert per_w * SC_WORKERS * chunk == n_p and per_w % 2 == 0
    assert ns_chunks * chunk == n_s and ns_chunks <= SC_WORKERS
    d3 = dest_t.reshape(dest_t.shape[0], (n_p + n_s) // chunk, chunk)
    width, dtype = x_p.shape[1], x_p.dtype

    @functools.partial(
        pl.kernel, mesh=_sc_mesh(),
        out_type=jax.ShapeDtypeStruct((n_slots, width), dtype),
        scratch_types=[pltpu.VMEM((TOP_K, per_w, chunk), jnp.int32),
                       pltpu.VMEM((TOP_K, 1, chunk), jnp.int32),
                       pltpu.VMEM((2, chunk, width), dtype),
                       pltpu.SemaphoreType.DMA, pltpu.SemaphoreType.DMA],
        compiler_params=pltpu.CompilerParams(use_tc_tiling_on_sc=True),
        name="dispatch")
    def run(xp_hbm, xs_hbm, d_hbm, out_hbm, ip_v, is_v, rows_v, rsem, wsem):
        wid = _sc_worker()
        pltpu.sync_copy(d_hbm.at[pl.ds(0, TOP_K), pl.ds(wid * per_w, per_w)], ip_v)

        def read(j, slot):
            return pltpu.make_async_copy(xp_hbm.at[pl.ds((wid * per_w + j) * chunk, chunk)], rows_v.at[slot], rsem)

        def scatter(idx_v, j, slot):
            copies = [pltpu.async_copy(rows_v.at[slot], out_hbm.at[idx_v.at[kx, j]], wsem) for kx in range(TOP_K)]
            for cp in copies:
                cp.wait()

        read(0, 0).start()

        def body(h, carry):
            j = 2 * h
            read(j, 0).wait()
            read(j + 1, 1).start()
            scatter(ip_v, j, 0)
            read(j + 1, 1).wait()

            @pl.when(j + 2 < per_w)
            def _():
                read(j + 2, 0).start()

            scatter(ip_v, j + 1, 1)
            return carry

        lax.fori_loop(0, per_w // 2, body, 0)

        @pl.when(wid < ns_chunks)
        def _():
            pltpu.sync_copy(d_hbm.at[pl.ds(0, TOP_K), pl.ds(n_p // chunk + wid, 1)], is_v)
            pltpu.sync_copy(xs_hbm.at[pl.ds(wid * chunk, chunk)], rows_v.at[0])
            scatter(is_v, 0, 0)

    return run(x_p, x_s, d3)


def _sc_collect(y_sorted, dest_t, n_p, n_s):
    chunk = SC_ROWS
    per_choice = SC_WORKERS // TOP_K
    per_w = n_p // (per_choice * chunk)
    assert per_w * per_choice * chunk == n_p and per_w % 2 == 0
    assert n_s == per_choice * chunk
    d3 = dest_t.reshape(dest_t.shape[0], (n_p + n_s) // chunk, chunk)
    width, dtype = y_sorted.shape[1], y_sorted.dtype

    @functools.partial(
        pl.kernel, mesh=_sc_mesh(),
        out_type=[jax.ShapeDtypeStruct((TOP_K * n_p, width), dtype), jax.ShapeDtypeStruct((TOP_K * n_s, width), dtype)],
        scratch_types=[pltpu.VMEM((per_w, chunk), jnp.int32),
                       pltpu.VMEM((1, chunk), jnp.int32),
                       pltpu.VMEM((2, chunk, width), dtype),
                       pltpu.SemaphoreType.DMA, pltpu.SemaphoreType.DMA],
        compiler_params=pltpu.CompilerParams(use_tc_tiling_on_sc=True),
        name="collect")
    def run(y_hbm, d_hbm, op_hbm, os_hbm, ip_v, is_v, rows_v, gsem, wsem):
        wid = _sc_worker()
        choice = wid // per_choice
        part = wid % per_choice
        pltpu.sync_copy(d_hbm.at[choice, pl.ds(part * per_w, per_w)], ip_v)
        pltpu.sync_copy(d_hbm.at[choice, pl.ds(n_p // chunk + part, 1)], is_v)

        def gather(idx_v, j, slot):
            return pltpu.make_async_copy(y_hbm.at[idx_v.at[j]], rows_v.at[slot], gsem)

        def write(j, slot):
            return pltpu.make_async_copy(rows_v.at[slot], op_hbm.at[pl.ds((wid * per_w + j) * chunk, chunk)], wsem)

        gather(ip_v, 0, 0).start()

        def body(h, carry):
            j = 2 * h
            gather(ip_v, j, 0).wait()

            @pl.when(h > 0)
            def _():
                write(j - 1, 1).wait()

            gather(ip_v, j + 1, 1).start()
            write(j, 0).start()
            gather(ip_v, j + 1, 1).wait()
            write(j, 0).wait()

            @pl.when(j + 2 < per_w)
            def _():
                gather(ip_v, j + 2, 0).start()

            write(j + 1, 1).start()
            return carry

        lax.fori_loop(0, per_w // 2, body, 0)
        write(per_w - 1, 1).wait()

        gather(is_v, 0, 0).start()
        gather(is_v, 0, 0).wait()
        pltpu.sync_copy(rows_v.at[0], os_hbm.at[pl.ds(wid * chunk, chunk)])

    return run(y_sorted, d3)


def _expert_body(blk0_ref, nblk_ref, cnt_ref, wup_hbm, wdn_hbm, bup_ref, bdn_ref, x_hbm, y_hbm,
                 wup_f, wdn_f, wup_s, wdn_s, xbuf, obuf, w_sem, in_sem, out_sem):
    e = pl.program_id(0)
    nb = nblk_ref[e]
    blk0 = blk0_ref[e]
    cnt = cnt_ref[e]
    tm = EXPERT_ROWS
    pair = 2 * LANES
    wslot = e % 2
    up_rows = D_MODEL // W_PIECES
    dn_rows = D_MODEL // (W_PIECES // 2)

    def w_piece(hbm, buf, ex, slot, p, rows):
        start = p * rows if isinstance(p, int) else pl.multiple_of(p * rows, rows)
        r = pl.ds(start, rows)
        return pltpu.make_async_copy(hbm.at[ex, r], buf.at[slot, r], w_sem.at[slot])

    def w_start(ex, slot, p):
        w_piece(wup_hbm, wup_f, ex, slot, p, up_rows).start()
        if isinstance(p, int):
            if p < W_PIECES // 2:
                w_piece(wdn_hbm, wdn_f, ex, slot, p, dn_rows).start()
        else:
            @pl.when(p < W_PIECES // 2)
            def _():
                w_piece(wdn_hbm, wdn_f, ex, slot, p, dn_rows).start()

    def w_wait(ex, slot):
        for p in range(W_PIECES):
            w_piece(wup_hbm, wup_f, ex, slot, p, up_rows).wait()
        for p in range(W_PIECES // 2):
            w_piece(wdn_hbm, wdn_f, ex, slot, p, dn_rows).wait()

    @pl.when(e == 0)
    def _():
        for p in range(W_PIECES):
            w_start(0, 0, p)

    w_wait(e, wslot)
    wup_ref = wup_f.at[wslot]
    wdn_ref = wdn_f.at[wslot]
    more = e + 1 < N_EXPERTS

    def x_copy(i, slot):
        rows = pl.ds(pl.multiple_of((blk0 + i) * tm, tm), tm)
        return pltpu.make_async_copy(x_hbm.at[rows], xbuf.at[slot], in_sem.at[slot])

    def y_copy(i, slot):
        rows = pl.ds(pl.multiple_of((blk0 + i) * tm, tm), tm)
        return pltpu.make_async_copy(obuf.at[slot], y_hbm.at[rows], out_sem.at[slot])

    @pl.when(nb > 0)
    def _():
        x_copy(0, 0).start(priority=1)
        r = lax.broadcasted_iota(jnp.int32, (pair, pair), 0)
        c = lax.broadcasted_iota(jnp.int32, (pair, pair), 1)
        perm = (r == jnp.where(c < LANES, 2 * c, 2 * (c - LANES) + 1)).astype(BF16)
        for g in range(2 * D_MODEL // pair):
            cols = slice(g * pair, (g + 1) * pair)
            wup_s[g] = jnp.dot(wup_ref[:, cols].astype(BF16), perm, preferred_element_type=F32).astype(BF16)
        for g in range(D_MODEL // pair):
            wdn_s[g] = wdn_ref[:, g * pair:(g + 1) * pair].astype(BF16)

        def block(i, carry):
            slot = i % 2
            x_copy(i, slot).wait()

            @pl.when(i + 1 < nb)
            def _():
                x_copy(i + 1, 1 - slot).start(priority=1)

            @pl.when(i >= 2)
            def _():
                y_copy(i - 2, slot).wait()

            @pl.when(jnp.logical_and(more, i < W_PIECES))
            def _():
                w_start(e + 1, 1 - wslot, i)

            row = lax.broadcasted_iota(jnp.int32, (tm, 1), 0)
            x = jnp.where(row < cnt - i * tm, _unpack_bf16_pair(xbuf[slot]), 0.0).astype(BF16)
            acts = []
            for g in range(2 * D_MODEL // pair):
                cols = slice(g * pair, (g + 1) * pair)
                h = jnp.dot(x, wup_s[g], preferred_element_type=F32) + bup_ref[0, :, cols]
                glu = jnp.minimum(h[:, :LANES], SWIGLU_LIMIT)
                lin = jnp.clip(h[:, LANES:], -SWIGLU_LIMIT, SWIGLU_LIMIT)
                acts.append((glu * jax.nn.sigmoid(SWIGLU_ALPHA * glu) * (lin + 1.0)).astype(BF16))
            act = jnp.concatenate(acts, axis=1)
            half_groups = D_MODEL // pair // 2
            for g in range(half_groups):
                ys = []
                for gg in (g, g + half_groups):
                    cols = slice(gg * pair, (gg + 1) * pair)
                    ys.append(jnp.dot(act, wdn_s[gg], preferred_element_type=F32) + bdn_ref[0, :, cols])
                obuf[slot, :, g * pair:(g + 1) * pair] = _pack_bf16_pair(ys[0], ys[1])
            y_copy(i, slot).start(priority=1)
            return carry

        lax.fori_loop(0, nb, block, 0)

        @pl.when(nb >= 2)
        def _():
            y_copy(nb - 2, nb % 2).wait()

        y_copy(nb - 1, (nb - 1) % 2).wait()

    for p in range(W_PIECES):
        @pl.when(jnp.logical_and(more, p >= nb))
        def _():
            w_start(e + 1, 1 - wslot, p)


def _experts(blk0, nblk, cnt, x_sorted, w_up, w_down, b_up_grouped, b_down):
    tm = EXPERT_ROWS
    per_expert = lambda a: pl.BlockSpec((1,) + a.shape[1:], lambda e, b0, nb, ct: (e, 0, 0))
    hbm = pl.BlockSpec(memory_space=pl.ANY)
    grid_spec = pltpu.PrefetchScalarGridSpec(
        num_scalar_prefetch=3,
        grid=(N_EXPERTS,),
        in_specs=[hbm, hbm, per_expert(b_up_grouped), per_expert(b_down), hbm],
        out_specs=hbm,
        scratch_shapes=[pltpu.VMEM((2,) + w_up.shape[1:], F32), pltpu.VMEM((2,) + w_down.shape[1:], F32),
                        pltpu.VMEM((2 * D_MODEL // (2 * LANES), D_MODEL, 2 * LANES), BF16),
                        pltpu.VMEM((D_MODEL // (2 * LANES), D_MODEL, 2 * LANES), BF16),
                        pltpu.VMEM((2, tm, x_sorted.shape[1]), x_sorted.dtype),
                        pltpu.VMEM((2, tm, D_MODEL // 2), jnp.uint32),
                        pltpu.SemaphoreType.DMA((2,)), pltpu.SemaphoreType.DMA((2,)), pltpu.SemaphoreType.DMA((2,))],
    )
    return pl.pallas_call(
        _expert_body,
        grid_spec=grid_spec,
        out_shape=jax.ShapeDtypeStruct((x_sorted.shape[0], D_MODEL // 2), jnp.uint32),
        compiler_params=_params(("arbitrary",)),
        name="experts",
    )(blk0, nblk, cnt, w_up, w_down, b_up_grouped, b_down, x_sorted)


def _combine_body(gate_ref, xmid_ref, gfin_ref, y0_ref, y1_ref, y2_ref, y3_ref, out_ref):
    gate = gate_ref[...]
    moe = _unpack_bf16_pair(y0_ref[...]) * gate[:, 0:1]
    for kx, y_ref in enumerate((y1_ref, y2_ref, y3_ref), start=1):
        moe = moe + _unpack_bf16_pair(y_ref[...]) * gate[:, kx:kx + 1]
    out_ref[...] = _rms(xmid_ref[...] + moe, gfin_ref[...])


def _combine(gates, first_token, xmid, gfin, y_rows):
    n = xmid.shape[0]
    tt = COMBINE_ROWS
    nblk = n // tt
    blk0 = first_token // tt
    assert blk0 * tt == first_token
    choice = lambda kx: pl.BlockSpec((tt, y_rows.shape[1]), lambda i: (i + kx * nblk, 0))
    return pl.pallas_call(
        _combine_body,
        grid=(nblk,),
        in_specs=[pl.BlockSpec((tt, LANES), lambda i: (i + blk0, 0)),
                  pl.BlockSpec((tt, D_MODEL), lambda i: (i, 0)),
                  pl.BlockSpec((1, D_MODEL), lambda i: (0, 0))] + [choice(kx) for kx in range(TOP_K)],
        out_specs=pl.BlockSpec((tt, D_MODEL), lambda i: (i, 0)),
        out_shape=jax.ShapeDtypeStruct((n, D_MODEL), F32),
        compiler_params=_params(("arbitrary",)),
        name="combine",
    )(gates, xmid, gfin, y_rows, y_rows, y_rows, y_rows)


def kernel(x_prompt, x_sample, cache_k_win, cache_v_win, norm_attn_g, w_in, ln_v_g, ln_v_b, w_spatial, b_spatial,
           attn_sinks, w_out, norm_ffn_g, w_router, b_router, w_up, b_up, w_down, b_down, norm_final_g):
    bp, tp, _ = x_prompt.shape
    bs, ts, _ = x_sample.shape
    w_buf = cache_k_win.shape[2]
    assert bp == 1 and tp % MIX_ROWS == 0 and w_buf == WINDOW and (bs * ts) % PROJ_ROWS == 0 and 8 % ts == 0
    n_p, n_s = bp * tp, bs * ts
    row2 = lambda a: a.reshape(1, -1)

    w_in_bf = w_in[0].astype(BF16)
    w_out_bf = w_out[0].astype(BF16)
    tril = jnp.tril(jnp.ones((CHUNK, CHUNK), dtype=bool))
    wsp = jnp.where(tril[None], w_spatial[0], 0.0)
    wsp_bf = wsp.astype(BF16)
    bsp = jnp.broadcast_to(b_spatial[0][:, :, None], (GMLP_GROUPS, CHUNK, LANES))
    b_up_grouped = b_up[0].reshape(N_EXPERTS, -1, LANES, 2).transpose(0, 1, 3, 2).reshape(N_EXPERTS, 1, -1)
    bd = b_down[0][:, None, :]
    g1, g2, gfin = row2(norm_attn_g[0]), row2(norm_ffn_g[0]), row2(norm_final_g)
    lng, lnb = row2(ln_v_g[0]), row2(ln_v_b[0])
    wr_hi = w_router[0].astype(BF16)
    wr = jnp.concatenate([wr_hi, (w_router[0] - wr_hi.astype(F32)).astype(BF16)], axis=1)
    br = row2(b_router[0])
    sinks = attn_sinks[0]

    xp = x_prompt.reshape(n_p, D_MODEL)
    cs_p = _rotary_inputs(jnp.arange(tp, dtype=jnp.int32))
    q_p, k_p, v_p, a_p, vn_p, sgb_p = _proj(xp, g1, w_in_bf, cs_p, lng, lnb)
    xmid_p, xn2_p, lg_p = _mix(sinks, q_p, k_p, v_p, a_p, vn_p, sgb_p, xp, wsp_bf, bsp, w_out_bf, g2, wr, br)

    xs = x_sample.reshape(n_s, D_MODEL)
    pos_s = PAST_LEN + jnp.arange(ts, dtype=jnp.int32)
    cs_s = _rotary_inputs(jnp.tile(pos_s, bs))
    q_s, k_s, v_s, a_s, vn_s, sgb_s = _proj(xs, g1, w_in_bf, cs_s, lng, lnb)
    n_keys = w_buf + ts
    key_pad = jnp.zeros((bs, (-n_keys) % 8, KV_WIDTH), F32)
    with_new = lambda cache, new: jnp.concatenate(
        [cache[0].reshape(bs, w_buf, KV_WIDTH), new.reshape(bs, ts, KV_WIDTH), key_pad], axis=1)
    k_all = with_new(cache_k_win, k_s)
    v_all = with_new(cache_v_win, v_s)
    sink_col = jnp.tile(sinks, ts).reshape(ts * N_HEADS, 1)
    o_s = _sample_attn(sink_col, q_s.reshape(bs, ts * N_HEADS, HEAD_DIM), k_all, v_all).reshape(n_s, Q_WIDTH)
    t_idx = jnp.arange(ts)
    coef = jnp.stack([jnp.where((t_idx >= d)[None, :], wsp[:, t_idx, jnp.maximum(t_idx - d, 0)], 0.0)
                      for d in range(ts)])
    coef = jnp.repeat(coef.transpose(0, 2, 1), GMLP_WIDTH // GMLP_GROUPS, axis=2)
    coef = jnp.tile(coef, (1, 8 // ts, 1))
    bias = jnp.tile(jnp.repeat(b_spatial[0][:, :ts].T, GMLP_WIDTH // GMLP_GROUPS, axis=1), (8 // ts, 1))
    xmid_s, xn2_s, lg_s = _mix_sample(a_s, vn_s, sgb_s, o_s, xs, coef, bias, w_out_bf, g2, wr, br)

    n_tok = n_p + n_s
    gate_w, dest_t, meta = _route(lg_p, lg_s)
    tm = EXPERT_ROWS
    n_blocks = (n_tok * TOP_K) // tm + N_EXPERTS

    x_sorted = _sc_dispatch(xn2_p, xn2_s, dest_t, n_blocks * tm)
    y_sorted = _experts(meta[:, 0], meta[:, 1], meta[:, 2], x_sorted, w_up[0], w_down[0], b_up_grouped, bd)
    yrows_p, yrows_s = _sc_collect(y_sorted, dest_t, n_p, n_s)
    y_p = _combine(gate_w, 0, xmid_p, gfin, yrows_p)
    y_s = _combine(gate_w, n_p, xmid_s, gfin, yrows_s)

    k4 = lambda t: t.reshape(1, bp, -1, N_KV_HEADS, HEAD_DIM)
    return (y_p.reshape(bp, tp, D_MODEL),
            y_s.reshape(bs, ts, D_MODEL),
            k4(k_p[n_p - WINDOW:]),
            k4(v_p[n_p - WINDOW:]),
            vn_p[n_p - CHUNK:].reshape(1, bp, CHUNK, GMLP_WIDTH),
            k_all[:, ts:n_keys].reshape(1, bs, w_buf, N_KV_HEADS, HEAD_DIM),
            v_all[:, ts:n_keys].reshape(1, bs, w_buf, N_KV_HEADS, HEAD_DIM),
            vn_s.reshape(1, bs, ts, GMLP_WIDTH))
```

```python
import functools

import numpy as np
import jax
import jax.numpy as jnp
from jax import lax
from jax.experimental import pallas as pl
from jax.experimental.pallas import tpu as pltpu
from jax.experimental.pallas import tpu_sc as plsc

F32 = jnp.float32
BF16 = jnp.bfloat16

D_MODEL = 1024
HEAD_DIM = 64
N_HEADS = 16
GQA_GROUP = 8
N_KV_HEADS = 2
Q_WIDTH = 1024
KV_WIDTH = 128
WINDOW = 128
ROT_DIM = 16
ROPE_THETA = 500000.0
CHUNK = 128
GMLP_WIDTH = 1024
GMLP_GROUPS = 8
N_EXPERTS = 32
TOP_K = 4
SWIGLU_LIMIT = 7.0
SWIGLU_ALPHA = 1.702
RMS_EPS = 1e-5
LN_EPS = 1e-5
NEG_INF = -1e30
PAST_LEN = 16384

LANES = 128
VMEM_LIMIT = 56 * 1024 * 1024

PROJ_ROWS = 256
MIX_ROWS = 512
ROUTE_ROWS = 512
EXPERT_ROWS = 256
W_PIECES = 8
COMBINE_ROWS = 512

SC_CORES = 2
SC_WORKERS = 32
SC_ROWS = 64

_C_Q, _C_KV, _C_U, _C_VG, _C_GA, _C_GB, _C_END = 0, 1024, 1280, 2304, 3328, 4352, 5376


def _params(sem):
    return pltpu.CompilerParams(dimension_semantics=sem, vmem_limit_bytes=VMEM_LIMIT)


def _rms(x, g):
    return x * lax.rsqrt(jnp.mean(x * x, axis=-1, keepdims=True) + RMS_EPS) * g


def _pack_bf16_pair(lo, hi):
    lo_bits = lax.bitcast_convert_type(lo.astype(BF16).astype(F32), jnp.uint32)
    hi_bits = lax.bitcast_convert_type(hi.astype(BF16).astype(F32), jnp.uint32)
    return (lo_bits >> 16) | hi_bits


def _unpack_bf16_pair(words):
    lo = lax.bitcast_convert_type(words << 16, F32)
    hi = lax.bitcast_convert_type(words & jnp.uint32(0xFFFF0000), F32)
    return jnp.concatenate([lo, hi], axis=1)


def _proj_body(x_ref, g_ref, w_ref, cs_ref, rot_ref, lng_ref, lnb_ref,
               q_ref, k_ref, v_ref, a_ref, vn_ref, sgb_ref):
    h = _rms(x_ref[...], g_ref[...]).astype(BF16)
    tabs = lax.dot_general(cs_ref[...], rot_ref[...], (((0,), (0,)), ((), ())), preferred_element_type=F32)
    rc, rs1, rs2 = tabs[:, :LANES], tabs[:, LANES:2 * LANES], tabs[:, 2 * LANES:]

    def rot(z):
        return z * rc + pltpu.roll(z, LANES - ROT_DIM // 2, 1) * rs1 + pltpu.roll(z, ROT_DIM // 2, 1) * rs2

    def mm(lo, hi):
        return jnp.dot(h, w_ref[:, lo:hi], preferred_element_type=F32)

    zq = mm(_C_Q, _C_KV)
    for c in range(Q_WIDTH // LANES):
        sl = slice(c * LANES, (c + 1) * LANES)
        q_ref[:, sl] = (rot(zq[:, sl]) * (HEAD_DIM ** -0.5)).astype(BF16)
    zkv = mm(_C_KV, _C_U)
    k_ref[...] = rot(zkv[:, :KV_WIDTH])
    v_ref[...] = zkv[:, KV_WIDTH:]
    a_ref[...] = jax.nn.sigmoid(mm(_C_GA, _C_GB)) * jax.nn.gelu(mm(_C_U, _C_VG))
    zv = jax.nn.gelu(mm(_C_VG, _C_GA))
    zc = zv - jnp.mean(zv, axis=-1, keepdims=True)
    var = jnp.mean(zc * zc, axis=-1, keepdims=True)
    vn_ref[...] = zc * lax.rsqrt(var + LN_EPS) * lng_ref[...] + lnb_ref[...]
    sgb_ref[...] = jax.nn.sigmoid(mm(_C_GB, _C_END))


def _proj(x, norm_g, w_in_bf, cs, ln_g, ln_b):
    n = x.shape[0]
    tm = PROJ_ROWS
    row = lambda w: pl.BlockSpec((tm, w), lambda i: (i, 0))
    full = lambda a: pl.BlockSpec(a.shape, lambda i: (0,) * a.ndim)
    rot = jnp.asarray(np.tile(_ROT_EXPAND, (3, 1)), dtype=BF16)
    return pl.pallas_call(
        _proj_body,
        grid=(n // tm,),
        in_specs=[row(D_MODEL), full(norm_g), full(w_in_bf), pl.BlockSpec((cs.shape[0], tm), lambda i: (0, i)),
                  full(rot),
                  full(ln_g), full(ln_b)],
        out_specs=[row(Q_WIDTH), row(KV_WIDTH), row(KV_WIDTH), row(GMLP_WIDTH), row(GMLP_WIDTH), row(D_MODEL)],
        out_shape=[jax.ShapeDtypeStruct((n, Q_WIDTH), BF16),
                   jax.ShapeDtypeStruct((n, KV_WIDTH), F32),
                   jax.ShapeDtypeStruct((n, KV_WIDTH), F32),
                   jax.ShapeDtypeStruct((n, GMLP_WIDTH), F32),
                   jax.ShapeDtypeStruct((n, GMLP_WIDTH), F32),
                   jax.ShapeDtypeStruct((n, D_MODEL), F32)],
        compiler_params=_params(("arbitrary",)),
        name="proj",
    )(x, norm_g, w_in_bf, cs, rot, ln_g, ln_b)


_ROT_COLS = 32


def _rot_expand():
    half = ROT_DIM // 2
    m = np.zeros((_ROT_COLS, 3 * LANES), np.float32)
    for lane in range(LANES):
        d = lane % HEAD_DIM
        if d < ROT_DIM:
            m[d % half, lane] = 1.0
        else:
            m[2 * half, lane] = 1.0
        if d < half:
            m[half + d, LANES + lane] = -1.0
        elif d < ROT_DIM:
            m[half + d - half, 2 * LANES + lane] = 1.0
    return m


_ROT_EXPAND = _rot_expand()


def _rotary_inputs(pos):
    half = ROT_DIM // 2
    inv_freq = ROPE_THETA ** (-jnp.arange(half, dtype=F32) / half)
    ang = inv_freq[:, None] * pos.astype(F32)[None, :]
    n = pos.shape[0]
    cs = jnp.concatenate([jnp.cos(ang), jnp.sin(ang), jnp.ones((1, n), F32),
                          jnp.zeros((_ROT_COLS - 2 * half - 1, n), F32)], axis=0)
    hi = cs.astype(BF16)
    rest = cs - hi.astype(F32)
    mid = rest.astype(BF16)
    lo = (rest - mid.astype(F32)).astype(BF16)
    return jnp.concatenate([hi, mid, lo], axis=0)


def _finish_rows(merged_bf, x, wout_ref, g2_ref, wr_ref, br_ref, xmid_ref, xn2_ref, lg_ref):
    xm = x + jnp.dot(merged_bf, wout_ref[...], preferred_element_type=F32)
    xmid_ref[...] = xm
    xn = _rms(xm, g2_ref[...])
    x_hi = xn.astype(BF16)
    x_lo = (xn - x_hi.astype(F32)).astype(BF16)
    w_hl = wr_ref[...]
    p_hi = jnp.dot(x_hi, w_hl, preferred_element_type=F32)
    p_lo = jnp.dot(x_lo, w_hl[:, :N_EXPERTS], preferred_element_type=F32)
    lg = p_hi[:, :N_EXPERTS] + (p_hi[:, N_EXPERTS:] + p_lo) + br_ref[...]
    wide = jnp.concatenate([lg, jnp.zeros((lg.shape[0], LANES - N_EXPERTS), F32)], axis=1)
    lg_ref[...] = wide.T[:N_EXPERTS]
    xn2_ref[...] = _pack_bf16_pair(xn[:, :D_MODEL // 2], xn[:, D_MODEL // 2:])


def _mix_body(sinks_ref, q_ref, k_ref, kp_ref, v_ref, vp_ref, a_ref, vn_ref, sgb_ref, x_ref,
              wsp_ref, bsp_ref, wout_ref, g2_ref, wr_ref, br_ref,
              xmid_ref, xn2_ref, lg_ref, kcat, vcat, mrg):
    i = pl.program_id(0)
    nsub = MIX_ROWS // WINDOW
    kcat[0:WINDOW] = kp_ref[...]
    kcat[WINDOW:] = k_ref[...]
    vcat[0:WINDOW] = vp_ref[...]
    vcat[WINDOW:] = v_ref[...]

    pair_rows = (GQA_GROUP // 2) * WINDOW
    rq = lax.broadcasted_iota(jnp.int32, (pair_rows, 4 * WINDOW), 0) & (WINDOW - 1)
    ck = lax.broadcasted_iota(jnp.int32, (pair_rows, 4 * WINDOW), 1) & (2 * WINDOW - 1)
    band = (ck > rq) & (ck <= rq + WINDOW)
    lane_kv = lax.broadcasted_iota(jnp.int32, (2 * WINDOW, LANES), 1)
    lane_o = lax.broadcasted_iota(jnp.int32, (pair_rows, LANES), 1)
    row_p = lax.broadcasted_iota(jnp.int32, (pair_rows, 1), 0) >> 7
    sink_cols = []
    for kk in range(N_KV_HEADS):
        h0 = kk * GQA_GROUP
        se = jnp.full((pair_rows, 1), sinks_ref[h0], F32)
        so = jnp.full((pair_rows, 1), sinks_ref[h0 + 1], F32)
        for p in range(1, GQA_GROUP // 2):
            se = jnp.where(row_p == p, sinks_ref[h0 + 2 * p], se)
            so = jnp.where(row_p == p, sinks_ref[h0 + 2 * p + 1], so)
        sink_cols.append((se, so))

    def sub(j, carry):
        off = pl.multiple_of(j * WINDOW, WINDOW)
        rows = pl.ds(off, WINDOW)
        for g in range(GMLP_GROUPS):
            cols = slice(g * LANES, (g + 1) * LANES)
            s = jnp.dot(wsp_ref[g], vn_ref[rows, cols].astype(BF16), preferred_element_type=F32) + bsp_ref[g]
            mrg[rows, cols] = a_ref[rows, cols] * s
        kblk = kcat[pl.ds(off, 2 * WINDOW), :]
        vblk = vcat[pl.ds(off, 2 * WINDOW), :]
        kswp = pltpu.roll(kblk, HEAD_DIM, 1)
        vswp = pltpu.roll(vblk, HEAD_DIM, 1)
        kmin = jnp.where(jnp.logical_and(i == 0, j == 0), WINDOW, 0)
        allowed = band & (ck >= kmin)
        for kk in range(N_KV_HEADS):
            lo_src, hi_src = (kblk, kswp) if kk == 0 else (kswp, kblk)
            kbd = jnp.concatenate([jnp.where(lane_kv < HEAD_DIM, lo_src, 0.0),
                                   jnp.where(lane_kv >= HEAD_DIM, hi_src, 0.0)], axis=0).astype(BF16)
            lo_src, hi_src = (vblk, vswp) if kk == 0 else (vswp, vblk)
            vbd = jnp.concatenate([jnp.where(lane_kv < HEAD_DIM, lo_src, 0.0),
                                   jnp.where(lane_kv >= HEAD_DIM, hi_src, 0.0)], axis=0).astype(BF16)
            pair0 = kk * (GQA_GROUP // 2)
            qs = jnp.concatenate([q_ref[rows, (pair0 + p) * LANES:(pair0 + p + 1) * LANES]
                                  for p in range(GQA_GROUP // 2)], axis=0)
            lg = lax.dot_general(qs, kbd, (((1,), (1,)), ((), ())), preferred_element_type=F32)
            lg = jnp.where(allowed, lg, NEG_INF)
            se, so = sink_cols[kk]
            le, lo = lg[:, :2 * WINDOW], lg[:, 2 * WINDOW:]
            me = jnp.maximum(jnp.max(le, axis=1, keepdims=True), se)
            mo = jnp.maximum(jnp.max(lo, axis=1, keepdims=True), so)
            pe = jnp.exp(le - me)
            po = jnp.exp(lo - mo)
            de = jnp.sum(pe, axis=1, keepdims=True) + jnp.exp(se - me)
            do = jnp.sum(po, axis=1, keepdims=True) + jnp.exp(so - mo)
            pr = jnp.concatenate([pe, po], axis=1).astype(BF16)
            o = jnp.dot(pr, vbd, preferred_element_type=F32)
            o = o / jnp.where(lane_o < HEAD_DIM, de, do)
            for p in range(GQA_GROUP // 2):
                cols = slice((pair0 + p) * LANES, (pair0 + p + 1) * LANES)
                mrg[rows, cols] += sgb_ref[rows, cols] * o[p * WINDOW:(p + 1) * WINDOW]
        return carry

    lax.fori_loop(0, nsub, sub, 0)
    _finish_rows(mrg[...].astype(BF16), x_ref[...], wout_ref, g2_ref, wr_ref, br_ref, xmid_ref, xn2_ref, lg_ref)


def _mix(sinks, q, k, v, a, vn, sgb, x, wsp, bsp, wout, g2, wr, br):
    n = x.shape[0]
    tm = MIX_ROWS
    nsub = tm // WINDOW
    row = lambda w: pl.BlockSpec((tm, w), lambda i: (i, 0))
    prev = pl.BlockSpec((WINDOW, KV_WIDTH), lambda i: (jnp.maximum(i * nsub - 1, 0), 0))
    full = lambda arr: pl.BlockSpec(arr.shape, lambda i: (0,) * arr.ndim)
    smem = pl.BlockSpec(memory_space=pltpu.SMEM)
    return pl.pallas_call(
        _mix_body,
        grid=(n // tm,),
        in_specs=[smem, row(Q_WIDTH), row(KV_WIDTH), prev, row(KV_WIDTH), prev,
                  row(GMLP_WIDTH), row(GMLP_WIDTH), row(D_MODEL), row(D_MODEL),
                  full(wsp), full(bsp), full(wout), full(g2), full(wr), full(br)],
        out_specs=[row(D_MODEL), row(D_MODEL // 2), pl.BlockSpec((N_EXPERTS, tm), lambda i: (0, i))],
        out_shape=[jax.ShapeDtypeStruct((n, D_MODEL), F32),
                   jax.ShapeDtypeStruct((n, D_MODEL // 2), jnp.uint32),
                   jax.ShapeDtypeStruct((N_EXPERTS, n), F32)],
        scratch_shapes=[pltpu.VMEM((tm + WINDOW, KV_WIDTH), F32),
                        pltpu.VMEM((tm + WINDOW, KV_WIDTH), F32),
                        pltpu.VMEM((tm, D_MODEL), F32)],
        compiler_params=_params(("arbitrary",)),
        name="mix_prompt",
    )(sinks, q, k, k, v, v, a, vn, sgb, x, wsp, bsp, wout, g2, wr, br)


def _sample_attn_body(sink_ref, q_ref, k_ref, v_ref, o_ref):
    q = q_ref[...]
    k = k_ref[...]
    v = v_ref[...]
    nq, nk = q.shape[1], k.shape[1]
    heads = lambda t, kk: t[:, :, kk * HEAD_DIM:(kk + 1) * HEAD_DIM].astype(BF16)
    row = lax.broadcasted_iota(jnp.int32, (1, nq, nk), 1)
    first_kv = (row % N_HEADS) < GQA_GROUP
    lg = jnp.where(first_kv,
                   jnp.einsum("bqd,bkd->bqk", q, heads(k, 0), preferred_element_type=F32),
                   jnp.einsum("bqd,bkd->bqk", q, heads(k, 1), preferred_element_type=F32))
    t = row // N_HEADS
    j = lax.broadcasted_iota(jnp.int32, (1, nq, nk), 2)
    lg = jnp.where((j > t) & (j <= t + WINDOW), lg, NEG_INF)
    sink = sink_ref[...][None]
    m = jnp.maximum(jnp.max(lg, axis=2, keepdims=True), sink)
    p = jnp.exp(lg - m)
    den = jnp.sum(p, axis=2, keepdims=True) + jnp.exp(sink - m)
    pb = p.astype(BF16)
    row_o = lax.broadcasted_iota(jnp.int32, (1, nq, HEAD_DIM), 1)
    o = jnp.where((row_o % N_HEADS) < GQA_GROUP,
                  jnp.einsum("bqk,bkd->bqd", pb, heads(v, 0), preferred_element_type=F32),
                  jnp.einsum("bqk,bkd->bqd", pb, heads(v, 1), preferred_element_type=F32))
    o_ref[...] = o / den


def _sample_attn(sink_col, q3, k_all, v_all):
    nb = q3.shape[0]
    bb = 32
    blk = lambda a: pl.BlockSpec((bb,) + a.shape[1:], lambda b: (b, 0, 0))
    return pl.pallas_call(
        _sample_attn_body,
        grid=(nb // bb,),
        in_specs=[pl.BlockSpec(sink_col.shape, lambda b: (0, 0)), blk(q3), blk(k_all), blk(v_all)],
        out_specs=blk(q3),
        out_shape=jax.ShapeDtypeStruct(q3.shape, F32),
        compiler_params=_params(("arbitrary",)),
        name="attn_sample",
    )(sink_col, q3, k_all, v_all)


def _mix_sample_body(a_ref, vn_ref, sgb_ref, o_ref, x_ref, coef_ref, bias_ref,
                     wout_ref, g2_ref, wr_ref, br_ref, xmid_ref, xn2_ref, lg_ref):
    vn = vn_ref[...]
    n, width = vn.shape
    rows8 = lambda t: t.reshape(n // 8, 8, width)
    s = bias_ref[...][None] + coef_ref[0][None] * rows8(vn)
    for d in range(1, coef_ref.shape[0]):
        s = s + coef_ref[d][None] * rows8(pltpu.roll(vn, d, 0))
    merged = a_ref[...] * s.reshape(n, width) + sgb_ref[...] * o_ref[...]
    _finish_rows(merged.astype(BF16), x_ref[...], wout_ref, g2_ref, wr_ref, br_ref, xmid_ref, xn2_ref, lg_ref)


def _mix_sample(a, vn, sgb, o, x, coef, bias, wout, g2, wr, br):
    n = x.shape[0]
    args = (a, vn, sgb, o, x, coef, bias, wout, g2, wr, br)
    full = lambda arr: pl.BlockSpec(arr.shape, lambda i: (0,) * arr.ndim)
    return pl.pallas_call(
        _mix_sample_body,
        grid=(1,),
        in_specs=[full(arr) for arr in args],
        out_specs=[pl.BlockSpec((n, D_MODEL), lambda i: (0, 0)), pl.BlockSpec((n, D_MODEL // 2), lambda i: (0, 0)),
                   pl.BlockSpec((N_EXPERTS, n), lambda i: (0, 0))],
        out_shape=[jax.ShapeDtypeStruct((n, D_MODEL), F32),
                   jax.ShapeDtypeStruct((n, D_MODEL // 2), jnp.uint32),
                   jax.ShapeDtypeStruct((N_EXPERTS, n), F32)],
        compiler_params=_params(("arbitrary",)),
        name="mix_sample",
    )(*args)


def _rows8(rows, dtype):
    n = rows[0].shape[1]
    sub = lax.broadcasted_iota(jnp.int32, (8, n), 0)
    out = jnp.zeros((8, n), dtype)
    for kx, r in enumerate(rows):
        out = jnp.where(sub == kx, r.astype(dtype), out)
    return out


def _route_body(nblk_p, nblk, lgp_ref, lgs_ref, gate_ref, dest_ref, meta_ref, idx_s, rank_s, base):
    i = pl.program_id(0)

    @pl.when(i == 0)
    def _():
        base[...] = jnp.zeros_like(base)

    l = jnp.where(jnp.full(lgp_ref.shape, i, jnp.int32) < nblk_p, lgp_ref[...], lgs_ref[...])
    tb = l.shape[1]
    sub = lax.broadcasted_iota(jnp.int32, l.shape, 0).astype(F32)
    vals, idxs, sels = [], [], []
    for _ in range(TOP_K):
        m = jnp.max(l, axis=0, keepdims=True)
        ik = jnp.min(jnp.where(l == m, sub, float(N_EXPERTS)), axis=0, keepdims=True)
        sel = sub == ik
        l = jnp.where(sel, -jnp.inf, l)
        vals.append(m)
        idxs.append(ik)
        sels.append(sel)
    es = [jnp.exp(vk - vals[0]) for vk in vals]
    den = es[0] + es[1] + es[2] + es[3]
    onehot = jnp.zeros(l.shape, F32)
    for sel in sels:
        onehot = onehot + sel.astype(F32)
    earlier = (lax.broadcasted_iota(jnp.int32, (tb, tb), 0) < lax.broadcasted_iota(jnp.int32, (tb, tb), 1))
    before = jnp.dot(onehot.astype(BF16), earlier.astype(BF16), preferred_element_type=F32) + base[...]
    ranks = [jnp.sum(jnp.where(sel, before, 0.0), axis=0, keepdims=True) for sel in sels]
    base[...] += jnp.sum(onehot, axis=1, keepdims=True)
    idx_s[i] = _rows8(idxs, F32)
    rank_s[i] = _rows8(ranks, F32)
    gates = jnp.concatenate([_rows8([e / den for e in es], F32), jnp.zeros((LANES - 8, tb), F32)], axis=0)
    gate_ref[...] = gates.T

    @pl.when(i == nblk - 1)
    def _():
        cnt = base[...]
        padded = jnp.ceil(cnt / EXPERT_ROWS) * EXPERT_ROWS
        lower = (lax.broadcasted_iota(jnp.int32, (N_EXPERTS, N_EXPERTS), 1) <
                 lax.broadcasted_iota(jnp.int32, (N_EXPERTS, N_EXPERTS), 0)).astype(F32)
        pstart = jnp.dot(lower, jnp.broadcast_to(padded, (N_EXPERTS, LANES)), preferred_element_type=F32,
                         precision=lax.Precision.HIGHEST)[:, :1]
        lane = lax.broadcasted_iota(jnp.int32, (N_EXPERTS, LANES), 1)
        meta = jnp.where(lane == 0, pstart / EXPERT_ROWS,
                         jnp.where(lane == 1, padded / EXPERT_ROWS, jnp.where(lane == 2, cnt, 0.0)))
        meta_ref[...] = meta.astype(jnp.int32)
        sub_e = lax.broadcasted_iota(jnp.int32, (N_EXPERTS, tb), 0).astype(F32)
        for b in range(nblk):
            idx, rank = idx_s[b], rank_s[b]
            rows = [jnp.sum(jnp.where(sub_e == idx[kx:kx + 1], pstart, 0.0), axis=0, keepdims=True)
                    + rank[kx:kx + 1] for kx in range(TOP_K)]
            dest_ref[:, b * tb:(b + 1) * tb] = _rows8(rows, jnp.int32)


def _route(logits_p, logits_s):
    tb = ROUTE_ROWS
    nblk_p, nblk_s = logits_p.shape[1] // tb, logits_s.shape[1] // tb
    nblk = nblk_p + nblk_s
    n = nblk * tb
    assert n == logits_p.shape[1] + logits_s.shape[1]
    return pl.pallas_call(
        functools.partial(_route_body, nblk_p, nblk),
        grid=(nblk,),
        in_specs=[pl.BlockSpec((N_EXPERTS, tb), lambda i: (0, jnp.minimum(i, nblk_p - 1))),
                  pl.BlockSpec((N_EXPERTS, tb), lambda i: (0, jnp.maximum(i - nblk_p, 0)))],
        out_specs=[pl.BlockSpec((tb, LANES), lambda i: (i, 0)),
                   pl.BlockSpec((8, n), lambda i: (0, 0)),
                   pl.BlockSpec((N_EXPERTS, LANES), lambda i: (0, 0))],
        out_shape=[jax.ShapeDtypeStruct((n, LANES), F32),
                   jax.ShapeDtypeStruct((8, n), jnp.int32),
                   jax.ShapeDtypeStruct((N_EXPERTS, LANES), jnp.int32)],
        scratch_shapes=[pltpu.VMEM((nblk, 8, tb), F32), pltpu.VMEM((nblk, 8, tb), F32),
                        pltpu.VMEM((N_EXPERTS, 1), F32)],
        compiler_params=_params(("arbitrary",)),
        name="route",
    )(logits_p, logits_s)


def _sc_mesh():
    return plsc.VectorSubcoreMesh(core_axis_name="c", subcore_axis_name="s")


def _sc_worker():
    return lax.axis_index("s") * SC_CORES + lax.axis_index("c")


def _sc_dispatch(x_p, x_s, dest_t, n_slots):
    chunk = SC_ROWS
    n_p, n_s = x_p.shape[0], x_s.shape[0]
    per_w = n_p // (SC_WORKERS * chunk)
    ns_chunks = n_s // chunk
    assert per_w * SC_WORKERS * chunk == n_p and per_w % 2 == 0
    assert ns_chunks * chunk == n_s and ns_chunks <= SC_WORKERS
    d3 = dest_t.reshape(dest_t.shape[0], (n_p + n_s) // chunk, chunk)
    width, dtype = x_p.shape[1], x_p.dtype

    @functools.partial(
        pl.kernel, mesh=_sc_mesh(),
        out_type=jax.ShapeDtypeStruct((n_slots, width), dtype),
        scratch_types=[pltpu.VMEM((TOP_K, per_w, chunk), jnp.int32),
                       pltpu.VMEM((TOP_K, 1, chunk), jnp.int32),
                       pltpu.VMEM((2, chunk, width), dtype),
                       pltpu.SemaphoreType.DMA, pltpu.SemaphoreType.DMA],
        compiler_params=pltpu.CompilerParams(use_tc_tiling_on_sc=True),
        name="dispatch")
    def run(xp_hbm, xs_hbm, d_hbm, out_hbm, ip_v, is_v, rows_v, rsem, wsem):
        wid = _sc_worker()
        pltpu.sync_copy(d_hbm.at[pl.ds(0, TOP_K), pl.ds(wid * per_w, per_w)], ip_v)

        def read(j, slot):
            return pltpu.make_async_copy(xp_hbm.at[pl.ds((wid * per_w + j) * chunk, chunk)], rows_v.at[slot], rsem)

        def scatter(idx_v, j, slot):
            copies = [pltpu.async_copy(rows_v.at[slot], out_hbm.at[idx_v.at[kx, j]], wsem) for kx in range(TOP_K)]
            for cp in copies:
                cp.wait()

        read(0, 0).start()

        def body(h, carry):
            j = 2 * h
            read(j, 0).wait()
            read(j + 1, 1).start()
            scatter(ip_v, j, 0)
            read(j + 1, 1).wait()

            @pl.when(j + 2 < per_w)
            def _():
                read(j + 2, 0).start()

            scatter(ip_v, j + 1, 1)
            return carry

        lax.fori_loop(0, per_w // 2, body, 0)

        @pl.when(wid < ns_chunks)
        def _():
            pltpu.sync_copy(d_hbm.at[pl.ds(0, TOP_K), pl.ds(n_p // chunk + wid, 1)], is_v)
            pltpu.sync_copy(xs_hbm.at[pl.ds(wid * chunk, chunk)], rows_v.at[0])
            scatter(is_v, 0, 0)

    return run(x_p, x_s, d3)


def _sc_collect(y_sorted, dest_t, n_p, n_s):
    chunk = SC_ROWS
    per_choice = SC_WORKERS // TOP_K
    per_w = n_p // (per_choice * chunk)
    assert per_w * per_choice * chunk == n_p and per_w % 2 == 0
    assert n_s == per_choice * chunk
    d3 = dest_t.reshape(dest_t.shape[0], (n_p + n_s) // chunk, chunk)
    width, dtype = y_sorted.shape[1], y_sorted.dtype

    @functools.partial(
        pl.kernel, mesh=_sc_mesh(),
        out_type=[jax.ShapeDtypeStruct((TOP_K * n_p, width), dtype), jax.ShapeDtypeStruct((TOP_K * n_s, width), dtype)],
        scratch_types=[pltpu.VMEM((per_w, chunk), jnp.int32),
                       pltpu.VMEM((1, chunk), jnp.int32),
                       pltpu.VMEM((2, chunk, width), dtype),
                       pltpu.SemaphoreType.DMA, pltpu.SemaphoreType.DMA],
        compiler_params=pltpu.CompilerParams(use_tc_tiling_on_sc=True),
        name="collect")
    def run(y_hbm, d_hbm, op_hbm, os_hbm, ip_v, is_v, rows_v, gsem, wsem):
        wid = _sc_worker()
        choice = wid // per_choice
        part = wid % per_choice
        pltpu.sync_copy(d_hbm.at[choice, pl.ds(part * per_w, per_w)], ip_v)
        pltpu.sync_copy(d_hbm.at[choice, pl.ds(n_p // chunk + part, 1)], is_v)

        def gather(idx_v, j, slot):
            return pltpu.make_async_copy(y_hbm.at[idx_v.at[j]], rows_v.at[slot], gsem)

        def write(j, slot):
            return pltpu.make_async_copy(rows_v.at[slot], op_hbm.at[pl.ds((wid * per_w + j) * chunk, chunk)], wsem)

        gather(ip_v, 0, 0).start()

        def body(h, carry):
            j = 2 * h
            gather(ip_v, j, 0).wait()

            @pl.when(h > 0)
            def _():
                write(j - 1, 1).wait()

            gather(ip_v, j + 1, 1).start()
            write(j, 0).start()
            gather(ip_v, j + 1, 1).wait()
            write(j, 0).wait()

            @pl.when(j + 2 < per_w)
            def _():
                gather(ip_v, j + 2, 0).start()

            write(j + 1, 1).start()
            return carry

        lax.fori_loop(0, per_w // 2, body, 0)
        write(per_w - 1, 1).wait()

        gather(is_v, 0, 0).start()
        gather(is_v, 0, 0).wait()
        pltpu.sync_copy(rows_v.at[0], os_hbm.at[pl.ds(wid * chunk, chunk)])

    return run(y_sorted, d3)


def _expert_body(blk0_ref, nblk_ref, cnt_ref, wup_hbm, wdn_hbm, bup_ref, bdn_ref, x_hbm, y_hbm,
                 wup_f, wdn_f, wup_s, wdn_s, xbuf, obuf, w_sem, in_sem, out_sem):
    e = pl.program_id(0)
    nb = nblk_ref[e]
    blk0 = blk0_ref[e]
    cnt = cnt_ref[e]
    tm = EXPERT_ROWS
    pair = 2 * LANES
    wslot = e % 2
    up_rows = D_MODEL // W_PIECES
    dn_rows = D_MODEL // (W_PIECES // 2)

    def w_piece(hbm, buf, ex, slot, p, rows):
        start = p * rows if isinstance(p, int) else pl.multiple_of(p * rows, rows)
        r = pl.ds(start, rows)
        return pltpu.make_async_copy(hbm.at[ex, r], buf.at[slot, r], w_sem.at[slot])

    def w_start(ex, slot, p):
        w_piece(wup_hbm, wup_f, ex, slot, p, up_rows).start()
        if isinstance(p, int):
            if p < W_PIECES // 2:
                w_piece(wdn_hbm, wdn_f, ex, slot, p, dn_rows).start()
        else:
            @pl.when(p < W_PIECES // 2)
            def _():
                w_piece(wdn_hbm, wdn_f, ex, slot, p, dn_rows).start()

    def w_wait(ex, slot):
        for p in range(W_PIECES):
            w_piece(wup_hbm, wup_f, ex, slot, p, up_rows).wait()
        for p in range(W_PIECES // 2):
            w_piece(wdn_hbm, wdn_f, ex, slot, p, dn_rows).wait()

    @pl.when(e == 0)
    def _():
        for p in range(W_PIECES):
            w_start(0, 0, p)

    w_wait(e, wslot)
    wup_ref = wup_f.at[wslot]
    wdn_ref = wdn_f.at[wslot]
    more = e + 1 < N_EXPERTS

    def x_copy(i, slot):
        rows = pl.ds(pl.multiple_of((blk0 + i) * tm, tm), tm)
        return pltpu.make_async_copy(x_hbm.at[rows], xbuf.at[slot], in_sem.at[slot])

    def y_copy(i, slot):
        rows = pl.ds(pl.multiple_of((blk0 + i) * tm, tm), tm)
        return pltpu.make_async_copy(obuf.at[slot], y_hbm.at[rows], out_sem.at[slot])

    @pl.when(nb > 0)
    def _():
        x_copy(0, 0).start(priority=1)
        r = lax.broadcasted_iota(jnp.int32, (pair, pair), 0)
        c = lax.broadcasted_iota(jnp.int32, (pair, pair), 1)
        perm = (r == jnp.where(c < LANES, 2 * c, 2 * (c - LANES) + 1)).astype(BF16)
        for g in range(2 * D_MODEL // pair):
            cols = slice(g * pair, (g + 1) * pair)
            wup_s[g] = jnp.dot(wup_ref[:, cols].astype(BF16), perm, preferred_element_type=F32).astype(BF16)
        for g in range(D_MODEL // pair):
            wdn_s[g] = wdn_ref[:, g * pair:(g + 1) * pair].astype(BF16)

        def block(i, carry):
            slot = i % 2
            x_copy(i, slot).wait()

            @pl.when(i + 1 < nb)
            def _():
                x_copy(i + 1, 1 - slot).start(priority=1)

            @pl.when(i >= 2)
            def _():
                y_copy(i - 2, slot).wait()

            @pl.when(jnp.logical_and(more, i < W_PIECES))
            def _():
                w_start(e + 1, 1 - wslot, i)

            row = lax.broadcasted_iota(jnp.int32, (tm, 1), 0)
            x = jnp.where(row < cnt - i * tm, _unpack_bf16_pair(xbuf[slot]), 0.0).astype(BF16)
            acts = []
            for g in range(2 * D_MODEL // pair):
                cols = slice(g * pair, (g + 1) * pair)
                h = jnp.dot(x, wup_s[g], preferred_element_type=F32) + bup_ref[0, :, cols]
                glu = jnp.minimum(h[:, :LANES], SWIGLU_LIMIT)
                lin = jnp.clip(h[:, LANES:], -SWIGLU_LIMIT, SWIGLU_LIMIT)
                acts.append((glu * jax.nn.sigmoid(SWIGLU_ALPHA * glu) * (lin + 1.0)).astype(BF16))
            act = jnp.concatenate(acts, axis=1)
            half_groups = D_MODEL // pair // 2
            for g in range(half_groups):
                ys = []
                for gg in (g, g + half_groups):
                    cols = slice(gg * pair, (gg + 1) * pair)
                    ys.append(jnp.dot(act, wdn_s[gg], preferred_element_type=F32) + bdn_ref[0, :, cols])
                obuf[slot, :, g * pair:(g + 1) * pair] = _pack_bf16_pair(ys[0], ys[1])
            y_copy(i, slot).start(priority=1)
            return carry

        lax.fori_loop(0, nb, block, 0)

        @pl.when(nb >= 2)
        def _():
            y_copy(nb - 2, nb % 2).wait()

        y_copy(nb - 1, (nb - 1) % 2).wait()

    for p in range(W_PIECES):
        @pl.when(jnp.logical_and(more, p >= nb))
        def _():
            w_start(e + 1, 1 - wslot, p)


def _experts(blk0, nblk, cnt, x_sorted, w_up, w_down, b_up_grouped, b_down):
    tm = EXPERT_ROWS
    per_expert = lambda a: pl.BlockSpec((1,) + a.shape[1:], lambda e, b0, nb, ct: (e, 0, 0))
    hbm = pl.BlockSpec(memory_space=pl.ANY)
    grid_spec = pltpu.PrefetchScalarGridSpec(
        num_scalar_prefetch=3,
        grid=(N_EXPERTS,),
        in_specs=[hbm, hbm, per_expert(b_up_grouped), per_expert(b_down), hbm],
        out_specs=hbm,
        scratch_shapes=[pltpu.VMEM((2,) + w_up.shape[1:], F32), pltpu.VMEM((2,) + w_down.shape[1:], F32),
                        pltpu.VMEM((2 * D_MODEL // (2 * LANES), D_MODEL, 2 * LANES), BF16),
                        pltpu.VMEM((D_MODEL // (2 * LANES), D_MODEL, 2 * LANES), BF16),
                        pltpu.VMEM((2, tm, x_sorted.shape[1]), x_sorted.dtype),
                        pltpu.VMEM((2, tm, D_MODEL // 2), jnp.uint32),
                        pltpu.SemaphoreType.DMA((2,)), pltpu.SemaphoreType.DMA((2,)), pltpu.SemaphoreType.DMA((2,))],
    )
    return pl.pallas_call(
        _expert_body,
        grid_spec=grid_spec,
        out_shape=jax.ShapeDtypeStruct((x_sorted.shape[0], D_MODEL // 2), jnp.uint32),
        compiler_params=_params(("arbitrary",)),
        name="experts",
    )(blk0, nblk, cnt, w_up, w_down, b_up_grouped, b_down, x_sorted)


def _combine_body(gate_ref, xmid_ref, gfin_ref, y0_ref, y1_ref, y2_ref, y3_ref, out_ref):
    gate = gate_ref[...]
    moe = _unpack_bf16_pair(y0_ref[...]) * gate[:, 0:1]
    for kx, y_ref in enumerate((y1_ref, y2_ref, y3_ref), start=1):
        moe = moe + _unpack_bf16_pair(y_ref[...]) * gate[:, kx:kx + 1]
    out_ref[...] = _rms(xmid_ref[...] + moe, gfin_ref[...])


def _combine(gates, first_token, xmid, gfin, y_rows):
    n = xmid.shape[0]
    tt = COMBINE_ROWS
    nblk = n // tt
    blk0 = first_token // tt
    assert blk0 * tt == first_token
    choice = lambda kx: pl.BlockSpec((tt, y_rows.shape[1]), lambda i: (i + kx * nblk, 0))
    return pl.pallas_call(
        _combine_body,
        grid=(nblk,),
        in_specs=[pl.BlockSpec((tt, LANES), lambda i: (i + blk0, 0)),
                  pl.BlockSpec((tt, D_MODEL), lambda i: (i, 0)),
                  pl.BlockSpec((1, D_MODEL), lambda i: (0, 0))] + [choice(kx) for kx in range(TOP_K)],
        out_specs=pl.BlockSpec((tt, D_MODEL), lambda i: (i, 0)),
        out_shape=jax.ShapeDtypeStruct((n, D_MODEL), F32),
        compiler_params=_params(("arbitrary",)),
        name="combine",
    )(gates, xmid, gfin, y_rows, y_rows, y_rows, y_rows)


def kernel(x_prompt, x_sample, cache_k_win, cache_v_win, norm_attn_g, w_in, ln_v_g, ln_v_b, w_spatial, b_spatial,
           attn_sinks, w_out, norm_ffn_g, w_router, b_router, w_up, b_up, w_down, b_down, norm_final_g):
    bp, tp, _ = x_prompt.shape
    bs, ts, _ = x_sample.shape
    w_buf = cache_k_win.shape[2]
    assert bp == 1 and tp % MIX_ROWS == 0 and w_buf == WINDOW and (bs * ts) % PROJ_ROWS == 0 and 8 % ts == 0
    n_p, n_s = bp * tp, bs * ts
    row2 = lambda a: a.reshape(1, -1)

    w_in_bf = w_in[0].astype(BF16)
    w_out_bf = w_out[0].astype(BF16)
    tril = jnp.tril(jnp.ones((CHUNK, CHUNK), dtype=bool))
    wsp = jnp.where(tril[None], w_spatial[0], 0.0)
    wsp_bf = wsp.astype(BF16)
    bsp = jnp.broadcast_to(b_spatial[0][:, :, None], (GMLP_GROUPS, CHUNK, LANES))
    b_up_grouped = b_up[0].reshape(N_EXPERTS, -1, LANES, 2).transpose(0, 1, 3, 2).reshape(N_EXPERTS, 1, -1)
    bd = b_down[0][:, None, :]
    g1, g2, gfin = row2(norm_attn_g[0]), row2(norm_ffn_g[0]), row2(norm_final_g)
    lng, lnb = row2(ln_v_g[0]), row2(ln_v_b[0])
    wr_hi = w_router[0].astype(BF16)
    wr = jnp.concatenate([wr_hi, (w_router[0] - wr_hi.astype(F32)).astype(BF16)], axis=1)
    br = row2(b_router[0])
    sinks = attn_sinks[0]

    xp = x_prompt.reshape(n_p, D_MODEL)
    cs_p = _rotary_inputs(jnp.arange(tp, dtype=jnp.int32))
    q_p, k_p, v_p, a_p, vn_p, sgb_p = _proj(xp, g1, w_in_bf, cs_p, lng, lnb)
    xmid_p, xn2_p, lg_p = _mix(sinks, q_p, k_p, v_p, a_p, vn_p, sgb_p, xp, wsp_bf, bsp, w_out_bf, g2, wr, br)

    xs = x_sample.reshape(n_s, D_MODEL)
    pos_s = PAST_LEN + jnp.arange(ts, dtype=jnp.int32)
    cs_s = _rotary_inputs(jnp.tile(pos_s, bs))
    q_s, k_s, v_s, a_s, vn_s, sgb_s = _proj(xs, g1, w_in_bf, cs_s, lng, lnb)
    n_keys = w_buf + ts
    key_pad = jnp.zeros((bs, (-n_keys) % 8, KV_WIDTH), F32)
    with_new = lambda cache, new: jnp.concatenate(
        [cache[0].reshape(bs, w_buf, KV_WIDTH), new.reshape(bs, ts, KV_WIDTH), key_pad], axis=1)
    k_all = with_new(cache_k_win, k_s)
    v_all = with_new(cache_v_win, v_s)
    sink_col = jnp.tile(sinks, ts).reshape(ts * N_HEADS, 1)
    o_s = _sample_attn(sink_col, q_s.reshape(bs, ts * N_HEADS, HEAD_DIM), k_all, v_all).reshape(n_s, Q_WIDTH)
    t_idx = jnp.arange(ts)
    coef = jnp.stack([jnp.where((t_idx >= d)[None, :], wsp[:, t_idx, jnp.maximum(t_idx - d, 0)], 0.0)
                      for d in range(ts)])
    coef = jnp.repeat(coef.transpose(0, 2, 1), GMLP_WIDTH // GMLP_GROUPS, axis=2)
    coef = jnp.tile(coef, (1, 8 // ts, 1))
    bias = jnp.tile(jnp.repeat(b_spatial[0][:, :ts].T, GMLP_WIDTH // GMLP_GROUPS, axis=1), (8 // ts, 1))
    xmid_s, xn2_s, lg_s = _mix_sample(a_s, vn_s, sgb_s, o_s, xs, coef, bias, w_out_bf, g2, wr, br)

    n_tok = n_p + n_s
    gate_w, dest_t, meta = _route(lg_p, lg_s)
    tm = EXPERT_ROWS
    n_blocks = (n_tok * TOP_K) // tm + N_EXPERTS

    x_sorted = _sc_dispatch(xn2_p, xn2_s, dest_t, n_blocks * tm)
    k_win_s = k_all[:, ts:n_keys].reshape(1, bs, w_buf, N_KV_HEADS, HEAD_DIM)
    v_win_s = v_all[:, ts:n_keys].reshape(1, bs, w_buf, N_KV_HEADS, HEAD_DIM)
    x_sorted, k_win_s, v_win_s = lax.optimization_barrier((x_sorted, k_win_s, v_win_s))
    y_sorted = _experts(meta[:, 0], meta[:, 1], meta[:, 2], x_sorted, w_up[0], w_down[0], b_up_grouped, bd)
    yrows_p, yrows_s = _sc_collect(y_sorted, dest_t, n_p, n_s)
    y_p = _combine(gate_w, 0, xmid_p, gfin, yrows_p)
    y_s = _combine(gate_w, n_p, xmid_s, gfin, yrows_s)

    k4 = lambda t: t.reshape(1, bp, -1, N_KV_HEADS, HEAD_DIM)
    return (y_p.reshape(bp, tp, D_MODEL),
            y_s.reshape(bs, ts, D_MODEL),
            k4(k_p[n_p - WINDOW:]),
            k4(v_p[n_p - WINDOW:]),
            vn_p[n_p - CHUNK:].reshape(1, bp, CHUNK, GMLP_WIDTH),
            k_win_s,
            v_win_s,
            vn_s.reshape(1, bs, ts, GMLP_WIDTH))
```

```python
import functools

import numpy as np
import jax
import jax.numpy as jnp
from jax import lax
from jax.experimental import pallas as pl
from jax.experimental.pallas import tpu as pltpu
from jax.experimental.pallas import tpu_sc as plsc

F32 = jnp.float32
BF16 = jnp.bfloat16

D_MODEL = 1024
HEAD_DIM = 64
N_HEADS = 16
GQA_GROUP = 8
N_KV_HEADS = 2
Q_WIDTH = 1024
KV_WIDTH = 128
WINDOW = 128
ROT_DIM = 16
ROPE_THETA = 500000.0
CHUNK = 128
GMLP_WIDTH = 1024
GMLP_GROUPS = 8
N_EXPERTS = 32
TOP_K = 4
SWIGLU_LIMIT = 7.0
SWIGLU_ALPHA = 1.702
RMS_EPS = 1e-5
LN_EPS = 1e-5
NEG_INF = -1e30
PAST_LEN = 16384

LANES = 128
VMEM_LIMIT = 56 * 1024 * 1024

PROJ_ROWS = 256
MIX_ROWS = 512
ROUTE_ROWS = 512
EXPERT_ROWS = 256
W_PIECES = 8
COMBINE_ROWS = 512

SC_CORES = 2
SC_WORKERS = 32
SC_ROWS = 64

_C_Q, _C_KV, _C_U, _C_VG, _C_GA, _C_GB, _C_END = 0, 1024, 1280, 2304, 3328, 4352, 5376


def _params(sem):
    return pltpu.CompilerParams(dimension_semantics=sem, vmem_limit_bytes=VMEM_LIMIT)


def _rms(x, g):
    return x * lax.rsqrt(jnp.mean(x * x, axis=-1, keepdims=True) + RMS_EPS) * g


def _pack_bf16_pair(lo, hi):
    lo_bits = lax.bitcast_convert_type(lo.astype(BF16).astype(F32), jnp.uint32)
    hi_bits = lax.bitcast_convert_type(hi.astype(BF16).astype(F32), jnp.uint32)
    return (lo_bits >> 16) | hi_bits


def _unpack_bf16_pair(words):
    lo = lax.bitcast_convert_type(words << 16, F32)
    hi = lax.bitcast_convert_type(words & jnp.uint32(0xFFFF0000), F32)
    return jnp.concatenate([lo, hi], axis=1)


def _proj_body(x_ref, g_ref, w_ref, cs_ref, rot_ref, lng_ref, lnb_ref,
               q_ref, k_ref, v_ref, a_ref, vn_ref, sgb_ref):
    h = _rms(x_ref[...], g_ref[...]).astype(BF16)
    tabs = lax.dot_general(cs_ref[...], rot_ref[...], (((0,), (0,)), ((), ())), preferred_element_type=F32)
    rc, rs1, rs2 = tabs[:, :LANES], tabs[:, LANES:2 * LANES], tabs[:, 2 * LANES:]

    def rot(z):
        return z * rc + pltpu.roll(z, LANES - ROT_DIM // 2, 1) * rs1 + pltpu.roll(z, ROT_DIM // 2, 1) * rs2

    def mm(lo, hi):
        return jnp.dot(h, w_ref[:, lo:hi], preferred_element_type=F32)

    zq = mm(_C_Q, _C_KV)
    for c in range(Q_WIDTH // LANES):
        sl = slice(c * LANES, (c + 1) * LANES)
        q_ref[:, sl] = (rot(zq[:, sl]) * (HEAD_DIM ** -0.5)).astype(BF16)
    zkv = mm(_C_KV, _C_U)
    k_ref[...] = rot(zkv[:, :KV_WIDTH])
    v_ref[...] = zkv[:, KV_WIDTH:]
    a_ref[...] = jax.nn.sigmoid(mm(_C_GA, _C_GB)) * jax.nn.gelu(mm(_C_U, _C_VG))
    zv = jax.nn.gelu(mm(_C_VG, _C_GA))
    zc = zv - jnp.mean(zv, axis=-1, keepdims=True)
    var = jnp.mean(zc * zc, axis=-1, keepdims=True)
    vn_ref[...] = zc * lax.rsqrt(var + LN_EPS) * lng_ref[...] + lnb_ref[...]
    sgb_ref[...] = jax.nn.sigmoid(mm(_C_GB, _C_END))


def _proj(x, norm_g, w_in_bf, cs, ln_g, ln_b):
    n = x.shape[0]
    tm = PROJ_ROWS
    row = lambda w: pl.BlockSpec((tm, w), lambda i: (i, 0))
    full = lambda a: pl.BlockSpec(a.shape, lambda i: (0,) * a.ndim)
    rot = jnp.asarray(np.tile(_ROT_EXPAND, (3, 1)), dtype=BF16)
    return pl.pallas_call(
        _proj_body,
        grid=(n // tm,),
        in_specs=[row(D_MODEL), full(norm_g), full(w_in_bf), pl.BlockSpec((cs.shape[0], tm), lambda i: (0, i)),
                  full(rot),
                  full(ln_g), full(ln_b)],
        out_specs=[row(Q_WIDTH), row(KV_WIDTH), row(KV_WIDTH), row(GMLP_WIDTH), row(GMLP_WIDTH), row(D_MODEL)],
        out_shape=[jax.ShapeDtypeStruct((n, Q_WIDTH), BF16),
                   jax.ShapeDtypeStruct((n, KV_WIDTH), F32),
                   jax.ShapeDtypeStruct((n, KV_WIDTH), F32),
                   jax.ShapeDtypeStruct((n, GMLP_WIDTH), F32),
                   jax.ShapeDtypeStruct((n, GMLP_WIDTH), F32),
                   jax.ShapeDtypeStruct((n, D_MODEL), F32)],
        compiler_params=_params(("arbitrary",)),
        name="proj",
    )(x, norm_g, w_in_bf, cs, rot, ln_g, ln_b)


_ROT_COLS = 32


def _rot_expand():
    half = ROT_DIM // 2
    m = np.zeros((_ROT_COLS, 3 * LANES), np.float32)
    for lane in range(LANES):
        d = lane % HEAD_DIM
        if d < ROT_DIM:
            m[d % half, lane] = 1.0
        else:
            m[2 * half, lane] = 1.0
        if d < half:
            m[half + d, LANES + lane] = -1.0
        elif d < ROT_DIM:
            m[half + d - half, 2 * LANES + lane] = 1.0
    return m


_ROT_EXPAND = _rot_expand()


def _rotary_inputs(pos):
    half = ROT_DIM // 2
    inv_freq = ROPE_THETA ** (-jnp.arange(half, dtype=F32) / half)
    ang = inv_freq[:, None] * pos.astype(F32)[None, :]
    n = pos.shape[0]
    cs = jnp.concatenate([jnp.cos(ang), jnp.sin(ang), jnp.ones((1, n), F32),
                          jnp.zeros((_ROT_COLS - 2 * half - 1, n), F32)], axis=0)
    hi = cs.astype(BF16)
    rest = cs - hi.astype(F32)
    mid = rest.astype(BF16)
    lo = (rest - mid.astype(F32)).astype(BF16)
    return jnp.concatenate([hi, mid, lo], axis=0)


def _finish_rows(merged_bf, x, wout_ref, g2_ref, wr_ref, br_ref, xmid_ref, xn2_ref, lg_ref):
    xm = x + jnp.dot(merged_bf, wout_ref[...], preferred_element_type=F32)
    xmid_ref[...] = xm
    xn = _rms(xm, g2_ref[...])
    x_hi = xn.astype(BF16)
    x_lo = (xn - x_hi.astype(F32)).astype(BF16)
    w_hl = wr_ref[...]
    p_hi = jnp.dot(x_hi, w_hl, preferred_element_type=F32)
    p_lo = jnp.dot(x_lo, w_hl[:, :N_EXPERTS], preferred_element_type=F32)
    lg = p_hi[:, :N_EXPERTS] + (p_hi[:, N_EXPERTS:] + p_lo) + br_ref[...]
    wide = jnp.concatenate([lg, jnp.zeros((lg.shape[0], LANES - N_EXPERTS), F32)], axis=1)
    lg_ref[...] = wide.T[:N_EXPERTS]
    xn2_ref[...] = _pack_bf16_pair(xn[:, :D_MODEL // 2], xn[:, D_MODEL // 2:])


def _mix_body(sinks_ref, q_ref, k_ref, kp_ref, v_ref, vp_ref, a_ref, vn_ref, sgb_ref, x_ref,
              wsp_ref, bsp_ref, wout_ref, g2_ref, wr_ref, br_ref,
              xmid_ref, xn2_ref, lg_ref, kcat, vcat, mrg, key_s):
    i = pl.program_id(0)
    nsub = MIX_ROWS // WINDOW
    kcat[0:WINDOW] = kp_ref[...]
    kcat[WINDOW:] = k_ref[...]
    vcat[0:WINDOW] = vp_ref[...]
    vcat[WINDOW:] = v_ref[...]

    pair_rows = (GQA_GROUP // 2) * WINDOW
    lane_kv = lax.broadcasted_iota(jnp.int32, (2 * WINDOW, LANES), 1)
    lane_o = lax.broadcasted_iota(jnp.int32, (pair_rows, LANES), 1)

    @pl.when(i == 0)
    def _():
        rq = lax.broadcasted_iota(jnp.int32, (pair_rows, 4 * WINDOW), 0) & (WINDOW - 1)
        ck = lax.broadcasted_iota(jnp.int32, (pair_rows, 4 * WINDOW), 1) & (2 * WINDOW - 1)
        key_s[...] = jnp.where((ck > rq) & (ck <= rq + WINDOW), ck, -1)

    row_p = lax.broadcasted_iota(jnp.int32, (pair_rows, 1), 0) >> 7
    sink_cols = []
    for kk in range(N_KV_HEADS):
        h0 = kk * GQA_GROUP
        se = jnp.full((pair_rows, 1), sinks_ref[h0], F32)
        so = jnp.full((pair_rows, 1), sinks_ref[h0 + 1], F32)
        for p in range(1, GQA_GROUP // 2):
            se = jnp.where(row_p == p, sinks_ref[h0 + 2 * p], se)
            so = jnp.where(row_p == p, sinks_ref[h0 + 2 * p + 1], so)
        sink_cols.append((se, so))

    def sub(j, carry):
        off = pl.multiple_of(j * WINDOW, WINDOW)
        rows = pl.ds(off, WINDOW)
        for g in range(GMLP_GROUPS):
            cols = slice(g * LANES, (g + 1) * LANES)
            s = jnp.dot(wsp_ref[g], vn_ref[rows, cols].astype(BF16), preferred_element_type=F32) + bsp_ref[g]
            mrg[rows, cols] = a_ref[rows, cols] * s
        kblk = kcat[pl.ds(off, 2 * WINDOW), :]
        vblk = vcat[pl.ds(off, 2 * WINDOW), :]
        kswp = pltpu.roll(kblk, HEAD_DIM, 1)
        vswp = pltpu.roll(vblk, HEAD_DIM, 1)
        kmin = jnp.where(jnp.logical_and(i == 0, j == 0), WINDOW, 0)
        allowed = key_s[...] >= kmin
        for kk in range(N_KV_HEADS):
            lo_src, hi_src = (kblk, kswp) if kk == 0 else (kswp, kblk)
            kbd = jnp.concatenate([jnp.where(lane_kv < HEAD_DIM, lo_src, 0.0),
                                   jnp.where(lane_kv >= HEAD_DIM, hi_src, 0.0)], axis=0).astype(BF16)
            lo_src, hi_src = (vblk, vswp) if kk == 0 else (vswp, vblk)
            vbd = jnp.concatenate([jnp.where(lane_kv < HEAD_DIM, lo_src, 0.0),
                                   jnp.where(lane_kv >= HEAD_DIM, hi_src, 0.0)], axis=0).astype(BF16)
            pair0 = kk * (GQA_GROUP // 2)
            qs = jnp.concatenate([q_ref[rows, (pair0 + p) * LANES:(pair0 + p + 1) * LANES]
                                  for p in range(GQA_GROUP // 2)], axis=0)
            lg = lax.dot_general(qs, kbd, (((1,), (1,)), ((), ())), preferred_element_type=F32)
            lg = jnp.where(allowed, lg, NEG_INF)
            se, so = sink_cols[kk]
            le, lo = lg[:, :2 * WINDOW], lg[:, 2 * WINDOW:]
            me = jnp.maximum(jnp.max(le, axis=1, keepdims=True), se)
            mo = jnp.maximum(jnp.max(lo, axis=1, keepdims=True), so)
            pe = jnp.exp(le - me)
            po = jnp.exp(lo - mo)
            de = jnp.sum(pe, axis=1, keepdims=True) + jnp.exp(se - me)
            do = jnp.sum(po, axis=1, keepdims=True) + jnp.exp(so - mo)
            pr = jnp.concatenate([pe, po], axis=1).astype(BF16)
            o = jnp.dot(pr, vbd, preferred_element_type=F32)
            o = o / jnp.where(lane_o < HEAD_DIM, de, do)
            for p in range(GQA_GROUP // 2):
                cols = slice((pair0 + p) * LANES, (pair0 + p + 1) * LANES)
                mrg[rows, cols] += sgb_ref[rows, cols] * o[p * WINDOW:(p + 1) * WINDOW]
        return carry

    lax.fori_loop(0, nsub, sub, 0)
    _finish_rows(mrg[...].astype(BF16), x_ref[...], wout_ref, g2_ref, wr_ref, br_ref, xmid_ref, xn2_ref, lg_ref)


def _mix(sinks, q, k, v, a, vn, sgb, x, wsp, bsp, wout, g2, wr, br):
    n = x.shape[0]
    tm = MIX_ROWS
    nsub = tm // WINDOW
    row = lambda w: pl.BlockSpec((tm, w), lambda i: (i, 0))
    prev = pl.BlockSpec((WINDOW, KV_WIDTH), lambda i: (jnp.maximum(i * nsub - 1, 0), 0))
    full = lambda arr: pl.BlockSpec(arr.shape, lambda i: (0,) * arr.ndim)
    smem = pl.BlockSpec(memory_space=pltpu.SMEM)
    return pl.pallas_call(
        _mix_body,
        grid=(n // tm,),
        in_specs=[smem, row(Q_WIDTH), row(KV_WIDTH), prev, row(KV_WIDTH), prev,
                  row(GMLP_WIDTH), row(GMLP_WIDTH), row(D_MODEL), row(D_MODEL),
                  full(wsp), full(bsp), full(wout), full(g2), full(wr), full(br)],
        out_specs=[row(D_MODEL), row(D_MODEL // 2), pl.BlockSpec((N_EXPERTS, tm), lambda i: (0, i))],
        out_shape=[jax.ShapeDtypeStruct((n, D_MODEL), F32),
                   jax.ShapeDtypeStruct((n, D_MODEL // 2), jnp.uint32),
                   jax.ShapeDtypeStruct((N_EXPERTS, n), F32)],
        scratch_shapes=[pltpu.VMEM((tm + WINDOW, KV_WIDTH), F32),
                        pltpu.VMEM((tm + WINDOW, KV_WIDTH), F32),
                        pltpu.VMEM((tm, D_MODEL), F32),
                        pltpu.VMEM((GQA_GROUP // 2 * WINDOW, 4 * WINDOW), jnp.int32)],
        compiler_params=_params(("arbitrary",)),
        name="mix_prompt",
    )(sinks, q, k, k, v, v, a, vn, sgb, x, wsp, bsp, wout, g2, wr, br)


def _sample_attn_body(sink_ref, q_ref, k_ref, v_ref, o_ref):
    q = q_ref[...]
    k = k_ref[...]
    v = v_ref[...]
    nq, nk = q.shape[1], k.shape[1]
    heads = lambda t, kk: t[:, :, kk * HEAD_DIM:(kk + 1) * HEAD_DIM].astype(BF16)
    row = lax.broadcasted_iota(jnp.int32, (1, nq, nk), 1)
    first_kv = (row % N_HEADS) < GQA_GROUP
    lg = jnp.where(first_kv,
                   jnp.einsum("bqd,bkd->bqk", q, heads(k, 0), preferred_element_type=F32),
                   jnp.einsum("bqd,bkd->bqk", q, heads(k, 1), preferred_element_type=F32))
    t = row // N_HEADS
    j = lax.broadcasted_iota(jnp.int32, (1, nq, nk), 2)
    lg = jnp.where((j > t) & (j <= t + WINDOW), lg, NEG_INF)
    sink = sink_ref[...][None]
    m = jnp.maximum(jnp.max(lg, axis=2, keepdims=True), sink)
    p = jnp.exp(lg - m)
    den = jnp.sum(p, axis=2, keepdims=True) + jnp.exp(sink - m)
    pb = p.astype(BF16)
    row_o = lax.broadcasted_iota(jnp.int32, (1, nq, HEAD_DIM), 1)
    o = jnp.where((row_o % N_HEADS) < GQA_GROUP,
                  jnp.einsum("bqk,bkd->bqd", pb, heads(v, 0), preferred_element_type=F32),
                  jnp.einsum("bqk,bkd->bqd", pb, heads(v, 1), preferred_element_type=F32))
    o_ref[...] = o / den


def _sample_attn(sink_col, q3, k_all, v_all):
    nb = q3.shape[0]
    bb = 32
    blk = lambda a: pl.BlockSpec((bb,) + a.shape[1:], lambda b: (b, 0, 0))
    return pl.pallas_call(
        _sample_attn_body,
        grid=(nb // bb,),
        in_specs=[pl.BlockSpec(sink_col.shape, lambda b: (0, 0)), blk(q3), blk(k_all), blk(v_all)],
        out_specs=blk(q3),
        out_shape=jax.ShapeDtypeStruct(q3.shape, F32),
        compiler_params=_params(("arbitrary",)),
        name="attn_sample",
    )(sink_col, q3, k_all, v_all)


def _mix_sample_body(a_ref, vn_ref, sgb_ref, o_ref, x_ref, coef_ref, bias_ref,
                     wout_ref, g2_ref, wr_ref, br_ref, xmid_ref, xn2_ref, lg_ref):
    vn = vn_ref[...]
    n, width = vn.shape
    rows8 = lambda t: t.reshape(n // 8, 8, width)
    s = bias_ref[...][None] + coef_ref[0][None] * rows8(vn)
    for d in range(1, coef_ref.shape[0]):
        s = s + coef_ref[d][None] * rows8(pltpu.roll(vn, d, 0))
    merged = a_ref[...] * s.reshape(n, width) + sgb_ref[...] * o_ref[...]
    _finish_rows(merged.astype(BF16), x_ref[...], wout_ref, g2_ref, wr_ref, br_ref, xmid_ref, xn2_ref, lg_ref)


def _mix_sample(a, vn, sgb, o, x, coef, bias, wout, g2, wr, br):
    n = x.shape[0]
    args = (a, vn, sgb, o, x, coef, bias, wout, g2, wr, br)
    full = lambda arr: pl.BlockSpec(arr.shape, lambda i: (0,) * arr.ndim)
    return pl.pallas_call(
        _mix_sample_body,
        grid=(1,),
        in_specs=[full(arr) for arr in args],
        out_specs=[pl.BlockSpec((n, D_MODEL), lambda i: (0, 0)), pl.BlockSpec((n, D_MODEL // 2), lambda i: (0, 0)),
                   pl.BlockSpec((N_EXPERTS, n), lambda i: (0, 0))],
        out_shape=[jax.ShapeDtypeStruct((n, D_MODEL), F32),
                   jax.ShapeDtypeStruct((n, D_MODEL // 2), jnp.uint32),
                   jax.ShapeDtypeStruct((N_EXPERTS, n), F32)],
        compiler_params=_params(("arbitrary",)),
        name="mix_sample",
    )(*args)


def _rows8(rows, dtype):
    n = rows[0].shape[1]
    sub = lax.broadcasted_iota(jnp.int32, (8, n), 0)
    out = jnp.zeros((8, n), dtype)
    for kx, r in enumerate(rows):
        out = jnp.where(sub == kx, r.astype(dtype), out)
    return out


def _route_body(nblk_p, nblk, lgp_ref, lgs_ref, gate_ref, dest_ref, meta_ref, idx_s, rank_s, base):
    i = pl.program_id(0)

    @pl.when(i == 0)
    def _():
        base[...] = jnp.zeros_like(base)

    l = jnp.where(jnp.full(lgp_ref.shape, i, jnp.int32) < nblk_p, lgp_ref[...], lgs_ref[...])
    tb = l.shape[1]
    sub = lax.broadcasted_iota(jnp.int32, l.shape, 0).astype(F32)
    vals, idxs, sels = [], [], []
    for _ in range(TOP_K):
        m = jnp.max(l, axis=0, keepdims=True)
        ik = jnp.min(jnp.where(l == m, sub, float(N_EXPERTS)), axis=0, keepdims=True)
        sel = sub == ik
        l = jnp.where(sel, -jnp.inf, l)
        vals.append(m)
        idxs.append(ik)
        sels.append(sel)
    es = [jnp.exp(vk - vals[0]) for vk in vals]
    den = es[0] + es[1] + es[2] + es[3]
    onehot = jnp.zeros(l.shape, F32)
    for sel in sels:
        onehot = onehot + sel.astype(F32)
    earlier = (lax.broadcasted_iota(jnp.int32, (tb, tb), 0) < lax.broadcasted_iota(jnp.int32, (tb, tb), 1))
    before = jnp.dot(onehot.astype(BF16), earlier.astype(BF16), preferred_element_type=F32) + base[...]
    ranks = [jnp.sum(jnp.where(sel, before, 0.0), axis=0, keepdims=True) for sel in sels]
    base[...] += jnp.sum(onehot, axis=1, keepdims=True)
    idx_s[i] = _rows8(idxs, F32)
    rank_s[i] = _rows8(ranks, F32)
    gates = jnp.concatenate([_rows8([e / den for e in es], F32), jnp.zeros((LANES - 8, tb), F32)], axis=0)
    gate_ref[...] = gates.T

    @pl.when(i == nblk - 1)
    def _():
        cnt = base[...]
        padded = jnp.ceil(cnt / EXPERT_ROWS) * EXPERT_ROWS
        lower = (lax.broadcasted_iota(jnp.int32, (N_EXPERTS, N_EXPERTS), 1) <
                 lax.broadcasted_iota(jnp.int32, (N_EXPERTS, N_EXPERTS), 0)).astype(F32)
        pstart = jnp.dot(lower, jnp.broadcast_to(padded, (N_EXPERTS, LANES)), preferred_element_type=F32,
                         precision=lax.Precision.HIGHEST)[:, :1]
        lane = lax.broadcasted_iota(jnp.int32, (N_EXPERTS, LANES), 1)
        meta = jnp.where(lane == 0, pstart / EXPERT_ROWS,
                         jnp.where(lane == 1, padded / EXPERT_ROWS, jnp.where(lane == 2, cnt, 0.0)))
        meta_ref[...] = meta.astype(jnp.int32)
        sub_e = lax.broadcasted_iota(jnp.int32, (N_EXPERTS, tb), 0).astype(F32)
        for b in range(nblk):
            idx, rank = idx_s[b], rank_s[b]
            rows = [jnp.sum(jnp.where(sub_e == idx[kx:kx + 1], pstart, 0.0), axis=0, keepdims=True)
                    + rank[kx:kx + 1] for kx in range(TOP_K)]
            dest_ref[:, b * tb:(b + 1) * tb] = _rows8(rows, jnp.int32)


def _route(logits_p, logits_s):
    tb = ROUTE_ROWS
    nblk_p, nblk_s = logits_p.shape[1] // tb, logits_s.shape[1] // tb
    nblk = nblk_p + nblk_s
    n = nblk * tb
    assert n == logits_p.shape[1] + logits_s.shape[1]
    return pl.pallas_call(
        functools.partial(_route_body, nblk_p, nblk),
        grid=(nblk,),
        in_specs=[pl.BlockSpec((N_EXPERTS, tb), lambda i: (0, jnp.minimum(i, nblk_p - 1))),
                  pl.BlockSpec((N_EXPERTS, tb), lambda i: (0, jnp.maximum(i - nblk_p, 0)))],
        out_specs=[pl.BlockSpec((tb, LANES), lambda i: (i, 0)),
                   pl.BlockSpec((8, n), lambda i: (0, 0)),
                   pl.BlockSpec((N_EXPERTS, LANES), lambda i: (0, 0))],
        out_shape=[jax.ShapeDtypeStruct((n, LANES), F32),
                   jax.ShapeDtypeStruct((8, n), jnp.int32),
                   jax.ShapeDtypeStruct((N_EXPERTS, LANES), jnp.int32)],
        scratch_shapes=[pltpu.VMEM((nblk, 8, tb), F32), pltpu.VMEM((nblk, 8, tb), F32),
                        pltpu.VMEM((N_EXPERTS, 1), F32)],
        compiler_params=_params(("arbitrary",)),
        name="route",
    )(logits_p, logits_s)


def _sc_mesh():
    return plsc.VectorSubcoreMesh(core_axis_name="c", subcore_axis_name="s")


def _sc_worker():
    return lax.axis_index("s") * SC_CORES + lax.axis_index("c")


def _sc_dispatch(x_p, x_s, dest_t, n_slots):
    chunk = SC_ROWS
    n_p, n_s = x_p.shape[0], x_s.shape[0]
    per_w = n_p // (SC_WORKERS * chunk)
    ns_chunks = n_s // chunk
    assert per_w * SC_WORKERS * chunk == n_p and per_w % 2 == 0
    assert ns_chunks * chunk == n_s and ns_chunks <= SC_WORKERS
    d3 = dest_t.reshape(dest_t.shape[0], (n_p + n_s) // chunk, chunk)
    width, dtype = x_p.shape[1], x_p.dtype

    @functools.partial(
        pl.kernel, mesh=_sc_mesh(),
        out_type=jax.ShapeDtypeStruct((n_slots, width), dtype),
        scratch_types=[pltpu.VMEM((TOP_K, per_w, chunk), jnp.int32),
                       pltpu.VMEM((TOP_K, 1, chunk), jnp.int32),
                       pltpu.VMEM((2, chunk, width), dtype),
                       pltpu.SemaphoreType.DMA, pltpu.SemaphoreType.DMA],
        compiler_params=pltpu.CompilerParams(use_tc_tiling_on_sc=True),
        name="dispatch")
    def run(xp_hbm, xs_hbm, d_hbm, out_hbm, ip_v, is_v, rows_v, rsem, wsem):
        wid = _sc_worker()
        pltpu.sync_copy(d_hbm.at[pl.ds(0, TOP_K), pl.ds(wid * per_w, per_w)], ip_v)

        def read(j, slot):
            return pltpu.make_async_copy(xp_hbm.at[pl.ds((wid * per_w + j) * chunk, chunk)], rows_v.at[slot], rsem)

        def scatter(idx_v, j, slot):
            copies = [pltpu.async_copy(rows_v.at[slot], out_hbm.at[idx_v.at[kx, j]], wsem) for kx in range(TOP_K)]
            for cp in copies:
                cp.wait()

        read(0, 0).start()

        def body(h, carry):
            j = 2 * h
            read(j, 0).wait()
            read(j + 1, 1).start()
            scatter(ip_v, j, 0)
            read(j + 1, 1).wait()

            @pl.when(j + 2 < per_w)
            def _():
                read(j + 2, 0).start()

            scatter(ip_v, j + 1, 1)
            return carry

        lax.fori_loop(0, per_w // 2, body, 0)

        @pl.when(wid < ns_chunks)
        def _():
            pltpu.sync_copy(d_hbm.at[pl.ds(0, TOP_K), pl.ds(n_p // chunk + wid, 1)], is_v)
            pltpu.sync_copy(xs_hbm.at[pl.ds(wid * chunk, chunk)], rows_v.at[0])
            scatter(is_v, 0, 0)

    return run(x_p, x_s, d3)


def _sc_collect(y_sorted, dest_t, n_p, n_s):
    chunk = SC_ROWS
    per_choice = SC_WORKERS // TOP_K
    per_w = n_p // (per_choice * chunk)
    assert per_w * per_choice * chunk == n_p and per_w % 2 == 0
    assert n_s == per_choice * chunk
    d3 = dest_t.reshape(dest_t.shape[0], (n_p + n_s) // chunk, chunk)
    width, dtype = y_sorted.shape[1], y_sorted.dtype

    @functools.partial(
        pl.kernel, mesh=_sc_mesh(),
        out_type=[jax.ShapeDtypeStruct((TOP_K * n_p, width), dtype), jax.ShapeDtypeStruct((TOP_K * n_s, width), dtype)],
        scratch_types=[pltpu.VMEM((per_w, chunk), jnp.int32),
                       pltpu.VMEM((1, chunk), jnp.int32),
                       pltpu.VMEM((2, chunk, width), dtype),
                       pltpu.SemaphoreType.DMA, pltpu.SemaphoreType.DMA],
        compiler_params=pltpu.CompilerParams(use_tc_tiling_on_sc=True),
        name="collect")
    def run(y_hbm, d_hbm, op_hbm, os_hbm, ip_v, is_v, rows_v, gsem, wsem):
        wid = _sc_worker()
        choice = wid // per_choice
        part = wid % per_choice
        pltpu.sync_copy(d_hbm.at[choice, pl.ds(part * per_w, per_w)], ip_v)
        pltpu.sync_copy(d_hbm.at[choice, pl.ds(n_p // chunk + part, 1)], is_v)

        def gather(idx_v, j, slot):
            return pltpu.make_async_copy(y_hbm.at[idx_v.at[j]], rows_v.at[slot], gsem)

        def write(j, slot):
            return pltpu.make_async_copy(rows_v.at[slot], op_hbm.at[pl.ds((wid * per_w + j) * chunk, chunk)], wsem)

        gather(ip_v, 0, 0).start()

        def body(h, carry):
            j = 2 * h
            gather(ip_v, j, 0).wait()

            @pl.when(h > 0)
            def _():
                write(j - 1, 1).wait()

            gather(ip_v, j + 1, 1).start()
            write(j, 0).start()
            gather(ip_v, j + 1, 1).wait()
            write(j, 0).wait()

            @pl.when(j + 2 < per_w)
            def _():
                gather(ip_v, j + 2, 0).start()

            write(j + 1, 1).start()
            return carry

        lax.fori_loop(0, per_w // 2, body, 0)
        write(per_w - 1, 1).wait()

        gather(is_v, 0, 0).start()
        gather(is_v, 0, 0).wait()
        pltpu.sync_copy(rows_v.at[0], os_hbm.at[pl.ds(wid * chunk, chunk)])

    return run(y_sorted, d3)


def _expert_body(blk0_ref, nblk_ref, cnt_ref, wup_hbm, wdn_hbm, bup_ref, bdn_ref, x_hbm, y_hbm,
                 wup_f, wdn_f, wup_s, wdn_s, xbuf, obuf, w_sem, in_sem, out_sem):
    e = pl.program_id(0)
    nb = nblk_ref[e]
    blk0 = blk0_ref[e]
    cnt = cnt_ref[e]
    tm = EXPERT_ROWS
    pair = 2 * LANES
    wslot = e % 2
    up_rows = D_MODEL // W_PIECES
    dn_rows = D_MODEL // (W_PIECES // 2)

    def w_piece(hbm, buf, ex, slot, p, rows):
        start = p * rows if isinstance(p, int) else pl.multiple_of(p * rows, rows)
        r = pl.ds(start, rows)
        return pltpu.make_async_copy(hbm.at[ex, r], buf.at[slot, r], w_sem.at[slot])

    def w_start(ex, slot, p):
        w_piece(wup_hbm, wup_f, ex, slot, p, up_rows).start()
        if isinstance(p, int):
            if p < W_PIECES // 2:
                w_piece(wdn_hbm, wdn_f, ex, slot, p, dn_rows).start()
        else:
            @pl.when(p < W_PIECES // 2)
            def _():
                w_piece(wdn_hbm, wdn_f, ex, slot, p, dn_rows).start()

    def w_wait(ex, slot):
        for p in range(W_PIECES):
            w_piece(wup_hbm, wup_f, ex, slot, p, up_rows).wait()
        for p in range(W_PIECES // 2):
            w_piece(wdn_hbm, wdn_f, ex, slot, p, dn_rows).wait()

    @pl.when(e == 0)
    def _():
        for p in range(W_PIECES):
            w_start(0, 0, p)

    w_wait(e, wslot)
    wup_ref = wup_f.at[wslot]
    wdn_ref = wdn_f.at[wslot]
    more = e + 1 < N_EXPERTS

    def x_copy(i, slot):
        rows = pl.ds(pl.multiple_of((blk0 + i) * tm, tm), tm)
        return pltpu.make_async_copy(x_hbm.at[rows], xbuf.at[slot], in_sem.at[slot])

    def y_copy(i, slot):
        rows = pl.ds(pl.multiple_of((blk0 + i) * tm, tm), tm)
        return pltpu.make_async_copy(obuf.at[slot], y_hbm.at[rows], out_sem.at[slot])

    @pl.when(nb > 0)
    def _():
        x_copy(0, 0).start(priority=1)
        r = lax.broadcasted_iota(jnp.int32, (pair, pair), 0)
        c = lax.broadcasted_iota(jnp.int32, (pair, pair), 1)
        perm = (r == jnp.where(c < LANES, 2 * c, 2 * (c - LANES) + 1)).astype(BF16)
        for g in range(2 * D_MODEL // pair):
            cols = slice(g * pair, (g + 1) * pair)
            wup_s[g] = jnp.dot(wup_ref[:, cols].astype(BF16), perm, preferred_element_type=F32).astype(BF16)
        for g in range(D_MODEL // pair):
            wdn_s[g] = wdn_ref[:, g * pair:(g + 1) * pair].astype(BF16)

        def block(i, carry):
            slot = i % 2
            x_copy(i, slot).wait()

            @pl.when(i + 1 < nb)
            def _():
                x_copy(i + 1, 1 - slot).start(priority=1)

            @pl.when(i >= 2)
            def _():
                y_copy(i - 2, slot).wait()

            @pl.when(jnp.logical_and(more, i < W_PIECES))
            def _():
                w_start(e + 1, 1 - wslot, i)

            row = lax.broadcasted_iota(jnp.int32, (tm, 1), 0)
            x = jnp.where(row < cnt - i * tm, _unpack_bf16_pair(xbuf[slot]), 0.0).astype(BF16)
            acts = []
            for g in range(2 * D_MODEL // pair):
                cols = slice(g * pair, (g + 1) * pair)
                h = jnp.dot(x, wup_s[g], preferred_element_type=F32) + bup_ref[0, :, cols]
                glu = jnp.minimum(h[:, :LANES], SWIGLU_LIMIT)
                lin = jnp.clip(h[:, LANES:], -SWIGLU_LIMIT, SWIGLU_LIMIT)
                acts.append((glu * jax.nn.sigmoid(SWIGLU_ALPHA * glu) * (lin + 1.0)).astype(BF16))
            act = jnp.concatenate(acts, axis=1)
            half_groups = D_MODEL // pair // 2
            for g in range(half_groups):
                ys = []
                for gg in (g, g + half_groups):
                    cols = slice(gg * pair, (gg + 1) * pair)
                    ys.append(jnp.dot(act, wdn_s[gg], preferred_element_type=F32) + bdn_ref[0, :, cols])
                obuf[slot, :, g * pair:(g + 1) * pair] = _pack_bf16_pair(ys[0], ys[1])
            y_copy(i, slot).start(priority=1)
            return carry

        lax.fori_loop(0, nb, block, 0)

        @pl.when(nb >= 2)
        def _():
            y_copy(nb - 2, nb % 2).wait()

        y_copy(nb - 1, (nb - 1) % 2).wait()

    for p in range(W_PIECES):
        @pl.when(jnp.logical_and(more, p >= nb))
        def _():
            w_start(e + 1, 1 - wslot, p)


def _experts(blk0, nblk, cnt, x_sorted, w_up, w_down, b_up_grouped, b_down):
    tm = EXPERT_ROWS
    per_expert = lambda a: pl.BlockSpec((1,) + a.shape[1:], lambda e, b0, nb, ct: (e, 0, 0))
    hbm = pl.BlockSpec(memory_space=pl.ANY)
    grid_spec = pltpu.PrefetchScalarGridSpec(
        num_scalar_prefetch=3,
        grid=(N_EXPERTS,),
        in_specs=[hbm, hbm, per_expert(b_up_grouped), per_expert(b_down), hbm],
        out_specs=hbm,
        scratch_shapes=[pltpu.VMEM((2,) + w_up.shape[1:], F32), pltpu.VMEM((2,) + w_down.shape[1:], F32),
                        pltpu.VMEM((2 * D_MODEL // (2 * LANES), D_MODEL, 2 * LANES), BF16),
                        pltpu.VMEM((D_MODEL // (2 * LANES), D_MODEL, 2 * LANES), BF16),
                        pltpu.VMEM((2, tm, x_sorted.shape[1]), x_sorted.dtype),
                        pltpu.VMEM((2, tm, D_MODEL // 2), jnp.uint32),
                        pltpu.SemaphoreType.DMA((2,)), pltpu.SemaphoreType.DMA((2,)), pltpu.SemaphoreType.DMA((2,))],
    )
    return pl.pallas_call(
        _expert_body,
        grid_spec=grid_spec,
        out_shape=jax.ShapeDtypeStruct((x_sorted.shape[0], D_MODEL // 2), jnp.uint32),
        compiler_params=_params(("arbitrary",)),
        name="experts",
    )(blk0, nblk, cnt, w_up, w_down, b_up_grouped, b_down, x_sorted)


def _combine_body(gate_ref, xmid_ref, gfin_ref, y0_ref, y1_ref, y2_ref, y3_ref, out_ref):
    gate = gate_ref[...]
    moe = _unpack_bf16_pair(y0_ref[...]) * gate[:, 0:1]
    for kx, y_ref in enumerate((y1_ref, y2_ref, y3_ref), start=1):
        moe = moe + _unpack_bf16_pair(y_ref[...]) * gate[:, kx:kx + 1]
    out_ref[...] = _rms(xmid_ref[...] + moe, gfin_ref[...])


def _combine(gates, first_token, xmid, gfin, y_rows):
    n = xmid.shape[0]
    tt = COMBINE_ROWS
    nblk = n // tt
    blk0 = first_token // tt
    assert blk0 * tt == first_token
    choice = lambda kx: pl.BlockSpec((tt, y_rows.shape[1]), lambda i: (i + kx * nblk, 0))
    return pl.pallas_call(
        _combine_body,
        grid=(nblk,),
        in_specs=[pl.BlockSpec((tt, LANES), lambda i: (i + blk0, 0)),
                  pl.BlockSpec((tt, D_MODEL), lambda i: (i, 0)),
                  pl.BlockSpec((1, D_MODEL), lambda i: (0, 0))] + [choice(kx) for kx in range(TOP_K)],
        out_specs=pl.BlockSpec((tt, D_MODEL), lambda i: (i, 0)),
        out_shape=jax.ShapeDtypeStruct((n, D_MODEL), F32),
        compiler_params=_params(("arbitrary",)),
        name="combine",
    )(gates, xmid, gfin, y_rows, y_rows, y_rows, y_rows)


def kernel(x_prompt, x_sample, cache_k_win, cache_v_win, norm_attn_g, w_in, ln_v_g, ln_v_b, w_spatial, b_spatial,
           attn_sinks, w_out, norm_ffn_g, w_router, b_router, w_up, b_up, w_down, b_down, norm_final_g):
    bp, tp, _ = x_prompt.shape
    bs, ts, _ = x_sample.shape
    w_buf = cache_k_win.shape[2]
    assert bp == 1 and tp % MIX_ROWS == 0 and w_buf == WINDOW and (bs * ts) % PROJ_ROWS == 0 and 8 % ts == 0
    n_p, n_s = bp * tp, bs * ts
    row2 = lambda a: a.reshape(1, -1)

    w_in_bf = w_in[0].astype(BF16)
    w_out_bf = w_out[0].astype(BF16)
    tril = jnp.tril(jnp.ones((CHUNK, CHUNK), dtype=bool))
    wsp = jnp.where(tril[None], w_spatial[0], 0.0)
    wsp_bf = wsp.astype(BF16)
    bsp = jnp.broadcast_to(b_spatial[0][:, :, None], (GMLP_GROUPS, CHUNK, LANES))
    b_up_grouped = b_up[0].reshape(N_EXPERTS, -1, LANES, 2).transpose(0, 1, 3, 2).reshape(N_EXPERTS, 1, -1)
    bd = b_down[0][:, None, :]
    g1, g2, gfin = row2(norm_attn_g[0]), row2(norm_ffn_g[0]), row2(norm_final_g)
    lng, lnb = row2(ln_v_g[0]), row2(ln_v_b[0])
    wr_hi = w_router[0].astype(BF16)
    wr = jnp.concatenate([wr_hi, (w_router[0] - wr_hi.astype(F32)).astype(BF16)], axis=1)
    br = row2(b_router[0])
    sinks = attn_sinks[0]

    xp = x_prompt.reshape(n_p, D_MODEL)
    cs_p = _rotary_inputs(jnp.arange(tp, dtype=jnp.int32))
    q_p, k_p, v_p, a_p, vn_p, sgb_p = _proj(xp, g1, w_in_bf, cs_p, lng, lnb)
    xmid_p, xn2_p, lg_p = _mix(sinks, q_p, k_p, v_p, a_p, vn_p, sgb_p, xp, wsp_bf, bsp, w_out_bf, g2, wr, br)

    xs = x_sample.reshape(n_s, D_MODEL)
    pos_s = PAST_LEN + jnp.arange(ts, dtype=jnp.int32)
    cs_s = _rotary_inputs(jnp.tile(pos_s, bs))
    q_s, k_s, v_s, a_s, vn_s, sgb_s = _proj(xs, g1, w_in_bf, cs_s, lng, lnb)
    n_keys = w_buf + ts
    key_pad = jnp.zeros((bs, (-n_keys) % 8, KV_WIDTH), F32)
    with_new = lambda cache, new: jnp.concatenate(
        [cache[0].reshape(bs, w_buf, KV_WIDTH), new.reshape(bs, ts, KV_WIDTH), key_pad], axis=1)
    k_all = with_new(cache_k_win, k_s)
    v_all = with_new(cache_v_win, v_s)
    sink_col = jnp.tile(sinks, ts).reshape(ts * N_HEADS, 1)
    o_s = _sample_attn(sink_col, q_s.reshape(bs, ts * N_HEADS, HEAD_DIM), k_all, v_all).reshape(n_s, Q_WIDTH)
    t_idx = jnp.arange(ts)
    coef = jnp.stack([jnp.where((t_idx >= d)[None, :], wsp[:, t_idx, jnp.maximum(t_idx - d, 0)], 0.0)
                      for d in range(ts)])
    coef = jnp.repeat(coef.transpose(0, 2, 1), GMLP_WIDTH // GMLP_GROUPS, axis=2)
    coef = jnp.tile(coef, (1, 8 // ts, 1))
    bias = jnp.tile(jnp.repeat(b_spatial[0][:, :ts].T, GMLP_WIDTH // GMLP_GROUPS, axis=1), (8 // ts, 1))
    xmid_s, xn2_s, lg_s = _mix_sample(a_s, vn_s, sgb_s, o_s, xs, coef, bias, w_out_bf, g2, wr, br)

    n_tok = n_p + n_s
    gate_w, dest_t, meta = _route(lg_p, lg_s)
    tm = EXPERT_ROWS
    n_blocks = (n_tok * TOP_K) // tm + N_EXPERTS

    x_sorted = _sc_dispatch(xn2_p, xn2_s, dest_t, n_blocks * tm)
    y_sorted = _experts(meta[:, 0], meta[:, 1], meta[:, 2], x_sorted, w_up[0], w_down[0], b_up_grouped, bd)
    yrows_p, yrows_s = _sc_collect(y_sorted, dest_t, n_p, n_s)
    y_p = _combine(gate_w, 0, xmid_p, gfin, yrows_p)
    y_s = _combine(gate_w, n_p, xmid_s, gfin, yrows_s)

    k4 = lambda t: t.reshape(1, bp, -1, N_KV_HEADS, HEAD_DIM)
    return (y_p.reshape(bp, tp, D_MODEL),
            y_s.reshape(bs, ts, D_MODEL),
            k4(k_p[n_p - WINDOW:]),
            k4(v_p[n_p - WINDOW:]),
            vn_p[n_p - CHUNK:].reshape(1, bp, CHUNK, GMLP_WIDTH),
            k_all[:, ts:n_keys].reshape(1, bs, w_buf, N_KV_HEADS, HEAD_DIM),
            v_all[:, ts:n_keys].reshape(1, bs, w_buf, N_KV_HEADS, HEAD_DIM),
            vn_s.reshape(1, bs, ts, GMLP_WIDTH))
```

```python
import functools

import numpy as np
import jax
import jax.numpy as jnp
from jax import lax
from jax.experimental import pallas as pl
from jax.experimental.pallas import tpu as pltpu
from jax.experimental.pallas import tpu_sc as plsc

F32 = jnp.float32
BF16 = jnp.bfloat16

D_MODEL = 1024
HEAD_DIM = 64
N_HEADS = 16
GQA_GROUP = 8
N_KV_HEADS = 2
Q_WIDTH = 1024
KV_WIDTH = 128
WINDOW = 128
ROT_DIM = 16
ROPE_THETA = 500000.0
CHUNK = 128
GMLP_WIDTH = 1024
GMLP_GROUPS = 8
N_EXPERTS = 32
TOP_K = 4
SWIGLU_LIMIT = 7.0
SWIGLU_ALPHA = 1.702
RMS_EPS = 1e-5
LN_EPS = 1e-5
NEG_INF = -1e30
PAST_LEN = 16384

LANES = 128
VMEM_LIMIT = 56 * 1024 * 1024

PROJ_ROWS = 256
MIX_ROWS = 512
ROUTE_ROWS = 512
EXPERT_ROWS = 256
W_PIECES = 8
COMBINE_ROWS = 1024

SC_CORES = 2
SC_WORKERS = 32
SC_ROWS = 64

_C_Q, _C_KV, _C_U, _C_VG, _C_GA, _C_GB, _C_END = 0, 1024, 1280, 2304, 3328, 4352, 5376


def _params(sem):
    return pltpu.CompilerParams(dimension_semantics=sem, vmem_limit_bytes=VMEM_LIMIT)


def _rms(x, g):
    return x * lax.rsqrt(jnp.mean(x * x, axis=-1, keepdims=True) + RMS_EPS) * g


def _pack_bf16_pair(lo, hi):
    lo_bits = lax.bitcast_convert_type(lo.astype(BF16).astype(F32), jnp.uint32)
    hi_bits = lax.bitcast_convert_type(hi.astype(BF16).astype(F32), jnp.uint32)
    return (lo_bits >> 16) | hi_bits


def _unpack_bf16_pair(words):
    lo = lax.bitcast_convert_type(words << 16, F32)
    hi = lax.bitcast_convert_type(words & jnp.uint32(0xFFFF0000), F32)
    return jnp.concatenate([lo, hi], axis=1)


def _proj_body(x_ref, g_ref, w_ref, cs_ref, rot_ref, lng_ref, lnb_ref,
               q_ref, k_ref, v_ref, a_ref, vn_ref, sgb_ref):
    h = _rms(x_ref[...], g_ref[...]).astype(BF16)
    tabs = lax.dot_general(cs_ref[...], rot_ref[...], (((0,), (0,)), ((), ())), preferred_element_type=F32)
    rc, rs1, rs2 = tabs[:, :LANES], tabs[:, LANES:2 * LANES], tabs[:, 2 * LANES:]

    def rot(z):
        return z * rc + pltpu.roll(z, LANES - ROT_DIM // 2, 1) * rs1 + pltpu.roll(z, ROT_DIM // 2, 1) * rs2

    def mm(lo, hi):
        return jnp.dot(h, w_ref[:, lo:hi], preferred_element_type=F32)

    zq = mm(_C_Q, _C_KV)
    for c in range(Q_WIDTH // LANES):
        sl = slice(c * LANES, (c + 1) * LANES)
        q_ref[:, sl] = (rot(zq[:, sl]) * (HEAD_DIM ** -0.5)).astype(BF16)
    zkv = mm(_C_KV, _C_U)
    k_ref[...] = rot(zkv[:, :KV_WIDTH])
    v_ref[...] = zkv[:, KV_WIDTH:]
    a_ref[...] = jax.nn.sigmoid(mm(_C_GA, _C_GB)) * jax.nn.gelu(mm(_C_U, _C_VG))
    zv = jax.nn.gelu(mm(_C_VG, _C_GA))
    zc = zv - jnp.mean(zv, axis=-1, keepdims=True)
    var = jnp.mean(zc * zc, axis=-1, keepdims=True)
    vn_ref[...] = zc * lax.rsqrt(var + LN_EPS) * lng_ref[...] + lnb_ref[...]
    sgb_ref[...] = jax.nn.sigmoid(mm(_C_GB, _C_END))


def _proj(x, norm_g, w_in_bf, cs, ln_g, ln_b):
    n = x.shape[0]
    tm = PROJ_ROWS
    row = lambda w: pl.BlockSpec((tm, w), lambda i: (i, 0))
    full = lambda a: pl.BlockSpec(a.shape, lambda i: (0,) * a.ndim)
    rot = jnp.asarray(np.tile(_ROT_EXPAND, (3, 1)), dtype=BF16)
    return pl.pallas_call(
        _proj_body,
        grid=(n // tm,),
        in_specs=[row(D_MODEL), full(norm_g), full(w_in_bf), pl.BlockSpec((cs.shape[0], tm), lambda i: (0, i)),
                  full(rot),
                  full(ln_g), full(ln_b)],
        out_specs=[row(Q_WIDTH), row(KV_WIDTH), row(KV_WIDTH), row(GMLP_WIDTH), row(GMLP_WIDTH), row(D_MODEL)],
        out_shape=[jax.ShapeDtypeStruct((n, Q_WIDTH), BF16),
                   jax.ShapeDtypeStruct((n, KV_WIDTH), F32),
                   jax.ShapeDtypeStruct((n, KV_WIDTH), F32),
                   jax.ShapeDtypeStruct((n, GMLP_WIDTH), F32),
                   jax.ShapeDtypeStruct((n, GMLP_WIDTH), F32),
                   jax.ShapeDtypeStruct((n, D_MODEL), F32)],
        compiler_params=_params(("arbitrary",)),
        name="proj",
    )(x, norm_g, w_in_bf, cs, rot, ln_g, ln_b)


_ROT_COLS = 32


def _rot_expand():
    half = ROT_DIM // 2
    m = np.zeros((_ROT_COLS, 3 * LANES), np.float32)
    for lane in range(LANES):
        d = lane % HEAD_DIM
        if d < ROT_DIM:
            m[d % half, lane] = 1.0
        else:
            m[2 * half, lane] = 1.0
        if d < half:
            m[half + d, LANES + lane] = -1.0
        elif d < ROT_DIM:
            m[half + d - half, 2 * LANES + lane] = 1.0
    return m


_ROT_EXPAND = _rot_expand()


def _rotary_inputs(pos):
    half = ROT_DIM // 2
    inv_freq = ROPE_THETA ** (-jnp.arange(half, dtype=F32) / half)
    ang = inv_freq[:, None] * pos.astype(F32)[None, :]
    n = pos.shape[0]
    cs = jnp.concatenate([jnp.cos(ang), jnp.sin(ang), jnp.ones((1, n), F32),
                          jnp.zeros((_ROT_COLS - 2 * half - 1, n), F32)], axis=0)
    hi = cs.astype(BF16)
    rest = cs - hi.astype(F32)
    mid = rest.astype(BF16)
    lo = (rest - mid.astype(F32)).astype(BF16)
    return jnp.concatenate([hi, mid, lo], axis=0)


def _finish_rows(merged_bf, x, wout_ref, g2_ref, wr_ref, br_ref, xmid_ref, xn2_ref, lg_ref):
    xm = x + jnp.dot(merged_bf, wout_ref[...], preferred_element_type=F32)
    xmid_ref[...] = xm
    xn = _rms(xm, g2_ref[...])
    x_hi = xn.astype(BF16)
    x_lo = (xn - x_hi.astype(F32)).astype(BF16)
    w_hl = wr_ref[...]
    p_hi = jnp.dot(x_hi, w_hl, preferred_element_type=F32)
    p_lo = jnp.dot(x_lo, w_hl[:, :N_EXPERTS], preferred_element_type=F32)
    lg = p_hi[:, :N_EXPERTS] + (p_hi[:, N_EXPERTS:] + p_lo) + br_ref[...]
    wide = jnp.concatenate([lg, jnp.zeros((lg.shape[0], LANES - N_EXPERTS), F32)], axis=1)
    lg_ref[...] = wide.T[:N_EXPERTS]
    xn2_ref[...] = _pack_bf16_pair(xn[:, :D_MODEL // 2], xn[:, D_MODEL // 2:])


def _mix_body(sinks_ref, q_ref, k_ref, kp_ref, v_ref, vp_ref, a_ref, vn_ref, sgb_ref, x_ref,
              wsp_ref, bsp_ref, wout_ref, g2_ref, wr_ref, br_ref,
              xmid_ref, xn2_ref, lg_ref, kcat, vcat, mrg, key_s):
    i = pl.program_id(0)
    nsub = MIX_ROWS // WINDOW
    kcat[0:WINDOW] = kp_ref[...]
    kcat[WINDOW:] = k_ref[...]
    vcat[0:WINDOW] = vp_ref[...]
    vcat[WINDOW:] = v_ref[...]

    pair_rows = (GQA_GROUP // 2) * WINDOW
    lane_kv = lax.broadcasted_iota(jnp.int32, (2 * WINDOW, LANES), 1)
    lane_o = lax.broadcasted_iota(jnp.int32, (pair_rows, LANES), 1)

    @pl.when(i == 0)
    def _():
        rq = lax.broadcasted_iota(jnp.int32, (pair_rows, 4 * WINDOW), 0) & (WINDOW - 1)
        ck = lax.broadcasted_iota(jnp.int32, (pair_rows, 4 * WINDOW), 1) & (2 * WINDOW - 1)
        key_s[...] = jnp.where((ck > rq) & (ck <= rq + WINDOW), ck, -1)

    row_p = lax.broadcasted_iota(jnp.int32, (pair_rows, 1), 0) >> 7
    sink_cols = []
    for kk in range(N_KV_HEADS):
        h0 = kk * GQA_GROUP
        se = jnp.full((pair_rows, 1), sinks_ref[h0], F32)
        so = jnp.full((pair_rows, 1), sinks_ref[h0 + 1], F32)
        for p in range(1, GQA_GROUP // 2):
            se = jnp.where(row_p == p, sinks_ref[h0 + 2 * p], se)
            so = jnp.where(row_p == p, sinks_ref[h0 + 2 * p + 1], so)
        sink_cols.append((se, so))

    def sub(j, carry):
        off = pl.multiple_of(j * WINDOW, WINDOW)
        rows = pl.ds(off, WINDOW)
        for g in range(GMLP_GROUPS):
            cols = slice(g * LANES, (g + 1) * LANES)
            s = jnp.dot(wsp_ref[g], vn_ref[rows, cols].astype(BF16), preferred_element_type=F32) + bsp_ref[g]
            mrg[rows, cols] = a_ref[rows, cols] * s
        kblk = kcat[pl.ds(off, 2 * WINDOW), :]
        vblk = vcat[pl.ds(off, 2 * WINDOW), :]
        kswp = pltpu.roll(kblk, HEAD_DIM, 1)
        vswp = pltpu.roll(vblk, HEAD_DIM, 1)
        kmin = jnp.where(jnp.logical_and(i == 0, j == 0), WINDOW, 0)
        allowed = key_s[...] >= kmin
        for kk in range(N_KV_HEADS):
            lo_src, hi_src = (kblk, kswp) if kk == 0 else (kswp, kblk)
            kbd = jnp.concatenate([jnp.where(lane_kv < HEAD_DIM, lo_src, 0.0),
                                   jnp.where(lane_kv >= HEAD_DIM, hi_src, 0.0)], axis=0).astype(BF16)
            lo_src, hi_src = (vblk, vswp) if kk == 0 else (vswp, vblk)
            vbd = jnp.concatenate([jnp.where(lane_kv < HEAD_DIM, lo_src, 0.0),
                                   jnp.where(lane_kv >= HEAD_DIM, hi_src, 0.0)], axis=0).astype(BF16)
            pair0 = kk * (GQA_GROUP // 2)
            qs = jnp.concatenate([q_ref[rows, (pair0 + p) * LANES:(pair0 + p + 1) * LANES]
                                  for p in range(GQA_GROUP // 2)], axis=0)
            lg = lax.dot_general(qs, kbd, (((1,), (1,)), ((), ())), preferred_element_type=F32)
            lg = jnp.where(allowed, lg, NEG_INF)
            se, so = sink_cols[kk]
            le, lo = lg[:, :2 * WINDOW], lg[:, 2 * WINDOW:]
            me = jnp.maximum(jnp.max(le, axis=1, keepdims=True), se)
            mo = jnp.maximum(jnp.max(lo, axis=1, keepdims=True), so)
            pe = jnp.exp(le - me)
            po = jnp.exp(lo - mo)
            de = jnp.sum(pe, axis=1, keepdims=True) + jnp.exp(se - me)
            do = jnp.sum(po, axis=1, keepdims=True) + jnp.exp(so - mo)
            pr = jnp.concatenate([pe, po], axis=1).astype(BF16)
            o = jnp.dot(pr, vbd, preferred_element_type=F32)
            o = o / jnp.where(lane_o < HEAD_DIM, de, do)
            for p in range(GQA_GROUP // 2):
                cols = slice((pair0 + p) * LANES, (pair0 + p + 1) * LANES)
                mrg[rows, cols] += sgb_ref[rows, cols] * o[p * WINDOW:(p + 1) * WINDOW]
        return carry

    lax.fori_loop(0, nsub, sub, 0)
    _finish_rows(mrg[...].astype(BF16), x_ref[...], wout_ref, g2_ref, wr_ref, br_ref, xmid_ref, xn2_ref, lg_ref)


def _mix(sinks, q, k, v, a, vn, sgb, x, wsp, bsp, wout, g2, wr, br):
    n = x.shape[0]
    tm = MIX_ROWS
    nsub = tm // WINDOW
    row = lambda w: pl.BlockSpec((tm, w), lambda i: (i, 0))
    prev = pl.BlockSpec((WINDOW, KV_WIDTH), lambda i: (jnp.maximum(i * nsub - 1, 0), 0))
    full = lambda arr: pl.BlockSpec(arr.shape, lambda i: (0,) * arr.ndim)
    smem = pl.BlockSpec(memory_space=pltpu.SMEM)
    return pl.pallas_call(
        _mix_body,
        grid=(n // tm,),
        in_specs=[smem, row(Q_WIDTH), row(KV_WIDTH), prev, row(KV_WIDTH), prev,
                  row(GMLP_WIDTH), row(GMLP_WIDTH), row(D_MODEL), row(D_MODEL),
                  full(wsp), full(bsp), full(wout), full(g2), full(wr), full(br)],
        out_specs=[row(D_MODEL), row(D_MODEL // 2), pl.BlockSpec((N_EXPERTS, tm), lambda i: (0, i))],
        out_shape=[jax.ShapeDtypeStruct((n, D_MODEL), F32),
                   jax.ShapeDtypeStruct((n, D_MODEL // 2), jnp.uint32),
                   jax.ShapeDtypeStruct((N_EXPERTS, n), F32)],
        scratch_shapes=[pltpu.VMEM((tm + WINDOW, KV_WIDTH), F32),
                        pltpu.VMEM((tm + WINDOW, KV_WIDTH), F32),
                        pltpu.VMEM((tm, D_MODEL), F32),
                        pltpu.VMEM((GQA_GROUP // 2 * WINDOW, 4 * WINDOW), jnp.int32)],
        compiler_params=_params(("arbitrary",)),
        name="mix_prompt",
    )(sinks, q, k, k, v, v, a, vn, sgb, x, wsp, bsp, wout, g2, wr, br)


def _sample_attn_body(sink_ref, q_ref, k_ref, v_ref, o_ref):
    q = q_ref[...]
    k = k_ref[...]
    v = v_ref[...]
    nq, nk = q.shape[1], k.shape[1]
    heads = lambda t, kk: t[:, :, kk * HEAD_DIM:(kk + 1) * HEAD_DIM].astype(BF16)
    row = lax.broadcasted_iota(jnp.int32, (1, nq, nk), 1)
    first_kv = (row % N_HEADS) < GQA_GROUP
    lg = jnp.where(first_kv,
                   jnp.einsum("bqd,bkd->bqk", q, heads(k, 0), preferred_element_type=F32),
                   jnp.einsum("bqd,bkd->bqk", q, heads(k, 1), preferred_element_type=F32))
    t = row // N_HEADS
    j = lax.broadcasted_iota(jnp.int32, (1, nq, nk), 2)
    lg = jnp.where((j > t) & (j <= t + WINDOW), lg, NEG_INF)
    sink = sink_ref[...][None]
    m = jnp.maximum(jnp.max(lg, axis=2, keepdims=True), sink)
    p = jnp.exp(lg - m)
    den = jnp.sum(p, axis=2, keepdims=True) + jnp.exp(sink - m)
    pb = p.astype(BF16)
    row_o = lax.broadcasted_iota(jnp.int32, (1, nq, HEAD_DIM), 1)
    o = jnp.where((row_o % N_HEADS) < GQA_GROUP,
                  jnp.einsum("bqk,bkd->bqd", pb, heads(v, 0), preferred_element_type=F32),
                  jnp.einsum("bqk,bkd->bqd", pb, heads(v, 1), preferred_element_type=F32))
    o_ref[...] = o / den


def _sample_attn(sink_col, q3, k_all, v_all):
    nb = q3.shape[0]
    bb = 32
    blk = lambda a: pl.BlockSpec((bb,) + a.shape[1:], lambda b: (b, 0, 0))
    return pl.pallas_call(
        _sample_attn_body,
        grid=(nb // bb,),
        in_specs=[pl.BlockSpec(sink_col.shape, lambda b: (0, 0)), blk(q3), blk(k_all), blk(v_all)],
        out_specs=blk(q3),
        out_shape=jax.ShapeDtypeStruct(q3.shape, F32),
        compiler_params=_params(("arbitrary",)),
        name="attn_sample",
    )(sink_col, q3, k_all, v_all)


def _mix_sample_body(a_ref, vn_ref, sgb_ref, o_ref, x_ref, coef_ref, bias_ref,
                     wout_ref, g2_ref, wr_ref, br_ref, xmid_ref, xn2_ref, lg_ref):
    vn = vn_ref[...]
    n, width = vn.shape
    rows8 = lambda t: t.reshape(n // 8, 8, width)
    s = bias_ref[...][None] + coef_ref[0][None] * rows8(vn)
    for d in range(1, coef_ref.shape[0]):
        s = s + coef_ref[d][None] * rows8(pltpu.roll(vn, d, 0))
    merged = a_ref[...] * s.reshape(n, width) + sgb_ref[...] * o_ref[...]
    _finish_rows(merged.astype(BF16), x_ref[...], wout_ref, g2_ref, wr_ref, br_ref, xmid_ref, xn2_ref, lg_ref)


def _mix_sample(a, vn, sgb, o, x, coef, bias, wout, g2, wr, br):
    n = x.shape[0]
    args = (a, vn, sgb, o, x, coef, bias, wout, g2, wr, br)
    full = lambda arr: pl.BlockSpec(arr.shape, lambda i: (0,) * arr.ndim)
    return pl.pallas_call(
        _mix_sample_body,
        grid=(1,),
        in_specs=[full(arr) for arr in args],
        out_specs=[pl.BlockSpec((n, D_MODEL), lambda i: (0, 0)), pl.BlockSpec((n, D_MODEL // 2), lambda i: (0, 0)),
                   pl.BlockSpec((N_EXPERTS, n), lambda i: (0, 0))],
        out_shape=[jax.ShapeDtypeStruct((n, D_MODEL), F32),
                   jax.ShapeDtypeStruct((n, D_MODEL // 2), jnp.uint32),
                   jax.ShapeDtypeStruct((N_EXPERTS, n), F32)],
        compiler_params=_params(("arbitrary",)),
        name="mix_sample",
    )(*args)


def _rows8(rows, dtype):
    n = rows[0].shape[1]
    sub = lax.broadcasted_iota(jnp.int32, (8, n), 0)
    out = jnp.zeros((8, n), dtype)
    for kx, r in enumerate(rows):
        out = jnp.where(sub == kx, r.astype(dtype), out)
    return out


def _route_body(nblk_p, nblk, lgp_ref, lgs_ref, gate_ref, dest_ref, meta_ref, idx_s, rank_s, base):
    i = pl.program_id(0)

    @pl.when(i == 0)
    def _():
        base[...] = jnp.zeros_like(base)

    l = jnp.where(jnp.full(lgp_ref.shape, i, jnp.int32) < nblk_p, lgp_ref[...], lgs_ref[...])
    tb = l.shape[1]
    sub = lax.broadcasted_iota(jnp.int32, l.shape, 0).astype(F32)
    vals, idxs, sels = [], [], []
    for _ in range(TOP_K):
        m = jnp.max(l, axis=0, keepdims=True)
        ik = jnp.min(jnp.where(l == m, sub, float(N_EXPERTS)), axis=0, keepdims=True)
        sel = sub == ik
        l = jnp.where(sel, -jnp.inf, l)
        vals.append(m)
        idxs.append(ik)
        sels.append(sel)
    es = [jnp.exp(vk - vals[0]) for vk in vals]
    den = es[0] + es[1] + es[2] + es[3]
    onehot = jnp.zeros(l.shape, F32)
    for sel in sels:
        onehot = onehot + sel.astype(F32)
    earlier = (lax.broadcasted_iota(jnp.int32, (tb, tb), 0) < lax.broadcasted_iota(jnp.int32, (tb, tb), 1))
    before = jnp.dot(onehot.astype(BF16), earlier.astype(BF16), preferred_element_type=F32) + base[...]
    ranks = [jnp.sum(jnp.where(sel, before, 0.0), axis=0, keepdims=True) for sel in sels]
    base[...] += jnp.sum(onehot, axis=1, keepdims=True)
    idx_s[i] = _rows8(idxs, F32)
    rank_s[i] = _rows8(ranks, F32)
    gates = jnp.concatenate([_rows8([e / den for e in es], F32), jnp.zeros((LANES - 8, tb), F32)], axis=0)
    gate_ref[...] = gates.T

    @pl.when(i == nblk - 1)
    def _():
        cnt = base[...]
        padded = jnp.ceil(cnt / EXPERT_ROWS) * EXPERT_ROWS
        lower = (lax.broadcasted_iota(jnp.int32, (N_EXPERTS, N_EXPERTS), 1) <
                 lax.broadcasted_iota(jnp.int32, (N_EXPERTS, N_EXPERTS), 0)).astype(F32)
        pstart = jnp.dot(lower, jnp.broadcast_to(padded, (N_EXPERTS, LANES)), preferred_element_type=F32,
                         precision=lax.Precision.HIGHEST)[:, :1]
        lane = lax.broadcasted_iota(jnp.int32, (N_EXPERTS, LANES), 1)
        meta = jnp.where(lane == 0, pstart / EXPERT_ROWS,
                         jnp.where(lane == 1, padded / EXPERT_ROWS, jnp.where(lane == 2, cnt, 0.0)))
        meta_ref[...] = meta.astype(jnp.int32)
        sub_e = lax.broadcasted_iota(jnp.int32, (N_EXPERTS, tb), 0).astype(F32)
        for b in range(nblk):
            idx, rank = idx_s[b], rank_s[b]
            rows = [jnp.sum(jnp.where(sub_e == idx[kx:kx + 1], pstart, 0.0), axis=0, keepdims=True)
                    + rank[kx:kx + 1] for kx in range(TOP_K)]
            dest_ref[:, b * tb:(b + 1) * tb] = _rows8(rows, jnp.int32)


def _route(logits_p, logits_s):
    tb = ROUTE_ROWS
    nblk_p, nblk_s = logits_p.shape[1] // tb, logits_s.shape[1] // tb
    nblk = nblk_p + nblk_s
    n = nblk * tb
    assert n == logits_p.shape[1] + logits_s.shape[1]
    return pl.pallas_call(
        functools.partial(_route_body, nblk_p, nblk),
        grid=(nblk,),
        in_specs=[pl.BlockSpec((N_EXPERTS, tb), lambda i: (0, jnp.minimum(i, nblk_p - 1))),
                  pl.BlockSpec((N_EXPERTS, tb), lambda i: (0, jnp.maximum(i - nblk_p, 0)))],
        out_specs=[pl.BlockSpec((tb, LANES), lambda i: (i, 0)),
                   pl.BlockSpec((8, n), lambda i: (0, 0)),
                   pl.BlockSpec((N_EXPERTS, LANES), lambda i: (0, 0))],
        out_shape=[jax.ShapeDtypeStruct((n, LANES), F32),
                   jax.ShapeDtypeStruct((8, n), jnp.int32),
                   jax.ShapeDtypeStruct((N_EXPERTS, LANES), jnp.int32)],
        scratch_shapes=[pltpu.VMEM((nblk, 8, tb), F32), pltpu.VMEM((nblk, 8, tb), F32),
                        pltpu.VMEM((N_EXPERTS, 1), F32)],
        compiler_params=_params(("arbitrary",)),
        name="route",
    )(logits_p, logits_s)


def _sc_mesh():
    return plsc.VectorSubcoreMesh(core_axis_name="c", subcore_axis_name="s")


def _sc_worker():
    return lax.axis_index("s") * SC_CORES + lax.axis_index("c")


def _sc_dispatch(x_p, x_s, dest_t, n_slots):
    chunk = SC_ROWS
    n_p, n_s = x_p.shape[0], x_s.shape[0]
    per_w = n_p // (SC_WORKERS * chunk)
    ns_chunks = n_s // chunk
    assert per_w * SC_WORKERS * chunk == n_p and per_w % 2 == 0
    assert ns_chunks * chunk == n_s and ns_chunks <= SC_WORKERS
    d3 = dest_t.reshape(dest_t.shape[0], (n_p + n_s) // chunk, chunk)
    width, dtype = x_p.shape[1], x_p.dtype

    @functools.partial(
        pl.kernel, mesh=_sc_mesh(),
        out_type=jax.ShapeDtypeStruct((n_slots, width), dtype),
        scratch_types=[pltpu.VMEM((TOP_K, per_w, chunk), jnp.int32),
                       pltpu.VMEM((TOP_K, 1, chunk), jnp.int32),
                       pltpu.VMEM((2, chunk, width), dtype),
                       pltpu.SemaphoreType.DMA, pltpu.SemaphoreType.DMA],
        compiler_params=pltpu.CompilerParams(use_tc_tiling_on_sc=True),
        name="dispatch")
    def run(xp_hbm, xs_hbm, d_hbm, out_hbm, ip_v, is_v, rows_v, rsem, wsem):
        wid = _sc_worker()
        pltpu.sync_copy(d_hbm.at[pl.ds(0, TOP_K), pl.ds(wid * per_w, per_w)], ip_v)

        def read(j, slot):
            return pltpu.make_async_copy(xp_hbm.at[pl.ds((wid * per_w + j) * chunk, chunk)], rows_v.at[slot], rsem)

        def scatter(idx_v, j, slot):
            copies = [pltpu.async_copy(rows_v.at[slot], out_hbm.at[idx_v.at[kx, j]], wsem) for kx in range(TOP_K)]
            for cp in copies:
                cp.wait()

        read(0, 0).start()

        def body(h, carry):
            j = 2 * h
            read(j, 0).wait()
            read(j + 1, 1).start()
            scatter(ip_v, j, 0)
            read(j + 1, 1).wait()

            @pl.when(j + 2 < per_w)
            def _():
                read(j + 2, 0).start()

            scatter(ip_v, j + 1, 1)
            return carry

        lax.fori_loop(0, per_w // 2, body, 0)

        @pl.when(wid < ns_chunks)
        def _():
            pltpu.sync_copy(d_hbm.at[pl.ds(0, TOP_K), pl.ds(n_p // chunk + wid, 1)], is_v)
            pltpu.sync_copy(xs_hbm.at[pl.ds(wid * chunk, chunk)], rows_v.at[0])
            scatter(is_v, 0, 0)

    return run(x_p, x_s, d3)


def _sc_collect(y_sorted, dest_t, n_p, n_s):
    chunk = SC_ROWS
    per_choice = SC_WORKERS // TOP_K
    per_w = n_p // (per_choice * chunk)
    assert per_w * per_choice * chunk == n_p and per_w % 2 == 0
    assert n_s == per_choice * chunk
    d3 = dest_t.reshape(dest_t.shape[0], (n_p + n_s) // chunk, chunk)
    width, dtype = y_sorted.shape[1], y_sorted.dtype

    @functools.partial(
        pl.kernel, mesh=_sc_mesh(),
        out_type=[jax.ShapeDtypeStruct((TOP_K * n_p, width), dtype), jax.ShapeDtypeStruct((TOP_K * n_s, width), dtype)],
        scratch_types=[pltpu.VMEM((per_w, chunk), jnp.int32),
                       pltpu.VMEM((1, chunk), jnp.int32),
                       pltpu.VMEM((2, chunk, width), dtype),
                       pltpu.SemaphoreType.DMA, pltpu.SemaphoreType.DMA],
        compiler_params=pltpu.CompilerParams(use_tc_tiling_on_sc=True),
        name="collect")
    def run(y_hbm, d_hbm, op_hbm, os_hbm, ip_v, is_v, rows_v, gsem, wsem):
        wid = _sc_worker()
        choice = wid // per_choice
        part = wid % per_choice
        pltpu.sync_copy(d_hbm.at[choice, pl.ds(part * per_w, per_w)], ip_v)
        pltpu.sync_copy(d_hbm.at[choice, pl.ds(n_p // chunk + part, 1)], is_v)

        def gather(idx_v, j, slot):
            return pltpu.make_async_copy(y_hbm.at[idx_v.at[j]], rows_v.at[slot], gsem)

        def write(j, slot):
            return pltpu.make_async_copy(rows_v.at[slot], op_hbm.at[pl.ds((wid * per_w + j) * chunk, chunk)], wsem)

        gather(ip_v, 0, 0).start()

        def body(h, carry):
            j = 2 * h
            gather(ip_v, j, 0).wait()

            @pl.when(h > 0)
            def _():
                write(j - 1, 1).wait()

            gather(ip_v, j + 1, 1).start()
            write(j, 0).start()
            gather(ip_v, j + 1, 1).wait()
            write(j, 0).wait()

            @pl.when(j + 2 < per_w)
            def _():
                gather(ip_v, j + 2, 0).start()

            write(j + 1, 1).start()
            return carry

        lax.fori_loop(0, per_w // 2, body, 0)
        write(per_w - 1, 1).wait()

        gather(is_v, 0, 0).start()
        gather(is_v, 0, 0).wait()
        pltpu.sync_copy(rows_v.at[0], os_hbm.at[pl.ds(wid * chunk, chunk)])

    return run(y_sorted, d3)


def _expert_body(blk0_ref, nblk_ref, cnt_ref, wup_hbm, wdn_hbm, bup_ref, bdn_ref, x_hbm, y_hbm,
                 wup_f, wdn_f, wup_s, wdn_s, xbuf, obuf, w_sem, in_sem, out_sem):
    e = pl.program_id(0)
    nb = nblk_ref[e]
    blk0 = blk0_ref[e]
    cnt = cnt_ref[e]
    tm = EXPERT_ROWS
    pair = 2 * LANES
    wslot = e % 2
    up_rows = D_MODEL // W_PIECES
    dn_rows = D_MODEL // (W_PIECES // 2)

    def w_piece(hbm, buf, ex, slot, p, rows):
        start = p * rows if isinstance(p, int) else pl.multiple_of(p * rows, rows)
        r = pl.ds(start, rows)
        return pltpu.make_async_copy(hbm.at[ex, r], buf.at[slot, r], w_sem.at[slot])

    def w_start(ex, slot, p):
        w_piece(wup_hbm, wup_f, ex, slot, p, up_rows).start()
        if isinstance(p, int):
            if p < W_PIECES // 2:
                w_piece(wdn_hbm, wdn_f, ex, slot, p, dn_rows).start()
        else:
            @pl.when(p < W_PIECES // 2)
            def _():
                w_piece(wdn_hbm, wdn_f, ex, slot, p, dn_rows).start()

    def w_wait(ex, slot):
        for p in range(W_PIECES):
            w_piece(wup_hbm, wup_f, ex, slot, p, up_rows).wait()
        for p in range(W_PIECES // 2):
            w_piece(wdn_hbm, wdn_f, ex, slot, p, dn_rows).wait()

    @pl.when(e == 0)
    def _():
        for p in range(W_PIECES):
            w_start(0, 0, p)

    w_wait(e, wslot)
    wup_ref = wup_f.at[wslot]
    wdn_ref = wdn_f.at[wslot]
    more = e + 1 < N_EXPERTS

    def x_copy(i, slot):
        rows = pl.ds(pl.multiple_of((blk0 + i) * tm, tm), tm)
        return pltpu.make_async_copy(x_hbm.at[rows], xbuf.at[slot], in_sem.at[slot])

    def y_copy(i, slot):
        rows = pl.ds(pl.multiple_of((blk0 + i) * tm, tm), tm)
        return pltpu.make_async_copy(obuf.at[slot], y_hbm.at[rows], out_sem.at[slot])

    @pl.when(nb > 0)
    def _():
        x_copy(0, 0).start(priority=1)
        r = lax.broadcasted_iota(jnp.int32, (pair, pair), 0)
        c = lax.broadcasted_iota(jnp.int32, (pair, pair), 1)
        perm = (r == jnp.where(c < LANES, 2 * c, 2 * (c - LANES) + 1)).astype(BF16)
        for g in range(2 * D_MODEL // pair):
            cols = slice(g * pair, (g + 1) * pair)
            wup_s[g] = jnp.dot(wup_ref[:, cols].astype(BF16), perm, preferred_element_type=F32).astype(BF16)
        for g in range(D_MODEL // pair):
            wdn_s[g] = wdn_ref[:, g * pair:(g + 1) * pair].astype(BF16)

        def block(i, carry):
            slot = i % 2
            x_copy(i, slot).wait()

            @pl.when(i + 1 < nb)
            def _():
                x_copy(i + 1, 1 - slot).start(priority=1)

            @pl.when(i >= 2)
            def _():
                y_copy(i - 2, slot).wait()

            @pl.when(jnp.logical_and(more, i < W_PIECES))
            def _():
                w_start(e + 1, 1 - wslot, i)

            row = lax.broadcasted_iota(jnp.int32, (tm, 1), 0)
            x = jnp.where(row < cnt - i * tm, _unpack_bf16_pair(xbuf[slot]), 0.0).astype(BF16)
            acts = []
            for g in range(2 * D_MODEL // pair):
                cols = slice(g * pair, (g + 1) * pair)
                h = jnp.dot(x, wup_s[g], preferred_element_type=F32) + bup_ref[0, :, cols]
                glu = jnp.minimum(h[:, :LANES], SWIGLU_LIMIT)
                lin = jnp.clip(h[:, LANES:], -SWIGLU_LIMIT, SWIGLU_LIMIT)
                acts.append((glu * jax.nn.sigmoid(SWIGLU_ALPHA * glu) * (lin + 1.0)).astype(BF16))
            act = jnp.concatenate(acts, axis=1)
            half_groups = D_MODEL // pair // 2
            for g in range(half_groups):
                ys = []
                for gg in (g, g + half_groups):
                    cols = slice(gg * pair, (gg + 1) * pair)
                    ys.append(jnp.dot(act, wdn_s[gg], preferred_element_type=F32) + bdn_ref[0, :, cols])
                obuf[slot, :, g * pair:(g + 1) * pair] = _pack_bf16_pair(ys[0], ys[1])
            y_copy(i, slot).start(priority=1)
            return carry

        lax.fori_loop(0, nb, block, 0)

        @pl.when(nb >= 2)
        def _():
            y_copy(nb - 2, nb % 2).wait()

        y_copy(nb - 1, (nb - 1) % 2).wait()

    for p in range(W_PIECES):
        @pl.when(jnp.logical_and(more, p >= nb))
        def _():
            w_start(e + 1, 1 - wslot, p)


def _experts(blk0, nblk, cnt, x_sorted, w_up, w_down, b_up_grouped, b_down):
    tm = EXPERT_ROWS
    per_expert = lambda a: pl.BlockSpec((1,) + a.shape[1:], lambda e, b0, nb, ct: (e, 0, 0))
    hbm = pl.BlockSpec(memory_space=pl.ANY)
    grid_spec = pltpu.PrefetchScalarGridSpec(
        num_scalar_prefetch=3,
        grid=(N_EXPERTS,),
        in_specs=[hbm, hbm, per_expert(b_up_grouped), per_expert(b_down), hbm],
        out_specs=hbm,
        scratch_shapes=[pltpu.VMEM((2,) + w_up.shape[1:], F32), pltpu.VMEM((2,) + w_down.shape[1:], F32),
                        pltpu.VMEM((2 * D_MODEL // (2 * LANES), D_MODEL, 2 * LANES), BF16),
                        pltpu.VMEM((D_MODEL // (2 * LANES), D_MODEL, 2 * LANES), BF16),
                        pltpu.VMEM((2, tm, x_sorted.shape[1]), x_sorted.dtype),
                        pltpu.VMEM((2, tm, D_MODEL // 2), jnp.uint32),
                        pltpu.SemaphoreType.DMA((2,)), pltpu.SemaphoreType.DMA((2,)), pltpu.SemaphoreType.DMA((2,))],
    )
    return pl.pallas_call(
        _expert_body,
        grid_spec=grid_spec,
        out_shape=jax.ShapeDtypeStruct((x_sorted.shape[0], D_MODEL // 2), jnp.uint32),
        compiler_params=_params(("arbitrary",)),
        name="experts",
    )(blk0, nblk, cnt, w_up, w_down, b_up_grouped, b_down, x_sorted)


def _combine_body(gate_ref, xmid_ref, gfin_ref, y0_ref, y1_ref, y2_ref, y3_ref, out_ref):
    gate = gate_ref[...]
    moe = _unpack_bf16_pair(y0_ref[...]) * gate[:, 0:1]
    for kx, y_ref in enumerate((y1_ref, y2_ref, y3_ref), start=1):
        moe = moe + _unpack_bf16_pair(y_ref[...]) * gate[:, kx:kx + 1]
    out_ref[...] = _rms(xmid_ref[...] + moe, gfin_ref[...])


def _combine(gates, first_token, xmid, gfin, y_rows):
    n = xmid.shape[0]
    tt = min(n, COMBINE_ROWS)
    nblk = n // tt
    blk0 = first_token // tt
    assert blk0 * tt == first_token
    choice = lambda kx: pl.BlockSpec((tt, y_rows.shape[1]), lambda i: (i + kx * nblk, 0))
    return pl.pallas_call(
        _combine_body,
        grid=(nblk,),
        in_specs=[pl.BlockSpec((tt, LANES), lambda i: (i + blk0, 0)),
                  pl.BlockSpec((tt, D_MODEL), lambda i: (i, 0)),
                  pl.BlockSpec((1, D_MODEL), lambda i: (0, 0))] + [choice(kx) for kx in range(TOP_K)],
        out_specs=pl.BlockSpec((tt, D_MODEL), lambda i: (i, 0)),
        out_shape=jax.ShapeDtypeStruct((n, D_MODEL), F32),
        compiler_params=_params(("arbitrary",)),
        name="combine",
    )(gates, xmid, gfin, y_rows, y_rows, y_rows, y_rows)


def kernel(x_prompt, x_sample, cache_k_win, cache_v_win, norm_attn_g, w_in, ln_v_g, ln_v_b, w_spatial, b_spatial,
           attn_sinks, w_out, norm_ffn_g, w_router, b_router, w_up, b_up, w_down, b_down, norm_final_g):
    bp, tp, _ = x_prompt.shape
    bs, ts, _ = x_sample.shape
    w_buf = cache_k_win.shape[2]
    assert bp == 1 and tp % MIX_ROWS == 0 and w_buf == WINDOW and (bs * ts) % PROJ_ROWS == 0 and 8 % ts == 0
    n_p, n_s = bp * tp, bs * ts
    row2 = lambda a: a.reshape(1, -1)

    w_in_bf = w_in[0].astype(BF16)
    w_out_bf = w_out[0].astype(BF16)
    tril = jnp.tril(jnp.ones((CHUNK, CHUNK), dtype=bool))
    wsp = jnp.where(tril[None], w_spatial[0], 0.0)
    wsp_bf = wsp.astype(BF16)
    bsp = jnp.broadcast_to(b_spatial[0][:, :, None], (GMLP_GROUPS, CHUNK, LANES))
    b_up_grouped = b_up[0].reshape(N_EXPERTS, -1, LANES, 2).transpose(0, 1, 3, 2).reshape(N_EXPERTS, 1, -1)
    bd = b_down[0][:, None, :]
    g1, g2, gfin = row2(norm_attn_g[0]), row2(norm_ffn_g[0]), row2(norm_final_g)
    lng, lnb = row2(ln_v_g[0]), row2(ln_v_b[0])
    wr_hi = w_router[0].astype(BF16)
    wr = jnp.concatenate([wr_hi, (w_router[0] - wr_hi.astype(F32)).astype(BF16)], axis=1)
    br = row2(b_router[0])
    sinks = attn_sinks[0]

    xp = x_prompt.reshape(n_p, D_MODEL)
    cs_p = _rotary_inputs(jnp.arange(tp, dtype=jnp.int32))
    q_p, k_p, v_p, a_p, vn_p, sgb_p = _proj(xp, g1, w_in_bf, cs_p, lng, lnb)
    xmid_p, xn2_p, lg_p = _mix(sinks, q_p, k_p, v_p, a_p, vn_p, sgb_p, xp, wsp_bf, bsp, w_out_bf, g2, wr, br)

    xs = x_sample.reshape(n_s, D_MODEL)
    pos_s = PAST_LEN + jnp.arange(ts, dtype=jnp.int32)
    cs_s = _rotary_inputs(jnp.tile(pos_s, bs))
    q_s, k_s, v_s, a_s, vn_s, sgb_s = _proj(xs, g1, w_in_bf, cs_s, lng, lnb)
    n_keys = w_buf + ts
    key_pad = jnp.zeros((bs, (-n_keys) % 8, KV_WIDTH), F32)
    with_new = lambda cache, new: jnp.concatenate(
        [cache[0].reshape(bs, w_buf, KV_WIDTH), new.reshape(bs, ts, KV_WIDTH), key_pad], axis=1)
    k_all = with_new(cache_k_win, k_s)
    v_all = with_new(cache_v_win, v_s)
    sink_col = jnp.tile(sinks, ts).reshape(ts * N_HEADS, 1)
    o_s = _sample_attn(sink_col, q_s.reshape(bs, ts * N_HEADS, HEAD_DIM), k_all, v_all).reshape(n_s, Q_WIDTH)
    lag = np.arange(ts)[:, None] - np.arange(ts)[None, :]
    coef = jnp.stack([jnp.sum(jnp.where(lag == d, wsp[:, :ts, :ts], 0.0), axis=2)
                      for d in range(ts)])
    coef = jnp.repeat(coef.transpose(0, 2, 1), GMLP_WIDTH // GMLP_GROUPS, axis=2)
    coef = jnp.tile(coef, (1, 8 // ts, 1))
    bias = jnp.tile(jnp.repeat(b_spatial[0][:, :ts].T, GMLP_WIDTH // GMLP_GROUPS, axis=1), (8 // ts, 1))
    xmid_s, xn2_s, lg_s = _mix_sample(a_s, vn_s, sgb_s, o_s, xs, coef, bias, w_out_bf, g2, wr, br)

    n_tok = n_p + n_s
    gate_w, dest_t, meta = _route(lg_p, lg_s)
    tm = EXPERT_ROWS
    n_blocks = (n_tok * TOP_K) // tm + N_EXPERTS

    x_sorted = _sc_dispatch(xn2_p, xn2_s, dest_t, n_blocks * tm)
    y_sorted = _experts(meta[:, 0], meta[:, 1], meta[:, 2], x_sorted, w_up[0], w_down[0], b_up_grouped, bd)
    yrows_p, yrows_s = _sc_collect(y_sorted, dest_t, n_p, n_s)
    y_p = _combine(gate_w, 0, xmid_p, gfin, yrows_p)
    y_s = _combine(gate_w, n_p, xmid_s, gfin, yrows_s)

    k4 = lambda t: t.reshape(1, bp, -1, N_KV_HEADS, HEAD_DIM)
    return (y_p.reshape(bp, tp, D_MODEL),
            y_s.reshape(bs, ts, D_MODEL),
            k4(k_p[n_p - WINDOW:]),
            k4(v_p[n_p - WINDOW:]),
            vn_p[n_p - CHUNK:].reshape(1, bp, CHUNK, GMLP_WIDTH),
            k_all[:, ts:n_keys].reshape(1, bs, w_buf, N_KV_HEADS, HEAD_DIM),
            v_all[:, ts:n_keys].reshape(1, bs, w_buf, N_KV_HEADS, HEAD_DIM),
            vn_s.reshape(1, bs, ts, GMLP_WIDTH))
```

```python
import functools

import numpy as np
import jax
import jax.numpy as jnp
from jax import lax
from jax.experimental import pallas as pl
from jax.experimental.pallas import tpu as pltpu
from jax.experimental.pallas import tpu_sc as plsc

F32 = jnp.float32
BF16 = jnp.bfloat16

D_MODEL = 1024
HEAD_DIM = 64
N_HEADS = 16
GQA_GROUP = 8
N_KV_HEADS = 2
Q_WIDTH = 1024
KV_WIDTH = 128
WINDOW = 128
ROT_DIM = 16
ROPE_THETA = 500000.0
CHUNK = 128
GMLP_WIDTH = 1024
GMLP_GROUPS = 8
N_EXPERTS = 32
TOP_K = 4
SWIGLU_LIMIT = 7.0
SWIGLU_ALPHA = 1.702
RMS_EPS = 1e-5
LN_EPS = 1e-5
NEG_INF = -1e30
PAST_LEN = 16384

LANES = 128
VMEM_LIMIT = 56 * 1024 * 1024

PROJ_ROWS = 256
MIX_ROWS = 512
ROUTE_ROWS = 512
EXPERT_ROWS = 256
W_PIECES = 8
COMBINE_ROWS = 1024

SC_CORES = 2
SC_WORKERS = 32
SC_ROWS = 64

_C_Q, _C_KV, _C_U, _C_VG, _C_GA, _C_GB, _C_END = 0, 1024, 1280, 2304, 3328, 4352, 5376


def _params(sem):
    return pltpu.CompilerParams(dimension_semantics=sem, vmem_limit_bytes=VMEM_LIMIT)


def _rms(x, g):
    return x * lax.rsqrt(jnp.mean(x * x, axis=-1, keepdims=True) + RMS_EPS) * g


def _pack_bf16_pair(lo, hi):
    lo_bits = lax.bitcast_convert_type(lo.astype(BF16).astype(F32), jnp.uint32)
    hi_bits = lax.bitcast_convert_type(hi.astype(BF16).astype(F32), jnp.uint32)
    return (lo_bits >> 16) | hi_bits


def _unpack_bf16_pair(words):
    lo = lax.bitcast_convert_type(words << 16, F32)
    hi = lax.bitcast_convert_type(words & jnp.uint32(0xFFFF0000), F32)
    return jnp.concatenate([lo, hi], axis=1)


def _proj_body(x_ref, g_ref, w_ref, cs_ref, rot_ref, lng_ref, lnb_ref,
               q_ref, k_ref, v_ref, a_ref, vn_ref, sgb_ref):
    h = _rms(x_ref[...], g_ref[...]).astype(BF16)
    tabs = lax.dot_general(cs_ref[...], rot_ref[...], (((0,), (0,)), ((), ())), preferred_element_type=F32)
    rc, rs1, rs2 = tabs[:, :LANES], tabs[:, LANES:2 * LANES], tabs[:, 2 * LANES:]

    def rot(z):
        return z * rc + pltpu.roll(z, LANES - ROT_DIM // 2, 1) * rs1 + pltpu.roll(z, ROT_DIM // 2, 1) * rs2

    def mm(lo, hi):
        return jnp.dot(h, w_ref[:, lo:hi], preferred_element_type=F32)

    zq = mm(_C_Q, _C_KV)
    for c in range(Q_WIDTH // LANES):
        sl = slice(c * LANES, (c + 1) * LANES)
        q_ref[:, sl] = (rot(zq[:, sl]) * (HEAD_DIM ** -0.5)).astype(BF16)
    zkv = mm(_C_KV, _C_U)
    k_ref[...] = rot(zkv[:, :KV_WIDTH])
    v_ref[...] = zkv[:, KV_WIDTH:]
    a_ref[...] = jax.nn.sigmoid(mm(_C_GA, _C_GB)) * jax.nn.gelu(mm(_C_U, _C_VG))
    zv = jax.nn.gelu(mm(_C_VG, _C_GA))
    zc = zv - jnp.mean(zv, axis=-1, keepdims=True)
    var = jnp.mean(zc * zc, axis=-1, keepdims=True)
    vn_ref[...] = zc * lax.rsqrt(var + LN_EPS) * lng_ref[...] + lnb_ref[...]
    sgb_ref[...] = jax.nn.sigmoid(mm(_C_GB, _C_END))


def _proj(x, norm_g, w_in_bf, cs, ln_g, ln_b):
    n = x.shape[0]
    tm = PROJ_ROWS
    row = lambda w: pl.BlockSpec((tm, w), lambda i: (i, 0))
    full = lambda a: pl.BlockSpec(a.shape, lambda i: (0,) * a.ndim)
    rot = jnp.asarray(np.tile(_ROT_EXPAND, (3, 1)), dtype=BF16)
    return pl.pallas_call(
        _proj_body,
        grid=(n // tm,),
        in_specs=[row(D_MODEL), full(norm_g), full(w_in_bf), pl.BlockSpec((cs.shape[0], tm), lambda i: (0, i)),
                  full(rot),
                  full(ln_g), full(ln_b)],
        out_specs=[row(Q_WIDTH), row(KV_WIDTH), row(KV_WIDTH), row(GMLP_WIDTH), row(GMLP_WIDTH), row(D_MODEL)],
        out_shape=[jax.ShapeDtypeStruct((n, Q_WIDTH), BF16),
                   jax.ShapeDtypeStruct((n, KV_WIDTH), F32),
                   jax.ShapeDtypeStruct((n, KV_WIDTH), F32),
                   jax.ShapeDtypeStruct((n, GMLP_WIDTH), F32),
                   jax.ShapeDtypeStruct((n, GMLP_WIDTH), F32),
                   jax.ShapeDtypeStruct((n, D_MODEL), F32)],
        compiler_params=_params(("arbitrary",)),
        name="proj",
    )(x, norm_g, w_in_bf, cs, rot, ln_g, ln_b)


_ROT_COLS = 32


def _rot_expand():
    half = ROT_DIM // 2
    m = np.zeros((_ROT_COLS, 3 * LANES), np.float32)
    for lane in range(LANES):
        d = lane % HEAD_DIM
        if d < ROT_DIM:
            m[d % half, lane] = 1.0
        else:
            m[2 * half, lane] = 1.0
        if d < half:
            m[half + d, LANES + lane] = -1.0
        elif d < ROT_DIM:
            m[half + d - half, 2 * LANES + lane] = 1.0
    return m


_ROT_EXPAND = _rot_expand()


def _rotary_inputs(pos):
    half = ROT_DIM // 2
    inv_freq = ROPE_THETA ** (-jnp.arange(half, dtype=F32) / half)
    ang = inv_freq[:, None] * pos.astype(F32)[None, :]
    n = pos.shape[0]
    cs = jnp.concatenate([jnp.cos(ang), jnp.sin(ang), jnp.ones((1, n), F32),
                          jnp.zeros((_ROT_COLS - 2 * half - 1, n), F32)], axis=0)
    hi = cs.astype(BF16)
    rest = cs - hi.astype(F32)
    mid = rest.astype(BF16)
    lo = (rest - mid.astype(F32)).astype(BF16)
    return jnp.concatenate([hi, mid, lo], axis=0)


def _finish_rows(merged_bf, x, wout_ref, g2_ref, wr_ref, br_ref, xmid_ref, xn2_ref, lg_ref):
    xm = x + jnp.dot(merged_bf, wout_ref[...], preferred_element_type=F32)
    xmid_ref[...] = xm
    xn = _rms(xm, g2_ref[...])
    x_hi = xn.astype(BF16)
    x_lo = (xn - x_hi.astype(F32)).astype(BF16)
    w_hl = wr_ref[...]
    p_hi = jnp.dot(x_hi, w_hl, preferred_element_type=F32)
    p_lo = jnp.dot(x_lo, w_hl[:, :N_EXPERTS], preferred_element_type=F32)
    lg = p_hi[:, :N_EXPERTS] + (p_hi[:, N_EXPERTS:] + p_lo) + br_ref[...]
    wide = jnp.concatenate([lg, jnp.zeros((lg.shape[0], LANES - N_EXPERTS), F32)], axis=1)
    lg_ref[...] = wide.T[:N_EXPERTS]
    xn2_ref[...] = _pack_bf16_pair(xn[:, :D_MODEL // 2], xn[:, D_MODEL // 2:])


def _mix_body(sinks_ref, q_ref, k_ref, kp_ref, v_ref, vp_ref, a_ref, vn_ref, sgb_ref, x_ref,
              wsp_ref, bsp_ref, wout_ref, g2_ref, wr_ref, br_ref,
              xmid_ref, xn2_ref, lg_ref, kcat, vcat, mrg, key_s):
    i = pl.program_id(0)
    nsub = MIX_ROWS // WINDOW
    kcat[0:WINDOW] = kp_ref[...]
    kcat[WINDOW:] = k_ref[...]
    vcat[0:WINDOW] = vp_ref[...]
    vcat[WINDOW:] = v_ref[...]

    pair_rows = (GQA_GROUP // 2) * WINDOW
    lane_kv = lax.broadcasted_iota(jnp.int32, (2 * WINDOW, LANES), 1)
    lane_o = lax.broadcasted_iota(jnp.int32, (pair_rows, LANES), 1)

    @pl.when(i == 0)
    def _():
        rq = lax.broadcasted_iota(jnp.int32, (pair_rows, 4 * WINDOW), 0) & (WINDOW - 1)
        ck = lax.broadcasted_iota(jnp.int32, (pair_rows, 4 * WINDOW), 1) & (2 * WINDOW - 1)
        key_s[...] = jnp.where((ck > rq) & (ck <= rq + WINDOW), ck, -1)

    row_p = lax.broadcasted_iota(jnp.int32, (pair_rows, 1), 0) >> 7
    sink_cols = []
    for kk in range(N_KV_HEADS):
        h0 = kk * GQA_GROUP
        se = jnp.full((pair_rows, 1), sinks_ref[h0], F32)
        so = jnp.full((pair_rows, 1), sinks_ref[h0 + 1], F32)
        for p in range(1, GQA_GROUP // 2):
            se = jnp.where(row_p == p, sinks_ref[h0 + 2 * p], se)
            so = jnp.where(row_p == p, sinks_ref[h0 + 2 * p + 1], so)
        sink_cols.append((se, so))

    def sub(j, carry):
        off = pl.multiple_of(j * WINDOW, WINDOW)
        rows = pl.ds(off, WINDOW)
        for g in range(GMLP_GROUPS):
            cols = slice(g * LANES, (g + 1) * LANES)
            s = jnp.dot(wsp_ref[g], vn_ref[rows, cols].astype(BF16), preferred_element_type=F32) + bsp_ref[g]
            mrg[rows, cols] = a_ref[rows, cols] * s
        kblk = kcat[pl.ds(off, 2 * WINDOW), :]
        vblk = vcat[pl.ds(off, 2 * WINDOW), :]
        kswp = pltpu.roll(kblk, HEAD_DIM, 1)
        vswp = pltpu.roll(vblk, HEAD_DIM, 1)
        kmin = jnp.where(jnp.logical_and(i == 0, j == 0), WINDOW, 0)
        allowed = key_s[...] >= kmin
        for kk in range(N_KV_HEADS):
            lo_src, hi_src = (kblk, kswp) if kk == 0 else (kswp, kblk)
            kbd = jnp.concatenate([jnp.where(lane_kv < HEAD_DIM, lo_src, 0.0),
                                   jnp.where(lane_kv >= HEAD_DIM, hi_src, 0.0)], axis=0).astype(BF16)
            lo_src, hi_src = (vblk, vswp) if kk == 0 else (vswp, vblk)
            vbd = jnp.concatenate([jnp.where(lane_kv < HEAD_DIM, lo_src, 0.0),
                                   jnp.where(lane_kv >= HEAD_DIM, hi_src, 0.0)], axis=0).astype(BF16)
            pair0 = kk * (GQA_GROUP // 2)
            qs = jnp.concatenate([q_ref[rows, (pair0 + p) * LANES:(pair0 + p + 1) * LANES]
                                  for p in range(GQA_GROUP // 2)], axis=0)
            lg = lax.dot_general(qs, kbd, (((1,), (1,)), ((), ())), preferred_element_type=F32)
            lg = jnp.where(allowed, lg, NEG_INF)
            se, so = sink_cols[kk]
            le, lo = lg[:, :2 * WINDOW], lg[:, 2 * WINDOW:]
            me = jnp.maximum(jnp.max(le, axis=1, keepdims=True), se)
            mo = jnp.maximum(jnp.max(lo, axis=1, keepdims=True), so)
            pe = jnp.exp(le - me)
            po = jnp.exp(lo - mo)
            de = jnp.sum(pe, axis=1, keepdims=True) + jnp.exp(se - me)
            do = jnp.sum(po, axis=1, keepdims=True) + jnp.exp(so - mo)
            pr = jnp.concatenate([pe, po], axis=1).astype(BF16)
            o = jnp.dot(pr, vbd, preferred_element_type=F32)
            o = o / jnp.where(lane_o < HEAD_DIM, de, do)
            for p in range(GQA_GROUP // 2):
                cols = slice((pair0 + p) * LANES, (pair0 + p + 1) * LANES)
                mrg[rows, cols] += sgb_ref[rows, cols] * o[p * WINDOW:(p + 1) * WINDOW]
        return carry

    lax.fori_loop(0, nsub, sub, 0)
    _finish_rows(mrg[...].astype(BF16), x_ref[...], wout_ref, g2_ref, wr_ref, br_ref, xmid_ref, xn2_ref, lg_ref)


def _mix(sinks, q, k, v, a, vn, sgb, x, wsp, bsp, wout, g2, wr, br):
    n = x.shape[0]
    tm = MIX_ROWS
    nsub = tm // WINDOW
    row = lambda w: pl.BlockSpec((tm, w), lambda i: (i, 0))
    prev = pl.BlockSpec((WINDOW, KV_WIDTH), lambda i: (jnp.maximum(i * nsub - 1, 0), 0))
    full = lambda arr: pl.BlockSpec(arr.shape, lambda i: (0,) * arr.ndim)
    smem = pl.BlockSpec(memory_space=pltpu.SMEM)
    return pl.pallas_call(
        _mix_body,
        grid=(n // tm,),
        in_specs=[smem, row(Q_WIDTH), row(KV_WIDTH), prev, row(KV_WIDTH), prev,
                  row(GMLP_WIDTH), row(GMLP_WIDTH), row(D_MODEL), row(D_MODEL),
                  full(wsp), full(bsp), full(wout), full(g2), full(wr), full(br)],
        out_specs=[row(D_MODEL), row(D_MODEL // 2), pl.BlockSpec((N_EXPERTS, tm), lambda i: (0, i))],
        out_shape=[jax.ShapeDtypeStruct((n, D_MODEL), F32),
                   jax.ShapeDtypeStruct((n, D_MODEL // 2), jnp.uint32),
                   jax.ShapeDtypeStruct((N_EXPERTS, n), F32)],
        scratch_shapes=[pltpu.VMEM((tm + WINDOW, KV_WIDTH), F32),
                        pltpu.VMEM((tm + WINDOW, KV_WIDTH), F32),
                        pltpu.VMEM((tm, D_MODEL), F32),
                        pltpu.VMEM((GQA_GROUP // 2 * WINDOW, 4 * WINDOW), jnp.int32)],
        compiler_params=_params(("arbitrary",)),
        name="mix_prompt",
    )(sinks, q, k, k, v, v, a, vn, sgb, x, wsp, bsp, wout, g2, wr, br)


def _sample_attn_body(sink_ref, q_ref, k_ref, v_ref, o_ref):
    q = q_ref[...]
    k = k_ref[...]
    v = v_ref[...]
    nq, nk = q.shape[1], k.shape[1]
    heads = lambda t, kk: t[:, :, kk * HEAD_DIM:(kk + 1) * HEAD_DIM].astype(BF16)
    row = lax.broadcasted_iota(jnp.int32, (1, nq, nk), 1)
    first_kv = (row % N_HEADS) < GQA_GROUP
    lg = jnp.where(first_kv,
                   jnp.einsum("bqd,bkd->bqk", q, heads(k, 0), preferred_element_type=F32),
                   jnp.einsum("bqd,bkd->bqk", q, heads(k, 1), preferred_element_type=F32))
    t = row // N_HEADS
    j = lax.broadcasted_iota(jnp.int32, (1, nq, nk), 2)
    lg = jnp.where((j > t) & (j <= t + WINDOW), lg, NEG_INF)
    sink = sink_ref[...][None]
    m = jnp.maximum(jnp.max(lg, axis=2, keepdims=True), sink)
    p = jnp.exp(lg - m)
    den = jnp.sum(p, axis=2, keepdims=True) + jnp.exp(sink - m)
    pb = p.astype(BF16)
    row_o = lax.broadcasted_iota(jnp.int32, (1, nq, HEAD_DIM), 1)
    o = jnp.where((row_o % N_HEADS) < GQA_GROUP,
                  jnp.einsum("bqk,bkd->bqd", pb, heads(v, 0), preferred_element_type=F32),
                  jnp.einsum("bqk,bkd->bqd", pb, heads(v, 1), preferred_element_type=F32))
    o_ref[...] = o / den


def _sample_attn(sink_col, q3, k_all, v_all):
    nb = q3.shape[0]
    bb = 32
    blk = lambda a: pl.BlockSpec((bb,) + a.shape[1:], lambda b: (b, 0, 0))
    return pl.pallas_call(
        _sample_attn_body,
        grid=(nb // bb,),
        in_specs=[pl.BlockSpec(sink_col.shape, lambda b: (0, 0)), blk(q3), blk(k_all), blk(v_all)],
        out_specs=blk(q3),
        out_shape=jax.ShapeDtypeStruct(q3.shape, F32),
        compiler_params=_params(("arbitrary",)),
        name="attn_sample",
    )(sink_col, q3, k_all, v_all)


def _mix_sample_body(a_ref, vn_ref, sgb_ref, o_ref, x_ref, coef_ref, bias_ref,
                     wout_ref, g2_ref, wr_ref, br_ref, xmid_ref, xn2_ref, lg_ref):
    vn = vn_ref[...]
    n, width = vn.shape
    rows8 = lambda t: t.reshape(n // 8, 8, width)
    s = bias_ref[...][None] + coef_ref[0][None] * rows8(vn)
    for d in range(1, coef_ref.shape[0]):
        s = s + coef_ref[d][None] * rows8(pltpu.roll(vn, d, 0))
    merged = a_ref[...] * s.reshape(n, width) + sgb_ref[...] * o_ref[...]
    _finish_rows(merged.astype(BF16), x_ref[...], wout_ref, g2_ref, wr_ref, br_ref, xmid_ref, xn2_ref, lg_ref)


def _mix_sample(a, vn, sgb, o, x, coef, bias, wout, g2, wr, br):
    n = x.shape[0]
    args = (a, vn, sgb, o, x, coef, bias, wout, g2, wr, br)
    full = lambda arr: pl.BlockSpec(arr.shape, lambda i: (0,) * arr.ndim)
    return pl.pallas_call(
        _mix_sample_body,
        grid=(1,),
        in_specs=[full(arr) for arr in args],
        out_specs=[pl.BlockSpec((n, D_MODEL), lambda i: (0, 0)), pl.BlockSpec((n, D_MODEL // 2), lambda i: (0, 0)),
                   pl.BlockSpec((N_EXPERTS, n), lambda i: (0, 0))],
        out_shape=[jax.ShapeDtypeStruct((n, D_MODEL), F32),
                   jax.ShapeDtypeStruct((n, D_MODEL // 2), jnp.uint32),
                   jax.ShapeDtypeStruct((N_EXPERTS, n), F32)],
        compiler_params=_params(("arbitrary",)),
        name="mix_sample",
    )(*args)


def _rows8(rows, dtype):
    n = rows[0].shape[1]
    sub = lax.broadcasted_iota(jnp.int32, (8, n), 0)
    out = jnp.zeros((8, n), dtype)
    for kx, r in enumerate(rows):
        out = jnp.where(sub == kx, r.astype(dtype), out)
    return out


def _route_body(nblk, reserve, lg_ref, prior_ref, gate_ref, dest_ref, meta_ref, idx_s, rank_s, base):
    i = pl.program_id(0)

    @pl.when(i == 0)
    def _():
        base[...] = jnp.zeros_like(base)

    l = lg_ref[...]
    tb = l.shape[1]
    sub = lax.broadcasted_iota(jnp.int32, l.shape, 0).astype(F32)
    vals, idxs, sels = [], [], []
    for _ in range(TOP_K):
        m = jnp.max(l, axis=0, keepdims=True)
        ik = jnp.min(jnp.where(l == m, sub, float(N_EXPERTS)), axis=0, keepdims=True)
        sel = sub == ik
        l = jnp.where(sel, -jnp.inf, l)
        vals.append(m)
        idxs.append(ik)
        sels.append(sel)
    es = [jnp.exp(vk - vals[0]) for vk in vals]
    den = es[0] + es[1] + es[2] + es[3]
    onehot = jnp.zeros(l.shape, F32)
    for sel in sels:
        onehot = onehot + sel.astype(F32)
    earlier = (lax.broadcasted_iota(jnp.int32, (tb, tb), 0) < lax.broadcasted_iota(jnp.int32, (tb, tb), 1))
    before = jnp.dot(onehot.astype(BF16), earlier.astype(BF16), preferred_element_type=F32) + base[...]
    ranks = [jnp.sum(jnp.where(sel, before, 0.0), axis=0, keepdims=True) for sel in sels]
    base[...] += jnp.sum(onehot, axis=1, keepdims=True)
    idx_s[i] = _rows8(idxs, F32)
    rank_s[i] = _rows8(ranks, F32)
    gates = jnp.concatenate([_rows8([e / den for e in es], F32), jnp.zeros((LANES - 8, tb), F32)], axis=0)
    gate_ref[...] = gates.T

    @pl.when(i == nblk - 1)
    def _():
        cnt = base[...]
        if reserve is None:
            seg0 = prior_ref[:, 0:1]
            placed = prior_ref[:, 1:2]
        else:
            cap = jnp.ceil((cnt + reserve) / EXPERT_ROWS) * EXPERT_ROWS
            lower = (lax.broadcasted_iota(jnp.int32, (N_EXPERTS, N_EXPERTS), 1) <
                     lax.broadcasted_iota(jnp.int32, (N_EXPERTS, N_EXPERTS), 0)).astype(F32)
            seg0 = jnp.dot(lower, jnp.broadcast_to(cap, (N_EXPERTS, LANES)), preferred_element_type=F32,
                           precision=lax.Precision.HIGHEST)[:, :1]
            placed = jnp.zeros_like(cnt)
        total = placed + cnt
        lane = lax.broadcasted_iota(jnp.int32, (N_EXPERTS, LANES), 1)
        meta_ref[...] = jnp.where(lane == 0, seg0, jnp.where(lane == 1, total, jnp.where(
            lane == 2, seg0 / EXPERT_ROWS, jnp.where(lane == 3, jnp.ceil(total / EXPERT_ROWS), 0.0))))
        first = seg0 + placed
        sub_e = lax.broadcasted_iota(jnp.int32, (N_EXPERTS, tb), 0).astype(F32)
        for b in range(nblk):
            idx, rank = idx_s[b], rank_s[b]
            rows = [jnp.sum(jnp.where(sub_e == idx[kx:kx + 1], first, 0.0), axis=0, keepdims=True)
                    + rank[kx:kx + 1] for kx in range(TOP_K)]
            dest_ref[:, b * tb:(b + 1) * tb] = _rows8(rows, jnp.int32)


def _route(logits, prior, reserve):
    tb = ROUTE_ROWS
    n = logits.shape[1]
    nblk = n // tb
    assert nblk * tb == n
    table = pl.BlockSpec((N_EXPERTS, LANES), lambda i: (0, 0))
    return pl.pallas_call(
        functools.partial(_route_body, nblk, reserve),
        grid=(nblk,),
        in_specs=[pl.BlockSpec((N_EXPERTS, tb), lambda i: (0, i)), table],
        out_specs=[pl.BlockSpec((tb, LANES), lambda i: (i, 0)), pl.BlockSpec((8, n), lambda i: (0, 0)), table],
        out_shape=[jax.ShapeDtypeStruct((n, LANES), F32),
                   jax.ShapeDtypeStruct((8, n), jnp.int32),
                   jax.ShapeDtypeStruct((N_EXPERTS, LANES), F32)],
        scratch_shapes=[pltpu.VMEM((nblk, 8, tb), F32), pltpu.VMEM((nblk, 8, tb), F32),
                        pltpu.VMEM((N_EXPERTS, 1), F32)],
        compiler_params=_params(("arbitrary",)),
        name="route",
    )(logits, prior)


def _sc_mesh():
    return plsc.VectorSubcoreMesh(core_axis_name="c", subcore_axis_name="s")


def _sc_worker():
    return lax.axis_index("s") * SC_CORES + lax.axis_index("c")


def _sc_dispatch(x, dest_t, n_slots):
    chunk = SC_ROWS
    n = x.shape[0]
    per_w = n // (SC_WORKERS * chunk)
    assert per_w * SC_WORKERS * chunk == n and per_w % 2 == 0
    d3 = dest_t.reshape(dest_t.shape[0], n // chunk, chunk)
    width, dtype = x.shape[1], x.dtype

    @functools.partial(
        pl.kernel, mesh=_sc_mesh(),
        out_type=jax.ShapeDtypeStruct((n_slots, width), dtype),
        scratch_types=[pltpu.VMEM((TOP_K, per_w, chunk), jnp.int32),
                       pltpu.VMEM((2, chunk, width), dtype),
                       pltpu.SemaphoreType.DMA, pltpu.SemaphoreType.DMA],
        compiler_params=pltpu.CompilerParams(use_tc_tiling_on_sc=True),
        name="dispatch")
    def run(x_hbm, d_hbm, out_hbm, idx_v, rows_v, rsem, wsem):
        wid = _sc_worker()
        pltpu.sync_copy(d_hbm.at[pl.ds(0, TOP_K), pl.ds(wid * per_w, per_w)], idx_v)

        def read(j, slot):
            return pltpu.make_async_copy(x_hbm.at[pl.ds((wid * per_w + j) * chunk, chunk)], rows_v.at[slot], rsem)

        def scatter(j, slot):
            copies = [pltpu.async_copy(rows_v.at[slot], out_hbm.at[idx_v.at[kx, j]], wsem) for kx in range(TOP_K)]
            for cp in copies:
                cp.wait()

        read(0, 0).start()

        def body(h, carry):
            j = 2 * h
            read(j, 0).wait()
            read(j + 1, 1).start()
            scatter(j, 0)
            read(j + 1, 1).wait()

            @pl.when(j + 2 < per_w)
            def _():
                read(j + 2, 0).start()

            scatter(j + 1, 1)
            return carry

        lax.fori_loop(0, per_w // 2, body, 0)

    return run(x, d3)


def _dispatch_small_body(dest_ref, x_ref, _, out_hbm, sem):
    n = x_ref.shape[0]

    def body(t, carry):
        for kx in range(TOP_K):
            pltpu.make_async_copy(x_ref.at[pl.ds(t, 1)], out_hbm.at[pl.ds(dest_ref[kx, t], 1)], sem).start()
        return carry

    lax.fori_loop(0, n, body, 0, unroll=4)
    for _ in range(TOP_K):
        pltpu.make_async_copy(x_ref, out_hbm.at[pl.ds(0, n)], sem).wait()


def _dispatch_small(x, dest_t, x_sorted):
    return pl.pallas_call(
        _dispatch_small_body,
        grid=(1,),
        in_specs=[pl.BlockSpec(memory_space=pltpu.SMEM), pl.BlockSpec(x.shape, lambda i: (0, 0)),
                  pl.BlockSpec(memory_space=pl.ANY)],
        out_specs=pl.BlockSpec(memory_space=pl.ANY),
        out_shape=jax.ShapeDtypeStruct(x_sorted.shape, x_sorted.dtype),
        scratch_shapes=[pltpu.SemaphoreType.DMA],
        input_output_aliases={2: 0},
        compiler_params=_params(("arbitrary",)),
        name="dispatch_small",
    )(dest_t, x, x_sorted)


def _sc_collect(y_sorted, dest_t, n_p, n_s):
    chunk = SC_ROWS
    per_choice = SC_WORKERS // TOP_K
    per_w = n_p // (per_choice * chunk)
    assert per_w * per_choice * chunk == n_p and per_w % 2 == 0
    assert n_s == per_choice * chunk
    d3 = dest_t.reshape(dest_t.shape[0], (n_p + n_s) // chunk, chunk)
    width, dtype = y_sorted.shape[1], y_sorted.dtype

    @functools.partial(
        pl.kernel, mesh=_sc_mesh(),
        out_type=[jax.ShapeDtypeStruct((TOP_K * n_p, width), dtype), jax.ShapeDtypeStruct((TOP_K * n_s, width), dtype)],
        scratch_types=[pltpu.VMEM((per_w, chunk), jnp.int32),
                       pltpu.VMEM((1, chunk), jnp.int32),
                       pltpu.VMEM((2, chunk, width), dtype),
                       pltpu.SemaphoreType.DMA, pltpu.SemaphoreType.DMA],
        compiler_params=pltpu.CompilerParams(use_tc_tiling_on_sc=True),
        name="collect")
    def run(y_hbm, d_hbm, op_hbm, os_hbm, ip_v, is_v, rows_v, gsem, wsem):
        wid = _sc_worker()
        choice = wid // per_choice
        part = wid % per_choice
        pltpu.sync_copy(d_hbm.at[choice, pl.ds(part * per_w, per_w)], ip_v)
        pltpu.sync_copy(d_hbm.at[choice, pl.ds(n_p // chunk + part, 1)], is_v)

        def gather(idx_v, j, slot):
            return pltpu.make_async_copy(y_hbm.at[idx_v.at[j]], rows_v.at[slot], gsem)

        def write(j, slot):
            return pltpu.make_async_copy(rows_v.at[slot], op_hbm.at[pl.ds((wid * per_w + j) * chunk, chunk)], wsem)

        gather(ip_v, 0, 0).start()

        def body(h, carry):
            j = 2 * h
            gather(ip_v, j, 0).wait()

            @pl.when(h > 0)
            def _():
                write(j - 1, 1).wait()

            gather(ip_v, j + 1, 1).start()
            write(j, 0).start()
            gather(ip_v, j + 1, 1).wait()
            write(j, 0).wait()

            @pl.when(j + 2 < per_w)
            def _():
                gather(ip_v, j + 2, 0).start()

            write(j + 1, 1).start()
            return carry

        lax.fori_loop(0, per_w // 2, body, 0)
        write(per_w - 1, 1).wait()

        gather(is_v, 0, 0).start()
        gather(is_v, 0, 0).wait()
        pltpu.sync_copy(rows_v.at[0], os_hbm.at[pl.ds(wid * chunk, chunk)])

    return run(y_sorted, d3)


def _expert_body(blk0_ref, nblk_ref, cnt_ref, wup_hbm, wdn_hbm, bup_ref, bdn_ref, x_hbm, y_hbm,
                 wup_f, wdn_f, wup_s, wdn_s, xbuf, obuf, w_sem, in_sem, out_sem):
    e = pl.program_id(0)
    nb = nblk_ref[e]
    blk0 = blk0_ref[e]
    cnt = cnt_ref[e]
    tm = EXPERT_ROWS
    pair = 2 * LANES
    wslot = e % 2
    up_rows = D_MODEL // W_PIECES
    dn_rows = D_MODEL // (W_PIECES // 2)

    def w_piece(hbm, buf, ex, slot, p, rows):
        start = p * rows if isinstance(p, int) else pl.multiple_of(p * rows, rows)
        r = pl.ds(start, rows)
        return pltpu.make_async_copy(hbm.at[ex, r], buf.at[slot, r], w_sem.at[slot])

    def w_start(ex, slot, p):
        w_piece(wup_hbm, wup_f, ex, slot, p, up_rows).start()
        if isinstance(p, int):
            if p < W_PIECES // 2:
                w_piece(wdn_hbm, wdn_f, ex, slot, p, dn_rows).start()
        else:
            @pl.when(p < W_PIECES // 2)
            def _():
                w_piece(wdn_hbm, wdn_f, ex, slot, p, dn_rows).start()

    def w_wait(ex, slot):
        for p in range(W_PIECES):
            w_piece(wup_hbm, wup_f, ex, slot, p, up_rows).wait()
        for p in range(W_PIECES // 2):
            w_piece(wdn_hbm, wdn_f, ex, slot, p, dn_rows).wait()

    @pl.when(e == 0)
    def _():
        for p in range(W_PIECES):
            w_start(0, 0, p)

    w_wait(e, wslot)
    wup_ref = wup_f.at[wslot]
    wdn_ref = wdn_f.at[wslot]
    more = e + 1 < N_EXPERTS

    def x_copy(i, slot):
        rows = pl.ds(pl.multiple_of((blk0 + i) * tm, tm), tm)
        return pltpu.make_async_copy(x_hbm.at[rows], xbuf.at[slot], in_sem.at[slot])

    def y_copy(i, slot):
        rows = pl.ds(pl.multiple_of((blk0 + i) * tm, tm), tm)
        return pltpu.make_async_copy(obuf.at[slot], y_hbm.at[rows], out_sem.at[slot])

    @pl.when(nb > 0)
    def _():
        x_copy(0, 0).start(priority=1)
        r = lax.broadcasted_iota(jnp.int32, (pair, pair), 0)
        c = lax.broadcasted_iota(jnp.int32, (pair, pair), 1)
        perm = (r == jnp.where(c < LANES, 2 * c, 2 * (c - LANES) + 1)).astype(BF16)
        for g in range(2 * D_MODEL // pair):
            cols = slice(g * pair, (g + 1) * pair)
            wup_s[g] = jnp.dot(wup_ref[:, cols].astype(BF16), perm, preferred_element_type=F32).astype(BF16)
        for g in range(D_MODEL // pair):
            wdn_s[g] = wdn_ref[:, g * pair:(g + 1) * pair].astype(BF16)

        def block(i, carry):
            slot = i % 2
            x_copy(i, slot).wait()

            @pl.when(i + 1 < nb)
            def _():
                x_copy(i + 1, 1 - slot).start(priority=1)

            @pl.when(i >= 2)
            def _():
                y_copy(i - 2, slot).wait()

            @pl.when(jnp.logical_and(more, i < W_PIECES))
            def _():
                w_start(e + 1, 1 - wslot, i)

            row = lax.broadcasted_iota(jnp.int32, (tm, 1), 0)
            x = jnp.where(row < cnt - i * tm, _unpack_bf16_pair(xbuf[slot]), 0.0).astype(BF16)
            acts = []
            for g in range(2 * D_MODEL // pair):
                cols = slice(g * pair, (g + 1) * pair)
                h = jnp.dot(x, wup_s[g], preferred_element_type=F32) + bup_ref[0, :, cols]
                glu = jnp.minimum(h[:, :LANES], SWIGLU_LIMIT)
                lin = jnp.clip(h[:, LANES:], -SWIGLU_LIMIT, SWIGLU_LIMIT)
                acts.append((glu * jax.nn.sigmoid(SWIGLU_ALPHA * glu) * (lin + 1.0)).astype(BF16))
            act = jnp.concatenate(acts, axis=1)
            half_groups = D_MODEL // pair // 2
            for g in range(half_groups):
                ys = []
                for gg in (g, g + half_groups):
                    cols = slice(gg * pair, (gg + 1) * pair)
                    ys.append(jnp.dot(act, wdn_s[gg], preferred_element_type=F32) + bdn_ref[0, :, cols])
                obuf[slot, :, g * pair:(g + 1) * pair] = _pack_bf16_pair(ys[0], ys[1])
            y_copy(i, slot).start(priority=1)
            return carry

        lax.fori_loop(0, nb, block, 0)

        @pl.when(nb >= 2)
        def _():
            y_copy(nb - 2, nb % 2).wait()

        y_copy(nb - 1, (nb - 1) % 2).wait()

    for p in range(W_PIECES):
        @pl.when(jnp.logical_and(more, p >= nb))
        def _():
            w_start(e + 1, 1 - wslot, p)


def _experts(blk0, nblk, cnt, x_sorted, w_up, w_down, b_up_grouped, b_down):
    tm = EXPERT_ROWS
    per_expert = lambda a: pl.BlockSpec((1,) + a.shape[1:], lambda e, b0, nb, ct: (e, 0, 0))
    hbm = pl.BlockSpec(memory_space=pl.ANY)
    grid_spec = pltpu.PrefetchScalarGridSpec(
        num_scalar_prefetch=3,
        grid=(N_EXPERTS,),
        in_specs=[hbm, hbm, per_expert(b_up_grouped), per_expert(b_down), hbm],
        out_specs=hbm,
        scratch_shapes=[pltpu.VMEM((2,) + w_up.shape[1:], F32), pltpu.VMEM((2,) + w_down.shape[1:], F32),
                        pltpu.VMEM((2 * D_MODEL // (2 * LANES), D_MODEL, 2 * LANES), BF16),
                        pltpu.VMEM((D_MODEL // (2 * LANES), D_MODEL, 2 * LANES), BF16),
                        pltpu.VMEM((2, tm, x_sorted.shape[1]), x_sorted.dtype),
                        pltpu.VMEM((2, tm, D_MODEL // 2), jnp.uint32),
                        pltpu.SemaphoreType.DMA((2,)), pltpu.SemaphoreType.DMA((2,)), pltpu.SemaphoreType.DMA((2,))],
    )
    return pl.pallas_call(
        _expert_body,
        grid_spec=grid_spec,
        out_shape=jax.ShapeDtypeStruct((x_sorted.shape[0], D_MODEL // 2), jnp.uint32),
        compiler_params=_params(("arbitrary",)),
        name="experts",
    )(blk0, nblk, cnt, w_up, w_down, b_up_grouped, b_down, x_sorted)


def _combine_body(gate_ref, xmid_ref, gfin_ref, y0_ref, y1_ref, y2_ref, y3_ref, out_ref):
    gate = gate_ref[...]
    moe = _unpack_bf16_pair(y0_ref[...]) * gate[:, 0:1]
    for kx, y_ref in enumerate((y1_ref, y2_ref, y3_ref), start=1):
        moe = moe + _unpack_bf16_pair(y_ref[...]) * gate[:, kx:kx + 1]
    out_ref[...] = _rms(xmid_ref[...] + moe, gfin_ref[...])


def _combine(gates, first_token, xmid, gfin, y_rows):
    n = xmid.shape[0]
    tt = min(n, COMBINE_ROWS)
    nblk = n // tt
    blk0 = first_token // tt
    assert blk0 * tt == first_token
    choice = lambda kx: pl.BlockSpec((tt, y_rows.shape[1]), lambda i: (i + kx * nblk, 0))
    return pl.pallas_call(
        _combine_body,
        grid=(nblk,),
        in_specs=[pl.BlockSpec((tt, LANES), lambda i: (i + blk0, 0)),
                  pl.BlockSpec((tt, D_MODEL), lambda i: (i, 0)),
                  pl.BlockSpec((1, D_MODEL), lambda i: (0, 0))] + [choice(kx) for kx in range(TOP_K)],
        out_specs=pl.BlockSpec((tt, D_MODEL), lambda i: (i, 0)),
        out_shape=jax.ShapeDtypeStruct((n, D_MODEL), F32),
        compiler_params=_params(("arbitrary",)),
        name="combine",
    )(gates, xmid, gfin, y_rows, y_rows, y_rows, y_rows)


def kernel(x_prompt, x_sample, cache_k_win, cache_v_win, norm_attn_g, w_in, ln_v_g, ln_v_b, w_spatial, b_spatial,
           attn_sinks, w_out, norm_ffn_g, w_router, b_router, w_up, b_up, w_down, b_down, norm_final_g):
    bp, tp, _ = x_prompt.shape
    bs, ts, _ = x_sample.shape
    w_buf = cache_k_win.shape[2]
    assert bp == 1 and tp % MIX_ROWS == 0 and w_buf == WINDOW and (bs * ts) % PROJ_ROWS == 0 and 8 % ts == 0
    n_p, n_s = bp * tp, bs * ts
    row2 = lambda a: a.reshape(1, -1)

    w_in_bf = w_in[0].astype(BF16)
    w_out_bf = w_out[0].astype(BF16)
    tril = jnp.tril(jnp.ones((CHUNK, CHUNK), dtype=bool))
    wsp = jnp.where(tril[None], w_spatial[0], 0.0)
    wsp_bf = wsp.astype(BF16)
    bsp = jnp.broadcast_to(b_spatial[0][:, :, None], (GMLP_GROUPS, CHUNK, LANES))
    b_up_grouped = b_up[0].reshape(N_EXPERTS, -1, LANES, 2).transpose(0, 1, 3, 2).reshape(N_EXPERTS, 1, -1)
    bd = b_down[0][:, None, :]
    g1, g2, gfin = row2(norm_attn_g[0]), row2(norm_ffn_g[0]), row2(norm_final_g)
    lng, lnb = row2(ln_v_g[0]), row2(ln_v_b[0])
    wr_hi = w_router[0].astype(BF16)
    wr = jnp.concatenate([wr_hi, (w_router[0] - wr_hi.astype(F32)).astype(BF16)], axis=1)
    br = row2(b_router[0])
    sinks = attn_sinks[0]

    xp = x_prompt.reshape(n_p, D_MODEL)
    cs_p = _rotary_inputs(jnp.arange(tp, dtype=jnp.int32))
    q_p, k_p, v_p, a_p, vn_p, sgb_p = _proj(xp, g1, w_in_bf, cs_p, lng, lnb)
    xmid_p, xn2_p, lg_p = _mix(sinks, q_p, k_p, v_p, a_p, vn_p, sgb_p, xp, wsp_bf, bsp, w_out_bf, g2, wr, br)

    xs = x_sample.reshape(n_s, D_MODEL)
    pos_s = PAST_LEN + jnp.arange(ts, dtype=jnp.int32)
    cs_s = _rotary_inputs(jnp.tile(pos_s, bs))
    q_s, k_s, v_s, a_s, vn_s, sgb_s = _proj(xs, g1, w_in_bf, cs_s, lng, lnb)
    n_keys = w_buf + ts
    key_pad = jnp.zeros((bs, (-n_keys) % 8, KV_WIDTH), F32)
    with_new = lambda cache, new: jnp.concatenate(
        [cache[0].reshape(bs, w_buf, KV_WIDTH), new.reshape(bs, ts, KV_WIDTH), key_pad], axis=1)
    k_all = with_new(cache_k_win, k_s)
    v_all = with_new(cache_v_win, v_s)
    sink_col = jnp.tile(sinks, ts).reshape(ts * N_HEADS, 1)
    o_s = _sample_attn(sink_col, q_s.reshape(bs, ts * N_HEADS, HEAD_DIM), k_all, v_all).reshape(n_s, Q_WIDTH)
    lag = np.arange(ts)[:, None] - np.arange(ts)[None, :]
    coef = jnp.stack([jnp.sum(jnp.where(lag == d, wsp[:, :ts, :ts], 0.0), axis=2)
                      for d in range(ts)])
    coef = jnp.repeat(coef.transpose(0, 2, 1), GMLP_WIDTH // GMLP_GROUPS, axis=2)
    coef = jnp.tile(coef, (1, 8 // ts, 1))
    bias = jnp.tile(jnp.repeat(b_spatial[0][:, :ts].T, GMLP_WIDTH // GMLP_GROUPS, axis=1), (8 // ts, 1))
    xmid_s, xn2_s, lg_s = _mix_sample(a_s, vn_s, sgb_s, o_s, xs, coef, bias, w_out_bf, g2, wr, br)

    tm = EXPERT_ROWS
    n_blocks = (n_p * TOP_K + N_EXPERTS * n_s) // tm + N_EXPERTS
    gate_p, dest_p, table_p = _route(lg_p, jnp.zeros((N_EXPERTS, LANES), F32), n_s)
    x_sorted = _sc_dispatch(xn2_p, dest_p, n_blocks * tm)
    gate_s, dest_s, table = _route(lg_s, table_p, None)
    x_sorted = _dispatch_small(xn2_s, dest_s, x_sorted)
    dest_t = jnp.concatenate([dest_p, dest_s], axis=1)
    meta = table.astype(jnp.int32)
    y_sorted = _experts(meta[:, 2], meta[:, 3], meta[:, 1], x_sorted, w_up[0], w_down[0], b_up_grouped, bd)
    yrows_p, yrows_s = _sc_collect(y_sorted, dest_t, n_p, n_s)
    y_p = _combine(gate_p, 0, xmid_p, gfin, yrows_p)
    y_s = _combine(gate_s, 0, xmid_s, gfin, yrows_s)

    k4 = lambda t: t.reshape(1, bp, -1, N_KV_HEADS, HEAD_DIM)
    return (y_p.reshape(bp, tp, D_MODEL),
            y_s.reshape(bs, ts, D_MODEL),
            k4(k_p[n_p - WINDOW:]),
            k4(v_p[n_p - WINDOW:]),
            vn_p[n_p - CHUNK:].reshape(1, bp, CHUNK, GMLP_WIDTH),
            k_all[:, ts:n_keys].reshape(1, bs, w_buf, N_KV_HEADS, HEAD_DIM),
            v_all[:, ts:n_keys].reshape(1, bs, w_buf, N_KV_HEADS, HEAD_DIM),
            vn_s.reshape(1, bs, ts, GMLP_WIDTH))
```

```python
import functools

import numpy as np
import jax
import jax.numpy as jnp
from jax import lax
from jax.experimental import pallas as pl
from jax.experimental.pallas import tpu as pltpu
from jax.experimental.pallas import tpu_sc as plsc

F32 = jnp.float32
BF16 = jnp.bfloat16

D_MODEL = 1024
HEAD_DIM = 64
N_HEADS = 16
GQA_GROUP = 8
N_KV_HEADS = 2
Q_WIDTH = 1024
KV_WIDTH = 128
WINDOW = 128
ROT_DIM = 16
ROPE_THETA = 500000.0
CHUNK = 128
GMLP_WIDTH = 1024
GMLP_GROUPS = 8
N_EXPERTS = 32
TOP_K = 4
SWIGLU_LIMIT = 7.0
SWIGLU_ALPHA = 1.702
RMS_EPS = 1e-5
LN_EPS = 1e-5
NEG_INF = -1e30
PAST_LEN = 16384

LANES = 128
VMEM_LIMIT = 56 * 1024 * 1024

PROJ_ROWS = 256
MIX_ROWS = 512
ROUTE_ROWS = 512
EXPERT_ROWS = 256
W_PIECES = 8
COMBINE_ROWS = 1024

SC_CORES = 2
SC_WORKERS = 32
SC_ROWS = 64

_C_Q, _C_KV, _C_U, _C_VG, _C_GA, _C_GB, _C_END = 0, 1024, 1280, 2304, 3328, 4352, 5376


def _params(sem):
    return pltpu.CompilerParams(dimension_semantics=sem, vmem_limit_bytes=VMEM_LIMIT)


def _rms(x, g):
    return x * lax.rsqrt(jnp.mean(x * x, axis=-1, keepdims=True) + RMS_EPS) * g


def _pack_bf16_pair(lo, hi):
    lo_bits = lax.bitcast_convert_type(lo.astype(BF16).astype(F32), jnp.uint32)
    hi_bits = lax.bitcast_convert_type(hi.astype(BF16).astype(F32), jnp.uint32)
    return (lo_bits >> 16) | hi_bits


def _unpack_bf16_pair(words):
    lo = lax.bitcast_convert_type(words << 16, F32)
    hi = lax.bitcast_convert_type(words & jnp.uint32(0xFFFF0000), F32)
    return jnp.concatenate([lo, hi], axis=1)


def _proj_body(nblk_p, xp_ref, xs_ref, g_ref, w_hbm, csp_ref, css_ref, rot_ref, lng_ref, lnb_ref,
               q_ref, k_ref, v_ref, a_ref, vn_ref, sgb_ref, w_ref, stage, sem):
    i = pl.program_id(0)

    @pl.when(i == 0)
    def _():
        piece = 2 * LANES
        npieces = w_ref.shape[1] // piece

        def fetch(c, slot):
            cols = pl.ds(pl.multiple_of(c * piece, piece), piece)
            return pltpu.make_async_copy(w_hbm.at[:, cols], stage.at[slot], sem.at[slot])

        fetch(0, 0).start()

        def convert(c, carry):
            slot = c % 2
            fetch(c, slot).wait()

            @pl.when(c + 1 < npieces)
            def _():
                fetch(c + 1, 1 - slot).start()

            w_ref[:, pl.ds(pl.multiple_of(c * piece, piece), piece)] = stage[slot].astype(BF16)
            return carry

        lax.fori_loop(0, npieces, convert, 0)

    from_prompt = jnp.full(xp_ref.shape, i, jnp.int32) < nblk_p
    h = _rms(jnp.where(from_prompt, xp_ref[...], xs_ref[...]), g_ref[...]).astype(BF16)
    cs = jnp.where(jnp.full(csp_ref.shape, i, jnp.int32) < nblk_p, csp_ref[...], css_ref[...])
    tabs = lax.dot_general(cs, rot_ref[...], (((0,), (0,)), ((), ())), preferred_element_type=F32)
    rc, rs1, rs2 = tabs[:, :LANES], tabs[:, LANES:2 * LANES], tabs[:, 2 * LANES:]

    def rot(z):
        return z * rc + pltpu.roll(z, LANES - ROT_DIM // 2, 1) * rs1 + pltpu.roll(z, ROT_DIM // 2, 1) * rs2

    def mm(lo, hi):
        return jnp.dot(h, w_ref[:, lo:hi], preferred_element_type=F32)

    zq = mm(_C_Q, _C_KV)
    for c in range(Q_WIDTH // LANES):
        sl = slice(c * LANES, (c + 1) * LANES)
        q_ref[:, sl] = (rot(zq[:, sl]) * (HEAD_DIM ** -0.5)).astype(BF16)
    zkv = mm(_C_KV, _C_U)
    k_ref[...] = rot(zkv[:, :KV_WIDTH])
    v_ref[...] = zkv[:, KV_WIDTH:]
    a_ref[...] = jax.nn.sigmoid(mm(_C_GA, _C_GB)) * jax.nn.gelu(mm(_C_U, _C_VG))
    zv = jax.nn.gelu(mm(_C_VG, _C_GA))
    zc = zv - jnp.mean(zv, axis=-1, keepdims=True)
    var = jnp.mean(zc * zc, axis=-1, keepdims=True)
    vn_ref[...] = zc * lax.rsqrt(var + LN_EPS) * lng_ref[...] + lnb_ref[...]
    sgb_ref[...] = jax.nn.sigmoid(mm(_C_GB, _C_END))


def _proj(x_p, x_s, norm_g, w_in, cs_p, cs_s, ln_g, ln_b):
    tm = PROJ_ROWS
    nblk_p, nblk_s = x_p.shape[0] // tm, x_s.shape[0] // tm
    n = (nblk_p + nblk_s) * tm
    assert n == x_p.shape[0] + x_s.shape[0]
    row = lambda w: pl.BlockSpec((tm, w), lambda i: (i, 0))
    full = lambda a: pl.BlockSpec(a.shape, lambda i: (0,) * a.ndim)
    first = lambda i: jnp.minimum(i, nblk_p - 1)
    second = lambda i: jnp.maximum(i - nblk_p, 0)
    rot = jnp.asarray(np.tile(_ROT_EXPAND, (3, 1)), dtype=BF16)
    return pl.pallas_call(
        functools.partial(_proj_body, nblk_p),
        grid=(nblk_p + nblk_s,),
        in_specs=[pl.BlockSpec((tm, D_MODEL), lambda i: (first(i), 0)),
                  pl.BlockSpec((tm, D_MODEL), lambda i: (second(i), 0)),
                  full(norm_g), pl.BlockSpec(memory_space=pl.ANY),
                  pl.BlockSpec((cs_p.shape[0], tm), lambda i: (0, first(i))),
                  pl.BlockSpec((cs_s.shape[0], tm), lambda i: (0, second(i))),
                  full(rot), full(ln_g), full(ln_b)],
        out_specs=[row(Q_WIDTH), row(KV_WIDTH), row(KV_WIDTH), row(GMLP_WIDTH), row(GMLP_WIDTH), row(D_MODEL)],
        out_shape=[jax.ShapeDtypeStruct((n, Q_WIDTH), BF16),
                   jax.ShapeDtypeStruct((n, KV_WIDTH), F32),
                   jax.ShapeDtypeStruct((n, KV_WIDTH), F32),
                   jax.ShapeDtypeStruct((n, GMLP_WIDTH), F32),
                   jax.ShapeDtypeStruct((n, GMLP_WIDTH), F32),
                   jax.ShapeDtypeStruct((n, D_MODEL), F32)],
        scratch_shapes=[pltpu.VMEM(w_in.shape, BF16), pltpu.VMEM((2, w_in.shape[0], 2 * LANES), F32),
                        pltpu.SemaphoreType.DMA((2,))],
        compiler_params=_params(("arbitrary",)),
        name="proj",
    )(x_p, x_s, norm_g, w_in, cs_p, cs_s, rot, ln_g, ln_b)


_ROT_COLS = 32


def _rot_expand():
    half = ROT_DIM // 2
    m = np.zeros((_ROT_COLS, 3 * LANES), np.float32)
    for lane in range(LANES):
        d = lane % HEAD_DIM
        if d < ROT_DIM:
            m[d % half, lane] = 1.0
        else:
            m[2 * half, lane] = 1.0
        if d < half:
            m[half + d, LANES + lane] = -1.0
        elif d < ROT_DIM:
            m[half + d - half, 2 * LANES + lane] = 1.0
    return m


_ROT_EXPAND = _rot_expand()


def _rotary_inputs(pos):
    half = ROT_DIM // 2
    inv_freq = ROPE_THETA ** (-jnp.arange(half, dtype=F32) / half)
    ang = inv_freq[:, None] * pos.astype(F32)[None, :]
    n = pos.shape[0]
    cs = jnp.concatenate([jnp.cos(ang), jnp.sin(ang), jnp.ones((1, n), F32),
                          jnp.zeros((_ROT_COLS - 2 * half - 1, n), F32)], axis=0)
    hi = cs.astype(BF16)
    rest = cs - hi.astype(F32)
    mid = rest.astype(BF16)
    lo = (rest - mid.astype(F32)).astype(BF16)
    return jnp.concatenate([hi, mid, lo], axis=0)


def _finish_rows(merged_bf, x, wout_ref, g2_ref, wr_ref, br_ref, xmid_ref, xn2_ref, lg_ref):
    xm = x + jnp.dot(merged_bf, wout_ref[...], preferred_element_type=F32)
    xmid_ref[...] = xm
    xn = _rms(xm, g2_ref[...])
    x_hi = xn.astype(BF16)
    x_lo = (xn - x_hi.astype(F32)).astype(BF16)
    w_hl = wr_ref[...]
    p_hi = jnp.dot(x_hi, w_hl, preferred_element_type=F32)
    p_lo = jnp.dot(x_lo, w_hl[:, :N_EXPERTS], preferred_element_type=F32)
    lg = p_hi[:, :N_EXPERTS] + (p_hi[:, N_EXPERTS:] + p_lo) + br_ref[...]
    wide = jnp.concatenate([lg, jnp.zeros((lg.shape[0], LANES - N_EXPERTS), F32)], axis=1)
    lg_ref[...] = wide.T[:N_EXPERTS]
    xn2_ref[...] = _pack_bf16_pair(xn[:, :D_MODEL // 2], xn[:, D_MODEL // 2:])


def _mix_body(sinks_ref, q_ref, k_ref, kp_ref, v_ref, vp_ref, a_ref, vn_ref, sgb_ref, x_ref,
              wsp_ref, bsp_ref, wout_ref, g2_ref, wr_ref, br_ref,
              xmid_ref, xn2_ref, lg_ref, kcat, vcat, mrg, key_s):
    i = pl.program_id(0)
    nsub = MIX_ROWS // WINDOW
    kcat[0:WINDOW] = kp_ref[...]
    kcat[WINDOW:] = k_ref[...]
    vcat[0:WINDOW] = vp_ref[...]
    vcat[WINDOW:] = v_ref[...]

    pair_rows = (GQA_GROUP // 2) * WINDOW
    lane_kv = lax.broadcasted_iota(jnp.int32, (2 * WINDOW, LANES), 1)
    lane_o = lax.broadcasted_iota(jnp.int32, (pair_rows, LANES), 1)

    @pl.when(i == 0)
    def _():
        rq = lax.broadcasted_iota(jnp.int32, (pair_rows, 4 * WINDOW), 0) & (WINDOW - 1)
        ck = lax.broadcasted_iota(jnp.int32, (pair_rows, 4 * WINDOW), 1) & (2 * WINDOW - 1)
        key_s[...] = jnp.where((ck > rq) & (ck <= rq + WINDOW), ck, -1)

    row_p = lax.broadcasted_iota(jnp.int32, (pair_rows, 1), 0) >> 7
    sink_cols = []
    for kk in range(N_KV_HEADS):
        h0 = kk * GQA_GROUP
        se = jnp.full((pair_rows, 1), sinks_ref[h0], F32)
        so = jnp.full((pair_rows, 1), sinks_ref[h0 + 1], F32)
        for p in range(1, GQA_GROUP // 2):
            se = jnp.where(row_p == p, sinks_ref[h0 + 2 * p], se)
            so = jnp.where(row_p == p, sinks_ref[h0 + 2 * p + 1], so)
        sink_cols.append((se, so))

    def sub(j, carry):
        off = pl.multiple_of(j * WINDOW, WINDOW)
        rows = pl.ds(off, WINDOW)
        for g in range(GMLP_GROUPS):
            cols = slice(g * LANES, (g + 1) * LANES)
            s = jnp.dot(wsp_ref[g], vn_ref[rows, cols].astype(BF16), preferred_element_type=F32) + bsp_ref[g]
            mrg[rows, cols] = a_ref[rows, cols] * s
        kblk = kcat[pl.ds(off, 2 * WINDOW), :]
        vblk = vcat[pl.ds(off, 2 * WINDOW), :]
        kswp = pltpu.roll(kblk, HEAD_DIM, 1)
        vswp = pltpu.roll(vblk, HEAD_DIM, 1)
        kmin = jnp.where(jnp.logical_and(i == 0, j == 0), WINDOW, 0)
        allowed = key_s[...] >= kmin
        for kk in range(N_KV_HEADS):
            lo_src, hi_src = (kblk, kswp) if kk == 0 else (kswp, kblk)
            kbd = jnp.concatenate([jnp.where(lane_kv < HEAD_DIM, lo_src, 0.0),
                                   jnp.where(lane_kv >= HEAD_DIM, hi_src, 0.0)], axis=0).astype(BF16)
            lo_src, hi_src = (vblk, vswp) if kk == 0 else (vswp, vblk)
            vbd = jnp.concatenate([jnp.where(lane_kv < HEAD_DIM, lo_src, 0.0),
                                   jnp.where(lane_kv >= HEAD_DIM, hi_src, 0.0)], axis=0).astype(BF16)
            pair0 = kk * (GQA_GROUP // 2)
            qs = jnp.concatenate([q_ref[rows, (pair0 + p) * LANES:(pair0 + p + 1) * LANES]
                                  for p in range(GQA_GROUP // 2)], axis=0)
            lg = lax.dot_general(qs, kbd, (((1,), (1,)), ((), ())), preferred_element_type=F32)
            lg = jnp.where(allowed, lg, NEG_INF)
            se, so = sink_cols[kk]
            le, lo = lg[:, :2 * WINDOW], lg[:, 2 * WINDOW:]
            me = jnp.maximum(jnp.max(le, axis=1, keepdims=True), se)
            mo = jnp.maximum(jnp.max(lo, axis=1, keepdims=True), so)
            pe = jnp.exp(le - me)
            po = jnp.exp(lo - mo)
            de = jnp.sum(pe, axis=1, keepdims=True) + jnp.exp(se - me)
            do = jnp.sum(po, axis=1, keepdims=True) + jnp.exp(so - mo)
            pr = jnp.concatenate([pe, po], axis=1).astype(BF16)
            o = jnp.dot(pr, vbd, preferred_element_type=F32)
            o = o / jnp.where(lane_o < HEAD_DIM, de, do)
            for p in range(GQA_GROUP // 2):
                cols = slice((pair0 + p) * LANES, (pair0 + p + 1) * LANES)
                mrg[rows, cols] += sgb_ref[rows, cols] * o[p * WINDOW:(p + 1) * WINDOW]
        return carry

    lax.fori_loop(0, nsub, sub, 0)
    _finish_rows(mrg[...].astype(BF16), x_ref[...], wout_ref, g2_ref, wr_ref, br_ref, xmid_ref, xn2_ref, lg_ref)


def _mix(sinks, q, k, v, a, vn, sgb, x, wsp, bsp, wout, g2, wr, br):
    n = x.shape[0]
    tm = MIX_ROWS
    nsub = tm // WINDOW
    row = lambda w: pl.BlockSpec((tm, w), lambda i: (i, 0))
    prev = pl.BlockSpec((WINDOW, KV_WIDTH), lambda i: (jnp.maximum(i * nsub - 1, 0), 0))
    full = lambda arr: pl.BlockSpec(arr.shape, lambda i: (0,) * arr.ndim)
    smem = pl.BlockSpec(memory_space=pltpu.SMEM)
    return pl.pallas_call(
        _mix_body,
        grid=(n // tm,),
        in_specs=[smem, row(Q_WIDTH), row(KV_WIDTH), prev, row(KV_WIDTH), prev,
                  row(GMLP_WIDTH), row(GMLP_WIDTH), row(D_MODEL), row(D_MODEL),
                  full(wsp), full(bsp), full(wout), full(g2), full(wr), full(br)],
        out_specs=[row(D_MODEL), row(D_MODEL // 2), pl.BlockSpec((N_EXPERTS, tm), lambda i: (0, i))],
        out_shape=[jax.ShapeDtypeStruct((n, D_MODEL), F32),
                   jax.ShapeDtypeStruct((n, D_MODEL // 2), jnp.uint32),
                   jax.ShapeDtypeStruct((N_EXPERTS, n), F32)],
        scratch_shapes=[pltpu.VMEM((tm + WINDOW, KV_WIDTH), F32),
                        pltpu.VMEM((tm + WINDOW, KV_WIDTH), F32),
                        pltpu.VMEM((tm, D_MODEL), F32),
                        pltpu.VMEM((GQA_GROUP // 2 * WINDOW, 4 * WINDOW), jnp.int32)],
        compiler_params=_params(("arbitrary",)),
        name="mix_prompt",
    )(sinks, q, k, k, v, v, a, vn, sgb, x, wsp, bsp, wout, g2, wr, br)


def _sample_attn_body(sink_ref, q_ref, k_ref, v_ref, o_ref):
    q = q_ref[...]
    k = k_ref[...]
    v = v_ref[...]
    nq, nk = q.shape[1], k.shape[1]
    heads = lambda t, kk: t[:, :, kk * HEAD_DIM:(kk + 1) * HEAD_DIM].astype(BF16)
    row = lax.broadcasted_iota(jnp.int32, (1, nq, nk), 1)
    first_kv = (row % N_HEADS) < GQA_GROUP
    lg = jnp.where(first_kv,
                   jnp.einsum("bqd,bkd->bqk", q, heads(k, 0), preferred_element_type=F32),
                   jnp.einsum("bqd,bkd->bqk", q, heads(k, 1), preferred_element_type=F32))
    t = row // N_HEADS
    j = lax.broadcasted_iota(jnp.int32, (1, nq, nk), 2)
    lg = jnp.where((j > t) & (j <= t + WINDOW), lg, NEG_INF)
    sink = sink_ref[...][None]
    m = jnp.maximum(jnp.max(lg, axis=2, keepdims=True), sink)
    p = jnp.exp(lg - m)
    den = jnp.sum(p, axis=2, keepdims=True) + jnp.exp(sink - m)
    pb = p.astype(BF16)
    row_o = lax.broadcasted_iota(jnp.int32, (1, nq, HEAD_DIM), 1)
    o = jnp.where((row_o % N_HEADS) < GQA_GROUP,
                  jnp.einsum("bqk,bkd->bqd", pb, heads(v, 0), preferred_element_type=F32),
                  jnp.einsum("bqk,bkd->bqd", pb, heads(v, 1), preferred_element_type=F32))
    o_ref[...] = o / den


def _sample_attn(sink_col, q3, k_all, v_all):
    nb = q3.shape[0]
    bb = 32
    blk = lambda a: pl.BlockSpec((bb,) + a.shape[1:], lambda b: (b, 0, 0))
    return pl.pallas_call(
        _sample_attn_body,
        grid=(nb // bb,),
        in_specs=[pl.BlockSpec(sink_col.shape, lambda b: (0, 0)), blk(q3), blk(k_all), blk(v_all)],
        out_specs=blk(q3),
        out_shape=jax.ShapeDtypeStruct(q3.shape, F32),
        compiler_params=_params(("arbitrary",)),
        name="attn_sample",
    )(sink_col, q3, k_all, v_all)


def _mix_sample_body(a_ref, vn_ref, sgb_ref, o_ref, x_ref, coef_ref, bias_ref,
                     wout_ref, g2_ref, wr_ref, br_ref, xmid_ref, xn2_ref, lg_ref):
    vn = vn_ref[...]
    n, width = vn.shape
    rows8 = lambda t: t.reshape(n // 8, 8, width)
    s = bias_ref[...][None] + coef_ref[0][None] * rows8(vn)
    for d in range(1, coef_ref.shape[0]):
        s = s + coef_ref[d][None] * rows8(pltpu.roll(vn, d, 0))
    merged = a_ref[...] * s.reshape(n, width) + sgb_ref[...] * o_ref[...]
    _finish_rows(merged.astype(BF16), x_ref[...], wout_ref, g2_ref, wr_ref, br_ref, xmid_ref, xn2_ref, lg_ref)


def _mix_sample(a, vn, sgb, o, x, coef, bias, wout, g2, wr, br):
    n = x.shape[0]
    args = (a, vn, sgb, o, x, coef, bias, wout, g2, wr, br)
    full = lambda arr: pl.BlockSpec(arr.shape, lambda i: (0,) * arr.ndim)
    return pl.pallas_call(
        _mix_sample_body,
        grid=(1,),
        in_specs=[full(arr) for arr in args],
        out_specs=[pl.BlockSpec((n, D_MODEL), lambda i: (0, 0)), pl.BlockSpec((n, D_MODEL // 2), lambda i: (0, 0)),
                   pl.BlockSpec((N_EXPERTS, n), lambda i: (0, 0))],
        out_shape=[jax.ShapeDtypeStruct((n, D_MODEL), F32),
                   jax.ShapeDtypeStruct((n, D_MODEL // 2), jnp.uint32),
                   jax.ShapeDtypeStruct((N_EXPERTS, n), F32)],
        compiler_params=_params(("arbitrary",)),
        name="mix_sample",
    )(*args)


def _rows8(rows, dtype):
    n = rows[0].shape[1]
    sub = lax.broadcasted_iota(jnp.int32, (8, n), 0)
    out = jnp.zeros((8, n), dtype)
    for kx, r in enumerate(rows):
        out = jnp.where(sub == kx, r.astype(dtype), out)
    return out


def _route_body(nblk, reserve, lg_ref, prior_ref, gate_ref, dest_ref, meta_ref, idx_s, rank_s, base):
    i = pl.program_id(0)

    @pl.when(i == 0)
    def _():
        base[...] = jnp.zeros_like(base)

    l = lg_ref[...]
    tb = l.shape[1]
    sub = lax.broadcasted_iota(jnp.int32, l.shape, 0).astype(F32)
    vals, idxs, sels = [], [], []
    for _ in range(TOP_K):
        m = jnp.max(l, axis=0, keepdims=True)
        ik = jnp.min(jnp.where(l == m, sub, float(N_EXPERTS)), axis=0, keepdims=True)
        sel = sub == ik
        l = jnp.where(sel, -jnp.inf, l)
        vals.append(m)
        idxs.append(ik)
        sels.append(sel)
    es = [jnp.exp(vk - vals[0]) for vk in vals]
    den = es[0] + es[1] + es[2] + es[3]
    onehot = jnp.zeros(l.shape, F32)
    for sel in sels:
        onehot = onehot + sel.astype(F32)
    earlier = (lax.broadcasted_iota(jnp.int32, (tb, tb), 0) < lax.broadcasted_iota(jnp.int32, (tb, tb), 1))
    before = jnp.dot(onehot.astype(BF16), earlier.astype(BF16), preferred_element_type=F32) + base[...]
    ranks = [jnp.sum(jnp.where(sel, before, 0.0), axis=0, keepdims=True) for sel in sels]
    base[...] += jnp.sum(onehot, axis=1, keepdims=True)
    idx_s[i] = _rows8(idxs, F32)
    rank_s[i] = _rows8(ranks, F32)
    gates = jnp.concatenate([_rows8([e / den for e in es], F32), jnp.zeros((LANES - 8, tb), F32)], axis=0)
    gate_ref[...] = gates.T

    @pl.when(i == nblk - 1)
    def _():
        cnt = base[...]
        if reserve is None:
            seg0 = prior_ref[:, 0:1]
            placed = prior_ref[:, 1:2]
        else:
            cap = jnp.ceil((cnt + reserve) / EXPERT_ROWS) * EXPERT_ROWS
            lower = (lax.broadcasted_iota(jnp.int32, (N_EXPERTS, N_EXPERTS), 1) <
                     lax.broadcasted_iota(jnp.int32, (N_EXPERTS, N_EXPERTS), 0)).astype(F32)
            seg0 = jnp.dot(lower, jnp.broadcast_to(cap, (N_EXPERTS, LANES)), preferred_element_type=F32,
                           precision=lax.Precision.HIGHEST)[:, :1]
            placed = jnp.zeros_like(cnt)
        total = placed + cnt
        lane = lax.broadcasted_iota(jnp.int32, (N_EXPERTS, LANES), 1)
        meta_ref[...] = jnp.where(lane == 0, seg0, jnp.where(lane == 1, total, jnp.where(
            lane == 2, seg0 / EXPERT_ROWS, jnp.where(lane == 3, jnp.ceil(total / EXPERT_ROWS), 0.0))))
        first = seg0 + placed
        sub_e = lax.broadcasted_iota(jnp.int32, (N_EXPERTS, tb), 0).astype(F32)
        for b in range(nblk):
            idx, rank = idx_s[b], rank_s[b]
            rows = [jnp.sum(jnp.where(sub_e == idx[kx:kx + 1], first, 0.0), axis=0, keepdims=True)
                    + rank[kx:kx + 1] for kx in range(TOP_K)]
            dest_ref[:, b * tb:(b + 1) * tb] = _rows8(rows, jnp.int32)


def _route(logits, prior, reserve):
    tb = ROUTE_ROWS
    n = logits.shape[1]
    nblk = n // tb
    assert nblk * tb == n
    table = pl.BlockSpec((N_EXPERTS, LANES), lambda i: (0, 0))
    return pl.pallas_call(
        functools.partial(_route_body, nblk, reserve),
        grid=(nblk,),
        in_specs=[pl.BlockSpec((N_EXPERTS, tb), lambda i: (0, i)), table],
        out_specs=[pl.BlockSpec((tb, LANES), lambda i: (i, 0)), pl.BlockSpec((8, n), lambda i: (0, 0)), table],
        out_shape=[jax.ShapeDtypeStruct((n, LANES), F32),
                   jax.ShapeDtypeStruct((8, n), jnp.int32),
                   jax.ShapeDtypeStruct((N_EXPERTS, LANES), F32)],
        scratch_shapes=[pltpu.VMEM((nblk, 8, tb), F32), pltpu.VMEM((nblk, 8, tb), F32),
                        pltpu.VMEM((N_EXPERTS, 1), F32)],
        compiler_params=_params(("arbitrary",)),
        name="route",
    )(logits, prior)


def _sc_mesh():
    return plsc.VectorSubcoreMesh(core_axis_name="c", subcore_axis_name="s")


def _sc_worker():
    return lax.axis_index("s") * SC_CORES + lax.axis_index("c")


def _sc_dispatch(x, dest_t, n_slots):
    chunk = SC_ROWS
    n = x.shape[0]
    per_w = n // (SC_WORKERS * chunk)
    assert per_w * SC_WORKERS * chunk == n and per_w % 2 == 0
    d3 = dest_t.reshape(dest_t.shape[0], n // chunk, chunk)
    width, dtype = x.shape[1], x.dtype

    @functools.partial(
        pl.kernel, mesh=_sc_mesh(),
        out_type=jax.ShapeDtypeStruct((n_slots, width), dtype),
        scratch_types=[pltpu.VMEM((TOP_K, per_w, chunk), jnp.int32),
                       pltpu.VMEM((2, chunk, width), dtype),
                       pltpu.SemaphoreType.DMA, pltpu.SemaphoreType.DMA],
        compiler_params=pltpu.CompilerParams(use_tc_tiling_on_sc=True),
        name="dispatch")
    def run(x_hbm, d_hbm, out_hbm, idx_v, rows_v, rsem, wsem):
        wid = _sc_worker()
        pltpu.sync_copy(d_hbm.at[pl.ds(0, TOP_K), pl.ds(wid * per_w, per_w)], idx_v)

        def read(j, slot):
            return pltpu.make_async_copy(x_hbm.at[pl.ds((wid * per_w + j) * chunk, chunk)], rows_v.at[slot], rsem)

        def scatter(j, slot):
            copies = [pltpu.async_copy(rows_v.at[slot], out_hbm.at[idx_v.at[kx, j]], wsem) for kx in range(TOP_K)]
            for cp in copies:
                cp.wait()

        read(0, 0).start()

        def body(h, carry):
            j = 2 * h
            read(j, 0).wait()
            read(j + 1, 1).start()
            scatter(j, 0)
            read(j + 1, 1).wait()

            @pl.when(j + 2 < per_w)
            def _():
                read(j + 2, 0).start()

            scatter(j + 1, 1)
            return carry

        lax.fori_loop(0, per_w // 2, body, 0)

    return run(x, d3)


def _dispatch_small_body(dest_ref, x_ref, _, out_hbm, sem):
    n = x_ref.shape[0]

    def body(t, carry):
        for kx in range(TOP_K):
            pltpu.make_async_copy(x_ref.at[pl.ds(t, 1)], out_hbm.at[pl.ds(dest_ref[kx, t], 1)], sem).start()
        return carry

    lax.fori_loop(0, n, body, 0, unroll=4)
    for _ in range(TOP_K):
        pltpu.make_async_copy(x_ref, out_hbm.at[pl.ds(0, n)], sem).wait()


def _dispatch_small(x, dest_t, x_sorted):
    return pl.pallas_call(
        _dispatch_small_body,
        grid=(1,),
        in_specs=[pl.BlockSpec(memory_space=pltpu.SMEM), pl.BlockSpec(x.shape, lambda i: (0, 0)),
                  pl.BlockSpec(memory_space=pl.ANY)],
        out_specs=pl.BlockSpec(memory_space=pl.ANY),
        out_shape=jax.ShapeDtypeStruct(x_sorted.shape, x_sorted.dtype),
        scratch_shapes=[pltpu.SemaphoreType.DMA],
        input_output_aliases={2: 0},
        compiler_params=_params(("arbitrary",)),
        name="dispatch_small",
    )(dest_t, x, x_sorted)


def _sc_collect(y_sorted, dest_t, n_p, n_s):
    chunk = SC_ROWS
    per_choice = SC_WORKERS // TOP_K
    per_w = n_p // (per_choice * chunk)
    assert per_w * per_choice * chunk == n_p and per_w % 2 == 0
    assert n_s == per_choice * chunk
    d3 = dest_t.reshape(dest_t.shape[0], (n_p + n_s) // chunk, chunk)
    width, dtype = y_sorted.shape[1], y_sorted.dtype

    @functools.partial(
        pl.kernel, mesh=_sc_mesh(),
        out_type=[jax.ShapeDtypeStruct((TOP_K * n_p, width), dtype), jax.ShapeDtypeStruct((TOP_K * n_s, width), dtype)],
        scratch_types=[pltpu.VMEM((per_w, chunk), jnp.int32),
                       pltpu.VMEM((1, chunk), jnp.int32),
                       pltpu.VMEM((2, chunk, width), dtype),
                       pltpu.SemaphoreType.DMA, pltpu.SemaphoreType.DMA],
        compiler_params=pltpu.CompilerParams(use_tc_tiling_on_sc=True),
        name="collect")
    def run(y_hbm, d_hbm, op_hbm, os_hbm, ip_v, is_v, rows_v, gsem, wsem):
        wid = _sc_worker()
        choice = wid // per_choice
        part = wid % per_choice
        pltpu.sync_copy(d_hbm.at[choice, pl.ds(part * per_w, per_w)], ip_v)
        pltpu.sync_copy(d_hbm.at[choice, pl.ds(n_p // chunk + part, 1)], is_v)

        def gather(idx_v, j, slot):
            return pltpu.make_async_copy(y_hbm.at[idx_v.at[j]], rows_v.at[slot], gsem)

        def write(j, slot):
            return pltpu.make_async_copy(rows_v.at[slot], op_hbm.at[pl.ds((wid * per_w + j) * chunk, chunk)], wsem)

        gather(ip_v, 0, 0).start()

        def body(h, carry):
            j = 2 * h
            gather(ip_v, j, 0).wait()

            @pl.when(h > 0)
            def _():
                write(j - 1, 1).wait()

            gather(ip_v, j + 1, 1).start()
            write(j, 0).start()
            gather(ip_v, j + 1, 1).wait()
            write(j, 0).wait()

            @pl.when(j + 2 < per_w)
            def _():
                gather(ip_v, j + 2, 0).start()

            write(j + 1, 1).start()
            return carry

        lax.fori_loop(0, per_w // 2, body, 0)
        write(per_w - 1, 1).wait()

        gather(is_v, 0, 0).start()
        gather(is_v, 0, 0).wait()
        pltpu.sync_copy(rows_v.at[0], os_hbm.at[pl.ds(wid * chunk, chunk)])

    return run(y_sorted, d3)


def _expert_body(blk0_ref, nblk_ref, cnt_ref, wup_hbm, wdn_hbm, bup_ref, bdn_ref, x_hbm, y_hbm,
                 wup_f, wdn_f, wup_s, wdn_s, xbuf, obuf, w_sem, in_sem, out_sem):
    e = pl.program_id(0)
    nb = nblk_ref[e]
    blk0 = blk0_ref[e]
    cnt = cnt_ref[e]
    tm = EXPERT_ROWS
    pair = 2 * LANES
    wslot = e % 2
    up_rows = D_MODEL // W_PIECES
    dn_rows = D_MODEL // (W_PIECES // 2)

    def w_piece(hbm, buf, ex, slot, p, rows):
        start = p * rows if isinstance(p, int) else pl.multiple_of(p * rows, rows)
        r = pl.ds(start, rows)
        return pltpu.make_async_copy(hbm.at[ex, r], buf.at[slot, r], w_sem.at[slot])

    def w_start(ex, slot, p):
        w_piece(wup_hbm, wup_f, ex, slot, p, up_rows).start()
        if isinstance(p, int):
            if p < W_PIECES // 2:
                w_piece(wdn_hbm, wdn_f, ex, slot, p, dn_rows).start()
        else:
            @pl.when(p < W_PIECES // 2)
            def _():
                w_piece(wdn_hbm, wdn_f, ex, slot, p, dn_rows).start()

    def w_wait(ex, slot):
        for p in range(W_PIECES):
            w_piece(wup_hbm, wup_f, ex, slot, p, up_rows).wait()
        for p in range(W_PIECES // 2):
            w_piece(wdn_hbm, wdn_f, ex, slot, p, dn_rows).wait()

    @pl.when(e == 0)
    def _():
        for p in range(W_PIECES):
            w_start(0, 0, p)

    w_wait(e, wslot)
    wup_ref = wup_f.at[wslot]
    wdn_ref = wdn_f.at[wslot]
    more = e + 1 < N_EXPERTS

    def x_copy(i, slot):
        rows = pl.ds(pl.multiple_of((blk0 + i) * tm, tm), tm)
        return pltpu.make_async_copy(x_hbm.at[rows], xbuf.at[slot], in_sem.at[slot])

    def y_copy(i, slot):
        rows = pl.ds(pl.multiple_of((blk0 + i) * tm, tm), tm)
        return pltpu.make_async_copy(obuf.at[slot], y_hbm.at[rows], out_sem.at[slot])

    @pl.when(nb > 0)
    def _():
        x_copy(0, 0).start(priority=1)
        r = lax.broadcasted_iota(jnp.int32, (pair, pair), 0)
        c = lax.broadcasted_iota(jnp.int32, (pair, pair), 1)
        perm = (r == jnp.where(c < LANES, 2 * c, 2 * (c - LANES) + 1)).astype(BF16)
        for g in range(2 * D_MODEL // pair):
            cols = slice(g * pair, (g + 1) * pair)
            wup_s[g] = jnp.dot(wup_ref[:, cols].astype(BF16), perm, preferred_element_type=F32).astype(BF16)
        for g in range(D_MODEL // pair):
            wdn_s[g] = wdn_ref[:, g * pair:(g + 1) * pair].astype(BF16)

        def block(i, carry):
            slot = i % 2
            x_copy(i, slot).wait()

            @pl.when(i + 1 < nb)
            def _():
                x_copy(i + 1, 1 - slot).start(priority=1)

            @pl.when(i >= 2)
            def _():
                y_copy(i - 2, slot).wait()

            @pl.when(jnp.logical_and(more, i < W_PIECES))
            def _():
                w_start(e + 1, 1 - wslot, i)

            row = lax.broadcasted_iota(jnp.int32, (tm, 1), 0)
            x = jnp.where(row < cnt - i * tm, _unpack_bf16_pair(xbuf[slot]), 0.0).astype(BF16)
            acts = []
            for g in range(2 * D_MODEL // pair):
                cols = slice(g * pair, (g + 1) * pair)
                h = jnp.dot(x, wup_s[g], preferred_element_type=F32) + bup_ref[0, :, cols]
                glu = jnp.minimum(h[:, :LANES], SWIGLU_LIMIT)
                lin = jnp.clip(h[:, LANES:], -SWIGLU_LIMIT, SWIGLU_LIMIT)
                acts.append((glu * jax.nn.sigmoid(SWIGLU_ALPHA * glu) * (lin + 1.0)).astype(BF16))
            act = jnp.concatenate(acts, axis=1)
            half_groups = D_MODEL // pair // 2
            for g in range(half_groups):
                ys = []
                for gg in (g, g + half_groups):
                    cols = slice(gg * pair, (gg + 1) * pair)
                    ys.append(jnp.dot(act, wdn_s[gg], preferred_element_type=F32) + bdn_ref[0, :, cols])
                obuf[slot, :, g * pair:(g + 1) * pair] = _pack_bf16_pair(ys[0], ys[1])
            y_copy(i, slot).start(priority=1)
            return carry

        lax.fori_loop(0, nb, block, 0)

        @pl.when(nb >= 2)
        def _():
            y_copy(nb - 2, nb % 2).wait()

        y_copy(nb - 1, (nb - 1) % 2).wait()

    for p in range(W_PIECES):
        @pl.when(jnp.logical_and(more, p >= nb))
        def _():
            w_start(e + 1, 1 - wslot, p)


def _experts(blk0, nblk, cnt, x_sorted, w_up, w_down, b_up_grouped, b_down):
    tm = EXPERT_ROWS
    per_expert = lambda a: pl.BlockSpec((1,) + a.shape[1:], lambda e, b0, nb, ct: (e, 0, 0))
    hbm = pl.BlockSpec(memory_space=pl.ANY)
    grid_spec = pltpu.PrefetchScalarGridSpec(
        num_scalar_prefetch=3,
        grid=(N_EXPERTS,),
        in_specs=[hbm, hbm, per_expert(b_up_grouped), per_expert(b_down), hbm],
        out_specs=hbm,
        scratch_shapes=[pltpu.VMEM((2,) + w_up.shape[1:], F32), pltpu.VMEM((2,) + w_down.shape[1:], F32),
                        pltpu.VMEM((2 * D_MODEL // (2 * LANES), D_MODEL, 2 * LANES), BF16),
                        pltpu.VMEM((D_MODEL // (2 * LANES), D_MODEL, 2 * LANES), BF16),
                        pltpu.VMEM((2, tm, x_sorted.shape[1]), x_sorted.dtype),
                        pltpu.VMEM((2, tm, D_MODEL // 2), jnp.uint32),
                        pltpu.SemaphoreType.DMA((2,)), pltpu.SemaphoreType.DMA((2,)), pltpu.SemaphoreType.DMA((2,))],
    )
    return pl.pallas_call(
        _expert_body,
        grid_spec=grid_spec,
        out_shape=jax.ShapeDtypeStruct((x_sorted.shape[0], D_MODEL // 2), jnp.uint32),
        compiler_params=_params(("arbitrary",)),
        name="experts",
    )(blk0, nblk, cnt, w_up, w_down, b_up_grouped, b_down, x_sorted)


def _combine_body(gate_ref, xmid_ref, gfin_ref, y0_ref, y1_ref, y2_ref, y3_ref, out_ref):
    gate = gate_ref[...]
    moe = _unpack_bf16_pair(y0_ref[...]) * gate[:, 0:1]
    for kx, y_ref in enumerate((y1_ref, y2_ref, y3_ref), start=1):
        moe = moe + _unpack_bf16_pair(y_ref[...]) * gate[:, kx:kx + 1]
    out_ref[...] = _rms(xmid_ref[...] + moe, gfin_ref[...])


def _combine(gates, first_token, xmid, gfin, y_rows):
    n = xmid.shape[0]
    tt = min(n, COMBINE_ROWS)
    nblk = n // tt
    blk0 = first_token // tt
    assert blk0 * tt == first_token
    choice = lambda kx: pl.BlockSpec((tt, y_rows.shape[1]), lambda i: (i + kx * nblk, 0))
    return pl.pallas_call(
        _combine_body,
        grid=(nblk,),
        in_specs=[pl.BlockSpec((tt, LANES), lambda i: (i + blk0, 0)),
                  pl.BlockSpec((tt, D_MODEL), lambda i: (i, 0)),
                  pl.BlockSpec((1, D_MODEL), lambda i: (0, 0))] + [choice(kx) for kx in range(TOP_K)],
        out_specs=pl.BlockSpec((tt, D_MODEL), lambda i: (i, 0)),
        out_shape=jax.ShapeDtypeStruct((n, D_MODEL), F32),
        compiler_params=_params(("arbitrary",)),
        name="combine",
    )(gates, xmid, gfin, y_rows, y_rows, y_rows, y_rows)


def kernel(x_prompt, x_sample, cache_k_win, cache_v_win, norm_attn_g, w_in, ln_v_g, ln_v_b, w_spatial, b_spatial,
           attn_sinks, w_out, norm_ffn_g, w_router, b_router, w_up, b_up, w_down, b_down, norm_final_g):
    bp, tp, _ = x_prompt.shape
    bs, ts, _ = x_sample.shape
    w_buf = cache_k_win.shape[2]
    assert bp == 1 and tp % MIX_ROWS == 0 and w_buf == WINDOW and (bs * ts) % PROJ_ROWS == 0 and 8 % ts == 0
    n_p, n_s = bp * tp, bs * ts
    row2 = lambda a: a.reshape(1, -1)

    w_out_bf = w_out[0].astype(BF16)
    tril = jnp.tril(jnp.ones((CHUNK, CHUNK), dtype=bool))
    wsp = jnp.where(tril[None], w_spatial[0], 0.0)
    wsp_bf = wsp.astype(BF16)
    bsp = jnp.broadcast_to(b_spatial[0][:, :, None], (GMLP_GROUPS, CHUNK, LANES))
    b_up_grouped = b_up[0].reshape(N_EXPERTS, -1, LANES, 2).transpose(0, 1, 3, 2).reshape(N_EXPERTS, 1, -1)
    bd = b_down[0][:, None, :]
    g1, g2, gfin = row2(norm_attn_g[0]), row2(norm_ffn_g[0]), row2(norm_final_g)
    lng, lnb = row2(ln_v_g[0]), row2(ln_v_b[0])
    wr_hi = w_router[0].astype(BF16)
    wr = jnp.concatenate([wr_hi, (w_router[0] - wr_hi.astype(F32)).astype(BF16)], axis=1)
    br = row2(b_router[0])
    sinks = attn_sinks[0]

    xp = x_prompt.reshape(n_p, D_MODEL)
    xs = x_sample.reshape(n_s, D_MODEL)
    cs_p = _rotary_inputs(jnp.arange(tp, dtype=jnp.int32))
    cs_s = _rotary_inputs(jnp.tile(PAST_LEN + jnp.arange(ts, dtype=jnp.int32), bs))
    projected = _proj(xp, xs, g1, w_in[0], cs_p, cs_s, lng, lnb)
    q_a, k_a, v_a, a_a, vn_a, sgb_a = projected
    xmid_p, xn2_p, lg_p = _mix(sinks, q_a, k_a, v_a, a_a, vn_a, sgb_a, xp, wsp_bf, bsp, w_out_bf, g2, wr, br)

    q_s, k_s, v_s, a_s, vn_s, sgb_s = (t[n_p:] for t in projected)
    n_keys = w_buf + ts
    key_pad = jnp.zeros((bs, (-n_keys) % 8, KV_WIDTH), F32)
    with_new = lambda cache, new: jnp.concatenate(
        [cache[0].reshape(bs, w_buf, KV_WIDTH), new.reshape(bs, ts, KV_WIDTH), key_pad], axis=1)
    k_all = with_new(cache_k_win, k_s)
    v_all = with_new(cache_v_win, v_s)
    sink_col = jnp.tile(sinks, ts).reshape(ts * N_HEADS, 1)
    o_s = _sample_attn(sink_col, q_s.reshape(bs, ts * N_HEADS, HEAD_DIM), k_all, v_all).reshape(n_s, Q_WIDTH)
    lag = np.arange(ts)[:, None] - np.arange(ts)[None, :]
    coef = jnp.stack([jnp.sum(jnp.where(lag == d, wsp[:, :ts, :ts], 0.0), axis=2)
                      for d in range(ts)])
    coef = jnp.repeat(coef.transpose(0, 2, 1), GMLP_WIDTH // GMLP_GROUPS, axis=2)
    coef = jnp.tile(coef, (1, 8 // ts, 1))
    bias = jnp.tile(jnp.repeat(b_spatial[0][:, :ts].T, GMLP_WIDTH // GMLP_GROUPS, axis=1), (8 // ts, 1))
    xmid_s, xn2_s, lg_s = _mix_sample(a_s, vn_s, sgb_s, o_s, xs, coef, bias, w_out_bf, g2, wr, br)

    tm = EXPERT_ROWS
    n_blocks = (n_p * TOP_K + N_EXPERTS * n_s) // tm + N_EXPERTS
    gate_p, dest_p, table_p = _route(lg_p, jnp.zeros((N_EXPERTS, LANES), F32), n_s)
    x_sorted = _sc_dispatch(xn2_p, dest_p, n_blocks * tm)
    gate_s, dest_s, table = _route(lg_s, table_p, None)
    x_sorted = _dispatch_small(xn2_s, dest_s, x_sorted)
    dest_t = jnp.concatenate([dest_p, dest_s], axis=1)
    meta = table.astype(jnp.int32)
    y_sorted = _experts(meta[:, 2], meta[:, 3], meta[:, 1], x_sorted, w_up[0], w_down[0], b_up_grouped, bd)
    yrows_p, yrows_s = _sc_collect(y_sorted, dest_t, n_p, n_s)
    y_p = _combine(gate_p, 0, xmid_p, gfin, yrows_p)
    y_s = _combine(gate_s, 0, xmid_s, gfin, yrows_s)

    k4 = lambda t: t.reshape(1, bp, -1, N_KV_HEADS, HEAD_DIM)
    return (y_p.reshape(bp, tp, D_MODEL),
            y_s.reshape(bs, ts, D_MODEL),
            k4(k_a[n_p - WINDOW:n_p]),
            k4(v_a[n_p - WINDOW:n_p]),
            vn_a[n_p - CHUNK:n_p].reshape(1, bp, CHUNK, GMLP_WIDTH),
            k_all[:, ts:n_keys].reshape(1, bs, w_buf, N_KV_HEADS, HEAD_DIM),
            v_all[:, ts:n_keys].reshape(1, bs, w_buf, N_KV_HEADS, HEAD_DIM),
            vn_s.reshape(1, bs, ts, GMLP_WIDTH))
```

```python
import functools

import numpy as np
import jax
import jax.numpy as jnp
from jax import lax
from jax.experimental import pallas as pl
from jax.experimental.pallas import tpu as pltpu
from jax.experimental.pallas import tpu_sc as plsc

F32 = jnp.float32
BF16 = jnp.bfloat16

D_MODEL = 1024
HEAD_DIM = 64
N_HEADS = 16
GQA_GROUP = 8
N_KV_HEADS = 2
Q_WIDTH = 1024
KV_WIDTH = 128
WINDOW = 128
ROT_DIM = 16
ROPE_THETA = 500000.0
CHUNK = 128
GMLP_WIDTH = 1024
GMLP_GROUPS = 8
N_EXPERTS = 32
TOP_K = 4
SWIGLU_LIMIT = 7.0
SWIGLU_ALPHA = 1.702
RMS_EPS = 1e-5
LN_EPS = 1e-5
NEG_INF = -1e30
PAST_LEN = 16384

LANES = 128
VMEM_LIMIT = 56 * 1024 * 1024

PROJ_ROWS = 256
MIX_ROWS = 512
ROUTE_ROWS = 512
EXPERT_ROWS = 256
W_PIECES = 8
COMBINE_ROWS = 1024

SC_CORES = 2
SC_WORKERS = 32
SC_ROWS = 64

_C_Q, _C_KV, _C_U, _C_VG, _C_GA, _C_GB, _C_END = 0, 1024, 1280, 2304, 3328, 4352, 5376


def _params(sem):
    return pltpu.CompilerParams(dimension_semantics=sem, vmem_limit_bytes=VMEM_LIMIT)


def _rms(x, g):
    return x * lax.rsqrt(jnp.mean(x * x, axis=-1, keepdims=True) + RMS_EPS) * g


def _pack_bf16_pair(lo, hi):
    lo_bits = lax.bitcast_convert_type(lo.astype(BF16).astype(F32), jnp.uint32)
    hi_bits = lax.bitcast_convert_type(hi.astype(BF16).astype(F32), jnp.uint32)
    return (lo_bits >> 16) | hi_bits


def _unpack_bf16_pair(words):
    lo = lax.bitcast_convert_type(words << 16, F32)
    hi = lax.bitcast_convert_type(words & jnp.uint32(0xFFFF0000), F32)
    return jnp.concatenate([lo, hi], axis=1)


def _proj_body(x_ref, g_ref, w_ref, cs_ref, rot_ref, lng_ref, lnb_ref,
               q_ref, k_ref, v_ref, a_ref, vn_ref, sgb_ref):
    h = _rms(x_ref[...], g_ref[...]).astype(BF16)
    tabs = lax.dot_general(cs_ref[...], rot_ref[...], (((0,), (0,)), ((), ())), preferred_element_type=F32)
    rc, rs1, rs2 = tabs[:, :LANES], tabs[:, LANES:2 * LANES], tabs[:, 2 * LANES:]

    def rot(z):
        return z * rc + pltpu.roll(z, LANES - ROT_DIM // 2, 1) * rs1 + pltpu.roll(z, ROT_DIM // 2, 1) * rs2

    def mm(lo, hi):
        return jnp.dot(h, w_ref[:, lo:hi], preferred_element_type=F32)

    zq = mm(_C_Q, _C_KV)
    for c in range(Q_WIDTH // LANES):
        sl = slice(c * LANES, (c + 1) * LANES)
        q_ref[:, sl] = (rot(zq[:, sl]) * (HEAD_DIM ** -0.5)).astype(BF16)
    zkv = mm(_C_KV, _C_U)
    k_ref[...] = rot(zkv[:, :KV_WIDTH])
    v_ref[...] = zkv[:, KV_WIDTH:]
    a_ref[...] = jax.nn.sigmoid(mm(_C_GA, _C_GB)) * jax.nn.gelu(mm(_C_U, _C_VG))
    zv = jax.nn.gelu(mm(_C_VG, _C_GA))
    zc = zv - jnp.mean(zv, axis=-1, keepdims=True)
    var = jnp.mean(zc * zc, axis=-1, keepdims=True)
    vn_ref[...] = zc * lax.rsqrt(var + LN_EPS) * lng_ref[...] + lnb_ref[...]
    sgb_ref[...] = jax.nn.sigmoid(mm(_C_GB, _C_END))


def _proj(x, norm_g, w_in_bf, cs, ln_g, ln_b):
    n = x.shape[0]
    tm = PROJ_ROWS
    row = lambda w: pl.BlockSpec((tm, w), lambda i: (i, 0))
    full = lambda a: pl.BlockSpec(a.shape, lambda i: (0,) * a.ndim)
    rot = jnp.asarray(np.tile(_ROT_EXPAND, (3, 1)), dtype=BF16)
    return pl.pallas_call(
        _proj_body,
        grid=(n // tm,),
        in_specs=[row(D_MODEL), full(norm_g), full(w_in_bf), pl.BlockSpec((cs.shape[0], tm), lambda i: (0, i)),
                  full(rot),
                  full(ln_g), full(ln_b)],
        out_specs=[row(Q_WIDTH), row(KV_WIDTH), row(KV_WIDTH), row(GMLP_WIDTH), row(GMLP_WIDTH), row(D_MODEL)],
        out_shape=[jax.ShapeDtypeStruct((n, Q_WIDTH), BF16),
                   jax.ShapeDtypeStruct((n, KV_WIDTH), F32),
                   jax.ShapeDtypeStruct((n, KV_WIDTH), F32),
                   jax.ShapeDtypeStruct((n, GMLP_WIDTH), F32),
                   jax.ShapeDtypeStruct((n, GMLP_WIDTH), F32),
                   jax.ShapeDtypeStruct((n, D_MODEL), F32)],
        compiler_params=_params(("arbitrary",)),
        name="proj",
    )(x, norm_g, w_in_bf, cs, rot, ln_g, ln_b)


_ROT_COLS = 32


def _rot_expand():
    half = ROT_DIM // 2
    m = np.zeros((_ROT_COLS, 3 * LANES), np.float32)
    for lane in range(LANES):
        d = lane % HEAD_DIM
        if d < ROT_DIM:
            m[d % half, lane] = 1.0
        else:
            m[2 * half, lane] = 1.0
        if d < half:
            m[half + d, LANES + lane] = -1.0
        elif d < ROT_DIM:
            m[half + d - half, 2 * LANES + lane] = 1.0
    return m


_ROT_EXPAND = _rot_expand()


def _rotary_inputs(pos):
    half = ROT_DIM // 2
    inv_freq = ROPE_THETA ** (-jnp.arange(half, dtype=F32) / half)
    ang = inv_freq[:, None] * pos.astype(F32)[None, :]
    n = pos.shape[0]
    cs = jnp.concatenate([jnp.cos(ang), jnp.sin(ang), jnp.ones((1, n), F32),
                          jnp.zeros((_ROT_COLS - 2 * half - 1, n), F32)], axis=0)
    hi = cs.astype(BF16)
    rest = cs - hi.astype(F32)
    mid = rest.astype(BF16)
    lo = (rest - mid.astype(F32)).astype(BF16)
    return jnp.concatenate([hi, mid, lo], axis=0)


def _finish_rows(merged_bf, x, wout_ref, g2_ref, wr_ref, br_ref, xmid_ref, xn2_ref, lg_ref):
    xm = x + jnp.dot(merged_bf, wout_ref[...], preferred_element_type=F32)
    xmid_ref[...] = xm
    xn = _rms(xm, g2_ref[...])
    x_hi = xn.astype(BF16)
    x_lo = (xn - x_hi.astype(F32)).astype(BF16)
    w_hl = wr_ref[...]
    p_hi = jnp.dot(x_hi, w_hl, preferred_element_type=F32)
    p_lo = jnp.dot(x_lo, w_hl[:, :N_EXPERTS], preferred_element_type=F32)
    lg = p_hi[:, :N_EXPERTS] + (p_hi[:, N_EXPERTS:] + p_lo) + br_ref[...]
    wide = jnp.concatenate([lg, jnp.zeros((lg.shape[0], LANES - N_EXPERTS), F32)], axis=1)
    lg_ref[...] = wide.T[:N_EXPERTS]
    xn2_ref[...] = _pack_bf16_pair(xn[:, :D_MODEL // 2], xn[:, D_MODEL // 2:])


def _mix_body(sinks_ref, q_ref, k_ref, kp_ref, v_ref, vp_ref, a_ref, vn_ref, sgb_ref, x_ref,
              wsp_ref, bsp_ref, wout_ref, g2_ref, wr_ref, br_ref,
              xmid_ref, xn2_ref, lg_ref, kcat, vcat, mrg, key_s):
    i = pl.program_id(0)
    nsub = MIX_ROWS // WINDOW
    kcat[0:WINDOW] = kp_ref[...]
    kcat[WINDOW:] = k_ref[...]
    vcat[0:WINDOW] = vp_ref[...]
    vcat[WINDOW:] = v_ref[...]

    pair_rows = (GQA_GROUP // 2) * WINDOW
    lane_kv = lax.broadcasted_iota(jnp.int32, (2 * WINDOW, LANES), 1)
    lane_o = lax.broadcasted_iota(jnp.int32, (pair_rows, LANES), 1)

    @pl.when(i == 0)
    def _():
        rq = lax.broadcasted_iota(jnp.int32, (pair_rows, 4 * WINDOW), 0) & (WINDOW - 1)
        ck = lax.broadcasted_iota(jnp.int32, (pair_rows, 4 * WINDOW), 1) & (2 * WINDOW - 1)
        key_s[...] = jnp.where((ck > rq) & (ck <= rq + WINDOW), ck, -1)

    row_p = lax.broadcasted_iota(jnp.int32, (pair_rows, 1), 0) >> 7
    sink_cols = []
    for kk in range(N_KV_HEADS):
        h0 = kk * GQA_GROUP
        se = jnp.full((pair_rows, 1), sinks_ref[h0], F32)
        so = jnp.full((pair_rows, 1), sinks_ref[h0 + 1], F32)
        for p in range(1, GQA_GROUP // 2):
            se = jnp.where(row_p == p, sinks_ref[h0 + 2 * p], se)
            so = jnp.where(row_p == p, sinks_ref[h0 + 2 * p + 1], so)
        sink_cols.append((se, so))

    def sub(j, carry):
        off = pl.multiple_of(j * WINDOW, WINDOW)
        rows = pl.ds(off, WINDOW)
        for g in range(GMLP_GROUPS):
            cols = slice(g * LANES, (g + 1) * LANES)
            s = jnp.dot(wsp_ref[g], vn_ref[rows, cols].astype(BF16), preferred_element_type=F32) + bsp_ref[g]
            mrg[rows, cols] = a_ref[rows, cols] * s
        kblk = kcat[pl.ds(off, 2 * WINDOW), :]
        vblk = vcat[pl.ds(off, 2 * WINDOW), :]
        kswp = pltpu.roll(kblk, HEAD_DIM, 1)
        vswp = pltpu.roll(vblk, HEAD_DIM, 1)
        kmin = jnp.where(jnp.logical_and(i == 0, j == 0), WINDOW, 0)
        allowed = key_s[...] >= kmin
        for kk in range(N_KV_HEADS):
            lo_src, hi_src = (kblk, kswp) if kk == 0 else (kswp, kblk)
            kbd = jnp.concatenate([jnp.where(lane_kv < HEAD_DIM, lo_src, 0.0),
                                   jnp.where(lane_kv >= HEAD_DIM, hi_src, 0.0)], axis=0).astype(BF16)
            lo_src, hi_src = (vblk, vswp) if kk == 0 else (vswp, vblk)
            vbd = jnp.concatenate([jnp.where(lane_kv < HEAD_DIM, lo_src, 0.0),
                                   jnp.where(lane_kv >= HEAD_DIM, hi_src, 0.0)], axis=0).astype(BF16)
            pair0 = kk * (GQA_GROUP // 2)
            qs = jnp.concatenate([q_ref[rows, (pair0 + p) * LANES:(pair0 + p + 1) * LANES]
                                  for p in range(GQA_GROUP // 2)], axis=0)
            lg = lax.dot_general(qs, kbd, (((1,), (1,)), ((), ())), preferred_element_type=F32)
            lg = jnp.where(allowed, lg, NEG_INF)
            se, so = sink_cols[kk]
            le, lo = lg[:, :2 * WINDOW], lg[:, 2 * WINDOW:]
            me = jnp.maximum(jnp.max(le, axis=1, keepdims=True), se)
            mo = jnp.maximum(jnp.max(lo, axis=1, keepdims=True), so)
            pe = jnp.exp(le - me)
            po = jnp.exp(lo - mo)
            de = jnp.sum(pe, axis=1, keepdims=True) + jnp.exp(se - me)
            do = jnp.sum(po, axis=1, keepdims=True) + jnp.exp(so - mo)
            pr = jnp.concatenate([pe, po], axis=1).astype(BF16)
            o = jnp.dot(pr, vbd, preferred_element_type=F32)
            o = o / jnp.where(lane_o < HEAD_DIM, de, do)
            for p in range(GQA_GROUP // 2):
                cols = slice((pair0 + p) * LANES, (pair0 + p + 1) * LANES)
                mrg[rows, cols] += sgb_ref[rows, cols] * o[p * WINDOW:(p + 1) * WINDOW]
        return carry

    lax.fori_loop(0, nsub, sub, 0)
    _finish_rows(mrg[...].astype(BF16), x_ref[...], wout_ref, g2_ref, wr_ref, br_ref, xmid_ref, xn2_ref, lg_ref)


def _mix(sinks, q, k, v, a, vn, sgb, x, wsp, bsp, wout, g2, wr, br):
    n = x.shape[0]
    tm = MIX_ROWS
    nsub = tm // WINDOW
    row = lambda w: pl.BlockSpec((tm, w), lambda i: (i, 0))
    prev = pl.BlockSpec((WINDOW, KV_WIDTH), lambda i: (jnp.maximum(i * nsub - 1, 0), 0))
    full = lambda arr: pl.BlockSpec(arr.shape, lambda i: (0,) * arr.ndim)
    smem = pl.BlockSpec(memory_space=pltpu.SMEM)
    return pl.pallas_call(
        _mix_body,
        grid=(n // tm,),
        in_specs=[smem, row(Q_WIDTH), row(KV_WIDTH), prev, row(KV_WIDTH), prev,
                  row(GMLP_WIDTH), row(GMLP_WIDTH), row(D_MODEL), row(D_MODEL),
                  full(wsp), full(bsp), full(wout), full(g2), full(wr), full(br)],
        out_specs=[row(D_MODEL), row(D_MODEL // 2), pl.BlockSpec((N_EXPERTS, tm), lambda i: (0, i))],
        out_shape=[jax.ShapeDtypeStruct((n, D_MODEL), F32),
                   jax.ShapeDtypeStruct((n, D_MODEL // 2), jnp.uint32),
                   jax.ShapeDtypeStruct((N_EXPERTS, n), F32)],
        scratch_shapes=[pltpu.VMEM((tm + WINDOW, KV_WIDTH), F32),
                        pltpu.VMEM((tm + WINDOW, KV_WIDTH), F32),
                        pltpu.VMEM((tm, D_MODEL), F32),
                        pltpu.VMEM((GQA_GROUP // 2 * WINDOW, 4 * WINDOW), jnp.int32)],
        compiler_params=_params(("arbitrary",)),
        name="mix_prompt",
    )(sinks, q, k, k, v, v, a, vn, sgb, x, wsp, bsp, wout, g2, wr, br)


def _sample_attn_body(sink_ref, q_ref, k_ref, v_ref, o_ref):
    q = q_ref[...]
    k = k_ref[...]
    v = v_ref[...]
    nq, nk = q.shape[1], k.shape[1]
    heads = lambda t, kk: t[:, :, kk * HEAD_DIM:(kk + 1) * HEAD_DIM].astype(BF16)
    row = lax.broadcasted_iota(jnp.int32, (1, nq, nk), 1)
    first_kv = (row % N_HEADS) < GQA_GROUP
    lg = jnp.where(first_kv,
                   jnp.einsum("bqd,bkd->bqk", q, heads(k, 0), preferred_element_type=F32),
                   jnp.einsum("bqd,bkd->bqk", q, heads(k, 1), preferred_element_type=F32))
    t = row // N_HEADS
    j = lax.broadcasted_iota(jnp.int32, (1, nq, nk), 2)
    lg = jnp.where((j > t) & (j <= t + WINDOW), lg, NEG_INF)
    sink = sink_ref[...][None]
    m = jnp.maximum(jnp.max(lg, axis=2, keepdims=True), sink)
    p = jnp.exp(lg - m)
    den = jnp.sum(p, axis=2, keepdims=True) + jnp.exp(sink - m)
    pb = p.astype(BF16)
    row_o = lax.broadcasted_iota(jnp.int32, (1, nq, HEAD_DIM), 1)
    o = jnp.where((row_o % N_HEADS) < GQA_GROUP,
                  jnp.einsum("bqk,bkd->bqd", pb, heads(v, 0), preferred_element_type=F32),
                  jnp.einsum("bqk,bkd->bqd", pb, heads(v, 1), preferred_element_type=F32))
    o_ref[...] = o / den


def _sample_attn(sink_col, q3, k_all, v_all):
    nb = q3.shape[0]
    bb = 32
    blk = lambda a: pl.BlockSpec((bb,) + a.shape[1:], lambda b: (b, 0, 0))
    return pl.pallas_call(
        _sample_attn_body,
        grid=(nb // bb,),
        in_specs=[pl.BlockSpec(sink_col.shape, lambda b: (0, 0)), blk(q3), blk(k_all), blk(v_all)],
        out_specs=blk(q3),
        out_shape=jax.ShapeDtypeStruct(q3.shape, F32),
        compiler_params=_params(("arbitrary",)),
        name="attn_sample",
    )(sink_col, q3, k_all, v_all)


def _mix_sample_body(a_ref, vn_ref, sgb_ref, o_ref, x_ref, coef_ref, bias_ref,
                     wout_ref, g2_ref, wr_ref, br_ref, xmid_ref, xn2_ref, lg_ref):
    vn = vn_ref[...]
    n, width = vn.shape
    rows8 = lambda t: t.reshape(n // 8, 8, width)
    s = bias_ref[...][None] + coef_ref[0][None] * rows8(vn)
    for d in range(1, coef_ref.shape[0]):
        s = s + coef_ref[d][None] * rows8(pltpu.roll(vn, d, 0))
    merged = a_ref[...] * s.reshape(n, width) + sgb_ref[...] * o_ref[...]
    _finish_rows(merged.astype(BF16), x_ref[...], wout_ref, g2_ref, wr_ref, br_ref, xmid_ref, xn2_ref, lg_ref)


def _mix_sample(a, vn, sgb, o, x, coef, bias, wout, g2, wr, br):
    n = x.shape[0]
    args = (a, vn, sgb, o, x, coef, bias, wout, g2, wr, br)
    full = lambda arr: pl.BlockSpec(arr.shape, lambda i: (0,) * arr.ndim)
    return pl.pallas_call(
        _mix_sample_body,
        grid=(1,),
        in_specs=[full(arr) for arr in args],
        out_specs=[pl.BlockSpec((n, D_MODEL), lambda i: (0, 0)), pl.BlockSpec((n, D_MODEL // 2), lambda i: (0, 0)),
                   pl.BlockSpec((N_EXPERTS, n), lambda i: (0, 0))],
        out_shape=[jax.ShapeDtypeStruct((n, D_MODEL), F32),
                   jax.ShapeDtypeStruct((n, D_MODEL // 2), jnp.uint32),
                   jax.ShapeDtypeStruct((N_EXPERTS, n), F32)],
        compiler_params=_params(("arbitrary",)),
        name="mix_sample",
    )(*args)


def _rows8(rows, dtype):
    n = rows[0].shape[1]
    sub = lax.broadcasted_iota(jnp.int32, (8, n), 0)
    out = jnp.zeros((8, n), dtype)
    for kx, r in enumerate(rows):
        out = jnp.where(sub == kx, r.astype(dtype), out)
    return out


def _route_body(nblk, reserve, lg_ref, prior_ref, gate_ref, dest_ref, meta_ref, idx_s, rank_s, base):
    i = pl.program_id(0)

    @pl.when(i == 0)
    def _():
        base[...] = jnp.zeros_like(base)

    l = lg_ref[...]
    tb = l.shape[1]
    sub = lax.broadcasted_iota(jnp.int32, l.shape, 0).astype(F32)
    vals, idxs, sels = [], [], []
    for _ in range(TOP_K):
        m = jnp.max(l, axis=0, keepdims=True)
        ik = jnp.min(jnp.where(l == m, sub, float(N_EXPERTS)), axis=0, keepdims=True)
        sel = sub == ik
        l = jnp.where(sel, -jnp.inf, l)
        vals.append(m)
        idxs.append(ik)
        sels.append(sel)
    es = [jnp.exp(vk - vals[0]) for vk in vals]
    den = es[0] + es[1] + es[2] + es[3]
    onehot = jnp.zeros(l.shape, F32)
    for sel in sels:
        onehot = onehot + sel.astype(F32)
    earlier = (lax.broadcasted_iota(jnp.int32, (tb, tb), 0) < lax.broadcasted_iota(jnp.int32, (tb, tb), 1))
    before = jnp.dot(onehot.astype(BF16), earlier.astype(BF16), preferred_element_type=F32) + base[...]
    ranks = [jnp.sum(jnp.where(sel, before, 0.0), axis=0, keepdims=True) for sel in sels]
    base[...] += jnp.sum(onehot, axis=1, keepdims=True)
    idx_s[i] = _rows8(idxs, F32)
    rank_s[i] = _rows8(ranks, F32)
    gates = jnp.concatenate([_rows8([e / den for e in es], F32), jnp.zeros((LANES - 8, tb), F32)], axis=0)
    gate_ref[...] = gates.T

    @pl.when(i == nblk - 1)
    def _():
        cnt = base[...]
        if reserve is None:
            seg0 = prior_ref[:, 0:1]
            placed = prior_ref[:, 1:2]
        else:
            cap = jnp.ceil((cnt + reserve) / EXPERT_ROWS) * EXPERT_ROWS
            lower = (lax.broadcasted_iota(jnp.int32, (N_EXPERTS, N_EXPERTS), 1) <
                     lax.broadcasted_iota(jnp.int32, (N_EXPERTS, N_EXPERTS), 0)).astype(F32)
            seg0 = jnp.dot(lower, jnp.broadcast_to(cap, (N_EXPERTS, LANES)), preferred_element_type=F32,
                           precision=lax.Precision.HIGHEST)[:, :1]
            placed = jnp.zeros_like(cnt)
        total = placed + cnt
        lane = lax.broadcasted_iota(jnp.int32, (N_EXPERTS, LANES), 1)
        meta_ref[...] = jnp.where(lane == 0, seg0, jnp.where(lane == 1, total, jnp.where(
            lane == 2, seg0 / EXPERT_ROWS, jnp.where(lane == 3, jnp.ceil(total / EXPERT_ROWS), 0.0))))
        first = seg0 + placed
        sub_e = lax.broadcasted_iota(jnp.int32, (N_EXPERTS, tb), 0).astype(F32)
        for b in range(nblk):
            idx, rank = idx_s[b], rank_s[b]
            rows = [jnp.sum(jnp.where(sub_e == idx[kx:kx + 1], first, 0.0), axis=0, keepdims=True)
                    + rank[kx:kx + 1] for kx in range(TOP_K)]
            dest_ref[:, b * tb:(b + 1) * tb] = _rows8(rows, jnp.int32)


def _route(logits, prior, reserve):
    tb = ROUTE_ROWS
    n = logits.shape[1]
    nblk = n // tb
    assert nblk * tb == n
    table = pl.BlockSpec((N_EXPERTS, LANES), lambda i: (0, 0))
    return pl.pallas_call(
        functools.partial(_route_body, nblk, reserve),
        grid=(nblk,),
        in_specs=[pl.BlockSpec((N_EXPERTS, tb), lambda i: (0, i)), table],
        out_specs=[pl.BlockSpec((tb, LANES), lambda i: (i, 0)), pl.BlockSpec((8, n), lambda i: (0, 0)), table],
        out_shape=[jax.ShapeDtypeStruct((n, LANES), F32),
                   jax.ShapeDtypeStruct((8, n), jnp.int32),
                   jax.ShapeDtypeStruct((N_EXPERTS, LANES), F32)],
        scratch_shapes=[pltpu.VMEM((nblk, 8, tb), F32), pltpu.VMEM((nblk, 8, tb), F32),
                        pltpu.VMEM((N_EXPERTS, 1), F32)],
        compiler_params=_params(("arbitrary",)),
        name="route",
    )(logits, prior)


def _sc_mesh():
    return plsc.VectorSubcoreMesh(core_axis_name="c", subcore_axis_name="s")


def _sc_worker():
    return lax.axis_index("s") * SC_CORES + lax.axis_index("c")


def _sc_dispatch(x, dest_t, n_slots):
    chunk = SC_ROWS
    n = x.shape[0]
    per_w = n // (SC_WORKERS * chunk)
    assert per_w * SC_WORKERS * chunk == n and per_w % 2 == 0
    d3 = dest_t.reshape(dest_t.shape[0], n // chunk, chunk)
    width, dtype = x.shape[1], x.dtype

    @functools.partial(
        pl.kernel, mesh=_sc_mesh(),
        out_type=jax.ShapeDtypeStruct((n_slots, width), dtype),
        scratch_types=[pltpu.VMEM((TOP_K, per_w, chunk), jnp.int32),
                       pltpu.VMEM((2, chunk, width), dtype),
                       pltpu.SemaphoreType.DMA, pltpu.SemaphoreType.DMA],
        compiler_params=pltpu.CompilerParams(use_tc_tiling_on_sc=True),
        name="dispatch")
    def run(x_hbm, d_hbm, out_hbm, idx_v, rows_v, rsem, wsem):
        wid = _sc_worker()
        pltpu.sync_copy(d_hbm.at[pl.ds(0, TOP_K), pl.ds(wid * per_w, per_w)], idx_v)

        def read(j, slot):
            return pltpu.make_async_copy(x_hbm.at[pl.ds((wid * per_w + j) * chunk, chunk)], rows_v.at[slot], rsem)

        def scatter(j, slot):
            copies = [pltpu.async_copy(rows_v.at[slot], out_hbm.at[idx_v.at[kx, j]], wsem) for kx in range(TOP_K)]
            for cp in copies:
                cp.wait()

        read(0, 0).start()

        def body(h, carry):
            j = 2 * h
            read(j, 0).wait()
            read(j + 1, 1).start()
            scatter(j, 0)
            read(j + 1, 1).wait()

            @pl.when(j + 2 < per_w)
            def _():
                read(j + 2, 0).start()

            scatter(j + 1, 1)
            return carry

        lax.fori_loop(0, per_w // 2, body, 0)

    return run(x, d3)


def _dispatch_small_body(dest_ref, x_ref, _, out_hbm, sem):
    n = x_ref.shape[0]

    def body(t, carry):
        for kx in range(TOP_K):
            pltpu.make_async_copy(x_ref.at[pl.ds(t, 1)], out_hbm.at[pl.ds(dest_ref[kx, t], 1)], sem).start()
        return carry

    lax.fori_loop(0, n, body, 0, unroll=4)
    for _ in range(TOP_K):
        pltpu.make_async_copy(x_ref, out_hbm.at[pl.ds(0, n)], sem).wait()


def _dispatch_small(x, dest_t, x_sorted):
    return pl.pallas_call(
        _dispatch_small_body,
        grid=(1,),
        in_specs=[pl.BlockSpec(memory_space=pltpu.SMEM), pl.BlockSpec(x.shape, lambda i: (0, 0)),
                  pl.BlockSpec(memory_space=pl.ANY)],
        out_specs=pl.BlockSpec(memory_space=pl.ANY),
        out_shape=jax.ShapeDtypeStruct(x_sorted.shape, x_sorted.dtype),
        scratch_shapes=[pltpu.SemaphoreType.DMA],
        input_output_aliases={2: 0},
        compiler_params=_params(("arbitrary",)),
        name="dispatch_small",
    )(dest_t, x, x_sorted)


def _sc_collect(y_sorted, dest_t, n_p, n_s):
    chunk = SC_ROWS
    per_choice = SC_WORKERS // TOP_K
    per_w = n_p // (per_choice * chunk)
    assert per_w * per_choice * chunk == n_p and per_w % 2 == 0
    assert n_s == per_choice * chunk
    d3 = dest_t.reshape(dest_t.shape[0], (n_p + n_s) // chunk, chunk)
    width, dtype = y_sorted.shape[1], y_sorted.dtype

    @functools.partial(
        pl.kernel, mesh=_sc_mesh(),
        out_type=[jax.ShapeDtypeStruct((TOP_K * n_p, width), dtype), jax.ShapeDtypeStruct((TOP_K * n_s, width), dtype)],
        scratch_types=[pltpu.VMEM((per_w, chunk), jnp.int32),
                       pltpu.VMEM((1, chunk), jnp.int32),
                       pltpu.VMEM((2, chunk, width), dtype),
                       pltpu.SemaphoreType.DMA, pltpu.SemaphoreType.DMA],
        compiler_params=pltpu.CompilerParams(use_tc_tiling_on_sc=True),
        name="collect")
    def run(y_hbm, d_hbm, op_hbm, os_hbm, ip_v, is_v, rows_v, gsem, wsem):
        wid = _sc_worker()
        choice = wid // per_choice
        part = wid % per_choice
        pltpu.sync_copy(d_hbm.at[choice, pl.ds(part * per_w, per_w)], ip_v)
        pltpu.sync_copy(d_hbm.at[choice, pl.ds(n_p // chunk + part, 1)], is_v)

        def gather(idx_v, j, slot):
            return pltpu.make_async_copy(y_hbm.at[idx_v.at[j]], rows_v.at[slot], gsem)

        def write(j, slot):
            return pltpu.make_async_copy(rows_v.at[slot], op_hbm.at[pl.ds((wid * per_w + j) * chunk, chunk)], wsem)

        gather(ip_v, 0, 0).start()

        def body(h, carry):
            j = 2 * h
            gather(ip_v, j, 0).wait()

            @pl.when(h > 0)
            def _():
                write(j - 1, 1).wait()

            gather(ip_v, j + 1, 1).start()
            write(j, 0).start()
            gather(ip_v, j + 1, 1).wait()
            write(j, 0).wait()

            @pl.when(j + 2 < per_w)
            def _():
                gather(ip_v, j + 2, 0).start()

            write(j + 1, 1).start()
            return carry

        lax.fori_loop(0, per_w // 2, body, 0)
        write(per_w - 1, 1).wait()

        gather(is_v, 0, 0).start()
        gather(is_v, 0, 0).wait()
        pltpu.sync_copy(rows_v.at[0], os_hbm.at[pl.ds(wid * chunk, chunk)])

    return run(y_sorted, d3)


def _expert_body(blk0_ref, nblk_ref, cnt_ref, wup_hbm, wdn_hbm, bup_ref, bdn_ref, x_hbm, y_hbm,
                 wup_f, wdn_f, wup_s, wdn_s, xbuf, obuf, w_sem, in_sem, out_sem):
    e = pl.program_id(0)
    nb = nblk_ref[e]
    blk0 = blk0_ref[e]
    cnt = cnt_ref[e]
    tm = EXPERT_ROWS
    pair = 2 * LANES
    wslot = e % 2
    up_rows = D_MODEL // W_PIECES
    dn_rows = D_MODEL // (W_PIECES // 2)

    def w_piece(hbm, buf, ex, slot, p, rows):
        start = p * rows if isinstance(p, int) else pl.multiple_of(p * rows, rows)
        r = pl.ds(start, rows)
        return pltpu.make_async_copy(hbm.at[ex, r], buf.at[slot, r], w_sem.at[slot])

    def w_start(ex, slot, p):
        w_piece(wup_hbm, wup_f, ex, slot, p, up_rows).start()
        if isinstance(p, int):
            if p < W_PIECES // 2:
                w_piece(wdn_hbm, wdn_f, ex, slot, p, dn_rows).start()
        else:
            @pl.when(p < W_PIECES // 2)
            def _():
                w_piece(wdn_hbm, wdn_f, ex, slot, p, dn_rows).start()

    def w_wait(ex, slot):
        for p in range(W_PIECES):
            w_piece(wup_hbm, wup_f, ex, slot, p, up_rows).wait()
        for p in range(W_PIECES // 2):
            w_piece(wdn_hbm, wdn_f, ex, slot, p, dn_rows).wait()

    @pl.when(e == 0)
    def _():
        for p in range(W_PIECES):
            w_start(0, 0, p)

    w_wait(e, wslot)
    wup_ref = wup_f.at[wslot]
    wdn_ref = wdn_f.at[wslot]
    more = e + 1 < N_EXPERTS

    def x_copy(i, slot):
        rows = pl.ds(pl.multiple_of((blk0 + i) * tm, tm), tm)
        return pltpu.make_async_copy(x_hbm.at[rows], xbuf.at[slot], in_sem.at[slot])

    def y_copy(i, slot):
        rows = pl.ds(pl.multiple_of((blk0 + i) * tm, tm), tm)
        return pltpu.make_async_copy(obuf.at[slot], y_hbm.at[rows], out_sem.at[slot])

    @pl.when(nb > 0)
    def _():
        x_copy(0, 0).start(priority=1)
        r = lax.broadcasted_iota(jnp.int32, (pair, pair), 0)
        c = lax.broadcasted_iota(jnp.int32, (pair, pair), 1)
        perm = (r == jnp.where(c < LANES, 2 * c, 2 * (c - LANES) + 1)).astype(BF16)
        for g in range(2 * D_MODEL // pair):
            cols = slice(g * pair, (g + 1) * pair)
            wup_s[g] = jnp.dot(wup_ref[:, cols].astype(BF16), perm, preferred_element_type=F32).astype(BF16)
        for g in range(D_MODEL // pair):
            wdn_s[g] = wdn_ref[:, g * pair:(g + 1) * pair].astype(BF16)

        def block(i, carry):
            slot = i % 2
            x_copy(i, slot).wait()

            @pl.when(i + 1 < nb)
            def _():
                x_copy(i + 1, 1 - slot).start(priority=1)

            @pl.when(i >= 2)
            def _():
                y_copy(i - 2, slot).wait()

            @pl.when(jnp.logical_and(more, i < W_PIECES))
            def _():
                w_start(e + 1, 1 - wslot, i)

            row = lax.broadcasted_iota(jnp.int32, (tm, 1), 0)
            x = jnp.where(row < cnt - i * tm, _unpack_bf16_pair(xbuf[slot]), 0.0).astype(BF16)
            acts = []
            for g in range(2 * D_MODEL // pair):
                cols = slice(g * pair, (g + 1) * pair)
                h = jnp.dot(x, wup_s[g], preferred_element_type=F32) + bup_ref[0, :, cols]
                glu = jnp.minimum(h[:, :LANES], SWIGLU_LIMIT)
                lin = jnp.clip(h[:, LANES:], -SWIGLU_LIMIT, SWIGLU_LIMIT)
                acts.append((glu * jax.nn.sigmoid(SWIGLU_ALPHA * glu) * (lin + 1.0)).astype(BF16))
            act = jnp.concatenate(acts, axis=1)
            half_groups = D_MODEL // pair // 2
            for g in range(half_groups):
                ys = []
                for gg in (g, g + half_groups):
                    cols = slice(gg * pair, (gg + 1) * pair)
                    ys.append(jnp.dot(act, wdn_s[gg], preferred_element_type=F32) + bdn_ref[0, :, cols])
                obuf[slot, :, g * pair:(g + 1) * pair] = _pack_bf16_pair(ys[0], ys[1])
            y_copy(i, slot).start(priority=1)
            return carry

        lax.fori_loop(0, nb, block, 0)

        @pl.when(nb >= 2)
        def _():
            y_copy(nb - 2, nb % 2).wait()

        y_copy(nb - 1, (nb - 1) % 2).wait()

    for p in range(W_PIECES):
        @pl.when(jnp.logical_and(more, p >= nb))
        def _():
            w_start(e + 1, 1 - wslot, p)


def _experts(blk0, nblk, cnt, x_sorted, w_up, w_down, b_up_grouped, b_down):
    tm = EXPERT_ROWS
    per_expert = lambda a: pl.BlockSpec((1,) + a.shape[1:], lambda e, b0, nb, ct: (e, 0, 0))
    hbm = pl.BlockSpec(memory_space=pl.ANY)
    grid_spec = pltpu.PrefetchScalarGridSpec(
        num_scalar_prefetch=3,
        grid=(N_EXPERTS,),
        in_specs=[hbm, hbm, per_expert(b_up_grouped), per_expert(b_down), hbm],
        out_specs=hbm,
        scratch_shapes=[pltpu.VMEM((2,) + w_up.shape[1:], F32), pltpu.VMEM((2,) + w_down.shape[1:], F32),
                        pltpu.VMEM((2 * D_MODEL // (2 * LANES), D_MODEL, 2 * LANES), BF16),
                        pltpu.VMEM((D_MODEL // (2 * LANES), D_MODEL, 2 * LANES), BF16),
                        pltpu.VMEM((2, tm, x_sorted.shape[1]), x_sorted.dtype),
                        pltpu.VMEM((2, tm, D_MODEL // 2), jnp.uint32),
                        pltpu.SemaphoreType.DMA((2,)), pltpu.SemaphoreType.DMA((2,)), pltpu.SemaphoreType.DMA((2,))],
    )
    return pl.pallas_call(
        _expert_body,
        grid_spec=grid_spec,
        out_shape=jax.ShapeDtypeStruct((x_sorted.shape[0], D_MODEL // 2), jnp.uint32),
        compiler_params=_params(("arbitrary",)),
        name="experts",
    )(blk0, nblk, cnt, w_up, w_down, b_up_grouped, b_down, x_sorted)


def _combine_body(gate_ref, xmid_ref, gfin_ref, y0_ref, y1_ref, y2_ref, y3_ref, out_ref):
    gate = gate_ref[...]
    moe = _unpack_bf16_pair(y0_ref[...]) * gate[:, 0:1]
    for kx, y_ref in enumerate((y1_ref, y2_ref, y3_ref), start=1):
        moe = moe + _unpack_bf16_pair(y_ref[...]) * gate[:, kx:kx + 1]
    out_ref[...] = _rms(xmid_ref[...] + moe, gfin_ref[...])


def _combine(gates, first_token, xmid, gfin, y_rows):
    n = xmid.shape[0]
    tt = min(n, COMBINE_ROWS)
    nblk = n // tt
    blk0 = first_token // tt
    assert blk0 * tt == first_token
    choice = lambda kx: pl.BlockSpec((tt, y_rows.shape[1]), lambda i: (i + kx * nblk, 0))
    return pl.pallas_call(
        _combine_body,
        grid=(nblk,),
        in_specs=[pl.BlockSpec((tt, LANES), lambda i: (i + blk0, 0)),
                  pl.BlockSpec((tt, D_MODEL), lambda i: (i, 0)),
                  pl.BlockSpec((1, D_MODEL), lambda i: (0, 0))] + [choice(kx) for kx in range(TOP_K)],
        out_specs=pl.BlockSpec((tt, D_MODEL), lambda i: (i, 0)),
        out_shape=jax.ShapeDtypeStruct((n, D_MODEL), F32),
        compiler_params=_params(("arbitrary",)),
        name="combine",
    )(gates, xmid, gfin, y_rows, y_rows, y_rows, y_rows)


def kernel(x_prompt, x_sample, cache_k_win, cache_v_win, norm_attn_g, w_in, ln_v_g, ln_v_b, w_spatial, b_spatial,
           attn_sinks, w_out, norm_ffn_g, w_router, b_router, w_up, b_up, w_down, b_down, norm_final_g):
    bp, tp, _ = x_prompt.shape
    bs, ts, _ = x_sample.shape
    w_buf = cache_k_win.shape[2]
    assert bp == 1 and tp % MIX_ROWS == 0 and w_buf == WINDOW and (bs * ts) % PROJ_ROWS == 0 and 8 % ts == 0
    n_p, n_s = bp * tp, bs * ts
    row2 = lambda a: a.reshape(1, -1)

    w_in_bf = w_in[0].astype(BF16)
    w_out_bf = w_out[0].astype(BF16)
    tril = jnp.tril(jnp.ones((CHUNK, CHUNK), dtype=bool))
    wsp = jnp.where(tril[None], w_spatial[0], 0.0)
    wsp_bf = wsp.astype(BF16)
    bsp = jnp.broadcast_to(b_spatial[0][:, :, None], (GMLP_GROUPS, CHUNK, LANES))
    b_up_grouped = b_up[0].reshape(N_EXPERTS, -1, LANES, 2).transpose(0, 1, 3, 2).reshape(N_EXPERTS, 1, -1)
    bd = b_down[0][:, None, :]
    g1, g2, gfin = row2(norm_attn_g[0]), row2(norm_ffn_g[0]), row2(norm_final_g)
    lng, lnb = row2(ln_v_g[0]), row2(ln_v_b[0])
    wr_hi = w_router[0].astype(BF16)
    wr = jnp.concatenate([wr_hi, (w_router[0] - wr_hi.astype(F32)).astype(BF16)], axis=1)
    br = row2(b_router[0])
    sinks = attn_sinks[0]

    xp = x_prompt.reshape(n_p, D_MODEL)
    cs_p = _rotary_inputs(jnp.arange(tp, dtype=jnp.int32))
    q_p, k_p, v_p, a_p, vn_p, sgb_p = _proj(xp, g1, w_in_bf, cs_p, lng, lnb)
    xmid_p, xn2_p, lg_p = _mix(sinks, q_p, k_p, v_p, a_p, vn_p, sgb_p, xp, wsp_bf, bsp, w_out_bf, g2, wr, br)

    xs = x_sample.reshape(n_s, D_MODEL)
    pos_s = PAST_LEN + jnp.arange(ts, dtype=jnp.int32)
    cs_s = _rotary_inputs(jnp.tile(pos_s, bs))
    q_s, k_s, v_s, a_s, vn_s, sgb_s = _proj(xs, g1, w_in_bf, cs_s, lng, lnb)
    n_keys = w_buf + ts
    key_pad = jnp.zeros((bs, (-n_keys) % 8, KV_WIDTH), F32)
    with_new = lambda cache, new: jnp.concatenate(
        [cache[0].reshape(bs, w_buf, KV_WIDTH), new.reshape(bs, ts, KV_WIDTH), key_pad], axis=1)
    k_all = with_new(cache_k_win, k_s)
    v_all = with_new(cache_v_win, v_s)
    sink_col = jnp.tile(sinks, ts).reshape(ts * N_HEADS, 1)
    o_s = _sample_attn(sink_col, q_s.reshape(bs, ts * N_HEADS, HEAD_DIM), k_all, v_all).reshape(n_s, Q_WIDTH)
    lag = np.arange(ts)[:, None] - np.arange(ts)[None, :]
    coef = jnp.stack([jnp.sum(jnp.where(lag == d, wsp[:, :ts, :ts], 0.0), axis=2)
                      for d in range(ts)])
    coef = jnp.repeat(coef.transpose(0, 2, 1), GMLP_WIDTH // GMLP_GROUPS, axis=2)
    coef = jnp.tile(coef, (1, 8 // ts, 1))
    bias = jnp.tile(jnp.repeat(b_spatial[0][:, :ts].T, GMLP_WIDTH // GMLP_GROUPS, axis=1), (8 // ts, 1))
    xmid_s, xn2_s, lg_s = _mix_sample(a_s, vn_s, sgb_s, o_s, xs, coef, bias, w_out_bf, g2, wr, br)

    tm = EXPERT_ROWS
    n_blocks = (n_p * TOP_K + N_EXPERTS * n_s) // tm + N_EXPERTS
    gate_p, dest_p, table_p = _route(lg_p, jnp.zeros((N_EXPERTS, LANES), F32), n_s)
    x_sorted = _sc_dispatch(xn2_p, dest_p, n_blocks * tm)
    gate_s, dest_s, table = _route(lg_s, table_p, None)
    x_sorted = _dispatch_small(xn2_s, dest_s, x_sorted)
    dest_t = jnp.concatenate([dest_p, dest_s], axis=1)
    meta = table.astype(jnp.int32)
    y_sorted = _experts(meta[:, 2], meta[:, 3], meta[:, 1], x_sorted, w_up[0], w_down[0], b_up_grouped, bd)
    yrows_p, yrows_s = _sc_collect(y_sorted, dest_t, n_p, n_s)
    y_p = _combine(gate_p, 0, xmid_p, gfin, yrows_p)
    y_s = _combine(gate_s, 0, xmid_s, gfin, yrows_s)

    k4 = lambda t: t.reshape(1, bp, -1, N_KV_HEADS, HEAD_DIM)
    window = lambda cache, new: jnp.concatenate(
        [cache[:, :, ts:], new.reshape(1, bs, ts, N_KV_HEADS, HEAD_DIM)], axis=2)
    return (y_p.reshape(bp, tp, D_MODEL),
            y_s.reshape(bs, ts, D_MODEL),
            k4(k_p[n_p - WINDOW:]),
            k4(v_p[n_p - WINDOW:]),
            vn_p[n_p - CHUNK:].reshape(1, bp, CHUNK, GMLP_WIDTH),
            window(cache_k_win, k_s),
            window(cache_v_win, v_s),
            vn_s.reshape(1, bs, ts, GMLP_WIDTH))
```

```python
import functools

import numpy as np
import jax
import jax.numpy as jnp
from jax import lax
from jax.experimental import pallas as pl
from jax.experimental.pallas import tpu as pltpu
from jax.experimental.pallas import tpu_sc as plsc

F32 = jnp.float32
BF16 = jnp.bfloat16

D_MODEL = 1024
HEAD_DIM = 64
N_HEADS = 16
GQA_GROUP = 8
N_KV_HEADS = 2
Q_WIDTH = 1024
KV_WIDTH = 128
WINDOW = 128
ROT_DIM = 16
ROPE_THETA = 500000.0
CHUNK = 128
GMLP_WIDTH = 1024
GMLP_GROUPS = 8
N_EXPERTS = 32
TOP_K = 4
SWIGLU_LIMIT = 7.0
SWIGLU_ALPHA = 1.702
RMS_EPS = 1e-5
LN_EPS = 1e-5
NEG_INF = -1e30
PAST_LEN = 16384

LANES = 128
VMEM_LIMIT = 56 * 1024 * 1024

PROJ_ROWS = 256
MIX_ROWS = 512
ROUTE_ROWS = 1024
EXPERT_ROWS = 256
W_PIECES = 8
COMBINE_ROWS = 1024

SC_CORES = 2
SC_WORKERS = 32
SC_ROWS = 64

_C_Q, _C_KV, _C_U, _C_VG, _C_GA, _C_GB, _C_END = 0, 1024, 1280, 2304, 3328, 4352, 5376


def _params(sem):
    return pltpu.CompilerParams(dimension_semantics=sem, vmem_limit_bytes=VMEM_LIMIT)


def _rms(x, g):
    return x * lax.rsqrt(jnp.mean(x * x, axis=-1, keepdims=True) + RMS_EPS) * g


def _pack_bf16_pair(lo, hi):
    lo_bits = lax.bitcast_convert_type(lo.astype(BF16).astype(F32), jnp.uint32)
    hi_bits = lax.bitcast_convert_type(hi.astype(BF16).astype(F32), jnp.uint32)
    return (lo_bits >> 16) | hi_bits


def _unpack_bf16_pair(words):
    lo = lax.bitcast_convert_type(words << 16, F32)
    hi = lax.bitcast_convert_type(words & jnp.uint32(0xFFFF0000), F32)
    return jnp.concatenate([lo, hi], axis=1)


def _proj_body(x_ref, g_ref, w_ref, cs_ref, rot_ref, lng_ref, lnb_ref,
               q_ref, k_ref, v_ref, a_ref, vn_ref, sgb_ref):
    h = _rms(x_ref[...], g_ref[...]).astype(BF16)
    tabs = lax.dot_general(cs_ref[...], rot_ref[...], (((0,), (0,)), ((), ())), preferred_element_type=F32)
    rc, rs1, rs2 = tabs[:, :LANES], tabs[:, LANES:2 * LANES], tabs[:, 2 * LANES:]

    def rot(z):
        return z * rc + pltpu.roll(z, LANES - ROT_DIM // 2, 1) * rs1 + pltpu.roll(z, ROT_DIM // 2, 1) * rs2

    def mm(lo, hi):
        return jnp.dot(h, w_ref[:, lo:hi], preferred_element_type=F32)

    zq = mm(_C_Q, _C_KV)
    for c in range(Q_WIDTH // LANES):
        sl = slice(c * LANES, (c + 1) * LANES)
        q_ref[:, sl] = (rot(zq[:, sl]) * (HEAD_DIM ** -0.5)).astype(BF16)
    zkv = mm(_C_KV, _C_U)
    k_ref[...] = rot(zkv[:, :KV_WIDTH])
    v_ref[...] = zkv[:, KV_WIDTH:]
    a_ref[...] = jax.nn.sigmoid(mm(_C_GA, _C_GB)) * jax.nn.gelu(mm(_C_U, _C_VG))
    zv = jax.nn.gelu(mm(_C_VG, _C_GA))
    zc = zv - jnp.mean(zv, axis=-1, keepdims=True)
    var = jnp.mean(zc * zc, axis=-1, keepdims=True)
    vn_ref[...] = zc * lax.rsqrt(var + LN_EPS) * lng_ref[...] + lnb_ref[...]
    sgb_ref[...] = jax.nn.sigmoid(mm(_C_GB, _C_END))


def _proj(x, norm_g, w_in_bf, cs, ln_g, ln_b):
    n = x.shape[0]
    tm = PROJ_ROWS
    row = lambda w: pl.BlockSpec((tm, w), lambda i: (i, 0))
    full = lambda a: pl.BlockSpec(a.shape, lambda i: (0,) * a.ndim)
    rot = jnp.asarray(np.tile(_ROT_EXPAND, (3, 1)), dtype=BF16)
    return pl.pallas_call(
        _proj_body,
        grid=(n // tm,),
        in_specs=[row(D_MODEL), full(norm_g), full(w_in_bf), pl.BlockSpec((cs.shape[0], tm), lambda i: (0, i)),
                  full(rot),
                  full(ln_g), full(ln_b)],
        out_specs=[row(Q_WIDTH), row(KV_WIDTH), row(KV_WIDTH), row(GMLP_WIDTH), row(GMLP_WIDTH), row(D_MODEL)],
        out_shape=[jax.ShapeDtypeStruct((n, Q_WIDTH), BF16),
                   jax.ShapeDtypeStruct((n, KV_WIDTH), F32),
                   jax.ShapeDtypeStruct((n, KV_WIDTH), F32),
                   jax.ShapeDtypeStruct((n, GMLP_WIDTH), F32),
                   jax.ShapeDtypeStruct((n, GMLP_WIDTH), F32),
                   jax.ShapeDtypeStruct((n, D_MODEL), F32)],
        compiler_params=_params(("arbitrary",)),
        name="proj",
    )(x, norm_g, w_in_bf, cs, rot, ln_g, ln_b)


_ROT_COLS = 32


def _rot_expand():
    half = ROT_DIM // 2
    m = np.zeros((_ROT_COLS, 3 * LANES), np.float32)
    for lane in range(LANES):
        d = lane % HEAD_DIM
        if d < ROT_DIM:
            m[d % half, lane] = 1.0
        else:
            m[2 * half, lane] = 1.0
        if d < half:
            m[half + d, LANES + lane] = -1.0
        elif d < ROT_DIM:
            m[half + d - half, 2 * LANES + lane] = 1.0
    return m


_ROT_EXPAND = _rot_expand()


def _rotary_inputs(pos):
    half = ROT_DIM // 2
    inv_freq = ROPE_THETA ** (-jnp.arange(half, dtype=F32) / half)
    ang = inv_freq[:, None] * pos.astype(F32)[None, :]
    n = pos.shape[0]
    cs = jnp.concatenate([jnp.cos(ang), jnp.sin(ang), jnp.ones((1, n), F32),
                          jnp.zeros((_ROT_COLS - 2 * half - 1, n), F32)], axis=0)
    hi = cs.astype(BF16)
    rest = cs - hi.astype(F32)
    mid = rest.astype(BF16)
    lo = (rest - mid.astype(F32)).astype(BF16)
    return jnp.concatenate([hi, mid, lo], axis=0)


def _finish_rows(merged_bf, x, wout_ref, g2_ref, wr_ref, br_ref, xmid_ref, xn2_ref, lg_ref):
    xm = x + jnp.dot(merged_bf, wout_ref[...], preferred_element_type=F32)
    xmid_ref[...] = xm
    xn = _rms(xm, g2_ref[...])
    x_hi = xn.astype(BF16)
    x_lo = (xn - x_hi.astype(F32)).astype(BF16)
    w_hl = wr_ref[...]
    p_hi = jnp.dot(x_hi, w_hl, preferred_element_type=F32)
    p_lo = jnp.dot(x_lo, w_hl[:, :N_EXPERTS], preferred_element_type=F32)
    lg = p_hi[:, :N_EXPERTS] + (p_hi[:, N_EXPERTS:] + p_lo) + br_ref[...]
    wide = jnp.concatenate([lg, jnp.zeros((lg.shape[0], LANES - N_EXPERTS), F32)], axis=1)
    lg_ref[...] = wide.T[:N_EXPERTS]
    xn2_ref[...] = _pack_bf16_pair(xn[:, :D_MODEL // 2], xn[:, D_MODEL // 2:])


def _mix_body(sinks_ref, q_ref, k_ref, kp_ref, v_ref, vp_ref, a_ref, vn_ref, sgb_ref, x_ref,
              wsp_ref, bsp_ref, wout_ref, g2_ref, wr_ref, br_ref,
              xmid_ref, xn2_ref, lg_ref, kcat, vcat, mrg, key_s):
    i = pl.program_id(0)
    nsub = MIX_ROWS // WINDOW
    kcat[0:WINDOW] = kp_ref[...]
    kcat[WINDOW:] = k_ref[...]
    vcat[0:WINDOW] = vp_ref[...]
    vcat[WINDOW:] = v_ref[...]

    pair_rows = (GQA_GROUP // 2) * WINDOW
    lane_kv = lax.broadcasted_iota(jnp.int32, (2 * WINDOW, LANES), 1)
    lane_o = lax.broadcasted_iota(jnp.int32, (pair_rows, LANES), 1)

    @pl.when(i == 0)
    def _():
        rq = lax.broadcasted_iota(jnp.int32, (pair_rows, 4 * WINDOW), 0) & (WINDOW - 1)
        ck = lax.broadcasted_iota(jnp.int32, (pair_rows, 4 * WINDOW), 1) & (2 * WINDOW - 1)
        key_s[...] = jnp.where((ck > rq) & (ck <= rq + WINDOW), ck, -1)

    row_p = lax.broadcasted_iota(jnp.int32, (pair_rows, 1), 0) >> 7
    sink_cols = []
    for kk in range(N_KV_HEADS):
        h0 = kk * GQA_GROUP
        se = jnp.full((pair_rows, 1), sinks_ref[h0], F32)
        so = jnp.full((pair_rows, 1), sinks_ref[h0 + 1], F32)
        for p in range(1, GQA_GROUP // 2):
            se = jnp.where(row_p == p, sinks_ref[h0 + 2 * p], se)
            so = jnp.where(row_p == p, sinks_ref[h0 + 2 * p + 1], so)
        sink_cols.append((se, so))

    def sub(j, carry):
        off = pl.multiple_of(j * WINDOW, WINDOW)
        rows = pl.ds(off, WINDOW)
        for g in range(GMLP_GROUPS):
            cols = slice(g * LANES, (g + 1) * LANES)
            s = jnp.dot(wsp_ref[g], vn_ref[rows, cols].astype(BF16), preferred_element_type=F32) + bsp_ref[g]
            mrg[rows, cols] = a_ref[rows, cols] * s
        kblk = kcat[pl.ds(off, 2 * WINDOW), :]
        vblk = vcat[pl.ds(off, 2 * WINDOW), :]
        kswp = pltpu.roll(kblk, HEAD_DIM, 1)
        vswp = pltpu.roll(vblk, HEAD_DIM, 1)
        kmin = jnp.where(jnp.logical_and(i == 0, j == 0), WINDOW, 0)
        allowed = key_s[...] >= kmin
        for kk in range(N_KV_HEADS):
            lo_src, hi_src = (kblk, kswp) if kk == 0 else (kswp, kblk)
            kbd = jnp.concatenate([jnp.where(lane_kv < HEAD_DIM, lo_src, 0.0),
                                   jnp.where(lane_kv >= HEAD_DIM, hi_src, 0.0)], axis=0).astype(BF16)
            lo_src, hi_src = (vblk, vswp) if kk == 0 else (vswp, vblk)
            vbd = jnp.concatenate([jnp.where(lane_kv < HEAD_DIM, lo_src, 0.0),
                                   jnp.where(lane_kv >= HEAD_DIM, hi_src, 0.0)], axis=0).astype(BF16)
            pair0 = kk * (GQA_GROUP // 2)
            qs = jnp.concatenate([q_ref[rows, (pair0 + p) * LANES:(pair0 + p + 1) * LANES]
                                  for p in range(GQA_GROUP // 2)], axis=0)
            lg = lax.dot_general(qs, kbd, (((1,), (1,)), ((), ())), preferred_element_type=F32)
            lg = jnp.where(allowed, lg, NEG_INF)
            se, so = sink_cols[kk]
            le, lo = lg[:, :2 * WINDOW], lg[:, 2 * WINDOW:]
            me = jnp.maximum(jnp.max(le, axis=1, keepdims=True), se)
            mo = jnp.maximum(jnp.max(lo, axis=1, keepdims=True), so)
            pe = jnp.exp(le - me)
            po = jnp.exp(lo - mo)
            de = jnp.sum(pe, axis=1, keepdims=True) + jnp.exp(se - me)
            do = jnp.sum(po, axis=1, keepdims=True) + jnp.exp(so - mo)
            pr = jnp.concatenate([pe, po], axis=1).astype(BF16)
            o = jnp.dot(pr, vbd, preferred_element_type=F32)
            o = o / jnp.where(lane_o < HEAD_DIM, de, do)
            for p in range(GQA_GROUP // 2):
                cols = slice((pair0 + p) * LANES, (pair0 + p + 1) * LANES)
                mrg[rows, cols] += sgb_ref[rows, cols] * o[p * WINDOW:(p + 1) * WINDOW]
        return carry

    lax.fori_loop(0, nsub, sub, 0)
    _finish_rows(mrg[...].astype(BF16), x_ref[...], wout_ref, g2_ref, wr_ref, br_ref, xmid_ref, xn2_ref, lg_ref)


def _mix(sinks, q, k, v, a, vn, sgb, x, wsp, bsp, wout, g2, wr, br):
    n = x.shape[0]
    tm = MIX_ROWS
    nsub = tm // WINDOW
    row = lambda w: pl.BlockSpec((tm, w), lambda i: (i, 0))
    prev = pl.BlockSpec((WINDOW, KV_WIDTH), lambda i: (jnp.maximum(i * nsub - 1, 0), 0))
    full = lambda arr: pl.BlockSpec(arr.shape, lambda i: (0,) * arr.ndim)
    smem = pl.BlockSpec(memory_space=pltpu.SMEM)
    return pl.pallas_call(
        _mix_body,
        grid=(n // tm,),
        in_specs=[smem, row(Q_WIDTH), row(KV_WIDTH), prev, row(KV_WIDTH), prev,
                  row(GMLP_WIDTH), row(GMLP_WIDTH), row(D_MODEL), row(D_MODEL),
                  full(wsp), full(bsp), full(wout), full(g2), full(wr), full(br)],
        out_specs=[row(D_MODEL), row(D_MODEL // 2), pl.BlockSpec((N_EXPERTS, tm), lambda i: (0, i))],
        out_shape=[jax.ShapeDtypeStruct((n, D_MODEL), F32),
                   jax.ShapeDtypeStruct((n, D_MODEL // 2), jnp.uint32),
                   jax.ShapeDtypeStruct((N_EXPERTS, n), F32)],
        scratch_shapes=[pltpu.VMEM((tm + WINDOW, KV_WIDTH), F32),
                        pltpu.VMEM((tm + WINDOW, KV_WIDTH), F32),
                        pltpu.VMEM((tm, D_MODEL), F32),
                        pltpu.VMEM((GQA_GROUP // 2 * WINDOW, 4 * WINDOW), jnp.int32)],
        compiler_params=_params(("arbitrary",)),
        name="mix_prompt",
    )(sinks, q, k, k, v, v, a, vn, sgb, x, wsp, bsp, wout, g2, wr, br)


def _sample_attn_body(sink_ref, q_ref, k_ref, v_ref, o_ref):
    q = q_ref[...]
    k = k_ref[...]
    v = v_ref[...]
    nq, nk = q.shape[1], k.shape[1]
    heads = lambda t, kk: t[:, :, kk * HEAD_DIM:(kk + 1) * HEAD_DIM].astype(BF16)
    row = lax.broadcasted_iota(jnp.int32, (1, nq, nk), 1)
    first_kv = (row % N_HEADS) < GQA_GROUP
    lg = jnp.where(first_kv,
                   jnp.einsum("bqd,bkd->bqk", q, heads(k, 0), preferred_element_type=F32),
                   jnp.einsum("bqd,bkd->bqk", q, heads(k, 1), preferred_element_type=F32))
    t = row // N_HEADS
    j = lax.broadcasted_iota(jnp.int32, (1, nq, nk), 2)
    lg = jnp.where((j > t) & (j <= t + WINDOW), lg, NEG_INF)
    sink = sink_ref[...][None]
    m = jnp.maximum(jnp.max(lg, axis=2, keepdims=True), sink)
    p = jnp.exp(lg - m)
    den = jnp.sum(p, axis=2, keepdims=True) + jnp.exp(sink - m)
    pb = p.astype(BF16)
    row_o = lax.broadcasted_iota(jnp.int32, (1, nq, HEAD_DIM), 1)
    o = jnp.where((row_o % N_HEADS) < GQA_GROUP,
                  jnp.einsum("bqk,bkd->bqd", pb, heads(v, 0), preferred_element_type=F32),
                  jnp.einsum("bqk,bkd->bqd", pb, heads(v, 1), preferred_element_type=F32))
    o_ref[...] = o / den


def _sample_attn(sink_col, q3, k_all, v_all):
    nb = q3.shape[0]
    bb = 64
    blk = lambda a: pl.BlockSpec((bb,) + a.shape[1:], lambda b: (b, 0, 0))
    return pl.pallas_call(
        _sample_attn_body,
        grid=(nb // bb,),
        in_specs=[pl.BlockSpec(sink_col.shape, lambda b: (0, 0)), blk(q3), blk(k_all), blk(v_all)],
        out_specs=blk(q3),
        out_shape=jax.ShapeDtypeStruct(q3.shape, F32),
        compiler_params=_params(("arbitrary",)),
        name="attn_sample",
    )(sink_col, q3, k_all, v_all)


def _mix_sample_body(a_ref, vn_ref, sgb_ref, o_ref, x_ref, coef_ref, bias_ref,
                     wout_ref, g2_ref, wr_ref, br_ref, xmid_ref, xn2_ref, lg_ref):
    vn = vn_ref[...]
    n, width = vn.shape
    rows8 = lambda t: t.reshape(n // 8, 8, width)
    s = bias_ref[...][None] + coef_ref[0][None] * rows8(vn)
    for d in range(1, coef_ref.shape[0]):
        s = s + coef_ref[d][None] * rows8(pltpu.roll(vn, d, 0))
    merged = a_ref[...] * s.reshape(n, width) + sgb_ref[...] * o_ref[...]
    _finish_rows(merged.astype(BF16), x_ref[...], wout_ref, g2_ref, wr_ref, br_ref, xmid_ref, xn2_ref, lg_ref)


def _mix_sample(a, vn, sgb, o, x, coef, bias, wout, g2, wr, br):
    n = x.shape[0]
    args = (a, vn, sgb, o, x, coef, bias, wout, g2, wr, br)
    full = lambda arr: pl.BlockSpec(arr.shape, lambda i: (0,) * arr.ndim)
    return pl.pallas_call(
        _mix_sample_body,
        grid=(1,),
        in_specs=[full(arr) for arr in args],
        out_specs=[pl.BlockSpec((n, D_MODEL), lambda i: (0, 0)), pl.BlockSpec((n, D_MODEL // 2), lambda i: (0, 0)),
                   pl.BlockSpec((N_EXPERTS, n), lambda i: (0, 0))],
        out_shape=[jax.ShapeDtypeStruct((n, D_MODEL), F32),
                   jax.ShapeDtypeStruct((n, D_MODEL // 2), jnp.uint32),
                   jax.ShapeDtypeStruct((N_EXPERTS, n), F32)],
        compiler_params=_params(("arbitrary",)),
        name="mix_sample",
    )(*args)


def _rows8(rows, dtype):
    n = rows[0].shape[1]
    sub = lax.broadcasted_iota(jnp.int32, (8, n), 0)
    out = jnp.zeros((8, n), dtype)
    for kx, r in enumerate(rows):
        out = jnp.where(sub == kx, r.astype(dtype), out)
    return out


def _route_body(nblk, reserve, lg_ref, prior_ref, gate_ref, dest_ref, meta_ref, idx_s, rank_s, base):
    i = pl.program_id(0)

    @pl.when(i == 0)
    def _():
        base[...] = jnp.zeros_like(base)

    l = lg_ref[...]
    tb = l.shape[1]
    sub = lax.broadcasted_iota(jnp.int32, l.shape, 0).astype(F32)
    vals, idxs, sels = [], [], []
    for _ in range(TOP_K):
        m = jnp.max(l, axis=0, keepdims=True)
        ik = jnp.min(jnp.where(l == m, sub, float(N_EXPERTS)), axis=0, keepdims=True)
        sel = sub == ik
        l = jnp.where(sel, -jnp.inf, l)
        vals.append(m)
        idxs.append(ik)
        sels.append(sel)
    es = [jnp.exp(vk - vals[0]) for vk in vals]
    den = es[0] + es[1] + es[2] + es[3]
    onehot = jnp.zeros(l.shape, F32)
    for sel in sels:
        onehot = onehot + sel.astype(F32)
    earlier = (lax.broadcasted_iota(jnp.int32, (tb, tb), 0) < lax.broadcasted_iota(jnp.int32, (tb, tb), 1))
    before = jnp.dot(onehot.astype(BF16), earlier.astype(BF16), preferred_element_type=F32) + base[...]
    ranks = [jnp.sum(jnp.where(sel, before, 0.0), axis=0, keepdims=True) for sel in sels]
    base[...] += jnp.sum(onehot, axis=1, keepdims=True)
    idx_s[i] = _rows8(idxs, F32)
    rank_s[i] = _rows8(ranks, F32)
    gates = jnp.concatenate([_rows8([e / den for e in es], F32), jnp.zeros((LANES - 8, tb), F32)], axis=0)
    gate_ref[...] = gates.T

    @pl.when(i == nblk - 1)
    def _():
        cnt = base[...]
        if reserve is None:
            seg0 = prior_ref[:, 0:1]
            placed = prior_ref[:, 1:2]
        else:
            cap = jnp.ceil((cnt + reserve) / EXPERT_ROWS) * EXPERT_ROWS
            lower = (lax.broadcasted_iota(jnp.int32, (N_EXPERTS, N_EXPERTS), 1) <
                     lax.broadcasted_iota(jnp.int32, (N_EXPERTS, N_EXPERTS), 0)).astype(F32)
            seg0 = jnp.dot(lower, jnp.broadcast_to(cap, (N_EXPERTS, LANES)), preferred_element_type=F32,
                           precision=lax.Precision.HIGHEST)[:, :1]
            placed = jnp.zeros_like(cnt)
        total = placed + cnt
        lane = lax.broadcasted_iota(jnp.int32, (N_EXPERTS, LANES), 1)
        meta_ref[...] = jnp.where(lane == 0, seg0, jnp.where(lane == 1, total, jnp.where(
            lane == 2, seg0 / EXPERT_ROWS, jnp.where(lane == 3, jnp.ceil(total / EXPERT_ROWS), 0.0))))
        first = seg0 + placed
        sub_e = lax.broadcasted_iota(jnp.int32, (N_EXPERTS, tb), 0).astype(F32)
        for b in range(nblk):
            idx, rank = idx_s[b], rank_s[b]
            rows = [jnp.sum(jnp.where(sub_e == idx[kx:kx + 1], first, 0.0), axis=0, keepdims=True)
                    + rank[kx:kx + 1] for kx in range(TOP_K)]
            dest_ref[:, b * tb:(b + 1) * tb] = _rows8(rows, jnp.int32)


def _route(logits, prior, reserve):
    n = logits.shape[1]
    tb = min(n, ROUTE_ROWS)
    nblk = n // tb
    assert nblk * tb == n
    table = pl.BlockSpec((N_EXPERTS, LANES), lambda i: (0, 0))
    return pl.pallas_call(
        functools.partial(_route_body, nblk, reserve),
        grid=(nblk,),
        in_specs=[pl.BlockSpec((N_EXPERTS, tb), lambda i: (0, i)), table],
        out_specs=[pl.BlockSpec((tb, LANES), lambda i: (i, 0)), pl.BlockSpec((8, n), lambda i: (0, 0)), table],
        out_shape=[jax.ShapeDtypeStruct((n, LANES), F32),
                   jax.ShapeDtypeStruct((8, n), jnp.int32),
                   jax.ShapeDtypeStruct((N_EXPERTS, LANES), F32)],
        scratch_shapes=[pltpu.VMEM((nblk, 8, tb), F32), pltpu.VMEM((nblk, 8, tb), F32),
                        pltpu.VMEM((N_EXPERTS, 1), F32)],
        compiler_params=_params(("arbitrary",)),
        name="route",
    )(logits, prior)


def _sc_mesh():
    return plsc.VectorSubcoreMesh(core_axis_name="c", subcore_axis_name="s")


def _sc_worker():
    return lax.axis_index("s") * SC_CORES + lax.axis_index("c")


def _sc_dispatch(x, dest_t, n_slots):
    chunk = SC_ROWS
    n = x.shape[0]
    per_w = n // (SC_WORKERS * chunk)
    assert per_w * SC_WORKERS * chunk == n and per_w % 2 == 0
    d3 = dest_t.reshape(dest_t.shape[0], n // chunk, chunk)
    width, dtype = x.shape[1], x.dtype

    @functools.partial(
        pl.kernel, mesh=_sc_mesh(),
        out_type=jax.ShapeDtypeStruct((n_slots, width), dtype),
        scratch_types=[pltpu.VMEM((TOP_K, per_w, chunk), jnp.int32),
                       pltpu.VMEM((2, chunk, width), dtype),
                       pltpu.SemaphoreType.DMA, pltpu.SemaphoreType.DMA],
        compiler_params=pltpu.CompilerParams(use_tc_tiling_on_sc=True),
        name="dispatch")
    def run(x_hbm, d_hbm, out_hbm, idx_v, rows_v, rsem, wsem):
        wid = _sc_worker()
        pltpu.sync_copy(d_hbm.at[pl.ds(0, TOP_K), pl.ds(wid * per_w, per_w)], idx_v)

        def read(j, slot):
            return pltpu.make_async_copy(x_hbm.at[pl.ds((wid * per_w + j) * chunk, chunk)], rows_v.at[slot], rsem)

        def scatter(j, slot):
            copies = [pltpu.async_copy(rows_v.at[slot], out_hbm.at[idx_v.at[kx, j]], wsem) for kx in range(TOP_K)]
            for cp in copies:
                cp.wait()

        read(0, 0).start()

        def body(h, carry):
            j = 2 * h
            read(j, 0).wait()
            read(j + 1, 1).start()
            scatter(j, 0)
            read(j + 1, 1).wait()

            @pl.when(j + 2 < per_w)
            def _():
                read(j + 2, 0).start()

            scatter(j + 1, 1)
            return carry

        lax.fori_loop(0, per_w // 2, body, 0)

    return run(x, d3)


def _dispatch_small_body(dest_ref, x_ref, _, out_hbm, sem):
    n = x_ref.shape[0]

    def body(t, carry):
        for kx in range(TOP_K):
            pltpu.make_async_copy(x_ref.at[pl.ds(t, 1)], out_hbm.at[pl.ds(dest_ref[kx, t], 1)],
                                  sem).start(priority=kx % 2)
        return carry

    lax.fori_loop(0, n, body, 0, unroll=4)
    for _ in range(TOP_K):
        pltpu.make_async_copy(x_ref, out_hbm.at[pl.ds(0, n)], sem).wait()


def _dispatch_small(x, dest_t, x_sorted):
    return pl.pallas_call(
        _dispatch_small_body,
        grid=(1,),
        in_specs=[pl.BlockSpec(memory_space=pltpu.SMEM), pl.BlockSpec(x.shape, lambda i: (0, 0)),
                  pl.BlockSpec(memory_space=pl.ANY)],
        out_specs=pl.BlockSpec(memory_space=pl.ANY),
        out_shape=jax.ShapeDtypeStruct(x_sorted.shape, x_sorted.dtype),
        scratch_shapes=[pltpu.SemaphoreType.DMA],
        input_output_aliases={2: 0},
        compiler_params=_params(("arbitrary",)),
        name="dispatch_small",
    )(dest_t, x, x_sorted)


def _sc_collect(y_sorted, dest_t, n_p, n_s):
    chunk = SC_ROWS
    per_choice = SC_WORKERS // TOP_K
    per_w = n_p // (per_choice * chunk)
    assert per_w * per_choice * chunk == n_p and per_w % 2 == 0
    assert n_s == per_choice * chunk
    d3 = dest_t.reshape(dest_t.shape[0], (n_p + n_s) // chunk, chunk)
    width, dtype = y_sorted.shape[1], y_sorted.dtype

    @functools.partial(
        pl.kernel, mesh=_sc_mesh(),
        out_type=[jax.ShapeDtypeStruct((TOP_K * n_p, width), dtype), jax.ShapeDtypeStruct((TOP_K * n_s, width), dtype)],
        scratch_types=[pltpu.VMEM((per_w, chunk), jnp.int32),
                       pltpu.VMEM((1, chunk), jnp.int32),
                       pltpu.VMEM((2, chunk, width), dtype),
                       pltpu.SemaphoreType.DMA, pltpu.SemaphoreType.DMA],
        compiler_params=pltpu.CompilerParams(use_tc_tiling_on_sc=True),
        name="collect")
    def run(y_hbm, d_hbm, op_hbm, os_hbm, ip_v, is_v, rows_v, gsem, wsem):
        wid = _sc_worker()
        choice = wid // per_choice
        part = wid % per_choice
        pltpu.sync_copy(d_hbm.at[choice, pl.ds(part * per_w, per_w)], ip_v)
        pltpu.sync_copy(d_hbm.at[choice, pl.ds(n_p // chunk + part, 1)], is_v)

        def gather(idx_v, j, slot):
            return pltpu.make_async_copy(y_hbm.at[idx_v.at[j]], rows_v.at[slot], gsem)

        def write(j, slot):
            return pltpu.make_async_copy(rows_v.at[slot], op_hbm.at[pl.ds((wid * per_w + j) * chunk, chunk)], wsem)

        gather(ip_v, 0, 0).start()

        def body(h, carry):
            j = 2 * h
            gather(ip_v, j, 0).wait()

            @pl.when(h > 0)
            def _():
                write(j - 1, 1).wait()

            gather(ip_v, j + 1, 1).start()
            write(j, 0).start()
            gather(ip_v, j + 1, 1).wait()
            write(j, 0).wait()

            @pl.when(j + 2 < per_w)
            def _():
                gather(ip_v, j + 2, 0).start()

            write(j + 1, 1).start()
            return carry

        lax.fori_loop(0, per_w // 2, body, 0)
        write(per_w - 1, 1).wait()

        gather(is_v, 0, 0).start()
        gather(is_v, 0, 0).wait()
        pltpu.sync_copy(rows_v.at[0], os_hbm.at[pl.ds(wid * chunk, chunk)])

    return run(y_sorted, d3)


def _expert_body(blk0_ref, nblk_ref, cnt_ref, wup_hbm, wdn_hbm, bup_ref, bdn_ref, x_hbm, y_hbm,
                 wup_f, wdn_f, wup_s, wdn_s, xbuf, obuf, w_sem, in_sem, out_sem):
    e = pl.program_id(0)
    nb = nblk_ref[e]
    blk0 = blk0_ref[e]
    cnt = cnt_ref[e]
    tm = EXPERT_ROWS
    pair = 2 * LANES
    wslot = e % 2
    up_rows = D_MODEL // W_PIECES
    dn_rows = D_MODEL // (W_PIECES // 2)

    def w_piece(hbm, buf, ex, slot, p, rows):
        start = p * rows if isinstance(p, int) else pl.multiple_of(p * rows, rows)
        r = pl.ds(start, rows)
        return pltpu.make_async_copy(hbm.at[ex, r], buf.at[slot, r], w_sem.at[slot])

    def w_start(ex, slot, p):
        w_piece(wup_hbm, wup_f, ex, slot, p, up_rows).start()
        if isinstance(p, int):
            if p < W_PIECES // 2:
                w_piece(wdn_hbm, wdn_f, ex, slot, p, dn_rows).start()
        else:
            @pl.when(p < W_PIECES // 2)
            def _():
                w_piece(wdn_hbm, wdn_f, ex, slot, p, dn_rows).start()

    def w_wait(ex, slot):
        for p in range(W_PIECES):
            w_piece(wup_hbm, wup_f, ex, slot, p, up_rows).wait()
        for p in range(W_PIECES // 2):
            w_piece(wdn_hbm, wdn_f, ex, slot, p, dn_rows).wait()

    @pl.when(e == 0)
    def _():
        for p in range(W_PIECES):
            w_start(0, 0, p)

    w_wait(e, wslot)
    wup_ref = wup_f.at[wslot]
    wdn_ref = wdn_f.at[wslot]
    more = e + 1 < N_EXPERTS

    def x_copy(i, slot):
        rows = pl.ds(pl.multiple_of((blk0 + i) * tm, tm), tm)
        return pltpu.make_async_copy(x_hbm.at[rows], xbuf.at[slot], in_sem.at[slot])

    def y_copy(i, slot):
        rows = pl.ds(pl.multiple_of((blk0 + i) * tm, tm), tm)
        return pltpu.make_async_copy(obuf.at[slot], y_hbm.at[rows], out_sem.at[slot])

    @pl.when(nb > 0)
    def _():
        x_copy(0, 0).start(priority=1)
        r = lax.broadcasted_iota(jnp.int32, (pair, pair), 0)
        c = lax.broadcasted_iota(jnp.int32, (pair, pair), 1)
        perm = (r == jnp.where(c < LANES, 2 * c, 2 * (c - LANES) + 1)).astype(BF16)
        for g in range(2 * D_MODEL // pair):
            cols = slice(g * pair, (g + 1) * pair)
            wup_s[g] = jnp.dot(wup_ref[:, cols].astype(BF16), perm, preferred_element_type=F32).astype(BF16)
        for g in range(D_MODEL // pair):
            wdn_s[g] = wdn_ref[:, g * pair:(g + 1) * pair].astype(BF16)

        def block(i, carry):
            slot = i % 2
            x_copy(i, slot).wait()

            @pl.when(i + 1 < nb)
            def _():
                x_copy(i + 1, 1 - slot).start(priority=1)

            @pl.when(i >= 2)
            def _():
                y_copy(i - 2, slot).wait()

            @pl.when(jnp.logical_and(more, i < W_PIECES))
            def _():
                w_start(e + 1, 1 - wslot, i)

            row = lax.broadcasted_iota(jnp.int32, (tm, 1), 0)
            x = jnp.where(row < cnt - i * tm, _unpack_bf16_pair(xbuf[slot]), 0.0).astype(BF16)
            acts = []
            for g in range(2 * D_MODEL // pair):
                cols = slice(g * pair, (g + 1) * pair)
                h = jnp.dot(x, wup_s[g], preferred_element_type=F32) + bup_ref[0, :, cols]
                glu = jnp.minimum(h[:, :LANES], SWIGLU_LIMIT)
                lin = jnp.clip(h[:, LANES:], -SWIGLU_LIMIT, SWIGLU_LIMIT)
                acts.append((glu * jax.nn.sigmoid(SWIGLU_ALPHA * glu) * (lin + 1.0)).astype(BF16))
            act = jnp.concatenate(acts, axis=1)
            half_groups = D_MODEL // pair // 2
            for g in range(half_groups):
                ys = []
                for gg in (g, g + half_groups):
                    cols = slice(gg * pair, (gg + 1) * pair)
                    ys.append(jnp.dot(act, wdn_s[gg], preferred_element_type=F32) + bdn_ref[0, :, cols])
                obuf[slot, :, g * pair:(g + 1) * pair] = _pack_bf16_pair(ys[0], ys[1])
            y_copy(i, slot).start(priority=1)
            return carry

        lax.fori_loop(0, nb, block, 0)

        @pl.when(nb >= 2)
        def _():
            y_copy(nb - 2, nb % 2).wait()

        y_copy(nb - 1, (nb - 1) % 2).wait()

    for p in range(W_PIECES):
        @pl.when(jnp.logical_and(more, p >= nb))
        def _():
            w_start(e + 1, 1 - wslot, p)


def _experts(blk0, nblk, cnt, x_sorted, w_up, w_down, b_up_grouped, b_down):
    tm = EXPERT_ROWS
    per_expert = lambda a: pl.BlockSpec((1,) + a.shape[1:], lambda e, b0, nb, ct: (e, 0, 0))
    hbm = pl.BlockSpec(memory_space=pl.ANY)
    grid_spec = pltpu.PrefetchScalarGridSpec(
        num_scalar_prefetch=3,
        grid=(N_EXPERTS,),
        in_specs=[hbm, hbm, per_expert(b_up_grouped), per_expert(b_down), hbm],
        out_specs=hbm,
        scratch_shapes=[pltpu.VMEM((2,) + w_up.shape[1:], F32), pltpu.VMEM((2,) + w_down.shape[1:], F32),
                        pltpu.VMEM((2 * D_MODEL // (2 * LANES), D_MODEL, 2 * LANES), BF16),
                        pltpu.VMEM((D_MODEL // (2 * LANES), D_MODEL, 2 * LANES), BF16),
                        pltpu.VMEM((2, tm, x_sorted.shape[1]), x_sorted.dtype),
                        pltpu.VMEM((2, tm, D_MODEL // 2), jnp.uint32),
                        pltpu.SemaphoreType.DMA((2,)), pltpu.SemaphoreType.DMA((2,)), pltpu.SemaphoreType.DMA((2,))],
    )
    return pl.pallas_call(
        _expert_body,
        grid_spec=grid_spec,
        out_shape=jax.ShapeDtypeStruct((x_sorted.shape[0], D_MODEL // 2), jnp.uint32),
        compiler_params=_params(("arbitrary",)),
        name="experts",
    )(blk0, nblk, cnt, w_up, w_down, b_up_grouped, b_down, x_sorted)


def _combine_body(gate_ref, xmid_ref, gfin_ref, y0_ref, y1_ref, y2_ref, y3_ref, out_ref):
    gate = gate_ref[...]
    moe = _unpack_bf16_pair(y0_ref[...]) * gate[:, 0:1]
    for kx, y_ref in enumerate((y1_ref, y2_ref, y3_ref), start=1):
        moe = moe + _unpack_bf16_pair(y_ref[...]) * gate[:, kx:kx + 1]
    out_ref[...] = _rms(xmid_ref[...] + moe, gfin_ref[...])


def _combine(gates, first_token, xmid, gfin, y_rows):
    n = xmid.shape[0]
    tt = min(n, COMBINE_ROWS)
    nblk = n // tt
    blk0 = first_token // tt
    assert blk0 * tt == first_token
    choice = lambda kx: pl.BlockSpec((tt, y_rows.shape[1]), lambda i: (i + kx * nblk, 0))
    return pl.pallas_call(
        _combine_body,
        grid=(nblk,),
        in_specs=[pl.BlockSpec((tt, LANES), lambda i: (i + blk0, 0)),
                  pl.BlockSpec((tt, D_MODEL), lambda i: (i, 0)),
                  pl.BlockSpec((1, D_MODEL), lambda i: (0, 0))] + [choice(kx) for kx in range(TOP_K)],
        out_specs=pl.BlockSpec((tt, D_MODEL), lambda i: (i, 0)),
        out_shape=jax.ShapeDtypeStruct((n, D_MODEL), F32),
        compiler_params=_params(("arbitrary",)),
        name="combine",
    )(gates, xmid, gfin, y_rows, y_rows, y_rows, y_rows)


def kernel(x_prompt, x_sample, cache_k_win, cache_v_win, norm_attn_g, w_in, ln_v_g, ln_v_b, w_spatial, b_spatial,
           attn_sinks, w_out, norm_ffn_g, w_router, b_router, w_up, b_up, w_down, b_down, norm_final_g):
    bp, tp, _ = x_prompt.shape
    bs, ts, _ = x_sample.shape
    w_buf = cache_k_win.shape[2]
    assert bp == 1 and tp % MIX_ROWS == 0 and w_buf == WINDOW and (bs * ts) % PROJ_ROWS == 0 and 8 % ts == 0
    n_p, n_s = bp * tp, bs * ts
    row2 = lambda a: a.reshape(1, -1)

    w_in_bf = w_in[0].astype(BF16)
    w_out_bf = w_out[0].astype(BF16)
    tril = jnp.tril(jnp.ones((CHUNK, CHUNK), dtype=bool))
    wsp = jnp.where(tril[None], w_spatial[0], 0.0)
    wsp_bf = wsp.astype(BF16)
    bsp = jnp.broadcast_to(b_spatial[0][:, :, None], (GMLP_GROUPS, CHUNK, LANES))
    b_up_grouped = b_up[0].reshape(N_EXPERTS, -1, LANES, 2).transpose(0, 1, 3, 2).reshape(N_EXPERTS, 1, -1)
    bd = b_down[0][:, None, :]
    g1, g2, gfin = row2(norm_attn_g[0]), row2(norm_ffn_g[0]), row2(norm_final_g)
    lng, lnb = row2(ln_v_g[0]), row2(ln_v_b[0])
    wr_hi = w_router[0].astype(BF16)
    wr = jnp.concatenate([wr_hi, (w_router[0] - wr_hi.astype(F32)).astype(BF16)], axis=1)
    br = row2(b_router[0])
    sinks = attn_sinks[0]

    xp = x_prompt.reshape(n_p, D_MODEL)
    cs_p = _rotary_inputs(jnp.arange(tp, dtype=jnp.int32))
    q_p, k_p, v_p, a_p, vn_p, sgb_p = _proj(xp, g1, w_in_bf, cs_p, lng, lnb)
    xmid_p, xn2_p, lg_p = _mix(sinks, q_p, k_p, v_p, a_p, vn_p, sgb_p, xp, wsp_bf, bsp, w_out_bf, g2, wr, br)

    xs = x_sample.reshape(n_s, D_MODEL)
    pos_s = PAST_LEN + jnp.arange(ts, dtype=jnp.int32)
    cs_s = _rotary_inputs(jnp.tile(pos_s, bs))
    q_s, k_s, v_s, a_s, vn_s, sgb_s = _proj(xs, g1, w_in_bf, cs_s, lng, lnb)
    n_keys = w_buf + ts
    key_pad = jnp.zeros((bs, (-n_keys) % 8, KV_WIDTH), F32)
    with_new = lambda cache, new: jnp.concatenate(
        [cache[0].reshape(bs, w_buf, KV_WIDTH), new.reshape(bs, ts, KV_WIDTH), key_pad], axis=1)
    k_all = with_new(cache_k_win, k_s)
    v_all = with_new(cache_v_win, v_s)
    sink_col = jnp.tile(sinks, ts).reshape(ts * N_HEADS, 1)
    o_s = _sample_attn(sink_col, q_s.reshape(bs, ts * N_HEADS, HEAD_DIM), k_all, v_all).reshape(n_s, Q_WIDTH)
    lag = np.arange(ts)[:, None] - np.arange(ts)[None, :]
    coef = jnp.stack([jnp.sum(jnp.where(lag == d, wsp[:, :ts, :ts], 0.0), axis=2)
                      for d in range(ts)])
    coef = jnp.repeat(coef.transpose(0, 2, 1), GMLP_WIDTH // GMLP_GROUPS, axis=2)
    coef = jnp.tile(coef, (1, 8 // ts, 1))
    bias = jnp.tile(jnp.repeat(b_spatial[0][:, :ts].T, GMLP_WIDTH // GMLP_GROUPS, axis=1), (8 // ts, 1))
    xmid_s, xn2_s, lg_s = _mix_sample(a_s, vn_s, sgb_s, o_s, xs, coef, bias, w_out_bf, g2, wr, br)

    tm = EXPERT_ROWS
    n_blocks = (n_p * TOP_K + N_EXPERTS * n_s) // tm + N_EXPERTS
    gate_p, dest_p, table_p = _route(lg_p, jnp.zeros((N_EXPERTS, LANES), F32), n_s)
    x_sorted = _sc_dispatch(xn2_p, dest_p, n_blocks * tm)
    gate_s, dest_s, table = _route(lg_s, table_p, None)
    x_sorted = _dispatch_small(xn2_s, dest_s, x_sorted)
    dest_t = jnp.concatenate([dest_p, dest_s], axis=1)
    meta = table.astype(jnp.int32)
    y_sorted = _experts(meta[:, 2], meta[:, 3], meta[:, 1], x_sorted, w_up[0], w_down[0], b_up_grouped, bd)
    yrows_p, yrows_s = _sc_collect(y_sorted, dest_t, n_p, n_s)
    y_p = _combine(gate_p, 0, xmid_p, gfin, yrows_p)
    y_s = _combine(gate_s, 0, xmid_s, gfin, yrows_s)

    k4 = lambda t: t.reshape(1, bp, -1, N_KV_HEADS, HEAD_DIM)
    return (y_p.reshape(bp, tp, D_MODEL),
            y_s.reshape(bs, ts, D_MODEL),
            k4(k_p[n_p - WINDOW:]),
            k4(v_p[n_p - WINDOW:]),
            vn_p[n_p - CHUNK:].reshape(1, bp, CHUNK, GMLP_WIDTH),
            k_all[:, ts:n_keys].reshape(1, bs, w_buf, N_KV_HEADS, HEAD_DIM),
            v_all[:, ts:n_keys].reshape(1, bs, w_buf, N_KV_HEADS, HEAD_DIM),
            vn_s.reshape(1, bs, ts, GMLP_WIDTH))
```

```python
import functools

import numpy as np
import jax
import jax.numpy as jnp
from jax import lax
from jax.experimental import pallas as pl
from jax.experimental.pallas import tpu as pltpu
from jax.experimental.pallas import tpu_sc as plsc

F32 = jnp.float32
BF16 = jnp.bfloat16

D_MODEL = 1024
HEAD_DIM = 64
N_HEADS = 16
GQA_GROUP = 8
N_KV_HEADS = 2
Q_WIDTH = 1024
KV_WIDTH = 128
WINDOW = 128
ROT_DIM = 16
ROPE_THETA = 500000.0
CHUNK = 128
GMLP_WIDTH = 1024
GMLP_GROUPS = 8
N_EXPERTS = 32
TOP_K = 4
SWIGLU_LIMIT = 7.0
SWIGLU_ALPHA = 1.702
RMS_EPS = 1e-5
LN_EPS = 1e-5
NEG_INF = -1e30
PAST_LEN = 16384

LANES = 128
VMEM_LIMIT = 56 * 1024 * 1024

PROJ_ROWS = 256
MIX_ROWS = 512
ROUTE_ROWS = 1024
EXPERT_ROWS = 256
W_PIECES = 8
COMBINE_ROWS = 1024

SC_CORES = 2
SC_WORKERS = 32
SC_ROWS = 64

_C_Q, _C_KV, _C_U, _C_VG, _C_GA, _C_GB, _C_END = 0, 1024, 1280, 2304, 3328, 4352, 5376


def _params(sem):
    return pltpu.CompilerParams(dimension_semantics=sem, vmem_limit_bytes=VMEM_LIMIT)


def _rms(x, g):
    return x * lax.rsqrt(jnp.mean(x * x, axis=-1, keepdims=True) + RMS_EPS) * g


def _pack_bf16_pair(lo, hi):
    lo_bits = lax.bitcast_convert_type(lo.astype(BF16).astype(F32), jnp.uint32)
    hi_bits = lax.bitcast_convert_type(hi.astype(BF16).astype(F32), jnp.uint32)
    return (lo_bits >> 16) | hi_bits


def _unpack_bf16_pair(words):
    lo = lax.bitcast_convert_type(words << 16, F32)
    hi = lax.bitcast_convert_type(words & jnp.uint32(0xFFFF0000), F32)
    return jnp.concatenate([lo, hi], axis=1)


def _proj_body(x_ref, g_ref, w_ref, cs_ref, rot_ref, lng_ref, lnb_ref,
               q_ref, k_ref, v_ref, a_ref, vn_ref, sgb_ref):
    h = _rms(x_ref[...], g_ref[...]).astype(BF16)
    tabs = lax.dot_general(cs_ref[...], rot_ref[...], (((0,), (0,)), ((), ())), preferred_element_type=F32)
    rc, rs1, rs2 = tabs[:, :LANES], tabs[:, LANES:2 * LANES], tabs[:, 2 * LANES:]

    def rot(z):
        return z * rc + pltpu.roll(z, LANES - ROT_DIM // 2, 1) * rs1 + pltpu.roll(z, ROT_DIM // 2, 1) * rs2

    def mm(lo, hi):
        return jnp.dot(h, w_ref[:, lo:hi], preferred_element_type=F32)

    zq = mm(_C_Q, _C_KV)
    for c in range(Q_WIDTH // LANES):
        sl = slice(c * LANES, (c + 1) * LANES)
        q_ref[:, sl] = (rot(zq[:, sl]) * (HEAD_DIM ** -0.5)).astype(BF16)
    zkv = mm(_C_KV, _C_U)
    k_ref[...] = rot(zkv[:, :KV_WIDTH])
    v_ref[...] = zkv[:, KV_WIDTH:]
    a_ref[...] = jax.nn.sigmoid(mm(_C_GA, _C_GB)) * jax.nn.gelu(mm(_C_U, _C_VG))
    zv = jax.nn.gelu(mm(_C_VG, _C_GA))
    zc = zv - jnp.mean(zv, axis=-1, keepdims=True)
    var = jnp.mean(zc * zc, axis=-1, keepdims=True)
    vn_ref[...] = zc * lax.rsqrt(var + LN_EPS) * lng_ref[...] + lnb_ref[...]
    sgb_ref[...] = jax.nn.sigmoid(mm(_C_GB, _C_END))


def _proj(x, norm_g, w_in_bf, cs, ln_g, ln_b):
    n = x.shape[0]
    tm = PROJ_ROWS
    row = lambda w: pl.BlockSpec((tm, w), lambda i: (i, 0))
    full = lambda a: pl.BlockSpec(a.shape, lambda i: (0,) * a.ndim)
    rot = jnp.asarray(np.tile(_ROT_EXPAND, (3, 1)), dtype=BF16)
    return pl.pallas_call(
        _proj_body,
        grid=(n // tm,),
        in_specs=[row(D_MODEL), full(norm_g), full(w_in_bf), pl.BlockSpec((cs.shape[0], tm), lambda i: (0, i)),
                  full(rot),
                  full(ln_g), full(ln_b)],
        out_specs=[row(Q_WIDTH), row(KV_WIDTH), row(KV_WIDTH), row(GMLP_WIDTH), row(GMLP_WIDTH), row(D_MODEL)],
        out_shape=[jax.ShapeDtypeStruct((n, Q_WIDTH), BF16),
                   jax.ShapeDtypeStruct((n, KV_WIDTH), F32),
                   jax.ShapeDtypeStruct((n, KV_WIDTH), F32),
                   jax.ShapeDtypeStruct((n, GMLP_WIDTH), F32),
                   jax.ShapeDtypeStruct((n, GMLP_WIDTH), F32),
                   jax.ShapeDtypeStruct((n, D_MODEL), F32)],
        compiler_params=_params(("arbitrary",)),
        name="proj",
    )(x, norm_g, w_in_bf, cs, rot, ln_g, ln_b)


_ROT_COLS = 32


def _rot_expand():
    half = ROT_DIM // 2
    m = np.zeros((_ROT_COLS, 3 * LANES), np.float32)
    for lane in range(LANES):
        d = lane % HEAD_DIM
        if d < ROT_DIM:
            m[d % half, lane] = 1.0
        else:
            m[2 * half, lane] = 1.0
        if d < half:
            m[half + d, LANES + lane] = -1.0
        elif d < ROT_DIM:
            m[half + d - half, 2 * LANES + lane] = 1.0
    return m


_ROT_EXPAND = _rot_expand()


def _rotary_inputs(pos):
    half = ROT_DIM // 2
    inv_freq = ROPE_THETA ** (-jnp.arange(half, dtype=F32) / half)
    ang = inv_freq[:, None] * pos.astype(F32)[None, :]
    n = pos.shape[0]
    cs = jnp.concatenate([jnp.cos(ang), jnp.sin(ang), jnp.ones((1, n), F32),
                          jnp.zeros((_ROT_COLS - 2 * half - 1, n), F32)], axis=0)
    hi = cs.astype(BF16)
    rest = cs - hi.astype(F32)
    mid = rest.astype(BF16)
    lo = (rest - mid.astype(F32)).astype(BF16)
    return jnp.concatenate([hi, mid, lo], axis=0)


def _finish_rows(merged_bf, x, wout_ref, g2_ref, wr_ref, br_ref, xmid_ref, xn2_ref, lg_ref):
    xm = x + jnp.dot(merged_bf, wout_ref[...], preferred_element_type=F32)
    xmid_ref[...] = xm
    xn = _rms(xm, g2_ref[...])
    x_hi = xn.astype(BF16)
    x_lo = (xn - x_hi.astype(F32)).astype(BF16)
    w_hl = wr_ref[...]
    p_hi = jnp.dot(x_hi, w_hl, preferred_element_type=F32)
    p_lo = jnp.dot(x_lo, w_hl[:, :N_EXPERTS], preferred_element_type=F32)
    lg = p_hi[:, :N_EXPERTS] + (p_hi[:, N_EXPERTS:] + p_lo) + br_ref[...]
    wide = jnp.concatenate([lg, jnp.zeros((lg.shape[0], LANES - N_EXPERTS), F32)], axis=1)
    lg_ref[...] = wide.T[:N_EXPERTS]
    xn2_ref[...] = _pack_bf16_pair(xn[:, :D_MODEL // 2], xn[:, D_MODEL // 2:])


def _mix_body(sinks_ref, q_ref, k_ref, kp_ref, v_ref, vp_ref, a_ref, vn_ref, sgb_ref, x_ref,
              wsp_ref, bsp_ref, wout_ref, g2_ref, wr_ref, br_ref,
              xmid_ref, xn2_ref, lg_ref, kcat, vcat, mrg, key_s):
    i = pl.program_id(0)
    nsub = MIX_ROWS // WINDOW
    kcat[0:WINDOW] = kp_ref[...]
    kcat[WINDOW:] = k_ref[...]
    vcat[0:WINDOW] = vp_ref[...]
    vcat[WINDOW:] = v_ref[...]

    pair_rows = (GQA_GROUP // 2) * WINDOW
    lane_kv = lax.broadcasted_iota(jnp.int32, (2 * WINDOW, LANES), 1)
    lane_o = lax.broadcasted_iota(jnp.int32, (pair_rows, LANES), 1)

    @pl.when(i == 0)
    def _():
        rq = lax.broadcasted_iota(jnp.int32, (pair_rows, 4 * WINDOW), 0) & (WINDOW - 1)
        ck = lax.broadcasted_iota(jnp.int32, (pair_rows, 4 * WINDOW), 1) & (2 * WINDOW - 1)
        key_s[...] = jnp.where((ck > rq) & (ck <= rq + WINDOW), ck, -1)

    row_p = lax.broadcasted_iota(jnp.int32, (pair_rows, 1), 0) >> 7
    sink_cols = []
    for kk in range(N_KV_HEADS):
        h0 = kk * GQA_GROUP
        se = jnp.full((pair_rows, 1), sinks_ref[h0], F32)
        so = jnp.full((pair_rows, 1), sinks_ref[h0 + 1], F32)
        for p in range(1, GQA_GROUP // 2):
            se = jnp.where(row_p == p, sinks_ref[h0 + 2 * p], se)
            so = jnp.where(row_p == p, sinks_ref[h0 + 2 * p + 1], so)
        sink_cols.append((se, so))

    def sub(j, carry):
        off = pl.multiple_of(j * WINDOW, WINDOW)
        rows = pl.ds(off, WINDOW)
        for g in range(GMLP_GROUPS):
            cols = slice(g * LANES, (g + 1) * LANES)
            s = jnp.dot(wsp_ref[g], vn_ref[rows, cols].astype(BF16), preferred_element_type=F32) + bsp_ref[g]
            mrg[rows, cols] = a_ref[rows, cols] * s
        kblk = kcat[pl.ds(off, 2 * WINDOW), :]
        vblk = vcat[pl.ds(off, 2 * WINDOW), :]
        kswp = pltpu.roll(kblk, HEAD_DIM, 1)
        vswp = pltpu.roll(vblk, HEAD_DIM, 1)
        kmin = jnp.where(jnp.logical_and(i == 0, j == 0), WINDOW, 0)
        allowed = key_s[...] >= kmin
        for kk in range(N_KV_HEADS):
            lo_src, hi_src = (kblk, kswp) if kk == 0 else (kswp, kblk)
            kbd = jnp.concatenate([jnp.where(lane_kv < HEAD_DIM, lo_src, 0.0),
                                   jnp.where(lane_kv >= HEAD_DIM, hi_src, 0.0)], axis=0).astype(BF16)
            lo_src, hi_src = (vblk, vswp) if kk == 0 else (vswp, vblk)
            vbd = jnp.concatenate([jnp.where(lane_kv < HEAD_DIM, lo_src, 0.0),
                                   jnp.where(lane_kv >= HEAD_DIM, hi_src, 0.0)], axis=0).astype(BF16)
            pair0 = kk * (GQA_GROUP // 2)
            qs = jnp.concatenate([q_ref[rows, (pair0 + p) * LANES:(pair0 + p + 1) * LANES]
                                  for p in range(GQA_GROUP // 2)], axis=0)
            lg = lax.dot_general(qs, kbd, (((1,), (1,)), ((), ())), preferred_element_type=F32)
            lg = jnp.where(allowed, lg, NEG_INF)
            se, so = sink_cols[kk]
            le, lo = lg[:, :2 * WINDOW], lg[:, 2 * WINDOW:]
            me = jnp.maximum(jnp.max(le, axis=1, keepdims=True), se)
            mo = jnp.maximum(jnp.max(lo, axis=1, keepdims=True), so)
            pe = jnp.exp(le - me)
            po = jnp.exp(lo - mo)
            de = jnp.sum(pe, axis=1, keepdims=True) + jnp.exp(se - me)
            do = jnp.sum(po, axis=1, keepdims=True) + jnp.exp(so - mo)
            pr = jnp.concatenate([pe, po], axis=1).astype(BF16)
            o = jnp.dot(pr, vbd, preferred_element_type=F32)
            o = o / jnp.where(lane_o < HEAD_DIM, de, do)
            for p in range(GQA_GROUP // 2):
                cols = slice((pair0 + p) * LANES, (pair0 + p + 1) * LANES)
                mrg[rows, cols] += sgb_ref[rows, cols] * o[p * WINDOW:(p + 1) * WINDOW]
        return carry

    lax.fori_loop(0, nsub, sub, 0)
    _finish_rows(mrg[...].astype(BF16), x_ref[...], wout_ref, g2_ref, wr_ref, br_ref, xmid_ref, xn2_ref, lg_ref)


def _mix(sinks, q, k, v, a, vn, sgb, x, wsp, bsp, wout, g2, wr, br):
    n = x.shape[0]
    tm = MIX_ROWS
    nsub = tm // WINDOW
    row = lambda w: pl.BlockSpec((tm, w), lambda i: (i, 0))
    prev = pl.BlockSpec((WINDOW, KV_WIDTH), lambda i: (jnp.maximum(i * nsub - 1, 0), 0))
    full = lambda arr: pl.BlockSpec(arr.shape, lambda i: (0,) * arr.ndim)
    smem = pl.BlockSpec(memory_space=pltpu.SMEM)
    return pl.pallas_call(
        _mix_body,
        grid=(n // tm,),
        in_specs=[smem, row(Q_WIDTH), row(KV_WIDTH), prev, row(KV_WIDTH), prev,
                  row(GMLP_WIDTH), row(GMLP_WIDTH), row(D_MODEL), row(D_MODEL),
                  full(wsp), full(bsp), full(wout), full(g2), full(wr), full(br)],
        out_specs=[row(D_MODEL), row(D_MODEL // 2), pl.BlockSpec((N_EXPERTS, tm), lambda i: (0, i))],
        out_shape=[jax.ShapeDtypeStruct((n, D_MODEL), F32),
                   jax.ShapeDtypeStruct((n, D_MODEL // 2), jnp.uint32),
                   jax.ShapeDtypeStruct((N_EXPERTS, n), F32)],
        scratch_shapes=[pltpu.VMEM((tm + WINDOW, KV_WIDTH), F32),
                        pltpu.VMEM((tm + WINDOW, KV_WIDTH), F32),
                        pltpu.VMEM((tm, D_MODEL), F32),
                        pltpu.VMEM((GQA_GROUP // 2 * WINDOW, 4 * WINDOW), jnp.int32)],
        compiler_params=_params(("arbitrary",)),
        name="mix_prompt",
    )(sinks, q, k, k, v, v, a, vn, sgb, x, wsp, bsp, wout, g2, wr, br)


def _sample_attn_body(sink_ref, q_ref, k_ref, v_ref, o_ref):
    q = q_ref[...]
    k = k_ref[...]
    v = v_ref[...]
    nq, nk = q.shape[1], k.shape[1]
    heads = lambda t, kk: t[:, :, kk * HEAD_DIM:(kk + 1) * HEAD_DIM].astype(BF16)
    row = lax.broadcasted_iota(jnp.int32, (1, nq, nk), 1)
    first_kv = (row % N_HEADS) < GQA_GROUP
    lg = jnp.where(first_kv,
                   jnp.einsum("bqd,bkd->bqk", q, heads(k, 0), preferred_element_type=F32),
                   jnp.einsum("bqd,bkd->bqk", q, heads(k, 1), preferred_element_type=F32))
    t = row // N_HEADS
    j = lax.broadcasted_iota(jnp.int32, (1, nq, nk), 2)
    lg = jnp.where((j > t) & (j <= t + WINDOW), lg, NEG_INF)
    sink = sink_ref[...][None]
    m = jnp.maximum(jnp.max(lg, axis=2, keepdims=True), sink)
    p = jnp.exp(lg - m)
    den = jnp.sum(p, axis=2, keepdims=True) + jnp.exp(sink - m)
    pb = p.astype(BF16)
    row_o = lax.broadcasted_iota(jnp.int32, (1, nq, HEAD_DIM), 1)
    o = jnp.where((row_o % N_HEADS) < GQA_GROUP,
                  jnp.einsum("bqk,bkd->bqd", pb, heads(v, 0), preferred_element_type=F32),
                  jnp.einsum("bqk,bkd->bqd", pb, heads(v, 1), preferred_element_type=F32))
    o_ref[...] = o / den


def _sample_attn(sink_col, q3, k_all, v_all):
    nb = q3.shape[0]
    bb = 32
    blk = lambda a: pl.BlockSpec((bb,) + a.shape[1:], lambda b: (b, 0, 0))
    return pl.pallas_call(
        _sample_attn_body,
        grid=(nb // bb,),
        in_specs=[pl.BlockSpec(sink_col.shape, lambda b: (0, 0)), blk(q3), blk(k_all), blk(v_all)],
        out_specs=blk(q3),
        out_shape=jax.ShapeDtypeStruct(q3.shape, F32),
        compiler_params=_params(("arbitrary",)),
        name="attn_sample",
    )(sink_col, q3, k_all, v_all)


def _mix_sample_body(a_ref, vn_ref, sgb_ref, o_ref, x_ref, coef_ref, bias_ref,
                     wout_ref, g2_ref, wr_ref, br_ref, xmid_ref, xn2_ref, lg_ref):
    vn = vn_ref[...]
    n, width = vn.shape
    rows8 = lambda t: t.reshape(n // 8, 8, width)
    s = bias_ref[...][None] + coef_ref[0][None] * rows8(vn)
    for d in range(1, coef_ref.shape[0]):
        s = s + coef_ref[d][None] * rows8(pltpu.roll(vn, d, 0))
    merged = a_ref[...] * s.reshape(n, width) + sgb_ref[...] * o_ref[...]
    _finish_rows(merged.astype(BF16), x_ref[...], wout_ref, g2_ref, wr_ref, br_ref, xmid_ref, xn2_ref, lg_ref)


def _mix_sample(a, vn, sgb, o, x, coef, bias, wout, g2, wr, br):
    n = x.shape[0]
    args = (a, vn, sgb, o, x, coef, bias, wout, g2, wr, br)
    full = lambda arr: pl.BlockSpec(arr.shape, lambda i: (0,) * arr.ndim)
    return pl.pallas_call(
        _mix_sample_body,
        grid=(1,),
        in_specs=[full(arr) for arr in args],
        out_specs=[pl.BlockSpec((n, D_MODEL), lambda i: (0, 0)), pl.BlockSpec((n, D_MODEL // 2), lambda i: (0, 0)),
                   pl.BlockSpec((N_EXPERTS, n), lambda i: (0, 0))],
        out_shape=[jax.ShapeDtypeStruct((n, D_MODEL), F32),
                   jax.ShapeDtypeStruct((n, D_MODEL // 2), jnp.uint32),
                   jax.ShapeDtypeStruct((N_EXPERTS, n), F32)],
        compiler_params=_params(("arbitrary",)),
        name="mix_sample",
    )(*args)


def _rows8(rows, dtype):
    n = rows[0].shape[1]
    sub = lax.broadcasted_iota(jnp.int32, (8, n), 0)
    out = jnp.zeros((8, n), dtype)
    for kx, r in enumerate(rows):
        out = jnp.where(sub == kx, r.astype(dtype), out)
    return out


def _route_body(nblk, reserve, lg_ref, prior_ref, gate_ref, dest_ref, meta_ref, idx_s, rank_s, base):
    i = pl.program_id(0)

    @pl.when(i == 0)
    def _():
        base[...] = jnp.zeros_like(base)

    l = lg_ref[...]
    tb = l.shape[1]
    sub = lax.broadcasted_iota(jnp.int32, l.shape, 0).astype(F32)
    vals, idxs, sels = [], [], []
    for _ in range(TOP_K):
        m = jnp.max(l, axis=0, keepdims=True)
        ik = jnp.min(jnp.where(l == m, sub, float(N_EXPERTS)), axis=0, keepdims=True)
        sel = sub == ik
        l = jnp.where(sel, -jnp.inf, l)
        vals.append(m)
        idxs.append(ik)
        sels.append(sel)
    es = [jnp.exp(vk - vals[0]) for vk in vals]
    den = es[0] + es[1] + es[2] + es[3]
    onehot = jnp.zeros(l.shape, F32)
    for sel in sels:
        onehot = onehot + sel.astype(F32)
    earlier = (lax.broadcasted_iota(jnp.int32, (tb, tb), 0) < lax.broadcasted_iota(jnp.int32, (tb, tb), 1))
    before = jnp.dot(onehot.astype(BF16), earlier.astype(BF16), preferred_element_type=F32) + base[...]
    ranks = [jnp.sum(jnp.where(sel, before, 0.0), axis=0, keepdims=True) for sel in sels]
    base[...] += jnp.sum(onehot, axis=1, keepdims=True)
    idx_s[i] = _rows8(idxs, F32)
    rank_s[i] = _rows8(ranks, F32)
    gates = jnp.concatenate([_rows8([e / den for e in es], F32), jnp.zeros((LANES - 8, tb), F32)], axis=0)
    gate_ref[...] = gates.T

    @pl.when(i == nblk - 1)
    def _():
        cnt = base[...]
        if reserve is None:
            seg0 = prior_ref[:, 0:1]
            placed = prior_ref[:, 1:2]
        else:
            cap = jnp.ceil((cnt + reserve) / EXPERT_ROWS) * EXPERT_ROWS
            lower = (lax.broadcasted_iota(jnp.int32, (N_EXPERTS, N_EXPERTS), 1) <
                     lax.broadcasted_iota(jnp.int32, (N_EXPERTS, N_EXPERTS), 0)).astype(F32)
            seg0 = jnp.dot(lower, jnp.broadcast_to(cap, (N_EXPERTS, LANES)), preferred_element_type=F32,
                           precision=lax.Precision.HIGHEST)[:, :1]
            placed = jnp.zeros_like(cnt)
        total = placed + cnt
        lane = lax.broadcasted_iota(jnp.int32, (N_EXPERTS, LANES), 1)
        meta_ref[...] = jnp.where(lane == 0, seg0, jnp.where(lane == 1, total, jnp.where(
            lane == 2, seg0 / EXPERT_ROWS, jnp.where(lane == 3, jnp.ceil(total / EXPERT_ROWS), 0.0))))
        first = seg0 + placed
        sub_e = lax.broadcasted_iota(jnp.int32, (N_EXPERTS, tb), 0).astype(F32)
        for b in range(nblk):
            idx, rank = idx_s[b], rank_s[b]
            rows = [jnp.sum(jnp.where(sub_e == idx[kx:kx + 1], first, 0.0), axis=0, keepdims=True)
                    + rank[kx:kx + 1] for kx in range(TOP_K)]
            dest_ref[:, b * tb:(b + 1) * tb] = _rows8(rows, jnp.int32)


def _route(logits, prior, reserve):
    n = logits.shape[1]
    tb = min(n, ROUTE_ROWS)
    nblk = n // tb
    assert nblk * tb == n
    table = pl.BlockSpec((N_EXPERTS, LANES), lambda i: (0, 0))
    return pl.pallas_call(
        functools.partial(_route_body, nblk, reserve),
        grid=(nblk,),
        in_specs=[pl.BlockSpec((N_EXPERTS, tb), lambda i: (0, i)), table],
        out_specs=[pl.BlockSpec((tb, LANES), lambda i: (i, 0)), pl.BlockSpec((8, n), lambda i: (0, 0)), table],
        out_shape=[jax.ShapeDtypeStruct((n, LANES), F32),
                   jax.ShapeDtypeStruct((8, n), jnp.int32),
                   jax.ShapeDtypeStruct((N_EXPERTS, LANES), F32)],
        scratch_shapes=[pltpu.VMEM((nblk, 8, tb), F32), pltpu.VMEM((nblk, 8, tb), F32),
                        pltpu.VMEM((N_EXPERTS, 1), F32)],
        compiler_params=_params(("arbitrary",)),
        name="route",
    )(logits, prior)


def _sc_mesh():
    return plsc.VectorSubcoreMesh(core_axis_name="c", subcore_axis_name="s")


def _sc_worker():
    return lax.axis_index("s") * SC_CORES + lax.axis_index("c")


def _sc_dispatch(x, dest_t, n_slots):
    chunk = SC_ROWS
    n = x.shape[0]
    per_w = n // (SC_WORKERS * chunk)
    assert per_w * SC_WORKERS * chunk == n and per_w % 2 == 0
    d3 = dest_t.reshape(dest_t.shape[0], n // chunk, chunk)
    width, dtype = x.shape[1], x.dtype

    @functools.partial(
        pl.kernel, mesh=_sc_mesh(),
        out_type=jax.ShapeDtypeStruct((n_slots, width), dtype),
        scratch_types=[pltpu.VMEM((TOP_K, per_w, chunk), jnp.int32),
                       pltpu.VMEM((2, chunk, width), dtype),
                       pltpu.SemaphoreType.DMA, pltpu.SemaphoreType.DMA],
        compiler_params=pltpu.CompilerParams(use_tc_tiling_on_sc=True),
        name="dispatch")
    def run(x_hbm, d_hbm, out_hbm, idx_v, rows_v, rsem, wsem):
        wid = _sc_worker()
        pltpu.sync_copy(d_hbm.at[pl.ds(0, TOP_K), pl.ds(wid * per_w, per_w)], idx_v)

        def read(j, slot):
            return pltpu.make_async_copy(x_hbm.at[pl.ds((wid * per_w + j) * chunk, chunk)], rows_v.at[slot], rsem)

        def scatter(j, slot):
            copies = [pltpu.async_copy(rows_v.at[slot], out_hbm.at[idx_v.at[kx, j]], wsem) for kx in range(TOP_K)]
            for cp in copies:
                cp.wait()

        read(0, 0).start()

        def body(h, carry):
            j = 2 * h
            read(j, 0).wait()
            read(j + 1, 1).start()
            scatter(j, 0)
            read(j + 1, 1).wait()

            @pl.when(j + 2 < per_w)
            def _():
                read(j + 2, 0).start()

            scatter(j + 1, 1)
            return carry

        lax.fori_loop(0, per_w // 2, body, 0)

    return run(x, d3)


def _dispatch_small_body(dest_ref, x_ref, _, out_hbm, sem):
    n = x_ref.shape[0]

    def body(t, carry):
        for kx in range(TOP_K):
            pltpu.make_async_copy(x_ref.at[pl.ds(t, 1)], out_hbm.at[pl.ds(dest_ref[kx, t], 1)], sem).start()
        return carry

    lax.fori_loop(0, n, body, 0, unroll=4)
    for _ in range(TOP_K):
        pltpu.make_async_copy(x_ref, out_hbm.at[pl.ds(0, n)], sem).wait()


def _dispatch_small(x, dest_t, x_sorted):
    return pl.pallas_call(
        _dispatch_small_body,
        grid=(1,),
        in_specs=[pl.BlockSpec(memory_space=pltpu.SMEM), pl.BlockSpec(x.shape, lambda i: (0, 0)),
                  pl.BlockSpec(memory_space=pl.ANY)],
        out_specs=pl.BlockSpec(memory_space=pl.ANY),
        out_shape=jax.ShapeDtypeStruct(x_sorted.shape, x_sorted.dtype),
        scratch_shapes=[pltpu.SemaphoreType.DMA],
        input_output_aliases={2: 0},
        compiler_params=_params(("arbitrary",)),
        name="dispatch_small",
    )(dest_t, x, x_sorted)


def _sc_collect(y_sorted, dest_t, n_p, n_s):
    chunk = SC_ROWS
    per_choice = SC_WORKERS // TOP_K
    per_w = n_p // (per_choice * chunk)
    assert per_w * per_choice * chunk == n_p and per_w % 2 == 0
    assert n_s == per_choice * chunk
    d3 = dest_t.reshape(dest_t.shape[0], (n_p + n_s) // chunk, chunk)
    width, dtype = y_sorted.shape[1], y_sorted.dtype

    @functools.partial(
        pl.kernel, mesh=_sc_mesh(),
        out_type=[jax.ShapeDtypeStruct((TOP_K * n_p, width), dtype), jax.ShapeDtypeStruct((TOP_K * n_s, width), dtype)],
        scratch_types=[pltpu.VMEM((per_w, chunk), jnp.int32),
                       pltpu.VMEM((1, chunk), jnp.int32),
                       pltpu.VMEM((2, chunk, width), dtype),
                       pltpu.SemaphoreType.DMA, pltpu.SemaphoreType.DMA],
        compiler_params=pltpu.CompilerParams(use_tc_tiling_on_sc=True),
        name="collect")
    def run(y_hbm, d_hbm, op_hbm, os_hbm, ip_v, is_v, rows_v, gsem, wsem):
        wid = _sc_worker()
        choice = wid // per_choice
        part = wid % per_choice
        pltpu.sync_copy(d_hbm.at[choice, pl.ds(part * per_w, per_w)], ip_v)
        pltpu.sync_copy(d_hbm.at[choice, pl.ds(n_p // chunk + part, 1)], is_v)

        def gather(idx_v, j, slot):
            return pltpu.make_async_copy(y_hbm.at[idx_v.at[j]], rows_v.at[slot], gsem)

        def write(j, slot):
            return pltpu.make_async_copy(rows_v.at[slot], op_hbm.at[pl.ds((wid * per_w + j) * chunk, chunk)], wsem)

        gather(ip_v, 0, 0).start()

        def body(h, carry):
            j = 2 * h
            gather(ip_v, j, 0).wait()

            @pl.when(h > 0)
            def _():
                write(j - 1, 1).wait()

            gather(ip_v, j + 1, 1).start()
            write(j, 0).start()
            gather(ip_v, j + 1, 1).wait()
            write(j, 0).wait()

            @pl.when(j + 2 < per_w)
            def _():
                gather(ip_v, j + 2, 0).start()

            write(j + 1, 1).start()
            return carry

        lax.fori_loop(0, per_w // 2, body, 0)
        write(per_w - 1, 1).wait()

        gather(is_v, 0, 0).start()
        gather(is_v, 0, 0).wait()
        pltpu.sync_copy(rows_v.at[0], os_hbm.at[pl.ds(wid * chunk, chunk)])

    return run(y_sorted, d3)


def _expert_body(blk0_ref, nblk_ref, cnt_ref, wup_hbm, wdn_hbm, bup_ref, bdn_ref, x_hbm, y_hbm,
                 wup_f, wdn_f, wup_s, wdn_s, xbuf, obuf, w_sem, in_sem, out_sem):
    e = pl.program_id(0)
    nb = nblk_ref[e]
    blk0 = blk0_ref[e]
    cnt = cnt_ref[e]
    tm = EXPERT_ROWS
    pair = 2 * LANES
    wslot = e % 2
    up_rows = D_MODEL // W_PIECES
    dn_rows = D_MODEL // (W_PIECES // 2)

    def w_piece(hbm, buf, ex, slot, p, rows):
        start = p * rows if isinstance(p, int) else pl.multiple_of(p * rows, rows)
        r = pl.ds(start, rows)
        return pltpu.make_async_copy(hbm.at[ex, r], buf.at[slot, r], w_sem.at[slot])

    def w_start(ex, slot, p):
        w_piece(wup_hbm, wup_f, ex, slot, p, up_rows).start()
        if isinstance(p, int):
            if p < W_PIECES // 2:
                w_piece(wdn_hbm, wdn_f, ex, slot, p, dn_rows).start()
        else:
            @pl.when(p < W_PIECES // 2)
            def _():
                w_piece(wdn_hbm, wdn_f, ex, slot, p, dn_rows).start()

    def w_wait(ex, slot):
        for p in range(W_PIECES):
            w_piece(wup_hbm, wup_f, ex, slot, p, up_rows).wait()
        for p in range(W_PIECES // 2):
            w_piece(wdn_hbm, wdn_f, ex, slot, p, dn_rows).wait()

    @pl.when(e == 0)
    def _():
        for p in range(W_PIECES):
            w_start(0, 0, p)

    w_wait(e, wslot)
    wup_ref = wup_f.at[wslot]
    wdn_ref = wdn_f.at[wslot]
    more = e + 1 < N_EXPERTS

    def x_copy(i, slot):
        rows = pl.ds(pl.multiple_of((blk0 + i) * tm, tm), tm)
        return pltpu.make_async_copy(x_hbm.at[rows], xbuf.at[slot], in_sem.at[slot])

    def y_copy(i, slot):
        rows = pl.ds(pl.multiple_of((blk0 + i) * tm, tm), tm)
        return pltpu.make_async_copy(obuf.at[slot], y_hbm.at[rows], out_sem.at[slot])

    @pl.when(nb > 0)
    def _():
        x_copy(0, 0).start(priority=1)
        r = lax.broadcasted_iota(jnp.int32, (pair, pair), 0)
        c = lax.broadcasted_iota(jnp.int32, (pair, pair), 1)
        perm = (r == jnp.where(c < LANES, 2 * c, 2 * (c - LANES) + 1)).astype(BF16)
        for g in range(2 * D_MODEL // pair):
            cols = slice(g * pair, (g + 1) * pair)
            wup_s[g] = jnp.dot(wup_ref[:, cols].astype(BF16), perm, preferred_element_type=F32).astype(BF16)
        for g in range(D_MODEL // pair):
            wdn_s[g] = wdn_ref[:, g * pair:(g + 1) * pair].astype(BF16)

        def block(i, carry):
            slot = i % 2
            x_copy(i, slot).wait()

            @pl.when(i + 1 < nb)
            def _():
                x_copy(i + 1, 1 - slot).start(priority=1)

            @pl.when(i >= 2)
            def _():
                y_copy(i - 2, slot).wait()

            @pl.when(jnp.logical_and(more, i < W_PIECES))
            def _():
                w_start(e + 1, 1 - wslot, i)

            row = lax.broadcasted_iota(jnp.int32, (tm, 1), 0)
            x = jnp.where(row < cnt - i * tm, _unpack_bf16_pair(xbuf[slot]), 0.0).astype(BF16)
            acts = []
            for g in range(2 * D_MODEL // pair):
                cols = slice(g * pair, (g + 1) * pair)
                h = jnp.dot(x, wup_s[g], preferred_element_type=F32) + bup_ref[0, :, cols]
                glu = jnp.minimum(h[:, :LANES], SWIGLU_LIMIT)
                lin = jnp.clip(h[:, LANES:], -SWIGLU_LIMIT, SWIGLU_LIMIT)
                acts.append((glu * jax.nn.sigmoid(SWIGLU_ALPHA * glu) * (lin + 1.0)).astype(BF16))
            act = jnp.concatenate(acts, axis=1)
            half_groups = D_MODEL // pair // 2
            for g in range(half_groups):
                ys = []
                for gg in (g, g + half_groups):
                    cols = slice(gg * pair, (gg + 1) * pair)
                    ys.append(jnp.dot(act, wdn_s[gg], preferred_element_type=F32) + bdn_ref[0, :, cols])
                obuf[slot, :, g * pair:(g + 1) * pair] = _pack_bf16_pair(ys[0], ys[1])
            y_copy(i, slot).start(priority=1)
            return carry

        lax.fori_loop(0, nb, block, 0)

        @pl.when(nb >= 2)
        def _():
            y_copy(nb - 2, nb % 2).wait()

        y_copy(nb - 1, (nb - 1) % 2).wait()

    for p in range(W_PIECES):
        @pl.when(jnp.logical_and(more, p >= nb))
        def _():
            w_start(e + 1, 1 - wslot, p)


def _experts(blk0, nblk, cnt, x_sorted, w_up, w_down, b_up_grouped, b_down):
    tm = EXPERT_ROWS
    per_expert = lambda a: pl.BlockSpec((1,) + a.shape[1:], lambda e, b0, nb, ct: (e, 0, 0))
    hbm = pl.BlockSpec(memory_space=pl.ANY)
    grid_spec = pltpu.PrefetchScalarGridSpec(
        num_scalar_prefetch=3,
        grid=(N_EXPERTS,),
        in_specs=[hbm, hbm, per_expert(b_up_grouped), per_expert(b_down), hbm],
        out_specs=hbm,
        scratch_shapes=[pltpu.VMEM((2,) + w_up.shape[1:], F32), pltpu.VMEM((2,) + w_down.shape[1:], F32),
                        pltpu.VMEM((2 * D_MODEL // (2 * LANES), D_MODEL, 2 * LANES), BF16),
                        pltpu.VMEM((D_MODEL // (2 * LANES), D_MODEL, 2 * LANES), BF16),
                        pltpu.VMEM((2, tm, x_sorted.shape[1]), x_sorted.dtype),
                        pltpu.VMEM((2, tm, D_MODEL // 2), jnp.uint32),
                        pltpu.SemaphoreType.DMA((2,)), pltpu.SemaphoreType.DMA((2,)), pltpu.SemaphoreType.DMA((2,))],
    )
    return pl.pallas_call(
        _expert_body,
        grid_spec=grid_spec,
        out_shape=jax.ShapeDtypeStruct((x_sorted.shape[0], D_MODEL // 2), jnp.uint32),
        compiler_params=_params(("arbitrary",)),
        name="experts",
    )(blk0, nblk, cnt, w_up, w_down, b_up_grouped, b_down, x_sorted)


def _combine_body(gate_ref, xmid_ref, gfin_ref, y0_ref, y1_ref, y2_ref, y3_ref, out_ref):
    gate = gate_ref[...]
    moe = _unpack_bf16_pair(y0_ref[...]) * gate[:, 0:1]
    for kx, y_ref in enumerate((y1_ref, y2_ref, y3_ref), start=1):
        moe = moe + _unpack_bf16_pair(y_ref[...]) * gate[:, kx:kx + 1]
    out_ref[...] = _rms(xmid_ref[...] + moe, gfin_ref[...])


def _combine(gates, first_token, xmid, gfin, y_rows):
    n = xmid.shape[0]
    tt = min(n, COMBINE_ROWS)
    nblk = n // tt
    blk0 = first_token // tt
    assert blk0 * tt == first_token
    choice = lambda kx: pl.BlockSpec((tt, y_rows.shape[1]), lambda i: (i + kx * nblk, 0))
    return pl.pallas_call(
        _combine_body,
        grid=(nblk,),
        in_specs=[pl.BlockSpec((tt, LANES), lambda i: (i + blk0, 0)),
                  pl.BlockSpec((tt, D_MODEL), lambda i: (i, 0)),
                  pl.BlockSpec((1, D_MODEL), lambda i: (0, 0))] + [choice(kx) for kx in range(TOP_K)],
        out_specs=pl.BlockSpec((tt, D_MODEL), lambda i: (i, 0)),
        out_shape=jax.ShapeDtypeStruct((n, D_MODEL), F32),
        compiler_params=_params(("arbitrary",)),
        name="combine",
    )(gates, xmid, gfin, y_rows, y_rows, y_rows, y_rows)


def kernel(x_prompt, x_sample, cache_k_win, cache_v_win, norm_attn_g, w_in, ln_v_g, ln_v_b, w_spatial, b_spatial,
           attn_sinks, w_out, norm_ffn_g, w_router, b_router, w_up, b_up, w_down, b_down, norm_final_g):
    bp, tp, _ = x_prompt.shape
    bs, ts, _ = x_sample.shape
    w_buf = cache_k_win.shape[2]
    assert bp == 1 and tp % MIX_ROWS == 0 and w_buf == WINDOW and (bs * ts) % PROJ_ROWS == 0 and 8 % ts == 0
    n_p, n_s = bp * tp, bs * ts
    row2 = lambda a: a.reshape(1, -1)

    w_in_bf = w_in[0].astype(BF16)
    w_out_bf = w_out[0].astype(BF16)
    tril = jnp.tril(jnp.ones((CHUNK, CHUNK), dtype=bool))
    wsp = jnp.where(tril[None], w_spatial[0], 0.0)
    wsp_bf = wsp.astype(BF16)
    bsp = jnp.broadcast_to(b_spatial[0][:, :, None], (GMLP_GROUPS, CHUNK, LANES))
    b_up_grouped = b_up[0].reshape(N_EXPERTS, -1, LANES, 2).transpose(0, 1, 3, 2).reshape(N_EXPERTS, 1, -1)
    bd = b_down[0][:, None, :]
    g1, g2, gfin = row2(norm_attn_g[0]), row2(norm_ffn_g[0]), row2(norm_final_g)
    lng, lnb = row2(ln_v_g[0]), row2(ln_v_b[0])
    wr_hi = w_router[0].astype(BF16)
    wr = jnp.concatenate([wr_hi, (w_router[0] - wr_hi.astype(F32)).astype(BF16)], axis=1)
    br = row2(b_router[0])
    sinks = attn_sinks[0]

    xp = x_prompt.reshape(n_p, D_MODEL)
    cs_p = _rotary_inputs(jnp.arange(tp, dtype=jnp.int32))
    q_p, k_p, v_p, a_p, vn_p, sgb_p = _proj(xp, g1, w_in_bf, cs_p, lng, lnb)
    xmid_p, xn2_p, lg_p = _mix(sinks, q_p, k_p, v_p, a_p, vn_p, sgb_p, xp, wsp_bf, bsp, w_out_bf, g2, wr, br)

    xs = x_sample.reshape(n_s, D_MODEL)
    pos_s = PAST_LEN + jnp.arange(ts, dtype=jnp.int32)
    cs_s = _rotary_inputs(jnp.tile(pos_s, bs))
    q_s, k_s, v_s, a_s, vn_s, sgb_s = _proj(xs, g1, w_in_bf, cs_s, lng, lnb)
    n_keys = w_buf + ts
    key_pad = jnp.zeros((bs, (-n_keys) % 8, KV_WIDTH), F32)
    with_new = lambda cache, new: jnp.concatenate(
        [cache[0].reshape(bs, w_buf, KV_WIDTH), new.reshape(bs, ts, KV_WIDTH), key_pad], axis=1)
    k_all = with_new(cache_k_win, k_s)
    v_all = with_new(cache_v_win, v_s)
    sink_col = jnp.tile(sinks, ts).reshape(ts * N_HEADS, 1)
    o_s = _sample_attn(sink_col, q_s.reshape(bs, ts * N_HEADS, HEAD_DIM), k_all, v_all).reshape(n_s, Q_WIDTH)
    lag = np.arange(ts)[:, None] - np.arange(ts)[None, :]
    coef = jnp.stack([jnp.sum(jnp.where(lag == d, wsp[:, :ts, :ts], 0.0), axis=2)
                      for d in range(ts)])
    coef = jnp.repeat(coef.transpose(0, 2, 1), GMLP_WIDTH // GMLP_GROUPS, axis=2)
    coef = jnp.tile(coef, (1, 8 // ts, 1))
    bias = jnp.tile(jnp.repeat(b_spatial[0][:, :ts].T, GMLP_WIDTH // GMLP_GROUPS, axis=1), (8 // ts, 1))
    xmid_s, xn2_s, lg_s = _mix_sample(a_s, vn_s, sgb_s, o_s, xs, coef, bias, w_out_bf, g2, wr, br)

    tm = EXPERT_ROWS
    n_blocks = (n_p * TOP_K + N_EXPERTS * n_s) // tm + N_EXPERTS
    gate_p, dest_p, table_p = _route(lg_p, jnp.zeros((N_EXPERTS, LANES), F32), n_s)
    x_sorted = _sc_dispatch(xn2_p, dest_p, n_blocks * tm)
    gate_s, dest_s, table = _route(lg_s, table_p, None)
    x_sorted = _dispatch_small(xn2_s, dest_s, x_sorted)
    dest_t = jnp.concatenate([dest_p, dest_s], axis=1)
    meta = table.astype(jnp.int32)
    y_sorted = _experts(meta[:, 2], meta[:, 3], meta[:, 1], x_sorted, w_up[0], w_down[0], b_up_grouped, bd)
    yrows_p, yrows_s = _sc_collect(y_sorted, dest_t, n_p, n_s)
    y_p = _combine(gate_p, 0, xmid_p, gfin, yrows_p)
    y_s = _combine(gate_s, 0, xmid_s, gfin, yrows_s)

    k4 = lambda t: t.reshape(1, bp, -1, N_KV_HEADS, HEAD_DIM)
    return (y_p.reshape(bp, tp, D_MODEL),
            y_s.reshape(bs, ts, D_MODEL),
            k4(k_p[n_p - WINDOW:]),
            k4(v_p[n_p - WINDOW:]),
            vn_p[n_p - CHUNK:].reshape(1, bp, CHUNK, GMLP_WIDTH),
            k_all[:, ts:n_keys].reshape(1, bs, w_buf, N_KV_HEADS, HEAD_DIM),
            v_all[:, ts:n_keys].reshape(1, bs, w_buf, N_KV_HEADS, HEAD_DIM),
            vn_s.reshape(1, bs, ts, GMLP_WIDTH))
```

```python
import functools

import numpy as np
import jax
import jax.numpy as jnp
from jax import lax
from jax.experimental import pallas as pl
from jax.experimental.pallas import tpu as pltpu
from jax.experimental.pallas import tpu_sc as plsc

F32 = jnp.float32
BF16 = jnp.bfloat16

D_MODEL = 1024
HEAD_DIM = 64
N_HEADS = 16
GQA_GROUP = 8
N_KV_HEADS = 2
Q_WIDTH = 1024
KV_WIDTH = 128
WINDOW = 128
ROT_DIM = 16
ROPE_THETA = 500000.0
CHUNK = 128
GMLP_WIDTH = 1024
GMLP_GROUPS = 8
N_EXPERTS = 32
TOP_K = 4
SWIGLU_LIMIT = 7.0
SWIGLU_ALPHA = 1.702
RMS_EPS = 1e-5
LN_EPS = 1e-5
NEG_INF = -1e30
PAST_LEN = 16384

LANES = 128
VMEM_LIMIT = 56 * 1024 * 1024

PROJ_ROWS = 256
MIX_ROWS = 512
ROUTE_ROWS = 1024
EXPERT_ROWS = 256
W_PIECES = 8
COMBINE_ROWS = 1024

SC_CORES = 2
SC_WORKERS = 32
SC_ROWS = 64

_C_Q, _C_KV, _C_U, _C_VG, _C_GA, _C_GB, _C_END = 0, 1024, 1280, 2304, 3328, 4352, 5376


def _params(sem):
    return pltpu.CompilerParams(dimension_semantics=sem, vmem_limit_bytes=VMEM_LIMIT)


def _rms(x, g):
    return x * lax.rsqrt(jnp.mean(x * x, axis=-1, keepdims=True) + RMS_EPS) * g


def _pack_bf16_pair(lo, hi):
    lo_bits = lax.bitcast_convert_type(lo.astype(BF16).astype(F32), jnp.uint32)
    hi_bits = lax.bitcast_convert_type(hi.astype(BF16).astype(F32), jnp.uint32)
    return (lo_bits >> 16) | hi_bits


def _unpack_bf16_pair(words):
    lo = lax.bitcast_convert_type(words << 16, F32)
    hi = lax.bitcast_convert_type(words & jnp.uint32(0xFFFF0000), F32)
    return jnp.concatenate([lo, hi], axis=1)


def _proj_body(x_ref, g_ref, w_ref, cs_ref, rot_ref, lng_ref, lnb_ref,
               q_ref, k_ref, v_ref, a_ref, vn_ref, sgb_ref):
    h = _rms(x_ref[...], g_ref[...]).astype(BF16)
    tabs = lax.dot_general(cs_ref[...], rot_ref[...], (((0,), (0,)), ((), ())), preferred_element_type=F32)
    rc, rs1, rs2 = tabs[:, :LANES], tabs[:, LANES:2 * LANES], tabs[:, 2 * LANES:]

    def rot(z):
        return z * rc + pltpu.roll(z, LANES - ROT_DIM // 2, 1) * rs1 + pltpu.roll(z, ROT_DIM // 2, 1) * rs2

    def mm(lo, hi):
        return jnp.dot(h, w_ref[:, lo:hi], preferred_element_type=F32)

    zq = mm(_C_Q, _C_KV)
    for c in range(Q_WIDTH // LANES):
        sl = slice(c * LANES, (c + 1) * LANES)
        q_ref[:, sl] = (rot(zq[:, sl]) * (HEAD_DIM ** -0.5)).astype(BF16)
    zkv = mm(_C_KV, _C_U)
    k_ref[...] = rot(zkv[:, :KV_WIDTH])
    v_ref[...] = zkv[:, KV_WIDTH:]
    a_ref[...] = jax.nn.sigmoid(mm(_C_GA, _C_GB)) * jax.nn.gelu(mm(_C_U, _C_VG))
    zv = jax.nn.gelu(mm(_C_VG, _C_GA))
    zc = zv - jnp.mean(zv, axis=-1, keepdims=True)
    var = jnp.mean(zc * zc, axis=-1, keepdims=True)
    vn_ref[...] = zc * lax.rsqrt(var + LN_EPS) * lng_ref[...] + lnb_ref[...]
    sgb_ref[...] = jax.nn.sigmoid(mm(_C_GB, _C_END))


def _proj(x, norm_g, w_in_bf, cs, ln_g, ln_b):
    n = x.shape[0]
    tm = PROJ_ROWS
    row = lambda w: pl.BlockSpec((tm, w), lambda i: (i, 0))
    full = lambda a: pl.BlockSpec(a.shape, lambda i: (0,) * a.ndim)
    rot = jnp.asarray(np.tile(_ROT_EXPAND, (3, 1)), dtype=BF16)
    return pl.pallas_call(
        _proj_body,
        grid=(n // tm,),
        in_specs=[row(D_MODEL), full(norm_g), full(w_in_bf), pl.BlockSpec((cs.shape[0], tm), lambda i: (0, i)),
                  full(rot),
                  full(ln_g), full(ln_b)],
        out_specs=[row(Q_WIDTH), row(KV_WIDTH), row(KV_WIDTH), row(GMLP_WIDTH), row(GMLP_WIDTH), row(D_MODEL)],
        out_shape=[jax.ShapeDtypeStruct((n, Q_WIDTH), BF16),
                   jax.ShapeDtypeStruct((n, KV_WIDTH), F32),
                   jax.ShapeDtypeStruct((n, KV_WIDTH), F32),
                   jax.ShapeDtypeStruct((n, GMLP_WIDTH), F32),
                   jax.ShapeDtypeStruct((n, GMLP_WIDTH), F32),
                   jax.ShapeDtypeStruct((n, D_MODEL), F32)],
        compiler_params=_params(("arbitrary",)),
        name="proj",
    )(x, norm_g, w_in_bf, cs, rot, ln_g, ln_b)


_ROT_COLS = 32


def _rot_expand():
    half = ROT_DIM // 2
    m = np.zeros((_ROT_COLS, 3 * LANES), np.float32)
    for lane in range(LANES):
        d = lane % HEAD_DIM
        if d < ROT_DIM:
            m[d % half, lane] = 1.0
        else:
            m[2 * half, lane] = 1.0
        if d < half:
            m[half + d, LANES + lane] = -1.0
        elif d < ROT_DIM:
            m[half + d - half, 2 * LANES + lane] = 1.0
    return m


_ROT_EXPAND = _rot_expand()


def _rotary_inputs(pos):
    half = ROT_DIM // 2
    inv_freq = ROPE_THETA ** (-jnp.arange(half, dtype=F32) / half)
    ang = inv_freq[:, None] * pos.astype(F32)[None, :]
    n = pos.shape[0]
    cs = jnp.concatenate([jnp.cos(ang), jnp.sin(ang), jnp.ones((1, n), F32),
                          jnp.zeros((_ROT_COLS - 2 * half - 1, n), F32)], axis=0)
    hi = cs.astype(BF16)
    rest = cs - hi.astype(F32)
    mid = rest.astype(BF16)
    lo = (rest - mid.astype(F32)).astype(BF16)
    return jnp.concatenate([hi, mid, lo], axis=0)


def _finish_rows(merged_bf, x, wout_ref, g2_ref, wr_ref, br_ref, xmid_ref, xn2_ref, lg_ref):
    xm = x + jnp.dot(merged_bf, wout_ref[...], preferred_element_type=F32)
    xmid_ref[...] = xm
    xn = _rms(xm, g2_ref[...])
    x_hi = xn.astype(BF16)
    x_lo = (xn - x_hi.astype(F32)).astype(BF16)
    w_hl = wr_ref[...]
    p_hi = jnp.dot(x_hi, w_hl, preferred_element_type=F32)
    p_lo = jnp.dot(x_lo, w_hl[:, :N_EXPERTS], preferred_element_type=F32)
    lg = p_hi[:, :N_EXPERTS] + (p_hi[:, N_EXPERTS:] + p_lo) + br_ref[...]
    wide = jnp.concatenate([lg, jnp.zeros((lg.shape[0], LANES - N_EXPERTS), F32)], axis=1)
    lg_ref[...] = wide.T[:N_EXPERTS]
    xn2_ref[...] = _pack_bf16_pair(xn[:, :D_MODEL // 2], xn[:, D_MODEL // 2:])


def _mix_body(sinks_ref, q_ref, k_ref, kp_ref, v_ref, vp_ref, a_ref, vn_ref, sgb_ref, x_ref,
              wsp_ref, bsp_ref, wout_ref, g2_ref, wr_ref, br_ref,
              xmid_ref, xn2_ref, lg_ref, kcat, vcat, mrg, key_s):
    i = pl.program_id(0)
    nsub = MIX_ROWS // WINDOW
    kcat[0:WINDOW] = kp_ref[...]
    kcat[WINDOW:] = k_ref[...]
    vcat[0:WINDOW] = vp_ref[...]
    vcat[WINDOW:] = v_ref[...]

    pair_rows = (GQA_GROUP // 2) * WINDOW
    lane_kv = lax.broadcasted_iota(jnp.int32, (2 * WINDOW, LANES), 1)
    lane_o = lax.broadcasted_iota(jnp.int32, (pair_rows, LANES), 1)

    @pl.when(i == 0)
    def _():
        rq = lax.broadcasted_iota(jnp.int32, (pair_rows, 4 * WINDOW), 0) & (WINDOW - 1)
        ck = lax.broadcasted_iota(jnp.int32, (pair_rows, 4 * WINDOW), 1) & (2 * WINDOW - 1)
        key_s[...] = jnp.where((ck > rq) & (ck <= rq + WINDOW), ck, -1)

    row_p = lax.broadcasted_iota(jnp.int32, (pair_rows, 1), 0) >> 7
    sink_cols = []
    for kk in range(N_KV_HEADS):
        h0 = kk * GQA_GROUP
        se = jnp.full((pair_rows, 1), sinks_ref[h0], F32)
        so = jnp.full((pair_rows, 1), sinks_ref[h0 + 1], F32)
        for p in range(1, GQA_GROUP // 2):
            se = jnp.where(row_p == p, sinks_ref[h0 + 2 * p], se)
            so = jnp.where(row_p == p, sinks_ref[h0 + 2 * p + 1], so)
        sink_cols.append((se, so))

    def sub(j, carry):
        off = pl.multiple_of(j * WINDOW, WINDOW)
        rows = pl.ds(off, WINDOW)
        for g in range(GMLP_GROUPS):
            cols = slice(g * LANES, (g + 1) * LANES)
            s = jnp.dot(wsp_ref[g], vn_ref[rows, cols].astype(BF16), preferred_element_type=F32) + bsp_ref[g]
            mrg[rows, cols] = a_ref[rows, cols] * s
        kblk = kcat[pl.ds(off, 2 * WINDOW), :]
        vblk = vcat[pl.ds(off, 2 * WINDOW), :]
        kswp = pltpu.roll(kblk, HEAD_DIM, 1)
        vswp = pltpu.roll(vblk, HEAD_DIM, 1)
        kmin = jnp.where(jnp.logical_and(i == 0, j == 0), WINDOW, 0)
        allowed = key_s[...] >= kmin
        for kk in range(N_KV_HEADS):
            lo_src, hi_src = (kblk, kswp) if kk == 0 else (kswp, kblk)
            kbd = jnp.concatenate([jnp.where(lane_kv < HEAD_DIM, lo_src, 0.0),
                                   jnp.where(lane_kv >= HEAD_DIM, hi_src, 0.0)], axis=0).astype(BF16)
            lo_src, hi_src = (vblk, vswp) if kk == 0 else (vswp, vblk)
            vbd = jnp.concatenate([jnp.where(lane_kv < HEAD_DIM, lo_src, 0.0),
                                   jnp.where(lane_kv >= HEAD_DIM, hi_src, 0.0)], axis=0).astype(BF16)
            pair0 = kk * (GQA_GROUP // 2)
            qs = jnp.concatenate([q_ref[rows, (pair0 + p) * LANES:(pair0 + p + 1) * LANES]
                                  for p in range(GQA_GROUP // 2)], axis=0)
            lg = lax.dot_general(qs, kbd, (((1,), (1,)), ((), ())), preferred_element_type=F32)
            lg = jnp.where(allowed, lg, NEG_INF)
            se, so = sink_cols[kk]
            le, lo = lg[:, :2 * WINDOW], lg[:, 2 * WINDOW:]
            me = jnp.maximum(jnp.max(le, axis=1, keepdims=True), se)
            mo = jnp.maximum(jnp.max(lo, axis=1, keepdims=True), so)
            pe = jnp.exp(le - me)
            po = jnp.exp(lo - mo)
            de = jnp.sum(pe, axis=1, keepdims=True) + jnp.exp(se - me)
            do = jnp.sum(po, axis=1, keepdims=True) + jnp.exp(so - mo)
            pr = jnp.concatenate([pe, po], axis=1).astype(BF16)
            o = jnp.dot(pr, vbd, preferred_element_type=F32)
            o = o / jnp.where(lane_o < HEAD_DIM, de, do)
            for p in range(GQA_GROUP // 2):
                cols = slice((pair0 + p) * LANES, (pair0 + p + 1) * LANES)
                mrg[rows, cols] += sgb_ref[rows, cols] * o[p * WINDOW:(p + 1) * WINDOW]
        return carry

    lax.fori_loop(0, nsub, sub, 0)
    _finish_rows(mrg[...].astype(BF16), x_ref[...], wout_ref, g2_ref, wr_ref, br_ref, xmid_ref, xn2_ref, lg_ref)


def _mix(sinks, q, k, v, a, vn, sgb, x, wsp, bsp, wout, g2, wr, br):
    n = x.shape[0]
    tm = MIX_ROWS
    nsub = tm // WINDOW
    row = lambda w: pl.BlockSpec((tm, w), lambda i: (i, 0))
    prev = pl.BlockSpec((WINDOW, KV_WIDTH), lambda i: (jnp.maximum(i * nsub - 1, 0), 0))
    full = lambda arr: pl.BlockSpec(arr.shape, lambda i: (0,) * arr.ndim)
    smem = pl.BlockSpec(memory_space=pltpu.SMEM)
    return pl.pallas_call(
        _mix_body,
        grid=(n // tm,),
        in_specs=[smem, row(Q_WIDTH), row(KV_WIDTH), prev, row(KV_WIDTH), prev,
                  row(GMLP_WIDTH), row(GMLP_WIDTH), row(D_MODEL), row(D_MODEL),
                  full(wsp), full(bsp), full(wout), full(g2), full(wr), full(br)],
        out_specs=[row(D_MODEL), row(D_MODEL // 2), pl.BlockSpec((N_EXPERTS, tm), lambda i: (0, i))],
        out_shape=[jax.ShapeDtypeStruct((n, D_MODEL), F32),
                   jax.ShapeDtypeStruct((n, D_MODEL // 2), jnp.uint32),
                   jax.ShapeDtypeStruct((N_EXPERTS, n), F32)],
        scratch_shapes=[pltpu.VMEM((tm + WINDOW, KV_WIDTH), F32),
                        pltpu.VMEM((tm + WINDOW, KV_WIDTH), F32),
                        pltpu.VMEM((tm, D_MODEL), F32),
                        pltpu.VMEM((GQA_GROUP // 2 * WINDOW, 4 * WINDOW), jnp.int32)],
        compiler_params=_params(("arbitrary",)),
        name="mix_prompt",
    )(sinks, q, k, k, v, v, a, vn, sgb, x, wsp, bsp, wout, g2, wr, br)


def _sample_attn_body(sink_ref, q_ref, k_ref, v_ref, o_ref):
    q = q_ref[...]
    k = k_ref[...]
    v = v_ref[...]
    nq, nk = q.shape[1], k.shape[1]
    heads = lambda t, kk: t[:, :, kk * HEAD_DIM:(kk + 1) * HEAD_DIM].astype(BF16)
    row = lax.broadcasted_iota(jnp.int32, (1, nq, nk), 1)
    first_kv = (row % N_HEADS) < GQA_GROUP
    lg = jnp.where(first_kv,
                   jnp.einsum("bqd,bkd->bqk", q, heads(k, 0), preferred_element_type=F32),
                   jnp.einsum("bqd,bkd->bqk", q, heads(k, 1), preferred_element_type=F32))
    t = row // N_HEADS
    j = lax.broadcasted_iota(jnp.int32, (1, nq, nk), 2)
    lg = jnp.where((j > t) & (j <= t + WINDOW), lg, NEG_INF)
    sink = sink_ref[...][None]
    m = jnp.maximum(jnp.max(lg, axis=2, keepdims=True), sink)
    p = jnp.exp(lg - m)
    den = jnp.sum(p, axis=2, keepdims=True) + jnp.exp(sink - m)
    pb = p.astype(BF16)
    row_o = lax.broadcasted_iota(jnp.int32, (1, nq, HEAD_DIM), 1)
    o = jnp.where((row_o % N_HEADS) < GQA_GROUP,
                  jnp.einsum("bqk,bkd->bqd", pb, heads(v, 0), preferred_element_type=F32),
                  jnp.einsum("bqk,bkd->bqd", pb, heads(v, 1), preferred_element_type=F32))
    o_ref[...] = o / den


def _sample_attn(sink_col, q3, k_all, v_all):
    nb = q3.shape[0]
    bb = 32
    blk = lambda a: pl.BlockSpec((bb,) + a.shape[1:], lambda b: (b, 0, 0))
    return pl.pallas_call(
        _sample_attn_body,
        grid=(nb // bb,),
        in_specs=[pl.BlockSpec(sink_col.shape, lambda b: (0, 0)), blk(q3), blk(k_all), blk(v_all)],
        out_specs=blk(q3),
        out_shape=jax.ShapeDtypeStruct(q3.shape, F32),
        compiler_params=_params(("arbitrary",)),
        name="attn_sample",
    )(sink_col, q3, k_all, v_all)


def _window_body(c_ref, n_ref, o_ref):
    old = pltpu.roll(c_ref[...], c_ref.shape[2] - n_ref.shape[2], 2)
    lane = lax.broadcasted_iota(jnp.int32, old.shape, 2)
    new = n_ref[...]
    first_new = old.shape[2] - new.shape[2]
    out = old
    for t in range(new.shape[2]):
        out = jnp.where(lane == first_new + t, new[:, :, t:t + 1], out)
    o_ref[...] = out


def _window(cache_t, new_t):
    nb = cache_t.shape[0]
    bb = 16
    blk = lambda a: pl.BlockSpec((bb,) + a.shape[1:], lambda b: (b, 0, 0))
    return pl.pallas_call(
        _window_body,
        grid=(nb // bb,),
        in_specs=[blk(cache_t), blk(new_t)],
        out_specs=blk(cache_t),
        out_shape=jax.ShapeDtypeStruct(cache_t.shape, cache_t.dtype),
        compiler_params=_params(("arbitrary",)),
        name="window",
    )(cache_t, new_t)


def _mix_sample_body(a_ref, vn_ref, sgb_ref, o_ref, x_ref, coef_ref, bias_ref,
                     wout_ref, g2_ref, wr_ref, br_ref, xmid_ref, xn2_ref, lg_ref):
    vn = vn_ref[...]
    n, width = vn.shape
    rows8 = lambda t: t.reshape(n // 8, 8, width)
    s = bias_ref[...][None] + coef_ref[0][None] * rows8(vn)
    for d in range(1, coef_ref.shape[0]):
        s = s + coef_ref[d][None] * rows8(pltpu.roll(vn, d, 0))
    merged = a_ref[...] * s.reshape(n, width) + sgb_ref[...] * o_ref[...]
    _finish_rows(merged.astype(BF16), x_ref[...], wout_ref, g2_ref, wr_ref, br_ref, xmid_ref, xn2_ref, lg_ref)


def _mix_sample(a, vn, sgb, o, x, coef, bias, wout, g2, wr, br):
    n = x.shape[0]
    args = (a, vn, sgb, o, x, coef, bias, wout, g2, wr, br)
    full = lambda arr: pl.BlockSpec(arr.shape, lambda i: (0,) * arr.ndim)
    return pl.pallas_call(
        _mix_sample_body,
        grid=(1,),
        in_specs=[full(arr) for arr in args],
        out_specs=[pl.BlockSpec((n, D_MODEL), lambda i: (0, 0)), pl.BlockSpec((n, D_MODEL // 2), lambda i: (0, 0)),
                   pl.BlockSpec((N_EXPERTS, n), lambda i: (0, 0))],
        out_shape=[jax.ShapeDtypeStruct((n, D_MODEL), F32),
                   jax.ShapeDtypeStruct((n, D_MODEL // 2), jnp.uint32),
                   jax.ShapeDtypeStruct((N_EXPERTS, n), F32)],
        compiler_params=_params(("arbitrary",)),
        name="mix_sample",
    )(*args)


def _rows8(rows, dtype):
    n = rows[0].shape[1]
    sub = lax.broadcasted_iota(jnp.int32, (8, n), 0)
    out = jnp.zeros((8, n), dtype)
    for kx, r in enumerate(rows):
        out = jnp.where(sub == kx, r.astype(dtype), out)
    return out


def _route_body(nblk, reserve, lg_ref, prior_ref, gate_ref, dest_ref, meta_ref, idx_s, rank_s, base):
    i = pl.program_id(0)

    @pl.when(i == 0)
    def _():
        base[...] = jnp.zeros_like(base)

    l = lg_ref[...]
    tb = l.shape[1]
    sub = lax.broadcasted_iota(jnp.int32, l.shape, 0).astype(F32)
    vals, idxs, sels = [], [], []
    for _ in range(TOP_K):
        m = jnp.max(l, axis=0, keepdims=True)
        ik = jnp.min(jnp.where(l == m, sub, float(N_EXPERTS)), axis=0, keepdims=True)
        sel = sub == ik
        l = jnp.where(sel, -jnp.inf, l)
        vals.append(m)
        idxs.append(ik)
        sels.append(sel)
    es = [jnp.exp(vk - vals[0]) for vk in vals]
    den = es[0] + es[1] + es[2] + es[3]
    onehot = jnp.zeros(l.shape, F32)
    for sel in sels:
        onehot = onehot + sel.astype(F32)
    earlier = (lax.broadcasted_iota(jnp.int32, (tb, tb), 0) < lax.broadcasted_iota(jnp.int32, (tb, tb), 1))
    before = jnp.dot(onehot.astype(BF16), earlier.astype(BF16), preferred_element_type=F32) + base[...]
    ranks = [jnp.sum(jnp.where(sel, before, 0.0), axis=0, keepdims=True) for sel in sels]
    base[...] += jnp.sum(onehot, axis=1, keepdims=True)
    idx_s[i] = _rows8(idxs, F32)
    rank_s[i] = _rows8(ranks, F32)
    gates = jnp.concatenate([_rows8([e / den for e in es], F32), jnp.zeros((LANES - 8, tb), F32)], axis=0)
    gate_ref[...] = gates.T

    @pl.when(i == nblk - 1)
    def _():
        cnt = base[...]
        if reserve is None:
            seg0 = prior_ref[:, 0:1]
            placed = prior_ref[:, 1:2]
        else:
            cap = jnp.ceil((cnt + reserve) / EXPERT_ROWS) * EXPERT_ROWS
            lower = (lax.broadcasted_iota(jnp.int32, (N_EXPERTS, N_EXPERTS), 1) <
                     lax.broadcasted_iota(jnp.int32, (N_EXPERTS, N_EXPERTS), 0)).astype(F32)
            seg0 = jnp.dot(lower, jnp.broadcast_to(cap, (N_EXPERTS, LANES)), preferred_element_type=F32,
                           precision=lax.Precision.HIGHEST)[:, :1]
            placed = jnp.zeros_like(cnt)
        total = placed + cnt
        lane = lax.broadcasted_iota(jnp.int32, (N_EXPERTS, LANES), 1)
        meta_ref[...] = jnp.where(lane == 0, seg0, jnp.where(lane == 1, total, jnp.where(
            lane == 2, seg0 / EXPERT_ROWS, jnp.where(lane == 3, jnp.ceil(total / EXPERT_ROWS), 0.0))))
        first = seg0 + placed
        sub_e = lax.broadcasted_iota(jnp.int32, (N_EXPERTS, tb), 0).astype(F32)
        for b in range(nblk):
            idx, rank = idx_s[b], rank_s[b]
            rows = [jnp.sum(jnp.where(sub_e == idx[kx:kx + 1], first, 0.0), axis=0, keepdims=True)
                    + rank[kx:kx + 1] for kx in range(TOP_K)]
            dest_ref[:, b * tb:(b + 1) * tb] = _rows8(rows, jnp.int32)


def _route(logits, prior, reserve):
    n = logits.shape[1]
    tb = min(n, ROUTE_ROWS)
    nblk = n // tb
    assert nblk * tb == n
    table = pl.BlockSpec((N_EXPERTS, LANES), lambda i: (0, 0))
    return pl.pallas_call(
        functools.partial(_route_body, nblk, reserve),
        grid=(nblk,),
        in_specs=[pl.BlockSpec((N_EXPERTS, tb), lambda i: (0, i)), table],
        out_specs=[pl.BlockSpec((tb, LANES), lambda i: (i, 0)), pl.BlockSpec((8, n), lambda i: (0, 0)), table],
        out_shape=[jax.ShapeDtypeStruct((n, LANES), F32),
                   jax.ShapeDtypeStruct((8, n), jnp.int32),
                   jax.ShapeDtypeStruct((N_EXPERTS, LANES), F32)],
        scratch_shapes=[pltpu.VMEM((nblk, 8, tb), F32), pltpu.VMEM((nblk, 8, tb), F32),
                        pltpu.VMEM((N_EXPERTS, 1), F32)],
        compiler_params=_params(("arbitrary",)),
        name="route",
    )(logits, prior)


def _sc_mesh():
    return plsc.VectorSubcoreMesh(core_axis_name="c", subcore_axis_name="s")


def _sc_worker():
    return lax.axis_index("s") * SC_CORES + lax.axis_index("c")


def _sc_dispatch(x, dest_t, n_slots):
    chunk = SC_ROWS
    n = x.shape[0]
    per_w = n // (SC_WORKERS * chunk)
    assert per_w * SC_WORKERS * chunk == n and per_w % 2 == 0
    d3 = dest_t.reshape(dest_t.shape[0], n // chunk, chunk)
    width, dtype = x.shape[1], x.dtype

    @functools.partial(
        pl.kernel, mesh=_sc_mesh(),
        out_type=jax.ShapeDtypeStruct((n_slots, width), dtype),
        scratch_types=[pltpu.VMEM((TOP_K, per_w, chunk), jnp.int32),
                       pltpu.VMEM((2, chunk, width), dtype),
                       pltpu.SemaphoreType.DMA, pltpu.SemaphoreType.DMA],
        compiler_params=pltpu.CompilerParams(use_tc_tiling_on_sc=True),
        name="dispatch")
    def run(x_hbm, d_hbm, out_hbm, idx_v, rows_v, rsem, wsem):
        wid = _sc_worker()
        pltpu.sync_copy(d_hbm.at[pl.ds(0, TOP_K), pl.ds(wid * per_w, per_w)], idx_v)

        def read(j, slot):
            return pltpu.make_async_copy(x_hbm.at[pl.ds((wid * per_w + j) * chunk, chunk)], rows_v.at[slot], rsem)

        def scatter(j, slot):
            copies = [pltpu.async_copy(rows_v.at[slot], out_hbm.at[idx_v.at[kx, j]], wsem) for kx in range(TOP_K)]
            for cp in copies:
                cp.wait()

        read(0, 0).start()

        def body(h, carry):
            j = 2 * h
            read(j, 0).wait()
            read(j + 1, 1).start()
            scatter(j, 0)
            read(j + 1, 1).wait()

            @pl.when(j + 2 < per_w)
            def _():
                read(j + 2, 0).start()

            scatter(j + 1, 1)
            return carry

        lax.fori_loop(0, per_w // 2, body, 0)

    return run(x, d3)


def _dispatch_small_body(dest_ref, x_ref, _, out_hbm, sem):
    n = x_ref.shape[0]

    def body(t, carry):
        for kx in range(TOP_K):
            pltpu.make_async_copy(x_ref.at[pl.ds(t, 1)], out_hbm.at[pl.ds(dest_ref[kx, t], 1)], sem).start()
        return carry

    lax.fori_loop(0, n, body, 0, unroll=4)
    for _ in range(TOP_K):
        pltpu.make_async_copy(x_ref, out_hbm.at[pl.ds(0, n)], sem).wait()


def _dispatch_small(x, dest_t, x_sorted):
    return pl.pallas_call(
        _dispatch_small_body,
        grid=(1,),
        in_specs=[pl.BlockSpec(memory_space=pltpu.SMEM), pl.BlockSpec(x.shape, lambda i: (0, 0)),
                  pl.BlockSpec(memory_space=pl.ANY)],
        out_specs=pl.BlockSpec(memory_space=pl.ANY),
        out_shape=jax.ShapeDtypeStruct(x_sorted.shape, x_sorted.dtype),
        scratch_shapes=[pltpu.SemaphoreType.DMA],
        input_output_aliases={2: 0},
        compiler_params=_params(("arbitrary",)),
        name="dispatch_small",
    )(dest_t, x, x_sorted)


def _sc_collect(y_sorted, dest_t, n_p, n_s):
    chunk = SC_ROWS
    per_choice = SC_WORKERS // TOP_K
    per_w = n_p // (per_choice * chunk)
    assert per_w * per_choice * chunk == n_p and per_w % 2 == 0
    assert n_s == per_choice * chunk
    d3 = dest_t.reshape(dest_t.shape[0], (n_p + n_s) // chunk, chunk)
    width, dtype = y_sorted.shape[1], y_sorted.dtype

    @functools.partial(
        pl.kernel, mesh=_sc_mesh(),
        out_type=[jax.ShapeDtypeStruct((TOP_K * n_p, width), dtype), jax.ShapeDtypeStruct((TOP_K * n_s, width), dtype)],
        scratch_types=[pltpu.VMEM((per_w, chunk), jnp.int32),
                       pltpu.VMEM((1, chunk), jnp.int32),
                       pltpu.VMEM((2, chunk, width), dtype),
                       pltpu.SemaphoreType.DMA, pltpu.SemaphoreType.DMA],
        compiler_params=pltpu.CompilerParams(use_tc_tiling_on_sc=True),
        name="collect")
    def run(y_hbm, d_hbm, op_hbm, os_hbm, ip_v, is_v, rows_v, gsem, wsem):
        wid = _sc_worker()
        choice = wid // per_choice
        part = wid % per_choice
        pltpu.sync_copy(d_hbm.at[choice, pl.ds(part * per_w, per_w)], ip_v)
        pltpu.sync_copy(d_hbm.at[choice, pl.ds(n_p // chunk + part, 1)], is_v)

        def gather(idx_v, j, slot):
            return pltpu.make_async_copy(y_hbm.at[idx_v.at[j]], rows_v.at[slot], gsem)

        def write(j, slot):
            return pltpu.make_async_copy(rows_v.at[slot], op_hbm.at[pl.ds((wid * per_w + j) * chunk, chunk)], wsem)

        gather(ip_v, 0, 0).start()

        def body(h, carry):
            j = 2 * h
            gather(ip_v, j, 0).wait()

            @pl.when(h > 0)
            def _():
                write(j - 1, 1).wait()

            gather(ip_v, j + 1, 1).start()
            write(j, 0).start()
            gather(ip_v, j + 1, 1).wait()
            write(j, 0).wait()

            @pl.when(j + 2 < per_w)
            def _():
                gather(ip_v, j + 2, 0).start()

            write(j + 1, 1).start()
            return carry

        lax.fori_loop(0, per_w // 2, body, 0)
        write(per_w - 1, 1).wait()

        gather(is_v, 0, 0).start()
        gather(is_v, 0, 0).wait()
        pltpu.sync_copy(rows_v.at[0], os_hbm.at[pl.ds(wid * chunk, chunk)])

    return run(y_sorted, d3)


def _expert_body(blk0_ref, nblk_ref, cnt_ref, wup_hbm, wdn_hbm, bup_ref, bdn_ref, x_hbm, y_hbm,
                 wup_f, wdn_f, wup_s, wdn_s, xbuf, obuf, w_sem, in_sem, out_sem):
    e = pl.program_id(0)
    nb = nblk_ref[e]
    blk0 = blk0_ref[e]
    cnt = cnt_ref[e]
    tm = EXPERT_ROWS
    pair = 2 * LANES
    wslot = e % 2
    up_rows = D_MODEL // W_PIECES
    dn_rows = D_MODEL // (W_PIECES // 2)

    def w_piece(hbm, buf, ex, slot, p, rows):
        start = p * rows if isinstance(p, int) else pl.multiple_of(p * rows, rows)
        r = pl.ds(start, rows)
        return pltpu.make_async_copy(hbm.at[ex, r], buf.at[slot, r], w_sem.at[slot])

    def w_start(ex, slot, p):
        w_piece(wup_hbm, wup_f, ex, slot, p, up_rows).start()
        if isinstance(p, int):
            if p < W_PIECES // 2:
                w_piece(wdn_hbm, wdn_f, ex, slot, p, dn_rows).start()
        else:
            @pl.when(p < W_PIECES // 2)
            def _():
                w_piece(wdn_hbm, wdn_f, ex, slot, p, dn_rows).start()

    def w_wait(ex, slot):
        for p in range(W_PIECES):
            w_piece(wup_hbm, wup_f, ex, slot, p, up_rows).wait()
        for p in range(W_PIECES // 2):
            w_piece(wdn_hbm, wdn_f, ex, slot, p, dn_rows).wait()

    @pl.when(e == 0)
    def _():
        for p in range(W_PIECES):
            w_start(0, 0, p)

    w_wait(e, wslot)
    wup_ref = wup_f.at[wslot]
    wdn_ref = wdn_f.at[wslot]
    more = e + 1 < N_EXPERTS

    def x_copy(i, slot):
        rows = pl.ds(pl.multiple_of((blk0 + i) * tm, tm), tm)
        return pltpu.make_async_copy(x_hbm.at[rows], xbuf.at[slot], in_sem.at[slot])

    def y_copy(i, slot):
        rows = pl.ds(pl.multiple_of((blk0 + i) * tm, tm), tm)
        return pltpu.make_async_copy(obuf.at[slot], y_hbm.at[rows], out_sem.at[slot])

    @pl.when(nb > 0)
    def _():
        x_copy(0, 0).start(priority=1)
        r = lax.broadcasted_iota(jnp.int32, (pair, pair), 0)
        c = lax.broadcasted_iota(jnp.int32, (pair, pair), 1)
        perm = (r == jnp.where(c < LANES, 2 * c, 2 * (c - LANES) + 1)).astype(BF16)
        for g in range(2 * D_MODEL // pair):
            cols = slice(g * pair, (g + 1) * pair)
            wup_s[g] = jnp.dot(wup_ref[:, cols].astype(BF16), perm, preferred_element_type=F32).astype(BF16)
        for g in range(D_MODEL // pair):
            wdn_s[g] = wdn_ref[:, g * pair:(g + 1) * pair].astype(BF16)

        def block(i, carry):
            slot = i % 2
            x_copy(i, slot).wait()

            @pl.when(i + 1 < nb)
            def _():
                x_copy(i + 1, 1 - slot).start(priority=1)

            @pl.when(i >= 2)
            def _():
                y_copy(i - 2, slot).wait()

            @pl.when(jnp.logical_and(more, i < W_PIECES))
            def _():
                w_start(e + 1, 1 - wslot, i)

            row = lax.broadcasted_iota(jnp.int32, (tm, 1), 0)
            x = jnp.where(row < cnt - i * tm, _unpack_bf16_pair(xbuf[slot]), 0.0).astype(BF16)
            acts = []
            for g in range(2 * D_MODEL // pair):
                cols = slice(g * pair, (g + 1) * pair)
                h = jnp.dot(x, wup_s[g], preferred_element_type=F32) + bup_ref[0, :, cols]
                glu = jnp.minimum(h[:, :LANES], SWIGLU_LIMIT)
                lin = jnp.clip(h[:, LANES:], -SWIGLU_LIMIT, SWIGLU_LIMIT)
                acts.append((glu * jax.nn.sigmoid(SWIGLU_ALPHA * glu) * (lin + 1.0)).astype(BF16))
            act = jnp.concatenate(acts, axis=1)
            half_groups = D_MODEL // pair // 2
            for g in range(half_groups):
                ys = []
                for gg in (g, g + half_groups):
                    cols = slice(gg * pair, (gg + 1) * pair)
                    ys.append(jnp.dot(act, wdn_s[gg], preferred_element_type=F32) + bdn_ref[0, :, cols])
                obuf[slot, :, g * pair:(g + 1) * pair] = _pack_bf16_pair(ys[0], ys[1])
            y_copy(i, slot).start(priority=1)
            return carry

        lax.fori_loop(0, nb, block, 0)

        @pl.when(nb >= 2)
        def _():
            y_copy(nb - 2, nb % 2).wait()

        y_copy(nb - 1, (nb - 1) % 2).wait()

    for p in range(W_PIECES):
        @pl.when(jnp.logical_and(more, p >= nb))
        def _():
            w_start(e + 1, 1 - wslot, p)


def _experts(blk0, nblk, cnt, x_sorted, w_up, w_down, b_up_grouped, b_down):
    tm = EXPERT_ROWS
    per_expert = lambda a: pl.BlockSpec((1,) + a.shape[1:], lambda e, b0, nb, ct: (e, 0, 0))
    hbm = pl.BlockSpec(memory_space=pl.ANY)
    grid_spec = pltpu.PrefetchScalarGridSpec(
        num_scalar_prefetch=3,
        grid=(N_EXPERTS,),
        in_specs=[hbm, hbm, per_expert(b_up_grouped), per_expert(b_down), hbm],
        out_specs=hbm,
        scratch_shapes=[pltpu.VMEM((2,) + w_up.shape[1:], F32), pltpu.VMEM((2,) + w_down.shape[1:], F32),
                        pltpu.VMEM((2 * D_MODEL // (2 * LANES), D_MODEL, 2 * LANES), BF16),
                        pltpu.VMEM((D_MODEL // (2 * LANES), D_MODEL, 2 * LANES), BF16),
                        pltpu.VMEM((2, tm, x_sorted.shape[1]), x_sorted.dtype),
                        pltpu.VMEM((2, tm, D_MODEL // 2), jnp.uint32),
                        pltpu.SemaphoreType.DMA((2,)), pltpu.SemaphoreType.DMA((2,)), pltpu.SemaphoreType.DMA((2,))],
    )
    return pl.pallas_call(
        _expert_body,
        grid_spec=grid_spec,
        out_shape=jax.ShapeDtypeStruct((x_sorted.shape[0], D_MODEL // 2), jnp.uint32),
        compiler_params=_params(("arbitrary",)),
        name="experts",
    )(blk0, nblk, cnt, w_up, w_down, b_up_grouped, b_down, x_sorted)


def _combine_body(gate_ref, xmid_ref, gfin_ref, y0_ref, y1_ref, y2_ref, y3_ref, out_ref):
    gate = gate_ref[...]
    moe = _unpack_bf16_pair(y0_ref[...]) * gate[:, 0:1]
    for kx, y_ref in enumerate((y1_ref, y2_ref, y3_ref), start=1):
        moe = moe + _unpack_bf16_pair(y_ref[...]) * gate[:, kx:kx + 1]
    out_ref[...] = _rms(xmid_ref[...] + moe, gfin_ref[...])


def _combine(gates, first_token, xmid, gfin, y_rows):
    n = xmid.shape[0]
    tt = min(n, COMBINE_ROWS)
    nblk = n // tt
    blk0 = first_token // tt
    assert blk0 * tt == first_token
    choice = lambda kx: pl.BlockSpec((tt, y_rows.shape[1]), lambda i: (i + kx * nblk, 0))
    return pl.pallas_call(
        _combine_body,
        grid=(nblk,),
        in_specs=[pl.BlockSpec((tt, LANES), lambda i: (i + blk0, 0)),
                  pl.BlockSpec((tt, D_MODEL), lambda i: (i, 0)),
                  pl.BlockSpec((1, D_MODEL), lambda i: (0, 0))] + [choice(kx) for kx in range(TOP_K)],
        out_specs=pl.BlockSpec((tt, D_MODEL), lambda i: (i, 0)),
        out_shape=jax.ShapeDtypeStruct((n, D_MODEL), F32),
        compiler_params=_params(("arbitrary",)),
        name="combine",
    )(gates, xmid, gfin, y_rows, y_rows, y_rows, y_rows)


def kernel(x_prompt, x_sample, cache_k_win, cache_v_win, norm_attn_g, w_in, ln_v_g, ln_v_b, w_spatial, b_spatial,
           attn_sinks, w_out, norm_ffn_g, w_router, b_router, w_up, b_up, w_down, b_down, norm_final_g):
    bp, tp, _ = x_prompt.shape
    bs, ts, _ = x_sample.shape
    w_buf = cache_k_win.shape[2]
    assert bp == 1 and tp % MIX_ROWS == 0 and w_buf == WINDOW and (bs * ts) % PROJ_ROWS == 0 and 8 % ts == 0
    n_p, n_s = bp * tp, bs * ts
    row2 = lambda a: a.reshape(1, -1)

    w_in_bf = w_in[0].astype(BF16)
    w_out_bf = w_out[0].astype(BF16)
    tril = jnp.tril(jnp.ones((CHUNK, CHUNK), dtype=bool))
    wsp = jnp.where(tril[None], w_spatial[0], 0.0)
    wsp_bf = wsp.astype(BF16)
    bsp = jnp.broadcast_to(b_spatial[0][:, :, None], (GMLP_GROUPS, CHUNK, LANES))
    b_up_grouped = b_up[0].reshape(N_EXPERTS, -1, LANES, 2).transpose(0, 1, 3, 2).reshape(N_EXPERTS, 1, -1)
    bd = b_down[0][:, None, :]
    g1, g2, gfin = row2(norm_attn_g[0]), row2(norm_ffn_g[0]), row2(norm_final_g)
    lng, lnb = row2(ln_v_g[0]), row2(ln_v_b[0])
    wr_hi = w_router[0].astype(BF16)
    wr = jnp.concatenate([wr_hi, (w_router[0] - wr_hi.astype(F32)).astype(BF16)], axis=1)
    br = row2(b_router[0])
    sinks = attn_sinks[0]

    xp = x_prompt.reshape(n_p, D_MODEL)
    cs_p = _rotary_inputs(jnp.arange(tp, dtype=jnp.int32))
    q_p, k_p, v_p, a_p, vn_p, sgb_p = _proj(xp, g1, w_in_bf, cs_p, lng, lnb)
    xmid_p, xn2_p, lg_p = _mix(sinks, q_p, k_p, v_p, a_p, vn_p, sgb_p, xp, wsp_bf, bsp, w_out_bf, g2, wr, br)

    xs = x_sample.reshape(n_s, D_MODEL)
    pos_s = PAST_LEN + jnp.arange(ts, dtype=jnp.int32)
    cs_s = _rotary_inputs(jnp.tile(pos_s, bs))
    q_s, k_s, v_s, a_s, vn_s, sgb_s = _proj(xs, g1, w_in_bf, cs_s, lng, lnb)
    n_keys = w_buf + ts
    key_pad = jnp.zeros((bs, (-n_keys) % 8, KV_WIDTH), F32)
    with_new = lambda cache, new: jnp.concatenate(
        [cache[0].reshape(bs, w_buf, KV_WIDTH), new.reshape(bs, ts, KV_WIDTH), key_pad], axis=1)
    k_all = with_new(cache_k_win, k_s)
    v_all = with_new(cache_v_win, v_s)
    sink_col = jnp.tile(sinks, ts).reshape(ts * N_HEADS, 1)
    o_s = _sample_attn(sink_col, q_s.reshape(bs, ts * N_HEADS, HEAD_DIM), k_all, v_all).reshape(n_s, Q_WIDTH)
    lag = np.arange(ts)[:, None] - np.arange(ts)[None, :]
    coef = jnp.stack([jnp.sum(jnp.where(lag == d, wsp[:, :ts, :ts], 0.0), axis=2)
                      for d in range(ts)])
    coef = jnp.repeat(coef.transpose(0, 2, 1), GMLP_WIDTH // GMLP_GROUPS, axis=2)
    coef = jnp.tile(coef, (1, 8 // ts, 1))
    bias = jnp.tile(jnp.repeat(b_spatial[0][:, :ts].T, GMLP_WIDTH // GMLP_GROUPS, axis=1), (8 // ts, 1))
    xmid_s, xn2_s, lg_s = _mix_sample(a_s, vn_s, sgb_s, o_s, xs, coef, bias, w_out_bf, g2, wr, br)

    tm = EXPERT_ROWS
    n_blocks = (n_p * TOP_K + N_EXPERTS * n_s) // tm + N_EXPERTS
    gate_p, dest_p, table_p = _route(lg_p, jnp.zeros((N_EXPERTS, LANES), F32), n_s)
    x_sorted = _sc_dispatch(xn2_p, dest_p, n_blocks * tm)
    gate_s, dest_s, table = _route(lg_s, table_p, None)
    x_sorted = _dispatch_small(xn2_s, dest_s, x_sorted)
    dest_t = jnp.concatenate([dest_p, dest_s], axis=1)
    meta = table.astype(jnp.int32)
    y_sorted = _experts(meta[:, 2], meta[:, 3], meta[:, 1], x_sorted, w_up[0], w_down[0], b_up_grouped, bd)
    yrows_p, yrows_s = _sc_collect(y_sorted, dest_t, n_p, n_s)
    y_p = _combine(gate_p, 0, xmid_p, gfin, yrows_p)
    y_s = _combine(gate_s, 0, xmid_s, gfin, yrows_s)

    k4 = lambda t: t.reshape(1, bp, -1, N_KV_HEADS, HEAD_DIM)
    keys_minor = lambda t: t.reshape(bs, -1, KV_WIDTH).transpose(0, 2, 1)
    next_window = lambda cache, new: _window(keys_minor(cache[0]), keys_minor(new)).transpose(0, 2, 1).reshape(
        1, bs, w_buf, N_KV_HEADS, HEAD_DIM)
    return (y_p.reshape(bp, tp, D_MODEL),
            y_s.reshape(bs, ts, D_MODEL),
            k4(k_p[n_p - WINDOW:]),
            k4(v_p[n_p - WINDOW:]),
            vn_p[n_p - CHUNK:].reshape(1, bp, CHUNK, GMLP_WIDTH),
            next_window(cache_k_win, k_s),
            next_window(cache_v_win, v_s),
            vn_s.reshape(1, bs, ts, GMLP_WIDTH))
```

```python
import functools

import numpy as np
import jax
import jax.numpy as jnp
from jax import lax
from jax.experimental import pallas as pl
from jax.experimental.pallas import tpu as pltpu
from jax.experimental.pallas import tpu_sc as plsc

F32 = jnp.float32
BF16 = jnp.bfloat16

D_MODEL = 1024
HEAD_DIM = 64
N_HEADS = 16
GQA_GROUP = 8
N_KV_HEADS = 2
Q_WIDTH = 1024
KV_WIDTH = 128
WINDOW = 128
ROT_DIM = 16
ROPE_THETA = 500000.0
CHUNK = 128
GMLP_WIDTH = 1024
GMLP_GROUPS = 8
N_EXPERTS = 32
TOP_K = 4
SWIGLU_LIMIT = 7.0
SWIGLU_ALPHA = 1.702
RMS_EPS = 1e-5
LN_EPS = 1e-5
NEG_INF = -1e30
PAST_LEN = 16384

LANES = 128
VMEM_LIMIT = 56 * 1024 * 1024

PROJ_ROWS = 256
MIX_ROWS = 512
ROUTE_ROWS = 1024
EXPERT_ROWS = 256
W_PIECES = 8
COMBINE_ROWS = 1024

SC_CORES = 2
SC_WORKERS = 32
SC_ROWS = 64

_C_Q, _C_KV, _C_U, _C_VG, _C_GA, _C_GB, _C_END = 0, 1024, 1280, 2304, 3328, 4352, 5376


def _params(sem):
    return pltpu.CompilerParams(dimension_semantics=sem, vmem_limit_bytes=VMEM_LIMIT)


def _rms(x, g):
    return x * lax.rsqrt(jnp.mean(x * x, axis=-1, keepdims=True) + RMS_EPS) * g


def _pack_bf16_pair(lo, hi):
    lo_bits = lax.bitcast_convert_type(lo.astype(BF16).astype(F32), jnp.uint32)
    hi_bits = lax.bitcast_convert_type(hi.astype(BF16).astype(F32), jnp.uint32)
    return (lo_bits >> 16) | hi_bits


def _unpack_bf16_pair(words):
    lo = lax.bitcast_convert_type(words << 16, F32)
    hi = lax.bitcast_convert_type(words & jnp.uint32(0xFFFF0000), F32)
    return jnp.concatenate([lo, hi], axis=1)


def _proj_body(x_ref, g_ref, w_ref, cs_ref, rot_ref, lng_ref, lnb_ref,
               q_ref, k_ref, v_ref, a_ref, vn_ref, sgb_ref):
    h = _rms(x_ref[...], g_ref[...]).astype(BF16)
    tabs = lax.dot_general(cs_ref[...], rot_ref[...], (((0,), (0,)), ((), ())), preferred_element_type=F32)
    rc, rs1, rs2 = tabs[:, :LANES], tabs[:, LANES:2 * LANES], tabs[:, 2 * LANES:]

    def rot(z):
        return z * rc + pltpu.roll(z, LANES - ROT_DIM // 2, 1) * rs1 + pltpu.roll(z, ROT_DIM // 2, 1) * rs2

    def mm(lo, hi):
        return jnp.dot(h, w_ref[:, lo:hi], preferred_element_type=F32)

    zq = mm(_C_Q, _C_KV)
    for c in range(Q_WIDTH // LANES):
        sl = slice(c * LANES, (c + 1) * LANES)
        q_ref[:, sl] = (rot(zq[:, sl]) * (HEAD_DIM ** -0.5)).astype(BF16)
    zkv = mm(_C_KV, _C_U)
    k_ref[...] = rot(zkv[:, :KV_WIDTH])
    v_ref[...] = zkv[:, KV_WIDTH:]
    a_ref[...] = jax.nn.sigmoid(mm(_C_GA, _C_GB)) * jax.nn.gelu(mm(_C_U, _C_VG))
    zv = jax.nn.gelu(mm(_C_VG, _C_GA))
    zc = zv - jnp.mean(zv, axis=-1, keepdims=True)
    var = jnp.mean(zc * zc, axis=-1, keepdims=True)
    vn_ref[...] = zc * lax.rsqrt(var + LN_EPS) * lng_ref[...] + lnb_ref[...]
    sgb_ref[...] = jax.nn.sigmoid(mm(_C_GB, _C_END))


def _proj(x, norm_g, w_in_bf, cs, ln_g, ln_b):
    n = x.shape[0]
    tm = PROJ_ROWS
    row = lambda w: pl.BlockSpec((tm, w), lambda i: (i, 0))
    full = lambda a: pl.BlockSpec(a.shape, lambda i: (0,) * a.ndim)
    rot = jnp.asarray(np.tile(_ROT_EXPAND, (3, 1)), dtype=BF16)
    return pl.pallas_call(
        _proj_body,
        grid=(n // tm,),
        in_specs=[row(D_MODEL), full(norm_g), full(w_in_bf), pl.BlockSpec((cs.shape[0], tm), lambda i: (0, i)),
                  full(rot),
                  full(ln_g), full(ln_b)],
        out_specs=[row(Q_WIDTH), row(KV_WIDTH), row(KV_WIDTH), row(GMLP_WIDTH), row(GMLP_WIDTH), row(D_MODEL)],
        out_shape=[jax.ShapeDtypeStruct((n, Q_WIDTH), BF16),
                   jax.ShapeDtypeStruct((n, KV_WIDTH), F32),
                   jax.ShapeDtypeStruct((n, KV_WIDTH), F32),
                   jax.ShapeDtypeStruct((n, GMLP_WIDTH), F32),
                   jax.ShapeDtypeStruct((n, GMLP_WIDTH), F32),
                   jax.ShapeDtypeStruct((n, D_MODEL), F32)],
        compiler_params=_params(("arbitrary",)),
        name="proj",
    )(x, norm_g, w_in_bf, cs, rot, ln_g, ln_b)


_ROT_COLS = 32


def _rot_expand():
    half = ROT_DIM // 2
    m = np.zeros((_ROT_COLS, 3 * LANES), np.float32)
    for lane in range(LANES):
        d = lane % HEAD_DIM
        if d < ROT_DIM:
            m[d % half, lane] = 1.0
        else:
            m[2 * half, lane] = 1.0
        if d < half:
            m[half + d, LANES + lane] = -1.0
        elif d < ROT_DIM:
            m[half + d - half, 2 * LANES + lane] = 1.0
    return m


_ROT_EXPAND = _rot_expand()


def _rotary_inputs(pos):
    half = ROT_DIM // 2
    inv_freq = ROPE_THETA ** (-jnp.arange(half, dtype=F32) / half)
    ang = inv_freq[:, None] * pos.astype(F32)[None, :]
    n = pos.shape[0]
    cs = jnp.concatenate([jnp.cos(ang), jnp.sin(ang), jnp.ones((1, n), F32),
                          jnp.zeros((_ROT_COLS - 2 * half - 1, n), F32)], axis=0)
    hi = cs.astype(BF16)
    rest = cs - hi.astype(F32)
    mid = rest.astype(BF16)
    lo = (rest - mid.astype(F32)).astype(BF16)
    return jnp.concatenate([hi, mid, lo], axis=0)


def _finish_rows(merged_bf, x, wout_ref, g2_ref, wr_ref, br_ref, xmid_ref, xn2_ref, lg_ref):
    xm = x + jnp.dot(merged_bf, wout_ref[...], preferred_element_type=F32)
    xmid_ref[...] = xm
    xn = _rms(xm, g2_ref[...])
    x_hi = xn.astype(BF16)
    x_lo = (xn - x_hi.astype(F32)).astype(BF16)
    w_hl = wr_ref[...]
    p_hi = jnp.dot(x_hi, w_hl, preferred_element_type=F32)
    p_lo = jnp.dot(x_lo, w_hl[:, :N_EXPERTS], preferred_element_type=F32)
    lg = p_hi[:, :N_EXPERTS] + (p_hi[:, N_EXPERTS:] + p_lo) + br_ref[...]
    wide = jnp.concatenate([lg, jnp.zeros((lg.shape[0], LANES - N_EXPERTS), F32)], axis=1)
    lg_ref[...] = wide.T[:N_EXPERTS]
    xn2_ref[...] = _pack_bf16_pair(xn[:, :D_MODEL // 2], xn[:, D_MODEL // 2:])


def _mix_body(sinks_ref, q_ref, k_ref, kp_ref, v_ref, vp_ref, a_ref, vn_ref, sgb_ref, x_ref,
              wsp_ref, bsp_ref, wout_ref, g2_ref, wr_ref, br_ref,
              xmid_ref, xn2_ref, lg_ref, kcat, vcat, mrg, key_s):
    i = pl.program_id(0)
    nsub = MIX_ROWS // WINDOW
    kcat[0:WINDOW] = kp_ref[...]
    kcat[WINDOW:] = k_ref[...]
    vcat[0:WINDOW] = vp_ref[...]
    vcat[WINDOW:] = v_ref[...]

    pair_rows = (GQA_GROUP // 2) * WINDOW
    lane_kv = lax.broadcasted_iota(jnp.int32, (2 * WINDOW, LANES), 1)
    lane_o = lax.broadcasted_iota(jnp.int32, (pair_rows, LANES), 1)

    @pl.when(i == 0)
    def _():
        rq = lax.broadcasted_iota(jnp.int32, (pair_rows, 4 * WINDOW), 0) & (WINDOW - 1)
        ck = lax.broadcasted_iota(jnp.int32, (pair_rows, 4 * WINDOW), 1) & (2 * WINDOW - 1)
        key_s[...] = jnp.where((ck > rq) & (ck <= rq + WINDOW), ck, -1)

    row_p = lax.broadcasted_iota(jnp.int32, (pair_rows, 1), 0) >> 7
    sink_cols = []
    for kk in range(N_KV_HEADS):
        h0 = kk * GQA_GROUP
        se = jnp.full((pair_rows, 1), sinks_ref[h0], F32)
        so = jnp.full((pair_rows, 1), sinks_ref[h0 + 1], F32)
        for p in range(1, GQA_GROUP // 2):
            se = jnp.where(row_p == p, sinks_ref[h0 + 2 * p], se)
            so = jnp.where(row_p == p, sinks_ref[h0 + 2 * p + 1], so)
        sink_cols.append((se, so))

    def sub(j, carry):
        off = pl.multiple_of(j * WINDOW, WINDOW)
        rows = pl.ds(off, WINDOW)
        for g in range(GMLP_GROUPS):
            cols = slice(g * LANES, (g + 1) * LANES)
            s = jnp.dot(wsp_ref[g], vn_ref[rows, cols].astype(BF16), preferred_element_type=F32) + bsp_ref[g]
            mrg[rows, cols] = a_ref[rows, cols] * s
        kblk = kcat[pl.ds(off, 2 * WINDOW), :]
        vblk = vcat[pl.ds(off, 2 * WINDOW), :]
        kswp = pltpu.roll(kblk, HEAD_DIM, 1)
        vswp = pltpu.roll(vblk, HEAD_DIM, 1)
        kmin = jnp.where(jnp.logical_and(i == 0, j == 0), WINDOW, 0)
        allowed = key_s[...] >= kmin
        for kk in range(N_KV_HEADS):
            lo_src, hi_src = (kblk, kswp) if kk == 0 else (kswp, kblk)
            kbd = jnp.concatenate([jnp.where(lane_kv < HEAD_DIM, lo_src, 0.0),
                                   jnp.where(lane_kv >= HEAD_DIM, hi_src, 0.0)], axis=0).astype(BF16)
            lo_src, hi_src = (vblk, vswp) if kk == 0 else (vswp, vblk)
            vbd = jnp.concatenate([jnp.where(lane_kv < HEAD_DIM, lo_src, 0.0),
                                   jnp.where(lane_kv >= HEAD_DIM, hi_src, 0.0)], axis=0).astype(BF16)
            pair0 = kk * (GQA_GROUP // 2)
            qs = jnp.concatenate([q_ref[rows, (pair0 + p) * LANES:(pair0 + p + 1) * LANES]
                                  for p in range(GQA_GROUP // 2)], axis=0)
            lg = lax.dot_general(qs, kbd, (((1,), (1,)), ((), ())), preferred_element_type=F32)
            lg = jnp.where(allowed, lg, NEG_INF)
            se, so = sink_cols[kk]
            le, lo = lg[:, :2 * WINDOW], lg[:, 2 * WINDOW:]
            me = jnp.maximum(jnp.max(le, axis=1, keepdims=True), se)
            mo = jnp.maximum(jnp.max(lo, axis=1, keepdims=True), so)
            pe = jnp.exp(le - me)
            po = jnp.exp(lo - mo)
            de = jnp.sum(pe, axis=1, keepdims=True) + jnp.exp(se - me)
            do = jnp.sum(po, axis=1, keepdims=True) + jnp.exp(so - mo)
            pr = jnp.concatenate([pe, po], axis=1).astype(BF16)
            o = jnp.dot(pr, vbd, preferred_element_type=F32)
            o = o / jnp.where(lane_o < HEAD_DIM, de, do)
            for p in range(GQA_GROUP // 2):
                cols = slice((pair0 + p) * LANES, (pair0 + p + 1) * LANES)
                mrg[rows, cols] += sgb_ref[rows, cols] * o[p * WINDOW:(p + 1) * WINDOW]
        return carry

    lax.fori_loop(0, nsub, sub, 0)
    _finish_rows(mrg[...].astype(BF16), x_ref[...], wout_ref, g2_ref, wr_ref, br_ref, xmid_ref, xn2_ref, lg_ref)


def _mix(sinks, q, k, v, a, vn, sgb, x, wsp, bsp, wout, g2, wr, br):
    n = x.shape[0]
    tm = MIX_ROWS
    nsub = tm // WINDOW
    row = lambda w: pl.BlockSpec((tm, w), lambda i: (i, 0))
    prev = pl.BlockSpec((WINDOW, KV_WIDTH), lambda i: (jnp.maximum(i * nsub - 1, 0), 0))
    full = lambda arr: pl.BlockSpec(arr.shape, lambda i: (0,) * arr.ndim)
    smem = pl.BlockSpec(memory_space=pltpu.SMEM)
    return pl.pallas_call(
        _mix_body,
        grid=(n // tm,),
        in_specs=[smem, row(Q_WIDTH), row(KV_WIDTH), prev, row(KV_WIDTH), prev,
                  row(GMLP_WIDTH), row(GMLP_WIDTH), row(D_MODEL), row(D_MODEL),
                  full(wsp), full(bsp), full(wout), full(g2), full(wr), full(br)],
        out_specs=[row(D_MODEL), row(D_MODEL // 2), pl.BlockSpec((N_EXPERTS, tm), lambda i: (0, i))],
        out_shape=[jax.ShapeDtypeStruct((n, D_MODEL), F32),
                   jax.ShapeDtypeStruct((n, D_MODEL // 2), jnp.uint32),
                   jax.ShapeDtypeStruct((N_EXPERTS, n), F32)],
        scratch_shapes=[pltpu.VMEM((tm + WINDOW, KV_WIDTH), F32),
                        pltpu.VMEM((tm + WINDOW, KV_WIDTH), F32),
                        pltpu.VMEM((tm, D_MODEL), F32),
                        pltpu.VMEM((GQA_GROUP // 2 * WINDOW, 4 * WINDOW), jnp.int32)],
        compiler_params=_params(("arbitrary",)),
        name="mix_prompt",
    )(sinks, q, k, k, v, v, a, vn, sgb, x, wsp, bsp, wout, g2, wr, br)


def _sample_attn_body(sink_ref, q_ref, kc_ref, vc_ref, kn_ref, vn_ref, o_ref):
    q = q_ref[...]
    nq, nc, nn = q.shape[1], kc_ref.shape[2], kn_ref.shape[1]
    row_c = lax.broadcasted_iota(jnp.int32, (1, nq, nc), 1)
    row_n = lax.broadcasted_iota(jnp.int32, (1, nq, nn), 1)
    row_o = lax.broadcasted_iota(jnp.int32, (1, nq, HEAD_DIM), 1)
    first_kv = lambda row: (row % N_HEADS) < GQA_GROUP
    cached = lambda ref, kk: ref[:, kk * HEAD_DIM:(kk + 1) * HEAD_DIM, :].astype(BF16)
    fresh = lambda ref, kk: ref[:, :, kk * HEAD_DIM:(kk + 1) * HEAD_DIM].astype(BF16)
    lg_c = jnp.where(first_kv(row_c),
                     jnp.einsum("bqd,bdk->bqk", q, cached(kc_ref, 0), preferred_element_type=F32),
                     jnp.einsum("bqd,bdk->bqk", q, cached(kc_ref, 1), preferred_element_type=F32))
    lg_n = jnp.where(first_kv(row_n),
                     jnp.einsum("bqd,bkd->bqk", q, fresh(kn_ref, 0), preferred_element_type=F32),
                     jnp.einsum("bqd,bkd->bqk", q, fresh(kn_ref, 1), preferred_element_type=F32))
    j_c = lax.broadcasted_iota(jnp.int32, (1, nq, nc), 2)
    j_n = lax.broadcasted_iota(jnp.int32, (1, nq, nn), 2)
    lg_c = jnp.where(j_c > row_c // N_HEADS + (nc - WINDOW), lg_c, NEG_INF)
    lg_n = jnp.where(j_n <= row_n // N_HEADS, lg_n, NEG_INF)
    sink = sink_ref[...][None]
    m = jnp.maximum(jnp.maximum(jnp.max(lg_c, axis=2, keepdims=True), jnp.max(lg_n, axis=2, keepdims=True)), sink)
    p_c = jnp.exp(lg_c - m)
    p_n = jnp.exp(lg_n - m)
    den = jnp.sum(p_c, axis=2, keepdims=True) + jnp.sum(p_n, axis=2, keepdims=True) + jnp.exp(sink - m)
    pb_c, pb_n = p_c.astype(BF16), p_n.astype(BF16)
    heads_out = lambda kk: (jnp.einsum("bqk,bdk->bqd", pb_c, cached(vc_ref, kk), preferred_element_type=F32)
                            + jnp.einsum("bqk,bkd->bqd", pb_n, fresh(vn_ref, kk), preferred_element_type=F32))
    o_ref[...] = jnp.where(first_kv(row_o), heads_out(0), heads_out(1)) / den


def _sample_attn(sink_col, q3, k_cache_t, v_cache_t, k_new, v_new):
    nb = q3.shape[0]
    bb = 32
    blk = lambda a: pl.BlockSpec((bb,) + a.shape[1:], lambda b: (b, 0, 0))
    args = (q3, k_cache_t, v_cache_t, k_new, v_new)
    return pl.pallas_call(
        _sample_attn_body,
        grid=(nb // bb,),
        in_specs=[pl.BlockSpec(sink_col.shape, lambda b: (0, 0))] + [blk(a) for a in args],
        out_specs=blk(q3),
        out_shape=jax.ShapeDtypeStruct(q3.shape, F32),
        compiler_params=_params(("arbitrary",)),
        name="attn_sample",
    )(sink_col, *args)


def _window_body(c_ref, n_ref, o_ref):
    old = pltpu.roll(c_ref[...], c_ref.shape[2] - n_ref.shape[2], 2)
    lane = lax.broadcasted_iota(jnp.int32, old.shape, 2)
    new = n_ref[...]
    first_new = old.shape[2] - new.shape[2]
    out = old
    for t in range(new.shape[2]):
        out = jnp.where(lane == first_new + t, new[:, :, t:t + 1], out)
    o_ref[...] = out


def _window(cache_t, new_t):
    nb = cache_t.shape[0]
    bb = 16
    blk = lambda a: pl.BlockSpec((bb,) + a.shape[1:], lambda b: (b, 0, 0))
    return pl.pallas_call(
        _window_body,
        grid=(nb // bb,),
        in_specs=[blk(cache_t), blk(new_t)],
        out_specs=blk(cache_t),
        out_shape=jax.ShapeDtypeStruct(cache_t.shape, cache_t.dtype),
        compiler_params=_params(("arbitrary",)),
        name="window",
    )(cache_t, new_t)


def _mix_sample_body(a_ref, vn_ref, sgb_ref, o_ref, x_ref, coef_ref, bias_ref,
                     wout_ref, g2_ref, wr_ref, br_ref, xmid_ref, xn2_ref, lg_ref):
    vn = vn_ref[...]
    n, width = vn.shape
    rows8 = lambda t: t.reshape(n // 8, 8, width)
    s = bias_ref[...][None] + coef_ref[0][None] * rows8(vn)
    for d in range(1, coef_ref.shape[0]):
        s = s + coef_ref[d][None] * rows8(pltpu.roll(vn, d, 0))
    merged = a_ref[...] * s.reshape(n, width) + sgb_ref[...] * o_ref[...]
    _finish_rows(merged.astype(BF16), x_ref[...], wout_ref, g2_ref, wr_ref, br_ref, xmid_ref, xn2_ref, lg_ref)


def _mix_sample(a, vn, sgb, o, x, coef, bias, wout, g2, wr, br):
    n = x.shape[0]
    args = (a, vn, sgb, o, x, coef, bias, wout, g2, wr, br)
    full = lambda arr: pl.BlockSpec(arr.shape, lambda i: (0,) * arr.ndim)
    return pl.pallas_call(
        _mix_sample_body,
        grid=(1,),
        in_specs=[full(arr) for arr in args],
        out_specs=[pl.BlockSpec((n, D_MODEL), lambda i: (0, 0)), pl.BlockSpec((n, D_MODEL // 2), lambda i: (0, 0)),
                   pl.BlockSpec((N_EXPERTS, n), lambda i: (0, 0))],
        out_shape=[jax.ShapeDtypeStruct((n, D_MODEL), F32),
                   jax.ShapeDtypeStruct((n, D_MODEL // 2), jnp.uint32),
                   jax.ShapeDtypeStruct((N_EXPERTS, n), F32)],
        compiler_params=_params(("arbitrary",)),
        name="mix_sample",
    )(*args)


def _rows8(rows, dtype):
    n = rows[0].shape[1]
    sub = lax.broadcasted_iota(jnp.int32, (8, n), 0)
    out = jnp.zeros((8, n), dtype)
    for kx, r in enumerate(rows):
        out = jnp.where(sub == kx, r.astype(dtype), out)
    return out


def _route_body(nblk, reserve, lg_ref, prior_ref, gate_ref, dest_ref, meta_ref, idx_s, rank_s, base):
    i = pl.program_id(0)

    @pl.when(i == 0)
    def _():
        base[...] = jnp.zeros_like(base)

    l = lg_ref[...]
    tb = l.shape[1]
    sub = lax.broadcasted_iota(jnp.int32, l.shape, 0).astype(F32)
    vals, idxs, sels = [], [], []
    for _ in range(TOP_K):
        m = jnp.max(l, axis=0, keepdims=True)
        ik = jnp.min(jnp.where(l == m, sub, float(N_EXPERTS)), axis=0, keepdims=True)
        sel = sub == ik
        l = jnp.where(sel, -jnp.inf, l)
        vals.append(m)
        idxs.append(ik)
        sels.append(sel)
    es = [jnp.exp(vk - vals[0]) for vk in vals]
    den = es[0] + es[1] + es[2] + es[3]
    onehot = jnp.zeros(l.shape, F32)
    for sel in sels:
        onehot = onehot + sel.astype(F32)
    earlier = (lax.broadcasted_iota(jnp.int32, (tb, tb), 0) < lax.broadcasted_iota(jnp.int32, (tb, tb), 1))
    before = jnp.dot(onehot.astype(BF16), earlier.astype(BF16), preferred_element_type=F32) + base[...]
    ranks = [jnp.sum(jnp.where(sel, before, 0.0), axis=0, keepdims=True) for sel in sels]
    base[...] += jnp.sum(onehot, axis=1, keepdims=True)
    idx_s[i] = _rows8(idxs, F32)
    rank_s[i] = _rows8(ranks, F32)
    gates = jnp.concatenate([_rows8([e / den for e in es], F32), jnp.zeros((LANES - 8, tb), F32)], axis=0)
    gate_ref[...] = gates.T

    @pl.when(i == nblk - 1)
    def _():
        cnt = base[...]
        if reserve is None:
            seg0 = prior_ref[:, 0:1]
            placed = prior_ref[:, 1:2]
        else:
            cap = jnp.ceil((cnt + reserve) / EXPERT_ROWS) * EXPERT_ROWS
            lower = (lax.broadcasted_iota(jnp.int32, (N_EXPERTS, N_EXPERTS), 1) <
                     lax.broadcasted_iota(jnp.int32, (N_EXPERTS, N_EXPERTS), 0)).astype(F32)
            seg0 = jnp.dot(lower, jnp.broadcast_to(cap, (N_EXPERTS, LANES)), preferred_element_type=F32,
                           precision=lax.Precision.HIGHEST)[:, :1]
            placed = jnp.zeros_like(cnt)
        total = placed + cnt
        lane = lax.broadcasted_iota(jnp.int32, (N_EXPERTS, LANES), 1)
        meta_ref[...] = jnp.where(lane == 0, seg0, jnp.where(lane == 1, total, jnp.where(
            lane == 2, seg0 / EXPERT_ROWS, jnp.where(lane == 3, jnp.ceil(total / EXPERT_ROWS), 0.0))))
        first = seg0 + placed
        sub_e = lax.broadcasted_iota(jnp.int32, (N_EXPERTS, tb), 0).astype(F32)
        for b in range(nblk):
            idx, rank = idx_s[b], rank_s[b]
            rows = [jnp.sum(jnp.where(sub_e == idx[kx:kx + 1], first, 0.0), axis=0, keepdims=True)
                    + rank[kx:kx + 1] for kx in range(TOP_K)]
            dest_ref[:, b * tb:(b + 1) * tb] = _rows8(rows, jnp.int32)


def _route(logits, prior, reserve):
    n = logits.shape[1]
    tb = min(n, ROUTE_ROWS)
    nblk = n // tb
    assert nblk * tb == n
    table = pl.BlockSpec((N_EXPERTS, LANES), lambda i: (0, 0))
    return pl.pallas_call(
        functools.partial(_route_body, nblk, reserve),
        grid=(nblk,),
        in_specs=[pl.BlockSpec((N_EXPERTS, tb), lambda i: (0, i)), table],
        out_specs=[pl.BlockSpec((tb, LANES), lambda i: (i, 0)), pl.BlockSpec((8, n), lambda i: (0, 0)), table],
        out_shape=[jax.ShapeDtypeStruct((n, LANES), F32),
                   jax.ShapeDtypeStruct((8, n), jnp.int32),
                   jax.ShapeDtypeStruct((N_EXPERTS, LANES), F32)],
        scratch_shapes=[pltpu.VMEM((nblk, 8, tb), F32), pltpu.VMEM((nblk, 8, tb), F32),
                        pltpu.VMEM((N_EXPERTS, 1), F32)],
        compiler_params=_params(("arbitrary",)),
        name="route",
    )(logits, prior)


def _sc_mesh():
    return plsc.VectorSubcoreMesh(core_axis_name="c", subcore_axis_name="s")


def _sc_worker():
    return lax.axis_index("s") * SC_CORES + lax.axis_index("c")


def _sc_dispatch(x, dest_t, n_slots):
    chunk = SC_ROWS
    n = x.shape[0]
    per_w = n // (SC_WORKERS * chunk)
    assert per_w * SC_WORKERS * chunk == n and per_w % 2 == 0
    d3 = dest_t.reshape(dest_t.shape[0], n // chunk, chunk)
    width, dtype = x.shape[1], x.dtype

    @functools.partial(
        pl.kernel, mesh=_sc_mesh(),
        out_type=jax.ShapeDtypeStruct((n_slots, width), dtype),
        scratch_types=[pltpu.VMEM((TOP_K, per_w, chunk), jnp.int32),
                       pltpu.VMEM((2, chunk, width), dtype),
                       pltpu.SemaphoreType.DMA, pltpu.SemaphoreType.DMA],
        compiler_params=pltpu.CompilerParams(use_tc_tiling_on_sc=True),
        name="dispatch")
    def run(x_hbm, d_hbm, out_hbm, idx_v, rows_v, rsem, wsem):
        wid = _sc_worker()
        pltpu.sync_copy(d_hbm.at[pl.ds(0, TOP_K), pl.ds(wid * per_w, per_w)], idx_v)

        def read(j, slot):
            return pltpu.make_async_copy(x_hbm.at[pl.ds((wid * per_w + j) * chunk, chunk)], rows_v.at[slot], rsem)

        def scatter(j, slot):
            copies = [pltpu.async_copy(rows_v.at[slot], out_hbm.at[idx_v.at[kx, j]], wsem) for kx in range(TOP_K)]
            for cp in copies:
                cp.wait()

        read(0, 0).start()

        def body(h, carry):
            j = 2 * h
            read(j, 0).wait()
            read(j + 1, 1).start()
            scatter(j, 0)
            read(j + 1, 1).wait()

            @pl.when(j + 2 < per_w)
            def _():
                read(j + 2, 0).start()

            scatter(j + 1, 1)
            return carry

        lax.fori_loop(0, per_w // 2, body, 0)

    return run(x, d3)


def _dispatch_small_body(dest_ref, x_ref, _, out_hbm, sem):
    n = x_ref.shape[0]

    def body(t, carry):
        for kx in range(TOP_K):
            pltpu.make_async_copy(x_ref.at[pl.ds(t, 1)], out_hbm.at[pl.ds(dest_ref[kx, t], 1)], sem).start()
        return carry

    lax.fori_loop(0, n, body, 0, unroll=4)
    for _ in range(TOP_K):
        pltpu.make_async_copy(x_ref, out_hbm.at[pl.ds(0, n)], sem).wait()


def _dispatch_small(x, dest_t, x_sorted):
    return pl.pallas_call(
        _dispatch_small_body,
        grid=(1,),
        in_specs=[pl.BlockSpec(memory_space=pltpu.SMEM), pl.BlockSpec(x.shape, lambda i: (0, 0)),
                  pl.BlockSpec(memory_space=pl.ANY)],
        out_specs=pl.BlockSpec(memory_space=pl.ANY),
        out_shape=jax.ShapeDtypeStruct(x_sorted.shape, x_sorted.dtype),
        scratch_shapes=[pltpu.SemaphoreType.DMA],
        input_output_aliases={2: 0},
        compiler_params=_params(("arbitrary",)),
        name="dispatch_small",
    )(dest_t, x, x_sorted)


def _sc_collect(y_sorted, dest_t, n_p, n_s):
    chunk = SC_ROWS
    per_choice = SC_WORKERS // TOP_K
    per_w = n_p // (per_choice * chunk)
    assert per_w * per_choice * chunk == n_p and per_w % 2 == 0
    assert n_s == per_choice * chunk
    d3 = dest_t.reshape(dest_t.shape[0], (n_p + n_s) // chunk, chunk)
    width, dtype = y_sorted.shape[1], y_sorted.dtype

    @functools.partial(
        pl.kernel, mesh=_sc_mesh(),
        out_type=[jax.ShapeDtypeStruct((TOP_K * n_p, width), dtype), jax.ShapeDtypeStruct((TOP_K * n_s, width), dtype)],
        scratch_types=[pltpu.VMEM((per_w, chunk), jnp.int32),
                       pltpu.VMEM((1, chunk), jnp.int32),
                       pltpu.VMEM((2, chunk, width), dtype),
                       pltpu.SemaphoreType.DMA, pltpu.SemaphoreType.DMA],
        compiler_params=pltpu.CompilerParams(use_tc_tiling_on_sc=True),
        name="collect")
    def run(y_hbm, d_hbm, op_hbm, os_hbm, ip_v, is_v, rows_v, gsem, wsem):
        wid = _sc_worker()
        choice = wid // per_choice
        part = wid % per_choice
        pltpu.sync_copy(d_hbm.at[choice, pl.ds(part * per_w, per_w)], ip_v)
        pltpu.sync_copy(d_hbm.at[choice, pl.ds(n_p // chunk + part, 1)], is_v)

        def gather(idx_v, j, slot):
            return pltpu.make_async_copy(y_hbm.at[idx_v.at[j]], rows_v.at[slot], gsem)

        def write(j, slot):
            return pltpu.make_async_copy(rows_v.at[slot], op_hbm.at[pl.ds((wid * per_w + j) * chunk, chunk)], wsem)

        gather(ip_v, 0, 0).start()

        def body(h, carry):
            j = 2 * h
            gather(ip_v, j, 0).wait()

            @pl.when(h > 0)
            def _():
                write(j - 1, 1).wait()

            gather(ip_v, j + 1, 1).start()
            write(j, 0).start()
            gather(ip_v, j + 1, 1).wait()
            write(j, 0).wait()

            @pl.when(j + 2 < per_w)
            def _():
                gather(ip_v, j + 2, 0).start()

            write(j + 1, 1).start()
            return carry

        lax.fori_loop(0, per_w // 2, body, 0)
        write(per_w - 1, 1).wait()

        gather(is_v, 0, 0).start()
        gather(is_v, 0, 0).wait()
        pltpu.sync_copy(rows_v.at[0], os_hbm.at[pl.ds(wid * chunk, chunk)])

    return run(y_sorted, d3)


def _expert_body(blk0_ref, nblk_ref, cnt_ref, wup_hbm, wdn_hbm, bup_ref, bdn_ref, x_hbm, y_hbm,
                 wup_f, wdn_f, wup_s, wdn_s, xbuf, obuf, w_sem, in_sem, out_sem):
    e = pl.program_id(0)
    nb = nblk_ref[e]
    blk0 = blk0_ref[e]
    cnt = cnt_ref[e]
    tm = EXPERT_ROWS
    pair = 2 * LANES
    wslot = e % 2
    up_rows = D_MODEL // W_PIECES
    dn_rows = D_MODEL // (W_PIECES // 2)

    def w_piece(hbm, buf, ex, slot, p, rows):
        start = p * rows if isinstance(p, int) else pl.multiple_of(p * rows, rows)
        r = pl.ds(start, rows)
        return pltpu.make_async_copy(hbm.at[ex, r], buf.at[slot, r], w_sem.at[slot])

    def w_start(ex, slot, p):
        w_piece(wup_hbm, wup_f, ex, slot, p, up_rows).start()
        if isinstance(p, int):
            if p < W_PIECES // 2:
                w_piece(wdn_hbm, wdn_f, ex, slot, p, dn_rows).start()
        else:
            @pl.when(p < W_PIECES // 2)
            def _():
                w_piece(wdn_hbm, wdn_f, ex, slot, p, dn_rows).start()

    def w_wait(ex, slot):
        for p in range(W_PIECES):
            w_piece(wup_hbm, wup_f, ex, slot, p, up_rows).wait()
        for p in range(W_PIECES // 2):
            w_piece(wdn_hbm, wdn_f, ex, slot, p, dn_rows).wait()

    @pl.when(e == 0)
    def _():
        for p in range(W_PIECES):
            w_start(0, 0, p)

    w_wait(e, wslot)
    wup_ref = wup_f.at[wslot]
    wdn_ref = wdn_f.at[wslot]
    more = e + 1 < N_EXPERTS

    def x_copy(i, slot):
        rows = pl.ds(pl.multiple_of((blk0 + i) * tm, tm), tm)
        return pltpu.make_async_copy(x_hbm.at[rows], xbuf.at[slot], in_sem.at[slot])

    def y_copy(i, slot):
        rows = pl.ds(pl.multiple_of((blk0 + i) * tm, tm), tm)
        return pltpu.make_async_copy(obuf.at[slot], y_hbm.at[rows], out_sem.at[slot])

    @pl.when(nb > 0)
    def _():
        x_copy(0, 0).start(priority=1)
        r = lax.broadcasted_iota(jnp.int32, (pair, pair), 0)
        c = lax.broadcasted_iota(jnp.int32, (pair, pair), 1)
        perm = (r == jnp.where(c < LANES, 2 * c, 2 * (c - LANES) + 1)).astype(BF16)
        for g in range(2 * D_MODEL // pair):
            cols = slice(g * pair, (g + 1) * pair)
            wup_s[g] = jnp.dot(wup_ref[:, cols].astype(BF16), perm, preferred_element_type=F32).astype(BF16)
        for g in range(D_MODEL // pair):
            wdn_s[g] = wdn_ref[:, g * pair:(g + 1) * pair].astype(BF16)

        def block(i, carry):
            slot = i % 2
            x_copy(i, slot).wait()

            @pl.when(i + 1 < nb)
            def _():
                x_copy(i + 1, 1 - slot).start(priority=1)

            @pl.when(i >= 2)
            def _():
                y_copy(i - 2, slot).wait()

            @pl.when(jnp.logical_and(more, i < W_PIECES))
            def _():
                w_start(e + 1, 1 - wslot, i)

            row = lax.broadcasted_iota(jnp.int32, (tm, 1), 0)
            x = jnp.where(row < cnt - i * tm, _unpack_bf16_pair(xbuf[slot]), 0.0).astype(BF16)
            acts = []
            for g in range(2 * D_MODEL // pair):
                cols = slice(g * pair, (g + 1) * pair)
                h = jnp.dot(x, wup_s[g], preferred_element_type=F32) + bup_ref[0, :, cols]
                glu = jnp.minimum(h[:, :LANES], SWIGLU_LIMIT)
                lin = jnp.clip(h[:, LANES:], -SWIGLU_LIMIT, SWIGLU_LIMIT)
                acts.append((glu * jax.nn.sigmoid(SWIGLU_ALPHA * glu) * (lin + 1.0)).astype(BF16))
            act = jnp.concatenate(acts, axis=1)
            half_groups = D_MODEL // pair // 2
            for g in range(half_groups):
                ys = []
                for gg in (g, g + half_groups):
                    cols = slice(gg * pair, (gg + 1) * pair)
                    ys.append(jnp.dot(act, wdn_s[gg], preferred_element_type=F32) + bdn_ref[0, :, cols])
                obuf[slot, :, g * pair:(g + 1) * pair] = _pack_bf16_pair(ys[0], ys[1])
            y_copy(i, slot).start(priority=1)
            return carry

        lax.fori_loop(0, nb, block, 0)

        @pl.when(nb >= 2)
        def _():
            y_copy(nb - 2, nb % 2).wait()

        y_copy(nb - 1, (nb - 1) % 2).wait()

    for p in range(W_PIECES):
        @pl.when(jnp.logical_and(more, p >= nb))
        def _():
            w_start(e + 1, 1 - wslot, p)


def _experts(blk0, nblk, cnt, x_sorted, w_up, w_down, b_up_grouped, b_down):
    tm = EXPERT_ROWS
    per_expert = lambda a: pl.BlockSpec((1,) + a.shape[1:], lambda e, b0, nb, ct: (e, 0, 0))
    hbm = pl.BlockSpec(memory_space=pl.ANY)
    grid_spec = pltpu.PrefetchScalarGridSpec(
        num_scalar_prefetch=3,
        grid=(N_EXPERTS,),
        in_specs=[hbm, hbm, per_expert(b_up_grouped), per_expert(b_down), hbm],
        out_specs=hbm,
        scratch_shapes=[pltpu.VMEM((2,) + w_up.shape[1:], F32), pltpu.VMEM((2,) + w_down.shape[1:], F32),
                        pltpu.VMEM((2 * D_MODEL // (2 * LANES), D_MODEL, 2 * LANES), BF16),
                        pltpu.VMEM((D_MODEL // (2 * LANES), D_MODEL, 2 * LANES), BF16),
                        pltpu.VMEM((2, tm, x_sorted.shape[1]), x_sorted.dtype),
                        pltpu.VMEM((2, tm, D_MODEL // 2), jnp.uint32),
                        pltpu.SemaphoreType.DMA((2,)), pltpu.SemaphoreType.DMA((2,)), pltpu.SemaphoreType.DMA((2,))],
    )
    return pl.pallas_call(
        _expert_body,
        grid_spec=grid_spec,
        out_shape=jax.ShapeDtypeStruct((x_sorted.shape[0], D_MODEL // 2), jnp.uint32),
        compiler_params=_params(("arbitrary",)),
        name="experts",
    )(blk0, nblk, cnt, w_up, w_down, b_up_grouped, b_down, x_sorted)


def _combine_body(gate_ref, xmid_ref, gfin_ref, y0_ref, y1_ref, y2_ref, y3_ref, out_ref):
    gate = gate_ref[...]
    moe = _unpack_bf16_pair(y0_ref[...]) * gate[:, 0:1]
    for kx, y_ref in enumerate((y1_ref, y2_ref, y3_ref), start=1):
        moe = moe + _unpack_bf16_pair(y_ref[...]) * gate[:, kx:kx + 1]
    out_ref[...] = _rms(xmid_ref[...] + moe, gfin_ref[...])


def _combine(gates, first_token, xmid, gfin, y_rows):
    n = xmid.shape[0]
    tt = min(n, COMBINE_ROWS)
    nblk = n // tt
    blk0 = first_token // tt
    assert blk0 * tt == first_token
    choice = lambda kx: pl.BlockSpec((tt, y_rows.shape[1]), lambda i: (i + kx * nblk, 0))
    return pl.pallas_call(
        _combine_body,
        grid=(nblk,),
        in_specs=[pl.BlockSpec((tt, LANES), lambda i: (i + blk0, 0)),
                  pl.BlockSpec((tt, D_MODEL), lambda i: (i, 0)),
                  pl.BlockSpec((1, D_MODEL), lambda i: (0, 0))] + [choice(kx) for kx in range(TOP_K)],
        out_specs=pl.BlockSpec((tt, D_MODEL), lambda i: (i, 0)),
        out_shape=jax.ShapeDtypeStruct((n, D_MODEL), F32),
        compiler_params=_params(("arbitrary",)),
        name="combine",
    )(gates, xmid, gfin, y_rows, y_rows, y_rows, y_rows)


def kernel(x_prompt, x_sample, cache_k_win, cache_v_win, norm_attn_g, w_in, ln_v_g, ln_v_b, w_spatial, b_spatial,
           attn_sinks, w_out, norm_ffn_g, w_router, b_router, w_up, b_up, w_down, b_down, norm_final_g):
    bp, tp, _ = x_prompt.shape
    bs, ts, _ = x_sample.shape
    w_buf = cache_k_win.shape[2]
    assert bp == 1 and tp % MIX_ROWS == 0 and w_buf == WINDOW and (bs * ts) % PROJ_ROWS == 0 and 8 % ts == 0
    n_p, n_s = bp * tp, bs * ts
    row2 = lambda a: a.reshape(1, -1)

    w_in_bf = w_in[0].astype(BF16)
    w_out_bf = w_out[0].astype(BF16)
    tril = jnp.tril(jnp.ones((CHUNK, CHUNK), dtype=bool))
    wsp = jnp.where(tril[None], w_spatial[0], 0.0)
    wsp_bf = wsp.astype(BF16)
    bsp = jnp.broadcast_to(b_spatial[0][:, :, None], (GMLP_GROUPS, CHUNK, LANES))
    b_up_grouped = b_up[0].reshape(N_EXPERTS, -1, LANES, 2).transpose(0, 1, 3, 2).reshape(N_EXPERTS, 1, -1)
    bd = b_down[0][:, None, :]
    g1, g2, gfin = row2(norm_attn_g[0]), row2(norm_ffn_g[0]), row2(norm_final_g)
    lng, lnb = row2(ln_v_g[0]), row2(ln_v_b[0])
    wr_hi = w_router[0].astype(BF16)
    wr = jnp.concatenate([wr_hi, (w_router[0] - wr_hi.astype(F32)).astype(BF16)], axis=1)
    br = row2(b_router[0])
    sinks = attn_sinks[0]

    xp = x_prompt.reshape(n_p, D_MODEL)
    cs_p = _rotary_inputs(jnp.arange(tp, dtype=jnp.int32))
    q_p, k_p, v_p, a_p, vn_p, sgb_p = _proj(xp, g1, w_in_bf, cs_p, lng, lnb)
    xmid_p, xn2_p, lg_p = _mix(sinks, q_p, k_p, v_p, a_p, vn_p, sgb_p, xp, wsp_bf, bsp, w_out_bf, g2, wr, br)

    xs = x_sample.reshape(n_s, D_MODEL)
    pos_s = PAST_LEN + jnp.arange(ts, dtype=jnp.int32)
    cs_s = _rotary_inputs(jnp.tile(pos_s, bs))
    q_s, k_s, v_s, a_s, vn_s, sgb_s = _proj(xs, g1, w_in_bf, cs_s, lng, lnb)
    keys_minor = lambda t: t.reshape(bs, -1, KV_WIDTH).transpose(0, 2, 1)
    k_cache_t, v_cache_t = keys_minor(cache_k_win[0]), keys_minor(cache_v_win[0])
    pad_new = lambda t: jnp.pad(t.reshape(bs, ts, KV_WIDTH), ((0, 0), (0, (-ts) % 16), (0, 0)))
    sink_col = jnp.tile(sinks, ts).reshape(ts * N_HEADS, 1)
    o_s = _sample_attn(sink_col, q_s.reshape(bs, ts * N_HEADS, HEAD_DIM), k_cache_t, v_cache_t,
                       pad_new(k_s), pad_new(v_s)).reshape(n_s, Q_WIDTH)
    lag = np.arange(ts)[:, None] - np.arange(ts)[None, :]
    coef = jnp.stack([jnp.sum(jnp.where(lag == d, wsp[:, :ts, :ts], 0.0), axis=2)
                      for d in range(ts)])
    coef = jnp.repeat(coef.transpose(0, 2, 1), GMLP_WIDTH // GMLP_GROUPS, axis=2)
    coef = jnp.tile(coef, (1, 8 // ts, 1))
    bias = jnp.tile(jnp.repeat(b_spatial[0][:, :ts].T, GMLP_WIDTH // GMLP_GROUPS, axis=1), (8 // ts, 1))
    xmid_s, xn2_s, lg_s = _mix_sample(a_s, vn_s, sgb_s, o_s, xs, coef, bias, w_out_bf, g2, wr, br)

    tm = EXPERT_ROWS
    n_blocks = (n_p * TOP_K + N_EXPERTS * n_s) // tm + N_EXPERTS
    gate_p, dest_p, table_p = _route(lg_p, jnp.zeros((N_EXPERTS, LANES), F32), n_s)
    x_sorted = _sc_dispatch(xn2_p, dest_p, n_blocks * tm)
    gate_s, dest_s, table = _route(lg_s, table_p, None)
    x_sorted = _dispatch_small(xn2_s, dest_s, x_sorted)
    dest_t = jnp.concatenate([dest_p, dest_s], axis=1)
    meta = table.astype(jnp.int32)
    y_sorted = _experts(meta[:, 2], meta[:, 3], meta[:, 1], x_sorted, w_up[0], w_down[0], b_up_grouped, bd)
    yrows_p, yrows_s = _sc_collect(y_sorted, dest_t, n_p, n_s)
    y_p = _combine(gate_p, 0, xmid_p, gfin, yrows_p)
    y_s = _combine(gate_s, 0, xmid_s, gfin, yrows_s)

    k4 = lambda t: t.reshape(1, bp, -1, N_KV_HEADS, HEAD_DIM)
    next_window = lambda cache_t, new: _window(cache_t, keys_minor(new)).transpose(0, 2, 1).reshape(
        1, bs, w_buf, N_KV_HEADS, HEAD_DIM)
    return (y_p.reshape(bp, tp, D_MODEL),
            y_s.reshape(bs, ts, D_MODEL),
            k4(k_p[n_p - WINDOW:]),
            k4(v_p[n_p - WINDOW:]),
            vn_p[n_p - CHUNK:].reshape(1, bp, CHUNK, GMLP_WIDTH),
            next_window(k_cache_t, k_s),
            next_window(v_cache_t, v_s),
            vn_s.reshape(1, bs, ts, GMLP_WIDTH))
```

```python
import functools

import numpy as np
import jax
import jax.numpy as jnp
from jax import lax
from jax.experimental import pallas as pl
from jax.experimental.pallas import tpu as pltpu
from jax.experimental.pallas import tpu_sc as plsc

F32 = jnp.float32
BF16 = jnp.bfloat16

D_MODEL = 1024
HEAD_DIM = 64
N_HEADS = 16
GQA_GROUP = 8
N_KV_HEADS = 2
Q_WIDTH = 1024
KV_WIDTH = 128
WINDOW = 128
ROT_DIM = 16
ROPE_THETA = 500000.0
CHUNK = 128
GMLP_WIDTH = 1024
GMLP_GROUPS = 8
N_EXPERTS = 32
TOP_K = 4
SWIGLU_LIMIT = 7.0
SWIGLU_ALPHA = 1.702
RMS_EPS = 1e-5
LN_EPS = 1e-5
NEG_INF = -1e30
PAST_LEN = 16384

LANES = 128
VMEM_LIMIT = 56 * 1024 * 1024

PROJ_ROWS = 256
MIX_ROWS = 512
ROUTE_ROWS = 1024
EXPERT_ROWS = 256
W_PIECES = 8
COMBINE_ROWS = 1024

SC_CORES = 2
SC_WORKERS = 32
SC_ROWS = 64

_C_Q, _C_KV, _C_U, _C_VG, _C_GA, _C_GB, _C_END = 0, 1024, 1280, 2304, 3328, 4352, 5376


def _params(sem):
    return pltpu.CompilerParams(dimension_semantics=sem, vmem_limit_bytes=VMEM_LIMIT)


def _rms(x, g):
    return x * lax.rsqrt(jnp.mean(x * x, axis=-1, keepdims=True) + RMS_EPS) * g


def _pack_bf16_pair(lo, hi):
    lo_bits = lax.bitcast_convert_type(lo.astype(BF16).astype(F32), jnp.uint32)
    hi_bits = lax.bitcast_convert_type(hi.astype(BF16).astype(F32), jnp.uint32)
    return (lo_bits >> 16) | hi_bits


def _unpack_bf16_pair(words):
    lo = lax.bitcast_convert_type(words << 16, F32)
    hi = lax.bitcast_convert_type(words & jnp.uint32(0xFFFF0000), F32)
    return jnp.concatenate([lo, hi], axis=1)


def _proj_body(x_ref, g_ref, w_ref, cs_ref, rot_ref, lng_ref, lnb_ref,
               q_ref, k_ref, v_ref, a_ref, vn_ref, sgb_ref):
    h = _rms(x_ref[...], g_ref[...]).astype(BF16)
    tabs = lax.dot_general(cs_ref[...], rot_ref[...], (((0,), (0,)), ((), ())), preferred_element_type=F32)
    rc, rs1, rs2 = tabs[:, :LANES], tabs[:, LANES:2 * LANES], tabs[:, 2 * LANES:]

    def rot(z):
        return z * rc + pltpu.roll(z, LANES - ROT_DIM // 2, 1) * rs1 + pltpu.roll(z, ROT_DIM // 2, 1) * rs2

    def mm(lo, hi):
        return jnp.dot(h, w_ref[:, lo:hi], preferred_element_type=F32)

    zq = mm(_C_Q, _C_KV)
    for c in range(Q_WIDTH // LANES):
        sl = slice(c * LANES, (c + 1) * LANES)
        q_ref[:, sl] = (rot(zq[:, sl]) * (HEAD_DIM ** -0.5)).astype(BF16)
    zkv = mm(_C_KV, _C_U)
    k_ref[...] = rot(zkv[:, :KV_WIDTH])
    v_ref[...] = zkv[:, KV_WIDTH:]
    a_ref[...] = jax.nn.sigmoid(mm(_C_GA, _C_GB)) * jax.nn.gelu(mm(_C_U, _C_VG))
    zv = jax.nn.gelu(mm(_C_VG, _C_GA))
    zc = zv - jnp.mean(zv, axis=-1, keepdims=True)
    var = jnp.mean(zc * zc, axis=-1, keepdims=True)
    vn_ref[...] = zc * lax.rsqrt(var + LN_EPS) * lng_ref[...] + lnb_ref[...]
    sgb_ref[...] = jax.nn.sigmoid(mm(_C_GB, _C_END))


def _proj(x, norm_g, w_in_bf, cs, ln_g, ln_b):
    n = x.shape[0]
    tm = PROJ_ROWS
    row = lambda w: pl.BlockSpec((tm, w), lambda i: (i, 0))
    full = lambda a: pl.BlockSpec(a.shape, lambda i: (0,) * a.ndim)
    rot = jnp.asarray(np.tile(_ROT_EXPAND, (3, 1)), dtype=BF16)
    return pl.pallas_call(
        _proj_body,
        grid=(n // tm,),
        in_specs=[row(D_MODEL), full(norm_g), full(w_in_bf), pl.BlockSpec((cs.shape[0], tm), lambda i: (0, i)),
                  full(rot),
                  full(ln_g), full(ln_b)],
        out_specs=[row(Q_WIDTH), row(KV_WIDTH), row(KV_WIDTH), row(GMLP_WIDTH), row(GMLP_WIDTH), row(D_MODEL)],
        out_shape=[jax.ShapeDtypeStruct((n, Q_WIDTH), BF16),
                   jax.ShapeDtypeStruct((n, KV_WIDTH), F32),
                   jax.ShapeDtypeStruct((n, KV_WIDTH), F32),
                   jax.ShapeDtypeStruct((n, GMLP_WIDTH), F32),
                   jax.ShapeDtypeStruct((n, GMLP_WIDTH), F32),
                   jax.ShapeDtypeStruct((n, D_MODEL), F32)],
        compiler_params=_params(("arbitrary",)),
        name="proj",
    )(x, norm_g, w_in_bf, cs, rot, ln_g, ln_b)


_ROT_COLS = 32


def _rot_expand():
    half = ROT_DIM // 2
    m = np.zeros((_ROT_COLS, 3 * LANES), np.float32)
    for lane in range(LANES):
        d = lane % HEAD_DIM
        if d < ROT_DIM:
            m[d % half, lane] = 1.0
        else:
            m[2 * half, lane] = 1.0
        if d < half:
            m[half + d, LANES + lane] = -1.0
        elif d < ROT_DIM:
            m[half + d - half, 2 * LANES + lane] = 1.0
    return m


_ROT_EXPAND = _rot_expand()


def _rotary_inputs(pos):
    half = ROT_DIM // 2
    inv_freq = ROPE_THETA ** (-jnp.arange(half, dtype=F32) / half)
    ang = inv_freq[:, None] * pos.astype(F32)[None, :]
    n = pos.shape[0]
    cs = jnp.concatenate([jnp.cos(ang), jnp.sin(ang), jnp.ones((1, n), F32),
                          jnp.zeros((_ROT_COLS - 2 * half - 1, n), F32)], axis=0)
    hi = cs.astype(BF16)
    rest = cs - hi.astype(F32)
    mid = rest.astype(BF16)
    lo = (rest - mid.astype(F32)).astype(BF16)
    return jnp.concatenate([hi, mid, lo], axis=0)


def _finish_rows(merged_bf, x, wout_ref, g2_ref, wr_ref, br_ref, xmid_ref, xn2_ref, lg_ref):
    xm = x + jnp.dot(merged_bf, wout_ref[...], preferred_element_type=F32)
    xmid_ref[...] = xm
    xn = _rms(xm, g2_ref[...])
    x_hi = xn.astype(BF16)
    x_lo = (xn - x_hi.astype(F32)).astype(BF16)
    w_hl = wr_ref[...]
    p_hi = jnp.dot(x_hi, w_hl, preferred_element_type=F32)
    p_lo = jnp.dot(x_lo, w_hl[:, :N_EXPERTS], preferred_element_type=F32)
    lg = p_hi[:, :N_EXPERTS] + (p_hi[:, N_EXPERTS:] + p_lo) + br_ref[...]
    wide = jnp.concatenate([lg, jnp.zeros((lg.shape[0], LANES - N_EXPERTS), F32)], axis=1)
    lg_ref[...] = wide.T[:N_EXPERTS]
    xn2_ref[...] = _pack_bf16_pair(xn[:, :D_MODEL // 2], xn[:, D_MODEL // 2:])


def _mix_body(sinks_ref, q_ref, k_ref, kp_ref, v_ref, vp_ref, a_ref, vn_ref, sgb_ref, x_ref,
              wsp_ref, bsp_ref, wout_ref, g2_ref, wr_ref, br_ref,
              xmid_ref, xn2_ref, lg_ref, kcat, vcat, mrg, key_s):
    i = pl.program_id(0)
    nsub = MIX_ROWS // WINDOW
    kcat[0:WINDOW] = kp_ref[...]
    kcat[WINDOW:] = k_ref[...]
    vcat[0:WINDOW] = vp_ref[...]
    vcat[WINDOW:] = v_ref[...]

    pair_rows = (GQA_GROUP // 2) * WINDOW
    lane_kv = lax.broadcasted_iota(jnp.int32, (2 * WINDOW, LANES), 1)
    lane_o = lax.broadcasted_iota(jnp.int32, (pair_rows, LANES), 1)

    @pl.when(i == 0)
    def _():
        rq = lax.broadcasted_iota(jnp.int32, (pair_rows, 4 * WINDOW), 0) & (WINDOW - 1)
        ck = lax.broadcasted_iota(jnp.int32, (pair_rows, 4 * WINDOW), 1) & (2 * WINDOW - 1)
        key_s[...] = jnp.where((ck > rq) & (ck <= rq + WINDOW), ck, -1)

    row_p = lax.broadcasted_iota(jnp.int32, (pair_rows, 1), 0) >> 7
    sink_cols = []
    for kk in range(N_KV_HEADS):
        h0 = kk * GQA_GROUP
        se = jnp.full((pair_rows, 1), sinks_ref[h0], F32)
        so = jnp.full((pair_rows, 1), sinks_ref[h0 + 1], F32)
        for p in range(1, GQA_GROUP // 2):
            se = jnp.where(row_p == p, sinks_ref[h0 + 2 * p], se)
            so = jnp.where(row_p == p, sinks_ref[h0 + 2 * p + 1], so)
        sink_cols.append((se, so))

    def sub(j, carry):
        off = pl.multiple_of(j * WINDOW, WINDOW)
        rows = pl.ds(off, WINDOW)
        for g in range(GMLP_GROUPS):
            cols = slice(g * LANES, (g + 1) * LANES)
            s = jnp.dot(wsp_ref[g], vn_ref[rows, cols].astype(BF16), preferred_element_type=F32) + bsp_ref[g]
            mrg[rows, cols] = a_ref[rows, cols] * s
        kblk = kcat[pl.ds(off, 2 * WINDOW), :]
        vblk = vcat[pl.ds(off, 2 * WINDOW), :]
        kswp = pltpu.roll(kblk, HEAD_DIM, 1)
        vswp = pltpu.roll(vblk, HEAD_DIM, 1)
        kmin = jnp.where(jnp.logical_and(i == 0, j == 0), WINDOW, 0)
        allowed = key_s[...] >= kmin
        for kk in range(N_KV_HEADS):
            lo_src, hi_src = (kblk, kswp) if kk == 0 else (kswp, kblk)
            kbd = jnp.concatenate([jnp.where(lane_kv < HEAD_DIM, lo_src, 0.0),
                                   jnp.where(lane_kv >= HEAD_DIM, hi_src, 0.0)], axis=0).astype(BF16)
            lo_src, hi_src = (vblk, vswp) if kk == 0 else (vswp, vblk)
            vbd = jnp.concatenate([jnp.where(lane_kv < HEAD_DIM, lo_src, 0.0),
                                   jnp.where(lane_kv >= HEAD_DIM, hi_src, 0.0)], axis=0).astype(BF16)
            pair0 = kk * (GQA_GROUP // 2)
            qs = jnp.concatenate([q_ref[rows, (pair0 + p) * LANES:(pair0 + p + 1) * LANES]
                                  for p in range(GQA_GROUP // 2)], axis=0)
            lg = lax.dot_general(qs, kbd, (((1,), (1,)), ((), ())), preferred_element_type=F32)
            lg = jnp.where(allowed, lg, NEG_INF)
            se, so = sink_cols[kk]
            le, lo = lg[:, :2 * WINDOW], lg[:, 2 * WINDOW:]
            me = jnp.maximum(jnp.max(le, axis=1, keepdims=True), se)
            mo = jnp.maximum(jnp.max(lo, axis=1, keepdims=True), so)
            pe = jnp.exp(le - me)
            po = jnp.exp(lo - mo)
            de = jnp.sum(pe, axis=1, keepdims=True) + jnp.exp(se - me)
            do = jnp.sum(po, axis=1, keepdims=True) + jnp.exp(so - mo)
            pr = jnp.concatenate([pe, po], axis=1).astype(BF16)
            o = jnp.dot(pr, vbd, preferred_element_type=F32)
            o = o / jnp.where(lane_o < HEAD_DIM, de, do)
            for p in range(GQA_GROUP // 2):
                cols = slice((pair0 + p) * LANES, (pair0 + p + 1) * LANES)
                mrg[rows, cols] += sgb_ref[rows, cols] * o[p * WINDOW:(p + 1) * WINDOW]
        return carry

    lax.fori_loop(0, nsub, sub, 0)
    _finish_rows(mrg[...].astype(BF16), x_ref[...], wout_ref, g2_ref, wr_ref, br_ref, xmid_ref, xn2_ref, lg_ref)


def _mix(sinks, q, k, v, a, vn, sgb, x, wsp, bsp, wout, g2, wr, br):
    n = x.shape[0]
    tm = MIX_ROWS
    nsub = tm // WINDOW
    row = lambda w: pl.BlockSpec((tm, w), lambda i: (i, 0))
    prev = pl.BlockSpec((WINDOW, KV_WIDTH), lambda i: (jnp.maximum(i * nsub - 1, 0), 0))
    full = lambda arr: pl.BlockSpec(arr.shape, lambda i: (0,) * arr.ndim)
    smem = pl.BlockSpec(memory_space=pltpu.SMEM)
    return pl.pallas_call(
        _mix_body,
        grid=(n // tm,),
        in_specs=[smem, row(Q_WIDTH), row(KV_WIDTH), prev, row(KV_WIDTH), prev,
                  row(GMLP_WIDTH), row(GMLP_WIDTH), row(D_MODEL), row(D_MODEL),
                  full(wsp), full(bsp), full(wout), full(g2), full(wr), full(br)],
        out_specs=[row(D_MODEL), row(D_MODEL // 2), pl.BlockSpec((N_EXPERTS, tm), lambda i: (0, i))],
        out_shape=[jax.ShapeDtypeStruct((n, D_MODEL), F32),
                   jax.ShapeDtypeStruct((n, D_MODEL // 2), jnp.uint32),
                   jax.ShapeDtypeStruct((N_EXPERTS, n), F32)],
        scratch_shapes=[pltpu.VMEM((tm + WINDOW, KV_WIDTH), F32),
                        pltpu.VMEM((tm + WINDOW, KV_WIDTH), F32),
                        pltpu.VMEM((tm, D_MODEL), F32),
                        pltpu.VMEM((GQA_GROUP // 2 * WINDOW, 4 * WINDOW), jnp.int32)],
        compiler_params=_params(("arbitrary",)),
        name="mix_prompt",
    )(sinks, q, k, k, v, v, a, vn, sgb, x, wsp, bsp, wout, g2, wr, br)


def _sample_attn_body(sink_ref, q_ref, kc_ref, vc_ref, kn_ref, vn_ref, o_ref):
    q = q_ref[...]
    nq, nc, nn = q.shape[1], kc_ref.shape[2], kn_ref.shape[1]
    row_c = lax.broadcasted_iota(jnp.int32, (1, nq, nc), 1)
    row_n = lax.broadcasted_iota(jnp.int32, (1, nq, nn), 1)
    row_o = lax.broadcasted_iota(jnp.int32, (1, nq, HEAD_DIM), 1)
    first_kv = lambda row: (row % N_HEADS) < GQA_GROUP
    cached = lambda ref, kk: ref[:, kk * HEAD_DIM:(kk + 1) * HEAD_DIM, :].astype(BF16)
    fresh = lambda ref, kk: ref[:, :, kk * HEAD_DIM:(kk + 1) * HEAD_DIM].astype(BF16)
    lg_c = jnp.where(first_kv(row_c),
                     jnp.einsum("bqd,bdk->bqk", q, cached(kc_ref, 0), preferred_element_type=F32),
                     jnp.einsum("bqd,bdk->bqk", q, cached(kc_ref, 1), preferred_element_type=F32))
    lg_n = jnp.where(first_kv(row_n),
                     jnp.einsum("bqd,bkd->bqk", q, fresh(kn_ref, 0), preferred_element_type=F32),
                     jnp.einsum("bqd,bkd->bqk", q, fresh(kn_ref, 1), preferred_element_type=F32))
    j_c = lax.broadcasted_iota(jnp.int32, (1, nq, nc), 2)
    j_n = lax.broadcasted_iota(jnp.int32, (1, nq, nn), 2)
    lg_c = jnp.where(j_c > row_c // N_HEADS + (nc - WINDOW), lg_c, NEG_INF)
    lg_n = jnp.where(j_n <= row_n // N_HEADS, lg_n, NEG_INF)
    sink = sink_ref[...][None]
    m = jnp.maximum(jnp.maximum(jnp.max(lg_c, axis=2, keepdims=True), jnp.max(lg_n, axis=2, keepdims=True)), sink)
    p_c = jnp.exp(lg_c - m)
    p_n = jnp.exp(lg_n - m)
    den = jnp.sum(p_c, axis=2, keepdims=True) + jnp.sum(p_n, axis=2, keepdims=True) + jnp.exp(sink - m)
    pb_c, pb_n = p_c.astype(BF16), p_n.astype(BF16)
    heads_out = lambda kk: (jnp.einsum("bqk,bdk->bqd", pb_c, cached(vc_ref, kk), preferred_element_type=F32)
                            + jnp.einsum("bqk,bkd->bqd", pb_n, fresh(vn_ref, kk), preferred_element_type=F32))
    o_ref[...] = jnp.where(first_kv(row_o), heads_out(0), heads_out(1)) / den


def _sample_attn(sink_col, q3, k_cache_t, v_cache_t, k_new, v_new):
    nb = q3.shape[0]
    bb = 32
    blk = lambda a: pl.BlockSpec((bb,) + a.shape[1:], lambda b: (b, 0, 0))
    args = (q3, k_cache_t, v_cache_t, k_new, v_new)
    return pl.pallas_call(
        _sample_attn_body,
        grid=(nb // bb,),
        in_specs=[pl.BlockSpec(sink_col.shape, lambda b: (0, 0))] + [blk(a) for a in args],
        out_specs=blk(q3),
        out_shape=jax.ShapeDtypeStruct(q3.shape, F32),
        compiler_params=_params(("arbitrary",)),
        name="attn_sample",
    )(sink_col, *args)


def _window_body(c_ref, n_ref, o_ref):
    old = pltpu.roll(c_ref[...], c_ref.shape[2] - n_ref.shape[2], 2)
    lane = lax.broadcasted_iota(jnp.int32, old.shape, 2)
    new = n_ref[...]
    first_new = old.shape[2] - new.shape[2]
    out = old
    for t in range(new.shape[2]):
        out = jnp.where(lane == first_new + t, new[:, :, t:t + 1], out)
    o_ref[...] = out


def _window(cache_t, new_t):
    nb = cache_t.shape[0]
    bb = 16
    blk = lambda a: pl.BlockSpec((bb,) + a.shape[1:], lambda b: (b, 0, 0))
    return pl.pallas_call(
        _window_body,
        grid=(nb // bb,),
        in_specs=[blk(cache_t), blk(new_t)],
        out_specs=blk(cache_t),
        out_shape=jax.ShapeDtypeStruct(cache_t.shape, cache_t.dtype),
        compiler_params=_params(("arbitrary",)),
        name="window",
    )(cache_t, new_t)


def _mix_sample_body(a_ref, vn_ref, sgb_ref, o_ref, x_ref, coef_ref, bias_ref,
                     wout_ref, g2_ref, wr_ref, br_ref, xmid_ref, xn2_ref, lg_ref):
    vn = vn_ref[...]
    n, width = vn.shape
    rows8 = lambda t: t.reshape(n // 8, 8, width)
    s = bias_ref[...][None] + coef_ref[0][None] * rows8(vn)
    for d in range(1, coef_ref.shape[0]):
        s = s + coef_ref[d][None] * rows8(pltpu.roll(vn, d, 0))
    merged = a_ref[...] * s.reshape(n, width) + sgb_ref[...] * o_ref[...]
    _finish_rows(merged.astype(BF16), x_ref[...], wout_ref, g2_ref, wr_ref, br_ref, xmid_ref, xn2_ref, lg_ref)


def _mix_sample(a, vn, sgb, o, x, coef, bias, wout, g2, wr, br):
    n = x.shape[0]
    args = (a, vn, sgb, o, x, coef, bias, wout, g2, wr, br)
    full = lambda arr: pl.BlockSpec(arr.shape, lambda i: (0,) * arr.ndim)
    return pl.pallas_call(
        _mix_sample_body,
        grid=(1,),
        in_specs=[full(arr) for arr in args],
        out_specs=[pl.BlockSpec((n, D_MODEL), lambda i: (0, 0)), pl.BlockSpec((n, D_MODEL // 2), lambda i: (0, 0)),
                   pl.BlockSpec((N_EXPERTS, n), lambda i: (0, 0))],
        out_shape=[jax.ShapeDtypeStruct((n, D_MODEL), F32),
                   jax.ShapeDtypeStruct((n, D_MODEL // 2), jnp.uint32),
                   jax.ShapeDtypeStruct((N_EXPERTS, n), F32)],
        compiler_params=_params(("arbitrary",)),
        name="mix_sample",
    )(*args)


def _rows8(rows, dtype):
    n = rows[0].shape[1]
    sub = lax.broadcasted_iota(jnp.int32, (8, n), 0)
    out = jnp.zeros((8, n), dtype)
    for kx, r in enumerate(rows):
        out = jnp.where(sub == kx, r.astype(dtype), out)
    return out


def _route_body(nblk, reserve, lg_ref, prior_ref, gate_ref, dest_ref, meta_ref, idx_s, rank_s, base):
    i = pl.program_id(0)

    @pl.when(i == 0)
    def _():
        base[...] = jnp.zeros_like(base)

    l = lg_ref[...]
    tb = l.shape[1]
    sub = lax.broadcasted_iota(jnp.int32, l.shape, 0).astype(F32)
    vals, idxs, sels = [], [], []
    for _ in range(TOP_K):
        m = jnp.max(l, axis=0, keepdims=True)
        ik = jnp.min(jnp.where(l == m, sub, float(N_EXPERTS)), axis=0, keepdims=True)
        sel = sub == ik
        l = jnp.where(sel, -jnp.inf, l)
        vals.append(m)
        idxs.append(ik)
        sels.append(sel)
    es = [jnp.exp(vk - vals[0]) for vk in vals]
    den = es[0] + es[1] + es[2] + es[3]
    onehot = jnp.zeros(l.shape, F32)
    for sel in sels:
        onehot = onehot + sel.astype(F32)
    earlier = (lax.broadcasted_iota(jnp.int32, (tb, tb), 0) < lax.broadcasted_iota(jnp.int32, (tb, tb), 1))
    before = jnp.dot(onehot.astype(BF16), earlier.astype(BF16), preferred_element_type=F32) + base[...]
    ranks = [jnp.sum(jnp.where(sel, before, 0.0), axis=0, keepdims=True) for sel in sels]
    base[...] += jnp.sum(onehot, axis=1, keepdims=True)
    idx_s[i] = _rows8(idxs, F32)
    rank_s[i] = _rows8(ranks, F32)
    gates = jnp.concatenate([_rows8([e / den for e in es], F32), jnp.zeros((LANES - 8, tb), F32)], axis=0)
    gate_ref[...] = gates.T

    @pl.when(i == nblk - 1)
    def _():
        cnt = base[...]
        if reserve is None:
            seg0 = prior_ref[:, 0:1]
            placed = prior_ref[:, 1:2]
        else:
            cap = jnp.ceil((cnt + reserve) / EXPERT_ROWS) * EXPERT_ROWS
            lower = (lax.broadcasted_iota(jnp.int32, (N_EXPERTS, N_EXPERTS), 1) <
                     lax.broadcasted_iota(jnp.int32, (N_EXPERTS, N_EXPERTS), 0)).astype(F32)
            seg0 = jnp.dot(lower, jnp.broadcast_to(cap, (N_EXPERTS, LANES)), preferred_element_type=F32,
                           precision=lax.Precision.HIGHEST)[:, :1]
            placed = jnp.zeros_like(cnt)
        total = placed + cnt
        lane = lax.broadcasted_iota(jnp.int32, (N_EXPERTS, LANES), 1)
        meta_ref[...] = jnp.where(lane == 0, seg0, jnp.where(lane == 1, total, jnp.where(
            lane == 2, seg0 / EXPERT_ROWS, jnp.where(lane == 3, jnp.ceil(total / EXPERT_ROWS), 0.0))))
        first = seg0 + placed
        sub_e = lax.broadcasted_iota(jnp.int32, (N_EXPERTS, tb), 0).astype(F32)
        for b in range(nblk):
            idx, rank = idx_s[b], rank_s[b]
            rows = [jnp.sum(jnp.where(sub_e == idx[kx:kx + 1], first, 0.0), axis=0, keepdims=True)
                    + rank[kx:kx + 1] for kx in range(TOP_K)]
            dest_ref[:, b * tb:(b + 1) * tb] = _rows8(rows, jnp.int32)


def _route(logits, prior, reserve):
    n = logits.shape[1]
    tb = min(n, ROUTE_ROWS)
    nblk = n // tb
    assert nblk * tb == n
    table = pl.BlockSpec((N_EXPERTS, LANES), lambda i: (0, 0))
    return pl.pallas_call(
        functools.partial(_route_body, nblk, reserve),
        grid=(nblk,),
        in_specs=[pl.BlockSpec((N_EXPERTS, tb), lambda i: (0, i)), table],
        out_specs=[pl.BlockSpec((tb, LANES), lambda i: (i, 0)), pl.BlockSpec((8, n), lambda i: (0, 0)), table],
        out_shape=[jax.ShapeDtypeStruct((n, LANES), F32),
                   jax.ShapeDtypeStruct((8, n), jnp.int32),
                   jax.ShapeDtypeStruct((N_EXPERTS, LANES), F32)],
        scratch_shapes=[pltpu.VMEM((nblk, 8, tb), F32), pltpu.VMEM((nblk, 8, tb), F32),
                        pltpu.VMEM((N_EXPERTS, 1), F32)],
        compiler_params=_params(("arbitrary",)),
        name="route",
    )(logits, prior)


def _sc_mesh():
    return plsc.VectorSubcoreMesh(core_axis_name="c", subcore_axis_name="s")


def _sc_worker():
    return lax.axis_index("s") * SC_CORES + lax.axis_index("c")


def _sc_dispatch(x, dest_t, n_slots):
    chunk = SC_ROWS
    n = x.shape[0]
    per_w = n // (SC_WORKERS * chunk)
    assert per_w * SC_WORKERS * chunk == n and per_w % 2 == 0
    d3 = dest_t.reshape(dest_t.shape[0], n // chunk, chunk)
    width, dtype = x.shape[1], x.dtype

    @functools.partial(
        pl.kernel, mesh=_sc_mesh(),
        out_type=jax.ShapeDtypeStruct((n_slots, width), dtype),
        scratch_types=[pltpu.VMEM((TOP_K, per_w, chunk), jnp.int32),
                       pltpu.VMEM((2, chunk, width), dtype),
                       pltpu.SemaphoreType.DMA, pltpu.SemaphoreType.DMA],
        compiler_params=pltpu.CompilerParams(use_tc_tiling_on_sc=True),
        name="dispatch")
    def run(x_hbm, d_hbm, out_hbm, idx_v, rows_v, rsem, wsem):
        wid = _sc_worker()
        pltpu.sync_copy(d_hbm.at[pl.ds(0, TOP_K), pl.ds(wid * per_w, per_w)], idx_v)

        def read(j, slot):
            return pltpu.make_async_copy(x_hbm.at[pl.ds((wid * per_w + j) * chunk, chunk)], rows_v.at[slot], rsem)

        def scatter(j, slot):
            copies = [pltpu.async_copy(rows_v.at[slot], out_hbm.at[idx_v.at[kx, j]], wsem) for kx in range(TOP_K)]
            for cp in copies:
                cp.wait()

        read(0, 0).start()

        def body(h, carry):
            j = 2 * h
            read(j, 0).wait()
            read(j + 1, 1).start()
            scatter(j, 0)
            read(j + 1, 1).wait()

            @pl.when(j + 2 < per_w)
            def _():
                read(j + 2, 0).start()

            scatter(j + 1, 1)
            return carry

        lax.fori_loop(0, per_w // 2, body, 0)

    return run(x, d3)


def _dispatch_small_body(dest_ref, x_ref, _, out_hbm, sem):
    n = x_ref.shape[0]

    def body(t, carry):
        for kx in range(TOP_K):
            pltpu.make_async_copy(x_ref.at[pl.ds(t, 1)], out_hbm.at[pl.ds(dest_ref[kx, t], 1)], sem).start()
        return carry

    lax.fori_loop(0, n, body, 0, unroll=4)
    for _ in range(TOP_K):
        pltpu.make_async_copy(x_ref, out_hbm.at[pl.ds(0, n)], sem).wait()


def _dispatch_small(x, dest_t, x_sorted):
    return pl.pallas_call(
        _dispatch_small_body,
        grid=(1,),
        in_specs=[pl.BlockSpec(memory_space=pltpu.SMEM), pl.BlockSpec(x.shape, lambda i: (0, 0)),
                  pl.BlockSpec(memory_space=pl.ANY)],
        out_specs=pl.BlockSpec(memory_space=pl.ANY),
        out_shape=jax.ShapeDtypeStruct(x_sorted.shape, x_sorted.dtype),
        scratch_shapes=[pltpu.SemaphoreType.DMA],
        input_output_aliases={2: 0},
        compiler_params=_params(("arbitrary",)),
        name="dispatch_small",
    )(dest_t, x, x_sorted)


def _sc_collect(y_sorted, dest_t, tok0, n, small=None):
    chunk = SC_ROWS
    per_choice = SC_WORKERS // TOP_K
    per_w = n // (per_choice * chunk)
    assert per_w * per_choice * chunk == n and per_w % 2 == 0 and tok0 % chunk == 0
    d3 = dest_t.reshape(dest_t.shape[0], dest_t.shape[1] // chunk, chunk)
    width, dtype = y_sorted.shape[1], y_sorted.dtype
    out_type = [jax.ShapeDtypeStruct((TOP_K * n, width), dtype)]
    if small is not None:
        small0, n_small = small
        assert n_small == per_choice * chunk and small0 % chunk == 0
        out_type.append(jax.ShapeDtypeStruct((TOP_K * n_small, width), dtype))

    @functools.partial(
        pl.kernel, mesh=_sc_mesh(),
        out_type=out_type,
        scratch_types=[pltpu.VMEM((per_w, chunk), jnp.int32),
                       pltpu.VMEM((1, chunk), jnp.int32),
                       pltpu.VMEM((2, chunk, width), dtype),
                       pltpu.SemaphoreType.DMA, pltpu.SemaphoreType.DMA],
        compiler_params=pltpu.CompilerParams(use_tc_tiling_on_sc=True),
        name="collect")
    def run(y_hbm, d_hbm, *rest):
        outs, (ip_v, is_v, rows_v, gsem, wsem) = rest[:len(out_type)], rest[len(out_type):]
        op_hbm = outs[0]
        wid = _sc_worker()
        choice = wid // per_choice
        part = wid % per_choice
        pltpu.sync_copy(d_hbm.at[choice, pl.ds(tok0 // chunk + part * per_w, per_w)], ip_v)

        def gather(idx_v, j, slot):
            return pltpu.make_async_copy(y_hbm.at[idx_v.at[j]], rows_v.at[slot], gsem)

        def write(j, slot):
            return pltpu.make_async_copy(rows_v.at[slot], op_hbm.at[pl.ds((wid * per_w + j) * chunk, chunk)], wsem)

        gather(ip_v, 0, 0).start()

        def body(h, carry):
            j = 2 * h
            gather(ip_v, j, 0).wait()

            @pl.when(h > 0)
            def _():
                write(j - 1, 1).wait()

            gather(ip_v, j + 1, 1).start()
            write(j, 0).start()
            gather(ip_v, j + 1, 1).wait()
            write(j, 0).wait()

            @pl.when(j + 2 < per_w)
            def _():
                gather(ip_v, j + 2, 0).start()

            write(j + 1, 1).start()
            return carry

        lax.fori_loop(0, per_w // 2, body, 0)
        write(per_w - 1, 1).wait()

        if small is not None:
            pltpu.sync_copy(d_hbm.at[choice, pl.ds(small0 // chunk + part, 1)], is_v)
            gather(is_v, 0, 0).start()
            gather(is_v, 0, 0).wait()
            pltpu.sync_copy(rows_v.at[0], outs[1].at[pl.ds(wid * chunk, chunk)])

    return run(y_sorted, d3)


def _expert_body(blk0_ref, nblk_ref, cnt_ref, wup_hbm, wdn_hbm, bup_ref, bdn_ref, x_hbm, y_hbm,
                 wup_f, wdn_f, wup_s, wdn_s, xbuf, obuf, w_sem, in_sem, out_sem):
    e = pl.program_id(0)
    nb = nblk_ref[e]
    blk0 = blk0_ref[e]
    cnt = cnt_ref[e]
    tm = EXPERT_ROWS
    pair = 2 * LANES
    wslot = e % 2
    up_rows = D_MODEL // W_PIECES
    dn_rows = D_MODEL // (W_PIECES // 2)

    def w_piece(hbm, buf, ex, slot, p, rows):
        start = p * rows if isinstance(p, int) else pl.multiple_of(p * rows, rows)
        r = pl.ds(start, rows)
        return pltpu.make_async_copy(hbm.at[ex, r], buf.at[slot, r], w_sem.at[slot])

    def w_start(ex, slot, p):
        w_piece(wup_hbm, wup_f, ex, slot, p, up_rows).start()
        if isinstance(p, int):
            if p < W_PIECES // 2:
                w_piece(wdn_hbm, wdn_f, ex, slot, p, dn_rows).start()
        else:
            @pl.when(p < W_PIECES // 2)
            def _():
                w_piece(wdn_hbm, wdn_f, ex, slot, p, dn_rows).start()

    def w_wait(ex, slot):
        for p in range(W_PIECES):
            w_piece(wup_hbm, wup_f, ex, slot, p, up_rows).wait()
        for p in range(W_PIECES // 2):
            w_piece(wdn_hbm, wdn_f, ex, slot, p, dn_rows).wait()

    @pl.when(e == 0)
    def _():
        for p in range(W_PIECES):
            w_start(0, 0, p)

    w_wait(e, wslot)
    wup_ref = wup_f.at[wslot]
    wdn_ref = wdn_f.at[wslot]
    more = e + 1 < N_EXPERTS

    def x_copy(i, slot):
        rows = pl.ds(pl.multiple_of((blk0 + i) * tm, tm), tm)
        return pltpu.make_async_copy(x_hbm.at[rows], xbuf.at[slot], in_sem.at[slot])

    def y_copy(i, slot):
        rows = pl.ds(pl.multiple_of((blk0 + i) * tm, tm), tm)
        return pltpu.make_async_copy(obuf.at[slot], y_hbm.at[rows], out_sem.at[slot])

    @pl.when(nb > 0)
    def _():
        x_copy(0, 0).start(priority=1)
        r = lax.broadcasted_iota(jnp.int32, (pair, pair), 0)
        c = lax.broadcasted_iota(jnp.int32, (pair, pair), 1)
        perm = (r == jnp.where(c < LANES, 2 * c, 2 * (c - LANES) + 1)).astype(BF16)
        for g in range(2 * D_MODEL // pair):
            cols = slice(g * pair, (g + 1) * pair)
            wup_s[g] = jnp.dot(wup_ref[:, cols].astype(BF16), perm, preferred_element_type=F32).astype(BF16)
        for g in range(D_MODEL // pair):
            wdn_s[g] = wdn_ref[:, g * pair:(g + 1) * pair].astype(BF16)

        def block(i, carry):
            slot = i % 2
            x_copy(i, slot).wait()

            @pl.when(i + 1 < nb)
            def _():
                x_copy(i + 1, 1 - slot).start(priority=1)

            @pl.when(i >= 2)
            def _():
                y_copy(i - 2, slot).wait()

            @pl.when(jnp.logical_and(more, i < W_PIECES))
            def _():
                w_start(e + 1, 1 - wslot, i)

            row = lax.broadcasted_iota(jnp.int32, (tm, 1), 0)
            x = jnp.where(row < cnt - i * tm, _unpack_bf16_pair(xbuf[slot]), 0.0).astype(BF16)
            acts = []
            for g in range(2 * D_MODEL // pair):
                cols = slice(g * pair, (g + 1) * pair)
                h = jnp.dot(x, wup_s[g], preferred_element_type=F32) + bup_ref[0, :, cols]
                glu = jnp.minimum(h[:, :LANES], SWIGLU_LIMIT)
                lin = jnp.clip(h[:, LANES:], -SWIGLU_LIMIT, SWIGLU_LIMIT)
                acts.append((glu * jax.nn.sigmoid(SWIGLU_ALPHA * glu) * (lin + 1.0)).astype(BF16))
            act = jnp.concatenate(acts, axis=1)
            half_groups = D_MODEL // pair // 2
            for g in range(half_groups):
                ys = []
                for gg in (g, g + half_groups):
                    cols = slice(gg * pair, (gg + 1) * pair)
                    ys.append(jnp.dot(act, wdn_s[gg], preferred_element_type=F32) + bdn_ref[0, :, cols])
                obuf[slot, :, g * pair:(g + 1) * pair] = _pack_bf16_pair(ys[0], ys[1])
            y_copy(i, slot).start(priority=1)
            return carry

        lax.fori_loop(0, nb, block, 0)

        @pl.when(nb >= 2)
        def _():
            y_copy(nb - 2, nb % 2).wait()

        y_copy(nb - 1, (nb - 1) % 2).wait()

    for p in range(W_PIECES):
        @pl.when(jnp.logical_and(more, p >= nb))
        def _():
            w_start(e + 1, 1 - wslot, p)


def _experts(blk0, nblk, cnt, x_sorted, w_up, w_down, b_up_grouped, b_down):
    tm = EXPERT_ROWS
    per_expert = lambda a: pl.BlockSpec((1,) + a.shape[1:], lambda e, b0, nb, ct: (e, 0, 0))
    hbm = pl.BlockSpec(memory_space=pl.ANY)
    grid_spec = pltpu.PrefetchScalarGridSpec(
        num_scalar_prefetch=3,
        grid=(N_EXPERTS,),
        in_specs=[hbm, hbm, per_expert(b_up_grouped), per_expert(b_down), hbm],
        out_specs=hbm,
        scratch_shapes=[pltpu.VMEM((2,) + w_up.shape[1:], F32), pltpu.VMEM((2,) + w_down.shape[1:], F32),
                        pltpu.VMEM((2 * D_MODEL // (2 * LANES), D_MODEL, 2 * LANES), BF16),
                        pltpu.VMEM((D_MODEL // (2 * LANES), D_MODEL, 2 * LANES), BF16),
                        pltpu.VMEM((2, tm, x_sorted.shape[1]), x_sorted.dtype),
                        pltpu.VMEM((2, tm, D_MODEL // 2), jnp.uint32),
                        pltpu.SemaphoreType.DMA((2,)), pltpu.SemaphoreType.DMA((2,)), pltpu.SemaphoreType.DMA((2,))],
    )
    return pl.pallas_call(
        _expert_body,
        grid_spec=grid_spec,
        out_shape=jax.ShapeDtypeStruct((x_sorted.shape[0], D_MODEL // 2), jnp.uint32),
        compiler_params=_params(("arbitrary",)),
        name="experts",
    )(blk0, nblk, cnt, w_up, w_down, b_up_grouped, b_down, x_sorted)


def _combine_body(gate_ref, xmid_ref, gfin_ref, y0_ref, y1_ref, y2_ref, y3_ref, out_ref):
    gate = gate_ref[...]
    moe = _unpack_bf16_pair(y0_ref[...]) * gate[:, 0:1]
    for kx, y_ref in enumerate((y1_ref, y2_ref, y3_ref), start=1):
        moe = moe + _unpack_bf16_pair(y_ref[...]) * gate[:, kx:kx + 1]
    out_ref[...] = _rms(xmid_ref[...] + moe, gfin_ref[...])


def _combine_update_body(gate_ref, xmid_ref, gfin_ref, y0_ref, y1_ref, y2_ref, y3_ref, prev_ref, out_ref):
    del prev_ref
    _combine_body(gate_ref, xmid_ref, gfin_ref, y0_ref, y1_ref, y2_ref, y3_ref, out_ref)


def _combine(gates, xmid, tok0, n, gfin, y_rows, out_so_far=None):
    tt = min(n, COMBINE_ROWS)
    nblk = n // tt
    blk0 = tok0 // tt
    assert nblk * tt == n and blk0 * tt == tok0
    choice = lambda kx: pl.BlockSpec((tt, y_rows.shape[1]), lambda i: (i + kx * nblk, 0))
    in_specs = [pl.BlockSpec((tt, LANES), lambda i: (i + blk0, 0)),
                pl.BlockSpec((tt, D_MODEL), lambda i: (i + blk0, 0)),
                pl.BlockSpec((1, D_MODEL), lambda i: (0, 0))] + [choice(kx) for kx in range(TOP_K)]
    args = [gates, xmid, gfin, y_rows, y_rows, y_rows, y_rows]
    body, aliases = _combine_body, {}
    if out_so_far is not None:
        in_specs.append(pl.BlockSpec(memory_space=pl.ANY))
        args.append(out_so_far)
        body, aliases = _combine_update_body, {len(args) - 1: 0}
    return pl.pallas_call(
        body,
        grid=(nblk,),
        in_specs=in_specs,
        out_specs=pl.BlockSpec((tt, D_MODEL), lambda i: (i + blk0, 0)),
        out_shape=jax.ShapeDtypeStruct(xmid.shape, F32),
        input_output_aliases=aliases,
        compiler_params=_params(("arbitrary",)),
        name="combine",
    )(*args)


def kernel(x_prompt, x_sample, cache_k_win, cache_v_win, norm_attn_g, w_in, ln_v_g, ln_v_b, w_spatial, b_spatial,
           attn_sinks, w_out, norm_ffn_g, w_router, b_router, w_up, b_up, w_down, b_down, norm_final_g):
    bp, tp, _ = x_prompt.shape
    bs, ts, _ = x_sample.shape
    w_buf = cache_k_win.shape[2]
    assert bp == 1 and tp % MIX_ROWS == 0 and w_buf == WINDOW and (bs * ts) % PROJ_ROWS == 0 and 8 % ts == 0
    n_p, n_s = bp * tp, bs * ts
    row2 = lambda a: a.reshape(1, -1)

    w_in_bf = w_in[0].astype(BF16)
    w_out_bf = w_out[0].astype(BF16)
    tril = jnp.tril(jnp.ones((CHUNK, CHUNK), dtype=bool))
    wsp = jnp.where(tril[None], w_spatial[0], 0.0)
    wsp_bf = wsp.astype(BF16)
    bsp = jnp.broadcast_to(b_spatial[0][:, :, None], (GMLP_GROUPS, CHUNK, LANES))
    b_up_grouped = b_up[0].reshape(N_EXPERTS, -1, LANES, 2).transpose(0, 1, 3, 2).reshape(N_EXPERTS, 1, -1)
    bd = b_down[0][:, None, :]
    g1, g2, gfin = row2(norm_attn_g[0]), row2(norm_ffn_g[0]), row2(norm_final_g)
    lng, lnb = row2(ln_v_g[0]), row2(ln_v_b[0])
    wr_hi = w_router[0].astype(BF16)
    wr = jnp.concatenate([wr_hi, (w_router[0] - wr_hi.astype(F32)).astype(BF16)], axis=1)
    br = row2(b_router[0])
    sinks = attn_sinks[0]

    xp = x_prompt.reshape(n_p, D_MODEL)
    cs_p = _rotary_inputs(jnp.arange(tp, dtype=jnp.int32))
    q_p, k_p, v_p, a_p, vn_p, sgb_p = _proj(xp, g1, w_in_bf, cs_p, lng, lnb)
    xmid_p, xn2_p, lg_p = _mix(sinks, q_p, k_p, v_p, a_p, vn_p, sgb_p, xp, wsp_bf, bsp, w_out_bf, g2, wr, br)

    xs = x_sample.reshape(n_s, D_MODEL)
    pos_s = PAST_LEN + jnp.arange(ts, dtype=jnp.int32)
    cs_s = _rotary_inputs(jnp.tile(pos_s, bs))
    q_s, k_s, v_s, a_s, vn_s, sgb_s = _proj(xs, g1, w_in_bf, cs_s, lng, lnb)
    keys_minor = lambda t: t.reshape(bs, -1, KV_WIDTH).transpose(0, 2, 1)
    k_cache_t, v_cache_t = keys_minor(cache_k_win[0]), keys_minor(cache_v_win[0])
    pad_new = lambda t: jnp.pad(t.reshape(bs, ts, KV_WIDTH), ((0, 0), (0, (-ts) % 16), (0, 0)))
    sink_col = jnp.tile(sinks, ts).reshape(ts * N_HEADS, 1)
    o_s = _sample_attn(sink_col, q_s.reshape(bs, ts * N_HEADS, HEAD_DIM), k_cache_t, v_cache_t,
                       pad_new(k_s), pad_new(v_s)).reshape(n_s, Q_WIDTH)
    lag = np.arange(ts)[:, None] - np.arange(ts)[None, :]
    coef = jnp.stack([jnp.sum(jnp.where(lag == d, wsp[:, :ts, :ts], 0.0), axis=2)
                      for d in range(ts)])
    coef = jnp.repeat(coef.transpose(0, 2, 1), GMLP_WIDTH // GMLP_GROUPS, axis=2)
    coef = jnp.tile(coef, (1, 8 // ts, 1))
    bias = jnp.tile(jnp.repeat(b_spatial[0][:, :ts].T, GMLP_WIDTH // GMLP_GROUPS, axis=1), (8 // ts, 1))
    xmid_s, xn2_s, lg_s = _mix_sample(a_s, vn_s, sgb_s, o_s, xs, coef, bias, w_out_bf, g2, wr, br)

    tm = EXPERT_ROWS
    n_blocks = (n_p * TOP_K + N_EXPERTS * n_s) // tm + N_EXPERTS
    gate_p, dest_p, table_p = _route(lg_p, jnp.zeros((N_EXPERTS, LANES), F32), n_s)
    x_sorted = _sc_dispatch(xn2_p, dest_p, n_blocks * tm)
    gate_s, dest_s, table = _route(lg_s, table_p, None)
    x_sorted = _dispatch_small(xn2_s, dest_s, x_sorted)
    dest_t = jnp.concatenate([dest_p, dest_s], axis=1)
    meta = table.astype(jnp.int32)
    y_sorted = _experts(meta[:, 2], meta[:, 3], meta[:, 1], x_sorted, w_up[0], w_down[0], b_up_grouped, bd)
    half = n_p // 2
    yrows_a, yrows_s = _sc_collect(y_sorted, dest_t, 0, half, small=(n_p, n_s))
    yrows_b, = _sc_collect(y_sorted, dest_t, half, half)
    y_s = _combine(gate_s, xmid_s, 0, n_s, gfin, yrows_s)
    y_p = _combine(gate_p, xmid_p, 0, half, gfin, yrows_a)
    y_p = _combine(gate_p, xmid_p, half, half, gfin, yrows_b, out_so_far=y_p)

    k4 = lambda t: t.reshape(1, bp, -1, N_KV_HEADS, HEAD_DIM)
    next_window = lambda cache_t, new: _window(cache_t, keys_minor(new)).transpose(0, 2, 1).reshape(
        1, bs, w_buf, N_KV_HEADS, HEAD_DIM)
    return (y_p.reshape(bp, tp, D_MODEL),
            y_s.reshape(bs, ts, D_MODEL),
            k4(k_p[n_p - WINDOW:]),
            k4(v_p[n_p - WINDOW:]),
            vn_p[n_p - CHUNK:].reshape(1, bp, CHUNK, GMLP_WIDTH),
            next_window(k_cache_t, k_s),
            next_window(v_cache_t, v_s),
            vn_s.reshape(1, bs, ts, GMLP_WIDTH))
```

```python
import functools

import numpy as np
import jax
import jax.numpy as jnp
from jax import lax
from jax.experimental import pallas as pl
from jax.experimental.pallas import tpu as pltpu
from jax.experimental.pallas import tpu_sc as plsc

F32 = jnp.float32
BF16 = jnp.bfloat16

D_MODEL = 1024
HEAD_DIM = 64
N_HEADS = 16
GQA_GROUP = 8
N_KV_HEADS = 2
Q_WIDTH = 1024
KV_WIDTH = 128
WINDOW = 128
ROT_DIM = 16
ROPE_THETA = 500000.0
CHUNK = 128
GMLP_WIDTH = 1024
GMLP_GROUPS = 8
N_EXPERTS = 32
TOP_K = 4
SWIGLU_LIMIT = 7.0
SWIGLU_ALPHA = 1.702
RMS_EPS = 1e-5
LN_EPS = 1e-5
NEG_INF = -1e30
PAST_LEN = 16384

LANES = 128
VMEM_LIMIT = 56 * 1024 * 1024

PROJ_ROWS = 256
MIX_ROWS = 512
ROUTE_ROWS = 1024
EXPERT_ROWS = 256
W_PIECES = 8
COMBINE_ROWS = 1024

SC_CORES = 2
SC_WORKERS = 32
SC_ROWS = 64

_C_Q, _C_KV, _C_U, _C_VG, _C_GA, _C_GB, _C_END = 0, 1024, 1280, 2304, 3328, 4352, 5376


def _params(sem):
    return pltpu.CompilerParams(dimension_semantics=sem, vmem_limit_bytes=VMEM_LIMIT)


def _rms(x, g):
    return x * lax.rsqrt(jnp.mean(x * x, axis=-1, keepdims=True) + RMS_EPS) * g


def _pack_bf16_pair(lo, hi):
    lo_bits = lax.bitcast_convert_type(lo.astype(BF16).astype(F32), jnp.uint32)
    hi_bits = lax.bitcast_convert_type(hi.astype(BF16).astype(F32), jnp.uint32)
    return (lo_bits >> 16) | hi_bits


def _unpack_bf16_pair(words):
    lo = lax.bitcast_convert_type(words << 16, F32)
    hi = lax.bitcast_convert_type(words & jnp.uint32(0xFFFF0000), F32)
    return jnp.concatenate([lo, hi], axis=1)


def _proj_body(x_ref, g_ref, w_ref, cs_ref, rot_ref, lng_ref, lnb_ref,
               q_ref, k_ref, v_ref, a_ref, vn_ref, sgb_ref):
    h = _rms(x_ref[...], g_ref[...]).astype(BF16)
    tabs = lax.dot_general(cs_ref[...], rot_ref[...], (((0,), (0,)), ((), ())), preferred_element_type=F32)
    rc, rs1, rs2 = tabs[:, :LANES], tabs[:, LANES:2 * LANES], tabs[:, 2 * LANES:]

    def rot(z):
        return z * rc + pltpu.roll(z, LANES - ROT_DIM // 2, 1) * rs1 + pltpu.roll(z, ROT_DIM // 2, 1) * rs2

    def mm(lo, hi):
        return jnp.dot(h, w_ref[:, lo:hi], preferred_element_type=F32)

    zq = mm(_C_Q, _C_KV)
    for c in range(Q_WIDTH // LANES):
        sl = slice(c * LANES, (c + 1) * LANES)
        q_ref[:, sl] = (rot(zq[:, sl]) * (HEAD_DIM ** -0.5)).astype(BF16)
    zkv = mm(_C_KV, _C_U)
    k_ref[...] = rot(zkv[:, :KV_WIDTH])
    v_ref[...] = zkv[:, KV_WIDTH:]
    a_ref[...] = jax.nn.sigmoid(mm(_C_GA, _C_GB)) * jax.nn.gelu(mm(_C_U, _C_VG))
    zv = jax.nn.gelu(mm(_C_VG, _C_GA))
    zc = zv - jnp.mean(zv, axis=-1, keepdims=True)
    var = jnp.mean(zc * zc, axis=-1, keepdims=True)
    vn_ref[...] = zc * lax.rsqrt(var + LN_EPS) * lng_ref[...] + lnb_ref[...]
    sgb_ref[...] = jax.nn.sigmoid(mm(_C_GB, _C_END))


def _proj(x, norm_g, w_in_bf, cs, ln_g, ln_b):
    n = x.shape[0]
    tm = PROJ_ROWS
    row = lambda w: pl.BlockSpec((tm, w), lambda i: (i, 0))
    full = lambda a: pl.BlockSpec(a.shape, lambda i: (0,) * a.ndim)
    rot = jnp.asarray(np.tile(_ROT_EXPAND, (3, 1)), dtype=BF16)
    return pl.pallas_call(
        _proj_body,
        grid=(n // tm,),
        in_specs=[row(D_MODEL), full(norm_g), full(w_in_bf), pl.BlockSpec((cs.shape[0], tm), lambda i: (0, i)),
                  full(rot),
                  full(ln_g), full(ln_b)],
        out_specs=[row(Q_WIDTH), row(KV_WIDTH), row(KV_WIDTH), row(GMLP_WIDTH), row(GMLP_WIDTH), row(D_MODEL)],
        out_shape=[jax.ShapeDtypeStruct((n, Q_WIDTH), BF16),
                   jax.ShapeDtypeStruct((n, KV_WIDTH), F32),
                   jax.ShapeDtypeStruct((n, KV_WIDTH), F32),
                   jax.ShapeDtypeStruct((n, GMLP_WIDTH), F32),
                   jax.ShapeDtypeStruct((n, GMLP_WIDTH), F32),
                   jax.ShapeDtypeStruct((n, D_MODEL), F32)],
        compiler_params=_params(("arbitrary",)),
        name="proj",
    )(x, norm_g, w_in_bf, cs, rot, ln_g, ln_b)


_ROT_COLS = 32


def _rot_expand():
    half = ROT_DIM // 2
    m = np.zeros((_ROT_COLS, 3 * LANES), np.float32)
    for lane in range(LANES):
        d = lane % HEAD_DIM
        if d < ROT_DIM:
            m[d % half, lane] = 1.0
        else:
            m[2 * half, lane] = 1.0
        if d < half:
            m[half + d, LANES + lane] = -1.0
        elif d < ROT_DIM:
            m[half + d - half, 2 * LANES + lane] = 1.0
    return m


_ROT_EXPAND = _rot_expand()


def _rotary_inputs(pos):
    half = ROT_DIM // 2
    inv_freq = ROPE_THETA ** (-jnp.arange(half, dtype=F32) / half)
    ang = inv_freq[:, None] * pos.astype(F32)[None, :]
    n = pos.shape[0]
    cs = jnp.concatenate([jnp.cos(ang), jnp.sin(ang), jnp.ones((1, n), F32),
                          jnp.zeros((_ROT_COLS - 2 * half - 1, n), F32)], axis=0)
    hi = cs.astype(BF16)
    rest = cs - hi.astype(F32)
    mid = rest.astype(BF16)
    lo = (rest - mid.astype(F32)).astype(BF16)
    return jnp.concatenate([hi, mid, lo], axis=0)


def _finish_rows(merged_bf, x, wout_ref, g2_ref, wr_ref, br_ref, xmid_ref, xn2_ref, lg_ref):
    xm = x + jnp.dot(merged_bf, wout_ref[...], preferred_element_type=F32)
    xmid_ref[...] = xm
    xn = _rms(xm, g2_ref[...])
    x_hi = xn.astype(BF16)
    x_lo = (xn - x_hi.astype(F32)).astype(BF16)
    w_hl = wr_ref[...]
    p_hi = jnp.dot(x_hi, w_hl, preferred_element_type=F32)
    p_lo = jnp.dot(x_lo, w_hl[:, :N_EXPERTS], preferred_element_type=F32)
    lg = p_hi[:, :N_EXPERTS] + (p_hi[:, N_EXPERTS:] + p_lo) + br_ref[...]
    wide = jnp.concatenate([lg, jnp.zeros((lg.shape[0], LANES - N_EXPERTS), F32)], axis=1)
    lg_ref[...] = wide.T[:N_EXPERTS]
    xn2_ref[...] = _pack_bf16_pair(xn[:, :D_MODEL // 2], xn[:, D_MODEL // 2:])


def _mix_body(sinks_ref, q_ref, k_ref, kp_ref, v_ref, vp_ref, a_ref, vn_ref, sgb_ref, x_ref,
              wsp_ref, bsp_ref, wout_ref, g2_ref, wr_ref, br_ref,
              xmid_ref, xn2_ref, lg_ref, kcat, vcat, mrg, key_s):
    i = pl.program_id(0)
    nsub = MIX_ROWS // WINDOW
    kcat[0:WINDOW] = kp_ref[...]
    kcat[WINDOW:] = k_ref[...]
    vcat[0:WINDOW] = vp_ref[...]
    vcat[WINDOW:] = v_ref[...]

    pair_rows = (GQA_GROUP // 2) * WINDOW
    lane_kv = lax.broadcasted_iota(jnp.int32, (2 * WINDOW, LANES), 1)
    lane_o = lax.broadcasted_iota(jnp.int32, (pair_rows, LANES), 1)

    @pl.when(i == 0)
    def _():
        rq = lax.broadcasted_iota(jnp.int32, (pair_rows, 4 * WINDOW), 0) & (WINDOW - 1)
        ck = lax.broadcasted_iota(jnp.int32, (pair_rows, 4 * WINDOW), 1) & (2 * WINDOW - 1)
        key_s[...] = jnp.where((ck > rq) & (ck <= rq + WINDOW), ck, -1)

    row_p = lax.broadcasted_iota(jnp.int32, (pair_rows, 1), 0) >> 7
    sink_cols = []
    for kk in range(N_KV_HEADS):
        h0 = kk * GQA_GROUP
        se = jnp.full((pair_rows, 1), sinks_ref[h0], F32)
        so = jnp.full((pair_rows, 1), sinks_ref[h0 + 1], F32)
        for p in range(1, GQA_GROUP // 2):
            se = jnp.where(row_p == p, sinks_ref[h0 + 2 * p], se)
            so = jnp.where(row_p == p, sinks_ref[h0 + 2 * p + 1], so)
        sink_cols.append((se, so))

    def sub(j, carry):
        off = pl.multiple_of(j * WINDOW, WINDOW)
        rows = pl.ds(off, WINDOW)
        for g in range(GMLP_GROUPS):
            cols = slice(g * LANES, (g + 1) * LANES)
            s = jnp.dot(wsp_ref[g], vn_ref[rows, cols].astype(BF16), preferred_element_type=F32) + bsp_ref[g]
            mrg[rows, cols] = a_ref[rows, cols] * s
        kblk = kcat[pl.ds(off, 2 * WINDOW), :]
        vblk = vcat[pl.ds(off, 2 * WINDOW), :]
        kswp = pltpu.roll(kblk, HEAD_DIM, 1)
        vswp = pltpu.roll(vblk, HEAD_DIM, 1)
        kmin = jnp.where(jnp.logical_and(i == 0, j == 0), WINDOW, 0)
        allowed = key_s[...] >= kmin
        for kk in range(N_KV_HEADS):
            lo_src, hi_src = (kblk, kswp) if kk == 0 else (kswp, kblk)
            kbd = jnp.concatenate([jnp.where(lane_kv < HEAD_DIM, lo_src, 0.0),
                                   jnp.where(lane_kv >= HEAD_DIM, hi_src, 0.0)], axis=0).astype(BF16)
            lo_src, hi_src = (vblk, vswp) if kk == 0 else (vswp, vblk)
            vbd = jnp.concatenate([jnp.where(lane_kv < HEAD_DIM, lo_src, 0.0),
                                   jnp.where(lane_kv >= HEAD_DIM, hi_src, 0.0)], axis=0).astype(BF16)
            pair0 = kk * (GQA_GROUP // 2)
            qs = jnp.concatenate([q_ref[rows, (pair0 + p) * LANES:(pair0 + p + 1) * LANES]
                                  for p in range(GQA_GROUP // 2)], axis=0)
            lg = lax.dot_general(qs, kbd, (((1,), (1,)), ((), ())), preferred_element_type=F32)
            lg = jnp.where(allowed, lg, NEG_INF)
            se, so = sink_cols[kk]
            le, lo = lg[:, :2 * WINDOW], lg[:, 2 * WINDOW:]
            me = jnp.maximum(jnp.max(le, axis=1, keepdims=True), se)
            mo = jnp.maximum(jnp.max(lo, axis=1, keepdims=True), so)
            pe = jnp.exp(le - me)
            po = jnp.exp(lo - mo)
            de = jnp.sum(pe, axis=1, keepdims=True) + jnp.exp(se - me)
            do = jnp.sum(po, axis=1, keepdims=True) + jnp.exp(so - mo)
            pr = jnp.concatenate([pe, po], axis=1).astype(BF16)
            o = jnp.dot(pr, vbd, preferred_element_type=F32)
            o = o / jnp.where(lane_o < HEAD_DIM, de, do)
            for p in range(GQA_GROUP // 2):
                cols = slice((pair0 + p) * LANES, (pair0 + p + 1) * LANES)
                mrg[rows, cols] += sgb_ref[rows, cols] * o[p * WINDOW:(p + 1) * WINDOW]
        return carry

    lax.fori_loop(0, nsub, sub, 0)
    _finish_rows(mrg[...].astype(BF16), x_ref[...], wout_ref, g2_ref, wr_ref, br_ref, xmid_ref, xn2_ref, lg_ref)


def _mix(sinks, q, k, v, a, vn, sgb, x, wsp, bsp, wout, g2, wr, br):
    n = x.shape[0]
    tm = MIX_ROWS
    nsub = tm // WINDOW
    row = lambda w: pl.BlockSpec((tm, w), lambda i: (i, 0))
    prev = pl.BlockSpec((WINDOW, KV_WIDTH), lambda i: (jnp.maximum(i * nsub - 1, 0), 0))
    full = lambda arr: pl.BlockSpec(arr.shape, lambda i: (0,) * arr.ndim)
    smem = pl.BlockSpec(memory_space=pltpu.SMEM)
    return pl.pallas_call(
        _mix_body,
        grid=(n // tm,),
        in_specs=[smem, row(Q_WIDTH), row(KV_WIDTH), prev, row(KV_WIDTH), prev,
                  row(GMLP_WIDTH), row(GMLP_WIDTH), row(D_MODEL), row(D_MODEL),
                  full(wsp), full(bsp), full(wout), full(g2), full(wr), full(br)],
        out_specs=[row(D_MODEL), row(D_MODEL // 2), pl.BlockSpec((N_EXPERTS, tm), lambda i: (0, i))],
        out_shape=[jax.ShapeDtypeStruct((n, D_MODEL), F32),
                   jax.ShapeDtypeStruct((n, D_MODEL // 2), jnp.uint32),
                   jax.ShapeDtypeStruct((N_EXPERTS, n), F32)],
        scratch_shapes=[pltpu.VMEM((tm + WINDOW, KV_WIDTH), F32),
                        pltpu.VMEM((tm + WINDOW, KV_WIDTH), F32),
                        pltpu.VMEM((tm, D_MODEL), F32),
                        pltpu.VMEM((GQA_GROUP // 2 * WINDOW, 4 * WINDOW), jnp.int32)],
        compiler_params=_params(("arbitrary",)),
        name="mix_prompt",
    )(sinks, q, k, k, v, v, a, vn, sgb, x, wsp, bsp, wout, g2, wr, br)


def _sample_attn_body(sink_ref, q_ref, kc_ref, vc_ref, kn_ref, vn_ref, o_ref):
    q = q_ref[...]
    nq, nc, nn = q.shape[1], kc_ref.shape[2], kn_ref.shape[1]
    row_c = lax.broadcasted_iota(jnp.int32, (1, nq, nc), 1)
    row_n = lax.broadcasted_iota(jnp.int32, (1, nq, nn), 1)
    row_o = lax.broadcasted_iota(jnp.int32, (1, nq, HEAD_DIM), 1)
    first_kv = lambda row: (row % N_HEADS) < GQA_GROUP
    cached = lambda ref, kk: ref[:, kk * HEAD_DIM:(kk + 1) * HEAD_DIM, :].astype(BF16)
    fresh = lambda ref, kk: ref[:, :, kk * HEAD_DIM:(kk + 1) * HEAD_DIM].astype(BF16)
    lg_c = jnp.where(first_kv(row_c),
                     jnp.einsum("bqd,bdk->bqk", q, cached(kc_ref, 0), preferred_element_type=F32),
                     jnp.einsum("bqd,bdk->bqk", q, cached(kc_ref, 1), preferred_element_type=F32))
    lg_n = jnp.where(first_kv(row_n),
                     jnp.einsum("bqd,bkd->bqk", q, fresh(kn_ref, 0), preferred_element_type=F32),
                     jnp.einsum("bqd,bkd->bqk", q, fresh(kn_ref, 1), preferred_element_type=F32))
    j_c = lax.broadcasted_iota(jnp.int32, (1, nq, nc), 2)
    j_n = lax.broadcasted_iota(jnp.int32, (1, nq, nn), 2)
    lg_c = jnp.where(j_c > row_c // N_HEADS + (nc - WINDOW), lg_c, NEG_INF)
    lg_n = jnp.where(j_n <= row_n // N_HEADS, lg_n, NEG_INF)
    sink = sink_ref[...][None]
    m = jnp.maximum(jnp.maximum(jnp.max(lg_c, axis=2, keepdims=True), jnp.max(lg_n, axis=2, keepdims=True)), sink)
    p_c = jnp.exp(lg_c - m)
    p_n = jnp.exp(lg_n - m)
    den = jnp.sum(p_c, axis=2, keepdims=True) + jnp.sum(p_n, axis=2, keepdims=True) + jnp.exp(sink - m)
    pb_c, pb_n = p_c.astype(BF16), p_n.astype(BF16)
    heads_out = lambda kk: (jnp.einsum("bqk,bdk->bqd", pb_c, cached(vc_ref, kk), preferred_element_type=F32)
                            + jnp.einsum("bqk,bkd->bqd", pb_n, fresh(vn_ref, kk), preferred_element_type=F32))
    o_ref[...] = jnp.where(first_kv(row_o), heads_out(0), heads_out(1)) / den


def _sample_attn(sink_col, q3, k_cache_t, v_cache_t, k_new, v_new):
    nb = q3.shape[0]
    bb = 32
    blk = lambda a: pl.BlockSpec((bb,) + a.shape[1:], lambda b: (b, 0, 0))
    args = (q3, k_cache_t, v_cache_t, k_new, v_new)
    return pl.pallas_call(
        _sample_attn_body,
        grid=(nb // bb,),
        in_specs=[pl.BlockSpec(sink_col.shape, lambda b: (0, 0))] + [blk(a) for a in args],
        out_specs=blk(q3),
        out_shape=jax.ShapeDtypeStruct(q3.shape, F32),
        compiler_params=_params(("arbitrary",)),
        name="attn_sample",
    )(sink_col, *args)


def _window_body(c_ref, n_ref, o_ref):
    bb, _, w = c_ref.shape
    ts = n_ref.shape[1] // bb
    new = n_ref[...]
    lane = lax.broadcasted_iota(jnp.int32, new.shape, 1)
    for b in range(bb):
        old = pltpu.roll(c_ref[b], w - ts, 1)
        o_ref[b] = jnp.where(lane >= w - ts, pltpu.roll(new, (w - ts - b * ts) % w, 1), old)


def _window(cache_t, new_t):
    nb, ch, w = cache_t.shape
    ts = new_t.shape[1] // nb
    bb = LANES // ts
    assert w == LANES and bb * ts == LANES and nb % bb == 0
    return pl.pallas_call(
        _window_body,
        grid=(nb // bb,),
        in_specs=[pl.BlockSpec((bb, ch, w), lambda b: (b, 0, 0)), pl.BlockSpec((ch, LANES), lambda b: (0, b))],
        out_specs=pl.BlockSpec((bb, ch, w), lambda b: (b, 0, 0)),
        out_shape=jax.ShapeDtypeStruct(cache_t.shape, cache_t.dtype),
        compiler_params=_params(("arbitrary",)),
        name="window",
    )(cache_t, new_t)


def _mix_sample_body(a_ref, vn_ref, sgb_ref, o_ref, x_ref, coef_ref, bias_ref,
                     wout_ref, g2_ref, wr_ref, br_ref, xmid_ref, xn2_ref, lg_ref):
    vn = vn_ref[...]
    n, width = vn.shape
    rows8 = lambda t: t.reshape(n // 8, 8, width)
    s = bias_ref[...][None] + coef_ref[0][None] * rows8(vn)
    for d in range(1, coef_ref.shape[0]):
        s = s + coef_ref[d][None] * rows8(pltpu.roll(vn, d, 0))
    merged = a_ref[...] * s.reshape(n, width) + sgb_ref[...] * o_ref[...]
    _finish_rows(merged.astype(BF16), x_ref[...], wout_ref, g2_ref, wr_ref, br_ref, xmid_ref, xn2_ref, lg_ref)


def _mix_sample(a, vn, sgb, o, x, coef, bias, wout, g2, wr, br):
    n = x.shape[0]
    args = (a, vn, sgb, o, x, coef, bias, wout, g2, wr, br)
    full = lambda arr: pl.BlockSpec(arr.shape, lambda i: (0,) * arr.ndim)
    return pl.pallas_call(
        _mix_sample_body,
        grid=(1,),
        in_specs=[full(arr) for arr in args],
        out_specs=[pl.BlockSpec((n, D_MODEL), lambda i: (0, 0)), pl.BlockSpec((n, D_MODEL // 2), lambda i: (0, 0)),
                   pl.BlockSpec((N_EXPERTS, n), lambda i: (0, 0))],
        out_shape=[jax.ShapeDtypeStruct((n, D_MODEL), F32),
                   jax.ShapeDtypeStruct((n, D_MODEL // 2), jnp.uint32),
                   jax.ShapeDtypeStruct((N_EXPERTS, n), F32)],
        compiler_params=_params(("arbitrary",)),
        name="mix_sample",
    )(*args)


def _rows8(rows, dtype):
    n = rows[0].shape[1]
    sub = lax.broadcasted_iota(jnp.int32, (8, n), 0)
    out = jnp.zeros((8, n), dtype)
    for kx, r in enumerate(rows):
        out = jnp.where(sub == kx, r.astype(dtype), out)
    return out


def _route_body(nblk, reserve, lg_ref, prior_ref, gate_ref, dest_ref, meta_ref, idx_s, rank_s, base):
    i = pl.program_id(0)

    @pl.when(i == 0)
    def _():
        base[...] = jnp.zeros_like(base)

    l = lg_ref[...]
    tb = l.shape[1]
    sub = lax.broadcasted_iota(jnp.int32, l.shape, 0).astype(F32)
    vals, idxs, sels = [], [], []
    for _ in range(TOP_K):
        m = jnp.max(l, axis=0, keepdims=True)
        ik = jnp.min(jnp.where(l == m, sub, float(N_EXPERTS)), axis=0, keepdims=True)
        sel = sub == ik
        l = jnp.where(sel, -jnp.inf, l)
        vals.append(m)
        idxs.append(ik)
        sels.append(sel)
    es = [jnp.exp(vk - vals[0]) for vk in vals]
    den = es[0] + es[1] + es[2] + es[3]
    onehot = jnp.zeros(l.shape, F32)
    for sel in sels:
        onehot = onehot + sel.astype(F32)
    earlier = (lax.broadcasted_iota(jnp.int32, (tb, tb), 0) < lax.broadcasted_iota(jnp.int32, (tb, tb), 1))
    before = jnp.dot(onehot.astype(BF16), earlier.astype(BF16), preferred_element_type=F32) + base[...]
    ranks = [jnp.sum(jnp.where(sel, before, 0.0), axis=0, keepdims=True) for sel in sels]
    base[...] += jnp.sum(onehot, axis=1, keepdims=True)
    idx_s[i] = _rows8(idxs, F32)
    rank_s[i] = _rows8(ranks, F32)
    gates = jnp.concatenate([_rows8([e / den for e in es], F32), jnp.zeros((LANES - 8, tb), F32)], axis=0)
    gate_ref[...] = gates.T

    @pl.when(i == nblk - 1)
    def _():
        cnt = base[...]
        if reserve is None:
            seg0 = prior_ref[:, 0:1]
            placed = prior_ref[:, 1:2]
        else:
            cap = jnp.ceil((cnt + reserve) / EXPERT_ROWS) * EXPERT_ROWS
            lower = (lax.broadcasted_iota(jnp.int32, (N_EXPERTS, N_EXPERTS), 1) <
                     lax.broadcasted_iota(jnp.int32, (N_EXPERTS, N_EXPERTS), 0)).astype(F32)
            seg0 = jnp.dot(lower, jnp.broadcast_to(cap, (N_EXPERTS, LANES)), preferred_element_type=F32,
                           precision=lax.Precision.HIGHEST)[:, :1]
            placed = jnp.zeros_like(cnt)
        total = placed + cnt
        lane = lax.broadcasted_iota(jnp.int32, (N_EXPERTS, LANES), 1)
        meta_ref[...] = jnp.where(lane == 0, seg0, jnp.where(lane == 1, total, jnp.where(
            lane == 2, seg0 / EXPERT_ROWS, jnp.where(lane == 3, jnp.ceil(total / EXPERT_ROWS), 0.0))))
        first = seg0 + placed
        sub_e = lax.broadcasted_iota(jnp.int32, (N_EXPERTS, tb), 0).astype(F32)
        for b in range(nblk):
            idx, rank = idx_s[b], rank_s[b]
            rows = [jnp.sum(jnp.where(sub_e == idx[kx:kx + 1], first, 0.0), axis=0, keepdims=True)
                    + rank[kx:kx + 1] for kx in range(TOP_K)]
            dest_ref[:, b * tb:(b + 1) * tb] = _rows8(rows, jnp.int32)


def _route(logits, prior, reserve):
    n = logits.shape[1]
    tb = min(n, ROUTE_ROWS)
    nblk = n // tb
    assert nblk * tb == n
    table = pl.BlockSpec((N_EXPERTS, LANES), lambda i: (0, 0))
    return pl.pallas_call(
        functools.partial(_route_body, nblk, reserve),
        grid=(nblk,),
        in_specs=[pl.BlockSpec((N_EXPERTS, tb), lambda i: (0, i)), table],
        out_specs=[pl.BlockSpec((tb, LANES), lambda i: (i, 0)), pl.BlockSpec((8, n), lambda i: (0, 0)), table],
        out_shape=[jax.ShapeDtypeStruct((n, LANES), F32),
                   jax.ShapeDtypeStruct((8, n), jnp.int32),
                   jax.ShapeDtypeStruct((N_EXPERTS, LANES), F32)],
        scratch_shapes=[pltpu.VMEM((nblk, 8, tb), F32), pltpu.VMEM((nblk, 8, tb), F32),
                        pltpu.VMEM((N_EXPERTS, 1), F32)],
        compiler_params=_params(("arbitrary",)),
        name="route",
    )(logits, prior)


def _sc_mesh():
    return plsc.VectorSubcoreMesh(core_axis_name="c", subcore_axis_name="s")


def _sc_worker():
    return lax.axis_index("s") * SC_CORES + lax.axis_index("c")


def _sc_dispatch(x, dest_t, n_slots):
    chunk = SC_ROWS
    n = x.shape[0]
    per_w = n // (SC_WORKERS * chunk)
    assert per_w * SC_WORKERS * chunk == n and per_w % 2 == 0
    d3 = dest_t.reshape(dest_t.shape[0], n // chunk, chunk)
    width, dtype = x.shape[1], x.dtype

    @functools.partial(
        pl.kernel, mesh=_sc_mesh(),
        out_type=jax.ShapeDtypeStruct((n_slots, width), dtype),
        scratch_types=[pltpu.VMEM((TOP_K, per_w, chunk), jnp.int32),
                       pltpu.VMEM((2, chunk, width), dtype),
                       pltpu.SemaphoreType.DMA, pltpu.SemaphoreType.DMA],
        compiler_params=pltpu.CompilerParams(use_tc_tiling_on_sc=True),
        name="dispatch")
    def run(x_hbm, d_hbm, out_hbm, idx_v, rows_v, rsem, wsem):
        wid = _sc_worker()
        pltpu.sync_copy(d_hbm.at[pl.ds(0, TOP_K), pl.ds(wid * per_w, per_w)], idx_v)

        def read(j, slot):
            return pltpu.make_async_copy(x_hbm.at[pl.ds((wid * per_w + j) * chunk, chunk)], rows_v.at[slot], rsem)

        def scatter(j, slot):
            copies = [pltpu.async_copy(rows_v.at[slot], out_hbm.at[idx_v.at[kx, j]], wsem) for kx in range(TOP_K)]
            for cp in copies:
                cp.wait()

        read(0, 0).start()

        def body(h, carry):
            j = 2 * h
            read(j, 0).wait()
            read(j + 1, 1).start()
            scatter(j, 0)
            read(j + 1, 1).wait()

            @pl.when(j + 2 < per_w)
            def _():
                read(j + 2, 0).start()

            scatter(j + 1, 1)
            return carry

        lax.fori_loop(0, per_w // 2, body, 0)

    return run(x, d3)


def _dispatch_small_body(dest_ref, x_ref, _, out_hbm, sem):
    n = x_ref.shape[0]

    def body(t, carry):
        for kx in range(TOP_K):
            pltpu.make_async_copy(x_ref.at[pl.ds(t, 1)], out_hbm.at[pl.ds(dest_ref[kx, t], 1)], sem).start()
        return carry

    lax.fori_loop(0, n, body, 0, unroll=4)
    for _ in range(TOP_K):
        pltpu.make_async_copy(x_ref, out_hbm.at[pl.ds(0, n)], sem).wait()


def _dispatch_small(x, dest_t, x_sorted):
    return pl.pallas_call(
        _dispatch_small_body,
        grid=(1,),
        in_specs=[pl.BlockSpec(memory_space=pltpu.SMEM), pl.BlockSpec(x.shape, lambda i: (0, 0)),
                  pl.BlockSpec(memory_space=pl.ANY)],
        out_specs=pl.BlockSpec(memory_space=pl.ANY),
        out_shape=jax.ShapeDtypeStruct(x_sorted.shape, x_sorted.dtype),
        scratch_shapes=[pltpu.SemaphoreType.DMA],
        input_output_aliases={2: 0},
        compiler_params=_params(("arbitrary",)),
        name="dispatch_small",
    )(dest_t, x, x_sorted)


def _sc_collect(y_sorted, dest_t, tok0, n, small=None):
    chunk = SC_ROWS
    per_choice = SC_WORKERS // TOP_K
    per_w = n // (per_choice * chunk)
    assert per_w * per_choice * chunk == n and per_w % 2 == 0 and tok0 % chunk == 0
    d3 = dest_t.reshape(dest_t.shape[0], dest_t.shape[1] // chunk, chunk)
    width, dtype = y_sorted.shape[1], y_sorted.dtype
    out_type = [jax.ShapeDtypeStruct((TOP_K * n, width), dtype)]
    if small is not None:
        small0, n_small = small
        assert n_small == per_choice * chunk and small0 % chunk == 0
        out_type.append(jax.ShapeDtypeStruct((TOP_K * n_small, width), dtype))

    @functools.partial(
        pl.kernel, mesh=_sc_mesh(),
        out_type=out_type,
        scratch_types=[pltpu.VMEM((per_w, chunk), jnp.int32),
                       pltpu.VMEM((1, chunk), jnp.int32),
                       pltpu.VMEM((2, chunk, width), dtype),
                       pltpu.SemaphoreType.DMA, pltpu.SemaphoreType.DMA],
        compiler_params=pltpu.CompilerParams(use_tc_tiling_on_sc=True),
        name="collect")
    def run(y_hbm, d_hbm, *rest):
        outs, (ip_v, is_v, rows_v, gsem, wsem) = rest[:len(out_type)], rest[len(out_type):]
        op_hbm = outs[0]
        wid = _sc_worker()
        choice = wid // per_choice
        part = wid % per_choice
        pltpu.sync_copy(d_hbm.at[choice, pl.ds(tok0 // chunk + part * per_w, per_w)], ip_v)

        def gather(idx_v, j, slot):
            return pltpu.make_async_copy(y_hbm.at[idx_v.at[j]], rows_v.at[slot], gsem)

        def write(j, slot):
            return pltpu.make_async_copy(rows_v.at[slot], op_hbm.at[pl.ds((wid * per_w + j) * chunk, chunk)], wsem)

        gather(ip_v, 0, 0).start()

        def body(h, carry):
            j = 2 * h
            gather(ip_v, j, 0).wait()

            @pl.when(h > 0)
            def _():
                write(j - 1, 1).wait()

            gather(ip_v, j + 1, 1).start()
            write(j, 0).start()
            gather(ip_v, j + 1, 1).wait()
            write(j, 0).wait()

            @pl.when(j + 2 < per_w)
            def _():
                gather(ip_v, j + 2, 0).start()

            write(j + 1, 1).start()
            return carry

        lax.fori_loop(0, per_w // 2, body, 0)
        write(per_w - 1, 1).wait()

        if small is not None:
            pltpu.sync_copy(d_hbm.at[choice, pl.ds(small0 // chunk + part, 1)], is_v)
            gather(is_v, 0, 0).start()
            gather(is_v, 0, 0).wait()
            pltpu.sync_copy(rows_v.at[0], outs[1].at[pl.ds(wid * chunk, chunk)])

    return run(y_sorted, d3)


def _expert_body(blk0_ref, nblk_ref, cnt_ref, wup_hbm, wdn_hbm, bup_ref, bdn_ref, x_hbm, y_hbm,
                 wup_f, wdn_f, wup_s, wdn_s, xbuf, obuf, w_sem, in_sem, out_sem):
    e = pl.program_id(0)
    nb = nblk_ref[e]
    blk0 = blk0_ref[e]
    cnt = cnt_ref[e]
    tm = EXPERT_ROWS
    pair = 2 * LANES
    wslot = e % 2
    up_rows = D_MODEL // W_PIECES
    dn_rows = D_MODEL // (W_PIECES // 2)

    def w_piece(hbm, buf, ex, slot, p, rows):
        start = p * rows if isinstance(p, int) else pl.multiple_of(p * rows, rows)
        r = pl.ds(start, rows)
        return pltpu.make_async_copy(hbm.at[ex, r], buf.at[slot, r], w_sem.at[slot])

    def w_start(ex, slot, p):
        w_piece(wup_hbm, wup_f, ex, slot, p, up_rows).start()
        if isinstance(p, int):
            if p < W_PIECES // 2:
                w_piece(wdn_hbm, wdn_f, ex, slot, p, dn_rows).start()
        else:
            @pl.when(p < W_PIECES // 2)
            def _():
                w_piece(wdn_hbm, wdn_f, ex, slot, p, dn_rows).start()

    def w_wait(ex, slot):
        for p in range(W_PIECES):
            w_piece(wup_hbm, wup_f, ex, slot, p, up_rows).wait()
        for p in range(W_PIECES // 2):
            w_piece(wdn_hbm, wdn_f, ex, slot, p, dn_rows).wait()

    @pl.when(e == 0)
    def _():
        for p in range(W_PIECES):
            w_start(0, 0, p)

    w_wait(e, wslot)
    wup_ref = wup_f.at[wslot]
    wdn_ref = wdn_f.at[wslot]
    more = e + 1 < N_EXPERTS

    def x_copy(i, slot):
        rows = pl.ds(pl.multiple_of((blk0 + i) * tm, tm), tm)
        return pltpu.make_async_copy(x_hbm.at[rows], xbuf.at[slot], in_sem.at[slot])

    def y_copy(i, slot):
        rows = pl.ds(pl.multiple_of((blk0 + i) * tm, tm), tm)
        return pltpu.make_async_copy(obuf.at[slot], y_hbm.at[rows], out_sem.at[slot])

    @pl.when(nb > 0)
    def _():
        x_copy(0, 0).start(priority=1)
        r = lax.broadcasted_iota(jnp.int32, (pair, pair), 0)
        c = lax.broadcasted_iota(jnp.int32, (pair, pair), 1)
        perm = (r == jnp.where(c < LANES, 2 * c, 2 * (c - LANES) + 1)).astype(BF16)
        for g in range(2 * D_MODEL // pair):
            cols = slice(g * pair, (g + 1) * pair)
            wup_s[g] = jnp.dot(wup_ref[:, cols].astype(BF16), perm, preferred_element_type=F32).astype(BF16)
        for g in range(D_MODEL // pair):
            wdn_s[g] = wdn_ref[:, g * pair:(g + 1) * pair].astype(BF16)

        def block(i, carry):
            slot = i % 2
            x_copy(i, slot).wait()

            @pl.when(i + 1 < nb)
            def _():
                x_copy(i + 1, 1 - slot).start(priority=1)

            @pl.when(i >= 2)
            def _():
                y_copy(i - 2, slot).wait()

            @pl.when(jnp.logical_and(more, i < W_PIECES))
            def _():
                w_start(e + 1, 1 - wslot, i)

            row = lax.broadcasted_iota(jnp.int32, (tm, 1), 0)
            x = jnp.where(row < cnt - i * tm, _unpack_bf16_pair(xbuf[slot]), 0.0).astype(BF16)
            acts = []
            for g in range(2 * D_MODEL // pair):
                cols = slice(g * pair, (g + 1) * pair)
                h = jnp.dot(x, wup_s[g], preferred_element_type=F32) + bup_ref[0, :, cols]
                glu = jnp.minimum(h[:, :LANES], SWIGLU_LIMIT)
                lin = jnp.clip(h[:, LANES:], -SWIGLU_LIMIT, SWIGLU_LIMIT)
                acts.append((glu * jax.nn.sigmoid(SWIGLU_ALPHA * glu) * (lin + 1.0)).astype(BF16))
            act = jnp.concatenate(acts, axis=1)
            half_groups = D_MODEL // pair // 2
            for g in range(half_groups):
                ys = []
                for gg in (g, g + half_groups):
                    cols = slice(gg * pair, (gg + 1) * pair)
                    ys.append(jnp.dot(act, wdn_s[gg], preferred_element_type=F32) + bdn_ref[0, :, cols])
                obuf[slot, :, g * pair:(g + 1) * pair] = _pack_bf16_pair(ys[0], ys[1])
            y_copy(i, slot).start(priority=1)
            return carry

        lax.fori_loop(0, nb, block, 0)

        @pl.when(nb >= 2)
        def _():
            y_copy(nb - 2, nb % 2).wait()

        y_copy(nb - 1, (nb - 1) % 2).wait()

    for p in range(W_PIECES):
        @pl.when(jnp.logical_and(more, p >= nb))
        def _():
            w_start(e + 1, 1 - wslot, p)


def _experts(blk0, nblk, cnt, x_sorted, w_up, w_down, b_up_grouped, b_down):
    tm = EXPERT_ROWS
    per_expert = lambda a: pl.BlockSpec((1,) + a.shape[1:], lambda e, b0, nb, ct: (e, 0, 0))
    hbm = pl.BlockSpec(memory_space=pl.ANY)
    grid_spec = pltpu.PrefetchScalarGridSpec(
        num_scalar_prefetch=3,
        grid=(N_EXPERTS,),
        in_specs=[hbm, hbm, per_expert(b_up_grouped), per_expert(b_down), hbm],
        out_specs=hbm,
        scratch_shapes=[pltpu.VMEM((2,) + w_up.shape[1:], F32), pltpu.VMEM((2,) + w_down.shape[1:], F32),
                        pltpu.VMEM((2 * D_MODEL // (2 * LANES), D_MODEL, 2 * LANES), BF16),
                        pltpu.VMEM((D_MODEL // (2 * LANES), D_MODEL, 2 * LANES), BF16),
                        pltpu.VMEM((2, tm, x_sorted.shape[1]), x_sorted.dtype),
                        pltpu.VMEM((2, tm, D_MODEL // 2), jnp.uint32),
                        pltpu.SemaphoreType.DMA((2,)), pltpu.SemaphoreType.DMA((2,)), pltpu.SemaphoreType.DMA((2,))],
    )
    return pl.pallas_call(
        _expert_body,
        grid_spec=grid_spec,
        out_shape=jax.ShapeDtypeStruct((x_sorted.shape[0], D_MODEL // 2), jnp.uint32),
        compiler_params=_params(("arbitrary",)),
        name="experts",
    )(blk0, nblk, cnt, w_up, w_down, b_up_grouped, b_down, x_sorted)


def _combine_body(gate_ref, xmid_ref, gfin_ref, y0_ref, y1_ref, y2_ref, y3_ref, out_ref):
    gate = gate_ref[...]
    moe = _unpack_bf16_pair(y0_ref[...]) * gate[:, 0:1]
    for kx, y_ref in enumerate((y1_ref, y2_ref, y3_ref), start=1):
        moe = moe + _unpack_bf16_pair(y_ref[...]) * gate[:, kx:kx + 1]
    out_ref[...] = _rms(xmid_ref[...] + moe, gfin_ref[...])


def _combine_update_body(gate_ref, xmid_ref, gfin_ref, y0_ref, y1_ref, y2_ref, y3_ref, prev_ref, out_ref):
    del prev_ref
    _combine_body(gate_ref, xmid_ref, gfin_ref, y0_ref, y1_ref, y2_ref, y3_ref, out_ref)


def _combine(gates, xmid, tok0, n, gfin, y_rows, out_so_far=None):
    tt = min(n, COMBINE_ROWS)
    nblk = n // tt
    blk0 = tok0 // tt
    assert nblk * tt == n and blk0 * tt == tok0
    choice = lambda kx: pl.BlockSpec((tt, y_rows.shape[1]), lambda i: (i + kx * nblk, 0))
    in_specs = [pl.BlockSpec((tt, LANES), lambda i: (i + blk0, 0)),
                pl.BlockSpec((tt, D_MODEL), lambda i: (i + blk0, 0)),
                pl.BlockSpec((1, D_MODEL), lambda i: (0, 0))] + [choice(kx) for kx in range(TOP_K)]
    args = [gates, xmid, gfin, y_rows, y_rows, y_rows, y_rows]
    body, aliases = _combine_body, {}
    if out_so_far is not None:
        in_specs.append(pl.BlockSpec(memory_space=pl.ANY))
        args.append(out_so_far)
        body, aliases = _combine_update_body, {len(args) - 1: 0}
    return pl.pallas_call(
        body,
        grid=(nblk,),
        in_specs=in_specs,
        out_specs=pl.BlockSpec((tt, D_MODEL), lambda i: (i + blk0, 0)),
        out_shape=jax.ShapeDtypeStruct(xmid.shape, F32),
        input_output_aliases=aliases,
        compiler_params=_params(("arbitrary",)),
        name="combine",
    )(*args)


def kernel(x_prompt, x_sample, cache_k_win, cache_v_win, norm_attn_g, w_in, ln_v_g, ln_v_b, w_spatial, b_spatial,
           attn_sinks, w_out, norm_ffn_g, w_router, b_router, w_up, b_up, w_down, b_down, norm_final_g):
    bp, tp, _ = x_prompt.shape
    bs, ts, _ = x_sample.shape
    w_buf = cache_k_win.shape[2]
    assert bp == 1 and tp % MIX_ROWS == 0 and w_buf == WINDOW and (bs * ts) % PROJ_ROWS == 0 and 8 % ts == 0
    n_p, n_s = bp * tp, bs * ts
    row2 = lambda a: a.reshape(1, -1)

    w_in_bf = w_in[0].astype(BF16)
    w_out_bf = w_out[0].astype(BF16)
    tril = jnp.tril(jnp.ones((CHUNK, CHUNK), dtype=bool))
    wsp = jnp.where(tril[None], w_spatial[0], 0.0)
    wsp_bf = wsp.astype(BF16)
    bsp = jnp.broadcast_to(b_spatial[0][:, :, None], (GMLP_GROUPS, CHUNK, LANES))
    b_up_grouped = b_up[0].reshape(N_EXPERTS, -1, LANES, 2).transpose(0, 1, 3, 2).reshape(N_EXPERTS, 1, -1)
    bd = b_down[0][:, None, :]
    g1, g2, gfin = row2(norm_attn_g[0]), row2(norm_ffn_g[0]), row2(norm_final_g)
    lng, lnb = row2(ln_v_g[0]), row2(ln_v_b[0])
    wr_hi = w_router[0].astype(BF16)
    wr = jnp.concatenate([wr_hi, (w_router[0] - wr_hi.astype(F32)).astype(BF16)], axis=1)
    br = row2(b_router[0])
    sinks = attn_sinks[0]

    xp = x_prompt.reshape(n_p, D_MODEL)
    cs_p = _rotary_inputs(jnp.arange(tp, dtype=jnp.int32))
    q_p, k_p, v_p, a_p, vn_p, sgb_p = _proj(xp, g1, w_in_bf, cs_p, lng, lnb)
    xmid_p, xn2_p, lg_p = _mix(sinks, q_p, k_p, v_p, a_p, vn_p, sgb_p, xp, wsp_bf, bsp, w_out_bf, g2, wr, br)

    xs = x_sample.reshape(n_s, D_MODEL)
    pos_s = PAST_LEN + jnp.arange(ts, dtype=jnp.int32)
    cs_s = _rotary_inputs(jnp.tile(pos_s, bs))
    q_s, k_s, v_s, a_s, vn_s, sgb_s = _proj(xs, g1, w_in_bf, cs_s, lng, lnb)
    keys_minor = lambda t: t.reshape(bs, -1, KV_WIDTH).transpose(0, 2, 1)
    k_cache_t, v_cache_t = keys_minor(cache_k_win[0]), keys_minor(cache_v_win[0])
    pad_new = lambda t: jnp.pad(t.reshape(bs, ts, KV_WIDTH), ((0, 0), (0, (-ts) % 16), (0, 0)))
    sink_col = jnp.tile(sinks, ts).reshape(ts * N_HEADS, 1)
    o_s = _sample_attn(sink_col, q_s.reshape(bs, ts * N_HEADS, HEAD_DIM), k_cache_t, v_cache_t,
                       pad_new(k_s), pad_new(v_s)).reshape(n_s, Q_WIDTH)
    lag = np.arange(ts)[:, None] - np.arange(ts)[None, :]
    coef = jnp.stack([jnp.sum(jnp.where(lag == d, wsp[:, :ts, :ts], 0.0), axis=2)
                      for d in range(ts)])
    coef = jnp.repeat(coef.transpose(0, 2, 1), GMLP_WIDTH // GMLP_GROUPS, axis=2)
    coef = jnp.tile(coef, (1, 8 // ts, 1))
    bias = jnp.tile(jnp.repeat(b_spatial[0][:, :ts].T, GMLP_WIDTH // GMLP_GROUPS, axis=1), (8 // ts, 1))
    xmid_s, xn2_s, lg_s = _mix_sample(a_s, vn_s, sgb_s, o_s, xs, coef, bias, w_out_bf, g2, wr, br)

    tm = EXPERT_ROWS
    n_blocks = (n_p * TOP_K + N_EXPERTS * n_s) // tm + N_EXPERTS
    gate_p, dest_p, table_p = _route(lg_p, jnp.zeros((N_EXPERTS, LANES), F32), n_s)
    x_sorted = _sc_dispatch(xn2_p, dest_p, n_blocks * tm)
    gate_s, dest_s, table = _route(lg_s, table_p, None)
    x_sorted = _dispatch_small(xn2_s, dest_s, x_sorted)
    dest_t = jnp.concatenate([dest_p, dest_s], axis=1)
    meta = table.astype(jnp.int32)
    y_sorted = _experts(meta[:, 2], meta[:, 3], meta[:, 1], x_sorted, w_up[0], w_down[0], b_up_grouped, bd)
    half = n_p // 2
    yrows_a, yrows_s = _sc_collect(y_sorted, dest_t, 0, half, small=(n_p, n_s))
    yrows_b, = _sc_collect(y_sorted, dest_t, half, half)
    y_s = _combine(gate_s, xmid_s, 0, n_s, gfin, yrows_s)
    y_p = _combine(gate_p, xmid_p, 0, half, gfin, yrows_a)
    y_p = _combine(gate_p, xmid_p, half, half, gfin, yrows_b, out_so_far=y_p)

    k4 = lambda t: t.reshape(1, bp, -1, N_KV_HEADS, HEAD_DIM)
    next_window = lambda cache_t, new: _window(cache_t, new.T).transpose(0, 2, 1).reshape(
        1, bs, w_buf, N_KV_HEADS, HEAD_DIM)
    return (y_p.reshape(bp, tp, D_MODEL),
            y_s.reshape(bs, ts, D_MODEL),
            k4(k_p[n_p - WINDOW:]),
            k4(v_p[n_p - WINDOW:]),
            vn_p[n_p - CHUNK:].reshape(1, bp, CHUNK, GMLP_WIDTH),
            next_window(k_cache_t, k_s),
            next_window(v_cache_t, v_s),
            vn_s.reshape(1, bs, ts, GMLP_WIDTH))
```

```python
import functools

import numpy as np
import jax
import jax.numpy as jnp
from jax import lax
from jax.experimental import pallas as pl
from jax.experimental.pallas import tpu as pltpu
from jax.experimental.pallas import tpu_sc as plsc

F32 = jnp.float32
BF16 = jnp.bfloat16

D_MODEL = 1024
HEAD_DIM = 64
N_HEADS = 16
GQA_GROUP = 8
N_KV_HEADS = 2
Q_WIDTH = 1024
KV_WIDTH = 128
WINDOW = 128
ROT_DIM = 16
ROPE_THETA = 500000.0
CHUNK = 128
GMLP_WIDTH = 1024
GMLP_GROUPS = 8
N_EXPERTS = 32
TOP_K = 4
SWIGLU_LIMIT = 7.0
SWIGLU_ALPHA = 1.702
RMS_EPS = 1e-5
LN_EPS = 1e-5
NEG_INF = -1e30
PAST_LEN = 16384

LANES = 128
VMEM_LIMIT = 56 * 1024 * 1024

PROJ_ROWS = 256
MIX_ROWS = 512
ROUTE_ROWS = 1024
EXPERT_ROWS = 256
W_PIECES = 8
COMBINE_ROWS = 1024

SC_CORES = 2
SC_WORKERS = 32
SC_ROWS = 64

_C_Q, _C_KV, _C_U, _C_VG, _C_GA, _C_GB, _C_END = 0, 1024, 1280, 2304, 3328, 4352, 5376


def _params(sem):
    return pltpu.CompilerParams(dimension_semantics=sem, vmem_limit_bytes=VMEM_LIMIT)


def _rms(x, g):
    return x * lax.rsqrt(jnp.mean(x * x, axis=-1, keepdims=True) + RMS_EPS) * g


def _pack_bf16_pair(lo, hi):
    lo_bits = lax.bitcast_convert_type(lo.astype(BF16).astype(F32), jnp.uint32)
    hi_bits = lax.bitcast_convert_type(hi.astype(BF16).astype(F32), jnp.uint32)
    return (lo_bits >> 16) | hi_bits


def _unpack_bf16_pair(words):
    lo = lax.bitcast_convert_type(words << 16, F32)
    hi = lax.bitcast_convert_type(words & jnp.uint32(0xFFFF0000), F32)
    return jnp.concatenate([lo, hi], axis=1)


def _proj_body(x_ref, g_ref, w_ref, cs_ref, rot_ref, lng_ref, lnb_ref,
               q_ref, k_ref, v_ref, a_ref, vn_ref, sgb_ref):
    h = _rms(x_ref[...], g_ref[...]).astype(BF16)
    tabs = lax.dot_general(cs_ref[...], rot_ref[...], (((0,), (0,)), ((), ())), preferred_element_type=F32)
    rc, rs1, rs2 = tabs[:, :LANES], tabs[:, LANES:2 * LANES], tabs[:, 2 * LANES:]

    def rot(z):
        return z * rc + pltpu.roll(z, LANES - ROT_DIM // 2, 1) * rs1 + pltpu.roll(z, ROT_DIM // 2, 1) * rs2

    def mm(lo, hi):
        return jnp.dot(h, w_ref[:, lo:hi], preferred_element_type=F32)

    zq = mm(_C_Q, _C_KV)
    for c in range(Q_WIDTH // LANES):
        sl = slice(c * LANES, (c + 1) * LANES)
        q_ref[:, sl] = (rot(zq[:, sl]) * (HEAD_DIM ** -0.5)).astype(BF16)
    zkv = mm(_C_KV, _C_U)
    k_ref[...] = rot(zkv[:, :KV_WIDTH])
    v_ref[...] = zkv[:, KV_WIDTH:]
    a_ref[...] = jax.nn.sigmoid(mm(_C_GA, _C_GB)) * jax.nn.gelu(mm(_C_U, _C_VG))
    zv = jax.nn.gelu(mm(_C_VG, _C_GA))
    zc = zv - jnp.mean(zv, axis=-1, keepdims=True)
    var = jnp.mean(zc * zc, axis=-1, keepdims=True)
    vn_ref[...] = zc * lax.rsqrt(var + LN_EPS) * lng_ref[...] + lnb_ref[...]
    sgb_ref[...] = jax.nn.sigmoid(mm(_C_GB, _C_END))


def _proj(x, norm_g, w_in_bf, cs, ln_g, ln_b):
    n = x.shape[0]
    tm = PROJ_ROWS
    row = lambda w: pl.BlockSpec((tm, w), lambda i: (i, 0))
    full = lambda a: pl.BlockSpec(a.shape, lambda i: (0,) * a.ndim)
    rot = jnp.asarray(np.tile(_ROT_EXPAND, (3, 1)), dtype=BF16)
    return pl.pallas_call(
        _proj_body,
        grid=(n // tm,),
        in_specs=[row(D_MODEL), full(norm_g), full(w_in_bf), pl.BlockSpec((cs.shape[0], tm), lambda i: (0, i)),
                  full(rot),
                  full(ln_g), full(ln_b)],
        out_specs=[row(Q_WIDTH), row(KV_WIDTH), row(KV_WIDTH), row(GMLP_WIDTH), row(GMLP_WIDTH), row(D_MODEL)],
        out_shape=[jax.ShapeDtypeStruct((n, Q_WIDTH), BF16),
                   jax.ShapeDtypeStruct((n, KV_WIDTH), F32),
                   jax.ShapeDtypeStruct((n, KV_WIDTH), F32),
                   jax.ShapeDtypeStruct((n, GMLP_WIDTH), F32),
                   jax.ShapeDtypeStruct((n, GMLP_WIDTH), F32),
                   jax.ShapeDtypeStruct((n, D_MODEL), F32)],
        compiler_params=_params(("arbitrary",)),
        name="proj",
    )(x, norm_g, w_in_bf, cs, rot, ln_g, ln_b)


_ROT_COLS = 32


def _rot_expand():
    half = ROT_DIM // 2
    m = np.zeros((_ROT_COLS, 3 * LANES), np.float32)
    for lane in range(LANES):
        d = lane % HEAD_DIM
        if d < ROT_DIM:
            m[d % half, lane] = 1.0
        else:
            m[2 * half, lane] = 1.0
        if d < half:
            m[half + d, LANES + lane] = -1.0
        elif d < ROT_DIM:
            m[half + d - half, 2 * LANES + lane] = 1.0
    return m


_ROT_EXPAND = _rot_expand()


def _rotary_inputs(pos):
    half = ROT_DIM // 2
    inv_freq = ROPE_THETA ** (-jnp.arange(half, dtype=F32) / half)
    ang = inv_freq[:, None] * pos.astype(F32)[None, :]
    n = pos.shape[0]
    cs = jnp.concatenate([jnp.cos(ang), jnp.sin(ang), jnp.ones((1, n), F32),
                          jnp.zeros((_ROT_COLS - 2 * half - 1, n), F32)], axis=0)
    hi = cs.astype(BF16)
    rest = cs - hi.astype(F32)
    mid = rest.astype(BF16)
    lo = (rest - mid.astype(F32)).astype(BF16)
    return jnp.concatenate([hi, mid, lo], axis=0)


def _finish_rows(merged_bf, x, wout_ref, g2_ref, wr_ref, br_ref, xmid_ref, xn2_ref, lg_ref):
    xm = x + jnp.dot(merged_bf, wout_ref[...], preferred_element_type=F32)
    xmid_ref[...] = xm
    xn = _rms(xm, g2_ref[...])
    x_hi = xn.astype(BF16)
    x_lo = (xn - x_hi.astype(F32)).astype(BF16)
    w_hl = wr_ref[...]
    p_hi = jnp.dot(x_hi, w_hl, preferred_element_type=F32)
    p_lo = jnp.dot(x_lo, w_hl[:, :N_EXPERTS], preferred_element_type=F32)
    lg = p_hi[:, :N_EXPERTS] + (p_hi[:, N_EXPERTS:] + p_lo) + br_ref[...]
    wide = jnp.concatenate([lg, jnp.zeros((lg.shape[0], LANES - N_EXPERTS), F32)], axis=1)
    lg_ref[...] = wide.T[:N_EXPERTS]
    xn2_ref[...] = _pack_bf16_pair(xn[:, :D_MODEL // 2], xn[:, D_MODEL // 2:])


def _mix_body(sinks_ref, q_ref, k_ref, kp_ref, v_ref, vp_ref, a_ref, vn_ref, sgb_ref, x_ref,
              wsp_ref, bsp_ref, wout_ref, g2_ref, wr_ref, br_ref,
              xmid_ref, xn2_ref, lg_ref, kcat, vcat, mrg, key_s):
    i = pl.program_id(0)
    nsub = MIX_ROWS // WINDOW
    kcat[0:WINDOW] = kp_ref[...]
    kcat[WINDOW:] = k_ref[...]
    vcat[0:WINDOW] = vp_ref[...]
    vcat[WINDOW:] = v_ref[...]

    pair_rows = (GQA_GROUP // 2) * WINDOW
    lane_kv = lax.broadcasted_iota(jnp.int32, (2 * WINDOW, LANES), 1)
    lane_o = lax.broadcasted_iota(jnp.int32, (pair_rows, LANES), 1)

    @pl.when(i == 0)
    def _():
        rq = lax.broadcasted_iota(jnp.int32, (pair_rows, 4 * WINDOW), 0) & (WINDOW - 1)
        ck = lax.broadcasted_iota(jnp.int32, (pair_rows, 4 * WINDOW), 1) & (2 * WINDOW - 1)
        key_s[...] = jnp.where((ck > rq) & (ck <= rq + WINDOW), ck, -1)

    row_p = lax.broadcasted_iota(jnp.int32, (pair_rows, 1), 0) >> 7
    sink_cols = []
    for kk in range(N_KV_HEADS):
        h0 = kk * GQA_GROUP
        se = jnp.full((pair_rows, 1), sinks_ref[h0], F32)
        so = jnp.full((pair_rows, 1), sinks_ref[h0 + 1], F32)
        for p in range(1, GQA_GROUP // 2):
            se = jnp.where(row_p == p, sinks_ref[h0 + 2 * p], se)
            so = jnp.where(row_p == p, sinks_ref[h0 + 2 * p + 1], so)
        sink_cols.append((se, so))

    def sub(j, carry):
        off = pl.multiple_of(j * WINDOW, WINDOW)
        rows = pl.ds(off, WINDOW)
        for g in range(GMLP_GROUPS):
            cols = slice(g * LANES, (g + 1) * LANES)
            s = jnp.dot(wsp_ref[g], vn_ref[rows, cols].astype(BF16), preferred_element_type=F32) + bsp_ref[g]
            mrg[rows, cols] = a_ref[rows, cols] * s
        kblk = kcat[pl.ds(off, 2 * WINDOW), :]
        vblk = vcat[pl.ds(off, 2 * WINDOW), :]
        kswp = pltpu.roll(kblk, HEAD_DIM, 1)
        vswp = pltpu.roll(vblk, HEAD_DIM, 1)
        kmin = jnp.where(jnp.logical_and(i == 0, j == 0), WINDOW, 0)
        allowed = key_s[...] >= kmin
        for kk in range(N_KV_HEADS):
            lo_src, hi_src = (kblk, kswp) if kk == 0 else (kswp, kblk)
            kbd = jnp.concatenate([jnp.where(lane_kv < HEAD_DIM, lo_src, 0.0),
                                   jnp.where(lane_kv >= HEAD_DIM, hi_src, 0.0)], axis=0).astype(BF16)
            lo_src, hi_src = (vblk, vswp) if kk == 0 else (vswp, vblk)
            vbd = jnp.concatenate([jnp.where(lane_kv < HEAD_DIM, lo_src, 0.0),
                                   jnp.where(lane_kv >= HEAD_DIM, hi_src, 0.0)], axis=0).astype(BF16)
            pair0 = kk * (GQA_GROUP // 2)
            qs = jnp.concatenate([q_ref[rows, (pair0 + p) * LANES:(pair0 + p + 1) * LANES]
                                  for p in range(GQA_GROUP // 2)], axis=0)
            lg = lax.dot_general(qs, kbd, (((1,), (1,)), ((), ())), preferred_element_type=F32)
            lg = jnp.where(allowed, lg, NEG_INF)
            se, so = sink_cols[kk]
            le, lo = lg[:, :2 * WINDOW], lg[:, 2 * WINDOW:]
            me = jnp.maximum(jnp.max(le, axis=1, keepdims=True), se)
            mo = jnp.maximum(jnp.max(lo, axis=1, keepdims=True), so)
            pe = jnp.exp(le - me)
            po = jnp.exp(lo - mo)
            de = jnp.sum(pe, axis=1, keepdims=True) + jnp.exp(se - me)
            do = jnp.sum(po, axis=1, keepdims=True) + jnp.exp(so - mo)
            pr = jnp.concatenate([pe, po], axis=1).astype(BF16)
            o = jnp.dot(pr, vbd, preferred_element_type=F32)
            o = o / jnp.where(lane_o < HEAD_DIM, de, do)
            for p in range(GQA_GROUP // 2):
                cols = slice((pair0 + p) * LANES, (pair0 + p + 1) * LANES)
                mrg[rows, cols] += sgb_ref[rows, cols] * o[p * WINDOW:(p + 1) * WINDOW]
        return carry

    lax.fori_loop(0, nsub, sub, 0)
    _finish_rows(mrg[...].astype(BF16), x_ref[...], wout_ref, g2_ref, wr_ref, br_ref, xmid_ref, xn2_ref, lg_ref)


def _mix(sinks, q, k, v, a, vn, sgb, x, wsp, bsp, wout, g2, wr, br):
    n = x.shape[0]
    tm = MIX_ROWS
    nsub = tm // WINDOW
    row = lambda w: pl.BlockSpec((tm, w), lambda i: (i, 0))
    prev = pl.BlockSpec((WINDOW, KV_WIDTH), lambda i: (jnp.maximum(i * nsub - 1, 0), 0))
    full = lambda arr: pl.BlockSpec(arr.shape, lambda i: (0,) * arr.ndim)
    smem = pl.BlockSpec(memory_space=pltpu.SMEM)
    return pl.pallas_call(
        _mix_body,
        grid=(n // tm,),
        in_specs=[smem, row(Q_WIDTH), row(KV_WIDTH), prev, row(KV_WIDTH), prev,
                  row(GMLP_WIDTH), row(GMLP_WIDTH), row(D_MODEL), row(D_MODEL),
                  full(wsp), full(bsp), full(wout), full(g2), full(wr), full(br)],
        out_specs=[row(D_MODEL), row(D_MODEL // 2), pl.BlockSpec((N_EXPERTS, tm), lambda i: (0, i))],
        out_shape=[jax.ShapeDtypeStruct((n, D_MODEL), F32),
                   jax.ShapeDtypeStruct((n, D_MODEL // 2), jnp.uint32),
                   jax.ShapeDtypeStruct((N_EXPERTS, n), F32)],
        scratch_shapes=[pltpu.VMEM((tm + WINDOW, KV_WIDTH), F32),
                        pltpu.VMEM((tm + WINDOW, KV_WIDTH), F32),
                        pltpu.VMEM((tm, D_MODEL), F32),
                        pltpu.VMEM((GQA_GROUP // 2 * WINDOW, 4 * WINDOW), jnp.int32)],
        compiler_params=_params(("arbitrary",)),
        name="mix_prompt",
    )(sinks, q, k, k, v, v, a, vn, sgb, x, wsp, bsp, wout, g2, wr, br)


def _sample_attn_body(sink_ref, q_ref, kc_ref, vc_ref, kn_ref, vn_ref, o_ref):
    q = q_ref[...]
    nq, nc, nn = q.shape[1], kc_ref.shape[2], kn_ref.shape[1]
    row_c = lax.broadcasted_iota(jnp.int32, (1, nq, nc), 1)
    row_n = lax.broadcasted_iota(jnp.int32, (1, nq, nn), 1)
    row_o = lax.broadcasted_iota(jnp.int32, (1, nq, HEAD_DIM), 1)
    first_kv = lambda row: (row % N_HEADS) < GQA_GROUP
    cached = lambda ref, kk: ref[:, kk * HEAD_DIM:(kk + 1) * HEAD_DIM, :].astype(BF16)
    fresh = lambda ref, kk: ref[:, :, kk * HEAD_DIM:(kk + 1) * HEAD_DIM].astype(BF16)
    lg_c = jnp.where(first_kv(row_c),
                     jnp.einsum("bqd,bdk->bqk", q, cached(kc_ref, 0), preferred_element_type=F32),
                     jnp.einsum("bqd,bdk->bqk", q, cached(kc_ref, 1), preferred_element_type=F32))
    lg_n = jnp.where(first_kv(row_n),
                     jnp.einsum("bqd,bkd->bqk", q, fresh(kn_ref, 0), preferred_element_type=F32),
                     jnp.einsum("bqd,bkd->bqk", q, fresh(kn_ref, 1), preferred_element_type=F32))
    j_c = lax.broadcasted_iota(jnp.int32, (1, nq, nc), 2)
    j_n = lax.broadcasted_iota(jnp.int32, (1, nq, nn), 2)
    lg_c = jnp.where(j_c > row_c // N_HEADS + (nc - WINDOW), lg_c, NEG_INF)
    lg_n = jnp.where(j_n <= row_n // N_HEADS, lg_n, NEG_INF)
    sink = sink_ref[...][None]
    m = jnp.maximum(jnp.maximum(jnp.max(lg_c, axis=2, keepdims=True), jnp.max(lg_n, axis=2, keepdims=True)), sink)
    p_c = jnp.exp(lg_c - m)
    p_n = jnp.exp(lg_n - m)
    den = jnp.sum(p_c, axis=2, keepdims=True) + jnp.sum(p_n, axis=2, keepdims=True) + jnp.exp(sink - m)
    pb_c, pb_n = p_c.astype(BF16), p_n.astype(BF16)
    heads_out = lambda kk: (jnp.einsum("bqk,bdk->bqd", pb_c, cached(vc_ref, kk), preferred_element_type=F32)
                            + jnp.einsum("bqk,bkd->bqd", pb_n, fresh(vn_ref, kk), preferred_element_type=F32))
    o_ref[...] = jnp.where(first_kv(row_o), heads_out(0), heads_out(1)) / den


def _sample_attn(sink_col, q3, k_cache_t, v_cache_t, k_new, v_new):
    nb = q3.shape[0]
    bb = 32
    blk = lambda a: pl.BlockSpec((bb,) + a.shape[1:], lambda b: (b, 0, 0))
    args = (q3, k_cache_t, v_cache_t, k_new, v_new)
    return pl.pallas_call(
        _sample_attn_body,
        grid=(nb // bb,),
        in_specs=[pl.BlockSpec(sink_col.shape, lambda b: (0, 0))] + [blk(a) for a in args],
        out_specs=blk(q3),
        out_shape=jax.ShapeDtypeStruct(q3.shape, F32),
        compiler_params=_params(("arbitrary",)),
        name="attn_sample",
    )(sink_col, *args)


def _window_body(c_ref, n_ref, o_ref):
    bb, _, w = c_ref.shape
    ts = n_ref.shape[1] // bb
    new = n_ref[...]
    lane = lax.broadcasted_iota(jnp.int32, new.shape, 1)
    for b in range(bb):
        old = pltpu.roll(c_ref[b], w - ts, 1)
        o_ref[b] = jnp.where(lane >= w - ts, pltpu.roll(new, (w - ts - b * ts) % w, 1), old)


def _window(cache_t, new_t):
    nb, ch, w = cache_t.shape
    ts = new_t.shape[1] // nb
    bb = LANES // ts
    assert w == LANES and bb * ts == LANES and nb % bb == 0
    return pl.pallas_call(
        _window_body,
        grid=(nb // bb,),
        in_specs=[pl.BlockSpec((bb, ch, w), lambda b: (b, 0, 0)), pl.BlockSpec((ch, LANES), lambda b: (0, b))],
        out_specs=pl.BlockSpec((bb, ch, w), lambda b: (b, 0, 0)),
        out_shape=jax.ShapeDtypeStruct(cache_t.shape, cache_t.dtype),
        compiler_params=_params(("arbitrary",)),
        name="window",
    )(cache_t, new_t)


def _mix_sample_body(a_ref, vn_ref, sgb_ref, o_ref, x_ref, coef_ref, bias_ref,
                     wout_ref, g2_ref, wr_ref, br_ref, xmid_ref, xn2_ref, lg_ref):
    vn = vn_ref[...]
    n, width = vn.shape
    rows8 = lambda t: t.reshape(n // 8, 8, width)
    s = bias_ref[...][None] + coef_ref[0][None] * rows8(vn)
    for d in range(1, coef_ref.shape[0]):
        s = s + coef_ref[d][None] * rows8(pltpu.roll(vn, d, 0))
    merged = a_ref[...] * s.reshape(n, width) + sgb_ref[...] * o_ref[...]
    _finish_rows(merged.astype(BF16), x_ref[...], wout_ref, g2_ref, wr_ref, br_ref, xmid_ref, xn2_ref, lg_ref)


def _mix_sample(a, vn, sgb, o, x, coef, bias, wout, g2, wr, br):
    n = x.shape[0]
    args = (a, vn, sgb, o, x, coef, bias, wout, g2, wr, br)
    full = lambda arr: pl.BlockSpec(arr.shape, lambda i: (0,) * arr.ndim)
    return pl.pallas_call(
        _mix_sample_body,
        grid=(1,),
        in_specs=[full(arr) for arr in args],
        out_specs=[pl.BlockSpec((n, D_MODEL), lambda i: (0, 0)), pl.BlockSpec((n, D_MODEL // 2), lambda i: (0, 0)),
                   pl.BlockSpec((N_EXPERTS, n), lambda i: (0, 0))],
        out_shape=[jax.ShapeDtypeStruct((n, D_MODEL), F32),
                   jax.ShapeDtypeStruct((n, D_MODEL // 2), jnp.uint32),
                   jax.ShapeDtypeStruct((N_EXPERTS, n), F32)],
        compiler_params=_params(("arbitrary",)),
        name="mix_sample",
    )(*args)


def _rows8(rows, dtype):
    n = rows[0].shape[1]
    sub = lax.broadcasted_iota(jnp.int32, (8, n), 0)
    out = jnp.zeros((8, n), dtype)
    for kx, r in enumerate(rows):
        out = jnp.where(sub == kx, r.astype(dtype), out)
    return out


def _route_body(nblk, reserve, lg_ref, prior_ref, gate_ref, dest_ref, meta_ref, idx_s, rank_s, base):
    i = pl.program_id(0)

    @pl.when(i == 0)
    def _():
        base[...] = jnp.zeros_like(base)

    l = lg_ref[...]
    tb = l.shape[1]
    sub = lax.broadcasted_iota(jnp.int32, l.shape, 0).astype(F32)
    vals, idxs, sels = [], [], []
    for _ in range(TOP_K):
        m = jnp.max(l, axis=0, keepdims=True)
        ik = jnp.min(jnp.where(l == m, sub, float(N_EXPERTS)), axis=0, keepdims=True)
        sel = sub == ik
        l = jnp.where(sel, -jnp.inf, l)
        vals.append(m)
        idxs.append(ik)
        sels.append(sel)
    es = [jnp.exp(vk - vals[0]) for vk in vals]
    den = es[0] + es[1] + es[2] + es[3]
    onehot = jnp.zeros(l.shape, F32)
    for sel in sels:
        onehot = onehot + sel.astype(F32)
    earlier = (lax.broadcasted_iota(jnp.int32, (tb, tb), 0) < lax.broadcasted_iota(jnp.int32, (tb, tb), 1))
    before = jnp.dot(onehot.astype(BF16), earlier.astype(BF16), preferred_element_type=F32) + base[...]
    ranks = [jnp.sum(jnp.where(sel, before, 0.0), axis=0, keepdims=True) for sel in sels]
    base[...] += jnp.sum(onehot, axis=1, keepdims=True)
    idx_s[i] = _rows8(idxs, F32)
    rank_s[i] = _rows8(ranks, F32)
    gates = jnp.concatenate([_rows8([e / den for e in es], F32), jnp.zeros((LANES - 8, tb), F32)], axis=0)
    gate_ref[...] = gates.T

    @pl.when(i == nblk - 1)
    def _():
        cnt = base[...]
        if reserve is None:
            seg0 = prior_ref[:, 0:1]
            placed = prior_ref[:, 1:2]
        else:
            cap = jnp.ceil((cnt + reserve) / EXPERT_ROWS) * EXPERT_ROWS
            lower = (lax.broadcasted_iota(jnp.int32, (N_EXPERTS, N_EXPERTS), 1) <
                     lax.broadcasted_iota(jnp.int32, (N_EXPERTS, N_EXPERTS), 0)).astype(F32)
            seg0 = jnp.dot(lower, jnp.broadcast_to(cap, (N_EXPERTS, LANES)), preferred_element_type=F32,
                           precision=lax.Precision.HIGHEST)[:, :1]
            placed = jnp.zeros_like(cnt)
        total = placed + cnt
        lane = lax.broadcasted_iota(jnp.int32, (N_EXPERTS, LANES), 1)
        meta_ref[...] = jnp.where(lane == 0, seg0, jnp.where(lane == 1, total, jnp.where(
            lane == 2, seg0 / EXPERT_ROWS, jnp.where(lane == 3, jnp.ceil(total / EXPERT_ROWS), 0.0))))
        first = seg0 + placed
        sub_e = lax.broadcasted_iota(jnp.int32, (N_EXPERTS, tb), 0).astype(F32)
        for b in range(nblk):
            idx, rank = idx_s[b], rank_s[b]
            rows = [jnp.sum(jnp.where(sub_e == idx[kx:kx + 1], first, 0.0), axis=0, keepdims=True)
                    + rank[kx:kx + 1] for kx in range(TOP_K)]
            dest_ref[:, b * tb:(b + 1) * tb] = _rows8(rows, jnp.int32)


def _route(logits, prior, reserve):
    n = logits.shape[1]
    tb = min(n, ROUTE_ROWS)
    nblk = n // tb
    assert nblk * tb == n
    table = pl.BlockSpec((N_EXPERTS, LANES), lambda i: (0, 0))
    return pl.pallas_call(
        functools.partial(_route_body, nblk, reserve),
        grid=(nblk,),
        in_specs=[pl.BlockSpec((N_EXPERTS, tb), lambda i: (0, i)), table],
        out_specs=[pl.BlockSpec((tb, LANES), lambda i: (i, 0)), pl.BlockSpec((8, n), lambda i: (0, 0)), table],
        out_shape=[jax.ShapeDtypeStruct((n, LANES), F32),
                   jax.ShapeDtypeStruct((8, n), jnp.int32),
                   jax.ShapeDtypeStruct((N_EXPERTS, LANES), F32)],
        scratch_shapes=[pltpu.VMEM((nblk, 8, tb), F32), pltpu.VMEM((nblk, 8, tb), F32),
                        pltpu.VMEM((N_EXPERTS, 1), F32)],
        compiler_params=_params(("arbitrary",)),
        name="route",
    )(logits, prior)


def _sc_mesh():
    return plsc.VectorSubcoreMesh(core_axis_name="c", subcore_axis_name="s")


def _sc_worker():
    return lax.axis_index("s") * SC_CORES + lax.axis_index("c")


def _sc_dispatch(x, dest_t, n_slots):
    chunk = SC_ROWS
    n = x.shape[0]
    per_w = n // (SC_WORKERS * chunk)
    assert per_w * SC_WORKERS * chunk == n and per_w % 2 == 0
    d3 = dest_t.reshape(dest_t.shape[0], n // chunk, chunk)
    width, dtype = x.shape[1], x.dtype

    @functools.partial(
        pl.kernel, mesh=_sc_mesh(),
        out_type=jax.ShapeDtypeStruct((n_slots, width), dtype),
        scratch_types=[pltpu.VMEM((TOP_K, per_w, chunk), jnp.int32),
                       pltpu.VMEM((2, chunk, width), dtype),
                       pltpu.SemaphoreType.DMA, pltpu.SemaphoreType.DMA],
        compiler_params=pltpu.CompilerParams(use_tc_tiling_on_sc=True),
        name="dispatch")
    def run(x_hbm, d_hbm, out_hbm, idx_v, rows_v, rsem, wsem):
        wid = _sc_worker()
        pltpu.sync_copy(d_hbm.at[pl.ds(0, TOP_K), pl.ds(wid * per_w, per_w)], idx_v)

        def read(j, slot):
            return pltpu.make_async_copy(x_hbm.at[pl.ds((wid * per_w + j) * chunk, chunk)], rows_v.at[slot], rsem)

        def scatter(j, slot):
            copies = [pltpu.async_copy(rows_v.at[slot], out_hbm.at[idx_v.at[kx, j]], wsem) for kx in range(TOP_K)]
            for cp in copies:
                cp.wait()

        read(0, 0).start()

        def body(h, carry):
            j = 2 * h
            read(j, 0).wait()
            read(j + 1, 1).start()
            scatter(j, 0)
            read(j + 1, 1).wait()

            @pl.when(j + 2 < per_w)
            def _():
                read(j + 2, 0).start()

            scatter(j + 1, 1)
            return carry

        lax.fori_loop(0, per_w // 2, body, 0)

    return run(x, d3)


def _dispatch_small_body(dest_ref, x_ref, _, out_hbm, sem):
    n = x_ref.shape[0]

    def body(t, carry):
        for kx in range(TOP_K):
            pltpu.make_async_copy(x_ref.at[pl.ds(t, 1)], out_hbm.at[pl.ds(dest_ref[kx, t], 1)], sem).start()
        return carry

    lax.fori_loop(0, n, body, 0, unroll=4)
    for _ in range(TOP_K):
        pltpu.make_async_copy(x_ref, out_hbm.at[pl.ds(0, n)], sem).wait()


def _dispatch_small(x, dest_t, x_sorted):
    return pl.pallas_call(
        _dispatch_small_body,
        grid=(1,),
        in_specs=[pl.BlockSpec(memory_space=pltpu.SMEM), pl.BlockSpec(x.shape, lambda i: (0, 0)),
                  pl.BlockSpec(memory_space=pl.ANY)],
        out_specs=pl.BlockSpec(memory_space=pl.ANY),
        out_shape=jax.ShapeDtypeStruct(x_sorted.shape, x_sorted.dtype),
        scratch_shapes=[pltpu.SemaphoreType.DMA],
        input_output_aliases={2: 0},
        compiler_params=_params(("arbitrary",)),
        name="dispatch_small",
    )(dest_t, x, x_sorted)


def _sc_collect(y_sorted, dest_t, tok0, n, small=None):
    chunk = SC_ROWS
    per_choice = SC_WORKERS // TOP_K
    per_w = n // (per_choice * chunk)
    assert per_w * per_choice * chunk == n and per_w % 2 == 0 and tok0 % chunk == 0
    d3 = dest_t.reshape(dest_t.shape[0], dest_t.shape[1] // chunk, chunk)
    width, dtype = y_sorted.shape[1], y_sorted.dtype
    out_type = [jax.ShapeDtypeStruct((TOP_K * n, width), dtype)]
    if small is not None:
        small0, n_small = small
        assert n_small == per_choice * chunk and small0 % chunk == 0
        out_type.append(jax.ShapeDtypeStruct((TOP_K * n_small, width), dtype))

    @functools.partial(
        pl.kernel, mesh=_sc_mesh(),
        out_type=out_type,
        scratch_types=[pltpu.VMEM((per_w, chunk), jnp.int32),
                       pltpu.VMEM((1, chunk), jnp.int32),
                       pltpu.VMEM((2, chunk, width), dtype),
                       pltpu.SemaphoreType.DMA, pltpu.SemaphoreType.DMA],
        compiler_params=pltpu.CompilerParams(use_tc_tiling_on_sc=True),
        name="collect")
    def run(y_hbm, d_hbm, *rest):
        outs, (ip_v, is_v, rows_v, gsem, wsem) = rest[:len(out_type)], rest[len(out_type):]
        op_hbm = outs[0]
        wid = _sc_worker()
        choice = wid // per_choice
        part = wid % per_choice
        pltpu.sync_copy(d_hbm.at[choice, pl.ds(tok0 // chunk + part * per_w, per_w)], ip_v)

        def gather(idx_v, j, slot):
            return pltpu.make_async_copy(y_hbm.at[idx_v.at[j]], rows_v.at[slot], gsem)

        def write(j, slot):
            return pltpu.make_async_copy(rows_v.at[slot], op_hbm.at[pl.ds((wid * per_w + j) * chunk, chunk)], wsem)

        gather(ip_v, 0, 0).start()

        def body(h, carry):
            j = 2 * h
            gather(ip_v, j, 0).wait()

            @pl.when(h > 0)
            def _():
                write(j - 1, 1).wait()

            gather(ip_v, j + 1, 1).start()
            write(j, 0).start()
            gather(ip_v, j + 1, 1).wait()
            write(j, 0).wait()

            @pl.when(j + 2 < per_w)
            def _():
                gather(ip_v, j + 2, 0).start()

            write(j + 1, 1).start()
            return carry

        lax.fori_loop(0, per_w // 2, body, 0)
        write(per_w - 1, 1).wait()

        if small is not None:
            pltpu.sync_copy(d_hbm.at[choice, pl.ds(small0 // chunk + part, 1)], is_v)
            gather(is_v, 0, 0).start()
            gather(is_v, 0, 0).wait()
            pltpu.sync_copy(rows_v.at[0], outs[1].at[pl.ds(wid * chunk, chunk)])

    return run(y_sorted, d3)


def _expert_body(blk0_ref, nblk_ref, cnt_ref, wup_hbm, wdn_hbm, bup_ref, bdn_ref, x_hbm, y_hbm,
                 wup_f, wdn_f, wup_s, wdn_s, xbuf, obuf, w_sem, in_sem, out_sem):
    e = pl.program_id(0)
    nb = nblk_ref[e]
    blk0 = blk0_ref[e]
    cnt = cnt_ref[e]
    tm = EXPERT_ROWS
    pair = 2 * LANES
    wslot = e % 2
    up_rows = D_MODEL // W_PIECES
    dn_rows = D_MODEL // (W_PIECES // 2)

    def w_piece(hbm, buf, ex, slot, p, rows):
        start = p * rows if isinstance(p, int) else pl.multiple_of(p * rows, rows)
        r = pl.ds(start, rows)
        return pltpu.make_async_copy(hbm.at[ex, r], buf.at[slot, r], w_sem.at[slot])

    def w_start(ex, slot, p):
        w_piece(wup_hbm, wup_f, ex, slot, p, up_rows).start()
        if isinstance(p, int):
            if p < W_PIECES // 2:
                w_piece(wdn_hbm, wdn_f, ex, slot, p, dn_rows).start()
        else:
            @pl.when(p < W_PIECES // 2)
            def _():
                w_piece(wdn_hbm, wdn_f, ex, slot, p, dn_rows).start()

    def w_wait(ex, slot):
        for p in range(W_PIECES):
            w_piece(wup_hbm, wup_f, ex, slot, p, up_rows).wait()
        for p in range(W_PIECES // 2):
            w_piece(wdn_hbm, wdn_f, ex, slot, p, dn_rows).wait()

    @pl.when(e == 0)
    def _():
        for p in range(W_PIECES):
            w_start(0, 0, p)

    w_wait(e, wslot)
    wup_ref = wup_f.at[wslot]
    wdn_ref = wdn_f.at[wslot]
    more = e + 1 < N_EXPERTS

    def x_copy(i, slot):
        rows = pl.ds(pl.multiple_of((blk0 + i) * tm, tm), tm)
        return pltpu.make_async_copy(x_hbm.at[rows], xbuf.at[slot], in_sem.at[slot])

    def y_copy(i, slot):
        rows = pl.ds(pl.multiple_of((blk0 + i) * tm, tm), tm)
        return pltpu.make_async_copy(obuf.at[slot], y_hbm.at[rows], out_sem.at[slot])

    @pl.when(nb > 0)
    def _():
        x_copy(0, 0).start(priority=1)
        r = lax.broadcasted_iota(jnp.int32, (pair, pair), 0)
        c = lax.broadcasted_iota(jnp.int32, (pair, pair), 1)
        perm = (r == jnp.where(c < LANES, 2 * c, 2 * (c - LANES) + 1)).astype(BF16)
        for g in range(2 * D_MODEL // pair):
            cols = slice(g * pair, (g + 1) * pair)
            wup_s[g] = jnp.dot(wup_ref[:, cols].astype(BF16), perm, preferred_element_type=F32).astype(BF16)
        for g in range(D_MODEL // pair):
            wdn_s[g] = wdn_ref[:, g * pair:(g + 1) * pair].astype(BF16)

        def block(i, carry):
            slot = i % 2
            x_copy(i, slot).wait()

            @pl.when(i + 1 < nb)
            def _():
                x_copy(i + 1, 1 - slot).start(priority=1)

            @pl.when(i >= 2)
            def _():
                y_copy(i - 2, slot).wait()

            @pl.when(jnp.logical_and(more, i < W_PIECES))
            def _():
                w_start(e + 1, 1 - wslot, i)

            left = cnt - i * tm

            def compute(rows):
                row = lax.broadcasted_iota(jnp.int32, (rows, 1), 0)
                x = jnp.where(row < left, _unpack_bf16_pair(xbuf[slot, :rows]), 0.0).astype(BF16)
                acts = []
                for g in range(2 * D_MODEL // pair):
                    cols = slice(g * pair, (g + 1) * pair)
                    h = jnp.dot(x, wup_s[g], preferred_element_type=F32) + bup_ref[0, :, cols]
                    glu = jnp.minimum(h[:, :LANES], SWIGLU_LIMIT)
                    lin = jnp.clip(h[:, LANES:], -SWIGLU_LIMIT, SWIGLU_LIMIT)
                    acts.append((glu * jax.nn.sigmoid(SWIGLU_ALPHA * glu) * (lin + 1.0)).astype(BF16))
                act = jnp.concatenate(acts, axis=1)
                half_groups = D_MODEL // pair // 2
                for g in range(half_groups):
                    ys = []
                    for gg in (g, g + half_groups):
                        cols = slice(gg * pair, (gg + 1) * pair)
                        ys.append(jnp.dot(act, wdn_s[gg], preferred_element_type=F32) + bdn_ref[0, :, cols])
                    obuf[slot, :rows, g * pair:(g + 1) * pair] = _pack_bf16_pair(ys[0], ys[1])

            @pl.when(left > tm // 2)
            def _():
                compute(tm)

            @pl.when(left <= tm // 2)
            def _():
                compute(tm // 2)
                obuf[slot, tm // 2:, :] = jnp.zeros((tm - tm // 2, obuf.shape[2]), obuf.dtype)

            y_copy(i, slot).start(priority=1)
            return carry

        lax.fori_loop(0, nb, block, 0)

        @pl.when(nb >= 2)
        def _():
            y_copy(nb - 2, nb % 2).wait()

        y_copy(nb - 1, (nb - 1) % 2).wait()

    for p in range(W_PIECES):
        @pl.when(jnp.logical_and(more, p >= nb))
        def _():
            w_start(e + 1, 1 - wslot, p)


def _experts(blk0, nblk, cnt, x_sorted, w_up, w_down, b_up_grouped, b_down):
    tm = EXPERT_ROWS
    per_expert = lambda a: pl.BlockSpec((1,) + a.shape[1:], lambda e, b0, nb, ct: (e, 0, 0))
    hbm = pl.BlockSpec(memory_space=pl.ANY)
    grid_spec = pltpu.PrefetchScalarGridSpec(
        num_scalar_prefetch=3,
        grid=(N_EXPERTS,),
        in_specs=[hbm, hbm, per_expert(b_up_grouped), per_expert(b_down), hbm],
        out_specs=hbm,
        scratch_shapes=[pltpu.VMEM((2,) + w_up.shape[1:], F32), pltpu.VMEM((2,) + w_down.shape[1:], F32),
                        pltpu.VMEM((2 * D_MODEL // (2 * LANES), D_MODEL, 2 * LANES), BF16),
                        pltpu.VMEM((D_MODEL // (2 * LANES), D_MODEL, 2 * LANES), BF16),
                        pltpu.VMEM((2, tm, x_sorted.shape[1]), x_sorted.dtype),
                        pltpu.VMEM((2, tm, D_MODEL // 2), jnp.uint32),
                        pltpu.SemaphoreType.DMA((2,)), pltpu.SemaphoreType.DMA((2,)), pltpu.SemaphoreType.DMA((2,))],
    )
    return pl.pallas_call(
        _expert_body,
        grid_spec=grid_spec,
        out_shape=jax.ShapeDtypeStruct((x_sorted.shape[0], D_MODEL // 2), jnp.uint32),
        compiler_params=_params(("arbitrary",)),
        name="experts",
    )(blk0, nblk, cnt, w_up, w_down, b_up_grouped, b_down, x_sorted)


def _combine_body(gate_ref, xmid_ref, gfin_ref, y0_ref, y1_ref, y2_ref, y3_ref, out_ref):
    gate = gate_ref[...]
    moe = _unpack_bf16_pair(y0_ref[...]) * gate[:, 0:1]
    for kx, y_ref in enumerate((y1_ref, y2_ref, y3_ref), start=1):
        moe = moe + _unpack_bf16_pair(y_ref[...]) * gate[:, kx:kx + 1]
    out_ref[...] = _rms(xmid_ref[...] + moe, gfin_ref[...])


def _combine_update_body(gate_ref, xmid_ref, gfin_ref, y0_ref, y1_ref, y2_ref, y3_ref, prev_ref, out_ref):
    del prev_ref
    _combine_body(gate_ref, xmid_ref, gfin_ref, y0_ref, y1_ref, y2_ref, y3_ref, out_ref)


def _combine(gates, xmid, tok0, n, gfin, y_rows, out_so_far=None):
    tt = min(n, COMBINE_ROWS)
    nblk = n // tt
    blk0 = tok0 // tt
    assert nblk * tt == n and blk0 * tt == tok0
    choice = lambda kx: pl.BlockSpec((tt, y_rows.shape[1]), lambda i: (i + kx * nblk, 0))
    in_specs = [pl.BlockSpec((tt, LANES), lambda i: (i + blk0, 0)),
                pl.BlockSpec((tt, D_MODEL), lambda i: (i + blk0, 0)),
                pl.BlockSpec((1, D_MODEL), lambda i: (0, 0))] + [choice(kx) for kx in range(TOP_K)]
    args = [gates, xmid, gfin, y_rows, y_rows, y_rows, y_rows]
    body, aliases = _combine_body, {}
    if out_so_far is not None:
        in_specs.append(pl.BlockSpec(memory_space=pl.ANY))
        args.append(out_so_far)
        body, aliases = _combine_update_body, {len(args) - 1: 0}
    return pl.pallas_call(
        body,
        grid=(nblk,),
        in_specs=in_specs,
        out_specs=pl.BlockSpec((tt, D_MODEL), lambda i: (i + blk0, 0)),
        out_shape=jax.ShapeDtypeStruct(xmid.shape, F32),
        input_output_aliases=aliases,
        compiler_params=_params(("arbitrary",)),
        name="combine",
    )(*args)


def kernel(x_prompt, x_sample, cache_k_win, cache_v_win, norm_attn_g, w_in, ln_v_g, ln_v_b, w_spatial, b_spatial,
           attn_sinks, w_out, norm_ffn_g, w_router, b_router, w_up, b_up, w_down, b_down, norm_final_g):
    bp, tp, _ = x_prompt.shape
    bs, ts, _ = x_sample.shape
    w_buf = cache_k_win.shape[2]
    assert bp == 1 and tp % MIX_ROWS == 0 and w_buf == WINDOW and (bs * ts) % PROJ_ROWS == 0 and 8 % ts == 0
    n_p, n_s = bp * tp, bs * ts
    row2 = lambda a: a.reshape(1, -1)

    w_in_bf = w_in[0].astype(BF16)
    w_out_bf = w_out[0].astype(BF16)
    tril = jnp.tril(jnp.ones((CHUNK, CHUNK), dtype=bool))
    wsp = jnp.where(tril[None], w_spatial[0], 0.0)
    wsp_bf = wsp.astype(BF16)
    bsp = jnp.broadcast_to(b_spatial[0][:, :, None], (GMLP_GROUPS, CHUNK, LANES))
    b_up_grouped = b_up[0].reshape(N_EXPERTS, -1, LANES, 2).transpose(0, 1, 3, 2).reshape(N_EXPERTS, 1, -1)
    bd = b_down[0][:, None, :]
    g1, g2, gfin = row2(norm_attn_g[0]), row2(norm_ffn_g[0]), row2(norm_final_g)
    lng, lnb = row2(ln_v_g[0]), row2(ln_v_b[0])
    wr_hi = w_router[0].astype(BF16)
    wr = jnp.concatenate([wr_hi, (w_router[0] - wr_hi.astype(F32)).astype(BF16)], axis=1)
    br = row2(b_router[0])
    sinks = attn_sinks[0]

    xp = x_prompt.reshape(n_p, D_MODEL)
    cs_p = _rotary_inputs(jnp.arange(tp, dtype=jnp.int32))
    q_p, k_p, v_p, a_p, vn_p, sgb_p = _proj(xp, g1, w_in_bf, cs_p, lng, lnb)
    xmid_p, xn2_p, lg_p = _mix(sinks, q_p, k_p, v_p, a_p, vn_p, sgb_p, xp, wsp_bf, bsp, w_out_bf, g2, wr, br)

    xs = x_sample.reshape(n_s, D_MODEL)
    pos_s = PAST_LEN + jnp.arange(ts, dtype=jnp.int32)
    cs_s = _rotary_inputs(jnp.tile(pos_s, bs))
    q_s, k_s, v_s, a_s, vn_s, sgb_s = _proj(xs, g1, w_in_bf, cs_s, lng, lnb)
    keys_minor = lambda t: t.reshape(bs, -1, KV_WIDTH).transpose(0, 2, 1)
    k_cache_t, v_cache_t = keys_minor(cache_k_win[0]), keys_minor(cache_v_win[0])
    pad_new = lambda t: jnp.pad(t.reshape(bs, ts, KV_WIDTH), ((0, 0), (0, (-ts) % 16), (0, 0)))
    sink_col = jnp.tile(sinks, ts).reshape(ts * N_HEADS, 1)
    o_s = _sample_attn(sink_col, q_s.reshape(bs, ts * N_HEADS, HEAD_DIM), k_cache_t, v_cache_t,
                       pad_new(k_s), pad_new(v_s)).reshape(n_s, Q_WIDTH)
    lag = np.arange(ts)[:, None] - np.arange(ts)[None, :]
    coef = jnp.stack([jnp.sum(jnp.where(lag == d, wsp[:, :ts, :ts], 0.0), axis=2)
                      for d in range(ts)])
    coef = jnp.repeat(coef.transpose(0, 2, 1), GMLP_WIDTH // GMLP_GROUPS, axis=2)
    coef = jnp.tile(coef, (1, 8 // ts, 1))
    bias = jnp.tile(jnp.repeat(b_spatial[0][:, :ts].T, GMLP_WIDTH // GMLP_GROUPS, axis=1), (8 // ts, 1))
    xmid_s, xn2_s, lg_s = _mix_sample(a_s, vn_s, sgb_s, o_s, xs, coef, bias, w_out_bf, g2, wr, br)

    tm = EXPERT_ROWS
    n_blocks = (n_p * TOP_K + N_EXPERTS * n_s) // tm + N_EXPERTS
    gate_p, dest_p, table_p = _route(lg_p, jnp.zeros((N_EXPERTS, LANES), F32), n_s)
    x_sorted = _sc_dispatch(xn2_p, dest_p, n_blocks * tm)
    gate_s, dest_s, table = _route(lg_s, table_p, None)
    x_sorted = _dispatch_small(xn2_s, dest_s, x_sorted)
    dest_t = jnp.concatenate([dest_p, dest_s], axis=1)
    meta = table.astype(jnp.int32)
    y_sorted = _experts(meta[:, 2], meta[:, 3], meta[:, 1], x_sorted, w_up[0], w_down[0], b_up_grouped, bd)
    half = n_p // 2
    yrows_a, yrows_s = _sc_collect(y_sorted, dest_t, 0, half, small=(n_p, n_s))
    yrows_b, = _sc_collect(y_sorted, dest_t, half, half)
    y_s = _combine(gate_s, xmid_s, 0, n_s, gfin, yrows_s)
    y_p = _combine(gate_p, xmid_p, 0, half, gfin, yrows_a)
    y_p = _combine(gate_p, xmid_p, half, half, gfin, yrows_b, out_so_far=y_p)

    k4 = lambda t: t.reshape(1, bp, -1, N_KV_HEADS, HEAD_DIM)
    next_window = lambda cache_t, new: _window(cache_t, new.T).transpose(0, 2, 1).reshape(
        1, bs, w_buf, N_KV_HEADS, HEAD_DIM)
    return (y_p.reshape(bp, tp, D_MODEL),
            y_s.reshape(bs, ts, D_MODEL),
            k4(k_p[n_p - WINDOW:]),
            k4(v_p[n_p - WINDOW:]),
            vn_p[n_p - CHUNK:].reshape(1, bp, CHUNK, GMLP_WIDTH),
            next_window(k_cache_t, k_s),
            next_window(v_cache_t, v_s),
            vn_s.reshape(1, bs, ts, GMLP_WIDTH))
```

```python
import functools

import numpy as np
import jax
import jax.numpy as jnp
from jax import lax
from jax.experimental import pallas as pl
from jax.experimental.pallas import tpu as pltpu
from jax.experimental.pallas import tpu_sc as plsc

F32 = jnp.float32
BF16 = jnp.bfloat16

D_MODEL = 1024
HEAD_DIM = 64
N_HEADS = 16
GQA_GROUP = 8
N_KV_HEADS = 2
Q_WIDTH = 1024
KV_WIDTH = 128
WINDOW = 128
ROT_DIM = 16
ROPE_THETA = 500000.0
CHUNK = 128
GMLP_WIDTH = 1024
GMLP_GROUPS = 8
N_EXPERTS = 32
TOP_K = 4
SWIGLU_LIMIT = 7.0
SWIGLU_ALPHA = 1.702
RMS_EPS = 1e-5
LN_EPS = 1e-5
NEG_INF = -1e30
PAST_LEN = 16384

LANES = 128
VMEM_LIMIT = 56 * 1024 * 1024

PROJ_ROWS = 256
MIX_ROWS = 512
ROUTE_ROWS = 1024
EXPERT_ROWS = 256
W_PIECES = 8
COMBINE_ROWS = 1024

SC_CORES = 2
SC_WORKERS = 32
SC_ROWS = 64

_C_Q, _C_KV, _C_U, _C_VG, _C_GA, _C_GB, _C_END = 0, 1024, 1280, 2304, 3328, 4352, 5376


def _params(sem):
    return pltpu.CompilerParams(dimension_semantics=sem, vmem_limit_bytes=VMEM_LIMIT)


def _rms(x, g):
    return x * lax.rsqrt(jnp.mean(x * x, axis=-1, keepdims=True) + RMS_EPS) * g


def _pack_bf16_pair(lo, hi):
    lo_bits = lax.bitcast_convert_type(lo.astype(BF16).astype(F32), jnp.uint32)
    hi_bits = lax.bitcast_convert_type(hi.astype(BF16).astype(F32), jnp.uint32)
    return (lo_bits >> 16) | hi_bits


def _unpack_bf16_pair(words):
    lo = lax.bitcast_convert_type(words << 16, F32)
    hi = lax.bitcast_convert_type(words & jnp.uint32(0xFFFF0000), F32)
    return jnp.concatenate([lo, hi], axis=1)


def _proj_body(x_ref, g_ref, w_ref, cs_ref, rot_ref, lng_ref, lnb_ref,
               q_ref, k_ref, v_ref, a_ref, vn_ref, sgb_ref):
    h = _rms(x_ref[...], g_ref[...]).astype(BF16)
    tabs = lax.dot_general(cs_ref[...], rot_ref[...], (((0,), (0,)), ((), ())), preferred_element_type=F32)
    rc, rs1, rs2 = tabs[:, :LANES], tabs[:, LANES:2 * LANES], tabs[:, 2 * LANES:]

    def rot(z):
        return z * rc + pltpu.roll(z, LANES - ROT_DIM // 2, 1) * rs1 + pltpu.roll(z, ROT_DIM // 2, 1) * rs2

    def mm(lo, hi):
        return jnp.dot(h, w_ref[:, lo:hi], preferred_element_type=F32)

    zq = mm(_C_Q, _C_KV)
    for c in range(Q_WIDTH // LANES):
        sl = slice(c * LANES, (c + 1) * LANES)
        q_ref[:, sl] = (rot(zq[:, sl]) * (HEAD_DIM ** -0.5)).astype(BF16)
    zkv = mm(_C_KV, _C_U)
    k_ref[...] = rot(zkv[:, :KV_WIDTH])
    v_ref[...] = zkv[:, KV_WIDTH:]
    a_ref[...] = jax.nn.sigmoid(mm(_C_GA, _C_GB)) * jax.nn.gelu(mm(_C_U, _C_VG))
    zv = jax.nn.gelu(mm(_C_VG, _C_GA))
    zc = zv - jnp.mean(zv, axis=-1, keepdims=True)
    var = jnp.mean(zc * zc, axis=-1, keepdims=True)
    vn_ref[...] = zc * lax.rsqrt(var + LN_EPS) * lng_ref[...] + lnb_ref[...]
    sgb_ref[...] = jax.nn.sigmoid(mm(_C_GB, _C_END))


def _proj(x, norm_g, w_in_bf, cs, ln_g, ln_b):
    n = x.shape[0]
    tm = PROJ_ROWS
    row = lambda w: pl.BlockSpec((tm, w), lambda i: (i, 0))
    full = lambda a: pl.BlockSpec(a.shape, lambda i: (0,) * a.ndim)
    rot = jnp.asarray(np.tile(_ROT_EXPAND, (3, 1)), dtype=BF16)
    return pl.pallas_call(
        _proj_body,
        grid=(n // tm,),
        in_specs=[row(D_MODEL), full(norm_g), full(w_in_bf), pl.BlockSpec((cs.shape[0], tm), lambda i: (0, i)),
                  full(rot),
                  full(ln_g), full(ln_b)],
        out_specs=[row(Q_WIDTH), row(KV_WIDTH), row(KV_WIDTH), row(GMLP_WIDTH), row(GMLP_WIDTH), row(D_MODEL)],
        out_shape=[jax.ShapeDtypeStruct((n, Q_WIDTH), BF16),
                   jax.ShapeDtypeStruct((n, KV_WIDTH), F32),
                   jax.ShapeDtypeStruct((n, KV_WIDTH), F32),
                   jax.ShapeDtypeStruct((n, GMLP_WIDTH), F32),
                   jax.ShapeDtypeStruct((n, GMLP_WIDTH), F32),
                   jax.ShapeDtypeStruct((n, D_MODEL), F32)],
        compiler_params=_params(("arbitrary",)),
        name="proj",
    )(x, norm_g, w_in_bf, cs, rot, ln_g, ln_b)


_ROT_COLS = 32


def _rot_expand():
    half = ROT_DIM // 2
    m = np.zeros((_ROT_COLS, 3 * LANES), np.float32)
    for lane in range(LANES):
        d = lane % HEAD_DIM
        if d < ROT_DIM:
            m[d % half, lane] = 1.0
        else:
            m[2 * half, lane] = 1.0
        if d < half:
            m[half + d, LANES + lane] = -1.0
        elif d < ROT_DIM:
            m[half + d - half, 2 * LANES + lane] = 1.0
    return m


_ROT_EXPAND = _rot_expand()


def _rotary_inputs(pos):
    half = ROT_DIM // 2
    inv_freq = ROPE_THETA ** (-jnp.arange(half, dtype=F32) / half)
    ang = inv_freq[:, None] * pos.astype(F32)[None, :]
    n = pos.shape[0]
    cs = jnp.concatenate([jnp.cos(ang), jnp.sin(ang), jnp.ones((1, n), F32),
                          jnp.zeros((_ROT_COLS - 2 * half - 1, n), F32)], axis=0)
    hi = cs.astype(BF16)
    rest = cs - hi.astype(F32)
    mid = rest.astype(BF16)
    lo = (rest - mid.astype(F32)).astype(BF16)
    return jnp.concatenate([hi, mid, lo], axis=0)


def _finish_rows(merged_bf, x, wout_ref, g2_ref, wr_ref, br_ref, xmid_ref, xn2_ref, lg_ref):
    xm = x + jnp.dot(merged_bf, wout_ref[...], preferred_element_type=F32)
    xmid_ref[...] = xm
    xn = _rms(xm, g2_ref[...])
    x_hi = xn.astype(BF16)
    x_lo = (xn - x_hi.astype(F32)).astype(BF16)
    w_hl = wr_ref[...]
    p_hi = jnp.dot(x_hi, w_hl, preferred_element_type=F32)
    p_lo = jnp.dot(x_lo, w_hl[:, :N_EXPERTS], preferred_element_type=F32)
    lg = p_hi[:, :N_EXPERTS] + (p_hi[:, N_EXPERTS:] + p_lo) + br_ref[...]
    wide = jnp.concatenate([lg, jnp.zeros((lg.shape[0], LANES - N_EXPERTS), F32)], axis=1)
    lg_ref[...] = wide.T[:N_EXPERTS]
    xn2_ref[...] = _pack_bf16_pair(xn[:, :D_MODEL // 2], xn[:, D_MODEL // 2:])


def _mix_body(sinks_ref, q_ref, k_ref, kp_ref, v_ref, vp_ref, a_ref, vn_ref, sgb_ref, x_ref,
              wsp_ref, bsp_ref, wout_ref, g2_ref, wr_ref, br_ref,
              xmid_ref, xn2_ref, lg_ref, kcat, vcat, mrg, key_s):
    i = pl.program_id(0)
    nsub = MIX_ROWS // WINDOW
    kcat[0:WINDOW] = kp_ref[...]
    kcat[WINDOW:] = k_ref[...]
    vcat[0:WINDOW] = vp_ref[...]
    vcat[WINDOW:] = v_ref[...]

    pair_rows = (GQA_GROUP // 2) * WINDOW
    lane_kv = lax.broadcasted_iota(jnp.int32, (2 * WINDOW, LANES), 1)
    lane_o = lax.broadcasted_iota(jnp.int32, (pair_rows, LANES), 1)

    @pl.when(i == 0)
    def _():
        rq = lax.broadcasted_iota(jnp.int32, (pair_rows, 4 * WINDOW), 0) & (WINDOW - 1)
        ck = lax.broadcasted_iota(jnp.int32, (pair_rows, 4 * WINDOW), 1) & (2 * WINDOW - 1)
        key_s[...] = jnp.where((ck > rq) & (ck <= rq + WINDOW), ck, -1)

    row_p = lax.broadcasted_iota(jnp.int32, (pair_rows, 1), 0) >> 7
    sink_cols = []
    for kk in range(N_KV_HEADS):
        h0 = kk * GQA_GROUP
        se = jnp.full((pair_rows, 1), sinks_ref[h0], F32)
        so = jnp.full((pair_rows, 1), sinks_ref[h0 + 1], F32)
        for p in range(1, GQA_GROUP // 2):
            se = jnp.where(row_p == p, sinks_ref[h0 + 2 * p], se)
            so = jnp.where(row_p == p, sinks_ref[h0 + 2 * p + 1], so)
        sink_cols.append((se, so))

    def sub(j, carry):
        off = pl.multiple_of(j * WINDOW, WINDOW)
        rows = pl.ds(off, WINDOW)
        for g in range(GMLP_GROUPS):
            cols = slice(g * LANES, (g + 1) * LANES)
            s = jnp.dot(wsp_ref[g], vn_ref[rows, cols].astype(BF16), preferred_element_type=F32) + bsp_ref[g]
            mrg[rows, cols] = a_ref[rows, cols] * s
        kblk = kcat[pl.ds(off, 2 * WINDOW), :]
        vblk = vcat[pl.ds(off, 2 * WINDOW), :]
        kswp = pltpu.roll(kblk, HEAD_DIM, 1)
        vswp = pltpu.roll(vblk, HEAD_DIM, 1)
        kmin = jnp.where(jnp.logical_and(i == 0, j == 0), WINDOW, 0)
        allowed = key_s[...] >= kmin
        for kk in range(N_KV_HEADS):
            lo_src, hi_src = (kblk, kswp) if kk == 0 else (kswp, kblk)
            kbd = jnp.concatenate([jnp.where(lane_kv < HEAD_DIM, lo_src, 0.0),
                                   jnp.where(lane_kv >= HEAD_DIM, hi_src, 0.0)], axis=0).astype(BF16)
            lo_src, hi_src = (vblk, vswp) if kk == 0 else (vswp, vblk)
            vbd = jnp.concatenate([jnp.where(lane_kv < HEAD_DIM, lo_src, 0.0),
                                   jnp.where(lane_kv >= HEAD_DIM, hi_src, 0.0)], axis=0).astype(BF16)
            pair0 = kk * (GQA_GROUP // 2)
            qs = jnp.concatenate([q_ref[rows, (pair0 + p) * LANES:(pair0 + p + 1) * LANES]
                                  for p in range(GQA_GROUP // 2)], axis=0)
            lg = lax.dot_general(qs, kbd, (((1,), (1,)), ((), ())), preferred_element_type=F32)
            lg = jnp.where(allowed, lg, NEG_INF)
            se, so = sink_cols[kk]
            le, lo = lg[:, :2 * WINDOW], lg[:, 2 * WINDOW:]
            me = jnp.maximum(jnp.max(le, axis=1, keepdims=True), se)
            mo = jnp.maximum(jnp.max(lo, axis=1, keepdims=True), so)
            pe = jnp.exp(le - me)
            po = jnp.exp(lo - mo)
            de = jnp.sum(pe, axis=1, keepdims=True) + jnp.exp(se - me)
            do = jnp.sum(po, axis=1, keepdims=True) + jnp.exp(so - mo)
            pr = jnp.concatenate([pe, po], axis=1).astype(BF16)
            o = jnp.dot(pr, vbd, preferred_element_type=F32)
            o = o / jnp.where(lane_o < HEAD_DIM, de, do)
            for p in range(GQA_GROUP // 2):
                cols = slice((pair0 + p) * LANES, (pair0 + p + 1) * LANES)
                mrg[rows, cols] += sgb_ref[rows, cols] * o[p * WINDOW:(p + 1) * WINDOW]
        return carry

    lax.fori_loop(0, nsub, sub, 0)
    _finish_rows(mrg[...].astype(BF16), x_ref[...], wout_ref, g2_ref, wr_ref, br_ref, xmid_ref, xn2_ref, lg_ref)


def _mix(sinks, q, k, v, a, vn, sgb, x, wsp, bsp, wout, g2, wr, br):
    n = x.shape[0]
    tm = MIX_ROWS
    nsub = tm // WINDOW
    row = lambda w: pl.BlockSpec((tm, w), lambda i: (i, 0))
    prev = pl.BlockSpec((WINDOW, KV_WIDTH), lambda i: (jnp.maximum(i * nsub - 1, 0), 0))
    full = lambda arr: pl.BlockSpec(arr.shape, lambda i: (0,) * arr.ndim)
    smem = pl.BlockSpec(memory_space=pltpu.SMEM)
    return pl.pallas_call(
        _mix_body,
        grid=(n // tm,),
        in_specs=[smem, row(Q_WIDTH), row(KV_WIDTH), prev, row(KV_WIDTH), prev,
                  row(GMLP_WIDTH), row(GMLP_WIDTH), row(D_MODEL), row(D_MODEL),
                  full(wsp), full(bsp), full(wout), full(g2), full(wr), full(br)],
        out_specs=[row(D_MODEL), row(D_MODEL // 2), pl.BlockSpec((N_EXPERTS, tm), lambda i: (0, i))],
        out_shape=[jax.ShapeDtypeStruct((n, D_MODEL), F32),
                   jax.ShapeDtypeStruct((n, D_MODEL // 2), jnp.uint32),
                   jax.ShapeDtypeStruct((N_EXPERTS, n), F32)],
        scratch_shapes=[pltpu.VMEM((tm + WINDOW, KV_WIDTH), F32),
                        pltpu.VMEM((tm + WINDOW, KV_WIDTH), F32),
                        pltpu.VMEM((tm, D_MODEL), F32),
                        pltpu.VMEM((GQA_GROUP // 2 * WINDOW, 4 * WINDOW), jnp.int32)],
        compiler_params=_params(("arbitrary",)),
        name="mix_prompt",
    )(sinks, q, k, k, v, v, a, vn, sgb, x, wsp, bsp, wout, g2, wr, br)


def _sample_attn_body(sink_ref, q_ref, kc_ref, vc_ref, kn_ref, vn_ref, o_ref):
    q = q_ref[...]
    nq, nc, nn = q.shape[1], kc_ref.shape[2], kn_ref.shape[1]
    row_c = lax.broadcasted_iota(jnp.int32, (1, nq, nc), 1)
    row_n = lax.broadcasted_iota(jnp.int32, (1, nq, nn), 1)
    row_o = lax.broadcasted_iota(jnp.int32, (1, nq, HEAD_DIM), 1)
    first_kv = lambda row: (row % N_HEADS) < GQA_GROUP
    cached = lambda ref, kk: ref[:, kk * HEAD_DIM:(kk + 1) * HEAD_DIM, :].astype(BF16)
    fresh = lambda ref, kk: ref[:, :, kk * HEAD_DIM:(kk + 1) * HEAD_DIM].astype(BF16)
    lg_c = jnp.where(first_kv(row_c),
                     jnp.einsum("bqd,bdk->bqk", q, cached(kc_ref, 0), preferred_element_type=F32),
                     jnp.einsum("bqd,bdk->bqk", q, cached(kc_ref, 1), preferred_element_type=F32))
    lg_n = jnp.where(first_kv(row_n),
                     jnp.einsum("bqd,bkd->bqk", q, fresh(kn_ref, 0), preferred_element_type=F32),
                     jnp.einsum("bqd,bkd->bqk", q, fresh(kn_ref, 1), preferred_element_type=F32))
    j_c = lax.broadcasted_iota(jnp.int32, (1, nq, nc), 2)
    j_n = lax.broadcasted_iota(jnp.int32, (1, nq, nn), 2)
    lg_c = jnp.where(j_c > row_c // N_HEADS + (nc - WINDOW), lg_c, NEG_INF)
    lg_n = jnp.where(j_n <= row_n // N_HEADS, lg_n, NEG_INF)
    sink = sink_ref[...][None]
    m = jnp.maximum(jnp.maximum(jnp.max(lg_c, axis=2, keepdims=True), jnp.max(lg_n, axis=2, keepdims=True)), sink)
    p_c = jnp.exp(lg_c - m)
    p_n = jnp.exp(lg_n - m)
    den = jnp.sum(p_c, axis=2, keepdims=True) + jnp.sum(p_n, axis=2, keepdims=True) + jnp.exp(sink - m)
    pb_c, pb_n = p_c.astype(BF16), p_n.astype(BF16)
    heads_out = lambda kk: (jnp.einsum("bqk,bdk->bqd", pb_c, cached(vc_ref, kk), preferred_element_type=F32)
                            + jnp.einsum("bqk,bkd->bqd", pb_n, fresh(vn_ref, kk), preferred_element_type=F32))
    o_ref[...] = jnp.where(first_kv(row_o), heads_out(0), heads_out(1)) / den


def _sample_attn(sink_col, q3, k_cache_t, v_cache_t, k_new, v_new):
    nb = q3.shape[0]
    bb = 32
    blk = lambda a: pl.BlockSpec((bb,) + a.shape[1:], lambda b: (b, 0, 0))
    args = (q3, k_cache_t, v_cache_t, k_new, v_new)
    return pl.pallas_call(
        _sample_attn_body,
        grid=(nb // bb,),
        in_specs=[pl.BlockSpec(sink_col.shape, lambda b: (0, 0))] + [blk(a) for a in args],
        out_specs=blk(q3),
        out_shape=jax.ShapeDtypeStruct(q3.shape, F32),
        compiler_params=_params(("arbitrary",)),
        name="attn_sample",
    )(sink_col, *args)


def _window_body(c_ref, n_ref, o_ref):
    bb, _, w = c_ref.shape
    ts = n_ref.shape[1] // bb
    new = n_ref[...]
    lane = lax.broadcasted_iota(jnp.int32, new.shape, 1)
    for b in range(bb):
        old = pltpu.roll(c_ref[b], w - ts, 1)
        o_ref[b] = jnp.where(lane >= w - ts, pltpu.roll(new, (w - ts - b * ts) % w, 1), old)


def _window(cache_t, new_t):
    nb, ch, w = cache_t.shape
    ts = new_t.shape[1] // nb
    bb = LANES // ts
    assert w == LANES and bb * ts == LANES and nb % bb == 0
    return pl.pallas_call(
        _window_body,
        grid=(nb // bb,),
        in_specs=[pl.BlockSpec((bb, ch, w), lambda b: (b, 0, 0)), pl.BlockSpec((ch, LANES), lambda b: (0, b))],
        out_specs=pl.BlockSpec((bb, ch, w), lambda b: (b, 0, 0)),
        out_shape=jax.ShapeDtypeStruct(cache_t.shape, cache_t.dtype),
        compiler_params=_params(("arbitrary",)),
        name="window",
    )(cache_t, new_t)


def _mix_sample_body(a_ref, vn_ref, sgb_ref, o_ref, x_ref, coef_ref, bias_ref,
                     wout_ref, g2_ref, wr_ref, br_ref, xmid_ref, xn2_ref, lg_ref):
    vn = vn_ref[...]
    n, width = vn.shape
    rows8 = lambda t: t.reshape(n // 8, 8, width)
    s = bias_ref[...][None] + coef_ref[0][None] * rows8(vn)
    for d in range(1, coef_ref.shape[0]):
        s = s + coef_ref[d][None] * rows8(pltpu.roll(vn, d, 0))
    merged = a_ref[...] * s.reshape(n, width) + sgb_ref[...] * o_ref[...]
    _finish_rows(merged.astype(BF16), x_ref[...], wout_ref, g2_ref, wr_ref, br_ref, xmid_ref, xn2_ref, lg_ref)


def _mix_sample(a, vn, sgb, o, x, coef, bias, wout, g2, wr, br):
    n = x.shape[0]
    args = (a, vn, sgb, o, x, coef, bias, wout, g2, wr, br)
    full = lambda arr: pl.BlockSpec(arr.shape, lambda i: (0,) * arr.ndim)
    return pl.pallas_call(
        _mix_sample_body,
        grid=(1,),
        in_specs=[full(arr) for arr in args],
        out_specs=[pl.BlockSpec((n, D_MODEL), lambda i: (0, 0)), pl.BlockSpec((n, D_MODEL // 2), lambda i: (0, 0)),
                   pl.BlockSpec((N_EXPERTS, n), lambda i: (0, 0))],
        out_shape=[jax.ShapeDtypeStruct((n, D_MODEL), F32),
                   jax.ShapeDtypeStruct((n, D_MODEL // 2), jnp.uint32),
                   jax.ShapeDtypeStruct((N_EXPERTS, n), F32)],
        compiler_params=_params(("arbitrary",)),
        name="mix_sample",
    )(*args)


def _rows8(rows, dtype):
    n = rows[0].shape[1]
    sub = lax.broadcasted_iota(jnp.int32, (8, n), 0)
    out = jnp.zeros((8, n), dtype)
    for kx, r in enumerate(rows):
        out = jnp.where(sub == kx, r.astype(dtype), out)
    return out


def _route_body(nblk, reserve, lg_ref, prior_ref, gate_ref, dest_ref, meta_ref, idx_s, rank_s, base, earlier_s):
    i = pl.program_id(0)
    tb = lg_ref.shape[1]

    @pl.when(i == 0)
    def _():
        base[...] = jnp.zeros_like(base)
        earlier_s[...] = (lax.broadcasted_iota(jnp.int32, (tb, tb), 0) <
                          lax.broadcasted_iota(jnp.int32, (tb, tb), 1)).astype(BF16)

    l = lg_ref[...]
    sub = lax.broadcasted_iota(jnp.int32, l.shape, 0).astype(F32)
    vals, idxs, sels = [], [], []
    for _ in range(TOP_K):
        m = jnp.max(l, axis=0, keepdims=True)
        ik = jnp.min(jnp.where(l == m, sub, float(N_EXPERTS)), axis=0, keepdims=True)
        sel = sub == ik
        l = jnp.where(sel, -jnp.inf, l)
        vals.append(m)
        idxs.append(ik)
        sels.append(sel)
    es = [jnp.exp(vk - vals[0]) for vk in vals]
    den = es[0] + es[1] + es[2] + es[3]
    onehot = jnp.zeros(l.shape, F32)
    for sel in sels:
        onehot = onehot + sel.astype(F32)
    before = jnp.dot(onehot.astype(BF16), earlier_s[...], preferred_element_type=F32) + base[...]
    ranks = [jnp.sum(jnp.where(sel, before, 0.0), axis=0, keepdims=True) for sel in sels]
    base[...] += jnp.sum(onehot, axis=1, keepdims=True)
    idx_s[i] = _rows8(idxs, F32)
    rank_s[i] = _rows8(ranks, F32)
    gates = jnp.concatenate([_rows8([e / den for e in es], F32), jnp.zeros((LANES - 8, tb), F32)], axis=0)
    gate_ref[...] = gates.T

    @pl.when(i == nblk - 1)
    def _():
        cnt = base[...]
        if reserve is None:
            seg0 = prior_ref[:, 0:1]
            placed = prior_ref[:, 1:2]
        else:
            cap = jnp.ceil((cnt + reserve) / EXPERT_ROWS) * EXPERT_ROWS
            lower = (lax.broadcasted_iota(jnp.int32, (N_EXPERTS, N_EXPERTS), 1) <
                     lax.broadcasted_iota(jnp.int32, (N_EXPERTS, N_EXPERTS), 0)).astype(F32)
            seg0 = jnp.dot(lower, jnp.broadcast_to(cap, (N_EXPERTS, LANES)), preferred_element_type=F32,
                           precision=lax.Precision.HIGHEST)[:, :1]
            placed = jnp.zeros_like(cnt)
        total = placed + cnt
        lane = lax.broadcasted_iota(jnp.int32, (N_EXPERTS, LANES), 1)
        meta_ref[...] = jnp.where(lane == 0, seg0, jnp.where(lane == 1, total, jnp.where(
            lane == 2, seg0 / EXPERT_ROWS, jnp.where(lane == 3, jnp.ceil(total / EXPERT_ROWS), 0.0))))
        first = seg0 + placed
        sub_e = lax.broadcasted_iota(jnp.int32, (N_EXPERTS, tb), 0).astype(F32)
        for b in range(nblk):
            idx, rank = idx_s[b], rank_s[b]
            rows = [jnp.sum(jnp.where(sub_e == idx[kx:kx + 1], first, 0.0), axis=0, keepdims=True)
                    + rank[kx:kx + 1] for kx in range(TOP_K)]
            dest_ref[:, b * tb:(b + 1) * tb] = _rows8(rows, jnp.int32)


def _route(logits, prior, reserve):
    n = logits.shape[1]
    tb = min(n, ROUTE_ROWS)
    nblk = n // tb
    assert nblk * tb == n
    table = pl.BlockSpec((N_EXPERTS, LANES), lambda i: (0, 0))
    return pl.pallas_call(
        functools.partial(_route_body, nblk, reserve),
        grid=(nblk,),
        in_specs=[pl.BlockSpec((N_EXPERTS, tb), lambda i: (0, i)), table],
        out_specs=[pl.BlockSpec((tb, LANES), lambda i: (i, 0)), pl.BlockSpec((8, n), lambda i: (0, 0)), table],
        out_shape=[jax.ShapeDtypeStruct((n, LANES), F32),
                   jax.ShapeDtypeStruct((8, n), jnp.int32),
                   jax.ShapeDtypeStruct((N_EXPERTS, LANES), F32)],
        scratch_shapes=[pltpu.VMEM((nblk, 8, tb), F32), pltpu.VMEM((nblk, 8, tb), F32),
                        pltpu.VMEM((N_EXPERTS, 1), F32), pltpu.VMEM((tb, tb), BF16)],
        compiler_params=_params(("arbitrary",)),
        name="route",
    )(logits, prior)


def _sc_mesh():
    return plsc.VectorSubcoreMesh(core_axis_name="c", subcore_axis_name="s")


def _sc_worker():
    return lax.axis_index("s") * SC_CORES + lax.axis_index("c")


def _sc_dispatch(x, dest_t, n_slots):
    chunk = SC_ROWS
    n = x.shape[0]
    per_w = n // (SC_WORKERS * chunk)
    assert per_w * SC_WORKERS * chunk == n and per_w % 2 == 0
    d3 = dest_t.reshape(dest_t.shape[0], n // chunk, chunk)
    width, dtype = x.shape[1], x.dtype

    @functools.partial(
        pl.kernel, mesh=_sc_mesh(),
        out_type=jax.ShapeDtypeStruct((n_slots, width), dtype),
        scratch_types=[pltpu.VMEM((TOP_K, per_w, chunk), jnp.int32),
                       pltpu.VMEM((2, chunk, width), dtype),
                       pltpu.SemaphoreType.DMA, pltpu.SemaphoreType.DMA],
        compiler_params=pltpu.CompilerParams(use_tc_tiling_on_sc=True),
        name="dispatch")
    def run(x_hbm, d_hbm, out_hbm, idx_v, rows_v, rsem, wsem):
        wid = _sc_worker()
        pltpu.sync_copy(d_hbm.at[pl.ds(0, TOP_K), pl.ds(wid * per_w, per_w)], idx_v)

        def read(j, slot):
            return pltpu.make_async_copy(x_hbm.at[pl.ds((wid * per_w + j) * chunk, chunk)], rows_v.at[slot], rsem)

        def scatter(j, slot):
            copies = [pltpu.async_copy(rows_v.at[slot], out_hbm.at[idx_v.at[kx, j]], wsem) for kx in range(TOP_K)]
            for cp in copies:
                cp.wait()

        read(0, 0).start()

        def body(h, carry):
            j = 2 * h
            read(j, 0).wait()
            read(j + 1, 1).start()
            scatter(j, 0)
            read(j + 1, 1).wait()

            @pl.when(j + 2 < per_w)
            def _():
                read(j + 2, 0).start()

            scatter(j + 1, 1)
            return carry

        lax.fori_loop(0, per_w // 2, body, 0)

    return run(x, d3)


def _dispatch_small_body(dest_ref, x_ref, _, out_hbm, sem):
    n = x_ref.shape[0]

    def body(t, carry):
        for kx in range(TOP_K):
            pltpu.make_async_copy(x_ref.at[pl.ds(t, 1)], out_hbm.at[pl.ds(dest_ref[kx, t], 1)], sem).start()
        return carry

    lax.fori_loop(0, n, body, 0, unroll=4)
    for _ in range(TOP_K):
        pltpu.make_async_copy(x_ref, out_hbm.at[pl.ds(0, n)], sem).wait()


def _dispatch_small(x, dest_t, x_sorted):
    return pl.pallas_call(
        _dispatch_small_body,
        grid=(1,),
        in_specs=[pl.BlockSpec(memory_space=pltpu.SMEM), pl.BlockSpec(x.shape, lambda i: (0, 0)),
                  pl.BlockSpec(memory_space=pl.ANY)],
        out_specs=pl.BlockSpec(memory_space=pl.ANY),
        out_shape=jax.ShapeDtypeStruct(x_sorted.shape, x_sorted.dtype),
        scratch_shapes=[pltpu.SemaphoreType.DMA],
        input_output_aliases={2: 0},
        compiler_params=_params(("arbitrary",)),
        name="dispatch_small",
    )(dest_t, x, x_sorted)


def _sc_collect(y_sorted, dest_t, tok0, n, small=None):
    chunk = SC_ROWS
    per_choice = SC_WORKERS // TOP_K
    per_w = n // (per_choice * chunk)
    assert per_w * per_choice * chunk == n and per_w % 2 == 0 and tok0 % chunk == 0
    d3 = dest_t.reshape(dest_t.shape[0], dest_t.shape[1] // chunk, chunk)
    width, dtype = y_sorted.shape[1], y_sorted.dtype
    out_type = [jax.ShapeDtypeStruct((TOP_K * n, width), dtype)]
    if small is not None:
        small0, n_small = small
        assert n_small == per_choice * chunk and small0 % chunk == 0
        out_type.append(jax.ShapeDtypeStruct((TOP_K * n_small, width), dtype))

    @functools.partial(
        pl.kernel, mesh=_sc_mesh(),
        out_type=out_type,
        scratch_types=[pltpu.VMEM((per_w, chunk), jnp.int32),
                       pltpu.VMEM((1, chunk), jnp.int32),
                       pltpu.VMEM((2, chunk, width), dtype),
                       pltpu.SemaphoreType.DMA, pltpu.SemaphoreType.DMA],
        compiler_params=pltpu.CompilerParams(use_tc_tiling_on_sc=True),
        name="collect")
    def run(y_hbm, d_hbm, *rest):
        outs, (ip_v, is_v, rows_v, gsem, wsem) = rest[:len(out_type)], rest[len(out_type):]
        op_hbm = outs[0]
        wid = _sc_worker()
        choice = wid // per_choice
        part = wid % per_choice
        pltpu.sync_copy(d_hbm.at[choice, pl.ds(tok0 // chunk + part * per_w, per_w)], ip_v)

        def gather(idx_v, j, slot):
            return pltpu.make_async_copy(y_hbm.at[idx_v.at[j]], rows_v.at[slot], gsem)

        def write(j, slot):
            return pltpu.make_async_copy(rows_v.at[slot], op_hbm.at[pl.ds((wid * per_w + j) * chunk, chunk)], wsem)

        gather(ip_v, 0, 0).start()

        def body(h, carry):
            j = 2 * h
            gather(ip_v, j, 0).wait()

            @pl.when(h > 0)
            def _():
                write(j - 1, 1).wait()

            gather(ip_v, j + 1, 1).start()
            write(j, 0).start()
            gather(ip_v, j + 1, 1).wait()
            write(j, 0).wait()

            @pl.when(j + 2 < per_w)
            def _():
                gather(ip_v, j + 2, 0).start()

            write(j + 1, 1).start()
            return carry

        lax.fori_loop(0, per_w // 2, body, 0)
        write(per_w - 1, 1).wait()

        if small is not None:
            pltpu.sync_copy(d_hbm.at[choice, pl.ds(small0 // chunk + part, 1)], is_v)
            gather(is_v, 0, 0).start()
            gather(is_v, 0, 0).wait()
            pltpu.sync_copy(rows_v.at[0], outs[1].at[pl.ds(wid * chunk, chunk)])

    return run(y_sorted, d3)


def _expert_body(blk0_ref, nblk_ref, cnt_ref, wup_hbm, wdn_hbm, bup_ref, bdn_ref, x_hbm, y_hbm,
                 wup_f, wdn_f, wup_s, wdn_s, xbuf, obuf, w_sem, in_sem, out_sem):
    e = pl.program_id(0)
    nb = nblk_ref[e]
    blk0 = blk0_ref[e]
    cnt = cnt_ref[e]
    tm = EXPERT_ROWS
    pair = 2 * LANES
    wslot = e % 2
    up_rows = D_MODEL // W_PIECES
    dn_rows = D_MODEL // (W_PIECES // 2)

    def w_piece(hbm, buf, ex, slot, p, rows):
        start = p * rows if isinstance(p, int) else pl.multiple_of(p * rows, rows)
        r = pl.ds(start, rows)
        return pltpu.make_async_copy(hbm.at[ex, r], buf.at[slot, r], w_sem.at[slot])

    def w_start(ex, slot, p):
        w_piece(wup_hbm, wup_f, ex, slot, p, up_rows).start()
        if isinstance(p, int):
            if p < W_PIECES // 2:
                w_piece(wdn_hbm, wdn_f, ex, slot, p, dn_rows).start()
        else:
            @pl.when(p < W_PIECES // 2)
            def _():
                w_piece(wdn_hbm, wdn_f, ex, slot, p, dn_rows).start()

    def w_wait(ex, slot):
        for p in range(W_PIECES):
            w_piece(wup_hbm, wup_f, ex, slot, p, up_rows).wait()
        for p in range(W_PIECES // 2):
            w_piece(wdn_hbm, wdn_f, ex, slot, p, dn_rows).wait()

    @pl.when(e == 0)
    def _():
        for p in range(W_PIECES):
            w_start(0, 0, p)

    w_wait(e, wslot)
    wup_ref = wup_f.at[wslot]
    wdn_ref = wdn_f.at[wslot]
    more = e + 1 < N_EXPERTS

    def x_copy(i, slot):
        rows = pl.ds(pl.multiple_of((blk0 + i) * tm, tm), tm)
        return pltpu.make_async_copy(x_hbm.at[rows], xbuf.at[slot], in_sem.at[slot])

    def y_copy(i, slot):
        rows = pl.ds(pl.multiple_of((blk0 + i) * tm, tm), tm)
        return pltpu.make_async_copy(obuf.at[slot], y_hbm.at[rows], out_sem.at[slot])

    @pl.when(nb > 0)
    def _():
        x_copy(0, 0).start(priority=1)
        r = lax.broadcasted_iota(jnp.int32, (pair, pair), 0)
        c = lax.broadcasted_iota(jnp.int32, (pair, pair), 1)
        perm = (r == jnp.where(c < LANES, 2 * c, 2 * (c - LANES) + 1)).astype(BF16)
        for g in range(2 * D_MODEL // pair):
            cols = slice(g * pair, (g + 1) * pair)
            wup_s[g] = jnp.dot(wup_ref[:, cols].astype(BF16), perm, preferred_element_type=F32).astype(BF16)
        for g in range(D_MODEL // pair):
            wdn_s[g] = wdn_ref[:, g * pair:(g + 1) * pair].astype(BF16)

        def block(i, carry):
            slot = i % 2
            x_copy(i, slot).wait()

            @pl.when(i + 1 < nb)
            def _():
                x_copy(i + 1, 1 - slot).start(priority=1)

            @pl.when(i >= 2)
            def _():
                y_copy(i - 2, slot).wait()

            @pl.when(jnp.logical_and(more, i < W_PIECES))
            def _():
                w_start(e + 1, 1 - wslot, i)

            row = lax.broadcasted_iota(jnp.int32, (tm, 1), 0)
            x = jnp.where(row < cnt - i * tm, _unpack_bf16_pair(xbuf[slot]), 0.0).astype(BF16)
            acts = []
            for g in range(2 * D_MODEL // pair):
                cols = slice(g * pair, (g + 1) * pair)
                h = jnp.dot(x, wup_s[g], preferred_element_type=F32) + bup_ref[0, :, cols]
                glu = jnp.minimum(h[:, :LANES], SWIGLU_LIMIT)
                lin = jnp.clip(h[:, LANES:], -SWIGLU_LIMIT, SWIGLU_LIMIT)
                acts.append((glu * jax.nn.sigmoid(SWIGLU_ALPHA * glu) * (lin + 1.0)).astype(BF16))
            act = jnp.concatenate(acts, axis=1)
            half_groups = D_MODEL // pair // 2
            for g in range(half_groups):
                ys = []
                for gg in (g, g + half_groups):
                    cols = slice(gg * pair, (gg + 1) * pair)
                    ys.append(jnp.dot(act, wdn_s[gg], preferred_element_type=F32) + bdn_ref[0, :, cols])
                obuf[slot, :, g * pair:(g + 1) * pair] = _pack_bf16_pair(ys[0], ys[1])
            y_copy(i, slot).start(priority=1)
            return carry

        lax.fori_loop(0, nb, block, 0)

        @pl.when(nb >= 2)
        def _():
            y_copy(nb - 2, nb % 2).wait()

        y_copy(nb - 1, (nb - 1) % 2).wait()

    for p in range(W_PIECES):
        @pl.when(jnp.logical_and(more, p >= nb))
        def _():
            w_start(e + 1, 1 - wslot, p)


def _experts(blk0, nblk, cnt, x_sorted, w_up, w_down, b_up_grouped, b_down):
    tm = EXPERT_ROWS
    per_expert = lambda a: pl.BlockSpec((1,) + a.shape[1:], lambda e, b0, nb, ct: (e, 0, 0))
    hbm = pl.BlockSpec(memory_space=pl.ANY)
    grid_spec = pltpu.PrefetchScalarGridSpec(
        num_scalar_prefetch=3,
        grid=(N_EXPERTS,),
        in_specs=[hbm, hbm, per_expert(b_up_grouped), per_expert(b_down), hbm],
        out_specs=hbm,
        scratch_shapes=[pltpu.VMEM((2,) + w_up.shape[1:], F32), pltpu.VMEM((2,) + w_down.shape[1:], F32),
                        pltpu.VMEM((2 * D_MODEL // (2 * LANES), D_MODEL, 2 * LANES), BF16),
                        pltpu.VMEM((D_MODEL // (2 * LANES), D_MODEL, 2 * LANES), BF16),
                        pltpu.VMEM((2, tm, x_sorted.shape[1]), x_sorted.dtype),
                        pltpu.VMEM((2, tm, D_MODEL // 2), jnp.uint32),
                        pltpu.SemaphoreType.DMA((2,)), pltpu.SemaphoreType.DMA((2,)), pltpu.SemaphoreType.DMA((2,))],
    )
    return pl.pallas_call(
        _expert_body,
        grid_spec=grid_spec,
        out_shape=jax.ShapeDtypeStruct((x_sorted.shape[0], D_MODEL // 2), jnp.uint32),
        compiler_params=_params(("arbitrary",)),
        name="experts",
    )(blk0, nblk, cnt, w_up, w_down, b_up_grouped, b_down, x_sorted)


def _combine_body(gate_ref, xmid_ref, gfin_ref, y0_ref, y1_ref, y2_ref, y3_ref, out_ref):
    gate = gate_ref[...]
    moe = _unpack_bf16_pair(y0_ref[...]) * gate[:, 0:1]
    for kx, y_ref in enumerate((y1_ref, y2_ref, y3_ref), start=1):
        moe = moe + _unpack_bf16_pair(y_ref[...]) * gate[:, kx:kx + 1]
    out_ref[...] = _rms(xmid_ref[...] + moe, gfin_ref[...])


def _combine_update_body(gate_ref, xmid_ref, gfin_ref, y0_ref, y1_ref, y2_ref, y3_ref, prev_ref, out_ref):
    del prev_ref
    _combine_body(gate_ref, xmid_ref, gfin_ref, y0_ref, y1_ref, y2_ref, y3_ref, out_ref)


def _combine(gates, xmid, tok0, n, gfin, y_rows, out_so_far=None):
    tt = min(n, COMBINE_ROWS)
    nblk = n // tt
    blk0 = tok0 // tt
    assert nblk * tt == n and blk0 * tt == tok0
    choice = lambda kx: pl.BlockSpec((tt, y_rows.shape[1]), lambda i: (i + kx * nblk, 0))
    in_specs = [pl.BlockSpec((tt, LANES), lambda i: (i + blk0, 0)),
                pl.BlockSpec((tt, D_MODEL), lambda i: (i + blk0, 0)),
                pl.BlockSpec((1, D_MODEL), lambda i: (0, 0))] + [choice(kx) for kx in range(TOP_K)]
    args = [gates, xmid, gfin, y_rows, y_rows, y_rows, y_rows]
    body, aliases = _combine_body, {}
    if out_so_far is not None:
        in_specs.append(pl.BlockSpec(memory_space=pl.ANY))
        args.append(out_so_far)
        body, aliases = _combine_update_body, {len(args) - 1: 0}
    return pl.pallas_call(
        body,
        grid=(nblk,),
        in_specs=in_specs,
        out_specs=pl.BlockSpec((tt, D_MODEL), lambda i: (i + blk0, 0)),
        out_shape=jax.ShapeDtypeStruct(xmid.shape, F32),
        input_output_aliases=aliases,
        compiler_params=_params(("arbitrary",)),
        name="combine",
    )(*args)


def kernel(x_prompt, x_sample, cache_k_win, cache_v_win, norm_attn_g, w_in, ln_v_g, ln_v_b, w_spatial, b_spatial,
           attn_sinks, w_out, norm_ffn_g, w_router, b_router, w_up, b_up, w_down, b_down, norm_final_g):
    bp, tp, _ = x_prompt.shape
    bs, ts, _ = x_sample.shape
    w_buf = cache_k_win.shape[2]
    assert bp == 1 and tp % MIX_ROWS == 0 and w_buf == WINDOW and (bs * ts) % PROJ_ROWS == 0 and 8 % ts == 0
    n_p, n_s = bp * tp, bs * ts
    row2 = lambda a: a.reshape(1, -1)

    w_in_bf = w_in[0].astype(BF16)
    w_out_bf = w_out[0].astype(BF16)
    tril = jnp.tril(jnp.ones((CHUNK, CHUNK), dtype=bool))
    wsp = jnp.where(tril[None], w_spatial[0], 0.0)
    wsp_bf = wsp.astype(BF16)
    bsp = jnp.broadcast_to(b_spatial[0][:, :, None], (GMLP_GROUPS, CHUNK, LANES))
    b_up_grouped = b_up[0].reshape(N_EXPERTS, -1, LANES, 2).transpose(0, 1, 3, 2).reshape(N_EXPERTS, 1, -1)
    bd = b_down[0][:, None, :]
    g1, g2, gfin = row2(norm_attn_g[0]), row2(norm_ffn_g[0]), row2(norm_final_g)
    lng, lnb = row2(ln_v_g[0]), row2(ln_v_b[0])
    wr_hi = w_router[0].astype(BF16)
    wr = jnp.concatenate([wr_hi, (w_router[0] - wr_hi.astype(F32)).astype(BF16)], axis=1)
    br = row2(b_router[0])
    sinks = attn_sinks[0]

    xp = x_prompt.reshape(n_p, D_MODEL)
    cs_p = _rotary_inputs(jnp.arange(tp, dtype=jnp.int32))
    q_p, k_p, v_p, a_p, vn_p, sgb_p = _proj(xp, g1, w_in_bf, cs_p, lng, lnb)
    xmid_p, xn2_p, lg_p = _mix(sinks, q_p, k_p, v_p, a_p, vn_p, sgb_p, xp, wsp_bf, bsp, w_out_bf, g2, wr, br)

    xs = x_sample.reshape(n_s, D_MODEL)
    pos_s = PAST_LEN + jnp.arange(ts, dtype=jnp.int32)
    cs_s = _rotary_inputs(jnp.tile(pos_s, bs))
    q_s, k_s, v_s, a_s, vn_s, sgb_s = _proj(xs, g1, w_in_bf, cs_s, lng, lnb)
    keys_minor = lambda t: t.reshape(bs, -1, KV_WIDTH).transpose(0, 2, 1)
    k_cache_t, v_cache_t = keys_minor(cache_k_win[0]), keys_minor(cache_v_win[0])
    pad_new = lambda t: jnp.pad(t.reshape(bs, ts, KV_WIDTH), ((0, 0), (0, (-ts) % 16), (0, 0)))
    sink_col = jnp.tile(sinks, ts).reshape(ts * N_HEADS, 1)
    o_s = _sample_attn(sink_col, q_s.reshape(bs, ts * N_HEADS, HEAD_DIM), k_cache_t, v_cache_t,
                       pad_new(k_s), pad_new(v_s)).reshape(n_s, Q_WIDTH)
    lag = np.arange(ts)[:, None] - np.arange(ts)[None, :]
    coef = jnp.stack([jnp.sum(jnp.where(lag == d, wsp[:, :ts, :ts], 0.0), axis=2)
                      for d in range(ts)])
    coef = jnp.repeat(coef.transpose(0, 2, 1), GMLP_WIDTH // GMLP_GROUPS, axis=2)
    coef = jnp.tile(coef, (1, 8 // ts, 1))
    bias = jnp.tile(jnp.repeat(b_spatial[0][:, :ts].T, GMLP_WIDTH // GMLP_GROUPS, axis=1), (8 // ts, 1))
    xmid_s, xn2_s, lg_s = _mix_sample(a_s, vn_s, sgb_s, o_s, xs, coef, bias, w_out_bf, g2, wr, br)

    tm = EXPERT_ROWS
    n_blocks = (n_p * TOP_K + N_EXPERTS * n_s) // tm + N_EXPERTS
    gate_p, dest_p, table_p = _route(lg_p, jnp.zeros((N_EXPERTS, LANES), F32), n_s)
    x_sorted = _sc_dispatch(xn2_p, dest_p, n_blocks * tm)
    gate_s, dest_s, table = _route(lg_s, table_p, None)
    x_sorted = _dispatch_small(xn2_s, dest_s, x_sorted)
    dest_t = jnp.concatenate([dest_p, dest_s], axis=1)
    meta = table.astype(jnp.int32)
    y_sorted = _experts(meta[:, 2], meta[:, 3], meta[:, 1], x_sorted, w_up[0], w_down[0], b_up_grouped, bd)
    half = n_p // 2
    yrows_a, yrows_s = _sc_collect(y_sorted, dest_t, 0, half, small=(n_p, n_s))
    yrows_b, = _sc_collect(y_sorted, dest_t, half, half)
    y_s = _combine(gate_s, xmid_s, 0, n_s, gfin, yrows_s)
    y_p = _combine(gate_p, xmid_p, 0, half, gfin, yrows_a)
    y_p = _combine(gate_p, xmid_p, half, half, gfin, yrows_b, out_so_far=y_p)

    k4 = lambda t: t.reshape(1, bp, -1, N_KV_HEADS, HEAD_DIM)
    next_window = lambda cache_t, new: _window(cache_t, new.T).transpose(0, 2, 1).reshape(
        1, bs, w_buf, N_KV_HEADS, HEAD_DIM)
    return (y_p.reshape(bp, tp, D_MODEL),
            y_s.reshape(bs, ts, D_MODEL),
            k4(k_p[n_p - WINDOW:]),
            k4(v_p[n_p - WINDOW:]),
            vn_p[n_p - CHUNK:].reshape(1, bp, CHUNK, GMLP_WIDTH),
            next_window(k_cache_t, k_s),
            next_window(v_cache_t, v_s),
            vn_s.reshape(1, bs, ts, GMLP_WIDTH))
```

```python
import functools

import numpy as np
import jax
import jax.numpy as jnp
from jax import lax
from jax.experimental import pallas as pl
from jax.experimental.pallas import tpu as pltpu
from jax.experimental.pallas import tpu_sc as plsc

F32 = jnp.float32
BF16 = jnp.bfloat16

D_MODEL = 1024
HEAD_DIM = 64
N_HEADS = 16
GQA_GROUP = 8
N_KV_HEADS = 2
Q_WIDTH = 1024
KV_WIDTH = 128
WINDOW = 128
ROT_DIM = 16
ROPE_THETA = 500000.0
CHUNK = 128
GMLP_WIDTH = 1024
GMLP_GROUPS = 8
N_EXPERTS = 32
TOP_K = 4
SWIGLU_LIMIT = 7.0
SWIGLU_ALPHA = 1.702
RMS_EPS = 1e-5
LN_EPS = 1e-5
NEG_INF = -1e30
PAST_LEN = 16384

LANES = 128
VMEM_LIMIT = 56 * 1024 * 1024

PROJ_ROWS = 256
MIX_ROWS = 512
ROUTE_ROWS = 1024
EXPERT_ROWS = 256
W_PIECES = 8
COMBINE_ROWS = 1024

SC_CORES = 2
SC_WORKERS = 32
SC_ROWS = 64

_C_Q, _C_KV, _C_U, _C_VG, _C_GA, _C_GB, _C_END = 0, 1024, 1280, 2304, 3328, 4352, 5376


def _params(sem):
    return pltpu.CompilerParams(dimension_semantics=sem, vmem_limit_bytes=VMEM_LIMIT)


def _rms(x, g):
    return x * lax.rsqrt(jnp.mean(x * x, axis=-1, keepdims=True) + RMS_EPS) * g


def _pack_bf16_pair(lo, hi):
    lo_bits = lax.bitcast_convert_type(lo.astype(BF16).astype(F32), jnp.uint32)
    hi_bits = lax.bitcast_convert_type(hi.astype(BF16).astype(F32), jnp.uint32)
    return (lo_bits >> 16) | hi_bits


def _unpack_bf16_pair(words):
    lo = lax.bitcast_convert_type(words << 16, F32)
    hi = lax.bitcast_convert_type(words & jnp.uint32(0xFFFF0000), F32)
    return jnp.concatenate([lo, hi], axis=1)


def _proj_body(x_ref, g_ref, w_ref, cs_ref, rot_ref, lng_ref, lnb_ref,
               q_ref, k_ref, v_ref, a_ref, vn_ref, sgb_ref):
    h = _rms(x_ref[...], g_ref[...]).astype(BF16)
    tabs = lax.dot_general(cs_ref[...], rot_ref[...], (((0,), (0,)), ((), ())), preferred_element_type=F32)
    rc, rs1, rs2 = tabs[:, :LANES], tabs[:, LANES:2 * LANES], tabs[:, 2 * LANES:]

    def rot(z):
        return z * rc + pltpu.roll(z, LANES - ROT_DIM // 2, 1) * rs1 + pltpu.roll(z, ROT_DIM // 2, 1) * rs2

    def mm(lo, hi):
        return jnp.dot(h, w_ref[:, lo:hi], preferred_element_type=F32)

    zq = mm(_C_Q, _C_KV)
    for c in range(Q_WIDTH // LANES):
        sl = slice(c * LANES, (c + 1) * LANES)
        q_ref[:, sl] = (rot(zq[:, sl]) * (HEAD_DIM ** -0.5)).astype(BF16)
    zkv = mm(_C_KV, _C_U)
    k_ref[...] = rot(zkv[:, :KV_WIDTH])
    v_ref[...] = zkv[:, KV_WIDTH:]
    a_ref[...] = jax.nn.sigmoid(mm(_C_GA, _C_GB)) * jax.nn.gelu(mm(_C_U, _C_VG))
    zv = jax.nn.gelu(mm(_C_VG, _C_GA))
    zc = zv - jnp.mean(zv, axis=-1, keepdims=True)
    var = jnp.mean(zc * zc, axis=-1, keepdims=True)
    vn_ref[...] = zc * lax.rsqrt(var + LN_EPS) * lng_ref[...] + lnb_ref[...]
    sgb_ref[...] = jax.nn.sigmoid(mm(_C_GB, _C_END))


def _proj(x, norm_g, w_in_bf, cs, ln_g, ln_b):
    n = x.shape[0]
    tm = PROJ_ROWS
    row = lambda w: pl.BlockSpec((tm, w), lambda i: (i, 0))
    full = lambda a: pl.BlockSpec(a.shape, lambda i: (0,) * a.ndim)
    rot = jnp.asarray(np.tile(_ROT_EXPAND, (3, 1)), dtype=BF16)
    return pl.pallas_call(
        _proj_body,
        grid=(n // tm,),
        in_specs=[row(D_MODEL), full(norm_g), full(w_in_bf), pl.BlockSpec((cs.shape[0], tm), lambda i: (0, i)),
                  full(rot),
                  full(ln_g), full(ln_b)],
        out_specs=[row(Q_WIDTH), row(KV_WIDTH), row(KV_WIDTH), row(GMLP_WIDTH), row(GMLP_WIDTH), row(D_MODEL)],
        out_shape=[jax.ShapeDtypeStruct((n, Q_WIDTH), BF16),
                   jax.ShapeDtypeStruct((n, KV_WIDTH), F32),
                   jax.ShapeDtypeStruct((n, KV_WIDTH), F32),
                   jax.ShapeDtypeStruct((n, GMLP_WIDTH), F32),
                   jax.ShapeDtypeStruct((n, GMLP_WIDTH), F32),
                   jax.ShapeDtypeStruct((n, D_MODEL), F32)],
        compiler_params=_params(("arbitrary",)),
        name="proj",
    )(x, norm_g, w_in_bf, cs, rot, ln_g, ln_b)


_ROT_COLS = 32


def _rot_expand():
    half = ROT_DIM // 2
    m = np.zeros((_ROT_COLS, 3 * LANES), np.float32)
    for lane in range(LANES):
        d = lane % HEAD_DIM
        if d < ROT_DIM:
            m[d % half, lane] = 1.0
        else:
            m[2 * half, lane] = 1.0
        if d < half:
            m[half + d, LANES + lane] = -1.0
        elif d < ROT_DIM:
            m[half + d - half, 2 * LANES + lane] = 1.0
    return m


_ROT_EXPAND = _rot_expand()


def _rotary_inputs(pos):
    half = ROT_DIM // 2
    inv_freq = ROPE_THETA ** (-jnp.arange(half, dtype=F32) / half)
    ang = inv_freq[:, None] * pos.astype(F32)[None, :]
    n = pos.shape[0]
    cs = jnp.concatenate([jnp.cos(ang), jnp.sin(ang), jnp.ones((1, n), F32),
                          jnp.zeros((_ROT_COLS - 2 * half - 1, n), F32)], axis=0)
    hi = cs.astype(BF16)
    rest = cs - hi.astype(F32)
    mid = rest.astype(BF16)
    lo = (rest - mid.astype(F32)).astype(BF16)
    return jnp.concatenate([hi, mid, lo], axis=0)


def _finish_rows(merged_bf, x, wout_ref, g2_ref, wr_ref, br_ref, xmid_ref, xn2_ref, lg_ref):
    xm = x + jnp.dot(merged_bf, wout_ref[...], preferred_element_type=F32)
    xmid_ref[...] = xm
    xn = _rms(xm, g2_ref[...])
    x_hi = xn.astype(BF16)
    x_lo = (xn - x_hi.astype(F32)).astype(BF16)
    w_hl = wr_ref[...]
    p_hi = jnp.dot(x_hi, w_hl, preferred_element_type=F32)
    p_lo = jnp.dot(x_lo, w_hl[:, :N_EXPERTS], preferred_element_type=F32)
    lg = p_hi[:, :N_EXPERTS] + (p_hi[:, N_EXPERTS:] + p_lo) + br_ref[...]
    wide = jnp.concatenate([lg, jnp.zeros((lg.shape[0], LANES - N_EXPERTS), F32)], axis=1)
    lg_ref[...] = wide.T[:N_EXPERTS]
    xn2_ref[...] = _pack_bf16_pair(xn[:, :D_MODEL // 2], xn[:, D_MODEL // 2:])


def _mix_body(sinks_ref, q_ref, k_ref, kp_ref, v_ref, vp_ref, a_ref, vn_ref, sgb_ref, x_ref,
              wsp_ref, bsp_ref, wout_ref, g2_ref, wr_ref, br_ref,
              xmid_ref, xn2_ref, lg_ref, kcat, vcat, mrg, key_s):
    i = pl.program_id(0)
    nsub = MIX_ROWS // WINDOW
    kcat[0:WINDOW] = kp_ref[...]
    kcat[WINDOW:] = k_ref[...]
    vcat[0:WINDOW] = vp_ref[...]
    vcat[WINDOW:] = v_ref[...]

    pair_rows = (GQA_GROUP // 2) * WINDOW
    lane_kv = lax.broadcasted_iota(jnp.int32, (2 * WINDOW, LANES), 1)
    lane_o = lax.broadcasted_iota(jnp.int32, (pair_rows, LANES), 1)

    @pl.when(i == 0)
    def _():
        rq = lax.broadcasted_iota(jnp.int32, (pair_rows, 4 * WINDOW), 0) & (WINDOW - 1)
        ck = lax.broadcasted_iota(jnp.int32, (pair_rows, 4 * WINDOW), 1) & (2 * WINDOW - 1)
        key_s[...] = jnp.where((ck > rq) & (ck <= rq + WINDOW), ck, -1)

    row_p = lax.broadcasted_iota(jnp.int32, (pair_rows, 1), 0) >> 7
    sink_cols = []
    for kk in range(N_KV_HEADS):
        h0 = kk * GQA_GROUP
        se = jnp.full((pair_rows, 1), sinks_ref[h0], F32)
        so = jnp.full((pair_rows, 1), sinks_ref[h0 + 1], F32)
        for p in range(1, GQA_GROUP // 2):
            se = jnp.where(row_p == p, sinks_ref[h0 + 2 * p], se)
            so = jnp.where(row_p == p, sinks_ref[h0 + 2 * p + 1], so)
        sink_cols.append((se, so))

    def sub(j, carry):
        off = pl.multiple_of(j * WINDOW, WINDOW)
        rows = pl.ds(off, WINDOW)
        for g in range(GMLP_GROUPS):
            cols = slice(g * LANES, (g + 1) * LANES)
            s = jnp.dot(wsp_ref[g], vn_ref[rows, cols].astype(BF16), preferred_element_type=F32) + bsp_ref[g]
            mrg[rows, cols] = a_ref[rows, cols] * s
        kblk = kcat[pl.ds(off, 2 * WINDOW), :]
        vblk = vcat[pl.ds(off, 2 * WINDOW), :]
        kswp = pltpu.roll(kblk, HEAD_DIM, 1)
        vswp = pltpu.roll(vblk, HEAD_DIM, 1)
        kmin = jnp.where(jnp.logical_and(i == 0, j == 0), WINDOW, 0)
        allowed = key_s[...] >= kmin
        for kk in range(N_KV_HEADS):
            lo_src, hi_src = (kblk, kswp) if kk == 0 else (kswp, kblk)
            kbd = jnp.concatenate([jnp.where(lane_kv < HEAD_DIM, lo_src, 0.0),
                                   jnp.where(lane_kv >= HEAD_DIM, hi_src, 0.0)], axis=0).astype(BF16)
            lo_src, hi_src = (vblk, vswp) if kk == 0 else (vswp, vblk)
            vbd = jnp.concatenate([jnp.where(lane_kv < HEAD_DIM, lo_src, 0.0),
                                   jnp.where(lane_kv >= HEAD_DIM, hi_src, 0.0)], axis=0).astype(BF16)
            pair0 = kk * (GQA_GROUP // 2)
            qs = jnp.concatenate([q_ref[rows, (pair0 + p) * LANES:(pair0 + p + 1) * LANES]
                                  for p in range(GQA_GROUP // 2)], axis=0)
            lg = lax.dot_general(qs, kbd, (((1,), (1,)), ((), ())), preferred_element_type=F32)
            lg = jnp.where(allowed, lg, NEG_INF)
            se, so = sink_cols[kk]
            le, lo = lg[:, :2 * WINDOW], lg[:, 2 * WINDOW:]
            me = jnp.maximum(jnp.max(le, axis=1, keepdims=True), se)
            mo = jnp.maximum(jnp.max(lo, axis=1, keepdims=True), so)
            pe = jnp.exp(le - me)
            po = jnp.exp(lo - mo)
            de = jnp.sum(pe, axis=1, keepdims=True) + jnp.exp(se - me)
            do = jnp.sum(po, axis=1, keepdims=True) + jnp.exp(so - mo)
            pr = jnp.concatenate([pe, po], axis=1).astype(BF16)
            o = jnp.dot(pr, vbd, preferred_element_type=F32)
            o = o / jnp.where(lane_o < HEAD_DIM, de, do)
            for p in range(GQA_GROUP // 2):
                cols = slice((pair0 + p) * LANES, (pair0 + p + 1) * LANES)
                mrg[rows, cols] += sgb_ref[rows, cols] * o[p * WINDOW:(p + 1) * WINDOW]
        return carry

    lax.fori_loop(0, nsub, sub, 0)
    _finish_rows(mrg[...].astype(BF16), x_ref[...], wout_ref, g2_ref, wr_ref, br_ref, xmid_ref, xn2_ref, lg_ref)


def _mix(sinks, q, k, v, a, vn, sgb, x, wsp, bsp, wout, g2, wr, br):
    n = x.shape[0]
    tm = MIX_ROWS
    nsub = tm // WINDOW
    row = lambda w: pl.BlockSpec((tm, w), lambda i: (i, 0))
    prev = pl.BlockSpec((WINDOW, KV_WIDTH), lambda i: (jnp.maximum(i * nsub - 1, 0), 0))
    full = lambda arr: pl.BlockSpec(arr.shape, lambda i: (0,) * arr.ndim)
    smem = pl.BlockSpec(memory_space=pltpu.SMEM)
    return pl.pallas_call(
        _mix_body,
        grid=(n // tm,),
        in_specs=[smem, row(Q_WIDTH), row(KV_WIDTH), prev, row(KV_WIDTH), prev,
                  row(GMLP_WIDTH), row(GMLP_WIDTH), row(D_MODEL), row(D_MODEL),
                  full(wsp), full(bsp), full(wout), full(g2), full(wr), full(br)],
        out_specs=[row(D_MODEL), row(D_MODEL // 2), pl.BlockSpec((N_EXPERTS, tm), lambda i: (0, i))],
        out_shape=[jax.ShapeDtypeStruct((n, D_MODEL), F32),
                   jax.ShapeDtypeStruct((n, D_MODEL // 2), jnp.uint32),
                   jax.ShapeDtypeStruct((N_EXPERTS, n), F32)],
        scratch_shapes=[pltpu.VMEM((tm + WINDOW, KV_WIDTH), F32),
                        pltpu.VMEM((tm + WINDOW, KV_WIDTH), F32),
                        pltpu.VMEM((tm, D_MODEL), F32),
                        pltpu.VMEM((GQA_GROUP // 2 * WINDOW, 4 * WINDOW), jnp.int32)],
        compiler_params=_params(("arbitrary",)),
        name="mix_prompt",
    )(sinks, q, k, k, v, v, a, vn, sgb, x, wsp, bsp, wout, g2, wr, br)


def _sample_attn_body(sink_ref, q_ref, kc_ref, vc_ref, kn_ref, vn_ref, o_ref):
    q = q_ref[...]
    nq, nc, nn = q.shape[1], kc_ref.shape[2], kn_ref.shape[1]
    row_c = lax.broadcasted_iota(jnp.int32, (1, nq, nc), 1)
    row_n = lax.broadcasted_iota(jnp.int32, (1, nq, nn), 1)
    row_o = lax.broadcasted_iota(jnp.int32, (1, nq, HEAD_DIM), 1)
    first_kv = lambda row: (row % N_HEADS) < GQA_GROUP
    cached = lambda ref, kk: ref[:, kk * HEAD_DIM:(kk + 1) * HEAD_DIM, :].astype(BF16)
    fresh = lambda ref, kk: ref[:, :, kk * HEAD_DIM:(kk + 1) * HEAD_DIM].astype(BF16)
    lg_c = jnp.where(first_kv(row_c),
                     jnp.einsum("bqd,bdk->bqk", q, cached(kc_ref, 0), preferred_element_type=F32),
                     jnp.einsum("bqd,bdk->bqk", q, cached(kc_ref, 1), preferred_element_type=F32))
    lg_n = jnp.where(first_kv(row_n),
                     jnp.einsum("bqd,bkd->bqk", q, fresh(kn_ref, 0), preferred_element_type=F32),
                     jnp.einsum("bqd,bkd->bqk", q, fresh(kn_ref, 1), preferred_element_type=F32))
    j_c = lax.broadcasted_iota(jnp.int32, (1, nq, nc), 2)
    j_n = lax.broadcasted_iota(jnp.int32, (1, nq, nn), 2)
    lg_c = jnp.where(j_c > row_c // N_HEADS + (nc - WINDOW), lg_c, NEG_INF)
    lg_n = jnp.where(j_n <= row_n // N_HEADS, lg_n, NEG_INF)
    sink = sink_ref[...][None]
    m = jnp.maximum(jnp.maximum(jnp.max(lg_c, axis=2, keepdims=True), jnp.max(lg_n, axis=2, keepdims=True)), sink)
    p_c = jnp.exp(lg_c - m)
    p_n = jnp.exp(lg_n - m)
    den = jnp.sum(p_c, axis=2, keepdims=True) + jnp.sum(p_n, axis=2, keepdims=True) + jnp.exp(sink - m)
    pb_c, pb_n = p_c.astype(BF16), p_n.astype(BF16)
    heads_out = lambda kk: (jnp.einsum("bqk,bdk->bqd", pb_c, cached(vc_ref, kk), preferred_element_type=F32)
                            + jnp.einsum("bqk,bkd->bqd", pb_n, fresh(vn_ref, kk), preferred_element_type=F32))
    o_ref[...] = jnp.where(first_kv(row_o), heads_out(0), heads_out(1)) / den


def _sample_attn(sink_col, q3, k_cache_t, v_cache_t, k_new, v_new):
    nb = q3.shape[0]
    bb = 32
    blk = lambda a: pl.BlockSpec((bb,) + a.shape[1:], lambda b: (b, 0, 0))
    args = (q3, k_cache_t, v_cache_t, k_new, v_new)
    return pl.pallas_call(
        _sample_attn_body,
        grid=(nb // bb,),
        in_specs=[pl.BlockSpec(sink_col.shape, lambda b: (0, 0))] + [blk(a) for a in args],
        out_specs=blk(q3),
        out_shape=jax.ShapeDtypeStruct(q3.shape, F32),
        compiler_params=_params(("arbitrary",)),
        name="attn_sample",
    )(sink_col, *args)


def _window_body(c_ref, n_ref, o_ref):
    bb, _, w = c_ref.shape
    ts = n_ref.shape[1] // bb
    new = n_ref[...]
    lane = lax.broadcasted_iota(jnp.int32, new.shape, 1)
    for b in range(bb):
        old = pltpu.roll(c_ref[b], w - ts, 1)
        o_ref[b] = jnp.where(lane >= w - ts, pltpu.roll(new, (w - ts - b * ts) % w, 1), old)


def _window(cache_t, new_t):
    nb, ch, w = cache_t.shape
    ts = new_t.shape[1] // nb
    bb = LANES // ts
    assert w == LANES and bb * ts == LANES and nb % bb == 0
    return pl.pallas_call(
        _window_body,
        grid=(nb // bb,),
        in_specs=[pl.BlockSpec((bb, ch, w), lambda b: (b, 0, 0)), pl.BlockSpec((ch, LANES), lambda b: (0, b))],
        out_specs=pl.BlockSpec((bb, ch, w), lambda b: (b, 0, 0)),
        out_shape=jax.ShapeDtypeStruct(cache_t.shape, cache_t.dtype),
        compiler_params=_params(("arbitrary",)),
        name="window",
    )(cache_t, new_t)


def _mix_sample_body(a_ref, vn_ref, sgb_ref, o_ref, x_ref, coef_ref, bias_ref,
                     wout_ref, g2_ref, wr_ref, br_ref, xmid_ref, xn2_ref, lg_ref):
    vn = vn_ref[...]
    n, width = vn.shape
    rows8 = lambda t: t.reshape(n // 8, 8, width)
    s = bias_ref[...][None] + coef_ref[0][None] * rows8(vn)
    for d in range(1, coef_ref.shape[0]):
        s = s + coef_ref[d][None] * rows8(pltpu.roll(vn, d, 0))
    merged = a_ref[...] * s.reshape(n, width) + sgb_ref[...] * o_ref[...]
    _finish_rows(merged.astype(BF16), x_ref[...], wout_ref, g2_ref, wr_ref, br_ref, xmid_ref, xn2_ref, lg_ref)


def _mix_sample(a, vn, sgb, o, x, coef, bias, wout, g2, wr, br):
    n = x.shape[0]
    args = (a, vn, sgb, o, x, coef, bias, wout, g2, wr, br)
    full = lambda arr: pl.BlockSpec(arr.shape, lambda i: (0,) * arr.ndim)
    return pl.pallas_call(
        _mix_sample_body,
        grid=(1,),
        in_specs=[full(arr) for arr in args],
        out_specs=[pl.BlockSpec((n, D_MODEL), lambda i: (0, 0)), pl.BlockSpec((n, D_MODEL // 2), lambda i: (0, 0)),
                   pl.BlockSpec((N_EXPERTS, n), lambda i: (0, 0))],
        out_shape=[jax.ShapeDtypeStruct((n, D_MODEL), F32),
                   jax.ShapeDtypeStruct((n, D_MODEL // 2), jnp.uint32),
                   jax.ShapeDtypeStruct((N_EXPERTS, n), F32)],
        compiler_params=_params(("arbitrary",)),
        name="mix_sample",
    )(*args)


def _rows8(rows, dtype):
    n = rows[0].shape[1]
    sub = lax.broadcasted_iota(jnp.int32, (8, n), 0)
    out = jnp.zeros((8, n), dtype)
    for kx, r in enumerate(rows):
        out = jnp.where(sub == kx, r.astype(dtype), out)
    return out


def _route_body(nblk, reserve, lg_ref, prior_ref, gate_ref, dest_ref, meta_ref, idx_s, rank_s, base):
    i = pl.program_id(0)

    @pl.when(i == 0)
    def _():
        base[...] = jnp.zeros_like(base)

    l = lg_ref[...]
    tb = l.shape[1]
    sub = lax.broadcasted_iota(jnp.int32, l.shape, 0).astype(F32)
    vals, idxs, sels = [], [], []
    for _ in range(TOP_K):
        m = jnp.max(l, axis=0, keepdims=True)
        ik = jnp.min(jnp.where(l == m, sub, float(N_EXPERTS)), axis=0, keepdims=True)
        sel = sub == ik
        l = jnp.where(sel, -jnp.inf, l)
        vals.append(m)
        idxs.append(ik)
        sels.append(sel)
    es = [jnp.exp(vk - vals[0]) for vk in vals]
    den = es[0] + es[1] + es[2] + es[3]
    onehot = jnp.zeros(l.shape, F32)
    for sel in sels:
        onehot = onehot + sel.astype(F32)
    earlier = (lax.broadcasted_iota(jnp.int32, (tb, tb), 0) < lax.broadcasted_iota(jnp.int32, (tb, tb), 1))
    before = jnp.dot(onehot.astype(BF16), earlier.astype(BF16), preferred_element_type=F32) + base[...]
    ranks = [jnp.sum(jnp.where(sel, before, 0.0), axis=0, keepdims=True) for sel in sels]
    base[...] += jnp.sum(onehot, axis=1, keepdims=True)
    idx_s[i] = _rows8(idxs, F32)
    rank_s[i] = _rows8(ranks, F32)
    gates = jnp.concatenate([_rows8([e / den for e in es], F32), jnp.zeros((LANES - 8, tb), F32)], axis=0)
    gate_ref[...] = gates.T

    @pl.when(i == nblk - 1)
    def _():
        cnt = base[...]
        if reserve is None:
            seg0 = prior_ref[:, 0:1]
            placed = prior_ref[:, 1:2]
        else:
            cap = jnp.ceil((cnt + reserve) / EXPERT_ROWS) * EXPERT_ROWS
            lower = (lax.broadcasted_iota(jnp.int32, (N_EXPERTS, N_EXPERTS), 1) <
                     lax.broadcasted_iota(jnp.int32, (N_EXPERTS, N_EXPERTS), 0)).astype(F32)
            seg0 = jnp.dot(lower, jnp.broadcast_to(cap, (N_EXPERTS, LANES)), preferred_element_type=F32,
                           precision=lax.Precision.HIGHEST)[:, :1]
            placed = jnp.zeros_like(cnt)
        total = placed + cnt
        lane = lax.broadcasted_iota(jnp.int32, (N_EXPERTS, LANES), 1)
        meta_ref[...] = jnp.where(lane == 0, seg0, jnp.where(lane == 1, total, jnp.where(
            lane == 2, seg0 / EXPERT_ROWS, jnp.where(lane == 3, jnp.ceil(total / EXPERT_ROWS), 0.0))))
        first = seg0 + placed
        sub_e = lax.broadcasted_iota(jnp.int32, (N_EXPERTS, tb), 0).astype(F32)
        for b in range(nblk):
            idx, rank = idx_s[b], rank_s[b]
            rows = [jnp.sum(jnp.where(sub_e == idx[kx:kx + 1], first, 0.0), axis=0, keepdims=True)
                    + rank[kx:kx + 1] for kx in range(TOP_K)]
            dest_ref[:, b * tb:(b + 1) * tb] = _rows8(rows, jnp.int32)


def _route(logits, prior, reserve):
    n = logits.shape[1]
    tb = min(n, ROUTE_ROWS)
    nblk = n // tb
    assert nblk * tb == n
    table = pl.BlockSpec((N_EXPERTS, LANES), lambda i: (0, 0))
    return pl.pallas_call(
        functools.partial(_route_body, nblk, reserve),
        grid=(nblk,),
        in_specs=[pl.BlockSpec((N_EXPERTS, tb), lambda i: (0, i)), table],
        out_specs=[pl.BlockSpec((tb, LANES), lambda i: (i, 0)), pl.BlockSpec((8, n), lambda i: (0, 0)), table],
        out_shape=[jax.ShapeDtypeStruct((n, LANES), F32),
                   jax.ShapeDtypeStruct((8, n), jnp.int32),
                   jax.ShapeDtypeStruct((N_EXPERTS, LANES), F32)],
        scratch_shapes=[pltpu.VMEM((nblk, 8, tb), F32), pltpu.VMEM((nblk, 8, tb), F32),
                        pltpu.VMEM((N_EXPERTS, 1), F32)],
        compiler_params=_params(("arbitrary",)),
        name="route",
    )(logits, prior)


def _sc_mesh():
    return plsc.VectorSubcoreMesh(core_axis_name="c", subcore_axis_name="s")


def _sc_worker():
    return lax.axis_index("s") * SC_CORES + lax.axis_index("c")


def _sc_dispatch(x, dest_t, n_slots):
    chunk = SC_ROWS
    n = x.shape[0]
    per_w = n // (SC_WORKERS * chunk)
    assert per_w * SC_WORKERS * chunk == n and per_w % 2 == 0
    d3 = dest_t.reshape(dest_t.shape[0], n // chunk, chunk)
    width, dtype = x.shape[1], x.dtype

    @functools.partial(
        pl.kernel, mesh=_sc_mesh(),
        out_type=jax.ShapeDtypeStruct((n_slots, width), dtype),
        scratch_types=[pltpu.VMEM((TOP_K, per_w, chunk), jnp.int32),
                       pltpu.VMEM((2, chunk, width), dtype),
                       pltpu.SemaphoreType.DMA, pltpu.SemaphoreType.DMA],
        compiler_params=pltpu.CompilerParams(use_tc_tiling_on_sc=True),
        name="dispatch")
    def run(x_hbm, d_hbm, out_hbm, idx_v, rows_v, rsem, wsem):
        wid = _sc_worker()
        pltpu.sync_copy(d_hbm.at[pl.ds(0, TOP_K), pl.ds(wid * per_w, per_w)], idx_v)

        def read(j, slot):
            return pltpu.make_async_copy(x_hbm.at[pl.ds((wid * per_w + j) * chunk, chunk)], rows_v.at[slot], rsem)

        def scatter(j, slot):
            copies = [pltpu.async_copy(rows_v.at[slot], out_hbm.at[idx_v.at[kx, j]], wsem) for kx in range(TOP_K)]
            for cp in copies:
                cp.wait()

        read(0, 0).start()

        def body(h, carry):
            j = 2 * h
            read(j, 0).wait()
            read(j + 1, 1).start()
            scatter(j, 0)
            read(j + 1, 1).wait()

            @pl.when(j + 2 < per_w)
            def _():
                read(j + 2, 0).start()

            scatter(j + 1, 1)
            return carry

        lax.fori_loop(0, per_w // 2, body, 0)

    return run(x, d3)


def _dispatch_small_body(dest_ref, x_ref, _, out_hbm, sem):
    n = x_ref.shape[0]

    def body(t, carry):
        for kx in range(TOP_K):
            pltpu.make_async_copy(x_ref.at[pl.ds(t, 1)], out_hbm.at[pl.ds(dest_ref[kx, t], 1)], sem).start()
        return carry

    lax.fori_loop(0, n, body, 0, unroll=4)
    for _ in range(TOP_K):
        pltpu.make_async_copy(x_ref, out_hbm.at[pl.ds(0, n)], sem).wait()


def _dispatch_small(x, dest_t, x_sorted):
    return pl.pallas_call(
        _dispatch_small_body,
        grid=(1,),
        in_specs=[pl.BlockSpec(memory_space=pltpu.SMEM), pl.BlockSpec(x.shape, lambda i: (0, 0)),
                  pl.BlockSpec(memory_space=pl.ANY)],
        out_specs=pl.BlockSpec(memory_space=pl.ANY),
        out_shape=jax.ShapeDtypeStruct(x_sorted.shape, x_sorted.dtype),
        scratch_shapes=[pltpu.SemaphoreType.DMA],
        input_output_aliases={2: 0},
        compiler_params=_params(("arbitrary",)),
        name="dispatch_small",
    )(dest_t, x, x_sorted)


def _sc_collect(y_sorted, dest_t, tok0, n, small=None):
    chunk = SC_ROWS
    per_choice = SC_WORKERS // TOP_K
    per_w = n // (per_choice * chunk)
    assert per_w * per_choice * chunk == n and per_w % 2 == 0 and tok0 % chunk == 0
    d3 = dest_t.reshape(dest_t.shape[0], dest_t.shape[1] // chunk, chunk)
    width, dtype = y_sorted.shape[1], y_sorted.dtype
    out_type = [jax.ShapeDtypeStruct((TOP_K * n, width), dtype)]
    if small is not None:
        small0, n_small = small
        assert n_small == per_choice * chunk and small0 % chunk == 0
        out_type.append(jax.ShapeDtypeStruct((TOP_K * n_small, width), dtype))

    @functools.partial(
        pl.kernel, mesh=_sc_mesh(),
        out_type=out_type,
        scratch_types=[pltpu.VMEM((per_w, chunk), jnp.int32),
                       pltpu.VMEM((1, chunk), jnp.int32),
                       pltpu.VMEM((2, chunk, width), dtype),
                       pltpu.SemaphoreType.DMA, pltpu.SemaphoreType.DMA],
        compiler_params=pltpu.CompilerParams(use_tc_tiling_on_sc=True),
        name="collect")
    def run(y_hbm, d_hbm, *rest):
        outs, (ip_v, is_v, rows_v, gsem, wsem) = rest[:len(out_type)], rest[len(out_type):]
        op_hbm = outs[0]
        wid = _sc_worker()
        choice = wid // per_choice
        part = wid % per_choice
        pltpu.sync_copy(d_hbm.at[choice, pl.ds(tok0 // chunk + part * per_w, per_w)], ip_v)

        def gather(idx_v, j, slot):
            return pltpu.make_async_copy(y_hbm.at[idx_v.at[j]], rows_v.at[slot], gsem)

        def write(j, slot):
            return pltpu.make_async_copy(rows_v.at[slot], op_hbm.at[pl.ds((wid * per_w + j) * chunk, chunk)], wsem)

        gather(ip_v, 0, 0).start()

        def body(h, carry):
            j = 2 * h
            gather(ip_v, j, 0).wait()

            @pl.when(h > 0)
            def _():
                write(j - 1, 1).wait()

            gather(ip_v, j + 1, 1).start()
            write(j, 0).start()
            gather(ip_v, j + 1, 1).wait()
            write(j, 0).wait()

            @pl.when(j + 2 < per_w)
            def _():
                gather(ip_v, j + 2, 0).start()

            write(j + 1, 1).start()
            return carry

        lax.fori_loop(0, per_w // 2, body, 0)
        write(per_w - 1, 1).wait()

        if small is not None:
            pltpu.sync_copy(d_hbm.at[choice, pl.ds(small0 // chunk + part, 1)], is_v)
            gather(is_v, 0, 0).start()
            gather(is_v, 0, 0).wait()
            pltpu.sync_copy(rows_v.at[0], outs[1].at[pl.ds(wid * chunk, chunk)])

    return run(y_sorted, d3)


def _expert_body(blk0_ref, nblk_ref, cnt_ref, wup_hbm, wdn_hbm, bup_ref, bdn_ref, x_hbm, y_hbm,
                 wup_f, wdn_f, wup_s, wdn_s, xbuf, obuf, w_sem, in_sem, out_sem):
    e = pl.program_id(0)
    nb = nblk_ref[e]
    blk0 = blk0_ref[e]
    cnt = cnt_ref[e]
    tm = EXPERT_ROWS
    pair = 2 * LANES
    wslot = e % 2
    up_rows = D_MODEL // W_PIECES
    dn_rows = D_MODEL // (W_PIECES // 2)

    def w_piece(hbm, buf, ex, slot, p, rows):
        start = p * rows if isinstance(p, int) else pl.multiple_of(p * rows, rows)
        r = pl.ds(start, rows)
        return pltpu.make_async_copy(hbm.at[ex, r], buf.at[slot, r], w_sem.at[slot])

    def w_start(ex, slot, p):
        w_piece(wup_hbm, wup_f, ex, slot, p, up_rows).start()
        if isinstance(p, int):
            if p < W_PIECES // 2:
                w_piece(wdn_hbm, wdn_f, ex, slot, p, dn_rows).start()
        else:
            @pl.when(p < W_PIECES // 2)
            def _():
                w_piece(wdn_hbm, wdn_f, ex, slot, p, dn_rows).start()

    def w_wait(ex, slot):
        for p in range(W_PIECES):
            w_piece(wup_hbm, wup_f, ex, slot, p, up_rows).wait()
        for p in range(W_PIECES // 2):
            w_piece(wdn_hbm, wdn_f, ex, slot, p, dn_rows).wait()

    @pl.when(e == 0)
    def _():
        for p in range(W_PIECES):
            w_start(0, 0, p)

    w_wait(e, wslot)
    wup_ref = wup_f.at[wslot]
    wdn_ref = wdn_f.at[wslot]
    more = e + 1 < N_EXPERTS

    def x_copy(i, slot):
        rows = pl.ds(pl.multiple_of((blk0 + i) * tm, tm), tm)
        return pltpu.make_async_copy(x_hbm.at[rows], xbuf.at[slot], in_sem.at[slot])

    def y_copy(i, slot):
        rows = pl.ds(pl.multiple_of((blk0 + i) * tm, tm), tm)
        return pltpu.make_async_copy(obuf.at[slot], y_hbm.at[rows], out_sem.at[slot])

    @pl.when(nb > 0)
    def _():
        x_copy(0, 0).start(priority=1)
        r = lax.broadcasted_iota(jnp.int32, (pair, pair), 0)
        c = lax.broadcasted_iota(jnp.int32, (pair, pair), 1)
        perm = (r == jnp.where(c < LANES, 2 * c, 2 * (c - LANES) + 1)).astype(BF16)
        for g in range(2 * D_MODEL // pair):
            cols = slice(g * pair, (g + 1) * pair)
            wup_s[g] = jnp.dot(wup_ref[:, cols].astype(BF16), perm, preferred_element_type=F32).astype(BF16)
        for g in range(D_MODEL // pair):
            wdn_s[g] = wdn_ref[:, g * pair:(g + 1) * pair].astype(BF16)

        def block(i, carry):
            slot = i % 2
            x_copy(i, slot).wait()

            @pl.when(i + 1 < nb)
            def _():
                x_copy(i + 1, 1 - slot).start(priority=1)

            @pl.when(i >= 2)
            def _():
                y_copy(i - 2, slot).wait()

            @pl.when(jnp.logical_and(more, i < W_PIECES))
            def _():
                w_start(e + 1, 1 - wslot, i)

            row = lax.broadcasted_iota(jnp.int32, (tm, 1), 0)
            x = jnp.where(row < cnt - i * tm, _unpack_bf16_pair(xbuf[slot]), 0.0).astype(BF16)
            acts = []
            for g in range(2 * D_MODEL // pair):
                cols = slice(g * pair, (g + 1) * pair)
                h = jnp.dot(x, wup_s[g], preferred_element_type=F32) + bup_ref[0, :, cols]
                glu = jnp.minimum(h[:, :LANES], SWIGLU_LIMIT)
                lin = jnp.clip(h[:, LANES:], -SWIGLU_LIMIT, SWIGLU_LIMIT)
                acts.append((glu * jax.nn.sigmoid(SWIGLU_ALPHA * glu) * (lin + 1.0)).astype(BF16))
            act = jnp.concatenate(acts, axis=1)
            half_groups = D_MODEL // pair // 2
            for g in range(half_groups):
                ys = []
                for gg in (g, g + half_groups):
                    cols = slice(gg * pair, (gg + 1) * pair)
                    ys.append(jnp.dot(act, wdn_s[gg], preferred_element_type=F32) + bdn_ref[0, :, cols])
                obuf[slot, :, g * pair:(g + 1) * pair] = _pack_bf16_pair(ys[0], ys[1])
            y_copy(i, slot).start(priority=1)
            return carry

        lax.fori_loop(0, nb, block, 0)

        @pl.when(nb >= 2)
        def _():
            y_copy(nb - 2, nb % 2).wait()

        y_copy(nb - 1, (nb - 1) % 2).wait()

    for p in range(W_PIECES):
        @pl.when(jnp.logical_and(more, p >= nb))
        def _():
            w_start(e + 1, 1 - wslot, p)


def _experts(blk0, nblk, cnt, x_sorted, w_up, w_down, b_up_grouped, b_down):
    tm = EXPERT_ROWS
    per_expert = lambda a: pl.BlockSpec((1,) + a.shape[1:], lambda e, b0, nb, ct: (e, 0, 0))
    hbm = pl.BlockSpec(memory_space=pl.ANY)
    grid_spec = pltpu.PrefetchScalarGridSpec(
        num_scalar_prefetch=3,
        grid=(N_EXPERTS,),
        in_specs=[hbm, hbm, per_expert(b_up_grouped), per_expert(b_down), hbm],
        out_specs=hbm,
        scratch_shapes=[pltpu.VMEM((2,) + w_up.shape[1:], F32), pltpu.VMEM((2,) + w_down.shape[1:], F32),
                        pltpu.VMEM((2 * D_MODEL // (2 * LANES), D_MODEL, 2 * LANES), BF16),
                        pltpu.VMEM((D_MODEL // (2 * LANES), D_MODEL, 2 * LANES), BF16),
                        pltpu.VMEM((2, tm, x_sorted.shape[1]), x_sorted.dtype),
                        pltpu.VMEM((2, tm, D_MODEL // 2), jnp.uint32),
                        pltpu.SemaphoreType.DMA((2,)), pltpu.SemaphoreType.DMA((2,)), pltpu.SemaphoreType.DMA((2,))],
    )
    return pl.pallas_call(
        _expert_body,
        grid_spec=grid_spec,
        out_shape=jax.ShapeDtypeStruct((x_sorted.shape[0], D_MODEL // 2), jnp.uint32),
        compiler_params=_params(("arbitrary",)),
        name="experts",
    )(blk0, nblk, cnt, w_up, w_down, b_up_grouped, b_down, x_sorted)


COMBINE_RING = 3


def _combine_body(nblk, with_prev, gate_ref, xmid_ref, gfin_ref, y_hbm, *rest):
    out_ref, ybuf, sem = rest[1:] if with_prev else rest
    i = pl.program_id(0)
    tt = ybuf.shape[2]

    def fetch(step, kx):
        rows = pl.ds(pl.multiple_of((step + kx * nblk) * tt, tt), tt)
        slot = step % COMBINE_RING
        return pltpu.make_async_copy(y_hbm.at[rows], ybuf.at[slot, kx], sem.at[slot])

    @pl.when(i == 0)
    def _():
        for step in range(min(COMBINE_RING - 1, nblk)):
            for kx in range(TOP_K):
                fetch(step, kx).start()

    @pl.when(i + COMBINE_RING - 1 < nblk)
    def _():
        for kx in range(TOP_K):
            fetch(i + COMBINE_RING - 1, kx).start()

    for kx in range(TOP_K):
        fetch(i, kx).wait()
    slot = i % COMBINE_RING
    gate = gate_ref[...]
    moe = _unpack_bf16_pair(ybuf[slot, 0]) * gate[:, 0:1]
    for kx in range(1, TOP_K):
        moe = moe + _unpack_bf16_pair(ybuf[slot, kx]) * gate[:, kx:kx + 1]
    out_ref[...] = _rms(xmid_ref[...] + moe, gfin_ref[...])


def _combine(gates, xmid, tok0, n, gfin, y_rows, out_so_far=None):
    tt = min(n, COMBINE_ROWS)
    nblk = n // tt
    blk0 = tok0 // tt
    assert nblk * tt == n and blk0 * tt == tok0
    hbm = pl.BlockSpec(memory_space=pl.ANY)
    in_specs = [pl.BlockSpec((tt, LANES), lambda i: (i + blk0, 0)),
                pl.BlockSpec((tt, D_MODEL), lambda i: (i + blk0, 0)),
                pl.BlockSpec((1, D_MODEL), lambda i: (0, 0)), hbm]
    args = [gates, xmid, gfin, y_rows]
    aliases = {}
    if out_so_far is not None:
        in_specs.append(hbm)
        args.append(out_so_far)
        aliases = {len(args) - 1: 0}
    return pl.pallas_call(
        functools.partial(_combine_body, nblk, out_so_far is not None),
        grid=(nblk,),
        in_specs=in_specs,
        out_specs=pl.BlockSpec((tt, D_MODEL), lambda i: (i + blk0, 0)),
        out_shape=jax.ShapeDtypeStruct(xmid.shape, F32),
        scratch_shapes=[pltpu.VMEM((COMBINE_RING, TOP_K, tt, y_rows.shape[1]), y_rows.dtype),
                        pltpu.SemaphoreType.DMA((COMBINE_RING,))],
        input_output_aliases=aliases,
        compiler_params=_params(("arbitrary",)),
        name="combine",
    )(*args)


def kernel(x_prompt, x_sample, cache_k_win, cache_v_win, norm_attn_g, w_in, ln_v_g, ln_v_b, w_spatial, b_spatial,
           attn_sinks, w_out, norm_ffn_g, w_router, b_router, w_up, b_up, w_down, b_down, norm_final_g):
    bp, tp, _ = x_prompt.shape
    bs, ts, _ = x_sample.shape
    w_buf = cache_k_win.shape[2]
    assert bp == 1 and tp % MIX_ROWS == 0 and w_buf == WINDOW and (bs * ts) % PROJ_ROWS == 0 and 8 % ts == 0
    n_p, n_s = bp * tp, bs * ts
    row2 = lambda a: a.reshape(1, -1)

    w_in_bf = w_in[0].astype(BF16)
    w_out_bf = w_out[0].astype(BF16)
    tril = jnp.tril(jnp.ones((CHUNK, CHUNK), dtype=bool))
    wsp = jnp.where(tril[None], w_spatial[0], 0.0)
    wsp_bf = wsp.astype(BF16)
    bsp = jnp.broadcast_to(b_spatial[0][:, :, None], (GMLP_GROUPS, CHUNK, LANES))
    b_up_grouped = b_up[0].reshape(N_EXPERTS, -1, LANES, 2).transpose(0, 1, 3, 2).reshape(N_EXPERTS, 1, -1)
    bd = b_down[0][:, None, :]
    g1, g2, gfin = row2(norm_attn_g[0]), row2(norm_ffn_g[0]), row2(norm_final_g)
    lng, lnb = row2(ln_v_g[0]), row2(ln_v_b[0])
    wr_hi = w_router[0].astype(BF16)
    wr = jnp.concatenate([wr_hi, (w_router[0] - wr_hi.astype(F32)).astype(BF16)], axis=1)
    br = row2(b_router[0])
    sinks = attn_sinks[0]

    xp = x_prompt.reshape(n_p, D_MODEL)
    cs_p = _rotary_inputs(jnp.arange(tp, dtype=jnp.int32))
    q_p, k_p, v_p, a_p, vn_p, sgb_p = _proj(xp, g1, w_in_bf, cs_p, lng, lnb)
    xmid_p, xn2_p, lg_p = _mix(sinks, q_p, k_p, v_p, a_p, vn_p, sgb_p, xp, wsp_bf, bsp, w_out_bf, g2, wr, br)

    xs = x_sample.reshape(n_s, D_MODEL)
    pos_s = PAST_LEN + jnp.arange(ts, dtype=jnp.int32)
    cs_s = _rotary_inputs(jnp.tile(pos_s, bs))
    q_s, k_s, v_s, a_s, vn_s, sgb_s = _proj(xs, g1, w_in_bf, cs_s, lng, lnb)
    keys_minor = lambda t: t.reshape(bs, -1, KV_WIDTH).transpose(0, 2, 1)
    k_cache_t, v_cache_t = keys_minor(cache_k_win[0]), keys_minor(cache_v_win[0])
    pad_new = lambda t: jnp.pad(t.reshape(bs, ts, KV_WIDTH), ((0, 0), (0, (-ts) % 16), (0, 0)))
    sink_col = jnp.tile(sinks, ts).reshape(ts * N_HEADS, 1)
    o_s = _sample_attn(sink_col, q_s.reshape(bs, ts * N_HEADS, HEAD_DIM), k_cache_t, v_cache_t,
                       pad_new(k_s), pad_new(v_s)).reshape(n_s, Q_WIDTH)
    lag = np.arange(ts)[:, None] - np.arange(ts)[None, :]
    coef = jnp.stack([jnp.sum(jnp.where(lag == d, wsp[:, :ts, :ts], 0.0), axis=2)
                      for d in range(ts)])
    coef = jnp.repeat(coef.transpose(0, 2, 1), GMLP_WIDTH // GMLP_GROUPS, axis=2)
    coef = jnp.tile(coef, (1, 8 // ts, 1))
    bias = jnp.tile(jnp.repeat(b_spatial[0][:, :ts].T, GMLP_WIDTH // GMLP_GROUPS, axis=1), (8 // ts, 1))
    xmid_s, xn2_s, lg_s = _mix_sample(a_s, vn_s, sgb_s, o_s, xs, coef, bias, w_out_bf, g2, wr, br)

    tm = EXPERT_ROWS
    n_blocks = (n_p * TOP_K + N_EXPERTS * n_s) // tm + N_EXPERTS
    gate_p, dest_p, table_p = _route(lg_p, jnp.zeros((N_EXPERTS, LANES), F32), n_s)
    x_sorted = _sc_dispatch(xn2_p, dest_p, n_blocks * tm)
    gate_s, dest_s, table = _route(lg_s, table_p, None)
    x_sorted = _dispatch_small(xn2_s, dest_s, x_sorted)
    dest_t = jnp.concatenate([dest_p, dest_s], axis=1)
    meta = table.astype(jnp.int32)
    y_sorted = _experts(meta[:, 2], meta[:, 3], meta[:, 1], x_sorted, w_up[0], w_down[0], b_up_grouped, bd)
    half = n_p // 2
    yrows_a, yrows_s = _sc_collect(y_sorted, dest_t, 0, half, small=(n_p, n_s))
    yrows_b, = _sc_collect(y_sorted, dest_t, half, half)
    y_s = _combine(gate_s, xmid_s, 0, n_s, gfin, yrows_s)
    y_p = _combine(gate_p, xmid_p, 0, half, gfin, yrows_a)
    y_p = _combine(gate_p, xmid_p, half, half, gfin, yrows_b, out_so_far=y_p)

    k4 = lambda t: t.reshape(1, bp, -1, N_KV_HEADS, HEAD_DIM)
    next_window = lambda cache_t, new: _window(cache_t, new.T).transpose(0, 2, 1).reshape(
        1, bs, w_buf, N_KV_HEADS, HEAD_DIM)
    return (y_p.reshape(bp, tp, D_MODEL),
            y_s.reshape(bs, ts, D_MODEL),
            k4(k_p[n_p - WINDOW:]),
            k4(v_p[n_p - WINDOW:]),
            vn_p[n_p - CHUNK:].reshape(1, bp, CHUNK, GMLP_WIDTH),
            next_window(k_cache_t, k_s),
            next_window(v_cache_t, v_s),
            vn_s.reshape(1, bs, ts, GMLP_WIDTH))
```
